```python
import math
import jax, jax.numpy as jnp
from jax import lax
import numpy as np

D_MODEL = 2048
BATCH = 8
SEQ = 4096
DEPTH = 1

SSM_WIDTH = 1024
SSM_GROUP = 16
SSM_GROUPS = SSM_WIDTH // SSM_GROUP
SSM_STATE = 64
CONV_WIDTH = 1024
CONV_GROUP = 64
CONV_K = 3
D_FF = 5632
EPS = 1e-6
DT_MIN = 1e-3
DT_MAX = 1e-1
IN_COLS = SSM_WIDTH + 3 * CONV_WIDTH + 2 * D_MODEL

kernel_name = "hybrid_s5_shortconv_gated_block"


def rmsnorm(x, g):
    xf = x.astype(jnp.float32)
    y = xf * lax.rsqrt(jnp.mean(xf * xf, axis=-1, keepdims=True) + EPS)
    return (y * g.astype(jnp.float32)).astype(x.dtype)


def causal_dwconv(x, w, b):
    c = x.shape[-1]
    y = lax.conv_general_dilated(
        x, w[:, None, :].astype(x.dtype), window_strides=(1,),
        padding=((CONV_K - 1, 0),), dimension_numbers=("NWC", "WIO", "NWC"),
        feature_group_count=c)
    return y + b.astype(x.dtype)


def s5_scan(u, a_re, a_im, log_dt, b_re, b_im, c_re, c_im, d_skip):
    bsz, seq_len, _ = u.shape
    f32 = jnp.float32
    uf = u.astype(f32).reshape(bsz, seq_len, SSM_GROUPS, SSM_GROUP)
    ar = a_re.astype(f32)
    ai = a_im.astype(f32)
    dt = jnp.exp(log_dt.astype(f32))[:, None]
    mag = jnp.exp(dt * ar)
    abar_re = mag * jnp.cos(dt * ai)
    abar_im = mag * jnp.sin(dt * ai)
    nr = abar_re - 1.0
    ni = abar_im
    den = ar * ar + ai * ai
    fr = (nr * ar + ni * ai) / den
    fi = (ni * ar - nr * ai) / den
    br = b_re.astype(f32)
    bi = b_im.astype(f32)
    bbar_re = fr[..., None] * br - fi[..., None] * bi
    bbar_im = fr[..., None] * bi + fi[..., None] * br
    bu_re = jnp.einsum("blgh,gph->blgp", uf, bbar_re)
    bu_im = jnp.einsum("blgh,gph->blgp", uf, bbar_im)
    a_seq_re = jnp.broadcast_to(abar_re[None, None], (1, seq_len, SSM_GROUPS, SSM_STATE))
    a_seq_im = jnp.broadcast_to(abar_im[None, None], (1, seq_len, SSM_GROUPS, SSM_STATE))

    def combine(e1, e2):
        a1r, a1i, b1r, b1i = e1
        a2r, a2i, b2r, b2i = e2
        return (a2r * a1r - a2i * a1i,
                a2r * a1i + a2i * a1r,
                a2r * b1r - a2i * b1i + b2r,
                a2r * b1i + a2i * b1r + b2i)

    _, _, xr, xi = lax.associative_scan(combine, (a_seq_re, a_seq_im, bu_re, bu_im), axis=1)
    y = (jnp.einsum("blgp,ghp->blgh", xr, c_re.astype(f32))
         - jnp.einsum("blgp,ghp->blgh", xi, c_im.astype(f32))
         + d_skip.astype(f32) * uf)
    return y.reshape(bsz, seq_len, SSM_WIDTH).astype(u.dtype)


def token_mixer(xn, w_in, a_re, a_im, log_dt, b_re, b_im, c_re, c_im, d_skip,
                w_glu, w_ssm_out, conv_w, conv_b, w_conv_out, w_o):
    proj = xn @ w_in
    s0 = SSM_WIDTH
    s1 = s0 + CONV_WIDTH
    s2 = s1 + CONV_WIDTH
    s3 = s2 + CONV_WIDTH
    s4 = s3 + D_MODEL
    u = proj[..., :s0]
    v = proj[..., s0:s1]
    gate_b = proj[..., s1:s2]
    gate_c = proj[..., s2:s3]
    merge_a = proj[..., s3:s4]
    merge_b = proj[..., s4:]
    ya = jax.nn.gelu(s5_scan(u, a_re, a_im, log_dt, b_re, b_im, c_re, c_im, d_skip))
    ya = ya * jax.nn.sigmoid(ya @ w_glu)
    ya = ya @ w_ssm_out
    yb = (gate_b * causal_dwconv(gate_c * v, conv_w, conv_b)) @ w_conv_out
    merged = jax.nn.sigmoid(merge_a) * ya + jax.nn.sigmoid(merge_b) * yb
    return merged @ w_o


def conv_ffn(xn, w_up, ffn_conv_w, ffn_conv_b, w_down):
    h = xn @ w_up
    a = causal_dwconv(h[..., :D_FF], ffn_conv_w, ffn_conv_b)
    return (jax.nn.gelu(a) * h[..., D_FF:]) @ w_down


def _fwd_setup_inputs(seed: int = 0) -> dict:
    key = jax.random.key(seed)
    ks = jax.random.split(key, 24)
    f32 = jnp.float32
    nrm = lambda k, s, sc: jax.random.normal(k, s, f32) * sc
    n_idx = jnp.arange(SSM_STATE, dtype=f32)
    a_re = -0.5 * jnp.exp(nrm(ks[3], (DEPTH, SSM_GROUPS, SSM_STATE), 0.05))
    a_im = math.pi * n_idx[None, None, :] + nrm(ks[4], (DEPTH, SSM_GROUPS, SSM_STATE), 0.05)
    log_dt = jax.random.uniform(ks[5], (DEPTH, SSM_GROUPS), f32, math.log(DT_MIN), math.log(DT_MAX))
    return {
        "x": nrm(ks[0], (BATCH, SEQ, D_MODEL), 1.0),
        "norm_tok": 1.0 + nrm(ks[1], (DEPTH, D_MODEL), 0.01),
        "w_in": nrm(ks[2], (DEPTH, D_MODEL, IN_COLS), D_MODEL ** -0.5),
        "a_re": a_re,
        "a_im": a_im,
        "log_dt": log_dt,
        "b_re": nrm(ks[6], (DEPTH, SSM_GROUPS, SSM_STATE, SSM_GROUP), (2 * SSM_GROUP) ** -0.5),
        "b_im": nrm(ks[7], (DEPTH, SSM_GROUPS, SSM_STATE, SSM_GROUP), (2 * SSM_GROUP) ** -0.5),
        "c_re": nrm(ks[8], (DEPTH, SSM_GROUPS, SSM_GROUP, SSM_STATE), SSM_STATE ** -0.5),
        "c_im": nrm(ks[9], (DEPTH, SSM_GROUPS, SSM_GROUP, SSM_STATE), SSM_STATE ** -0.5),
        "d_skip": nrm(ks[10], (DEPTH, SSM_GROUPS, SSM_GROUP), 1.0),
        "w_glu": nrm(ks[11], (DEPTH, SSM_WIDTH, SSM_WIDTH), SSM_WIDTH ** -0.5),
        "w_ssm_out": nrm(ks[12], (DEPTH, SSM_WIDTH, D_MODEL), SSM_WIDTH ** -0.5),
        "conv_w": nrm(ks[13], (DEPTH, CONV_K, CONV_WIDTH), CONV_K ** -0.5),
        "conv_b": nrm(ks[14], (DEPTH, CONV_WIDTH), 0.01),
        "w_conv_out": nrm(ks[15], (DEPTH, CONV_WIDTH, D_MODEL), CONV_WIDTH ** -0.5),
        "w_o": nrm(ks[16], (DEPTH, D_MODEL, D_MODEL), D_MODEL ** -0.5),
        "norm_ffn": 1.0 + nrm(ks[17], (DEPTH, D_MODEL), 0.01),
        "w_up": nrm(ks[18], (DEPTH, D_MODEL, 2 * D_FF), D_MODEL ** -0.5),
        "ffn_conv_w": nrm(ks[19], (DEPTH, CONV_K, D_FF), CONV_K ** -0.5),
        "ffn_conv_b": nrm(ks[20], (DEPTH, D_FF), 0.01),
        "w_down": nrm(ks[21], (DEPTH, D_FF, D_MODEL), D_FF ** -0.5),
        "norm_final": 1.0 + nrm(ks[22], (D_MODEL,), 0.01),
    }


def _fwd_reference(x, norm_tok, w_in, a_re, a_im, log_dt, b_re, b_im, c_re, c_im, d_skip,
              w_glu, w_ssm_out, conv_w, conv_b, w_conv_out, w_o,
              norm_ffn, w_up, ffn_conv_w, ffn_conv_b, w_down, norm_final):
    h = x
    for l in range(DEPTH):
        h = h + token_mixer(rmsnorm(h, norm_tok[l]), w_in[l], a_re[l], a_im[l], log_dt[l],
                            b_re[l], b_im[l], c_re[l], c_im[l], d_skip[l],
                            w_glu[l], w_ssm_out[l], conv_w[l], conv_b[l], w_conv_out[l], w_o[l])
        h = h + conv_ffn(rmsnorm(h, norm_ffn[l]), w_up[l], ffn_conv_w[l], ffn_conv_b[l], w_down[l])
    return rmsnorm(h, norm_final)


import jax as _jax
import jax.numpy as _jnp

TWIN_FORMAT = 'train_step'
FWD_PARAMS = ['x', 'norm_tok', 'w_in', 'a_re', 'a_im', 'log_dt', 'b_re', 'b_im', 'c_re', 'c_im', 'd_skip', 'w_glu', 'w_ssm_out', 'conv_w', 'conv_b', 'w_conv_out', 'w_o', 'norm_ffn', 'w_up', 'ffn_conv_w', 'ffn_conv_b', 'w_down', 'norm_final']
TWIN_WEIGHTS = ['norm_tok', 'w_in', 'a_re', 'a_im', 'log_dt', 'b_re', 'b_im', 'c_re', 'c_im', 'd_skip', 'w_glu', 'w_ssm_out', 'conv_w', 'conv_b', 'w_conv_out', 'w_o', 'norm_ffn', 'w_up', 'ffn_conv_w', 'ffn_conv_b', 'w_down', 'norm_final']
TWIN_DIFF_INPUT = 'x'
TWIN_INPUTS = ['x', 'norm_tok', 'w_in', 'a_re', 'a_im', 'log_dt', 'b_re', 'b_im', 'c_re', 'c_im', 'd_skip', 'w_glu', 'w_ssm_out', 'conv_w', 'conv_b', 'w_conv_out', 'w_o', 'norm_ffn', 'w_up', 'ffn_conv_w', 'ffn_conv_b', 'w_down', 'norm_final', 'loss_target', 'm_norm_tok', 'm_w_in', 'm_a_re', 'm_a_im', 'm_log_dt', 'm_b_re', 'm_b_im', 'm_c_re', 'm_c_im', 'm_d_skip', 'm_w_glu', 'm_w_ssm_out', 'm_conv_w', 'm_conv_b', 'm_w_conv_out', 'm_w_o', 'm_norm_ffn', 'm_w_up', 'm_ffn_conv_w', 'm_ffn_conv_b', 'm_w_down', 'm_norm_final', 'v_norm_tok', 'v_w_in', 'v_a_re', 'v_a_im', 'v_log_dt', 'v_b_re', 'v_b_im', 'v_c_re', 'v_c_im', 'v_d_skip', 'v_w_glu', 'v_w_ssm_out', 'v_conv_w', 'v_conv_b', 'v_w_conv_out', 'v_w_o', 'v_norm_ffn', 'v_w_up', 'v_ffn_conv_w', 'v_ffn_conv_b', 'v_w_down', 'v_norm_final']
TWIN_OUTPUTS = ['loss', 'grad_x', 'grad_norm_tok', 'grad_w_in', 'grad_a_re', 'grad_a_im', 'grad_log_dt', 'grad_b_re', 'grad_b_im', 'grad_c_re', 'grad_c_im', 'grad_d_skip', 'grad_w_glu', 'grad_w_ssm_out', 'grad_conv_w', 'grad_conv_b', 'grad_w_conv_out', 'grad_w_o', 'grad_norm_ffn', 'grad_w_up', 'grad_ffn_conv_w', 'grad_ffn_conv_b', 'grad_w_down', 'grad_norm_final', 'delta_norm_tok', 'delta_w_in', 'delta_a_re', 'delta_a_im', 'delta_log_dt', 'delta_b_re', 'delta_b_im', 'delta_c_re', 'delta_c_im', 'delta_d_skip', 'delta_w_glu', 'delta_w_ssm_out', 'delta_conv_w', 'delta_conv_b', 'delta_w_conv_out', 'delta_w_o', 'delta_norm_ffn', 'delta_w_up', 'delta_ffn_conv_w', 'delta_ffn_conv_b', 'delta_w_down', 'delta_norm_final', 'new_m_norm_tok', 'new_m_w_in', 'new_m_a_re', 'new_m_a_im', 'new_m_log_dt', 'new_m_b_re', 'new_m_b_im', 'new_m_c_re', 'new_m_c_im', 'new_m_d_skip', 'new_m_w_glu', 'new_m_w_ssm_out', 'new_m_conv_w', 'new_m_conv_b', 'new_m_w_conv_out', 'new_m_w_o', 'new_m_norm_ffn', 'new_m_w_up', 'new_m_ffn_conv_w', 'new_m_ffn_conv_b', 'new_m_w_down', 'new_m_norm_final', 'new_v_norm_tok', 'new_v_w_in', 'new_v_a_re', 'new_v_a_im', 'new_v_log_dt', 'new_v_b_re', 'new_v_b_im', 'new_v_c_re', 'new_v_c_im', 'new_v_d_skip', 'new_v_w_glu', 'new_v_w_ssm_out', 'new_v_conv_w', 'new_v_conv_b', 'new_v_w_conv_out', 'new_v_w_o', 'new_v_norm_ffn', 'new_v_w_up', 'new_v_ffn_conv_w', 'new_v_ffn_conv_b', 'new_v_w_down', 'new_v_norm_final']
TWIN_LEAF_KINDS = {'loss': 'loss', 'grad_x': 'grad_x', 'grad_norm_tok': 'grad_w', 'grad_w_in': 'grad_w', 'grad_a_re': 'grad_w', 'grad_a_im': 'grad_w', 'grad_log_dt': 'grad_w', 'grad_b_re': 'grad_w', 'grad_b_im': 'grad_w', 'grad_c_re': 'grad_w', 'grad_c_im': 'grad_w', 'grad_d_skip': 'grad_w', 'grad_w_glu': 'grad_w', 'grad_w_ssm_out': 'grad_w', 'grad_conv_w': 'grad_w', 'grad_conv_b': 'grad_w', 'grad_w_conv_out': 'grad_w', 'grad_w_o': 'grad_w', 'grad_norm_ffn': 'grad_w', 'grad_w_up': 'grad_w', 'grad_ffn_conv_w': 'grad_w', 'grad_ffn_conv_b': 'grad_w', 'grad_w_down': 'grad_w', 'grad_norm_final': 'grad_w', 'delta_norm_tok': 'delta_w', 'delta_w_in': 'delta_w', 'delta_a_re': 'delta_w', 'delta_a_im': 'delta_w', 'delta_log_dt': 'delta_w', 'delta_b_re': 'delta_w', 'delta_b_im': 'delta_w', 'delta_c_re': 'delta_w', 'delta_c_im': 'delta_w', 'delta_d_skip': 'delta_w', 'delta_w_glu': 'delta_w', 'delta_w_ssm_out': 'delta_w', 'delta_conv_w': 'delta_w', 'delta_conv_b': 'delta_w', 'delta_w_conv_out': 'delta_w', 'delta_w_o': 'delta_w', 'delta_norm_ffn': 'delta_w', 'delta_w_up': 'delta_w', 'delta_ffn_conv_w': 'delta_w', 'delta_ffn_conv_b': 'delta_w', 'delta_w_down': 'delta_w', 'delta_norm_final': 'delta_w', 'new_m_norm_tok': 'new_m', 'new_m_w_in': 'new_m', 'new_m_a_re': 'new_m', 'new_m_a_im': 'new_m', 'new_m_log_dt': 'new_m', 'new_m_b_re': 'new_m', 'new_m_b_im': 'new_m', 'new_m_c_re': 'new_m', 'new_m_c_im': 'new_m', 'new_m_d_skip': 'new_m', 'new_m_w_glu': 'new_m', 'new_m_w_ssm_out': 'new_m', 'new_m_conv_w': 'new_m', 'new_m_conv_b': 'new_m', 'new_m_w_conv_out': 'new_m', 'new_m_w_o': 'new_m', 'new_m_norm_ffn': 'new_m', 'new_m_w_up': 'new_m', 'new_m_ffn_conv_w': 'new_m', 'new_m_ffn_conv_b': 'new_m', 'new_m_w_down': 'new_m', 'new_m_norm_final': 'new_m', 'new_v_norm_tok': 'new_v', 'new_v_w_in': 'new_v', 'new_v_a_re': 'new_v', 'new_v_a_im': 'new_v', 'new_v_log_dt': 'new_v', 'new_v_b_re': 'new_v', 'new_v_b_im': 'new_v', 'new_v_c_re': 'new_v', 'new_v_c_im': 'new_v', 'new_v_d_skip': 'new_v', 'new_v_w_glu': 'new_v', 'new_v_w_ssm_out': 'new_v', 'new_v_conv_w': 'new_v', 'new_v_conv_b': 'new_v', 'new_v_w_conv_out': 'new_v', 'new_v_w_o': 'new_v', 'new_v_norm_ffn': 'new_v', 'new_v_w_up': 'new_v', 'new_v_ffn_conv_w': 'new_v', 'new_v_ffn_conv_b': 'new_v', 'new_v_w_down': 'new_v', 'new_v_norm_final': 'new_v'}


def _forward(args):
    return _fwd_reference(*[args[k] for k in FWD_PARAMS])


def _output_shape():
    def fwd():
        inp = _fwd_setup_inputs(0)
        return _fwd_reference(*[inp[k] for k in FWD_PARAMS])
    out = _jax.eval_shape(fwd)
    return out.shape, out.dtype

N_MICROBATCH = 1
ADAM_LR = 0.001
ADAM_B1 = 0.9
ADAM_B2 = 0.999
ADAM_EPS = 1e-08
ADAM_WD = 0.01
ADAM_STEP = 10
PER_EXAMPLE_BATCH_AXIS = {'x': 0, 'loss_target': 0}
SHARED_INPUTS = []
_WEIGHT_DTYPES = {'norm_tok': _jnp.float32, 'w_in': _jnp.float32, 'a_re': _jnp.float32, 'a_im': _jnp.float32, 'log_dt': _jnp.float32, 'b_re': _jnp.float32, 'b_im': _jnp.float32, 'c_re': _jnp.float32, 'c_im': _jnp.float32, 'd_skip': _jnp.float32, 'w_glu': _jnp.float32, 'w_ssm_out': _jnp.float32, 'conv_w': _jnp.float32, 'conv_b': _jnp.float32, 'w_conv_out': _jnp.float32, 'w_o': _jnp.float32, 'norm_ffn': _jnp.float32, 'w_up': _jnp.float32, 'ffn_conv_w': _jnp.float32, 'ffn_conv_b': _jnp.float32, 'w_down': _jnp.float32, 'norm_final': _jnp.float32}
MOMENT_SCALE = {'norm_tok': 8.547774e-02, 'w_in': 4.270737e-02, 'a_re': 1.788360e-03, 'a_im': 1.770263e-03, 'log_dt': 1.344824e+00, 'b_re': 1.177976e-03, 'b_im': 1.164208e-03, 'c_re': 1.606639e-03, 'c_im': 1.641888e-03, 'd_skip': 2.438311e-02, 'w_glu': 6.872421e-03, 'w_ssm_out': 1.612890e-02, 'conv_w': 6.673808e-02, 'conv_b': 6.880750e-02, 'w_conv_out': 4.704282e-02, 'w_o': 4.979440e-02, 'norm_ffn': 6.283886e-02, 'w_up': 2.679784e-02, 'ffn_conv_w': 2.742024e-02, 'ffn_conv_b': 2.643462e-02, 'w_down': 4.369446e-02, 'norm_final': 1.599488e+01}


def _to_microbatches(a, axis):
    t = _jnp.moveaxis(a, axis, 0)
    t = t.reshape((N_MICROBATCH, t.shape[0] // N_MICROBATCH) + t.shape[1:])
    return _jnp.moveaxis(t, 1, axis + 1)


def setup_inputs(seed: int = 0) -> dict:
    inp = _fwd_setup_inputs(seed)
    key = _jax.random.fold_in(_jax.random.key(seed), 7919)
    shape, _ = _output_shape()
    out = dict(inp)
    out["loss_target"] = _jax.random.normal(_jax.random.fold_in(key, 0), shape, _jnp.float32)
    for i, name in enumerate(TWIN_WEIGHTS):
        w = inp[name].astype(_jnp.float32)
        if MOMENT_SCALE is None:
            s = _jnp.sqrt(_jnp.mean(_jnp.square(w)) + 1e-30)
        else:
            s = MOMENT_SCALE[name]
        km, kv = _jax.random.split(_jax.random.fold_in(key, i + 1))
        out[name] = w
        out["m_" + name] = s * _jax.random.normal(km, w.shape, _jnp.float32)
        out["v_" + name] = (s * s) * _jax.random.uniform(kv, w.shape, _jnp.float32, 0.5, 1.5)
    if N_MICROBATCH > 1:
        for name, axis in PER_EXAMPLE_BATCH_AXIS.items():
            out[name] = _to_microbatches(out[name], axis)
    return {'x': out['x'], 'norm_tok': out['norm_tok'], 'w_in': out['w_in'], 'a_re': out['a_re'], 'a_im': out['a_im'], 'log_dt': out['log_dt'], 'b_re': out['b_re'], 'b_im': out['b_im'], 'c_re': out['c_re'], 'c_im': out['c_im'], 'd_skip': out['d_skip'], 'w_glu': out['w_glu'], 'w_ssm_out': out['w_ssm_out'], 'conv_w': out['conv_w'], 'conv_b': out['conv_b'], 'w_conv_out': out['w_conv_out'], 'w_o': out['w_o'], 'norm_ffn': out['norm_ffn'], 'w_up': out['w_up'], 'ffn_conv_w': out['ffn_conv_w'], 'ffn_conv_b': out['ffn_conv_b'], 'w_down': out['w_down'], 'norm_final': out['norm_final'], 'loss_target': out['loss_target'], 'm_norm_tok': out['m_norm_tok'], 'm_w_in': out['m_w_in'], 'm_a_re': out['m_a_re'], 'm_a_im': out['m_a_im'], 'm_log_dt': out['m_log_dt'], 'm_b_re': out['m_b_re'], 'm_b_im': out['m_b_im'], 'm_c_re': out['m_c_re'], 'm_c_im': out['m_c_im'], 'm_d_skip': out['m_d_skip'], 'm_w_glu': out['m_w_glu'], 'm_w_ssm_out': out['m_w_ssm_out'], 'm_conv_w': out['m_conv_w'], 'm_conv_b': out['m_conv_b'], 'm_w_conv_out': out['m_w_conv_out'], 'm_w_o': out['m_w_o'], 'm_norm_ffn': out['m_norm_ffn'], 'm_w_up': out['m_w_up'], 'm_ffn_conv_w': out['m_ffn_conv_w'], 'm_ffn_conv_b': out['m_ffn_conv_b'], 'm_w_down': out['m_w_down'], 'm_norm_final': out['m_norm_final'], 'v_norm_tok': out['v_norm_tok'], 'v_w_in': out['v_w_in'], 'v_a_re': out['v_a_re'], 'v_a_im': out['v_a_im'], 'v_log_dt': out['v_log_dt'], 'v_b_re': out['v_b_re'], 'v_b_im': out['v_b_im'], 'v_c_re': out['v_c_re'], 'v_c_im': out['v_c_im'], 'v_d_skip': out['v_d_skip'], 'v_w_glu': out['v_w_glu'], 'v_w_ssm_out': out['v_w_ssm_out'], 'v_conv_w': out['v_conv_w'], 'v_conv_b': out['v_conv_b'], 'v_w_conv_out': out['v_w_conv_out'], 'v_w_o': out['v_w_o'], 'v_norm_ffn': out['v_norm_ffn'], 'v_w_up': out['v_w_up'], 'v_ffn_conv_w': out['v_ffn_conv_w'], 'v_ffn_conv_b': out['v_ffn_conv_b'], 'v_w_down': out['v_w_down'], 'v_norm_final': out['v_norm_final']}


def _loss(weights, diff, rest, loss_target):
    with _jax.named_scope("forward"):
        args = {**rest, TWIN_DIFF_INPUT: diff, **{k: w.astype(_WEIGHT_DTYPES[k]) for k, w in weights.items()}}
        y = _forward(args)
    with _jax.named_scope("loss_head"):
        err = _jnp.square(y.astype(_jnp.float32) - loss_target)
        return 0.5 * _jnp.sum(_jnp.mean(err, axis=-1)) if err.ndim else 0.5 * err


def _adamw(w, g, m, v):
    m = ADAM_B1 * m + (1.0 - ADAM_B1) * g
    v = ADAM_B2 * v + (1.0 - ADAM_B2) * _jnp.square(g)
    m_hat = m / (1.0 - ADAM_B1 ** ADAM_STEP)
    v_hat = v / (1.0 - ADAM_B2 ** ADAM_STEP)
    delta = -ADAM_LR * (m_hat / (_jnp.sqrt(v_hat) + ADAM_EPS) + ADAM_WD * w)
    return delta, m, v


def reference(x, norm_tok, w_in, a_re, a_im, log_dt, b_re, b_im, c_re, c_im, d_skip, w_glu, w_ssm_out, conv_w, conv_b, w_conv_out, w_o, norm_ffn, w_up, ffn_conv_w, ffn_conv_b, w_down, norm_final, loss_target, m_norm_tok, m_w_in, m_a_re, m_a_im, m_log_dt, m_b_re, m_b_im, m_c_re, m_c_im, m_d_skip, m_w_glu, m_w_ssm_out, m_conv_w, m_conv_b, m_w_conv_out, m_w_o, m_norm_ffn, m_w_up, m_ffn_conv_w, m_ffn_conv_b, m_w_down, m_norm_final, v_norm_tok, v_w_in, v_a_re, v_a_im, v_log_dt, v_b_re, v_b_im, v_c_re, v_c_im, v_d_skip, v_w_glu, v_w_ssm_out, v_conv_w, v_conv_b, v_w_conv_out, v_w_o, v_norm_ffn, v_w_up, v_ffn_conv_w, v_ffn_conv_b, v_w_down, v_norm_final):
    given = dict(x=x, norm_tok=norm_tok, w_in=w_in, a_re=a_re, a_im=a_im, log_dt=log_dt, b_re=b_re, b_im=b_im, c_re=c_re, c_im=c_im, d_skip=d_skip, w_glu=w_glu, w_ssm_out=w_ssm_out, conv_w=conv_w, conv_b=conv_b, w_conv_out=w_conv_out, w_o=w_o, norm_ffn=norm_ffn, w_up=w_up, ffn_conv_w=ffn_conv_w, ffn_conv_b=ffn_conv_b, w_down=w_down, norm_final=norm_final, loss_target=loss_target, m_norm_tok=m_norm_tok, m_w_in=m_w_in, m_a_re=m_a_re, m_a_im=m_a_im, m_log_dt=m_log_dt, m_b_re=m_b_re, m_b_im=m_b_im, m_c_re=m_c_re, m_c_im=m_c_im, m_d_skip=m_d_skip, m_w_glu=m_w_glu, m_w_ssm_out=m_w_ssm_out, m_conv_w=m_conv_w, m_conv_b=m_conv_b, m_w_conv_out=m_w_conv_out, m_w_o=m_w_o, m_norm_ffn=m_norm_ffn, m_w_up=m_w_up, m_ffn_conv_w=m_ffn_conv_w, m_ffn_conv_b=m_ffn_conv_b, m_w_down=m_w_down, m_norm_final=m_norm_final, v_norm_tok=v_norm_tok, v_w_in=v_w_in, v_a_re=v_a_re, v_a_im=v_a_im, v_log_dt=v_log_dt, v_b_re=v_b_re, v_b_im=v_b_im, v_c_re=v_c_re, v_c_im=v_c_im, v_d_skip=v_d_skip, v_w_glu=v_w_glu, v_w_ssm_out=v_w_ssm_out, v_conv_w=v_conv_w, v_conv_b=v_conv_b, v_w_conv_out=v_w_conv_out, v_w_o=v_w_o, v_norm_ffn=v_norm_ffn, v_w_up=v_w_up, v_ffn_conv_w=v_ffn_conv_w, v_ffn_conv_b=v_ffn_conv_b, v_w_down=v_w_down, v_norm_final=v_norm_final)
    weights = {n: given[n] for n in TWIN_WEIGHTS}
    shared = {n: given[n] for n in SHARED_INPUTS}
    per_example = {n: given[n] for n in ['x']}
    grad_fn = _jax.value_and_grad(_loss, argnums=(0, 1))

    def one_microbatch(ex, loss_target):
        ex = dict(ex)
        diff = ex.pop(TWIN_DIFF_INPUT)
        return grad_fn(weights, diff, {**shared, **ex}, loss_target)

    if N_MICROBATCH == 1:
        loss, (grad_w, grad_x) = one_microbatch(per_example, given["loss_target"])
    else:
        def body(carry, xs):
            loss_sum, grad_sum = carry
            l_k, (gw_k, gx_k) = one_microbatch(xs[0], xs[1])
            with _jax.named_scope("update"):
                return (loss_sum + l_k, _jax.tree.map(_jnp.add, grad_sum, gw_k)), gx_k

        init = (_jnp.zeros((), _jnp.float32), _jax.tree.map(_jnp.zeros_like, weights))
        (loss, grad_w), grad_x = _jax.lax.scan(body, init, (per_example, given["loss_target"]))
    with _jax.named_scope("update"):
        delta_w, new_m, new_v = {}, {}, {}
        for n in TWIN_WEIGHTS:
            delta_w[n], new_m[n], new_v[n] = _adamw(weights[n], grad_w[n], given["m_" + n], given["v_" + n])
    return (loss, grad_x, *[grad_w[n] for n in TWIN_WEIGHTS], *[delta_w[n] for n in TWIN_WEIGHTS],
            *[new_m[n] for n in TWIN_WEIGHTS], *[new_v[n] for n in TWIN_WEIGHTS])
```

```python
import functools
import math
from typing import NamedTuple

import jax
import jax.numpy as jnp
from jax import lax
from jax.experimental import pallas as pl
from jax.experimental.pallas import tpu as pltpu

f32 = jnp.float32
bf16 = jnp.bfloat16
_MESH = pl.DeviceIdType.MESH

_EPS = 1e-6
_ADAM_LR = 0.001
_ADAM_B1 = 0.9
_ADAM_B2 = 0.999
_ADAM_EPS = 1e-08
_ADAM_WD = 0.01
_ADAM_STEP = 10
_SSM_GROUP = 16
_SSM_STATE = 64
_SLAB_GROUPS = 16
_NCHIP = 4
_VMEM_LIMIT = 56 * 2**20
_GELU_C = math.sqrt(2.0 / math.pi)
_GELU_A = 0.044715


class _Cfg(NamedTuple):
    L: int = 4096
    D: int = 2048
    SW: int = 1024
    CW: int = 1024
    F: int = 5632
    T: int = 256


def _tile(n, pref, align):
    t = min(n, pref)
    t -= t % align
    while t > align and n % t:
        t -= align
    assert t > 0 and n % t == 0, (n, pref, align)
    return t


def _cparams(sem):
    return pltpu.CompilerParams(dimension_semantics=sem, vmem_limit_bytes=_VMEM_LIMIT)


def _gelu(x):
    return 0.5 * x * (1.0 + jnp.tanh(_GELU_C * (x + _GELU_A * x * x * x)))


def _gelu_grad(x):
    th = jnp.tanh(_GELU_C * (x + _GELU_A * x * x * x))
    return 0.5 * (1.0 + th) + 0.5 * x * (1.0 - th * th) * _GELU_C * (1.0 + 3.0 * _GELU_A * x * x)


def _matmul(name, a_pieces, b_pieces, *, grid, contract, outs, acc_shape, extras=(), epilogue=None,
            a_sel=None, b_sel=None):
    na, nb, ne = len(a_pieces), len(b_pieces), len(extras)
    nk = grid[2]

    def body(*refs):
        a_refs = refs[:na]
        b_refs = refs[na:na + nb]
        e_refs = refs[na + nb:na + nb + ne]
        o_refs = refs[na + nb + ne:-1]
        acc = refs[-1]
        ids = [pl.program_id(0), pl.program_id(1), pl.program_id(2)]
        k = ids[2]

        @pl.when(k == 0)
        def _():
            acc[...] = jnp.zeros(acc_shape, f32)

        for ai, (_, _, alo, ahi) in enumerate(a_pieces):
            for bi, (_, _, blo, bhi) in enumerate(b_pieces):
                conds = []
                if na > 1:
                    conds.append((ids[a_sel] >= alo) & (ids[a_sel] < ahi))
                if nb > 1:
                    conds.append((ids[b_sel] >= blo) & (ids[b_sel] < bhi))

                def upd(ai=ai, bi=bi):
                    acc[...] += lax.dot_general(a_refs[ai][...], b_refs[bi][...], contract,
                                                preferred_element_type=f32)

                if conds:
                    pl.when(functools.reduce(lambda p, q: p & q, conds))(upd)
                else:
                    upd()

        @pl.when(k == nk - 1)
        def _():
            if epilogue is None:
                res = (acc[...],)
            else:
                res = epilogue(acc[...], *[r[...] for r in e_refs])
            for o, r in zip(o_refs, res):
                o[...] = r.astype(o.dtype)

    in_specs = [p[1] for p in a_pieces] + [p[1] for p in b_pieces] + [e[1] for e in extras]
    args = [p[0] for p in a_pieces] + [p[0] for p in b_pieces] + [e[0] for e in extras]
    res = pl.pallas_call(
        body, grid=grid, in_specs=in_specs, out_specs=[o[1] for o in outs], out_shape=[o[0] for o in outs],
        scratch_shapes=[pltpu.VMEM(acc_shape, f32)], name=name,
        compiler_params=_cparams(("parallel", "parallel", "arbitrary")))(*args)
    return res


def _clipmap(v, lo, hi):
    return jnp.clip(v - lo, 0, hi - lo - 1)


def _mm_nn(name, a, w, out_dtype, *, tm=1024, tn=1024, tk=512, extras_fn=None, epilogue=None, out_dtypes=None):
    M, K = a.shape
    sharded = w.ndim == 3
    Ns = w.shape[-1]
    N = Ns * (w.shape[0] if sharded else 1)
    tm, tn, tk = _tile(M, tm, 16), _tile(Ns, tn, 128), _tile(K, tk, 128)
    nb = Ns // tn
    grid = (M // tm, N // tn, K // tk)
    a_spec = pl.BlockSpec((tm, tk), lambda i, j, k: (i, k))
    if sharded:
        b_spec = pl.BlockSpec((None, tk, tn), lambda i, j, k: (j // nb, k, j % nb))
    else:
        b_spec = pl.BlockSpec((tk, tn), lambda i, j, k: (k, j))
    o_spec = pl.BlockSpec((tm, tn), lambda i, j, k: (i, j))
    dts = out_dtypes if out_dtypes is not None else [out_dtype]
    outs = [(jax.ShapeDtypeStruct((M, N), dt), o_spec) for dt in dts]
    extras = extras_fn(tm, tn) if extras_fn is not None else ()
    return _matmul(name, [(a, a_spec, 0, 0)], [(w, b_spec, 0, 0)], grid=grid, contract=(((1,), (0,)), ((), ())),
                   outs=outs, acc_shape=(tm, tn), extras=extras, epilogue=epilogue)


def _mm_nt(name, a_list, w, out_dtypes, *, tm=1024, tn=1024, tk=512, extras_fn=None, epilogue=None):
    M = a_list[0].shape[0]
    sharded = w.ndim == 3
    Ns = w.shape[-1]
    K = w.shape[-2]
    N = Ns * (w.shape[0] if sharded else 1)
    assert sum(a.shape[1] for a in a_list) == N
    tk = _tile(Ns, tk, 128)
    for a in a_list:
        tk = math.gcd(tk, a.shape[1])
    assert tk % 128 == 0
    tm, tn = _tile(M, tm, 16), _tile(K, tn, 128)
    nb = Ns // tk
    grid = (M // tm, K // tn, N // tk)
    pieces, lo = [], 0
    for a in a_list:
        hi = lo + a.shape[1] // tk
        pieces.append((a, pl.BlockSpec((tm, tk), lambda i, j, k, lo=lo, hi=hi: (i, _clipmap(k, lo, hi))), lo, hi))
        lo = hi
    if sharded:
        b_spec = pl.BlockSpec((None, tn, tk), lambda i, j, k: (k // nb, j, k % nb))
    else:
        b_spec = pl.BlockSpec((tn, tk), lambda i, j, k: (j, k))
    o_spec = pl.BlockSpec((tm, tn), lambda i, j, k: (i, j))
    outs = [(jax.ShapeDtypeStruct((M, K), dt), o_spec) for dt in out_dtypes]
    extras = extras_fn(tm, tn) if extras_fn is not None else ()
    return _matmul(name, pieces, [(w, b_spec, 0, 0)], grid=grid, contract=(((1,), (1,)), ((), ())),
                   outs=outs, acc_shape=(tm, tn), extras=extras, epilogue=epilogue, a_sel=2)


def _mm_tn(name, a, b_list, *, shards=None, tm=1024, tn=1024, tk=512):
    M, K = a.shape
    N = sum(b.shape[1] for b in b_list)
    Ns = N // shards if shards else N
    tn = _tile(Ns, tn, 128)
    for b in b_list:
        tn = math.gcd(tn, b.shape[1])
    assert tn % 128 == 0
    tm, tk = _tile(K, tm, 128), _tile(M, tk, 16)
    nb = Ns // tn
    grid = (K // tm, N // tn, M // tk)
    a_spec = pl.BlockSpec((tk, tm), lambda i, j, k: (k, i))
    pieces, lo = [], 0
    for b in b_list:
        hi = lo + b.shape[1] // tn
        pieces.append((b, pl.BlockSpec((tk, tn), lambda i, j, k, lo=lo, hi=hi: (k, _clipmap(j, lo, hi))), lo, hi))
        lo = hi
    if shards:
        out = (jax.ShapeDtypeStruct((shards, K, Ns), f32), pl.BlockSpec((None, tm, tn), lambda i, j, k: (j // nb, i, j % nb)))
    else:
        out = (jax.ShapeDtypeStruct((K, N), f32), pl.BlockSpec((tm, tn), lambda i, j, k: (i, j)))
    return _matmul(name, [(a, a_spec, 0, 0)], pieces, grid=grid, contract=(((0,), (0,)), ((), ())),
                   outs=[out], acc_shape=(tm, tn), b_sel=1)[0]


def _rms_fwd(name, x, g):
    L, D = x.shape
    tm = _tile(L, 256, 16)

    def body(x_ref, g_ref, xn_ref, r_ref):
        xv = x_ref[...]
        r = lax.rsqrt(jnp.mean(xv * xv, axis=-1, keepdims=True) + _EPS)
        xn_ref[...] = (xv * r * g_ref[...]).astype(bf16)
        r_ref[...] = r

    return pl.pallas_call(
        body, grid=(L // tm,),
        in_specs=[pl.BlockSpec((tm, D), lambda i: (i, 0)), pl.BlockSpec((1, D), lambda i: (0, 0))],
        out_specs=[pl.BlockSpec((tm, D), lambda i: (i, 0)), pl.BlockSpec((tm, 1), lambda i: (i, 0))],
        out_shape=[jax.ShapeDtypeStruct((L, D), bf16), jax.ShapeDtypeStruct((L, 1), f32)],
        name=name, compiler_params=_cparams(("parallel",)))(x, g)


def _rms_bwd(name, dxn, h, r, g, dres):
    L, D = h.shape
    tm = _tile(L, 256, 16)

    def body(dxn_ref, h_ref, r_ref, g_ref, dres_ref, dh_ref, dhb_ref, dg_ref):
        i = pl.program_id(0)
        d = dxn_ref[...].astype(f32)
        hv = h_ref[...]
        rv = r_ref[...]
        dyg = d * g_ref[...]
        m = jnp.mean(dyg * hv, axis=-1, keepdims=True)
        dh = dres_ref[...] + rv * dyg - hv * (rv * rv * rv) * m
        dh_ref[...] = dh
        dhb_ref[...] = dh.astype(bf16)

        @pl.when(i == 0)
        def _():
            dg_ref[...] = jnp.zeros_like(dg_ref)

        dg_ref[...] += jnp.sum(d * hv * rv, axis=0, keepdims=True)

    row = lambda i: (i, 0)
    return pl.pallas_call(
        body, grid=(L // tm,),
        in_specs=[pl.BlockSpec((tm, D), row), pl.BlockSpec((tm, D), row), pl.BlockSpec((tm, 1), row),
                  pl.BlockSpec((1, D), lambda i: (0, 0)), pl.BlockSpec((tm, D), row)],
        out_specs=[pl.BlockSpec((tm, D), row), pl.BlockSpec((tm, D), row), pl.BlockSpec((1, D), lambda i: (0, 0))],
        out_shape=[jax.ShapeDtypeStruct((L, D), f32), jax.ShapeDtypeStruct((L, D), bf16), jax.ShapeDtypeStruct((1, D), f32)],
        name=name, compiler_params=_cparams(("arbitrary",)))(dxn, h, r, g, dres)


def _loss_head(name, h2, tgt, g):
    L, D = h2.shape
    tm = _tile(L, 256, 16)

    def body(h_ref, t_ref, g_ref, dh_ref, dhb_ref, dg_ref, loss_ref):
        i = pl.program_id(0)
        hv = h_ref[...]
        gv = g_ref[...]
        r = lax.rsqrt(jnp.mean(hv * hv, axis=-1, keepdims=True) + _EPS)
        err = hv * r * gv - t_ref[...]
        dy = err * (1.0 / D)
        dyg = dy * gv
        m = jnp.mean(dyg * hv, axis=-1, keepdims=True)
        dh = r * dyg - hv * (r * r * r) * m
        dh_ref[...] = dh
        dhb_ref[...] = dh.astype(bf16)

        @pl.when(i == 0)
        def _():
            dg_ref[...] = jnp.zeros_like(dg_ref)
            loss_ref[...] = jnp.zeros_like(loss_ref)

        dg_ref[...] += jnp.sum(dy * hv * r, axis=0, keepdims=True)
        part = jnp.sum(jnp.sum(err * err, axis=-1, keepdims=True), axis=0, keepdims=True) * (0.5 / D)
        loss_ref[...] += jnp.broadcast_to(part, (8, 128))

    row = lambda i: (i, 0)
    return pl.pallas_call(
        body, grid=(L // tm,),
        in_specs=[pl.BlockSpec((tm, D), row), pl.BlockSpec((tm, D), row), pl.BlockSpec((1, D), lambda i: (0, 0))],
        out_specs=[pl.BlockSpec((tm, D), row), pl.BlockSpec((tm, D), row), pl.BlockSpec((1, D), lambda i: (0, 0)),
                   pl.BlockSpec((8, 128), lambda i: (0, 0))],
        out_shape=[jax.ShapeDtypeStruct((L, D), f32), jax.ShapeDtypeStruct((L, D), bf16),
                   jax.ShapeDtypeStruct((1, D), f32), jax.ShapeDtypeStruct((8, 128), f32)],
        name=name, compiler_params=_cparams(("arbitrary",)))(h2, tgt, g)


def _shift_down(tile, halo, k, rows8):
    tm = tile.shape[0]
    r = pltpu.roll(tile, k, axis=0)
    hh = pltpu.roll(halo, k, axis=0)
    top = jnp.where(rows8 < k, hh, r[:8])
    return jnp.concatenate([top, r[8:]], axis=0) if tm > 8 else top


def _shift_up(tile, halo, k, rows8):
    tm = tile.shape[0]
    r = pltpu.roll(tile, tm - k, axis=0)
    hh = pltpu.roll(halo, 8 - k, axis=0)
    bot = jnp.where(rows8 >= 8 - k, hh, r[tm - 8:])
    return jnp.concatenate([r[:tm - 8], bot], axis=0) if tm > 8 else bot


def _conv3(x, halo, w_ref, b_ref, rows8):
    return (w_ref[0:1, :] * _shift_down(x, halo, 2, rows8) + w_ref[1:2, :] * _shift_down(x, halo, 1, rows8)
            + w_ref[2:3, :] * x + b_ref[...])


def _prev_halo_spec(tm, tc, col):
    return pl.BlockSpec((16, tc), lambda i, *_: (jnp.maximum(i * (tm // 16) - 1, 0), col))


def _convb_fwd(cfg, proj, w, b):
    L, CW = cfg.L, cfg.CW
    assert cfg.SW == CW
    tm = _tile(L, 512, 16)

    def body(v_ref, vh_ref, gb_ref, gc_ref, gch_ref, w_ref, b_ref, o_ref):
        i = pl.program_id(0)
        rows8 = lax.broadcasted_iota(jnp.int32, (8, CW), 0)
        cv = gc_ref[...].astype(f32) * v_ref[...].astype(f32)
        cvh = gch_ref[...].astype(f32)[8:] * vh_ref[...].astype(f32)[8:]
        cvh = jnp.where(i == 0, 0.0, cvh)
        cc = _conv3(cv, cvh, w_ref, b_ref, rows8)
        o_ref[...] = (gb_ref[...].astype(f32) * cc).astype(bf16)

    blk = lambda col: pl.BlockSpec((tm, CW), lambda i: (i, col))
    halo = lambda col: pl.BlockSpec((16, CW), lambda i: (jnp.maximum(i * (tm // 16) - 1, 0), col))
    return pl.pallas_call(
        body, grid=(L // tm,),
        in_specs=[blk(1), halo(1), blk(2), blk(3), halo(3),
                  pl.BlockSpec((3, CW), lambda i: (0, 0)), pl.BlockSpec((1, CW), lambda i: (0, 0))],
        out_specs=pl.BlockSpec((tm, CW), lambda i: (i, 0)),
        out_shape=jax.ShapeDtypeStruct((L, CW), bf16),
        name="convb_fwd", compiler_params=_cparams(("parallel",)))(proj, proj, proj, proj, proj, w, b)


def _convb_bwd(cfg, proj, dyb0, w, b):
    L, CW = cfg.L, cfg.CW
    tm = _tile(L, 512, 16)
    nt = L // tm

    def body(v_ref, vh_ref, gb_ref, gbn_ref, gc_ref, gch_ref, d_ref, dn_ref, w_ref, b_ref, o_ref, dw_ref, db_ref):
        i = pl.program_id(0)
        rows8 = lax.broadcasted_iota(jnp.int32, (8, CW), 0)
        v = v_ref[...].astype(f32)
        gb = gb_ref[...].astype(f32)
        gc = gc_ref[...].astype(f32)
        d = d_ref[...].astype(f32)
        cv = gc * v
        cvh = gch_ref[...].astype(f32)[8:] * vh_ref[...].astype(f32)[8:]
        cvh = jnp.where(i == 0, 0.0, cvh)
        s2 = _shift_down(cv, cvh, 2, rows8)
        s1 = _shift_down(cv, cvh, 1, rows8)
        cc = w_ref[0:1, :] * s2 + w_ref[1:2, :] * s1 + w_ref[2:3, :] * cv + b_ref[...]
        dcc = d * gb
        dccn = dn_ref[...].astype(f32)[:8] * gbn_ref[...].astype(f32)[:8]
        dccn = jnp.where(i == nt - 1, 0.0, dccn)
        dcv = (w_ref[2:3, :] * dcc + w_ref[1:2, :] * _shift_up(dcc, dccn, 1, rows8)
               + w_ref[0:1, :] * _shift_up(dcc, dccn, 2, rows8))
        o_ref[:, 0:CW] = (dcv * gc).astype(bf16)
        o_ref[:, CW:2 * CW] = (d * cc).astype(bf16)
        o_ref[:, 2 * CW:3 * CW] = (dcv * v).astype(bf16)

        @pl.when(i == 0)
        def _():
            dw_ref[...] = jnp.zeros_like(dw_ref)
            db_ref[...] = jnp.zeros_like(db_ref)

        dw_ref[0:1, :] += jnp.sum(dcc * s2, axis=0, keepdims=True)
        dw_ref[1:2, :] += jnp.sum(dcc * s1, axis=0, keepdims=True)
        dw_ref[2:3, :] += jnp.sum(dcc * cv, axis=0, keepdims=True)
        db_ref[...] += jnp.sum(dcc, axis=0, keepdims=True)

    blk = lambda col: pl.BlockSpec((tm, CW), lambda i: (i, col))
    prev = lambda col: pl.BlockSpec((16, CW), lambda i: (jnp.maximum(i * (tm // 16) - 1, 0), col))
    nxt = lambda col: pl.BlockSpec((16, CW), lambda i: (jnp.minimum((i + 1) * (tm // 16), L // 16 - 1), col))
    const = lambda r: pl.BlockSpec((r, CW), lambda i: (0, 0))
    return pl.pallas_call(
        body, grid=(nt,),
        in_specs=[blk(1), prev(1), blk(2), nxt(2), blk(3), prev(3), blk(0), nxt(0), const(3), const(1)],
        out_specs=[pl.BlockSpec((tm, 3 * CW), lambda i: (i, 0)), const(3), const(1)],
        out_shape=[jax.ShapeDtypeStruct((L, 3 * CW), bf16), jax.ShapeDtypeStruct((3, CW), f32),
                   jax.ShapeDtypeStruct((1, CW), f32)],
        name="convb_bwd", compiler_params=_cparams(("arbitrary",)))(proj, proj, proj, proj, proj, proj, dyb0, dyb0, w, b)


def _ffn_act(cfg, hh, w, b):
    L, F = cfg.L, cfg.F
    tm = _tile(L, 512, 16)
    tc = _tile(F, 1408, 128)
    ncb = F // tc

    def body(a_ref, ah_ref, g_ref, w_ref, b_ref, o_ref):
        i = pl.program_id(0)
        rows8 = lax.broadcasted_iota(jnp.int32, (8, tc), 0)
        a = a_ref[...].astype(f32)
        ah = jnp.where(i == 0, 0.0, ah_ref[...].astype(f32)[8:])
        o_ref[...] = (_gelu(_conv3(a, ah, w_ref, b_ref, rows8)) * g_ref[...].astype(f32)).astype(bf16)

    return pl.pallas_call(
        body, grid=(L // tm, ncb),
        in_specs=[pl.BlockSpec((tm, tc), lambda i, j: (i, j)),
                  pl.BlockSpec((16, tc), lambda i, j: (jnp.maximum(i * (tm // 16) - 1, 0), j)),
                  pl.BlockSpec((tm, tc), lambda i, j: (i, j + ncb)),
                  pl.BlockSpec((3, tc), lambda i, j: (0, j)), pl.BlockSpec((1, tc), lambda i, j: (0, j))],
        out_specs=pl.BlockSpec((tm, tc), lambda i, j: (i, j)),
        out_shape=jax.ShapeDtypeStruct((L, F), bf16),
        name="ffn_act", compiler_params=_cparams(("parallel", "parallel")))(hh, hh, hh, w, b)


def _ffn_act_bwd(cfg, hh, df, w, b):
    L, F = cfg.L, cfg.F
    tm = _tile(L, 512, 16)
    tc = _tile(F, 1408, 128)
    ncb = F // tc
    nt = L // tm

    def body(a_ref, ah_ref, an_ref, g_ref, gn_ref, d_ref, dn_ref, w_ref, b_ref, da_ref, dg_ref, dw_ref, db_ref):
        i = pl.program_id(1)
        rows8 = lax.broadcasted_iota(jnp.int32, (8, tc), 0)
        a = a_ref[...].astype(f32)
        ah = jnp.where(i == 0, 0.0, ah_ref[...].astype(f32)[8:])
        s2 = _shift_down(a, ah, 2, rows8)
        s1 = _shift_down(a, ah, 1, rows8)
        act = w_ref[0:1, :] * s2 + w_ref[1:2, :] * s1 + w_ref[2:3, :] * a + b_ref[...]
        d = d_ref[...].astype(f32)
        dg_ref[...] = (d * _gelu(act)).astype(bf16)
        dact = d * g_ref[...].astype(f32) * _gelu_grad(act)
        an = an_ref[...].astype(f32)[:8]
        actn = _conv3(an, a[tm - 8:], w_ref, b_ref, rows8)
        dactn = dn_ref[...].astype(f32)[:8] * gn_ref[...].astype(f32)[:8] * _gelu_grad(actn)
        dactn = jnp.where(i == nt - 1, 0.0, dactn)
        da = (w_ref[2:3, :] * dact + w_ref[1:2, :] * _shift_up(dact, dactn, 1, rows8)
              + w_ref[0:1, :] * _shift_up(dact, dactn, 2, rows8))
        da_ref[...] = da.astype(bf16)

        @pl.when(i == 0)
        def _():
            dw_ref[...] = jnp.zeros_like(dw_ref)
            db_ref[...] = jnp.zeros_like(db_ref)

        dw_ref[0:1, :] += jnp.sum(dact * s2, axis=0, keepdims=True)
        dw_ref[1:2, :] += jnp.sum(dact * s1, axis=0, keepdims=True)
        dw_ref[2:3, :] += jnp.sum(dact * a, axis=0, keepdims=True)
        db_ref[...] += jnp.sum(dact, axis=0, keepdims=True)

    blk = lambda off: pl.BlockSpec((tm, tc), lambda j, i: (i, j + off))
    prev = lambda off: pl.BlockSpec((16, tc), lambda j, i: (jnp.maximum(i * (tm // 16) - 1, 0), j + off))
    nxt = lambda off: pl.BlockSpec((16, tc), lambda j, i: (jnp.minimum((i + 1) * (tm // 16), L // 16 - 1), j + off))
    const = lambda r: pl.BlockSpec((r, tc), lambda j, i: (0, j))
    return pl.pallas_call(
        body, grid=(ncb, nt),
        in_specs=[blk(0), prev(0), nxt(0), blk(ncb), nxt(ncb), blk(0), nxt(0), const(3), const(1)],
        out_specs=[blk(0), blk(0), const(3), const(1)],
        out_shape=[jax.ShapeDtypeStruct((L, F), bf16), jax.ShapeDtypeStruct((L, F), bf16),
                   jax.ShapeDtypeStruct((3, F), f32), jax.ShapeDtypeStruct((1, F), f32)],
        name="ffn_act_bwd", compiler_params=_cparams(("parallel", "arbitrary")))(hh, hh, hh, hh, hh, df, df, w, b)


def _merge_fwd(cfg, ya1, yb0, proj, wso, wco):
    L, D, SW, CW = cfg.L, cfg.D, cfg.SW, cfg.CW
    Ns = D // _NCHIP
    tm = _tile(L, 1024, 16)
    tn = _tile(Ns, 512, 128)
    nb = Ns // tn
    off_a = (SW + 3 * CW) // tn
    off_b = (SW + 3 * CW + D) // tn

    def body(a_ref, b_ref, wa_ref, wb_ref, ma_ref, mb_ref, m_ref, ya_ref, yb_ref):
        ya = jnp.dot(a_ref[...], wa_ref[...], preferred_element_type=f32)
        yb = jnp.dot(b_ref[...], wb_ref[...], preferred_element_type=f32)
        sa = jax.nn.sigmoid(ma_ref[...].astype(f32))
        sb = jax.nn.sigmoid(mb_ref[...].astype(f32))
        m_ref[...] = (sa * ya + sb * yb).astype(bf16)
        ya_ref[...] = ya.astype(bf16)
        yb_ref[...] = yb.astype(bf16)

    o_spec = pl.BlockSpec((tm, tn), lambda i, j: (i, j))
    o_shape = jax.ShapeDtypeStruct((L, D), bf16)
    return pl.pallas_call(
        body, grid=(L // tm, D // tn),
        in_specs=[pl.BlockSpec((tm, SW), lambda i, j: (i, 0)), pl.BlockSpec((tm, CW), lambda i, j: (i, 0)),
                  pl.BlockSpec((None, SW, tn), lambda i, j: (j // nb, 0, j % nb)),
                  pl.BlockSpec((None, CW, tn), lambda i, j: (j // nb, 0, j % nb)),
                  pl.BlockSpec((tm, tn), lambda i, j: (i, off_a + j)), pl.BlockSpec((tm, tn), lambda i, j: (i, off_b + j))],
        out_specs=[o_spec, o_spec, o_spec], out_shape=[o_shape, o_shape, o_shape],
        name="merge_fwd", compiler_params=_cparams(("parallel", "parallel")))(ya1, yb0, wso, wco, proj, proj)


def _s5_dims(cfg):
    G = cfg.SW // _SSM_GROUP
    NS = G // _SLAB_GROUPS
    SC = _SLAB_GROUPS * _SSM_GROUP
    SH = _SLAB_GROUPS * _SSM_STATE
    NST = 2 * SH * NS
    return G, NS, SC, SH, NST


def _lane_slabs(cfg, W):
    _, NS, _, SH, _ = _s5_dims(cfg)
    return [(2 * SH * s + w0, 2 * SH * s + SH + w0) for s in range(NS) for w0 in range(0, SH, W)]


def _discretize(a_re, a_im, log_dt, b_re, b_im):
    dt = jnp.exp(log_dt)[:, None]
    mag = jnp.exp(dt * a_re)
    abr = mag * jnp.cos(dt * a_im)
    abi = mag * jnp.sin(dt * a_im)
    nr = abr - 1.0
    ni = abi
    den = a_re * a_re + a_im * a_im
    fr = (nr * a_re + ni * a_im) / den
    fi = (ni * a_re - nr * a_im) / den
    bbr = fr[..., None] * b_re - fi[..., None] * b_im
    bbi = fr[..., None] * b_im + fi[..., None] * b_re
    return abr, abi, bbr, bbi


def _state_rows(cfg, re, im):
    _, NS, _, SH, _ = _s5_dims(cfg)
    return jnp.concatenate([re.reshape(NS, SH), im.reshape(NS, SH)], axis=1).reshape(-1)


def _s5_tables(cfg, abr, abi, bbr, bbi, c_re, c_im):
    G, NS, SC, SH, NST = _s5_dims(cfg)
    S = cfg.T // 8
    eye = jnp.eye(_SLAB_GROUPS, dtype=f32)
    bb = jnp.stack([bbr, bbi]).reshape(2, NS, _SLAB_GROUPS, _SSM_STATE, _SSM_GROUP)
    bs = jnp.einsum("rsgph,gq->sghrqp", bb, eye).reshape(NS, SC, 2 * SH).astype(bf16)
    cc = jnp.stack([c_re, -c_im]).reshape(2, NS, _SLAB_GROUPS, _SSM_GROUP, _SSM_STATE)
    cs = jnp.einsum("rsghp,gq->srqpgh", cc, eye).reshape(NS, 2 * SH, SC).astype(bf16)
    arep = jnp.broadcast_to(_state_rows(cfg, abr, abi)[None, :], (8, NST))
    pr, pi = abr, abi
    for _ in range(S - 1):
        pr, pi = pr * abr - pi * abi, pr * abi + pi * abr
    apow = jnp.broadcast_to(_state_rows(cfg, pr, pi)[None, :], (8, NST))
    t = jnp.arange(cfg.T)
    perm = (t % 8) * S + t // 8
    pm = jax.nn.one_hot(perm, cfg.T, dtype=bf16)
    return bs, cs, arep, apow, pm, pm.T


def _cmul_add(ar, ai, xr, xi, br, bi):
    return ar * xr - ai * xi + br, ar * xi + ai * xr + bi


def _s5_forward_chunk(cfg, W, upb, bs_ref, arep_ref, apow_ref, st, x0, cin_store):
    _, NS, SC, SH, _ = _s5_dims(cfg)
    S = cfg.T // 8
    for s in range(NS):
        st[:, 2 * SH * s:2 * SH * (s + 1)] = jnp.dot(upb[:, SC * s:SC * (s + 1)], bs_ref[s], preferred_element_type=f32)
    rows = lax.broadcasted_iota(jnp.int32, (8, W), 0)
    zero = jnp.zeros((8, W), f32)
    for rc, ic in _lane_slabs(cfg, W):
        ar = arep_ref[:, rc:rc + W]
        ai = arep_ref[:, ic:ic + W]

        def step(i, carry, rc=rc, ic=ic, ar=ar, ai=ai):
            xr, xi = carry
            r0 = pl.multiple_of(i * 8, 8)
            nr, ni = _cmul_add(ar, ai, xr, xi, st[pl.ds(r0, 8), rc:rc + W], st[pl.ds(r0, 8), ic:ic + W])
            st[pl.ds(r0, 8), rc:rc + W] = nr
            st[pl.ds(r0, 8), ic:ic + W] = ni
            return nr, ni

        er, ei = lax.fori_loop(0, S, step, (zero, zero))
        pr = apow_ref[:, rc:rc + W]
        pi = apow_ref[:, ic:ic + W]
        x0r = x0[:, rc:rc + W]
        x0i = x0[:, ic:ic + W]
        cr = jnp.where(rows == 0, x0r, 0.0)
        ci = jnp.where(rows == 0, x0i, 0.0)
        for _ in range(7):
            fr, fi = _cmul_add(pr, pi, cr, ci, er, ei)
            cr = jnp.where(rows == 0, x0r, pltpu.roll(fr, 1, axis=0))
            ci = jnp.where(rows == 0, x0i, pltpu.roll(fi, 1, axis=0))
        fr, fi = _cmul_add(pr, pi, cr, ci, er, ei)
        x0[:, rc:rc + W] = jnp.broadcast_to(fr[7:8, :], (8, W))
        x0[:, ic:ic + W] = jnp.broadcast_to(fi[7:8, :], (8, W))
        cin_store(rc, ic, cr, ci)

        def fix(i, carry, rc=rc, ic=ic, ar=ar, ai=ai):
            kr, ki = carry
            r0 = pl.multiple_of(i * 8, 8)
            nr, ni = ar * kr - ai * ki, ar * ki + ai * kr
            st[pl.ds(r0, 8), rc:rc + W] = st[pl.ds(r0, 8), rc:rc + W] + nr
            st[pl.ds(r0, 8), ic:ic + W] = st[pl.ds(r0, 8), ic:ic + W] + ni
            return nr, ni

        lax.fori_loop(0, S, fix, (cr, ci))


def _s5_fwd(cfg, proj, tabs, dskip):
    L, SW, T = cfg.L, cfg.SW, cfg.T
    G, NS, SC, SH, NST = _s5_dims(cfg)
    bs, cs, arep, apow, pm, pmt = tabs
    W = min(512, SH)
    NC = L // T

    def body(u_ref, pm_ref, pmt_ref, bs_ref, cs_ref, arep_ref, apow_ref, dskip_ref, y_ref, ya0_ref, cin_ref, st, x0):
        c = pl.program_id(0)

        @pl.when(c == 0)
        def _():
            x0[...] = jnp.zeros_like(x0)

        up = jnp.dot(pm_ref[...], u_ref[...], preferred_element_type=f32)
        upb = up.astype(bf16)

        def cin_store(rc, ic, cr, ci):
            cin_ref[0, :, rc:rc + W] = cr
            cin_ref[0, :, ic:ic + W] = ci

        _s5_forward_chunk(cfg, W, upb, bs_ref, arep_ref, apow_ref, st, x0, cin_store)
        yp = jnp.concatenate(
            [jnp.dot(st[:, 2 * SH * s:2 * SH * (s + 1)].astype(bf16), cs_ref[s], preferred_element_type=f32)
             for s in range(NS)], axis=1) + dskip_ref[...] * up
        y = jnp.dot(pmt_ref[...], yp.astype(bf16), preferred_element_type=f32)
        y_ref[...] = y.astype(bf16)
        ya0_ref[...] = _gelu(y).astype(bf16)

    const2 = lambda shape: pl.BlockSpec(shape, lambda c: (0, 0))
    const3 = lambda shape: pl.BlockSpec(shape, lambda c: (0, 0, 0))
    return pl.pallas_call(
        body, grid=(NC,),
        in_specs=[pl.BlockSpec((T, SW), lambda c: (c, 0)), const2((T, T)), const2((T, T)), const3((NS, SC, 2 * SH)),
                  const3((NS, 2 * SH, SC)), const2((8, NST)), const2((8, NST)), const2((1, SW))],
        out_specs=[pl.BlockSpec((T, SW), lambda c: (c, 0)), pl.BlockSpec((T, SW), lambda c: (c, 0)),
                   pl.BlockSpec((1, 8, NST), lambda c: (c, 0, 0))],
        out_shape=[jax.ShapeDtypeStruct((L, SW), bf16), jax.ShapeDtypeStruct((L, SW), bf16),
                   jax.ShapeDtypeStruct((NC, 8, NST), f32)],
        scratch_shapes=[pltpu.VMEM((T, NST), f32), pltpu.VMEM((8, NST), f32)],
        name="s5_fwd", compiler_params=_cparams(("arbitrary",)))(proj, pm, pmt, bs, cs, arep, apow, dskip)


def _s5_bwd(cfg, proj, dy, cin, tabs, dskip):
    L, SW, T = cfg.L, cfg.SW, cfg.T
    G, NS, SC, SH, NST = _s5_dims(cfg)
    bs, cs, arep, apow, pm, pmt = tabs
    W = min(512, SH)
    S = T // 8
    NC = L // T

    def body(u_ref, dy_ref, cin_ref, pm_ref, pmt_ref, bs_ref, cs_ref, arep_ref, apow_ref, dskip_ref,
             du_ref, da_ref, db_hbm, dc_hbm, dd_ref, st, gs, x0, g0, db_acc, dc_acc, sem):
        c = pl.program_id(0)

        @pl.when(c == 0)
        def _():
            g0[...] = jnp.zeros_like(g0)
            da_ref[...] = jnp.zeros_like(da_ref)
            dd_ref[...] = jnp.zeros_like(dd_ref)
            db_acc[...] = jnp.zeros_like(db_acc)
            dc_acc[...] = jnp.zeros_like(dc_acc)

        up = jnp.dot(pm_ref[...], u_ref[...], preferred_element_type=f32)
        upb = up.astype(bf16)
        dyp = jnp.dot(pm_ref[...], dy_ref[...], preferred_element_type=f32)
        dypb = dyp.astype(bf16)
        x0[...] = jnp.zeros_like(x0)
        for rc, ic in _lane_slabs(cfg, W):
            x0[:, rc:rc + W] = jnp.broadcast_to(cin_ref[0, 0:1, rc:rc + W], (8, W))
            x0[:, ic:ic + W] = jnp.broadcast_to(cin_ref[0, 0:1, ic:ic + W], (8, W))
        _s5_forward_chunk(cfg, W, upb, bs_ref, arep_ref, apow_ref, st, x0, lambda *a: None)
        for s in range(NS):
            gs[:, 2 * SH * s:2 * SH * (s + 1)] = lax.dot_general(
                dypb[:, SC * s:SC * (s + 1)], cs_ref[s], (((1,), (1,)), ((), ())), preferred_element_type=f32)
        rows = lax.broadcasted_iota(jnp.int32, (8, W), 0)
        zero = jnp.zeros((8, W), f32)
        for rc, ic in _lane_slabs(cfg, W):
            ar = arep_ref[:, rc:rc + W]
            ai = arep_ref[:, ic:ic + W]

            def rstep(k, carry, rc=rc, ic=ic, ar=ar, ai=ai):
                gr, gi = carry
                r0 = pl.multiple_of((S - 1 - k) * 8, 8)
                nr = ar * gr + ai * gi + gs[pl.ds(r0, 8), rc:rc + W]
                ni = ar * gi - ai * gr + gs[pl.ds(r0, 8), ic:ic + W]
                gs[pl.ds(r0, 8), rc:rc + W] = nr
                gs[pl.ds(r0, 8), ic:ic + W] = ni
                return nr, ni

            er, ei = lax.fori_loop(0, S, rstep, (zero, zero))
            pr = apow_ref[:, rc:rc + W]
            pi = apow_ref[:, ic:ic + W]
            g0r = g0[:, rc:rc + W]
            g0i = g0[:, ic:ic + W]
            cr = jnp.where(rows == 7, g0r, 0.0)
            ci = jnp.where(rows == 7, g0i, 0.0)
            for _ in range(7):
                fr = er + pr * cr + pi * ci
                fi = ei + pr * ci - pi * cr
                cr = jnp.where(rows == 7, g0r, pltpu.roll(fr, 7, axis=0))
                ci = jnp.where(rows == 7, g0i, pltpu.roll(fi, 7, axis=0))
            fr = er + pr * cr + pi * ci
            fi = ei + pr * ci - pi * cr
            g0[:, rc:rc + W] = jnp.broadcast_to(fr[0:1, :], (8, W))
            g0[:, ic:ic + W] = jnp.broadcast_to(fi[0:1, :], (8, W))

            def fix(k, carry, rc=rc, ic=ic, ar=ar, ai=ai):
                kr, ki, accr, acci = carry
                i = S - 1 - k
                r0 = pl.multiple_of(i * 8, 8)
                rp = pl.multiple_of((i - 1) * 8, 8)
                nr = ar * kr + ai * ki
                ni = ar * ki - ai * kr
                gr = gs[pl.ds(r0, 8), rc:rc + W] + nr
                gi = gs[pl.ds(r0, 8), ic:ic + W] + ni
                gs[pl.ds(r0, 8), rc:rc + W] = gr
                gs[pl.ds(r0, 8), ic:ic + W] = gi
                xr = st[pl.ds(rp, 8), rc:rc + W]
                xi = st[pl.ds(rp, 8), ic:ic + W]
                return nr, ni, accr + gr * xr + gi * xi, acci + gi * xr - gr * xi

            kr, ki, accr, acci = lax.fori_loop(0, S - 1, fix, (cr, ci, zero, zero))
            nr = ar * kr + ai * ki
            ni = ar * ki - ai * kr
            gr = gs[0:8, rc:rc + W] + nr
            gi = gs[0:8, ic:ic + W] + ni
            gs[0:8, rc:rc + W] = gr
            gs[0:8, ic:ic + W] = gi
            xr = cin_ref[0, :, rc:rc + W]
            xi = cin_ref[0, :, ic:ic + W]
            da_ref[:, rc:rc + W] += accr + gr * xr + gi * xi
            da_ref[:, ic:ic + W] += acci + gi * xr - gr * xi

        dups = []
        for s in range(NS):
            gsb = gs[:, 2 * SH * s:2 * SH * (s + 1)].astype(bf16)
            dups.append(lax.dot_general(gsb, bs_ref[s], (((1,), (1,)), ((), ())), preferred_element_type=f32))
            db_acc[s] += lax.dot_general(upb[:, SC * s:SC * (s + 1)], gsb, (((0,), (0,)), ((), ())),
                                         preferred_element_type=f32)
            dc_acc[s] += lax.dot_general(st[:, 2 * SH * s:2 * SH * (s + 1)].astype(bf16), dypb[:, SC * s:SC * (s + 1)],
                                         (((0,), (0,)), ((), ())), preferred_element_type=f32)
        dup = jnp.concatenate(dups, axis=1) + dskip_ref[...] * dyp
        du_ref[...] = jnp.dot(pmt_ref[...], dup.astype(bf16), preferred_element_type=f32).astype(bf16)
        dd_ref[...] += jnp.sum(dyp * up, axis=0, keepdims=True)

        @pl.when(c == NC - 1)
        def _():
            cp1 = pltpu.make_async_copy(db_acc, db_hbm, sem.at[0])
            cp2 = pltpu.make_async_copy(dc_acc, dc_hbm, sem.at[1])
            cp1.start()
            cp2.start()
            cp1.wait()
            cp2.wait()

    rev = lambda c: (NC - 1 - c, 0)
    const2 = lambda shape: pl.BlockSpec(shape, lambda c: (0, 0))
    const3 = lambda shape: pl.BlockSpec(shape, lambda c: (0, 0, 0))
    return pl.pallas_call(
        body, grid=(NC,),
        in_specs=[pl.BlockSpec((T, SW), rev), pl.BlockSpec((T, SW), rev), pl.BlockSpec((1, 8, NST), lambda c: (NC - 1 - c, 0, 0)),
                  const2((T, T)), const2((T, T)), const3((NS, SC, 2 * SH)), const3((NS, 2 * SH, SC)),
                  const2((8, NST)), const2((8, NST)), const2((1, SW))],
        out_specs=[pl.BlockSpec((T, SW), rev), const2((8, NST)), pl.BlockSpec(memory_space=pl.ANY),
                   pl.BlockSpec(memory_space=pl.ANY), const2((1, SW))],
        out_shape=[jax.ShapeDtypeStruct((L, SW), bf16), jax.ShapeDtypeStruct((8, NST), f32),
                   jax.ShapeDtypeStruct((NS, SC, 2 * SH), f32), jax.ShapeDtypeStruct((NS, 2 * SH, SC), f32),
                   jax.ShapeDtypeStruct((1, SW), f32)],
        scratch_shapes=[pltpu.VMEM((T, NST), f32), pltpu.VMEM((T, NST), f32), pltpu.VMEM((8, NST), f32),
                        pltpu.VMEM((8, NST), f32), pltpu.VMEM((NS, SC, 2 * SH), f32), pltpu.VMEM((NS, 2 * SH, SC), f32),
                        pltpu.SemaphoreType.DMA((2,))],
        name="s5_bwd", compiler_params=_cparams(("arbitrary",)))(proj, dy, cin, pm, pmt, bs, cs, arep, apow, dskip)


def _s5_param_grads(cfg, da, db_full, dc_full):
    G, NS, SC, SH, NST = _s5_dims(cfg)
    das = da.sum(axis=0).reshape(NS, 2, SH)
    dabr = das[:, 0].reshape(G, _SSM_STATE)
    dabi = das[:, 1].reshape(G, _SSM_STATE)
    dbf = db_full.reshape(NS, _SLAB_GROUPS, _SSM_GROUP, 2, _SLAB_GROUPS, _SSM_STATE)
    dbb = jnp.einsum("sghrgp->rsgph", dbf).reshape(2, G, _SSM_STATE, _SSM_GROUP)
    dcf = dc_full.reshape(NS, 2, _SLAB_GROUPS, _SSM_STATE, _SLAB_GROUPS, _SSM_GROUP)
    dcc = jnp.einsum("srgpgh->rsghp", dcf).reshape(2, G, _SSM_GROUP, _SSM_STATE)
    return dabr, dabi, dbb[0], dbb[1], dcc[0], -dcc[1]


def _coords():
    return lax.axis_index("x"), lax.axis_index("y"), lax.axis_index("c")


def _other_chips(x, y):
    return [(1 - x, y), (x, 1 - y), (1 - x, 1 - y)]


def _allreduce8(name, v):
    R = v.shape[0]

    def body(v_ref, o_ref, sib, chips, mine, ssem, rsem):
        x, y, c = _coords()
        d2d = pltpu.make_async_remote_copy(src_ref=v_ref, dst_ref=sib, send_sem=ssem.at[0], recv_sem=rsem.at[0],
                                           device_id=(x, y, 1 - c), device_id_type=_MESH)
        d2d.start()
        d2d.wait()
        mine[...] = v_ref[...] + sib[...]
        cps = [pltpu.make_async_remote_copy(src_ref=mine, dst_ref=chips.at[j], send_sem=ssem.at[1 + j],
                                            recv_sem=rsem.at[1 + j], device_id=(*chip, c), device_id_type=_MESH)
               for j, chip in enumerate(_other_chips(x, y))]
        for cp in cps:
            cp.start()
        for cp in cps:
            cp.wait()
        o_ref[...] = (mine[...] + chips[1]) + (chips[0] + chips[2])

    vm = pl.BlockSpec(memory_space=pltpu.VMEM)
    return pl.pallas_call(
        body, in_specs=[vm], out_specs=vm, out_shape=jax.ShapeDtypeStruct((R, 128), f32),
        scratch_shapes=[pltpu.VMEM((R, 128), f32), pltpu.VMEM((3, R, 128), f32), pltpu.VMEM((R, 128), f32),
                        pltpu.SemaphoreType.DMA((4,)), pltpu.SemaphoreType.DMA((4,))],
        name=name, compiler_params=pltpu.CompilerParams(vmem_limit_bytes=_VMEM_LIMIT))(v)


def _allgather_weights(shards):
    n = len(shards)

    def body(*refs):
        ins, outs = refs[:n], refs[n:2 * n]
        lsem, ssem, rsem, fssem, frsem = refs[2 * n:]
        x, y, c = _coords()
        k = 2 * x + y
        others = _other_chips(x, y)
        locals_, sends, fwds = [], [], []
        for w in range(n):
            rh = ins[w].shape[0] // 2
            mine = pl.ds(c * rh, rh)
            cp = pltpu.make_async_copy(ins[w], outs[w].at[k], lsem.at[w])
            cp.start()
            locals_.append(cp)
            for j, chip in enumerate(others):
                cp = pltpu.make_async_remote_copy(
                    src_ref=ins[w].at[mine], dst_ref=outs[w].at[k, mine], send_sem=ssem.at[w, j],
                    recv_sem=rsem.at[w, j], device_id=(*chip, c), device_id_type=_MESH)
                cp.start()
                sends.append(cp)
        for w in range(n):
            rh = ins[w].shape[0] // 2
            mine = pl.ds(c * rh, rh)
            for j, (ox, oy) in enumerate(others):
                ko = 2 * ox + oy
                landed = outs[w].at[ko, mine]
                pltpu.make_async_remote_copy(
                    src_ref=landed, dst_ref=landed, send_sem=ssem.at[w, j], recv_sem=rsem.at[w, j],
                    device_id=(ox, oy, c), device_id_type=_MESH).wait_recv()
                cp = pltpu.make_async_remote_copy(
                    src_ref=landed, dst_ref=landed, send_sem=fssem.at[w, j], recv_sem=frsem.at[w, j],
                    device_id=(x, y, 1 - c), device_id_type=_MESH)
                cp.start()
                fwds.append(cp)
        for w in range(n):
            rh = ins[w].shape[0] // 2
            theirs = pl.ds((1 - c) * rh, rh)
            for j, (ox, oy) in enumerate(others):
                passed = outs[w].at[2 * ox + oy, theirs]
                pltpu.make_async_remote_copy(
                    src_ref=passed, dst_ref=passed, send_sem=fssem.at[w, j], recv_sem=frsem.at[w, j],
                    device_id=(x, y, 1 - c), device_id_type=_MESH).wait_recv()
        for cp in sends + fwds:
            cp.wait_send()
        for cp in locals_:
            cp.wait()

    hbm = pl.BlockSpec(memory_space=pl.ANY)
    return pl.pallas_call(
        body, in_specs=[hbm] * n, out_specs=[hbm] * n,
        out_shape=[jax.ShapeDtypeStruct((_NCHIP,) + s.shape, s.dtype) for s in shards],
        scratch_shapes=[pltpu.SemaphoreType.DMA((n,)), pltpu.SemaphoreType.DMA((n, 3)), pltpu.SemaphoreType.DMA((n, 3)),
                        pltpu.SemaphoreType.DMA((n, 3)), pltpu.SemaphoreType.DMA((n, 3))],
        name="allgather_weights")(*shards)


def _exchange_halves(grads):
    n = len(grads)

    def body(*refs):
        ins, outs = refs[:n], refs[n:2 * n]
        ssem, rsem = refs[2 * n:]
        x, y, c = _coords()
        cps = []
        for w in range(n):
            rh = ins[w].shape[1] // 2
            cp = pltpu.make_async_remote_copy(
                src_ref=ins[w].at[:, pl.ds((1 - c) * rh, rh)], dst_ref=outs[w], send_sem=ssem.at[w], recv_sem=rsem.at[w],
                device_id=(x, y, 1 - c), device_id_type=_MESH)
            cp.start()
            cps.append(cp)
        for cp in cps:
            cp.wait()

    hbm = pl.BlockSpec(memory_space=pl.ANY)
    return pl.pallas_call(
        body, in_specs=[hbm] * n, out_specs=[hbm] * n,
        out_shape=[jax.ShapeDtypeStruct((g.shape[0], g.shape[1] // 2, g.shape[2]), g.dtype) for g in grads],
        scratch_shapes=[pltpu.SemaphoreType.DMA((n,)), pltpu.SemaphoreType.DMA((n,))],
        name="grad_exchange_halves")(*grads)


def _scatter_shards(parts):
    n = len(parts)

    def body(*refs):
        ins, outs = refs[:n], refs[n:2 * n]
        ssem, rsem = refs[2 * n:]
        x, y, c = _coords()
        cps = []
        for w in range(n):
            for j, (ox, oy) in enumerate(_other_chips(x, y)):
                cp = pltpu.make_async_remote_copy(
                    src_ref=ins[w].at[2 * ox + oy], dst_ref=outs[w].at[j], send_sem=ssem.at[w, j], recv_sem=rsem.at[w, j],
                    device_id=(ox, oy, c), device_id_type=_MESH)
                cp.start()
                cps.append(cp)
        for cp in cps:
            cp.wait()

    hbm = pl.BlockSpec(memory_space=pl.ANY)
    return pl.pallas_call(
        body, in_specs=[hbm] * n, out_specs=[hbm] * n,
        out_shape=[jax.ShapeDtypeStruct((3,) + p.shape[1:], p.dtype) for p in parts],
        scratch_shapes=[pltpu.SemaphoreType.DMA((n, 3)), pltpu.SemaphoreType.DMA((n, 3))],
        name="grad_scatter_shards")(*parts)


def _join_halves(halves):
    n = len(halves)

    def body(*refs):
        ins, outs = refs[:n], refs[n:2 * n]
        lsem, ssem, rsem = refs[2 * n:]
        x, y, c = _coords()
        cps = []
        for w in range(n):
            rh = ins[w].shape[0]
            mine = outs[w].at[pl.ds(c * rh, rh)]
            lc = pltpu.make_async_copy(ins[w], mine, lsem.at[w])
            lc.start()
            cps.append(lc)
        rcs = []
        for w in range(n):
            rh = ins[w].shape[0]
            mine = outs[w].at[pl.ds(c * rh, rh)]
            cp = pltpu.make_async_remote_copy(src_ref=ins[w], dst_ref=mine, send_sem=ssem.at[w], recv_sem=rsem.at[w],
                                              device_id=(x, y, 1 - c), device_id_type=_MESH)
            cp.start()
            rcs.append(cp)
        for w in range(n):
            rh = ins[w].shape[0]
            theirs = outs[w].at[pl.ds((1 - c) * rh, rh)]
            pltpu.make_async_remote_copy(src_ref=ins[w], dst_ref=theirs, send_sem=ssem.at[w], recv_sem=rsem.at[w],
                                         device_id=(x, y, 1 - c), device_id_type=_MESH).wait_recv()
        for cp in rcs:
            cp.wait_send()
        for cp in cps:
            cp.wait()

    hbm = pl.BlockSpec(memory_space=pl.ANY)
    return pl.pallas_call(
        body, in_specs=[hbm] * n, out_specs=[hbm] * n,
        out_shape=[jax.ShapeDtypeStruct((2 * h.shape[0], h.shape[1]), h.dtype) for h in halves],
        scratch_shapes=[pltpu.SemaphoreType.DMA((n,)), pltpu.SemaphoreType.DMA((n,)), pltpu.SemaphoreType.DMA((n,))],
        name="grad_join_halves")(*halves)


def _add_own_half(name, g, t, c_idx):
    S, R, C = g.shape
    rh = R // 2
    tr = _tile(rh, 256, 8)
    nrb = rh // tr

    def body(c_ref, g_ref, t_ref, o_ref):
        o_ref[...] = g_ref[...] + t_ref[...]

    gs = pltpu.PrefetchScalarGridSpec(
        num_scalar_prefetch=1, grid=(S, nrb),
        in_specs=[pl.BlockSpec((None, tr, C), lambda s, r, cr: (s, cr[0] * nrb + r, 0)),
                  pl.BlockSpec((None, tr, C), lambda s, r, cr: (s, r, 0))],
        out_specs=pl.BlockSpec((None, tr, C), lambda s, r, cr: (s, r, 0)))
    return pl.pallas_call(body, grid_spec=gs, out_shape=jax.ShapeDtypeStruct((S, rh, C), f32), name=name,
                          compiler_params=_cparams(("parallel", "parallel")))(c_idx, g, t)


def _add_shard_parts(name, p, t, k_idx):
    S, rh, C = p.shape
    tr = _tile(rh, 256, 8)

    def body(k_ref, p_ref, t_ref, o_ref):
        o_ref[...] = (p_ref[...] + t_ref[1]) + (t_ref[0] + t_ref[2])

    gs = pltpu.PrefetchScalarGridSpec(
        num_scalar_prefetch=1, grid=(rh // tr,),
        in_specs=[pl.BlockSpec((None, tr, C), lambda r, kr: (kr[0], r, 0)),
                  pl.BlockSpec((3, tr, C), lambda r, kr: (0, r, 0))],
        out_specs=pl.BlockSpec((tr, C), lambda r, kr: (r, 0)))
    return pl.pallas_call(body, grid_spec=gs, out_shape=jax.ShapeDtypeStruct((rh, C), f32), name=name,
                          compiler_params=_cparams(("parallel",)))(k_idx, p, t)


def _adamw(name, w, g, m, v):
    R, C = w.shape
    tr = _tile(R, 256, 8) if R % 8 == 0 else R

    def body(w_ref, g_ref, m_ref, v_ref, d_ref, nm_ref, nv_ref):
        gv = g_ref[...]
        nm = _ADAM_B1 * m_ref[...] + (1.0 - _ADAM_B1) * gv
        nv = _ADAM_B2 * v_ref[...] + (1.0 - _ADAM_B2) * (gv * gv)
        m_hat = nm / (1.0 - _ADAM_B1 ** _ADAM_STEP)
        v_hat = nv / (1.0 - _ADAM_B2 ** _ADAM_STEP)
        d_ref[...] = -_ADAM_LR * (m_hat / (jnp.sqrt(v_hat) + _ADAM_EPS) + _ADAM_WD * w_ref[...])
        nm_ref[...] = nm
        nv_ref[...] = nv

    spec = pl.BlockSpec((tr, C), lambda i: (i, 0))
    shape = jax.ShapeDtypeStruct((R, C), f32)
    return pl.pallas_call(body, grid=(R // tr,), in_specs=[spec] * 4, out_specs=[spec] * 3, out_shape=[shape] * 3,
                          name=name, compiler_params=_cparams(("parallel",)))(w, g, m, v)


def _pack(arrs):
    flat = jnp.concatenate([a.reshape(-1).astype(f32) for a in arrs])
    n = flat.shape[0]
    pad = (-n) % 1024
    return jnp.pad(flat, (0, pad)).reshape(-1, 128)


def _unpack(packed, shapes):
    flat = packed.reshape(-1)
    out, off = [], 0
    for s in shapes:
        n = math.prod(s)
        out.append(flat[off:off + n].reshape(s))
        off += n
    return out


_BIG = ("w_in", "w_glu", "w_ssm_out", "w_conv_out", "w_o", "w_up", "w_down")
_SMALL = ("norm_tok", "a_re", "a_im", "log_dt", "b_re", "b_im", "c_re", "c_im", "d_skip", "conv_w", "conv_b",
          "norm_ffn", "ffn_conv_w", "ffn_conv_b", "norm_final")
_WEIGHTS = ("norm_tok", "w_in", "a_re", "a_im", "log_dt", "b_re", "b_im", "c_re", "c_im", "d_skip", "w_glu",
            "w_ssm_out", "conv_w", "conv_b", "w_conv_out", "w_o", "norm_ffn", "w_up", "ffn_conv_w", "ffn_conv_b",
            "w_down", "norm_final")


def _step(cfg, x, tgt, p, m, v):
    L, D, SW, CW, F = cfg.L, cfg.D, cfg.SW, cfg.CW, cfg.F
    xi, yi, ci = _coords()
    k_idx = (2 * xi + yi).astype(jnp.int32).reshape(1)
    c_idx = ci.astype(jnp.int32).reshape(1)
    x = x.reshape(L, D)
    tgt = tgt.reshape(L, D)

    big2d = {n: p[n].reshape(p[n].shape[-2], p[n].shape[-1]) for n in _BIG}
    gathered = _allgather_weights([big2d[n].astype(bf16) for n in _BIG])
    wg = dict(zip(_BIG, gathered))
    w_in, w_so, w_co, w_up = wg["w_in"], wg["w_ssm_out"], wg["w_conv_out"], wg["w_up"]
    w_glu = wg["w_glu"].reshape(SW, SW)
    w_o = wg["w_o"].reshape(D, D)
    w_down = wg["w_down"].reshape(F, D)
    kk = k_idx[0]
    cw_full = lax.dynamic_update_slice(jnp.zeros((3, CW), f32), p["conv_w"].reshape(3, CW // _NCHIP), (0, kk * (CW // _NCHIP)))
    fw_full = lax.dynamic_update_slice(jnp.zeros((3, F), f32), p["ffn_conv_w"].reshape(3, F // _NCHIP), (0, kk * (F // _NCHIP)))
    south = (ci == 0).astype(f32)
    conv_w, ffn_conv_w = _unpack(_allreduce8("allgather_conv_filters", _pack([cw_full * south, fw_full * south])),
                                 [(3, CW), (3, F)])
    conv_b = p["conv_b"].reshape(1, CW)
    ffn_conv_b = p["ffn_conv_b"].reshape(1, F)
    norm_tok = p["norm_tok"].reshape(1, D)
    norm_ffn = p["norm_ffn"].reshape(1, D)
    norm_final = p["norm_final"].reshape(1, D)
    dskip = p["d_skip"].reshape(1, SW)

    s5_in = (p["a_re"][0], p["a_im"][0], p["log_dt"][0], p["b_re"][0], p["b_im"][0])
    (abr, abi, bbr, bbi), disc_vjp = jax.vjp(_discretize, *s5_in)
    tabs = _s5_tables(cfg, abr, abi, bbr, bbi, p["c_re"][0], p["c_im"][0])

    xn1, r1 = _rms_fwd("rms_tok", x, norm_tok)
    proj = _mm_nn("in_proj", xn1, w_in, bf16)[0]
    y_s, ya0, cin = _s5_fwd(cfg, proj, tabs, dskip)

    def glu_extras(tm, tn):
        return [(ya0, pl.BlockSpec((tm, tn), lambda i, j, k: (i, j)))]

    def glu_epi(acc, a0):
        return a0.astype(f32) * jax.nn.sigmoid(acc), acc

    ya1, z = _mm_nn("glu", ya0, w_glu, None, extras_fn=glu_extras, epilogue=glu_epi, out_dtypes=[bf16, bf16])
    yb0 = _convb_fwd(cfg, proj, conv_w, conv_b)
    merged, ya, yb = _merge_fwd(cfg, ya1, yb0, proj, w_so, w_co)

    def res_extras(res):
        return lambda tm, tn: [(res, pl.BlockSpec((tm, tn), lambda i, j, k: (i, j)))]

    def res_epi(acc, res):
        return (res + acc,)

    h1 = _mm_nn("out_proj", merged, w_o, f32, extras_fn=res_extras(x), epilogue=res_epi)[0]
    xn2, r2 = _rms_fwd("rms_ffn", h1, norm_ffn)
    hh = _mm_nn("ffn_up", xn2, w_up, bf16, tn=1408)[0]
    fact = _ffn_act(cfg, hh, ffn_conv_w, ffn_conv_b)
    h2 = _mm_nn("ffn_down", fact, w_down, f32, tk=_tile(F, 1408, 128), extras_fn=res_extras(h1), epilogue=res_epi)[0]
    dh2, dh2b, g_norm_final, loss_tile = _loss_head("loss_head", h2, tgt, norm_final)

    df = _mm_nt("ffn_down_dx", [dh2b], w_down, [bf16], tn=1408)[0]
    g_w_down = _mm_tn("ffn_down_dw", fact, [dh2b], tm=1408)
    da_pre, dgate, g_ffn_conv_w, g_ffn_conv_b = _ffn_act_bwd(cfg, hh, df, ffn_conv_w, ffn_conv_b)
    dxn2 = _mm_nt("ffn_up_dx", [da_pre, dgate], w_up, [bf16], tk=1408)[0]
    g_w_up = _mm_tn("ffn_up_dw", xn2, [da_pre, dgate], shards=_NCHIP, tn=1408)
    dh1, dh1b, g_norm_ffn = _rms_bwd("rms_ffn_bwd", dxn2, h1, r2, norm_ffn, dh2)

    def merge_extras(tm, tn):
        off_a = (SW + 3 * CW) // tn
        off_b = (SW + 3 * CW + D) // tn
        blk = pl.BlockSpec((tm, tn), lambda i, j, k: (i, j))
        return [(ya, blk), (yb, blk), (proj, pl.BlockSpec((tm, tn), lambda i, j, k: (i, off_a + j))),
                (proj, pl.BlockSpec((tm, tn), lambda i, j, k: (i, off_b + j)))]

    def merge_epi(acc, ya_t, yb_t, ma_t, mb_t):
        sa = jax.nn.sigmoid(ma_t.astype(f32))
        sb = jax.nn.sigmoid(mb_t.astype(f32))
        return (acc * sa, acc * sb, acc * ya_t.astype(f32) * sa * (1.0 - sa), acc * yb_t.astype(f32) * sb * (1.0 - sb))

    dya, dyb, dma, dmb = _mm_nt("out_proj_dx", [dh1b], w_o, [bf16] * 4, extras_fn=merge_extras, epilogue=merge_epi)
    g_w_o = _mm_tn("out_proj_dw", merged, [dh1b])

    def glu_bwd_extras(tm, tn):
        blk = pl.BlockSpec((tm, tn), lambda i, j, k: (i, j))
        return [(ya0, blk), (z, blk)]

    def glu_bwd_epi(acc, a0, zz):
        s = jax.nn.sigmoid(zz.astype(f32))
        return (acc * a0.astype(f32) * s * (1.0 - s), acc * s)

    dz, t1 = _mm_nt("ssm_out_dx", [dya], w_so, [bf16, bf16], extras_fn=glu_bwd_extras, epilogue=glu_bwd_epi)
    g_w_so = _mm_tn("ssm_out_dw", ya1, [dya], shards=_NCHIP)
    dyb0 = _mm_nt("conv_out_dx", [dyb], w_co, [bf16])[0]
    g_w_co = _mm_tn("conv_out_dw", yb0, [dyb], shards=_NCHIP)
    dvbc, g_conv_w, g_conv_b = _convb_bwd(cfg, proj, dyb0, conv_w, conv_b)

    def gelu_bwd_extras(tm, tn):
        blk = pl.BlockSpec((tm, tn), lambda i, j, k: (i, j))
        return [(t1, blk), (y_s, blk)]

    def gelu_bwd_epi(acc, tt, yy):
        return ((tt.astype(f32) + acc) * _gelu_grad(yy.astype(f32)),)

    dy_s = _mm_nt("glu_dx", [dz], w_glu, [bf16], extras_fn=gelu_bwd_extras, epilogue=gelu_bwd_epi)[0]
    g_w_glu = _mm_tn("glu_dw", ya0, [dz])
    du, da_acc, db_full, dc_full, g_dskip = _s5_bwd(cfg, proj, dy_s, cin, tabs, dskip)
    dproj = [du, dvbc, dma, dmb]
    dxn1 = _mm_nt("in_proj_dx", dproj, w_in, [bf16], tk=1024)[0]
    g_w_in = _mm_tn("in_proj_dw", xn1, dproj, shards=_NCHIP)
    dx, _, g_norm_tok = _rms_bwd("rms_tok_bwd", dxn1, x, r1, norm_tok, dh1)

    dabr, dabi, dbbr, dbbi, g_c_re, g_c_im = _s5_param_grads(cfg, da_acc, db_full, dc_full)
    g_a_re, g_a_im, g_log_dt, g_b_re, g_b_im = disc_vjp((dabr, dabi, dbbr, dbbi))

    big_g = {"w_in": g_w_in, "w_glu": g_w_glu.reshape(_NCHIP, SW // _NCHIP, SW), "w_ssm_out": g_w_so,
             "w_conv_out": g_w_co, "w_o": g_w_o.reshape(_NCHIP, D // _NCHIP, D), "w_up": g_w_up,
             "w_down": g_w_down.reshape(_NCHIP, F // _NCHIP, D)}
    glist = [big_g[n] for n in _BIG]
    from_sibling = _exchange_halves(glist)
    chip_parts = [_add_own_half("grad_add_halves_" + n, g, t, c_idx) for n, g, t in zip(_BIG, glist, from_sibling)]
    from_chips = _scatter_shards(chip_parts)
    halves = [_add_shard_parts("grad_add_chips_" + n, pp, t, k_idx) for n, pp, t in zip(_BIG, chip_parts, from_chips)]
    reduced = dict(zip(_BIG, _join_halves(halves)))

    small_g = {"norm_tok": g_norm_tok, "a_re": g_a_re, "a_im": g_a_im, "log_dt": g_log_dt, "b_re": g_b_re, "b_im": g_b_im,
               "c_re": g_c_re, "c_im": g_c_im, "d_skip": g_dskip, "conv_w": g_conv_w, "conv_b": g_conv_b,
               "norm_ffn": g_norm_ffn, "ffn_conv_w": g_ffn_conv_w, "ffn_conv_b": g_ffn_conv_b, "norm_final": g_norm_final}
    small_shapes = [small_g[n].shape for n in _SMALL]
    summed = dict(zip(_SMALL, _unpack(_allreduce8("allreduce_small_grads", _pack([small_g[n] for n in _SMALL])), small_shapes)))
    summed["conv_w"] = lax.dynamic_slice(summed["conv_w"], (0, kk * (CW // _NCHIP)), (3, CW // _NCHIP))
    summed["ffn_conv_w"] = lax.dynamic_slice(summed["ffn_conv_w"], (0, kk * (F // _NCHIP)), (3, F // _NCHIP))

    grads, deltas, new_m, new_v = {}, {}, {}, {}
    for n in _BIG:
        d_, m_, v_ = _adamw("adamw_" + n, big2d[n], reduced[n], m[n].reshape(big2d[n].shape), v[n].reshape(big2d[n].shape))
        grads[n] = reduced[n].reshape(p[n].shape)
        deltas[n], new_m[n], new_v[n] = d_.reshape(p[n].shape), m_.reshape(p[n].shape), v_.reshape(p[n].shape)
    shapes = [p[n].shape for n in _SMALL]
    sg = _pack([summed[n] for n in _SMALL])
    d_, m_, v_ = _adamw("adamw_small", _pack([p[n] for n in _SMALL]), sg, _pack([m[n] for n in _SMALL]),
                        _pack([v[n] for n in _SMALL]))
    for n, dd, mm, vv in zip(_SMALL, _unpack(d_, shapes), _unpack(m_, shapes), _unpack(v_, shapes)):
        grads[n] = summed[n].reshape(p[n].shape)
        deltas[n], new_m[n], new_v[n] = dd, mm, vv

    loss = lax.psum(loss_tile[0, 0], ("x", "y", "c"))
    return (loss, dx.reshape(1, L, D), *[grads[n] for n in _WEIGHTS], *[deltas[n] for n in _WEIGHTS],
            *[new_m[n] for n in _WEIGHTS], *[new_v[n] for n in _WEIGHTS])


def kernel(x, norm_tok, w_in, a_re, a_im, log_dt, b_re, b_im, c_re, c_im, d_skip, w_glu, w_ssm_out, conv_w, conv_b, w_conv_out, w_o, norm_ffn, w_up, ffn_conv_w, ffn_conv_b, w_down, norm_final, loss_target, m_norm_tok, m_w_in, m_a_re, m_a_im, m_log_dt, m_b_re, m_b_im, m_c_re, m_c_im, m_d_skip, m_w_glu, m_w_ssm_out, m_conv_w, m_conv_b, m_w_conv_out, m_w_o, m_norm_ffn, m_w_up, m_ffn_conv_w, m_ffn_conv_b, m_w_down, m_norm_final, v_norm_tok, v_w_in, v_a_re, v_a_im, v_log_dt, v_b_re, v_b_im, v_c_re, v_c_im, v_d_skip, v_w_glu, v_w_ssm_out, v_conv_w, v_conv_b, v_w_conv_out, v_w_o, v_norm_ffn, v_w_up, v_ffn_conv_w, v_ffn_conv_b, v_w_down, v_norm_final):
    p = dict(norm_tok=norm_tok, w_in=w_in, a_re=a_re, a_im=a_im, log_dt=log_dt, b_re=b_re, b_im=b_im, c_re=c_re,
             c_im=c_im, d_skip=d_skip, w_glu=w_glu, w_ssm_out=w_ssm_out, conv_w=conv_w, conv_b=conv_b,
             w_conv_out=w_conv_out, w_o=w_o, norm_ffn=norm_ffn, w_up=w_up, ffn_conv_w=ffn_conv_w,
             ffn_conv_b=ffn_conv_b, w_down=w_down, norm_final=norm_final)
    m = dict(norm_tok=m_norm_tok, w_in=m_w_in, a_re=m_a_re, a_im=m_a_im, log_dt=m_log_dt, b_re=m_b_re, b_im=m_b_im,
             c_re=m_c_re, c_im=m_c_im, d_skip=m_d_skip, w_glu=m_w_glu, w_ssm_out=m_w_ssm_out, conv_w=m_conv_w,
             conv_b=m_conv_b, w_conv_out=m_w_conv_out, w_o=m_w_o, norm_ffn=m_norm_ffn, w_up=m_w_up,
             ffn_conv_w=m_ffn_conv_w, ffn_conv_b=m_ffn_conv_b, w_down=m_w_down, norm_final=m_norm_final)
    v = dict(norm_tok=v_norm_tok, w_in=v_w_in, a_re=v_a_re, a_im=v_a_im, log_dt=v_log_dt, b_re=v_b_re, b_im=v_b_im,
             c_re=v_c_re, c_im=v_c_im, d_skip=v_d_skip, w_glu=v_w_glu, w_ssm_out=v_w_ssm_out, conv_w=v_conv_w,
             conv_b=v_conv_b, w_conv_out=v_w_conv_out, w_o=v_w_o, norm_ffn=v_norm_ffn, w_up=v_w_up,
             ffn_conv_w=v_ffn_conv_w, ffn_conv_b=v_ffn_conv_b, w_down=v_w_down, norm_final=v_norm_final)
    return _step(_Cfg(), x, loss_target, p, m, v)
```

```python
import functools
import math
from typing import NamedTuple

import jax
import jax.numpy as jnp
from jax import lax
from jax.experimental import pallas as pl
from jax.experimental.pallas import tpu as pltpu

f32 = jnp.float32
bf16 = jnp.bfloat16
_MESH = pl.DeviceIdType.MESH

_EPS = 1e-6
_ADAM_LR = 0.001
_ADAM_B1 = 0.9
_ADAM_B2 = 0.999
_ADAM_EPS = 1e-08
_ADAM_WD = 0.01
_ADAM_STEP = 10
_SSM_GROUP = 16
_SSM_STATE = 64
_SLAB_GROUPS = 16
_NCHIP = 4
_VMEM_LIMIT = 56 * 2**20
_GELU_C = math.sqrt(2.0 / math.pi)
_GELU_A = 0.044715


class _Cfg(NamedTuple):
    L: int = 4096
    D: int = 2048
    SW: int = 1024
    CW: int = 1024
    F: int = 5632
    T: int = 256


def _tile(n, pref, align):
    t = min(n, pref)
    t -= t % align
    while t > align and n % t:
        t -= align
    assert t > 0 and n % t == 0, (n, pref, align)
    return t


def _cparams(sem):
    return pltpu.CompilerParams(dimension_semantics=sem, vmem_limit_bytes=_VMEM_LIMIT)


def _gelu(x):
    return 0.5 * x * (1.0 + jnp.tanh(_GELU_C * (x + _GELU_A * x * x * x)))


def _gelu_grad(x):
    th = jnp.tanh(_GELU_C * (x + _GELU_A * x * x * x))
    return 0.5 * (1.0 + th) + 0.5 * x * (1.0 - th * th) * _GELU_C * (1.0 + 3.0 * _GELU_A * x * x)


def _matmul(name, a_pieces, b_pieces, *, grid, contract, outs, acc_shape, extras=(), epilogue=None,
            a_sel=None, b_sel=None):
    na, nb, ne = len(a_pieces), len(b_pieces), len(extras)
    nk = grid[2]

    def body(*refs):
        a_refs = refs[:na]
        b_refs = refs[na:na + nb]
        e_refs = refs[na + nb:na + nb + ne]
        o_refs = refs[na + nb + ne:-1]
        acc = refs[-1]
        ids = [pl.program_id(0), pl.program_id(1), pl.program_id(2)]
        k = ids[2]

        @pl.when(k == 0)
        def _():
            acc[...] = jnp.zeros(acc_shape, f32)

        for ai, (_, _, alo, ahi) in enumerate(a_pieces):
            for bi, (_, _, blo, bhi) in enumerate(b_pieces):
                conds = []
                if na > 1:
                    conds.append((ids[a_sel] >= alo) & (ids[a_sel] < ahi))
                if nb > 1:
                    conds.append((ids[b_sel] >= blo) & (ids[b_sel] < bhi))

                def upd(ai=ai, bi=bi):
                    acc[...] += lax.dot_general(a_refs[ai][...], b_refs[bi][...], contract,
                                                preferred_element_type=f32)

                if conds:
                    pl.when(functools.reduce(lambda p, q: p & q, conds))(upd)
                else:
                    upd()

        @pl.when(k == nk - 1)
        def _():
            if epilogue is None:
                res = (acc[...],)
            else:
                res = epilogue(acc[...], *[r[...] for r in e_refs])
            for o, r in zip(o_refs, res):
                o[...] = r.astype(o.dtype)

    in_specs = [p[1] for p in a_pieces] + [p[1] for p in b_pieces] + [e[1] for e in extras]
    args = [p[0] for p in a_pieces] + [p[0] for p in b_pieces] + [e[0] for e in extras]
    res = pl.pallas_call(
        body, grid=grid, in_specs=in_specs, out_specs=[o[1] for o in outs], out_shape=[o[0] for o in outs],
        scratch_shapes=[pltpu.VMEM(acc_shape, f32)], name=name,
        compiler_params=_cparams(("parallel", "parallel", "arbitrary")))(*args)
    return res


def _clipmap(v, lo, hi):
    return jnp.clip(v - lo, 0, hi - lo - 1)


def _mm_nn(name, a, w, out_dtype, *, tm=1024, tn=1024, tk=512, extras_fn=None, epilogue=None, out_dtypes=None):
    M, K = a.shape
    sharded = w.ndim == 3
    Ns = w.shape[-1]
    N = Ns * (w.shape[0] if sharded else 1)
    tm, tn, tk = _tile(M, tm, 16), _tile(Ns, tn, 128), _tile(K, tk, 128)
    nb = Ns // tn
    grid = (M // tm, N // tn, K // tk)
    a_spec = pl.BlockSpec((tm, tk), lambda i, j, k: (i, k))
    if sharded:
        b_spec = pl.BlockSpec((None, tk, tn), lambda i, j, k: (j // nb, k, j % nb))
    else:
        b_spec = pl.BlockSpec((tk, tn), lambda i, j, k: (k, j))
    o_spec = pl.BlockSpec((tm, tn), lambda i, j, k: (i, j))
    dts = out_dtypes if out_dtypes is not None else [out_dtype]
    outs = [(jax.ShapeDtypeStruct((M, N), dt), o_spec) for dt in dts]
    extras = extras_fn(tm, tn) if extras_fn is not None else ()
    return _matmul(name, [(a, a_spec, 0, 0)], [(w, b_spec, 0, 0)], grid=grid, contract=(((1,), (0,)), ((), ())),
                   outs=outs, acc_shape=(tm, tn), extras=extras, epilogue=epilogue)


def _mm_nt(name, a_list, w, out_dtypes, *, tm=1024, tn=1024, tk=512, extras_fn=None, epilogue=None):
    M = a_list[0].shape[0]
    sharded = w.ndim == 3
    Ns = w.shape[-1]
    K = w.shape[-2]
    N = Ns * (w.shape[0] if sharded else 1)
    assert sum(a.shape[1] for a in a_list) == N
    tk = _tile(Ns, tk, 128)
    for a in a_list:
        tk = math.gcd(tk, a.shape[1])
    assert tk % 128 == 0
    tm, tn = _tile(M, tm, 16), _tile(K, tn, 128)
    nb = Ns // tk
    grid = (M // tm, K // tn, N // tk)
    pieces, lo = [], 0
    for a in a_list:
        hi = lo + a.shape[1] // tk
        pieces.append((a, pl.BlockSpec((tm, tk), lambda i, j, k, lo=lo, hi=hi: (i, _clipmap(k, lo, hi))), lo, hi))
        lo = hi
    if sharded:
        b_spec = pl.BlockSpec((None, tn, tk), lambda i, j, k: (k // nb, j, k % nb))
    else:
        b_spec = pl.BlockSpec((tn, tk), lambda i, j, k: (j, k))
    o_spec = pl.BlockSpec((tm, tn), lambda i, j, k: (i, j))
    outs = [(jax.ShapeDtypeStruct((M, K), dt), o_spec) for dt in out_dtypes]
    extras = extras_fn(tm, tn) if extras_fn is not None else ()
    return _matmul(name, pieces, [(w, b_spec, 0, 0)], grid=grid, contract=(((1,), (1,)), ((), ())),
                   outs=outs, acc_shape=(tm, tn), extras=extras, epilogue=epilogue, a_sel=2)


def _mm_tn(name, a, b_list, *, shards=None, tm=1024, tn=1024, tk=512):
    M, K = a.shape
    N = sum(b.shape[1] for b in b_list)
    Ns = N // shards if shards else N
    tn = _tile(Ns, tn, 128)
    for b in b_list:
        tn = math.gcd(tn, b.shape[1])
    assert tn % 128 == 0
    tm, tk = _tile(K, tm, 128), _tile(M, tk, 16)
    nb = Ns // tn
    grid = (K // tm, N // tn, M // tk)
    a_spec = pl.BlockSpec((tk, tm), lambda i, j, k: (k, i))
    pieces, lo = [], 0
    for b in b_list:
        hi = lo + b.shape[1] // tn
        pieces.append((b, pl.BlockSpec((tk, tn), lambda i, j, k, lo=lo, hi=hi: (k, _clipmap(j, lo, hi))), lo, hi))
        lo = hi
    if shards:
        out = (jax.ShapeDtypeStruct((shards, K, Ns), f32), pl.BlockSpec((None, tm, tn), lambda i, j, k: (j // nb, i, j % nb)))
    else:
        out = (jax.ShapeDtypeStruct((K, N), f32), pl.BlockSpec((tm, tn), lambda i, j, k: (i, j)))
    return _matmul(name, [(a, a_spec, 0, 0)], pieces, grid=grid, contract=(((0,), (0,)), ((), ())),
                   outs=[out], acc_shape=(tm, tn), b_sel=1)[0]


def _rms_fwd(name, x, g):
    L, D = x.shape
    tm = _tile(L, 256, 16)

    def body(x_ref, g_ref, xn_ref, r_ref):
        xv = x_ref[...]
        r = lax.rsqrt(jnp.mean(xv * xv, axis=-1, keepdims=True) + _EPS)
        xn_ref[...] = (xv * r * g_ref[...]).astype(bf16)
        r_ref[...] = r

    return pl.pallas_call(
        body, grid=(L // tm,),
        in_specs=[pl.BlockSpec((tm, D), lambda i: (i, 0)), pl.BlockSpec((1, D), lambda i: (0, 0))],
        out_specs=[pl.BlockSpec((tm, D), lambda i: (i, 0)), pl.BlockSpec((tm, 1), lambda i: (i, 0))],
        out_shape=[jax.ShapeDtypeStruct((L, D), bf16), jax.ShapeDtypeStruct((L, 1), f32)],
        name=name, compiler_params=_cparams(("parallel",)))(x, g)


def _rms_bwd(name, dxn, h, r, g, dres):
    L, D = h.shape
    tm = _tile(L, 256, 16)

    def body(dxn_ref, h_ref, r_ref, g_ref, dres_ref, dh_ref, dhb_ref, dg_ref):
        i = pl.program_id(0)
        d = dxn_ref[...].astype(f32)
        hv = h_ref[...]
        rv = r_ref[...]
        dyg = d * g_ref[...]
        m = jnp.mean(dyg * hv, axis=-1, keepdims=True)
        dh = dres_ref[...] + rv * dyg - hv * (rv * rv * rv) * m
        dh_ref[...] = dh
        dhb_ref[...] = dh.astype(bf16)

        @pl.when(i == 0)
        def _():
            dg_ref[...] = jnp.zeros_like(dg_ref)

        dg_ref[...] += jnp.sum(d * hv * rv, axis=0, keepdims=True)

    row = lambda i: (i, 0)
    return pl.pallas_call(
        body, grid=(L // tm,),
        in_specs=[pl.BlockSpec((tm, D), row), pl.BlockSpec((tm, D), row), pl.BlockSpec((tm, 1), row),
                  pl.BlockSpec((1, D), lambda i: (0, 0)), pl.BlockSpec((tm, D), row)],
        out_specs=[pl.BlockSpec((tm, D), row), pl.BlockSpec((tm, D), row), pl.BlockSpec((1, D), lambda i: (0, 0))],
        out_shape=[jax.ShapeDtypeStruct((L, D), f32), jax.ShapeDtypeStruct((L, D), bf16), jax.ShapeDtypeStruct((1, D), f32)],
        name=name, compiler_params=_cparams(("arbitrary",)))(dxn, h, r, g, dres)


def _loss_head(name, h2, tgt, g):
    L, D = h2.shape
    tm = _tile(L, 256, 16)

    def body(h_ref, t_ref, g_ref, dh_ref, dhb_ref, dg_ref, loss_ref):
        i = pl.program_id(0)
        hv = h_ref[...]
        gv = g_ref[...]
        r = lax.rsqrt(jnp.mean(hv * hv, axis=-1, keepdims=True) + _EPS)
        err = hv * r * gv - t_ref[...]
        dy = err * (1.0 / D)
        dyg = dy * gv
        m = jnp.mean(dyg * hv, axis=-1, keepdims=True)
        dh = r * dyg - hv * (r * r * r) * m
        dh_ref[...] = dh
        dhb_ref[...] = dh.astype(bf16)

        @pl.when(i == 0)
        def _():
            dg_ref[...] = jnp.zeros_like(dg_ref)
            loss_ref[...] = jnp.zeros_like(loss_ref)

        dg_ref[...] += jnp.sum(dy * hv * r, axis=0, keepdims=True)
        part = jnp.sum(jnp.sum(err * err, axis=-1, keepdims=True), axis=0, keepdims=True) * (0.5 / D)
        loss_ref[...] += jnp.broadcast_to(part, (8, 128))

    row = lambda i: (i, 0)
    return pl.pallas_call(
        body, grid=(L // tm,),
        in_specs=[pl.BlockSpec((tm, D), row), pl.BlockSpec((tm, D), row), pl.BlockSpec((1, D), lambda i: (0, 0))],
        out_specs=[pl.BlockSpec((tm, D), row), pl.BlockSpec((tm, D), row), pl.BlockSpec((1, D), lambda i: (0, 0)),
                   pl.BlockSpec((8, 128), lambda i: (0, 0))],
        out_shape=[jax.ShapeDtypeStruct((L, D), f32), jax.ShapeDtypeStruct((L, D), bf16),
                   jax.ShapeDtypeStruct((1, D), f32), jax.ShapeDtypeStruct((8, 128), f32)],
        name=name, compiler_params=_cparams(("arbitrary",)))(h2, tgt, g)


def _shift_down(tile, halo, k, rows8):
    tm = tile.shape[0]
    r = pltpu.roll(tile, k, axis=0)
    hh = pltpu.roll(halo, k, axis=0)
    top = jnp.where(rows8 < k, hh, r[:8])
    return jnp.concatenate([top, r[8:]], axis=0) if tm > 8 else top


def _shift_up(tile, halo, k, rows8):
    tm = tile.shape[0]
    r = pltpu.roll(tile, tm - k, axis=0)
    hh = pltpu.roll(halo, 8 - k, axis=0)
    bot = jnp.where(rows8 >= 8 - k, hh, r[tm - 8:])
    return jnp.concatenate([r[:tm - 8], bot], axis=0) if tm > 8 else bot


def _conv3(x, halo, w_ref, b_ref, rows8):
    return (w_ref[0:1, :] * _shift_down(x, halo, 2, rows8) + w_ref[1:2, :] * _shift_down(x, halo, 1, rows8)
            + w_ref[2:3, :] * x + b_ref[...])


def _convb_fwd(cfg, proj, w, b):
    L, CW = cfg.L, cfg.CW
    assert cfg.SW == CW
    tm = _tile(L, 512, 16)

    def body(v_ref, vh_ref, gb_ref, gc_ref, gch_ref, w_ref, b_ref, o_ref):
        i = pl.program_id(0)
        rows8 = lax.broadcasted_iota(jnp.int32, (8, CW), 0)
        cv = gc_ref[...].astype(f32) * v_ref[...].astype(f32)
        cvh = gch_ref[...].astype(f32)[8:] * vh_ref[...].astype(f32)[8:]
        cvh = jnp.where(i == 0, 0.0, cvh)
        cc = _conv3(cv, cvh, w_ref, b_ref, rows8)
        o_ref[...] = (gb_ref[...].astype(f32) * cc).astype(bf16)

    blk = lambda col: pl.BlockSpec((tm, CW), lambda i: (i, col))
    halo = lambda col: pl.BlockSpec((16, CW), lambda i: (jnp.maximum(i * (tm // 16) - 1, 0), col))
    return pl.pallas_call(
        body, grid=(L // tm,),
        in_specs=[blk(1), halo(1), blk(2), blk(3), halo(3),
                  pl.BlockSpec((3, CW), lambda i: (0, 0)), pl.BlockSpec((1, CW), lambda i: (0, 0))],
        out_specs=pl.BlockSpec((tm, CW), lambda i: (i, 0)),
        out_shape=jax.ShapeDtypeStruct((L, CW), bf16),
        name="convb_fwd", compiler_params=_cparams(("parallel",)))(proj, proj, proj, proj, proj, w, b)


def _convb_bwd(cfg, proj, dyb0, w, b):
    L, CW = cfg.L, cfg.CW
    tm = _tile(L, 512, 16)
    nt = L // tm

    def body(v_ref, vh_ref, gb_ref, gbn_ref, gc_ref, gch_ref, d_ref, dn_ref, w_ref, b_ref, o_ref, dw_ref, db_ref):
        i = pl.program_id(0)
        rows8 = lax.broadcasted_iota(jnp.int32, (8, CW), 0)
        v = v_ref[...].astype(f32)
        gb = gb_ref[...].astype(f32)
        gc = gc_ref[...].astype(f32)
        d = d_ref[...].astype(f32)
        cv = gc * v
        cvh = gch_ref[...].astype(f32)[8:] * vh_ref[...].astype(f32)[8:]
        cvh = jnp.where(i == 0, 0.0, cvh)
        s2 = _shift_down(cv, cvh, 2, rows8)
        s1 = _shift_down(cv, cvh, 1, rows8)
        cc = w_ref[0:1, :] * s2 + w_ref[1:2, :] * s1 + w_ref[2:3, :] * cv + b_ref[...]
        dcc = d * gb
        dccn = dn_ref[...].astype(f32)[:8] * gbn_ref[...].astype(f32)[:8]
        dccn = jnp.where(i == nt - 1, 0.0, dccn)
        dcv = (w_ref[2:3, :] * dcc + w_ref[1:2, :] * _shift_up(dcc, dccn, 1, rows8)
               + w_ref[0:1, :] * _shift_up(dcc, dccn, 2, rows8))
        o_ref[:, 0:CW] = (dcv * gc).astype(bf16)
        o_ref[:, CW:2 * CW] = (d * cc).astype(bf16)
        o_ref[:, 2 * CW:3 * CW] = (dcv * v).astype(bf16)

        @pl.when(i == 0)
        def _():
            dw_ref[...] = jnp.zeros_like(dw_ref)
            db_ref[...] = jnp.zeros_like(db_ref)

        dw_ref[0:1, :] += jnp.sum(dcc * s2, axis=0, keepdims=True)
        dw_ref[1:2, :] += jnp.sum(dcc * s1, axis=0, keepdims=True)
        dw_ref[2:3, :] += jnp.sum(dcc * cv, axis=0, keepdims=True)
        db_ref[...] += jnp.sum(dcc, axis=0, keepdims=True)

    blk = lambda col: pl.BlockSpec((tm, CW), lambda i: (i, col))
    prev = lambda col: pl.BlockSpec((16, CW), lambda i: (jnp.maximum(i * (tm // 16) - 1, 0), col))
    nxt = lambda col: pl.BlockSpec((16, CW), lambda i: (jnp.minimum((i + 1) * (tm // 16), L // 16 - 1), col))
    const = lambda r: pl.BlockSpec((r, CW), lambda i: (0, 0))
    return pl.pallas_call(
        body, grid=(nt,),
        in_specs=[blk(1), prev(1), blk(2), nxt(2), blk(3), prev(3), blk(0), nxt(0), const(3), const(1)],
        out_specs=[pl.BlockSpec((tm, 3 * CW), lambda i: (i, 0)), const(3), const(1)],
        out_shape=[jax.ShapeDtypeStruct((L, 3 * CW), bf16), jax.ShapeDtypeStruct((3, CW), f32),
                   jax.ShapeDtypeStruct((1, CW), f32)],
        name="convb_bwd", compiler_params=_cparams(("arbitrary",)))(proj, proj, proj, proj, proj, proj, dyb0, dyb0, w, b)


def _ffn_act(cfg, hh, w, b):
    L, F = cfg.L, cfg.F
    tm = _tile(L, 512, 16)
    tc = _tile(F, 1408, 128)
    ncb = F // tc

    def body(a_ref, ah_ref, g_ref, w_ref, b_ref, o_ref):
        i = pl.program_id(0)
        rows8 = lax.broadcasted_iota(jnp.int32, (8, tc), 0)
        a = a_ref[...].astype(f32)
        ah = jnp.where(i == 0, 0.0, ah_ref[...].astype(f32)[8:])
        o_ref[...] = (_gelu(_conv3(a, ah, w_ref, b_ref, rows8)) * g_ref[...].astype(f32)).astype(bf16)

    return pl.pallas_call(
        body, grid=(L // tm, ncb),
        in_specs=[pl.BlockSpec((tm, tc), lambda i, j: (i, j)),
                  pl.BlockSpec((16, tc), lambda i, j: (jnp.maximum(i * (tm // 16) - 1, 0), j)),
                  pl.BlockSpec((tm, tc), lambda i, j: (i, j + ncb)),
                  pl.BlockSpec((3, tc), lambda i, j: (0, j)), pl.BlockSpec((1, tc), lambda i, j: (0, j))],
        out_specs=pl.BlockSpec((tm, tc), lambda i, j: (i, j)),
        out_shape=jax.ShapeDtypeStruct((L, F), bf16),
        name="ffn_act", compiler_params=_cparams(("parallel", "parallel")))(hh, hh, hh, w, b)


def _ffn_act_bwd(cfg, hh, df, w, b):
    L, F = cfg.L, cfg.F
    tm = _tile(L, 512, 16)
    tc = _tile(F, 1408, 128)
    ncb = F // tc
    nt = L // tm

    def body(a_ref, ah_ref, an_ref, g_ref, gn_ref, d_ref, dn_ref, w_ref, b_ref, da_ref, dg_ref, dw_ref, db_ref):
        i = pl.program_id(1)
        rows8 = lax.broadcasted_iota(jnp.int32, (8, tc), 0)
        a = a_ref[...].astype(f32)
        ah = jnp.where(i == 0, 0.0, ah_ref[...].astype(f32)[8:])
        s2 = _shift_down(a, ah, 2, rows8)
        s1 = _shift_down(a, ah, 1, rows8)
        act = w_ref[0:1, :] * s2 + w_ref[1:2, :] * s1 + w_ref[2:3, :] * a + b_ref[...]
        d = d_ref[...].astype(f32)
        dg_ref[...] = (d * _gelu(act)).astype(bf16)
        dact = d * g_ref[...].astype(f32) * _gelu_grad(act)
        an = an_ref[...].astype(f32)[:8]
        actn = _conv3(an, a[tm - 8:], w_ref, b_ref, rows8)
        dactn = dn_ref[...].astype(f32)[:8] * gn_ref[...].astype(f32)[:8] * _gelu_grad(actn)
        dactn = jnp.where(i == nt - 1, 0.0, dactn)
        da = (w_ref[2:3, :] * dact + w_ref[1:2, :] * _shift_up(dact, dactn, 1, rows8)
              + w_ref[0:1, :] * _shift_up(dact, dactn, 2, rows8))
        da_ref[...] = da.astype(bf16)

        @pl.when(i == 0)
        def _():
            dw_ref[...] = jnp.zeros_like(dw_ref)
            db_ref[...] = jnp.zeros_like(db_ref)

        dw_ref[0:1, :] += jnp.sum(dact * s2, axis=0, keepdims=True)
        dw_ref[1:2, :] += jnp.sum(dact * s1, axis=0, keepdims=True)
        dw_ref[2:3, :] += jnp.sum(dact * a, axis=0, keepdims=True)
        db_ref[...] += jnp.sum(dact, axis=0, keepdims=True)

    blk = lambda off: pl.BlockSpec((tm, tc), lambda j, i: (i, j + off))
    prev = lambda off: pl.BlockSpec((16, tc), lambda j, i: (jnp.maximum(i * (tm // 16) - 1, 0), j + off))
    nxt = lambda off: pl.BlockSpec((16, tc), lambda j, i: (jnp.minimum((i + 1) * (tm // 16), L // 16 - 1), j + off))
    const = lambda r: pl.BlockSpec((r, tc), lambda j, i: (0, j))
    return pl.pallas_call(
        body, grid=(ncb, nt),
        in_specs=[blk(0), prev(0), nxt(0), blk(ncb), nxt(ncb), blk(0), nxt(0), const(3), const(1)],
        out_specs=[blk(0), blk(0), const(3), const(1)],
        out_shape=[jax.ShapeDtypeStruct((L, F), bf16), jax.ShapeDtypeStruct((L, F), bf16),
                   jax.ShapeDtypeStruct((3, F), f32), jax.ShapeDtypeStruct((1, F), f32)],
        name="ffn_act_bwd", compiler_params=_cparams(("parallel", "arbitrary")))(hh, hh, hh, hh, hh, df, df, w, b)


def _merge_fwd(cfg, ya1, yb0, proj, wso, wco):
    L, D, SW, CW = cfg.L, cfg.D, cfg.SW, cfg.CW
    Ns = D // _NCHIP
    tm = _tile(L, 1024, 16)
    tn = _tile(Ns, 512, 128)
    nb = Ns // tn
    off_a = (SW + 3 * CW) // tn
    off_b = (SW + 3 * CW + D) // tn

    def body(a_ref, b_ref, wa_ref, wb_ref, ma_ref, mb_ref, m_ref, ya_ref, yb_ref):
        ya = jnp.dot(a_ref[...], wa_ref[...], preferred_element_type=f32)
        yb = jnp.dot(b_ref[...], wb_ref[...], preferred_element_type=f32)
        sa = jax.nn.sigmoid(ma_ref[...].astype(f32))
        sb = jax.nn.sigmoid(mb_ref[...].astype(f32))
        m_ref[...] = (sa * ya + sb * yb).astype(bf16)
        ya_ref[...] = ya.astype(bf16)
        yb_ref[...] = yb.astype(bf16)

    o_spec = pl.BlockSpec((tm, tn), lambda i, j: (i, j))
    o_shape = jax.ShapeDtypeStruct((L, D), bf16)
    return pl.pallas_call(
        body, grid=(L // tm, D // tn),
        in_specs=[pl.BlockSpec((tm, SW), lambda i, j: (i, 0)), pl.BlockSpec((tm, CW), lambda i, j: (i, 0)),
                  pl.BlockSpec((None, SW, tn), lambda i, j: (j // nb, 0, j % nb)),
                  pl.BlockSpec((None, CW, tn), lambda i, j: (j // nb, 0, j % nb)),
                  pl.BlockSpec((tm, tn), lambda i, j: (i, off_a + j)), pl.BlockSpec((tm, tn), lambda i, j: (i, off_b + j))],
        out_specs=[o_spec, o_spec, o_spec], out_shape=[o_shape, o_shape, o_shape],
        name="merge_fwd", compiler_params=_cparams(("parallel", "parallel")))(ya1, yb0, wso, wco, proj, proj)


def _s5_dims(cfg):
    G = cfg.SW // _SSM_GROUP
    NS = G // _SLAB_GROUPS
    SC = _SLAB_GROUPS * _SSM_GROUP
    SH = _SLAB_GROUPS * _SSM_STATE
    NST = 2 * SH * NS
    return G, NS, SC, SH, NST


def _lane_slabs(cfg, W):
    _, NS, _, SH, _ = _s5_dims(cfg)
    return [(2 * SH * s + w0, 2 * SH * s + SH + w0) for s in range(NS) for w0 in range(0, SH, W)]


def _discretize(a_re, a_im, log_dt, b_re, b_im):
    dt = jnp.exp(log_dt)[:, None]
    mag = jnp.exp(dt * a_re)
    abr = mag * jnp.cos(dt * a_im)
    abi = mag * jnp.sin(dt * a_im)
    nr = abr - 1.0
    ni = abi
    den = a_re * a_re + a_im * a_im
    fr = (nr * a_re + ni * a_im) / den
    fi = (ni * a_re - nr * a_im) / den
    bbr = fr[..., None] * b_re - fi[..., None] * b_im
    bbi = fr[..., None] * b_im + fi[..., None] * b_re
    return abr, abi, bbr, bbi


def _state_rows(cfg, re, im):
    _, NS, _, SH, _ = _s5_dims(cfg)
    return jnp.concatenate([re.reshape(NS, SH), im.reshape(NS, SH)], axis=1).reshape(-1)


def _s5_tables(cfg, abr, abi, bbr, bbi, c_re, c_im):
    G, NS, SC, SH, NST = _s5_dims(cfg)
    S = cfg.T // 8
    eye = jnp.eye(_SLAB_GROUPS, dtype=f32)
    bb = jnp.stack([bbr, bbi]).reshape(2, NS, _SLAB_GROUPS, _SSM_STATE, _SSM_GROUP)
    bs = jnp.einsum("rsgph,gq->sghrqp", bb, eye).reshape(NS, SC, 2 * SH).astype(bf16)
    cc = jnp.stack([c_re, -c_im]).reshape(2, NS, _SLAB_GROUPS, _SSM_GROUP, _SSM_STATE)
    cs = jnp.einsum("rsghp,gq->srqpgh", cc, eye).reshape(NS, 2 * SH, SC).astype(bf16)
    arep = jnp.broadcast_to(_state_rows(cfg, abr, abi)[None, :], (8, NST))
    pr, pi = abr, abi
    for _ in range(S - 1):
        pr, pi = pr * abr - pi * abi, pr * abi + pi * abr
    apow = jnp.broadcast_to(_state_rows(cfg, pr, pi)[None, :], (8, NST))
    t = jnp.arange(cfg.T)
    perm = (t % 8) * S + t // 8
    pm = jax.nn.one_hot(perm, cfg.T, dtype=bf16)
    return bs, cs, arep, apow, pm, pm.T


def _cmul_add(ar, ai, xr, xi, br, bi):
    return ar * xr - ai * xi + br, ar * xi + ai * xr + bi


def _s5_forward_chunk(cfg, W, upb, bs_ref, arep_ref, apow_ref, st, x0, cin_store):
    _, NS, SC, SH, _ = _s5_dims(cfg)
    S = cfg.T // 8
    for s in range(NS):
        st[:, 2 * SH * s:2 * SH * (s + 1)] = jnp.dot(upb[:, SC * s:SC * (s + 1)], bs_ref[s], preferred_element_type=f32)
    rows = lax.broadcasted_iota(jnp.int32, (8, W), 0)
    zero = jnp.zeros((8, W), f32)
    for rc, ic in _lane_slabs(cfg, W):
        ar = arep_ref[:, rc:rc + W]
        ai = arep_ref[:, ic:ic + W]

        def step(i, carry, rc=rc, ic=ic, ar=ar, ai=ai):
            xr, xi = carry
            r0 = pl.multiple_of(i * 8, 8)
            nr, ni = _cmul_add(ar, ai, xr, xi, st[pl.ds(r0, 8), rc:rc + W], st[pl.ds(r0, 8), ic:ic + W])
            st[pl.ds(r0, 8), rc:rc + W] = nr
            st[pl.ds(r0, 8), ic:ic + W] = ni
            return nr, ni

        er, ei = lax.fori_loop(0, S, step, (zero, zero))
        pr = apow_ref[:, rc:rc + W]
        pi = apow_ref[:, ic:ic + W]
        x0r = x0[:, rc:rc + W]
        x0i = x0[:, ic:ic + W]
        cr = jnp.where(rows == 0, x0r, 0.0)
        ci = jnp.where(rows == 0, x0i, 0.0)
        for _ in range(7):
            fr, fi = _cmul_add(pr, pi, cr, ci, er, ei)
            cr = jnp.where(rows == 0, x0r, pltpu.roll(fr, 1, axis=0))
            ci = jnp.where(rows == 0, x0i, pltpu.roll(fi, 1, axis=0))
        fr, fi = _cmul_add(pr, pi, cr, ci, er, ei)
        x0[:, rc:rc + W] = jnp.broadcast_to(fr[7:8, :], (8, W))
        x0[:, ic:ic + W] = jnp.broadcast_to(fi[7:8, :], (8, W))
        cin_store(rc, ic, cr, ci)

        def fix(i, carry, rc=rc, ic=ic, ar=ar, ai=ai):
            kr, ki = carry
            r0 = pl.multiple_of(i * 8, 8)
            nr, ni = ar * kr - ai * ki, ar * ki + ai * kr
            st[pl.ds(r0, 8), rc:rc + W] = st[pl.ds(r0, 8), rc:rc + W] + nr
            st[pl.ds(r0, 8), ic:ic + W] = st[pl.ds(r0, 8), ic:ic + W] + ni
            return nr, ni

        lax.fori_loop(0, S, fix, (cr, ci))


def _s5_fwd(cfg, proj, tabs, dskip):
    L, SW, T = cfg.L, cfg.SW, cfg.T
    G, NS, SC, SH, NST = _s5_dims(cfg)
    bs, cs, arep, apow, pm, pmt = tabs
    W = min(512, SH)
    NC = L // T

    def body(u_ref, pm_ref, pmt_ref, bs_ref, cs_ref, arep_ref, apow_ref, dskip_ref, y_ref, ya0_ref, cin_ref, st, x0):
        c = pl.program_id(0)

        @pl.when(c == 0)
        def _():
            x0[...] = jnp.zeros_like(x0)

        up = jnp.dot(pm_ref[...], u_ref[...], preferred_element_type=f32)
        upb = up.astype(bf16)

        def cin_store(rc, ic, cr, ci):
            cin_ref[0, :, rc:rc + W] = cr
            cin_ref[0, :, ic:ic + W] = ci

        _s5_forward_chunk(cfg, W, upb, bs_ref, arep_ref, apow_ref, st, x0, cin_store)
        yp = jnp.concatenate(
            [jnp.dot(st[:, 2 * SH * s:2 * SH * (s + 1)].astype(bf16), cs_ref[s], preferred_element_type=f32)
             for s in range(NS)], axis=1) + dskip_ref[...] * up
        y = jnp.dot(pmt_ref[...], yp.astype(bf16), preferred_element_type=f32)
        y_ref[...] = y.astype(bf16)
        ya0_ref[...] = _gelu(y).astype(bf16)

    const2 = lambda shape: pl.BlockSpec(shape, lambda c: (0, 0))
    const3 = lambda shape: pl.BlockSpec(shape, lambda c: (0, 0, 0))
    return pl.pallas_call(
        body, grid=(NC,),
        in_specs=[pl.BlockSpec((T, SW), lambda c: (c, 0)), const2((T, T)), const2((T, T)), const3((NS, SC, 2 * SH)),
                  const3((NS, 2 * SH, SC)), const2((8, NST)), const2((8, NST)), const2((1, SW))],
        out_specs=[pl.BlockSpec((T, SW), lambda c: (c, 0)), pl.BlockSpec((T, SW), lambda c: (c, 0)),
                   pl.BlockSpec((1, 8, NST), lambda c: (c, 0, 0))],
        out_shape=[jax.ShapeDtypeStruct((L, SW), bf16), jax.ShapeDtypeStruct((L, SW), bf16),
                   jax.ShapeDtypeStruct((NC, 8, NST), f32)],
        scratch_shapes=[pltpu.VMEM((T, NST), f32), pltpu.VMEM((8, NST), f32)],
        name="s5_fwd", compiler_params=_cparams(("arbitrary",)))(proj, pm, pmt, bs, cs, arep, apow, dskip)


def _s5_bwd(cfg, proj, dy, cin, tabs, dskip):
    L, SW, T = cfg.L, cfg.SW, cfg.T
    G, NS, SC, SH, NST = _s5_dims(cfg)
    bs, cs, arep, apow, pm, pmt = tabs
    W = min(512, SH)
    S = T // 8
    NC = L // T

    def body(u_ref, dy_ref, cin_ref, pm_ref, pmt_ref, bs_ref, cs_ref, arep_ref, apow_ref, dskip_ref,
             du_ref, da_ref, db_hbm, dc_hbm, dd_ref, st, gs, x0, g0, db_acc, dc_acc, sem):
        c = pl.program_id(0)

        @pl.when(c == 0)
        def _():
            g0[...] = jnp.zeros_like(g0)
            da_ref[...] = jnp.zeros_like(da_ref)
            dd_ref[...] = jnp.zeros_like(dd_ref)
            db_acc[...] = jnp.zeros_like(db_acc)
            dc_acc[...] = jnp.zeros_like(dc_acc)

        up = jnp.dot(pm_ref[...], u_ref[...], preferred_element_type=f32)
        upb = up.astype(bf16)
        dyp = jnp.dot(pm_ref[...], dy_ref[...], preferred_element_type=f32)
        dypb = dyp.astype(bf16)
        x0[...] = jnp.zeros_like(x0)
        for rc, ic in _lane_slabs(cfg, W):
            x0[:, rc:rc + W] = jnp.broadcast_to(cin_ref[0, 0:1, rc:rc + W], (8, W))
            x0[:, ic:ic + W] = jnp.broadcast_to(cin_ref[0, 0:1, ic:ic + W], (8, W))
        _s5_forward_chunk(cfg, W, upb, bs_ref, arep_ref, apow_ref, st, x0, lambda *a: None)
        for s in range(NS):
            gs[:, 2 * SH * s:2 * SH * (s + 1)] = lax.dot_general(
                dypb[:, SC * s:SC * (s + 1)], cs_ref[s], (((1,), (1,)), ((), ())), preferred_element_type=f32)
        rows = lax.broadcasted_iota(jnp.int32, (8, W), 0)
        zero = jnp.zeros((8, W), f32)
        for rc, ic in _lane_slabs(cfg, W):
            ar = arep_ref[:, rc:rc + W]
            ai = arep_ref[:, ic:ic + W]

            def rstep(k, carry, rc=rc, ic=ic, ar=ar, ai=ai):
                gr, gi = carry
                r0 = pl.multiple_of((S - 1 - k) * 8, 8)
                nr = ar * gr + ai * gi + gs[pl.ds(r0, 8), rc:rc + W]
                ni = ar * gi - ai * gr + gs[pl.ds(r0, 8), ic:ic + W]
                gs[pl.ds(r0, 8), rc:rc + W] = nr
                gs[pl.ds(r0, 8), ic:ic + W] = ni
                return nr, ni

            er, ei = lax.fori_loop(0, S, rstep, (zero, zero))
            pr = apow_ref[:, rc:rc + W]
            pi = apow_ref[:, ic:ic + W]
            g0r = g0[:, rc:rc + W]
            g0i = g0[:, ic:ic + W]
            cr = jnp.where(rows == 7, g0r, 0.0)
            ci = jnp.where(rows == 7, g0i, 0.0)
            for _ in range(7):
                fr = er + pr * cr + pi * ci
                fi = ei + pr * ci - pi * cr
                cr = jnp.where(rows == 7, g0r, pltpu.roll(fr, 7, axis=0))
                ci = jnp.where(rows == 7, g0i, pltpu.roll(fi, 7, axis=0))
            fr = er + pr * cr + pi * ci
            fi = ei + pr * ci - pi * cr
            g0[:, rc:rc + W] = jnp.broadcast_to(fr[0:1, :], (8, W))
            g0[:, ic:ic + W] = jnp.broadcast_to(fi[0:1, :], (8, W))

            def fix(k, carry, rc=rc, ic=ic, ar=ar, ai=ai):
                kr, ki, accr, acci = carry
                i = S - 1 - k
                r0 = pl.multiple_of(i * 8, 8)
                rp = pl.multiple_of((i - 1) * 8, 8)
                nr = ar * kr + ai * ki
                ni = ar * ki - ai * kr
                gr = gs[pl.ds(r0, 8), rc:rc + W] + nr
                gi = gs[pl.ds(r0, 8), ic:ic + W] + ni
                gs[pl.ds(r0, 8), rc:rc + W] = gr
                gs[pl.ds(r0, 8), ic:ic + W] = gi
                xr = st[pl.ds(rp, 8), rc:rc + W]
                xi = st[pl.ds(rp, 8), ic:ic + W]
                return nr, ni, accr + gr * xr + gi * xi, acci + gi * xr - gr * xi

            kr, ki, accr, acci = lax.fori_loop(0, S - 1, fix, (cr, ci, zero, zero))
            nr = ar * kr + ai * ki
            ni = ar * ki - ai * kr
            gr = gs[0:8, rc:rc + W] + nr
            gi = gs[0:8, ic:ic + W] + ni
            gs[0:8, rc:rc + W] = gr
            gs[0:8, ic:ic + W] = gi
            xr = cin_ref[0, :, rc:rc + W]
            xi = cin_ref[0, :, ic:ic + W]
            da_ref[:, rc:rc + W] += accr + gr * xr + gi * xi
            da_ref[:, ic:ic + W] += acci + gi * xr - gr * xi

        dups = []
        for s in range(NS):
            gsb = gs[:, 2 * SH * s:2 * SH * (s + 1)].astype(bf16)
            dups.append(lax.dot_general(gsb, bs_ref[s], (((1,), (1,)), ((), ())), preferred_element_type=f32))
            db_acc[s] += lax.dot_general(upb[:, SC * s:SC * (s + 1)], gsb, (((0,), (0,)), ((), ())),
                                         preferred_element_type=f32)
            dc_acc[s] += lax.dot_general(st[:, 2 * SH * s:2 * SH * (s + 1)].astype(bf16), dypb[:, SC * s:SC * (s + 1)],
                                         (((0,), (0,)), ((), ())), preferred_element_type=f32)
        dup = jnp.concatenate(dups, axis=1) + dskip_ref[...] * dyp
        du_ref[...] = jnp.dot(pmt_ref[...], dup.astype(bf16), preferred_element_type=f32).astype(bf16)
        dd_ref[...] += jnp.sum(dyp * up, axis=0, keepdims=True)

        @pl.when(c == NC - 1)
        def _():
            cp1 = pltpu.make_async_copy(db_acc, db_hbm, sem.at[0])
            cp2 = pltpu.make_async_copy(dc_acc, dc_hbm, sem.at[1])
            cp1.start()
            cp2.start()
            cp1.wait()
            cp2.wait()

    rev = lambda c: (NC - 1 - c, 0)
    const2 = lambda shape: pl.BlockSpec(shape, lambda c: (0, 0))
    const3 = lambda shape: pl.BlockSpec(shape, lambda c: (0, 0, 0))
    return pl.pallas_call(
        body, grid=(NC,),
        in_specs=[pl.BlockSpec((T, SW), rev), pl.BlockSpec((T, SW), rev), pl.BlockSpec((1, 8, NST), lambda c: (NC - 1 - c, 0, 0)),
                  const2((T, T)), const2((T, T)), const3((NS, SC, 2 * SH)), const3((NS, 2 * SH, SC)),
                  const2((8, NST)), const2((8, NST)), const2((1, SW))],
        out_specs=[pl.BlockSpec((T, SW), rev), const2((8, NST)), pl.BlockSpec(memory_space=pl.ANY),
                   pl.BlockSpec(memory_space=pl.ANY), const2((1, SW))],
        out_shape=[jax.ShapeDtypeStruct((L, SW), bf16), jax.ShapeDtypeStruct((8, NST), f32),
                   jax.ShapeDtypeStruct((NS, SC, 2 * SH), f32), jax.ShapeDtypeStruct((NS, 2 * SH, SC), f32),
                   jax.ShapeDtypeStruct((1, SW), f32)],
        scratch_shapes=[pltpu.VMEM((T, NST), f32), pltpu.VMEM((T, NST), f32), pltpu.VMEM((8, NST), f32),
                        pltpu.VMEM((8, NST), f32), pltpu.VMEM((NS, SC, 2 * SH), f32), pltpu.VMEM((NS, 2 * SH, SC), f32),
                        pltpu.SemaphoreType.DMA((2,))],
        name="s5_bwd", compiler_params=_cparams(("arbitrary",)))(proj, dy, cin, pm, pmt, bs, cs, arep, apow, dskip)


def _s5_param_grads(cfg, da, db_full, dc_full):
    G, NS, SC, SH, NST = _s5_dims(cfg)
    das = da.sum(axis=0).reshape(NS, 2, SH)
    dabr = das[:, 0].reshape(G, _SSM_STATE)
    dabi = das[:, 1].reshape(G, _SSM_STATE)
    dbf = db_full.reshape(NS, _SLAB_GROUPS, _SSM_GROUP, 2, _SLAB_GROUPS, _SSM_STATE)
    dbb = jnp.einsum("sghrgp->rsgph", dbf).reshape(2, G, _SSM_STATE, _SSM_GROUP)
    dcf = dc_full.reshape(NS, 2, _SLAB_GROUPS, _SSM_STATE, _SLAB_GROUPS, _SSM_GROUP)
    dcc = jnp.einsum("srgpgh->rsghp", dcf).reshape(2, G, _SSM_GROUP, _SSM_STATE)
    return dabr, dabi, dbb[0], dbb[1], dcc[0], -dcc[1]


def _coords():
    return lax.axis_index("x"), lax.axis_index("y"), lax.axis_index("c")


def _other_chips(x, y):
    return [(1 - x, y), (x, 1 - y), (1 - x, 1 - y)]


def _allreduce8(name, v):
    R = v.shape[0]

    def body(v_ref, o_ref, sib, chips, mine, ssem, rsem):
        x, y, c = _coords()
        d2d = pltpu.make_async_remote_copy(src_ref=v_ref, dst_ref=sib, send_sem=ssem.at[0], recv_sem=rsem.at[0],
                                           device_id=(x, y, 1 - c), device_id_type=_MESH)
        d2d.start()
        d2d.wait()
        mine[...] = v_ref[...] + sib[...]
        cps = [pltpu.make_async_remote_copy(src_ref=mine, dst_ref=chips.at[j], send_sem=ssem.at[1 + j],
                                            recv_sem=rsem.at[1 + j], device_id=(*chip, c), device_id_type=_MESH)
               for j, chip in enumerate(_other_chips(x, y))]
        for cp in cps:
            cp.start()
        for cp in cps:
            cp.wait()
        o_ref[...] = (mine[...] + chips[1]) + (chips[0] + chips[2])

    vm = pl.BlockSpec(memory_space=pltpu.VMEM)
    return pl.pallas_call(
        body, in_specs=[vm], out_specs=vm, out_shape=jax.ShapeDtypeStruct((R, 128), f32),
        scratch_shapes=[pltpu.VMEM((R, 128), f32), pltpu.VMEM((3, R, 128), f32), pltpu.VMEM((R, 128), f32),
                        pltpu.SemaphoreType.DMA((4,)), pltpu.SemaphoreType.DMA((4,))],
        name=name, compiler_params=pltpu.CompilerParams(vmem_limit_bytes=_VMEM_LIMIT))(v)


def _cast_into_slot(name, w, k_idx):
    R, C = w.shape
    tr = _tile(R, 256, 16)

    def body(k_ref, w_ref, o_ref):
        o_ref[...] = w_ref[...].astype(bf16)

    gs = pltpu.PrefetchScalarGridSpec(
        num_scalar_prefetch=1, grid=(R // tr,),
        in_specs=[pl.BlockSpec((tr, C), lambda r, kr: (r, 0))],
        out_specs=pl.BlockSpec((None, tr, C), lambda r, kr: (kr[0], r, 0)))
    return pl.pallas_call(body, grid_spec=gs, out_shape=jax.ShapeDtypeStruct((_NCHIP, R, C), bf16), name=name,
                          compiler_params=_cparams(("parallel",)))(k_idx, w)


def _allgather_weights(bufs):
    n = len(bufs)

    def body(*refs):
        outs = refs[n:2 * n]
        ssem, rsem, fssem, frsem = refs[2 * n:]
        x, y, c = _coords()
        k = 2 * x + y
        others = _other_chips(x, y)
        sends, fwds = [], []
        for w in range(n):
            rh = outs[w].shape[1] // 2
            mine = outs[w].at[k, pl.ds(c * rh, rh)]
            for j, chip in enumerate(others):
                cp = pltpu.make_async_remote_copy(
                    src_ref=mine, dst_ref=mine, send_sem=ssem.at[w, j], recv_sem=rsem.at[w, j],
                    device_id=(*chip, c), device_id_type=_MESH)
                cp.start()
                sends.append(cp)
        for w in range(n):
            rh = outs[w].shape[1] // 2
            for j, (ox, oy) in enumerate(others):
                landed = outs[w].at[2 * ox + oy, pl.ds(c * rh, rh)]
                pltpu.make_async_remote_copy(
                    src_ref=landed, dst_ref=landed, send_sem=ssem.at[w, j], recv_sem=rsem.at[w, j],
                    device_id=(ox, oy, c), device_id_type=_MESH).wait_recv()
                cp = pltpu.make_async_remote_copy(
                    src_ref=landed, dst_ref=landed, send_sem=fssem.at[w, j], recv_sem=frsem.at[w, j],
                    device_id=(x, y, 1 - c), device_id_type=_MESH)
                cp.start()
                fwds.append(cp)
        for w in range(n):
            rh = outs[w].shape[1] // 2
            for j, (ox, oy) in enumerate(others):
                passed = outs[w].at[2 * ox + oy, pl.ds((1 - c) * rh, rh)]
                pltpu.make_async_remote_copy(
                    src_ref=passed, dst_ref=passed, send_sem=fssem.at[w, j], recv_sem=frsem.at[w, j],
                    device_id=(x, y, 1 - c), device_id_type=_MESH).wait_recv()
        for cp in sends + fwds:
            cp.wait_send()

    hbm = pl.BlockSpec(memory_space=pl.ANY)
    return pl.pallas_call(
        body, in_specs=[hbm] * n, out_specs=[hbm] * n,
        out_shape=[jax.ShapeDtypeStruct(b.shape, b.dtype) for b in bufs],
        input_output_aliases={i: i for i in range(n)},
        scratch_shapes=[pltpu.SemaphoreType.DMA((n, 3)), pltpu.SemaphoreType.DMA((n, 3)),
                        pltpu.SemaphoreType.DMA((n, 3)), pltpu.SemaphoreType.DMA((n, 3))],
        name="allgather_weights")(*bufs)


def _exchange_halves(grads):
    n = len(grads)

    def body(*refs):
        ins, outs = refs[:n], refs[n:2 * n]
        ssem, rsem = refs[2 * n:]
        x, y, c = _coords()
        cps = []
        for w in range(n):
            rh = ins[w].shape[1] // 2
            cp = pltpu.make_async_remote_copy(
                src_ref=ins[w].at[:, pl.ds((1 - c) * rh, rh)], dst_ref=outs[w], send_sem=ssem.at[w], recv_sem=rsem.at[w],
                device_id=(x, y, 1 - c), device_id_type=_MESH)
            cp.start()
            cps.append(cp)
        for cp in cps:
            cp.wait()

    hbm = pl.BlockSpec(memory_space=pl.ANY)
    return pl.pallas_call(
        body, in_specs=[hbm] * n, out_specs=[hbm] * n,
        out_shape=[jax.ShapeDtypeStruct((g.shape[0], g.shape[1] // 2, g.shape[2]), g.dtype) for g in grads],
        scratch_shapes=[pltpu.SemaphoreType.DMA((n,)), pltpu.SemaphoreType.DMA((n,))],
        name="grad_exchange_halves")(*grads)


def _scatter_shards(parts):
    n = len(parts)

    def body(*refs):
        ins, outs = refs[:n], refs[n:2 * n]
        ssem, rsem = refs[2 * n:]
        x, y, c = _coords()
        cps = []
        for w in range(n):
            for j, (ox, oy) in enumerate(_other_chips(x, y)):
                cp = pltpu.make_async_remote_copy(
                    src_ref=ins[w].at[2 * ox + oy], dst_ref=outs[w].at[j], send_sem=ssem.at[w, j], recv_sem=rsem.at[w, j],
                    device_id=(ox, oy, c), device_id_type=_MESH)
                cp.start()
                cps.append(cp)
        for cp in cps:
            cp.wait()

    hbm = pl.BlockSpec(memory_space=pl.ANY)
    return pl.pallas_call(
        body, in_specs=[hbm] * n, out_specs=[hbm] * n,
        out_shape=[jax.ShapeDtypeStruct((3,) + p.shape[1:], p.dtype) for p in parts],
        scratch_shapes=[pltpu.SemaphoreType.DMA((n, 3)), pltpu.SemaphoreType.DMA((n, 3))],
        name="grad_scatter_shards")(*parts)


def _join_halves(bufs):
    n = len(bufs)

    def body(*refs):
        outs = refs[n:2 * n]
        ssem, rsem = refs[2 * n:]
        x, y, c = _coords()
        cps = []
        for w in range(n):
            rh = outs[w].shape[0] // 2
            mine = outs[w].at[pl.ds(c * rh, rh)]
            cp = pltpu.make_async_remote_copy(src_ref=mine, dst_ref=mine, send_sem=ssem.at[w], recv_sem=rsem.at[w],
                                              device_id=(x, y, 1 - c), device_id_type=_MESH)
            cp.start()
            cps.append(cp)
        for w in range(n):
            rh = outs[w].shape[0] // 2
            theirs = outs[w].at[pl.ds((1 - c) * rh, rh)]
            pltpu.make_async_remote_copy(src_ref=theirs, dst_ref=theirs, send_sem=ssem.at[w], recv_sem=rsem.at[w],
                                         device_id=(x, y, 1 - c), device_id_type=_MESH).wait_recv()
        for cp in cps:
            cp.wait_send()

    hbm = pl.BlockSpec(memory_space=pl.ANY)
    return pl.pallas_call(
        body, in_specs=[hbm] * n, out_specs=[hbm] * n,
        out_shape=[jax.ShapeDtypeStruct(b.shape, b.dtype) for b in bufs],
        input_output_aliases={i: i for i in range(n)},
        scratch_shapes=[pltpu.SemaphoreType.DMA((n,)), pltpu.SemaphoreType.DMA((n,))],
        name="grad_join_halves")(*bufs)


def _add_own_half(name, g, t, c_idx):
    S, R, C = g.shape
    rh = R // 2
    tr = _tile(rh, 256, 16)
    nrb = rh // tr

    def body(c_ref, g_ref, t_ref, o_ref):
        o_ref[...] = (g_ref[...] + t_ref[...]).astype(bf16)

    gs = pltpu.PrefetchScalarGridSpec(
        num_scalar_prefetch=1, grid=(S, nrb),
        in_specs=[pl.BlockSpec((None, tr, C), lambda s, r, cr: (s, cr[0] * nrb + r, 0)),
                  pl.BlockSpec((None, tr, C), lambda s, r, cr: (s, r, 0))],
        out_specs=pl.BlockSpec((None, tr, C), lambda s, r, cr: (s, r, 0)))
    return pl.pallas_call(body, grid_spec=gs, out_shape=jax.ShapeDtypeStruct((S, rh, C), bf16), name=name,
                          compiler_params=_cparams(("parallel", "parallel")))(c_idx, g, t)


def _add_shard_parts(name, g, t, r, kc_idx):
    S, R, C = g.shape
    rh = R // 2
    tr = _tile(rh, 256, 16)
    nrb = rh // tr

    def body(kc_ref, g_ref, t_ref, r_ref, o_ref):
        own = g_ref[...] + t_ref[...]
        o_ref[...] = (own + r_ref[1].astype(f32)) + (r_ref[0].astype(f32) + r_ref[2].astype(f32))

    gs = pltpu.PrefetchScalarGridSpec(
        num_scalar_prefetch=1, grid=(nrb,),
        in_specs=[pl.BlockSpec((None, tr, C), lambda i, kc: (kc[0], kc[1] * nrb + i, 0)),
                  pl.BlockSpec((None, tr, C), lambda i, kc: (kc[0], i, 0)),
                  pl.BlockSpec((3, tr, C), lambda i, kc: (0, i, 0))],
        out_specs=pl.BlockSpec((tr, C), lambda i, kc: (kc[1] * nrb + i, 0)))
    return pl.pallas_call(body, grid_spec=gs, out_shape=jax.ShapeDtypeStruct((R, C), f32), name=name,
                          compiler_params=_cparams(("parallel",)))(kc_idx, g, t, r)


def _adamw(name, w, g, m, v):
    R, C = w.shape
    tr = _tile(R, 128, 8)

    def body(w_ref, g_ref, m_ref, v_ref, d_ref, nm_ref, nv_ref):
        gv = g_ref[...]
        nm = _ADAM_B1 * m_ref[...] + (1.0 - _ADAM_B1) * gv
        nv = _ADAM_B2 * v_ref[...] + (1.0 - _ADAM_B2) * (gv * gv)
        m_hat = nm / (1.0 - _ADAM_B1 ** _ADAM_STEP)
        v_hat = nv / (1.0 - _ADAM_B2 ** _ADAM_STEP)
        d_ref[...] = -_ADAM_LR * (m_hat / (jnp.sqrt(v_hat) + _ADAM_EPS) + _ADAM_WD * w_ref[...])
        nm_ref[...] = nm
        nv_ref[...] = nv

    spec = pl.BlockSpec((tr, C), lambda i: (i, 0))
    shape = jax.ShapeDtypeStruct((R, C), f32)
    return pl.pallas_call(body, grid=(R // tr,), in_specs=[spec] * 4, out_specs=[spec] * 3, out_shape=[shape] * 3,
                          name=name, compiler_params=_cparams(("parallel",)))(w, g, m, v)


def _pack(arrs):
    flat = jnp.concatenate([a.reshape(-1).astype(f32) for a in arrs])
    n = flat.shape[0]
    pad = (-n) % (128 * 128)
    return jnp.pad(flat, (0, pad)).reshape(-1, 128)


def _unpack(packed, shapes):
    flat = packed.reshape(-1)
    out, off = [], 0
    for s in shapes:
        n = math.prod(s)
        out.append(flat[off:off + n].reshape(s))
        off += n
    return out


_BIG = ("w_in", "w_glu", "w_ssm_out", "w_conv_out", "w_o", "w_up", "w_down")
_SMALL = ("norm_tok", "a_re", "a_im", "log_dt", "b_re", "b_im", "c_re", "c_im", "d_skip", "conv_w", "conv_b",
          "norm_ffn", "ffn_conv_w", "ffn_conv_b", "norm_final")
_WEIGHTS = ("norm_tok", "w_in", "a_re", "a_im", "log_dt", "b_re", "b_im", "c_re", "c_im", "d_skip", "w_glu",
            "w_ssm_out", "conv_w", "conv_b", "w_conv_out", "w_o", "norm_ffn", "w_up", "ffn_conv_w", "ffn_conv_b",
            "w_down", "norm_final")


def _step(cfg, x, tgt, p, m, v):
    L, D, SW, CW, F = cfg.L, cfg.D, cfg.SW, cfg.CW, cfg.F
    xi, yi, ci = _coords()
    k_idx = (2 * xi + yi).astype(jnp.int32).reshape(1)
    c_idx = ci.astype(jnp.int32).reshape(1)
    x = x.reshape(L, D)
    tgt = tgt.reshape(L, D)

    big2d = {n: p[n].reshape(p[n].shape[-2], p[n].shape[-1]) for n in _BIG}
    gathered = _allgather_weights([_cast_into_slot("cast_" + n, big2d[n], k_idx) for n in _BIG])
    wg = dict(zip(_BIG, gathered))
    w_in, w_so, w_co, w_up = wg["w_in"], wg["w_ssm_out"], wg["w_conv_out"], wg["w_up"]
    w_glu = wg["w_glu"].reshape(SW, SW)
    w_o = wg["w_o"].reshape(D, D)
    w_down = wg["w_down"].reshape(F, D)
    kk = k_idx[0]
    cw_full = lax.dynamic_update_slice(jnp.zeros((3, CW), f32), p["conv_w"].reshape(3, CW // _NCHIP), (0, kk * (CW // _NCHIP)))
    fw_full = lax.dynamic_update_slice(jnp.zeros((3, F), f32), p["ffn_conv_w"].reshape(3, F // _NCHIP), (0, kk * (F // _NCHIP)))
    south = (ci == 0).astype(f32)
    conv_w, ffn_conv_w = _unpack(_allreduce8("allgather_conv_filters", _pack([cw_full * south, fw_full * south])),
                                 [(3, CW), (3, F)])
    conv_b = p["conv_b"].reshape(1, CW)
    ffn_conv_b = p["ffn_conv_b"].reshape(1, F)
    norm_tok = p["norm_tok"].reshape(1, D)
    norm_ffn = p["norm_ffn"].reshape(1, D)
    norm_final = p["norm_final"].reshape(1, D)
    dskip = p["d_skip"].reshape(1, SW)

    s5_in = (p["a_re"][0], p["a_im"][0], p["log_dt"][0], p["b_re"][0], p["b_im"][0])
    (abr, abi, bbr, bbi), disc_vjp = jax.vjp(_discretize, *s5_in)
    tabs = _s5_tables(cfg, abr, abi, bbr, bbi, p["c_re"][0], p["c_im"][0])

    xn1, r1 = _rms_fwd("rms_tok", x, norm_tok)
    proj = _mm_nn("in_proj", xn1, w_in, bf16)[0]
    y_s, ya0, cin = _s5_fwd(cfg, proj, tabs, dskip)

    def glu_extras(tm, tn):
        return [(ya0, pl.BlockSpec((tm, tn), lambda i, j, k: (i, j)))]

    def glu_epi(acc, a0):
        return a0.astype(f32) * jax.nn.sigmoid(acc), acc

    ya1, z = _mm_nn("glu", ya0, w_glu, None, extras_fn=glu_extras, epilogue=glu_epi, out_dtypes=[bf16, bf16])
    yb0 = _convb_fwd(cfg, proj, conv_w, conv_b)
    merged, ya, yb = _merge_fwd(cfg, ya1, yb0, proj, w_so, w_co)

    def res_extras(res):
        return lambda tm, tn: [(res, pl.BlockSpec((tm, tn), lambda i, j, k: (i, j)))]

    def res_epi(acc, res):
        return (res + acc,)

    h1 = _mm_nn("out_proj", merged, w_o, f32, extras_fn=res_extras(x), epilogue=res_epi)[0]
    xn2, r2 = _rms_fwd("rms_ffn", h1, norm_ffn)
    hh = _mm_nn("ffn_up", xn2, w_up, bf16, tn=1408)[0]
    fact = _ffn_act(cfg, hh, ffn_conv_w, ffn_conv_b)
    h2 = _mm_nn("ffn_down", fact, w_down, f32, tk=_tile(F, 1408, 128), extras_fn=res_extras(h1), epilogue=res_epi)[0]
    dh2, dh2b, g_norm_final, loss_tile = _loss_head("loss_head", h2, tgt, norm_final)

    df = _mm_nt("ffn_down_dx", [dh2b], w_down, [bf16], tn=1408)[0]
    g_w_down = _mm_tn("ffn_down_dw", fact, [dh2b], tm=1408)
    da_pre, dgate, g_ffn_conv_w, g_ffn_conv_b = _ffn_act_bwd(cfg, hh, df, ffn_conv_w, ffn_conv_b)
    dxn2 = _mm_nt("ffn_up_dx", [da_pre, dgate], w_up, [bf16], tk=1408)[0]
    g_w_up = _mm_tn("ffn_up_dw", xn2, [da_pre, dgate], shards=_NCHIP, tn=1408)
    dh1, dh1b, g_norm_ffn = _rms_bwd("rms_ffn_bwd", dxn2, h1, r2, norm_ffn, dh2)

    def merge_extras(tm, tn):
        off_a = (SW + 3 * CW) // tn
        off_b = (SW + 3 * CW + D) // tn
        blk = pl.BlockSpec((tm, tn), lambda i, j, k: (i, j))
        return [(ya, blk), (yb, blk), (proj, pl.BlockSpec((tm, tn), lambda i, j, k: (i, off_a + j))),
                (proj, pl.BlockSpec((tm, tn), lambda i, j, k: (i, off_b + j)))]

    def merge_epi(acc, ya_t, yb_t, ma_t, mb_t):
        sa = jax.nn.sigmoid(ma_t.astype(f32))
        sb = jax.nn.sigmoid(mb_t.astype(f32))
        return (acc * sa, acc * sb, acc * ya_t.astype(f32) * sa * (1.0 - sa), acc * yb_t.astype(f32) * sb * (1.0 - sb))

    dya, dyb, dma, dmb = _mm_nt("out_proj_dx", [dh1b], w_o, [bf16] * 4, extras_fn=merge_extras, epilogue=merge_epi)
    g_w_o = _mm_tn("out_proj_dw", merged, [dh1b])

    def glu_bwd_extras(tm, tn):
        blk = pl.BlockSpec((tm, tn), lambda i, j, k: (i, j))
        return [(ya0, blk), (z, blk)]

    def glu_bwd_epi(acc, a0, zz):
        s = jax.nn.sigmoid(zz.astype(f32))
        return (acc * a0.astype(f32) * s * (1.0 - s), acc * s)

    dz, t1 = _mm_nt("ssm_out_dx", [dya], w_so, [bf16, bf16], extras_fn=glu_bwd_extras, epilogue=glu_bwd_epi)
    g_w_so = _mm_tn("ssm_out_dw", ya1, [dya], shards=_NCHIP)
    dyb0 = _mm_nt("conv_out_dx", [dyb], w_co, [bf16])[0]
    g_w_co = _mm_tn("conv_out_dw", yb0, [dyb], shards=_NCHIP)
    dvbc, g_conv_w, g_conv_b = _convb_bwd(cfg, proj, dyb0, conv_w, conv_b)

    def gelu_bwd_extras(tm, tn):
        blk = pl.BlockSpec((tm, tn), lambda i, j, k: (i, j))
        return [(t1, blk), (y_s, blk)]

    def gelu_bwd_epi(acc, tt, yy):
        return ((tt.astype(f32) + acc) * _gelu_grad(yy.astype(f32)),)

    dy_s = _mm_nt("glu_dx", [dz], w_glu, [bf16], extras_fn=gelu_bwd_extras, epilogue=gelu_bwd_epi)[0]
    g_w_glu = _mm_tn("glu_dw", ya0, [dz])
    du, da_acc, db_full, dc_full, g_dskip = _s5_bwd(cfg, proj, dy_s, cin, tabs, dskip)
    dproj = [du, dvbc, dma, dmb]
    dxn1 = _mm_nt("in_proj_dx", dproj, w_in, [bf16], tk=1024)[0]
    g_w_in = _mm_tn("in_proj_dw", xn1, dproj, shards=_NCHIP)
    dx, _, g_norm_tok = _rms_bwd("rms_tok_bwd", dxn1, x, r1, norm_tok, dh1)

    dabr, dabi, dbbr, dbbi, g_c_re, g_c_im = _s5_param_grads(cfg, da_acc, db_full, dc_full)
    g_a_re, g_a_im, g_log_dt, g_b_re, g_b_im = disc_vjp((dabr, dabi, dbbr, dbbi))

    big_g = {"w_in": g_w_in, "w_glu": g_w_glu.reshape(_NCHIP, SW // _NCHIP, SW), "w_ssm_out": g_w_so,
             "w_conv_out": g_w_co, "w_o": g_w_o.reshape(_NCHIP, D // _NCHIP, D), "w_up": g_w_up,
             "w_down": g_w_down.reshape(_NCHIP, F // _NCHIP, D)}
    glist = [big_g[n] for n in _BIG]
    from_sibling = _exchange_halves(glist)
    chip_parts = [_add_own_half("grad_add_halves_" + n, g, t, c_idx) for n, g, t in zip(_BIG, glist, from_sibling)]
    from_chips = _scatter_shards(chip_parts)
    kc_idx = jnp.concatenate([k_idx, c_idx])
    halves = [_add_shard_parts("grad_add_chips_" + n, g, t, r, kc_idx)
              for n, g, t, r in zip(_BIG, glist, from_sibling, from_chips)]
    reduced = dict(zip(_BIG, _join_halves(halves)))

    small_g = {"norm_tok": g_norm_tok, "a_re": g_a_re, "a_im": g_a_im, "log_dt": g_log_dt, "b_re": g_b_re, "b_im": g_b_im,
               "c_re": g_c_re, "c_im": g_c_im, "d_skip": g_dskip, "conv_w": g_conv_w, "conv_b": g_conv_b,
               "norm_ffn": g_norm_ffn, "ffn_conv_w": g_ffn_conv_w, "ffn_conv_b": g_ffn_conv_b, "norm_final": g_norm_final}
    small_shapes = [small_g[n].shape for n in _SMALL]
    summed = dict(zip(_SMALL, _unpack(_allreduce8("allreduce_small_grads", _pack([small_g[n] for n in _SMALL])), small_shapes)))
    summed["conv_w"] = lax.dynamic_slice(summed["conv_w"], (0, kk * (CW // _NCHIP)), (3, CW // _NCHIP))
    summed["ffn_conv_w"] = lax.dynamic_slice(summed["ffn_conv_w"], (0, kk * (F // _NCHIP)), (3, F // _NCHIP))

    grads, deltas, new_m, new_v = {}, {}, {}, {}
    for n in _BIG:
        d_, m_, v_ = _adamw("adamw_" + n, big2d[n], reduced[n], m[n].reshape(big2d[n].shape), v[n].reshape(big2d[n].shape))
        grads[n] = reduced[n].reshape(p[n].shape)
        deltas[n], new_m[n], new_v[n] = d_.reshape(p[n].shape), m_.reshape(p[n].shape), v_.reshape(p[n].shape)
    shapes = [p[n].shape for n in _SMALL]
    sg = _pack([summed[n] for n in _SMALL])
    d_, m_, v_ = _adamw("adamw_small", _pack([p[n] for n in _SMALL]), sg, _pack([m[n] for n in _SMALL]),
                        _pack([v[n] for n in _SMALL]))
    for n, dd, mm, vv in zip(_SMALL, _unpack(d_, shapes), _unpack(m_, shapes), _unpack(v_, shapes)):
        grads[n] = summed[n].reshape(p[n].shape)
        deltas[n], new_m[n], new_v[n] = dd, mm, vv

    loss = lax.psum(loss_tile[0, 0], ("x", "y", "c"))
    return (loss, dx.reshape(1, L, D), *[grads[n] for n in _WEIGHTS], *[deltas[n] for n in _WEIGHTS],
            *[new_m[n] for n in _WEIGHTS], *[new_v[n] for n in _WEIGHTS])


def kernel(x, norm_tok, w_in, a_re, a_im, log_dt, b_re, b_im, c_re, c_im, d_skip, w_glu, w_ssm_out, conv_w, conv_b, w_conv_out, w_o, norm_ffn, w_up, ffn_conv_w, ffn_conv_b, w_down, norm_final, loss_target, m_norm_tok, m_w_in, m_a_re, m_a_im, m_log_dt, m_b_re, m_b_im, m_c_re, m_c_im, m_d_skip, m_w_glu, m_w_ssm_out, m_conv_w, m_conv_b, m_w_conv_out, m_w_o, m_norm_ffn, m_w_up, m_ffn_conv_w, m_ffn_conv_b, m_w_down, m_norm_final, v_norm_tok, v_w_in, v_a_re, v_a_im, v_log_dt, v_b_re, v_b_im, v_c_re, v_c_im, v_d_skip, v_w_glu, v_w_ssm_out, v_conv_w, v_conv_b, v_w_conv_out, v_w_o, v_norm_ffn, v_w_up, v_ffn_conv_w, v_ffn_conv_b, v_w_down, v_norm_final):
    p = dict(norm_tok=norm_tok, w_in=w_in, a_re=a_re, a_im=a_im, log_dt=log_dt, b_re=b_re, b_im=b_im, c_re=c_re,
             c_im=c_im, d_skip=d_skip, w_glu=w_glu, w_ssm_out=w_ssm_out, conv_w=conv_w, conv_b=conv_b,
             w_conv_out=w_conv_out, w_o=w_o, norm_ffn=norm_ffn, w_up=w_up, ffn_conv_w=ffn_conv_w,
             ffn_conv_b=ffn_conv_b, w_down=w_down, norm_final=norm_final)
    m = dict(norm_tok=m_norm_tok, w_in=m_w_in, a_re=m_a_re, a_im=m_a_im, log_dt=m_log_dt, b_re=m_b_re, b_im=m_b_im,
             c_re=m_c_re, c_im=m_c_im, d_skip=m_d_skip, w_glu=m_w_glu, w_ssm_out=m_w_ssm_out, conv_w=m_conv_w,
             conv_b=m_conv_b, w_conv_out=m_w_conv_out, w_o=m_w_o, norm_ffn=m_norm_ffn, w_up=m_w_up,
             ffn_conv_w=m_ffn_conv_w, ffn_conv_b=m_ffn_conv_b, w_down=m_w_down, norm_final=m_norm_final)
    v = dict(norm_tok=v_norm_tok, w_in=v_w_in, a_re=v_a_re, a_im=v_a_im, log_dt=v_log_dt, b_re=v_b_re, b_im=v_b_im,
             c_re=v_c_re, c_im=v_c_im, d_skip=v_d_skip, w_glu=v_w_glu, w_ssm_out=v_w_ssm_out, conv_w=v_conv_w,
             conv_b=v_conv_b, w_conv_out=v_w_conv_out, w_o=v_w_o, norm_ffn=v_norm_ffn, w_up=v_w_up,
             ffn_conv_w=v_ffn_conv_w, ffn_conv_b=v_ffn_conv_b, w_down=v_w_down, norm_final=v_norm_final)
    return _step(_Cfg(), x, loss_target, p, m, v)
```

```python
import functools
import math
from typing import NamedTuple

import jax
import jax.numpy as jnp
from jax import lax
from jax.experimental import pallas as pl
from jax.experimental.pallas import tpu as pltpu
from jax.experimental.pallas import tpu_sc as plsc

f32 = jnp.float32
bf16 = jnp.bfloat16
_MESH = pl.DeviceIdType.MESH

_EPS = 1e-6
_ADAM_LR = 0.001
_ADAM_B1 = 0.9
_ADAM_B2 = 0.999
_ADAM_EPS = 1e-08
_ADAM_WD = 0.01
_ADAM_STEP = 10
_SSM_GROUP = 16
_SSM_STATE = 64
_SLAB_GROUPS = 16
_NCHIP = 4
_VMEM_LIMIT = 56 * 2**20
_GELU_C = math.sqrt(2.0 / math.pi)
_GELU_A = 0.044715


class _Cfg(NamedTuple):
    L: int = 4096
    D: int = 2048
    SW: int = 1024
    CW: int = 1024
    F: int = 5632
    T: int = 256


def _tile(n, pref, align):
    t = min(n, pref)
    t -= t % align
    while t > align and n % t:
        t -= align
    assert t > 0 and n % t == 0, (n, pref, align)
    return t


def _cparams(sem):
    return pltpu.CompilerParams(dimension_semantics=sem, vmem_limit_bytes=_VMEM_LIMIT)


def _gelu(x):
    return 0.5 * x * (1.0 + jnp.tanh(_GELU_C * (x + _GELU_A * x * x * x)))


def _gelu_grad(x):
    th = jnp.tanh(_GELU_C * (x + _GELU_A * x * x * x))
    return 0.5 * (1.0 + th) + 0.5 * x * (1.0 - th * th) * _GELU_C * (1.0 + 3.0 * _GELU_A * x * x)


def _matmul(name, a_pieces, b_pieces, *, grid, contract, outs, acc_shape, extras=(), epilogue=None,
            a_sel=None, b_sel=None):
    na, nb, ne = len(a_pieces), len(b_pieces), len(extras)
    nk = grid[2]

    def body(*refs):
        a_refs = refs[:na]
        b_refs = refs[na:na + nb]
        e_refs = refs[na + nb:na + nb + ne]
        o_refs = refs[na + nb + ne:-1]
        acc = refs[-1]
        ids = [pl.program_id(0), pl.program_id(1), pl.program_id(2)]
        k = ids[2]

        @pl.when(k == 0)
        def _():
            acc[...] = jnp.zeros(acc_shape, f32)

        for ai, (_, _, alo, ahi) in enumerate(a_pieces):
            for bi, (_, _, blo, bhi) in enumerate(b_pieces):
                conds = []
                if na > 1:
                    conds.append((ids[a_sel] >= alo) & (ids[a_sel] < ahi))
                if nb > 1:
                    conds.append((ids[b_sel] >= blo) & (ids[b_sel] < bhi))

                def upd(ai=ai, bi=bi):
                    acc[...] += lax.dot_general(a_refs[ai][...], b_refs[bi][...], contract,
                                                preferred_element_type=f32)

                if conds:
                    pl.when(functools.reduce(lambda p, q: p & q, conds))(upd)
                else:
                    upd()

        @pl.when(k == nk - 1)
        def _():
            if epilogue is None:
                res = (acc[...],)
            else:
                res = epilogue(acc[...], *[r[...] for r in e_refs])
            for o, r in zip(o_refs, res):
                o[...] = r.astype(o.dtype)

    in_specs = [p[1] for p in a_pieces] + [p[1] for p in b_pieces] + [e[1] for e in extras]
    args = [p[0] for p in a_pieces] + [p[0] for p in b_pieces] + [e[0] for e in extras]
    res = pl.pallas_call(
        body, grid=grid, in_specs=in_specs, out_specs=[o[1] for o in outs], out_shape=[o[0] for o in outs],
        scratch_shapes=[pltpu.VMEM(acc_shape, f32)], name=name,
        compiler_params=_cparams(("parallel", "parallel", "arbitrary")))(*args)
    return res


def _clipmap(v, lo, hi):
    return jnp.clip(v - lo, 0, hi - lo - 1)


def _mm_nn(name, a, w, out_dtype, *, tm=1024, tn=1024, tk=512, extras_fn=None, epilogue=None, out_dtypes=None):
    M, K = a.shape
    sharded = w.ndim == 3
    Ns = w.shape[-1]
    N = Ns * (w.shape[0] if sharded else 1)
    tm, tn, tk = _tile(M, tm, 16), _tile(Ns, tn, 128), _tile(K, tk, 128)
    nb = Ns // tn
    grid = (M // tm, N // tn, K // tk)
    a_spec = pl.BlockSpec((tm, tk), lambda i, j, k: (i, k))
    if sharded:
        b_spec = pl.BlockSpec((None, tk, tn), lambda i, j, k: (j // nb, k, j % nb))
    else:
        b_spec = pl.BlockSpec((tk, tn), lambda i, j, k: (k, j))
    o_spec = pl.BlockSpec((tm, tn), lambda i, j, k: (i, j))
    dts = out_dtypes if out_dtypes is not None else [out_dtype]
    outs = [(jax.ShapeDtypeStruct((M, N), dt), o_spec) for dt in dts]
    extras = extras_fn(tm, tn) if extras_fn is not None else ()
    return _matmul(name, [(a, a_spec, 0, 0)], [(w, b_spec, 0, 0)], grid=grid, contract=(((1,), (0,)), ((), ())),
                   outs=outs, acc_shape=(tm, tn), extras=extras, epilogue=epilogue)


def _mm_nt(name, a_list, w, out_dtypes, *, tm=1024, tn=1024, tk=512, extras_fn=None, epilogue=None):
    M = a_list[0].shape[0]
    sharded = w.ndim == 3
    Ns = w.shape[-1]
    K = w.shape[-2]
    N = Ns * (w.shape[0] if sharded else 1)
    assert sum(a.shape[1] for a in a_list) == N
    tk = _tile(Ns, tk, 128)
    for a in a_list:
        tk = math.gcd(tk, a.shape[1])
    assert tk % 128 == 0
    tm, tn = _tile(M, tm, 16), _tile(K, tn, 128)
    nb = Ns // tk
    grid = (M // tm, K // tn, N // tk)
    pieces, lo = [], 0
    for a in a_list:
        hi = lo + a.shape[1] // tk
        pieces.append((a, pl.BlockSpec((tm, tk), lambda i, j, k, lo=lo, hi=hi: (i, _clipmap(k, lo, hi))), lo, hi))
        lo = hi
    if sharded:
        b_spec = pl.BlockSpec((None, tn, tk), lambda i, j, k: (k // nb, j, k % nb))
    else:
        b_spec = pl.BlockSpec((tn, tk), lambda i, j, k: (j, k))
    o_spec = pl.BlockSpec((tm, tn), lambda i, j, k: (i, j))
    outs = [(jax.ShapeDtypeStruct((M, K), dt), o_spec) for dt in out_dtypes]
    extras = extras_fn(tm, tn) if extras_fn is not None else ()
    return _matmul(name, pieces, [(w, b_spec, 0, 0)], grid=grid, contract=(((1,), (1,)), ((), ())),
                   outs=outs, acc_shape=(tm, tn), extras=extras, epilogue=epilogue, a_sel=2)


def _mm_tn(name, a, b_list, *, shards=None, tm=1024, tn=1024, tk=512):
    M, K = a.shape
    N = sum(b.shape[1] for b in b_list)
    Ns = N // shards if shards else N
    tn = _tile(Ns, tn, 128)
    for b in b_list:
        tn = math.gcd(tn, b.shape[1])
    assert tn % 128 == 0
    tm, tk = _tile(K, tm, 128), _tile(M, tk, 16)
    nb = Ns // tn
    grid = (K // tm, N // tn, M // tk)
    a_spec = pl.BlockSpec((tk, tm), lambda i, j, k: (k, i))
    pieces, lo = [], 0
    for b in b_list:
        hi = lo + b.shape[1] // tn
        pieces.append((b, pl.BlockSpec((tk, tn), lambda i, j, k, lo=lo, hi=hi: (k, _clipmap(j, lo, hi))), lo, hi))
        lo = hi
    if shards:
        out = (jax.ShapeDtypeStruct((shards, K, Ns), f32), pl.BlockSpec((None, tm, tn), lambda i, j, k: (j // nb, i, j % nb)))
    else:
        out = (jax.ShapeDtypeStruct((K, N), f32), pl.BlockSpec((tm, tn), lambda i, j, k: (i, j)))
    return _matmul(name, [(a, a_spec, 0, 0)], pieces, grid=grid, contract=(((0,), (0,)), ((), ())),
                   outs=[out], acc_shape=(tm, tn), b_sel=1)[0]


def _rms_fwd(name, x, g):
    L, D = x.shape
    tm = _tile(L, 256, 16)

    def body(x_ref, g_ref, xn_ref, r_ref):
        xv = x_ref[...]
        r = lax.rsqrt(jnp.mean(xv * xv, axis=-1, keepdims=True) + _EPS)
        xn_ref[...] = (xv * r * g_ref[...]).astype(bf16)
        r_ref[...] = r

    return pl.pallas_call(
        body, grid=(L // tm,),
        in_specs=[pl.BlockSpec((tm, D), lambda i: (i, 0)), pl.BlockSpec((1, D), lambda i: (0, 0))],
        out_specs=[pl.BlockSpec((tm, D), lambda i: (i, 0)), pl.BlockSpec((tm, 1), lambda i: (i, 0))],
        out_shape=[jax.ShapeDtypeStruct((L, D), bf16), jax.ShapeDtypeStruct((L, 1), f32)],
        name=name, compiler_params=_cparams(("parallel",)))(x, g)


def _rms_bwd(name, dxn, h, r, g, dres):
    L, D = h.shape
    tm = _tile(L, 256, 16)

    def body(dxn_ref, h_ref, r_ref, g_ref, dres_ref, dh_ref, dhb_ref, dg_ref):
        i = pl.program_id(0)
        d = dxn_ref[...].astype(f32)
        hv = h_ref[...]
        rv = r_ref[...]
        dyg = d * g_ref[...]
        m = jnp.mean(dyg * hv, axis=-1, keepdims=True)
        dh = dres_ref[...] + rv * dyg - hv * (rv * rv * rv) * m
        dh_ref[...] = dh
        dhb_ref[...] = dh.astype(bf16)

        @pl.when(i == 0)
        def _():
            dg_ref[...] = jnp.zeros_like(dg_ref)

        dg_ref[...] += jnp.sum(d * hv * rv, axis=0, keepdims=True)

    row = lambda i: (i, 0)
    return pl.pallas_call(
        body, grid=(L // tm,),
        in_specs=[pl.BlockSpec((tm, D), row), pl.BlockSpec((tm, D), row), pl.BlockSpec((tm, 1), row),
                  pl.BlockSpec((1, D), lambda i: (0, 0)), pl.BlockSpec((tm, D), row)],
        out_specs=[pl.BlockSpec((tm, D), row), pl.BlockSpec((tm, D), row), pl.BlockSpec((1, D), lambda i: (0, 0))],
        out_shape=[jax.ShapeDtypeStruct((L, D), f32), jax.ShapeDtypeStruct((L, D), bf16), jax.ShapeDtypeStruct((1, D), f32)],
        name=name, compiler_params=_cparams(("arbitrary",)))(dxn, h, r, g, dres)


def _loss_head(name, h2, tgt, g):
    L, D = h2.shape
    tm = _tile(L, 256, 16)

    def body(h_ref, t_ref, g_ref, dh_ref, dhb_ref, dg_ref, loss_ref):
        i = pl.program_id(0)
        hv = h_ref[...]
        gv = g_ref[...]
        r = lax.rsqrt(jnp.mean(hv * hv, axis=-1, keepdims=True) + _EPS)
        err = hv * r * gv - t_ref[...]
        dy = err * (1.0 / D)
        dyg = dy * gv
        m = jnp.mean(dyg * hv, axis=-1, keepdims=True)
        dh = r * dyg - hv * (r * r * r) * m
        dh_ref[...] = dh
        dhb_ref[...] = dh.astype(bf16)

        @pl.when(i == 0)
        def _():
            dg_ref[...] = jnp.zeros_like(dg_ref)
            loss_ref[...] = jnp.zeros_like(loss_ref)

        dg_ref[...] += jnp.sum(dy * hv * r, axis=0, keepdims=True)
        part = jnp.sum(jnp.sum(err * err, axis=-1, keepdims=True), axis=0, keepdims=True) * (0.5 / D)
        loss_ref[...] += jnp.broadcast_to(part, (8, 128))

    row = lambda i: (i, 0)
    return pl.pallas_call(
        body, grid=(L // tm,),
        in_specs=[pl.BlockSpec((tm, D), row), pl.BlockSpec((tm, D), row), pl.BlockSpec((1, D), lambda i: (0, 0))],
        out_specs=[pl.BlockSpec((tm, D), row), pl.BlockSpec((tm, D), row), pl.BlockSpec((1, D), lambda i: (0, 0)),
                   pl.BlockSpec((8, 128), lambda i: (0, 0))],
        out_shape=[jax.ShapeDtypeStruct((L, D), f32), jax.ShapeDtypeStruct((L, D), bf16),
                   jax.ShapeDtypeStruct((1, D), f32), jax.ShapeDtypeStruct((8, 128), f32)],
        name=name, compiler_params=_cparams(("arbitrary",)))(h2, tgt, g)


def _shift_down(tile, halo, k, rows8):
    tm = tile.shape[0]
    r = pltpu.roll(tile, k, axis=0)
    hh = pltpu.roll(halo, k, axis=0)
    top = jnp.where(rows8 < k, hh, r[:8])
    return jnp.concatenate([top, r[8:]], axis=0) if tm > 8 else top


def _shift_up(tile, halo, k, rows8):
    tm = tile.shape[0]
    r = pltpu.roll(tile, tm - k, axis=0)
    hh = pltpu.roll(halo, 8 - k, axis=0)
    bot = jnp.where(rows8 >= 8 - k, hh, r[tm - 8:])
    return jnp.concatenate([r[:tm - 8], bot], axis=0) if tm > 8 else bot


def _conv3(x, halo, w_ref, b_ref, rows8):
    return (w_ref[0:1, :] * _shift_down(x, halo, 2, rows8) + w_ref[1:2, :] * _shift_down(x, halo, 1, rows8)
            + w_ref[2:3, :] * x + b_ref[...])


def _convb_fwd(cfg, proj, w, b):
    L, CW = cfg.L, cfg.CW
    assert cfg.SW == CW
    tm = _tile(L, 512, 16)

    def body(v_ref, vh_ref, gb_ref, gc_ref, gch_ref, w_ref, b_ref, o_ref):
        i = pl.program_id(0)
        rows8 = lax.broadcasted_iota(jnp.int32, (8, CW), 0)
        cv = gc_ref[...].astype(f32) * v_ref[...].astype(f32)
        cvh = gch_ref[...].astype(f32)[8:] * vh_ref[...].astype(f32)[8:]
        cvh = jnp.where(i == 0, 0.0, cvh)
        cc = _conv3(cv, cvh, w_ref, b_ref, rows8)
        o_ref[...] = (gb_ref[...].astype(f32) * cc).astype(bf16)

    blk = lambda col: pl.BlockSpec((tm, CW), lambda i: (i, col))
    halo = lambda col: pl.BlockSpec((16, CW), lambda i: (jnp.maximum(i * (tm // 16) - 1, 0), col))
    return pl.pallas_call(
        body, grid=(L // tm,),
        in_specs=[blk(1), halo(1), blk(2), blk(3), halo(3),
                  pl.BlockSpec((3, CW), lambda i: (0, 0)), pl.BlockSpec((1, CW), lambda i: (0, 0))],
        out_specs=pl.BlockSpec((tm, CW), lambda i: (i, 0)),
        out_shape=jax.ShapeDtypeStruct((L, CW), bf16),
        name="convb_fwd", compiler_params=_cparams(("parallel",)))(proj, proj, proj, proj, proj, w, b)


def _convb_bwd(cfg, proj, dyb0, w, b):
    L, CW = cfg.L, cfg.CW
    tm = _tile(L, 512, 16)
    nt = L // tm

    def body(v_ref, vh_ref, gb_ref, gbn_ref, gc_ref, gch_ref, d_ref, dn_ref, w_ref, b_ref, o_ref, dw_ref, db_ref):
        i = pl.program_id(0)
        rows8 = lax.broadcasted_iota(jnp.int32, (8, CW), 0)
        v = v_ref[...].astype(f32)
        gb = gb_ref[...].astype(f32)
        gc = gc_ref[...].astype(f32)
        d = d_ref[...].astype(f32)
        cv = gc * v
        cvh = gch_ref[...].astype(f32)[8:] * vh_ref[...].astype(f32)[8:]
        cvh = jnp.where(i == 0, 0.0, cvh)
        s2 = _shift_down(cv, cvh, 2, rows8)
        s1 = _shift_down(cv, cvh, 1, rows8)
        cc = w_ref[0:1, :] * s2 + w_ref[1:2, :] * s1 + w_ref[2:3, :] * cv + b_ref[...]
        dcc = d * gb
        dccn = dn_ref[...].astype(f32)[:8] * gbn_ref[...].astype(f32)[:8]
        dccn = jnp.where(i == nt - 1, 0.0, dccn)
        dcv = (w_ref[2:3, :] * dcc + w_ref[1:2, :] * _shift_up(dcc, dccn, 1, rows8)
               + w_ref[0:1, :] * _shift_up(dcc, dccn, 2, rows8))
        o_ref[:, 0:CW] = (dcv * gc).astype(bf16)
        o_ref[:, CW:2 * CW] = (d * cc).astype(bf16)
        o_ref[:, 2 * CW:3 * CW] = (dcv * v).astype(bf16)

        @pl.when(i == 0)
        def _():
            dw_ref[...] = jnp.zeros_like(dw_ref)
            db_ref[...] = jnp.zeros_like(db_ref)

        dw_ref[0:1, :] += jnp.sum(dcc * s2, axis=0, keepdims=True)
        dw_ref[1:2, :] += jnp.sum(dcc * s1, axis=0, keepdims=True)
        dw_ref[2:3, :] += jnp.sum(dcc * cv, axis=0, keepdims=True)
        db_ref[...] += jnp.sum(dcc, axis=0, keepdims=True)

    blk = lambda col: pl.BlockSpec((tm, CW), lambda i: (i, col))
    prev = lambda col: pl.BlockSpec((16, CW), lambda i: (jnp.maximum(i * (tm // 16) - 1, 0), col))
    nxt = lambda col: pl.BlockSpec((16, CW), lambda i: (jnp.minimum((i + 1) * (tm // 16), L // 16 - 1), col))
    const = lambda r: pl.BlockSpec((r, CW), lambda i: (0, 0))
    return pl.pallas_call(
        body, grid=(nt,),
        in_specs=[blk(1), prev(1), blk(2), nxt(2), blk(3), prev(3), blk(0), nxt(0), const(3), const(1)],
        out_specs=[pl.BlockSpec((tm, 3 * CW), lambda i: (i, 0)), const(3), const(1)],
        out_shape=[jax.ShapeDtypeStruct((L, 3 * CW), bf16), jax.ShapeDtypeStruct((3, CW), f32),
                   jax.ShapeDtypeStruct((1, CW), f32)],
        name="convb_bwd", compiler_params=_cparams(("arbitrary",)))(proj, proj, proj, proj, proj, proj, dyb0, dyb0, w, b)


def _ffn_act(cfg, hh, w, b):
    L, F = cfg.L, cfg.F
    tm = _tile(L, 512, 16)
    tc = _tile(F, 1408, 128)
    ncb = F // tc

    def body(a_ref, ah_ref, g_ref, w_ref, b_ref, o_ref):
        i = pl.program_id(0)
        rows8 = lax.broadcasted_iota(jnp.int32, (8, tc), 0)
        a = a_ref[...].astype(f32)
        ah = jnp.where(i == 0, 0.0, ah_ref[...].astype(f32)[8:])
        o_ref[...] = (_gelu(_conv3(a, ah, w_ref, b_ref, rows8)) * g_ref[...].astype(f32)).astype(bf16)

    return pl.pallas_call(
        body, grid=(L // tm, ncb),
        in_specs=[pl.BlockSpec((tm, tc), lambda i, j: (i, j)),
                  pl.BlockSpec((16, tc), lambda i, j: (jnp.maximum(i * (tm // 16) - 1, 0), j)),
                  pl.BlockSpec((tm, tc), lambda i, j: (i, j + ncb)),
                  pl.BlockSpec((3, tc), lambda i, j: (0, j)), pl.BlockSpec((1, tc), lambda i, j: (0, j))],
        out_specs=pl.BlockSpec((tm, tc), lambda i, j: (i, j)),
        out_shape=jax.ShapeDtypeStruct((L, F), bf16),
        name="ffn_act", compiler_params=_cparams(("parallel", "parallel")))(hh, hh, hh, w, b)


def _ffn_act_bwd(cfg, hh, df, w, b):
    L, F = cfg.L, cfg.F
    tm = _tile(L, 512, 16)
    tc = _tile(F, 1408, 128)
    ncb = F // tc
    nt = L // tm

    def body(a_ref, ah_ref, an_ref, g_ref, gn_ref, d_ref, dn_ref, w_ref, b_ref, da_ref, dg_ref, dw_ref, db_ref):
        i = pl.program_id(1)
        rows8 = lax.broadcasted_iota(jnp.int32, (8, tc), 0)
        a = a_ref[...].astype(f32)
        ah = jnp.where(i == 0, 0.0, ah_ref[...].astype(f32)[8:])
        s2 = _shift_down(a, ah, 2, rows8)
        s1 = _shift_down(a, ah, 1, rows8)
        act = w_ref[0:1, :] * s2 + w_ref[1:2, :] * s1 + w_ref[2:3, :] * a + b_ref[...]
        d = d_ref[...].astype(f32)
        dg_ref[...] = (d * _gelu(act)).astype(bf16)
        dact = d * g_ref[...].astype(f32) * _gelu_grad(act)
        an = an_ref[...].astype(f32)[:8]
        actn = _conv3(an, a[tm - 8:], w_ref, b_ref, rows8)
        dactn = dn_ref[...].astype(f32)[:8] * gn_ref[...].astype(f32)[:8] * _gelu_grad(actn)
        dactn = jnp.where(i == nt - 1, 0.0, dactn)
        da = (w_ref[2:3, :] * dact + w_ref[1:2, :] * _shift_up(dact, dactn, 1, rows8)
              + w_ref[0:1, :] * _shift_up(dact, dactn, 2, rows8))
        da_ref[...] = da.astype(bf16)

        @pl.when(i == 0)
        def _():
            dw_ref[...] = jnp.zeros_like(dw_ref)
            db_ref[...] = jnp.zeros_like(db_ref)

        dw_ref[0:1, :] += jnp.sum(dact * s2, axis=0, keepdims=True)
        dw_ref[1:2, :] += jnp.sum(dact * s1, axis=0, keepdims=True)
        dw_ref[2:3, :] += jnp.sum(dact * a, axis=0, keepdims=True)
        db_ref[...] += jnp.sum(dact, axis=0, keepdims=True)

    blk = lambda off: pl.BlockSpec((tm, tc), lambda j, i: (i, j + off))
    prev = lambda off: pl.BlockSpec((16, tc), lambda j, i: (jnp.maximum(i * (tm // 16) - 1, 0), j + off))
    nxt = lambda off: pl.BlockSpec((16, tc), lambda j, i: (jnp.minimum((i + 1) * (tm // 16), L // 16 - 1), j + off))
    const = lambda r: pl.BlockSpec((r, tc), lambda j, i: (0, j))
    return pl.pallas_call(
        body, grid=(ncb, nt),
        in_specs=[blk(0), prev(0), nxt(0), blk(ncb), nxt(ncb), blk(0), nxt(0), const(3), const(1)],
        out_specs=[blk(0), blk(0), const(3), const(1)],
        out_shape=[jax.ShapeDtypeStruct((L, F), bf16), jax.ShapeDtypeStruct((L, F), bf16),
                   jax.ShapeDtypeStruct((3, F), f32), jax.ShapeDtypeStruct((1, F), f32)],
        name="ffn_act_bwd", compiler_params=_cparams(("parallel", "arbitrary")))(hh, hh, hh, hh, hh, df, df, w, b)


def _merge_fwd(cfg, ya1, yb0, proj, wso, wco):
    L, D, SW, CW = cfg.L, cfg.D, cfg.SW, cfg.CW
    Ns = D // _NCHIP
    tm = _tile(L, 1024, 16)
    tn = _tile(Ns, 512, 128)
    nb = Ns // tn
    off_a = (SW + 3 * CW) // tn
    off_b = (SW + 3 * CW + D) // tn

    def body(a_ref, b_ref, wa_ref, wb_ref, ma_ref, mb_ref, m_ref, ya_ref, yb_ref):
        ya = jnp.dot(a_ref[...], wa_ref[...], preferred_element_type=f32)
        yb = jnp.dot(b_ref[...], wb_ref[...], preferred_element_type=f32)
        sa = jax.nn.sigmoid(ma_ref[...].astype(f32))
        sb = jax.nn.sigmoid(mb_ref[...].astype(f32))
        m_ref[...] = (sa * ya + sb * yb).astype(bf16)
        ya_ref[...] = ya.astype(bf16)
        yb_ref[...] = yb.astype(bf16)

    o_spec = pl.BlockSpec((tm, tn), lambda i, j: (i, j))
    o_shape = jax.ShapeDtypeStruct((L, D), bf16)
    return pl.pallas_call(
        body, grid=(L // tm, D // tn),
        in_specs=[pl.BlockSpec((tm, SW), lambda i, j: (i, 0)), pl.BlockSpec((tm, CW), lambda i, j: (i, 0)),
                  pl.BlockSpec((None, SW, tn), lambda i, j: (j // nb, 0, j % nb)),
                  pl.BlockSpec((None, CW, tn), lambda i, j: (j // nb, 0, j % nb)),
                  pl.BlockSpec((tm, tn), lambda i, j: (i, off_a + j)), pl.BlockSpec((tm, tn), lambda i, j: (i, off_b + j))],
        out_specs=[o_spec, o_spec, o_spec], out_shape=[o_shape, o_shape, o_shape],
        name="merge_fwd", compiler_params=_cparams(("parallel", "parallel")))(ya1, yb0, wso, wco, proj, proj)


def _s5_dims(cfg):
    G = cfg.SW // _SSM_GROUP
    NS = G // _SLAB_GROUPS
    SC = _SLAB_GROUPS * _SSM_GROUP
    SH = _SLAB_GROUPS * _SSM_STATE
    NST = 2 * SH * NS
    return G, NS, SC, SH, NST


def _lane_slabs(cfg, W):
    _, NS, _, SH, _ = _s5_dims(cfg)
    return [(2 * SH * s + w0, 2 * SH * s + SH + w0) for s in range(NS) for w0 in range(0, SH, W)]


def _discretize(a_re, a_im, log_dt, b_re, b_im):
    dt = jnp.exp(log_dt)[:, None]
    mag = jnp.exp(dt * a_re)
    abr = mag * jnp.cos(dt * a_im)
    abi = mag * jnp.sin(dt * a_im)
    nr = abr - 1.0
    ni = abi
    den = a_re * a_re + a_im * a_im
    fr = (nr * a_re + ni * a_im) / den
    fi = (ni * a_re - nr * a_im) / den
    bbr = fr[..., None] * b_re - fi[..., None] * b_im
    bbi = fr[..., None] * b_im + fi[..., None] * b_re
    return abr, abi, bbr, bbi


def _state_rows(cfg, re, im):
    _, NS, _, SH, _ = _s5_dims(cfg)
    return jnp.concatenate([re.reshape(NS, SH), im.reshape(NS, SH)], axis=1).reshape(-1)


def _s5_tables(cfg, abr, abi, bbr, bbi, c_re, c_im):
    G, NS, SC, SH, NST = _s5_dims(cfg)
    S = cfg.T // 8
    eye = jnp.eye(_SLAB_GROUPS, dtype=f32)
    bb = jnp.stack([bbr, bbi]).reshape(2, NS, _SLAB_GROUPS, _SSM_STATE, _SSM_GROUP)
    bs = jnp.einsum("rsgph,gq->sghrqp", bb, eye).reshape(NS, SC, 2 * SH).astype(bf16)
    cc = jnp.stack([c_re, -c_im]).reshape(2, NS, _SLAB_GROUPS, _SSM_GROUP, _SSM_STATE)
    cs = jnp.einsum("rsghp,gq->srqpgh", cc, eye).reshape(NS, 2 * SH, SC).astype(bf16)
    arep = jnp.broadcast_to(_state_rows(cfg, abr, abi)[None, :], (8, NST))
    pr, pi = abr, abi
    for _ in range(S - 1):
        pr, pi = pr * abr - pi * abi, pr * abi + pi * abr
    apow = jnp.broadcast_to(_state_rows(cfg, pr, pi)[None, :], (8, NST))
    t = jnp.arange(cfg.T)
    perm = (t % 8) * S + t // 8
    pm = jax.nn.one_hot(perm, cfg.T, dtype=bf16)
    return bs, cs, arep, apow, pm, pm.T


def _cmul_add(ar, ai, xr, xi, br, bi):
    return ar * xr - ai * xi + br, ar * xi + ai * xr + bi


def _s5_forward_chunk(cfg, W, upb, bs_ref, arep_ref, apow_ref, st, x0, cin_store):
    _, NS, SC, SH, _ = _s5_dims(cfg)
    S = cfg.T // 8
    for s in range(NS):
        st[:, 2 * SH * s:2 * SH * (s + 1)] = jnp.dot(upb[:, SC * s:SC * (s + 1)], bs_ref[s], preferred_element_type=f32)
    rows = lax.broadcasted_iota(jnp.int32, (8, W), 0)
    zero = jnp.zeros((8, W), f32)
    for rc, ic in _lane_slabs(cfg, W):
        ar = arep_ref[:, rc:rc + W]
        ai = arep_ref[:, ic:ic + W]

        def step(i, carry, rc=rc, ic=ic, ar=ar, ai=ai):
            xr, xi = carry
            r0 = pl.multiple_of(i * 8, 8)
            nr, ni = _cmul_add(ar, ai, xr, xi, st[pl.ds(r0, 8), rc:rc + W], st[pl.ds(r0, 8), ic:ic + W])
            st[pl.ds(r0, 8), rc:rc + W] = nr
            st[pl.ds(r0, 8), ic:ic + W] = ni
            return nr, ni

        er, ei = lax.fori_loop(0, S, step, (zero, zero))
        pr = apow_ref[:, rc:rc + W]
        pi = apow_ref[:, ic:ic + W]
        x0r = x0[:, rc:rc + W]
        x0i = x0[:, ic:ic + W]
        cr = jnp.where(rows == 0, x0r, 0.0)
        ci = jnp.where(rows == 0, x0i, 0.0)
        for _ in range(7):
            fr, fi = _cmul_add(pr, pi, cr, ci, er, ei)
            cr = jnp.where(rows == 0, x0r, pltpu.roll(fr, 1, axis=0))
            ci = jnp.where(rows == 0, x0i, pltpu.roll(fi, 1, axis=0))
        fr, fi = _cmul_add(pr, pi, cr, ci, er, ei)
        x0[:, rc:rc + W] = jnp.broadcast_to(fr[7:8, :], (8, W))
        x0[:, ic:ic + W] = jnp.broadcast_to(fi[7:8, :], (8, W))
        cin_store(rc, ic, cr, ci)

        def fix(i, carry, rc=rc, ic=ic, ar=ar, ai=ai):
            kr, ki = carry
            r0 = pl.multiple_of(i * 8, 8)
            nr, ni = ar * kr - ai * ki, ar * ki + ai * kr
            st[pl.ds(r0, 8), rc:rc + W] = st[pl.ds(r0, 8), rc:rc + W] + nr
            st[pl.ds(r0, 8), ic:ic + W] = st[pl.ds(r0, 8), ic:ic + W] + ni
            return nr, ni

        lax.fori_loop(0, S, fix, (cr, ci))


def _s5_fwd(cfg, proj, tabs, dskip):
    L, SW, T = cfg.L, cfg.SW, cfg.T
    G, NS, SC, SH, NST = _s5_dims(cfg)
    bs, cs, arep, apow, pm, pmt = tabs
    W = min(512, SH)
    NC = L // T

    def body(u_ref, pm_ref, pmt_ref, bs_ref, cs_ref, arep_ref, apow_ref, dskip_ref, y_ref, ya0_ref, cin_ref, st, x0):
        c = pl.program_id(0)

        @pl.when(c == 0)
        def _():
            x0[...] = jnp.zeros_like(x0)

        up = jnp.dot(pm_ref[...], u_ref[...], preferred_element_type=f32)
        upb = up.astype(bf16)

        def cin_store(rc, ic, cr, ci):
            cin_ref[0, :, rc:rc + W] = cr
            cin_ref[0, :, ic:ic + W] = ci

        _s5_forward_chunk(cfg, W, upb, bs_ref, arep_ref, apow_ref, st, x0, cin_store)
        yp = jnp.concatenate(
            [jnp.dot(st[:, 2 * SH * s:2 * SH * (s + 1)].astype(bf16), cs_ref[s], preferred_element_type=f32)
             for s in range(NS)], axis=1) + dskip_ref[...] * up
        y = jnp.dot(pmt_ref[...], yp.astype(bf16), preferred_element_type=f32)
        y_ref[...] = y.astype(bf16)
        ya0_ref[...] = _gelu(y).astype(bf16)

    const2 = lambda shape: pl.BlockSpec(shape, lambda c: (0, 0))
    const3 = lambda shape: pl.BlockSpec(shape, lambda c: (0, 0, 0))
    return pl.pallas_call(
        body, grid=(NC,),
        in_specs=[pl.BlockSpec((T, SW), lambda c: (c, 0)), const2((T, T)), const2((T, T)), const3((NS, SC, 2 * SH)),
                  const3((NS, 2 * SH, SC)), const2((8, NST)), const2((8, NST)), const2((1, SW))],
        out_specs=[pl.BlockSpec((T, SW), lambda c: (c, 0)), pl.BlockSpec((T, SW), lambda c: (c, 0)),
                   pl.BlockSpec((1, 8, NST), lambda c: (c, 0, 0))],
        out_shape=[jax.ShapeDtypeStruct((L, SW), bf16), jax.ShapeDtypeStruct((L, SW), bf16),
                   jax.ShapeDtypeStruct((NC, 8, NST), f32)],
        scratch_shapes=[pltpu.VMEM((T, NST), f32), pltpu.VMEM((8, NST), f32)],
        name="s5_fwd", compiler_params=_cparams(("arbitrary",)))(proj, pm, pmt, bs, cs, arep, apow, dskip)


def _s5_bwd(cfg, proj, dy, cin, tabs, dskip):
    L, SW, T = cfg.L, cfg.SW, cfg.T
    G, NS, SC, SH, NST = _s5_dims(cfg)
    bs, cs, arep, apow, pm, pmt = tabs
    W = min(512, SH)
    S = T // 8
    NC = L // T

    def body(u_ref, dy_ref, cin_ref, pm_ref, pmt_ref, bs_ref, cs_ref, arep_ref, apow_ref, dskip_ref,
             du_ref, da_ref, db_hbm, dc_hbm, dd_ref, st, gs, x0, g0, db_acc, dc_acc, sem):
        c = pl.program_id(0)

        @pl.when(c == 0)
        def _():
            g0[...] = jnp.zeros_like(g0)
            da_ref[...] = jnp.zeros_like(da_ref)
            dd_ref[...] = jnp.zeros_like(dd_ref)
            db_acc[...] = jnp.zeros_like(db_acc)
            dc_acc[...] = jnp.zeros_like(dc_acc)

        up = jnp.dot(pm_ref[...], u_ref[...], preferred_element_type=f32)
        upb = up.astype(bf16)
        dyp = jnp.dot(pm_ref[...], dy_ref[...], preferred_element_type=f32)
        dypb = dyp.astype(bf16)
        x0[...] = jnp.zeros_like(x0)
        for rc, ic in _lane_slabs(cfg, W):
            x0[:, rc:rc + W] = jnp.broadcast_to(cin_ref[0, 0:1, rc:rc + W], (8, W))
            x0[:, ic:ic + W] = jnp.broadcast_to(cin_ref[0, 0:1, ic:ic + W], (8, W))
        _s5_forward_chunk(cfg, W, upb, bs_ref, arep_ref, apow_ref, st, x0, lambda *a: None)
        for s in range(NS):
            gs[:, 2 * SH * s:2 * SH * (s + 1)] = lax.dot_general(
                dypb[:, SC * s:SC * (s + 1)], cs_ref[s], (((1,), (1,)), ((), ())), preferred_element_type=f32)
        rows = lax.broadcasted_iota(jnp.int32, (8, W), 0)
        zero = jnp.zeros((8, W), f32)
        for rc, ic in _lane_slabs(cfg, W):
            ar = arep_ref[:, rc:rc + W]
            ai = arep_ref[:, ic:ic + W]

            def rstep(k, carry, rc=rc, ic=ic, ar=ar, ai=ai):
                gr, gi = carry
                r0 = pl.multiple_of((S - 1 - k) * 8, 8)
                nr = ar * gr + ai * gi + gs[pl.ds(r0, 8), rc:rc + W]
                ni = ar * gi - ai * gr + gs[pl.ds(r0, 8), ic:ic + W]
                gs[pl.ds(r0, 8), rc:rc + W] = nr
                gs[pl.ds(r0, 8), ic:ic + W] = ni
                return nr, ni

            er, ei = lax.fori_loop(0, S, rstep, (zero, zero))
            pr = apow_ref[:, rc:rc + W]
            pi = apow_ref[:, ic:ic + W]
            g0r = g0[:, rc:rc + W]
            g0i = g0[:, ic:ic + W]
            cr = jnp.where(rows == 7, g0r, 0.0)
            ci = jnp.where(rows == 7, g0i, 0.0)
            for _ in range(7):
                fr = er + pr * cr + pi * ci
                fi = ei + pr * ci - pi * cr
                cr = jnp.where(rows == 7, g0r, pltpu.roll(fr, 7, axis=0))
                ci = jnp.where(rows == 7, g0i, pltpu.roll(fi, 7, axis=0))
            fr = er + pr * cr + pi * ci
            fi = ei + pr * ci - pi * cr
            g0[:, rc:rc + W] = jnp.broadcast_to(fr[0:1, :], (8, W))
            g0[:, ic:ic + W] = jnp.broadcast_to(fi[0:1, :], (8, W))

            def fix(k, carry, rc=rc, ic=ic, ar=ar, ai=ai):
                kr, ki, accr, acci = carry
                i = S - 1 - k
                r0 = pl.multiple_of(i * 8, 8)
                rp = pl.multiple_of((i - 1) * 8, 8)
                nr = ar * kr + ai * ki
                ni = ar * ki - ai * kr
                gr = gs[pl.ds(r0, 8), rc:rc + W] + nr
                gi = gs[pl.ds(r0, 8), ic:ic + W] + ni
                gs[pl.ds(r0, 8), rc:rc + W] = gr
                gs[pl.ds(r0, 8), ic:ic + W] = gi
                xr = st[pl.ds(rp, 8), rc:rc + W]
                xi = st[pl.ds(rp, 8), ic:ic + W]
                return nr, ni, accr + gr * xr + gi * xi, acci + gi * xr - gr * xi

            kr, ki, accr, acci = lax.fori_loop(0, S - 1, fix, (cr, ci, zero, zero))
            nr = ar * kr + ai * ki
            ni = ar * ki - ai * kr
            gr = gs[0:8, rc:rc + W] + nr
            gi = gs[0:8, ic:ic + W] + ni
            gs[0:8, rc:rc + W] = gr
            gs[0:8, ic:ic + W] = gi
            xr = cin_ref[0, :, rc:rc + W]
            xi = cin_ref[0, :, ic:ic + W]
            da_ref[:, rc:rc + W] += accr + gr * xr + gi * xi
            da_ref[:, ic:ic + W] += acci + gi * xr - gr * xi

        dups = []
        for s in range(NS):
            gsb = gs[:, 2 * SH * s:2 * SH * (s + 1)].astype(bf16)
            dups.append(lax.dot_general(gsb, bs_ref[s], (((1,), (1,)), ((), ())), preferred_element_type=f32))
            db_acc[s] += lax.dot_general(upb[:, SC * s:SC * (s + 1)], gsb, (((0,), (0,)), ((), ())),
                                         preferred_element_type=f32)
            dc_acc[s] += lax.dot_general(st[:, 2 * SH * s:2 * SH * (s + 1)].astype(bf16), dypb[:, SC * s:SC * (s + 1)],
                                         (((0,), (0,)), ((), ())), preferred_element_type=f32)
        dup = jnp.concatenate(dups, axis=1) + dskip_ref[...] * dyp
        du_ref[...] = jnp.dot(pmt_ref[...], dup.astype(bf16), preferred_element_type=f32).astype(bf16)
        dd_ref[...] += jnp.sum(dyp * up, axis=0, keepdims=True)

        @pl.when(c == NC - 1)
        def _():
            cp1 = pltpu.make_async_copy(db_acc, db_hbm, sem.at[0])
            cp2 = pltpu.make_async_copy(dc_acc, dc_hbm, sem.at[1])
            cp1.start()
            cp2.start()
            cp1.wait()
            cp2.wait()

    rev = lambda c: (NC - 1 - c, 0)
    const2 = lambda shape: pl.BlockSpec(shape, lambda c: (0, 0))
    const3 = lambda shape: pl.BlockSpec(shape, lambda c: (0, 0, 0))
    return pl.pallas_call(
        body, grid=(NC,),
        in_specs=[pl.BlockSpec((T, SW), rev), pl.BlockSpec((T, SW), rev), pl.BlockSpec((1, 8, NST), lambda c: (NC - 1 - c, 0, 0)),
                  const2((T, T)), const2((T, T)), const3((NS, SC, 2 * SH)), const3((NS, 2 * SH, SC)),
                  const2((8, NST)), const2((8, NST)), const2((1, SW))],
        out_specs=[pl.BlockSpec((T, SW), rev), const2((8, NST)), pl.BlockSpec(memory_space=pl.ANY),
                   pl.BlockSpec(memory_space=pl.ANY), const2((1, SW))],
        out_shape=[jax.ShapeDtypeStruct((L, SW), bf16), jax.ShapeDtypeStruct((8, NST), f32),
                   jax.ShapeDtypeStruct((NS, SC, 2 * SH), f32), jax.ShapeDtypeStruct((NS, 2 * SH, SC), f32),
                   jax.ShapeDtypeStruct((1, SW), f32)],
        scratch_shapes=[pltpu.VMEM((T, NST), f32), pltpu.VMEM((T, NST), f32), pltpu.VMEM((8, NST), f32),
                        pltpu.VMEM((8, NST), f32), pltpu.VMEM((NS, SC, 2 * SH), f32), pltpu.VMEM((NS, 2 * SH, SC), f32),
                        pltpu.SemaphoreType.DMA((2,))],
        name="s5_bwd", compiler_params=_cparams(("arbitrary",)))(proj, dy, cin, pm, pmt, bs, cs, arep, apow, dskip)


def _s5_param_grads(cfg, da, db_full, dc_full):
    G, NS, SC, SH, NST = _s5_dims(cfg)
    das = da.sum(axis=0).reshape(NS, 2, SH)
    dabr = das[:, 0].reshape(G, _SSM_STATE)
    dabi = das[:, 1].reshape(G, _SSM_STATE)
    dbf = db_full.reshape(NS, _SLAB_GROUPS, _SSM_GROUP, 2, _SLAB_GROUPS, _SSM_STATE)
    dbb = jnp.einsum("sghrgp->rsgph", dbf).reshape(2, G, _SSM_STATE, _SSM_GROUP)
    dcf = dc_full.reshape(NS, 2, _SLAB_GROUPS, _SSM_STATE, _SLAB_GROUPS, _SSM_GROUP)
    dcc = jnp.einsum("srgpgh->rsghp", dcf).reshape(2, G, _SSM_GROUP, _SSM_STATE)
    return dabr, dabi, dbb[0], dbb[1], dcc[0], -dcc[1]


def _coords():
    return lax.axis_index("x"), lax.axis_index("y"), lax.axis_index("c")


def _other_chips(x, y):
    return [(1 - x, y), (x, 1 - y), (1 - x, 1 - y)]


def _allreduce8(name, v):
    R = v.shape[0]

    def body(v_ref, o_ref, sib, chips, mine, ssem, rsem):
        x, y, c = _coords()
        d2d = pltpu.make_async_remote_copy(src_ref=v_ref, dst_ref=sib, send_sem=ssem.at[0], recv_sem=rsem.at[0],
                                           device_id=(x, y, 1 - c), device_id_type=_MESH)
        d2d.start()
        d2d.wait()
        mine[...] = v_ref[...] + sib[...]
        cps = [pltpu.make_async_remote_copy(src_ref=mine, dst_ref=chips.at[j], send_sem=ssem.at[1 + j],
                                            recv_sem=rsem.at[1 + j], device_id=(*chip, c), device_id_type=_MESH)
               for j, chip in enumerate(_other_chips(x, y))]
        for cp in cps:
            cp.start()
        for cp in cps:
            cp.wait()
        o_ref[...] = (mine[...] + chips[1]) + (chips[0] + chips[2])

    vm = pl.BlockSpec(memory_space=pltpu.VMEM)
    return pl.pallas_call(
        body, in_specs=[vm], out_specs=vm, out_shape=jax.ShapeDtypeStruct((R, 128), f32),
        scratch_shapes=[pltpu.VMEM((R, 128), f32), pltpu.VMEM((3, R, 128), f32), pltpu.VMEM((R, 128), f32),
                        pltpu.SemaphoreType.DMA((4,)), pltpu.SemaphoreType.DMA((4,))],
        name=name, compiler_params=pltpu.CompilerParams(vmem_limit_bytes=_VMEM_LIMIT))(v)


def _cast_into_slot(name, w, k_idx):
    R, C = w.shape
    tr = _tile(R, 256, 16)

    def body(k_ref, w_ref, o_ref):
        o_ref[...] = w_ref[...].astype(bf16)

    gs = pltpu.PrefetchScalarGridSpec(
        num_scalar_prefetch=1, grid=(R // tr,),
        in_specs=[pl.BlockSpec((tr, C), lambda r, kr: (r, 0))],
        out_specs=pl.BlockSpec((None, tr, C), lambda r, kr: (kr[0], r, 0)))
    return pl.pallas_call(body, grid_spec=gs, out_shape=jax.ShapeDtypeStruct((_NCHIP, R, C), bf16), name=name,
                          compiler_params=_cparams(("parallel",)))(k_idx, w)


def _handshake(peers):
    barrier = pltpu.get_barrier_semaphore()
    for peer in peers:
        pl.semaphore_signal(barrier, inc=1, device_id=peer, device_id_type=_MESH)
    pl.semaphore_wait(barrier, len(peers))


def _allgather_weights(name, bufs, collective_id):
    n = len(bufs)
    refs = [jax.new_ref(b, memory_space=pltpu.MemorySpace.HBM) for b in bufs]

    @pl.kernel(mesh=plsc.ScalarSubcoreMesh(axis_name="seq", num_cores=1), name=name,
               scratch_types=(pltpu.SemaphoreType.DMA((n, 3)), pltpu.SemaphoreType.DMA((n, 3)),
                              pltpu.SemaphoreType.DMA((n, 3)), pltpu.SemaphoreType.DMA((n, 3))),
               compiler_params=pltpu.CompilerParams(collective_id=collective_id))
    def launch(ssem, rsem, fssem, frsem):
        x, y, c = _coords()
        k = 2 * x + y
        others = _other_chips(x, y)
        _handshake([(x, y, 1 - c)] + [(*chip, c) for chip in others])
        sends, fwds = [], []
        for w in range(n):
            rh = refs[w].shape[1] // 2
            mine = refs[w].at[k, pl.ds(c * rh, rh)]
            for j, chip in enumerate(others):
                cp = pltpu.make_async_remote_copy(
                    src_ref=mine, dst_ref=mine, send_sem=ssem.at[w, j], recv_sem=rsem.at[w, j],
                    device_id=(*chip, c), device_id_type=_MESH)
                cp.start()
                sends.append(cp)
        for w in range(n):
            rh = refs[w].shape[1] // 2
            for j, (ox, oy) in enumerate(others):
                landed = refs[w].at[2 * ox + oy, pl.ds(c * rh, rh)]
                pltpu.make_async_remote_copy(
                    src_ref=landed, dst_ref=landed, send_sem=ssem.at[w, j], recv_sem=rsem.at[w, j],
                    device_id=(ox, oy, c), device_id_type=_MESH).wait_recv()
                cp = pltpu.make_async_remote_copy(
                    src_ref=landed, dst_ref=landed, send_sem=fssem.at[w, j], recv_sem=frsem.at[w, j],
                    device_id=(x, y, 1 - c), device_id_type=_MESH)
                cp.start()
                fwds.append(cp)
        for w in range(n):
            rh = refs[w].shape[1] // 2
            for j, (ox, oy) in enumerate(others):
                passed = refs[w].at[2 * ox + oy, pl.ds((1 - c) * rh, rh)]
                pltpu.make_async_remote_copy(
                    src_ref=passed, dst_ref=passed, send_sem=fssem.at[w, j], recv_sem=frsem.at[w, j],
                    device_id=(x, y, 1 - c), device_id_type=_MESH).wait_recv()
        for cp in sends + fwds:
            cp.wait_send()

    launch()
    return [r[...] for r in refs]


def _sequencer_kernel(name, collective_id, sems, body):
    pl.kernel(body, mesh=plsc.ScalarSubcoreMesh(axis_name="seq", num_cores=1), name=name, scratch_types=sems,
              compiler_params=pltpu.CompilerParams(collective_id=collective_id))()


def _hbm_ref(a):
    return jax.new_ref(a, memory_space=pltpu.MemorySpace.HBM)


def _exchange_halves(name, grads, collective_id):
    n = len(grads)
    srcs = [_hbm_ref(g) for g in grads]
    dsts = [jax.empty_ref(jax.ShapeDtypeStruct((g.shape[0], g.shape[1] // 2, g.shape[2]), g.dtype),
                          memory_space=pltpu.MemorySpace.HBM) for g in grads]

    def body(ssem, rsem):
        x, y, c = _coords()
        _handshake([(x, y, 1 - c)])
        cps = []
        for w in range(n):
            rh = srcs[w].shape[1] // 2
            cp = pltpu.make_async_remote_copy(
                src_ref=srcs[w].at[:, pl.ds((1 - c) * rh, rh)], dst_ref=dsts[w], send_sem=ssem.at[w], recv_sem=rsem.at[w],
                device_id=(x, y, 1 - c), device_id_type=_MESH)
            cp.start()
            cps.append(cp)
        for cp in cps:
            cp.wait()

    _sequencer_kernel(name, collective_id, (pltpu.SemaphoreType.DMA((n,)), pltpu.SemaphoreType.DMA((n,))), body)
    return [d[...] for d in dsts]


def _scatter_shards(name, parts, collective_id):
    n = len(parts)
    srcs = [_hbm_ref(p) for p in parts]
    dsts = [jax.empty_ref(jax.ShapeDtypeStruct((3,) + p.shape[1:], p.dtype), memory_space=pltpu.MemorySpace.HBM)
            for p in parts]

    def body(ssem, rsem):
        x, y, c = _coords()
        others = _other_chips(x, y)
        _handshake([(*chip, c) for chip in others])
        cps = []
        for w in range(n):
            for j, (ox, oy) in enumerate(others):
                cp = pltpu.make_async_remote_copy(
                    src_ref=srcs[w].at[2 * ox + oy], dst_ref=dsts[w].at[j], send_sem=ssem.at[w, j], recv_sem=rsem.at[w, j],
                    device_id=(ox, oy, c), device_id_type=_MESH)
                cp.start()
                cps.append(cp)
        for cp in cps:
            cp.wait()

    _sequencer_kernel(name, collective_id, (pltpu.SemaphoreType.DMA((n, 3)), pltpu.SemaphoreType.DMA((n, 3))), body)
    return [d[...] for d in dsts]


def _join_halves(name, bufs, collective_id):
    n = len(bufs)
    refs = [_hbm_ref(b) for b in bufs]

    def body(ssem, rsem):
        x, y, c = _coords()
        _handshake([(x, y, 1 - c)])
        cps = []
        for w in range(n):
            rh = refs[w].shape[0] // 2
            mine = refs[w].at[pl.ds(c * rh, rh)]
            cp = pltpu.make_async_remote_copy(src_ref=mine, dst_ref=mine, send_sem=ssem.at[w], recv_sem=rsem.at[w],
                                              device_id=(x, y, 1 - c), device_id_type=_MESH)
            cp.start()
            cps.append(cp)
        for w in range(n):
            rh = refs[w].shape[0] // 2
            theirs = refs[w].at[pl.ds((1 - c) * rh, rh)]
            pltpu.make_async_remote_copy(src_ref=theirs, dst_ref=theirs, send_sem=ssem.at[w], recv_sem=rsem.at[w],
                                         device_id=(x, y, 1 - c), device_id_type=_MESH).wait_recv()
        for cp in cps:
            cp.wait_send()

    _sequencer_kernel(name, collective_id, (pltpu.SemaphoreType.DMA((n,)), pltpu.SemaphoreType.DMA((n,))), body)
    return [r[...] for r in refs]


def _add_own_half(name, g, t, c_idx, after):
    S, R, C = g.shape
    rh = R // 2
    tr = _tile(rh, 256, 16)
    nrb = rh // tr

    def body(c_ref, g_ref, t_ref, after_ref, o_ref):
        o_ref[...] = (g_ref[...] + t_ref[...]).astype(bf16)

    gs = pltpu.PrefetchScalarGridSpec(
        num_scalar_prefetch=1, grid=(S, nrb),
        in_specs=[pl.BlockSpec((None, tr, C), lambda s, r, cr: (s, cr[0] * nrb + r, 0)),
                  pl.BlockSpec((None, tr, C), lambda s, r, cr: (s, r, 0)),
                  pl.BlockSpec(memory_space=pl.ANY)],
        out_specs=pl.BlockSpec((None, tr, C), lambda s, r, cr: (s, r, 0)))
    return pl.pallas_call(body, grid_spec=gs, out_shape=jax.ShapeDtypeStruct((S, rh, C), bf16), name=name,
                          compiler_params=_cparams(("parallel", "parallel")))(c_idx, g, t, after)


def _add_shard_parts(name, g, t, r, kc_idx, after):
    S, R, C = g.shape
    rh = R // 2
    tr = _tile(rh, 256, 16)
    nrb = rh // tr

    def body(kc_ref, g_ref, t_ref, r_ref, after_ref, o_ref):
        own = g_ref[...] + t_ref[...]
        o_ref[...] = (own + r_ref[1].astype(f32)) + (r_ref[0].astype(f32) + r_ref[2].astype(f32))

    gs = pltpu.PrefetchScalarGridSpec(
        num_scalar_prefetch=1, grid=(nrb,),
        in_specs=[pl.BlockSpec((None, tr, C), lambda i, kc: (kc[0], kc[1] * nrb + i, 0)),
                  pl.BlockSpec((None, tr, C), lambda i, kc: (kc[0], i, 0)),
                  pl.BlockSpec((3, tr, C), lambda i, kc: (0, i, 0)),
                  pl.BlockSpec(memory_space=pl.ANY)],
        out_specs=pl.BlockSpec((tr, C), lambda i, kc: (kc[1] * nrb + i, 0)))
    return pl.pallas_call(body, grid_spec=gs, out_shape=jax.ShapeDtypeStruct((R, C), f32), name=name,
                          compiler_params=_cparams(("parallel",)))(kc_idx, g, t, r, after)


def _adamw(name, w, g, m, v):
    R, C = w.shape
    tr = _tile(R, 128, 8)

    def body(w_ref, g_ref, m_ref, v_ref, d_ref, nm_ref, nv_ref):
        gv = g_ref[...]
        nm = _ADAM_B1 * m_ref[...] + (1.0 - _ADAM_B1) * gv
        nv = _ADAM_B2 * v_ref[...] + (1.0 - _ADAM_B2) * (gv * gv)
        m_hat = nm / (1.0 - _ADAM_B1 ** _ADAM_STEP)
        v_hat = nv / (1.0 - _ADAM_B2 ** _ADAM_STEP)
        d_ref[...] = -_ADAM_LR * (m_hat / (jnp.sqrt(v_hat) + _ADAM_EPS) + _ADAM_WD * w_ref[...])
        nm_ref[...] = nm
        nv_ref[...] = nv

    spec = pl.BlockSpec((tr, C), lambda i: (i, 0))
    shape = jax.ShapeDtypeStruct((R, C), f32)
    return pl.pallas_call(body, grid=(R // tr,), in_specs=[spec] * 4, out_specs=[spec] * 3, out_shape=[shape] * 3,
                          name=name, compiler_params=_cparams(("parallel",)))(w, g, m, v)


def _pack(arrs):
    flat = jnp.concatenate([a.reshape(-1).astype(f32) for a in arrs])
    n = flat.shape[0]
    pad = (-n) % (128 * 128)
    return jnp.pad(flat, (0, pad)).reshape(-1, 128)


def _unpack(packed, shapes):
    flat = packed.reshape(-1)
    out, off = [], 0
    for s in shapes:
        n = math.prod(s)
        out.append(flat[off:off + n].reshape(s))
        off += n
    return out


_BIG = ("w_in", "w_glu", "w_ssm_out", "w_conv_out", "w_o", "w_up", "w_down")
_SMALL = ("norm_tok", "a_re", "a_im", "log_dt", "b_re", "b_im", "c_re", "c_im", "d_skip", "conv_w", "conv_b",
          "norm_ffn", "ffn_conv_w", "ffn_conv_b", "norm_final")
_WEIGHTS = ("norm_tok", "w_in", "a_re", "a_im", "log_dt", "b_re", "b_im", "c_re", "c_im", "d_skip", "w_glu",
            "w_ssm_out", "conv_w", "conv_b", "w_conv_out", "w_o", "norm_ffn", "w_up", "ffn_conv_w", "ffn_conv_b",
            "w_down", "norm_final")


def _step(cfg, x, tgt, p, m, v):
    L, D, SW, CW, F = cfg.L, cfg.D, cfg.SW, cfg.CW, cfg.F
    xi, yi, ci = _coords()
    k_idx = (2 * xi + yi).astype(jnp.int32).reshape(1)
    c_idx = ci.astype(jnp.int32).reshape(1)
    x = x.reshape(L, D)
    tgt = tgt.reshape(L, D)

    big2d = {n: p[n].reshape(p[n].shape[-2], p[n].shape[-1]) for n in _BIG}
    slots = {n: _cast_into_slot("cast_" + n, big2d[n], k_idx) for n in _BIG}
    wg = {}
    for cid, (gname, group) in enumerate((("allgather_w_in", ("w_in",)),
                                          ("allgather_w_mixer", ("w_glu", "w_ssm_out", "w_conv_out", "w_o")),
                                          ("allgather_w_up", ("w_up",)), ("allgather_w_down", ("w_down",)))):
        wg.update(zip(group, _allgather_weights(gname, [slots[n] for n in group], cid)))
    w_in, w_so, w_co, w_up = wg["w_in"], wg["w_ssm_out"], wg["w_conv_out"], wg["w_up"]
    w_glu = wg["w_glu"].reshape(SW, SW)
    w_o = wg["w_o"].reshape(D, D)
    w_down = wg["w_down"].reshape(F, D)
    kk = k_idx[0]
    cw_full = lax.dynamic_update_slice(jnp.zeros((3, CW), f32), p["conv_w"].reshape(3, CW // _NCHIP), (0, kk * (CW // _NCHIP)))
    fw_full = lax.dynamic_update_slice(jnp.zeros((3, F), f32), p["ffn_conv_w"].reshape(3, F // _NCHIP), (0, kk * (F // _NCHIP)))
    south = (ci == 0).astype(f32)
    conv_w, ffn_conv_w = _unpack(_allreduce8("allgather_conv_filters", _pack([cw_full * south, fw_full * south])),
                                 [(3, CW), (3, F)])
    conv_b = p["conv_b"].reshape(1, CW)
    ffn_conv_b = p["ffn_conv_b"].reshape(1, F)
    norm_tok = p["norm_tok"].reshape(1, D)
    norm_ffn = p["norm_ffn"].reshape(1, D)
    norm_final = p["norm_final"].reshape(1, D)
    dskip = p["d_skip"].reshape(1, SW)

    s5_in = (p["a_re"][0], p["a_im"][0], p["log_dt"][0], p["b_re"][0], p["b_im"][0])
    (abr, abi, bbr, bbi), disc_vjp = jax.vjp(_discretize, *s5_in)
    tabs = _s5_tables(cfg, abr, abi, bbr, bbi, p["c_re"][0], p["c_im"][0])

    xn1, r1 = _rms_fwd("rms_tok", x, norm_tok)
    proj = _mm_nn("in_proj", xn1, w_in, bf16)[0]
    y_s, ya0, cin = _s5_fwd(cfg, proj, tabs, dskip)

    def glu_extras(tm, tn):
        return [(ya0, pl.BlockSpec((tm, tn), lambda i, j, k: (i, j)))]

    def glu_epi(acc, a0):
        return a0.astype(f32) * jax.nn.sigmoid(acc), acc

    ya1, z = _mm_nn("glu", ya0, w_glu, None, extras_fn=glu_extras, epilogue=glu_epi, out_dtypes=[bf16, bf16])
    yb0 = _convb_fwd(cfg, proj, conv_w, conv_b)
    merged, ya, yb = _merge_fwd(cfg, ya1, yb0, proj, w_so, w_co)

    def res_extras(res):
        return lambda tm, tn: [(res, pl.BlockSpec((tm, tn), lambda i, j, k: (i, j)))]

    def res_epi(acc, res):
        return (res + acc,)

    h1 = _mm_nn("out_proj", merged, w_o, f32, extras_fn=res_extras(x), epilogue=res_epi)[0]
    xn2, r2 = _rms_fwd("rms_ffn", h1, norm_ffn)
    hh = _mm_nn("ffn_up", xn2, w_up, bf16, tn=1408)[0]
    fact = _ffn_act(cfg, hh, ffn_conv_w, ffn_conv_b)
    h2 = _mm_nn("ffn_down", fact, w_down, f32, tk=_tile(F, 1408, 128), extras_fn=res_extras(h1), epilogue=res_epi)[0]
    dh2, dh2b, g_norm_final, loss_tile = _loss_head("loss_head", h2, tgt, norm_final)

    kc_idx = jnp.concatenate([k_idx, c_idx])
    reduced, chains = {}, {}

    def rs_halves(tag, collective_id, names, gs):
        chains[tag] = dict(cid=collective_id, names=names, gs=gs,
                           sib=_exchange_halves("grad_halves_" + tag, gs, collective_id))

    def rs_shards(tag, after):
        ch = chains[tag]
        parts = [_add_own_half("grad_add_halves_" + n, g, t, c_idx, after)
                 for n, g, t in zip(ch["names"], ch["gs"], ch["sib"])]
        ch["chips"] = _scatter_shards("grad_shards_" + tag, parts, ch["cid"] + 1)

    def rs_join(tag, after):
        ch = chains[tag]
        halves = [_add_shard_parts("grad_add_chips_" + n, g, t, r, kc_idx, after)
                  for n, g, t, r in zip(ch["names"], ch["gs"], ch["sib"], ch["chips"])]
        ch["joined"] = _join_halves("grad_join_" + tag, halves, ch["cid"] + 2)

    def rs_done(tag):
        reduced.update(zip(chains[tag]["names"], chains[tag]["joined"]))

    df = _mm_nt("ffn_down_dx", [dh2b], w_down, [bf16], tn=1408)[0]
    g_w_down = _mm_tn("ffn_down_dw", fact, [dh2b], tm=1408)
    rs_halves("ffn_down", 4, ["w_down"], [g_w_down.reshape(_NCHIP, F // _NCHIP, D)])
    da_pre, dgate, g_ffn_conv_w, g_ffn_conv_b = _ffn_act_bwd(cfg, hh, df, ffn_conv_w, ffn_conv_b)
    rs_shards("ffn_down", da_pre)
    g_w_up = _mm_tn("ffn_up_dw", xn2, [da_pre, dgate], shards=_NCHIP, tn=1408)
    rs_halves("ffn_up", 7, ["w_up"], [g_w_up])
    dxn2 = _mm_nt("ffn_up_dx", [da_pre, dgate], w_up, [bf16], tk=1408)[0]
    rs_shards("ffn_up", dxn2)
    rs_join("ffn_down", dxn2)
    dh1, dh1b, g_norm_ffn = _rms_bwd("rms_ffn_bwd", dxn2, h1, r2, norm_ffn, dh2)

    def merge_extras(tm, tn):
        off_a = (SW + 3 * CW) // tn
        off_b = (SW + 3 * CW + D) // tn
        blk = pl.BlockSpec((tm, tn), lambda i, j, k: (i, j))
        return [(ya, blk), (yb, blk), (proj, pl.BlockSpec((tm, tn), lambda i, j, k: (i, off_a + j))),
                (proj, pl.BlockSpec((tm, tn), lambda i, j, k: (i, off_b + j)))]

    def merge_epi(acc, ya_t, yb_t, ma_t, mb_t):
        sa = jax.nn.sigmoid(ma_t.astype(f32))
        sb = jax.nn.sigmoid(mb_t.astype(f32))
        return (acc * sa, acc * sb, acc * ya_t.astype(f32) * sa * (1.0 - sa), acc * yb_t.astype(f32) * sb * (1.0 - sb))

    dya, dyb, dma, dmb = _mm_nt("out_proj_dx", [dh1b], w_o, [bf16] * 4, extras_fn=merge_extras, epilogue=merge_epi)
    g_w_o = _mm_tn("out_proj_dw", merged, [dh1b])

    def glu_bwd_extras(tm, tn):
        blk = pl.BlockSpec((tm, tn), lambda i, j, k: (i, j))
        return [(ya0, blk), (z, blk)]

    def glu_bwd_epi(acc, a0, zz):
        s = jax.nn.sigmoid(zz.astype(f32))
        return (acc * a0.astype(f32) * s * (1.0 - s), acc * s)

    dz, t1 = _mm_nt("ssm_out_dx", [dya], w_so, [bf16, bf16], extras_fn=glu_bwd_extras, epilogue=glu_bwd_epi)
    g_w_so = _mm_tn("ssm_out_dw", ya1, [dya], shards=_NCHIP)
    dyb0 = _mm_nt("conv_out_dx", [dyb], w_co, [bf16])[0]
    g_w_co = _mm_tn("conv_out_dw", yb0, [dyb], shards=_NCHIP)
    dvbc, g_conv_w, g_conv_b = _convb_bwd(cfg, proj, dyb0, conv_w, conv_b)

    def gelu_bwd_extras(tm, tn):
        blk = pl.BlockSpec((tm, tn), lambda i, j, k: (i, j))
        return [(t1, blk), (y_s, blk)]

    def gelu_bwd_epi(acc, tt, yy):
        return ((tt.astype(f32) + acc) * _gelu_grad(yy.astype(f32)),)

    dy_s = _mm_nt("glu_dx", [dz], w_glu, [bf16], extras_fn=gelu_bwd_extras, epilogue=gelu_bwd_epi)[0]
    g_w_glu = _mm_tn("glu_dw", ya0, [dz])
    rs_halves("mixer", 10, ["w_o", "w_ssm_out", "w_conv_out", "w_glu"],
              [g_w_o.reshape(_NCHIP, D // _NCHIP, D), g_w_so, g_w_co, g_w_glu.reshape(_NCHIP, SW // _NCHIP, SW)])
    rs_join("ffn_up", g_w_glu)
    du, da_acc, db_full, dc_full, g_dskip = _s5_bwd(cfg, proj, dy_s, cin, tabs, dskip)
    rs_shards("mixer", du)
    dproj = [du, dvbc, dma, dmb]
    g_w_in = _mm_tn("in_proj_dw", xn1, dproj, shards=_NCHIP)
    rs_halves("in_proj", 13, ["w_in"], [g_w_in])
    dxn1 = _mm_nt("in_proj_dx", dproj, w_in, [bf16], tk=1024)[0]
    rs_shards("in_proj", dxn1)
    rs_join("mixer", dxn1)
    dx, _, g_norm_tok = _rms_bwd("rms_tok_bwd", dxn1, x, r1, norm_tok, dh1)

    dabr, dabi, dbbr, dbbi, g_c_re, g_c_im = _s5_param_grads(cfg, da_acc, db_full, dc_full)
    g_a_re, g_a_im, g_log_dt, g_b_re, g_b_im = disc_vjp((dabr, dabi, dbbr, dbbi))

    small_g = {"norm_tok": g_norm_tok, "a_re": g_a_re, "a_im": g_a_im, "log_dt": g_log_dt, "b_re": g_b_re, "b_im": g_b_im,
               "c_re": g_c_re, "c_im": g_c_im, "d_skip": g_dskip, "conv_w": g_conv_w, "conv_b": g_conv_b,
               "norm_ffn": g_norm_ffn, "ffn_conv_w": g_ffn_conv_w, "ffn_conv_b": g_ffn_conv_b, "norm_final": g_norm_final}
    small_shapes = [small_g[n].shape for n in _SMALL]
    summed = dict(zip(_SMALL, _unpack(_allreduce8("allreduce_small_grads", _pack([small_g[n] for n in _SMALL])), small_shapes)))
    summed["conv_w"] = lax.dynamic_slice(summed["conv_w"], (0, kk * (CW // _NCHIP)), (3, CW // _NCHIP))
    summed["ffn_conv_w"] = lax.dynamic_slice(summed["ffn_conv_w"], (0, kk * (F // _NCHIP)), (3, F // _NCHIP))

    grads, deltas, new_m, new_v = {}, {}, {}, {}

    def adamw_big(names):
        for n in names:
            d_, m_, v_ = _adamw("adamw_" + n, big2d[n], reduced[n], m[n].reshape(big2d[n].shape), v[n].reshape(big2d[n].shape))
            grads[n] = reduced[n].reshape(p[n].shape)
            deltas[n], new_m[n], new_v[n] = d_.reshape(p[n].shape), m_.reshape(p[n].shape), v_.reshape(p[n].shape)

    rs_done("ffn_down")
    rs_done("ffn_up")
    adamw_big(["w_down", "w_up"])
    rs_join("in_proj", deltas["w_up"])
    rs_done("mixer")
    adamw_big(["w_o", "w_ssm_out", "w_conv_out", "w_glu"])
    shapes = [p[n].shape for n in _SMALL]
    sg = _pack([summed[n] for n in _SMALL])
    d_, m_, v_ = _adamw("adamw_small", _pack([p[n] for n in _SMALL]), sg, _pack([m[n] for n in _SMALL]),
                        _pack([v[n] for n in _SMALL]))
    for n, dd, mm, vv in zip(_SMALL, _unpack(d_, shapes), _unpack(m_, shapes), _unpack(v_, shapes)):
        grads[n] = summed[n].reshape(p[n].shape)
        deltas[n], new_m[n], new_v[n] = dd, mm, vv
    rs_done("in_proj")
    adamw_big(["w_in"])

    loss = lax.psum(loss_tile[0, 0], ("x", "y", "c"))
    return (loss, dx.reshape(1, L, D), *[grads[n] for n in _WEIGHTS], *[deltas[n] for n in _WEIGHTS],
            *[new_m[n] for n in _WEIGHTS], *[new_v[n] for n in _WEIGHTS])


def kernel(x, norm_tok, w_in, a_re, a_im, log_dt, b_re, b_im, c_re, c_im, d_skip, w_glu, w_ssm_out, conv_w, conv_b, w_conv_out, w_o, norm_ffn, w_up, ffn_conv_w, ffn_conv_b, w_down, norm_final, loss_target, m_norm_tok, m_w_in, m_a_re, m_a_im, m_log_dt, m_b_re, m_b_im, m_c_re, m_c_im, m_d_skip, m_w_glu, m_w_ssm_out, m_conv_w, m_conv_b, m_w_conv_out, m_w_o, m_norm_ffn, m_w_up, m_ffn_conv_w, m_ffn_conv_b, m_w_down, m_norm_final, v_norm_tok, v_w_in, v_a_re, v_a_im, v_log_dt, v_b_re, v_b_im, v_c_re, v_c_im, v_d_skip, v_w_glu, v_w_ssm_out, v_conv_w, v_conv_b, v_w_conv_out, v_w_o, v_norm_ffn, v_w_up, v_ffn_conv_w, v_ffn_conv_b, v_w_down, v_norm_final):
    p = dict(norm_tok=norm_tok, w_in=w_in, a_re=a_re, a_im=a_im, log_dt=log_dt, b_re=b_re, b_im=b_im, c_re=c_re,
             c_im=c_im, d_skip=d_skip, w_glu=w_glu, w_ssm_out=w_ssm_out, conv_w=conv_w, conv_b=conv_b,
             w_conv_out=w_conv_out, w_o=w_o, norm_ffn=norm_ffn, w_up=w_up, ffn_conv_w=ffn_conv_w,
             ffn_conv_b=ffn_conv_b, w_down=w_down, norm_final=norm_final)
    m = dict(norm_tok=m_norm_tok, w_in=m_w_in, a_re=m_a_re, a_im=m_a_im, log_dt=m_log_dt, b_re=m_b_re, b_im=m_b_im,
             c_re=m_c_re, c_im=m_c_im, d_skip=m_d_skip, w_glu=m_w_glu, w_ssm_out=m_w_ssm_out, conv_w=m_conv_w,
             conv_b=m_conv_b, w_conv_out=m_w_conv_out, w_o=m_w_o, norm_ffn=m_norm_ffn, w_up=m_w_up,
             ffn_conv_w=m_ffn_conv_w, ffn_conv_b=m_ffn_conv_b, w_down=m_w_down, norm_final=m_norm_final)
    v = dict(norm_tok=v_norm_tok, w_in=v_w_in, a_re=v_a_re, a_im=v_a_im, log_dt=v_log_dt, b_re=v_b_re, b_im=v_b_im,
             c_re=v_c_re, c_im=v_c_im, d_skip=v_d_skip, w_glu=v_w_glu, w_ssm_out=v_w_ssm_out, conv_w=v_conv_w,
             conv_b=v_conv_b, w_conv_out=v_w_conv_out, w_o=v_w_o, norm_ffn=v_norm_ffn, w_up=v_w_up,
             ffn_conv_w=v_ffn_conv_w, ffn_conv_b=v_ffn_conv_b, w_down=v_w_down, norm_final=v_norm_final)
    return _step(_Cfg(), x, loss_target, p, m, v)
```

```python
import functools
import math
from typing import NamedTuple

import jax
import jax.numpy as jnp
from jax import lax
from jax.experimental import pallas as pl
from jax.experimental.pallas import tpu as pltpu
from jax.experimental.pallas import tpu_sc as plsc

f32 = jnp.float32
bf16 = jnp.bfloat16
_MESH = pl.DeviceIdType.MESH

_EPS = 1e-6
_ADAM_LR = 0.001
_ADAM_B1 = 0.9
_ADAM_B2 = 0.999
_ADAM_EPS = 1e-08
_ADAM_WD = 0.01
_ADAM_STEP = 10
_SSM_GROUP = 16
_SSM_STATE = 64
_SLAB_GROUPS = 16
_NCHIP = 4
_VMEM_LIMIT = 56 * 2**20
_GELU_C = math.sqrt(2.0 / math.pi)
_GELU_A = 0.044715


class _Cfg(NamedTuple):
    L: int = 4096
    D: int = 2048
    SW: int = 1024
    CW: int = 1024
    F: int = 5632
    T: int = 256


def _tile(n, pref, align):
    t = min(n, pref)
    t -= t % align
    while t > align and n % t:
        t -= align
    assert t > 0 and n % t == 0, (n, pref, align)
    return t


def _cparams(sem):
    return pltpu.CompilerParams(dimension_semantics=sem, vmem_limit_bytes=_VMEM_LIMIT)


def _gelu(x):
    return 0.5 * x * (1.0 + jnp.tanh(_GELU_C * (x + _GELU_A * x * x * x)))


def _gelu_grad(x):
    th = jnp.tanh(_GELU_C * (x + _GELU_A * x * x * x))
    return 0.5 * (1.0 + th) + 0.5 * x * (1.0 - th * th) * _GELU_C * (1.0 + 3.0 * _GELU_A * x * x)


_NN = (((1,), (0,)), ((), ()))
_NT = (((1,), (1,)), ((), ()))
_TN = (((0,), (0,)), ((), ()))


def _whole(ref):
    return ref[...]


def _mm(name, operands, steps, *, grid, contract, outs, extras=(), epilogue=None, acc_shape=None, aliases=None):
    nop, ne = len(operands), len(extras)
    nk = len(steps)

    def body(*refs):
        op_refs = refs[:nop]
        e_refs = refs[nop:nop + ne]
        o_refs = refs[nop + ne:nop + ne + len(outs)]

        def partial(terms):
            tot = None
            for ai, av, bi, bv in terms:
                d = lax.dot_general(av(op_refs[ai]), bv(op_refs[bi]), contract, preferred_element_type=f32)
                tot = d if tot is None else tot + d
            return tot

        def finish(res):
            if epilogue is None:
                for o in o_refs:
                    o[...] = res.astype(o.dtype)
            else:
                epilogue(res, e_refs, o_refs)

        if nk == 1:
            finish(partial(steps[0][1]))
            return
        acc = refs[-1]
        kid = pl.program_id(len(grid) - 1)
        for k, terms in steps:
            def run(k=k, terms=terms):
                d = partial(terms)
                if k == 0:
                    acc[...] = d
                elif k < nk - 1:
                    acc[...] += d
                else:
                    finish(acc[...] + d)

            pl.when(kid == k)(run)

    sem = ("parallel",) * (len(grid) - (nk > 1)) + (("arbitrary",) if nk > 1 else ())
    return pl.pallas_call(
        body, grid=grid, in_specs=[o[1] for o in operands] + [e[1] for e in extras],
        out_specs=[o[1] for o in outs], out_shape=[o[0] for o in outs],
        scratch_shapes=[pltpu.VMEM(acc_shape, f32)] if nk > 1 else [],
        input_output_aliases=aliases or {}, name=name,
        compiler_params=_cparams(sem))(*[o[0] for o in operands], *[e[0] for e in extras])


def _mm_nn(name, a, w, out_dtypes, *, tm=1024, tn=1024, extras_fn=None, epilogue=None):
    M, K = a.shape
    sharded = w.ndim == 3
    Ns = w.shape[-1]
    N = Ns * (w.shape[0] if sharded else 1)
    tm, tn = _tile(M, tm, 16), _tile(Ns, tn, 128)
    nb = Ns // tn
    a_spec = pl.BlockSpec((tm, K), lambda i, j: (i, 0))
    if sharded:
        b_spec = pl.BlockSpec((None, K, tn), lambda i, j: (j // nb, 0, j % nb))
    else:
        b_spec = pl.BlockSpec((K, tn), lambda i, j: (0, j))
    o_spec = pl.BlockSpec((tm, tn), lambda i, j: (i, j))
    outs = [(jax.ShapeDtypeStruct((M, N), dt), o_spec) for dt in out_dtypes]
    extras = extras_fn(tm, tn) if extras_fn is not None else ()
    return _mm(name, [(a, a_spec), (w, b_spec)], [(None, [(0, _whole, 1, _whole)])], grid=(M // tm, N // tn),
               contract=_NN, outs=outs, extras=extras, epilogue=epilogue)


def _mm_nt(name, a, w, out_dtypes, *, tm=1024, tn=1024, extras_fn=None, epilogue=None):
    M, N = a.shape
    sharded = w.ndim == 3
    Ns = w.shape[-1]
    K = w.shape[-2]
    tm, tn = _tile(M, tm, 16), _tile(K, tn, 128)
    a_spec = pl.BlockSpec((tm, N), lambda i, j: (i, 0))
    if sharded:
        S = w.shape[0]
        assert S * Ns == N
        b_spec = pl.BlockSpec((S, tn, Ns), lambda i, j: (0, j, 0))
        terms = [(0, lambda r, s=s: r[:, s * Ns:(s + 1) * Ns], 1, lambda r, s=s: r[s]) for s in range(S)]
    else:
        b_spec = pl.BlockSpec((tn, N), lambda i, j: (j, 0))
        terms = [(0, _whole, 1, _whole)]
    o_spec = pl.BlockSpec((tm, tn), lambda i, j: (i, j))
    outs = [(jax.ShapeDtypeStruct((M, K), dt), o_spec) for dt in out_dtypes]
    extras = extras_fn(tm, tn) if extras_fn is not None else ()
    return _mm(name, [(a, a_spec), (w, b_spec)], [(None, terms)], grid=(M // tm, K // tn), contract=_NT,
               outs=outs, extras=extras, epilogue=epilogue)


def _mm_tn(name, a, b, *, shards=None, tm=1024, tn=1024, b_block=None, b_resident=False):
    M, K = a.shape
    halves = b.shape[0] if b.ndim == 3 else 1
    Nh = b.shape[-1]
    N = Nh * halves
    Ns = N // shards if shards else N
    tm, tn = _tile(K, tm, 128), _tile(math.gcd(Ns, Nh), tn, 128)
    nb, nbh = Ns // tn, Nh // tn
    ij = (lambda g0, g1: (g1, g0)) if b_resident else (lambda g0, g1: (g0, g1))
    bmap = b_block if b_block is not None else (lambda j: j)
    a_spec = pl.BlockSpec((M, tm), lambda g0, g1: (0, ij(g0, g1)[0]))
    if halves > 1:
        b_spec = pl.BlockSpec((None, M, tn), lambda g0, g1: (bmap(ij(g0, g1)[1]) // nbh, 0, bmap(ij(g0, g1)[1]) % nbh))
    else:
        b_spec = pl.BlockSpec((M, tn), lambda g0, g1: (0, bmap(ij(g0, g1)[1])))
    if shards:
        out = (jax.ShapeDtypeStruct((shards, K, Ns), f32),
               pl.BlockSpec((None, tm, tn), lambda g0, g1: (ij(g0, g1)[1] // nb, ij(g0, g1)[0], ij(g0, g1)[1] % nb)))
    else:
        out = (jax.ShapeDtypeStruct((K, N), f32), pl.BlockSpec((tm, tn), lambda g0, g1: ij(g0, g1)))
    grid = (N // tn, K // tm) if b_resident else (K // tm, N // tn)
    return _mm(name, [(a, a_spec), (b, b_spec)], [(None, [(0, _whole, 1, _whole)])], grid=grid, contract=_TN,
               outs=[out])[0]


def _in_proj_dx(cfg, dproj, w_in):
    L, D, SW, CW = cfg.L, cfg.D, cfg.SW, cfg.CW
    NP = SW + 3 * CW + 2 * D
    Ns = NP // _NCHIP
    assert SW + CW == Ns and 2 * CW == Ns and D == Ns
    tm, tn = _tile(L, 1024, 16), _tile(D, 1024, 128)
    a_spec = pl.BlockSpec((tm, NP // 2), lambda i, j, k: (i, k))
    b_spec = pl.BlockSpec((2, tn, Ns), lambda i, j, k: (k, j, 0))
    first = [(0, lambda r: r[:, 0:CW], 1, lambda r: r[0, :, SW:SW + CW]),
             (0, lambda r: r[:, CW:3 * CW], 1, lambda r: r[1]),
             (0, lambda r: r[:, 3 * CW:3 * CW + SW], 1, lambda r: r[0, :, 0:SW])]
    second = [(0, lambda r: r[:, 0:D], 1, lambda r: r[0]), (0, lambda r: r[:, D:2 * D], 1, lambda r: r[1])]
    out = (jax.ShapeDtypeStruct((L, D), bf16), pl.BlockSpec((tm, tn), lambda i, j, k: (i, j)))
    return _mm("in_proj_dx", [(dproj, a_spec), (w_in, b_spec)], [(0, first), (1, second)], grid=(L // tm, D // tn, 2),
               contract=_NT, outs=[out], acc_shape=(tm, tn))[0]


def _out_proj_dx(cfg, dh1b, w_o, ya, yb, proj):
    L, D = cfg.L, cfg.D
    NP = cfg.SW + 3 * cfg.CW + 2 * D
    assert NP == 4 * D
    tm = _tile(L, 256, 16)

    def epilogue(acc, e, o):
        sa = jax.nn.sigmoid(e[2][:, 0:D].astype(f32))
        sb = jax.nn.sigmoid(e[2][:, D:2 * D].astype(f32))
        o[0][...] = (acc * sa).astype(bf16)
        o[1][...] = (acc * sb).astype(bf16)
        o[2][:, 0:D] = (acc * e[0][...].astype(f32) * sa * (1.0 - sa)).astype(bf16)
        o[2][:, D:2 * D] = (acc * e[1][...].astype(f32) * sb * (1.0 - sb)).astype(bf16)

    row = pl.BlockSpec((tm, D), lambda i, j: (i, 0))
    half = pl.BlockSpec((tm, 2 * D), lambda i, j: (i, 1))
    return _mm("out_proj_dx", [(dh1b, row), (w_o, pl.BlockSpec((D, D), lambda i, j: (0, 0)))],
               [(None, [(0, _whole, 1, _whole)])], grid=(L // tm, 1), contract=_NT,
               outs=[(jax.ShapeDtypeStruct((L, D), bf16), row), (jax.ShapeDtypeStruct((L, D), bf16), row),
                     (jax.ShapeDtypeStruct((L, NP), bf16), half)],
               extras=[(ya, row), (yb, row), (proj, half)], epilogue=epilogue)


def _ffn_up_dx(cfg, dhh, w_up):
    L, D, F = cfg.L, cfg.D, cfg.F
    Fh = F // 2
    tm, tn = _tile(L, 1024, 16), _tile(D, 512, 128)
    a_spec = pl.BlockSpec((None, tm, F), lambda i, j, k: (k, i, 0))
    b_spec = pl.BlockSpec((2, tn, Fh), lambda i, j, k: (k, j, 0))
    terms = [(0, lambda r: r[:, 0:Fh], 1, lambda r: r[0]), (0, lambda r: r[:, Fh:F], 1, lambda r: r[1])]
    out = (jax.ShapeDtypeStruct((L, D), bf16), pl.BlockSpec((tm, tn), lambda i, j, k: (i, j)))
    return _mm("ffn_up_dx", [(dhh, a_spec), (w_up, b_spec)], [(0, terms), (1, terms)], grid=(L // tm, D // tn, 2),
               contract=_NT, outs=[out], acc_shape=(tm, tn))[0]


def _rms_fwd(name, x, g):
    L, D = x.shape
    tm = _tile(L, 256, 16)

    def body(x_ref, g_ref, xn_ref, r_ref):
        xv = x_ref[...]
        r = lax.rsqrt(jnp.mean(xv * xv, axis=-1, keepdims=True) + _EPS)
        xn_ref[...] = (xv * r * g_ref[...]).astype(bf16)
        r_ref[...] = r

    return pl.pallas_call(
        body, grid=(L // tm,),
        in_specs=[pl.BlockSpec((tm, D), lambda i: (i, 0)), pl.BlockSpec((1, D), lambda i: (0, 0))],
        out_specs=[pl.BlockSpec((tm, D), lambda i: (i, 0)), pl.BlockSpec((tm, 1), lambda i: (i, 0))],
        out_shape=[jax.ShapeDtypeStruct((L, D), bf16), jax.ShapeDtypeStruct((L, 1), f32)],
        name=name, compiler_params=_cparams(("parallel",)))(x, g)


def _rms_bwd(name, dxn, h, r, g, dres):
    L, D = h.shape
    tm = _tile(L, 256, 16)

    def body(dxn_ref, h_ref, r_ref, g_ref, dres_ref, dh_ref, dhb_ref, dg_ref):
        i = pl.program_id(0)
        d = dxn_ref[...].astype(f32)
        hv = h_ref[...]
        rv = r_ref[...]
        dyg = d * g_ref[...]
        m = jnp.mean(dyg * hv, axis=-1, keepdims=True)
        dh = dres_ref[...] + rv * dyg - hv * (rv * rv * rv) * m
        dh_ref[...] = dh
        dhb_ref[...] = dh.astype(bf16)

        @pl.when(i == 0)
        def _():
            dg_ref[...] = jnp.zeros_like(dg_ref)

        dg_ref[...] += jnp.sum(d * hv * rv, axis=0, keepdims=True)

    row = lambda i: (i, 0)
    return pl.pallas_call(
        body, grid=(L // tm,),
        in_specs=[pl.BlockSpec((tm, D), row), pl.BlockSpec((tm, D), row), pl.BlockSpec((tm, 1), row),
                  pl.BlockSpec((1, D), lambda i: (0, 0)), pl.BlockSpec((tm, D), row)],
        out_specs=[pl.BlockSpec((tm, D), row), pl.BlockSpec((tm, D), row), pl.BlockSpec((1, D), lambda i: (0, 0))],
        out_shape=[jax.ShapeDtypeStruct((L, D), f32), jax.ShapeDtypeStruct((L, D), bf16), jax.ShapeDtypeStruct((1, D), f32)],
        name=name, compiler_params=_cparams(("arbitrary",)))(dxn, h, r, g, dres)


def _loss_head(name, h2, tgt, g):
    L, D = h2.shape
    tm = _tile(L, 256, 16)

    def body(h_ref, t_ref, g_ref, dh_ref, dhb_ref, dg_ref, loss_ref):
        i = pl.program_id(0)
        hv = h_ref[...]
        gv = g_ref[...]
        r = lax.rsqrt(jnp.mean(hv * hv, axis=-1, keepdims=True) + _EPS)
        err = hv * r * gv - t_ref[...]
        dy = err * (1.0 / D)
        dyg = dy * gv
        m = jnp.mean(dyg * hv, axis=-1, keepdims=True)
        dh = r * dyg - hv * (r * r * r) * m
        dh_ref[...] = dh
        dhb_ref[...] = dh.astype(bf16)

        @pl.when(i == 0)
        def _():
            dg_ref[...] = jnp.zeros_like(dg_ref)
            loss_ref[...] = jnp.zeros_like(loss_ref)

        dg_ref[...] += jnp.sum(dy * hv * r, axis=0, keepdims=True)
        part = jnp.sum(jnp.sum(err * err, axis=-1, keepdims=True), axis=0, keepdims=True) * (0.5 / D)
        loss_ref[...] += jnp.broadcast_to(part, (8, 128))

    row = lambda i: (i, 0)
    return pl.pallas_call(
        body, grid=(L // tm,),
        in_specs=[pl.BlockSpec((tm, D), row), pl.BlockSpec((tm, D), row), pl.BlockSpec((1, D), lambda i: (0, 0))],
        out_specs=[pl.BlockSpec((tm, D), row), pl.BlockSpec((tm, D), row), pl.BlockSpec((1, D), lambda i: (0, 0)),
                   pl.BlockSpec((8, 128), lambda i: (0, 0))],
        out_shape=[jax.ShapeDtypeStruct((L, D), f32), jax.ShapeDtypeStruct((L, D), bf16),
                   jax.ShapeDtypeStruct((1, D), f32), jax.ShapeDtypeStruct((8, 128), f32)],
        name=name, compiler_params=_cparams(("arbitrary",)))(h2, tgt, g)


def _shift_down(tile, halo, k, rows8):
    tm = tile.shape[0]
    r = pltpu.roll(tile, k, axis=0)
    hh = pltpu.roll(halo, k, axis=0)
    top = jnp.where(rows8 < k, hh, r[:8])
    return jnp.concatenate([top, r[8:]], axis=0) if tm > 8 else top


def _shift_up(tile, halo, k, rows8):
    tm = tile.shape[0]
    r = pltpu.roll(tile, tm - k, axis=0)
    hh = pltpu.roll(halo, 8 - k, axis=0)
    bot = jnp.where(rows8 >= 8 - k, hh, r[tm - 8:])
    return jnp.concatenate([r[:tm - 8], bot], axis=0) if tm > 8 else bot


def _conv3(x, halo, w_ref, b_ref, rows8):
    return (w_ref[0:1, :] * _shift_down(x, halo, 2, rows8) + w_ref[1:2, :] * _shift_down(x, halo, 1, rows8)
            + w_ref[2:3, :] * x + b_ref[...])


def _convb_fwd(cfg, proj, w, b):
    L, CW = cfg.L, cfg.CW
    assert cfg.SW == CW
    tm = _tile(L, 512, 16)

    def body(v_ref, vh_ref, gb_ref, gc_ref, gch_ref, w_ref, b_ref, o_ref):
        i = pl.program_id(0)
        rows8 = lax.broadcasted_iota(jnp.int32, (8, CW), 0)
        cv = gc_ref[...].astype(f32) * v_ref[...].astype(f32)
        cvh = gch_ref[...].astype(f32)[8:] * vh_ref[...].astype(f32)[8:]
        cvh = jnp.where(i == 0, 0.0, cvh)
        cc = _conv3(cv, cvh, w_ref, b_ref, rows8)
        o_ref[...] = (gb_ref[...].astype(f32) * cc).astype(bf16)

    blk = lambda col: pl.BlockSpec((tm, CW), lambda i: (i, col))
    halo = lambda col: pl.BlockSpec((16, CW), lambda i: (jnp.maximum(i * (tm // 16) - 1, 0), col))
    return pl.pallas_call(
        body, grid=(L // tm,),
        in_specs=[blk(1), halo(1), blk(2), blk(3), halo(3),
                  pl.BlockSpec((3, CW), lambda i: (0, 0)), pl.BlockSpec((1, CW), lambda i: (0, 0))],
        out_specs=pl.BlockSpec((tm, CW), lambda i: (i, 0)),
        out_shape=jax.ShapeDtypeStruct((L, CW), bf16),
        name="convb_fwd", compiler_params=_cparams(("parallel",)))(proj, proj, proj, proj, proj, w, b)


def _convb_bwd(cfg, proj, dyb0, w, b, dproj):
    L, CW = cfg.L, cfg.CW
    tm = _tile(L, 512, 16)
    nt = L // tm

    def body(v_ref, vh_ref, gb_ref, gbn_ref, gc_ref, gch_ref, d_ref, dn_ref, w_ref, b_ref, dproj_ref,
             o_ref, dw_ref, db_ref):
        i = pl.program_id(0)
        rows8 = lax.broadcasted_iota(jnp.int32, (8, CW), 0)
        v = v_ref[...].astype(f32)
        gb = gb_ref[...].astype(f32)
        gc = gc_ref[...].astype(f32)
        d = d_ref[...].astype(f32)
        cv = gc * v
        cvh = gch_ref[...].astype(f32)[8:] * vh_ref[...].astype(f32)[8:]
        cvh = jnp.where(i == 0, 0.0, cvh)
        s2 = _shift_down(cv, cvh, 2, rows8)
        s1 = _shift_down(cv, cvh, 1, rows8)
        cc = w_ref[0:1, :] * s2 + w_ref[1:2, :] * s1 + w_ref[2:3, :] * cv + b_ref[...]
        dcc = d * gb
        dccn = dn_ref[...].astype(f32)[:8] * gbn_ref[...].astype(f32)[:8]
        dccn = jnp.where(i == nt - 1, 0.0, dccn)
        dcv = (w_ref[2:3, :] * dcc + w_ref[1:2, :] * _shift_up(dcc, dccn, 1, rows8)
               + w_ref[0:1, :] * _shift_up(dcc, dccn, 2, rows8))
        o_ref[:, 0:CW] = (dcv * gc).astype(bf16)
        o_ref[:, CW:2 * CW] = (d * cc).astype(bf16)
        o_ref[:, 2 * CW:3 * CW] = (dcv * v).astype(bf16)

        @pl.when(i == 0)
        def _():
            dw_ref[...] = jnp.zeros_like(dw_ref)
            db_ref[...] = jnp.zeros_like(db_ref)

        dw_ref[0:1, :] += jnp.sum(dcc * s2, axis=0, keepdims=True)
        dw_ref[1:2, :] += jnp.sum(dcc * s1, axis=0, keepdims=True)
        dw_ref[2:3, :] += jnp.sum(dcc * cv, axis=0, keepdims=True)
        db_ref[...] += jnp.sum(dcc, axis=0, keepdims=True)

    blk = lambda col: pl.BlockSpec((tm, CW), lambda i: (i, col))
    prev = lambda col: pl.BlockSpec((16, CW), lambda i: (jnp.maximum(i * (tm // 16) - 1, 0), col))
    nxt = lambda col: pl.BlockSpec((16, CW), lambda i: (jnp.minimum((i + 1) * (tm // 16), L // 16 - 1), col))
    const = lambda r: pl.BlockSpec((r, CW), lambda i: (0, 0))
    return pl.pallas_call(
        body, grid=(nt,),
        in_specs=[blk(1), prev(1), blk(2), nxt(2), blk(3), prev(3), blk(0), nxt(0), const(3), const(1),
                  pl.BlockSpec(memory_space=pl.ANY)],
        out_specs=[pl.BlockSpec((tm, 3 * CW), lambda i: (i, 0)), const(3), const(1)],
        out_shape=[jax.ShapeDtypeStruct(dproj.shape, bf16), jax.ShapeDtypeStruct((3, CW), f32),
                   jax.ShapeDtypeStruct((1, CW), f32)],
        input_output_aliases={10: 0},
        name="convb_bwd", compiler_params=_cparams(("arbitrary",)))(proj, proj, proj, proj, proj, proj, dyb0, dyb0, w, b,
                                                                    dproj)


def _ffn_act(cfg, hh, w, b):
    L, F = cfg.L, cfg.F
    tm = _tile(L, 512, 16)
    tc = _tile(F, 1408, 128)
    ncb = F // tc

    def body(a_ref, ah_ref, g_ref, w_ref, b_ref, o_ref):
        i = pl.program_id(0)
        rows8 = lax.broadcasted_iota(jnp.int32, (8, tc), 0)
        a = a_ref[...].astype(f32)
        ah = jnp.where(i == 0, 0.0, ah_ref[...].astype(f32)[8:])
        o_ref[...] = (_gelu(_conv3(a, ah, w_ref, b_ref, rows8)) * g_ref[...].astype(f32)).astype(bf16)

    return pl.pallas_call(
        body, grid=(L // tm, ncb),
        in_specs=[pl.BlockSpec((tm, tc), lambda i, j: (i, j)),
                  pl.BlockSpec((16, tc), lambda i, j: (jnp.maximum(i * (tm // 16) - 1, 0), j)),
                  pl.BlockSpec((tm, tc), lambda i, j: (i, j + ncb)),
                  pl.BlockSpec((3, tc), lambda i, j: (0, j)), pl.BlockSpec((1, tc), lambda i, j: (0, j))],
        out_specs=pl.BlockSpec((tm, tc), lambda i, j: (i, j)),
        out_shape=jax.ShapeDtypeStruct((L, F), bf16),
        name="ffn_act", compiler_params=_cparams(("parallel", "parallel")))(hh, hh, hh, w, b)


def _ffn_act_bwd(cfg, hh, df, w, b):
    L, F = cfg.L, cfg.F
    tm = _tile(L, 512, 16)
    tc = _tile(F, 1408, 128)
    ncb = F // tc
    nt = L // tm

    def body(a_ref, ah_ref, an_ref, g_ref, gn_ref, d_ref, dn_ref, w_ref, b_ref, dhh_ref, dw_ref, db_ref):
        i = pl.program_id(1)
        rows8 = lax.broadcasted_iota(jnp.int32, (8, tc), 0)
        a = a_ref[...].astype(f32)
        ah = jnp.where(i == 0, 0.0, ah_ref[...].astype(f32)[8:])
        s2 = _shift_down(a, ah, 2, rows8)
        s1 = _shift_down(a, ah, 1, rows8)
        act = w_ref[0:1, :] * s2 + w_ref[1:2, :] * s1 + w_ref[2:3, :] * a + b_ref[...]
        d = d_ref[...].astype(f32)
        dhh_ref[1] = (d * _gelu(act)).astype(bf16)
        dact = d * g_ref[...].astype(f32) * _gelu_grad(act)
        an = an_ref[...].astype(f32)[:8]
        actn = _conv3(an, a[tm - 8:], w_ref, b_ref, rows8)
        dactn = dn_ref[...].astype(f32)[:8] * gn_ref[...].astype(f32)[:8] * _gelu_grad(actn)
        dactn = jnp.where(i == nt - 1, 0.0, dactn)
        da = (w_ref[2:3, :] * dact + w_ref[1:2, :] * _shift_up(dact, dactn, 1, rows8)
              + w_ref[0:1, :] * _shift_up(dact, dactn, 2, rows8))
        dhh_ref[0] = da.astype(bf16)

        @pl.when(i == 0)
        def _():
            dw_ref[...] = jnp.zeros_like(dw_ref)
            db_ref[...] = jnp.zeros_like(db_ref)

        dw_ref[0:1, :] += jnp.sum(dact * s2, axis=0, keepdims=True)
        dw_ref[1:2, :] += jnp.sum(dact * s1, axis=0, keepdims=True)
        dw_ref[2:3, :] += jnp.sum(dact * a, axis=0, keepdims=True)
        db_ref[...] += jnp.sum(dact, axis=0, keepdims=True)

    blk = lambda off: pl.BlockSpec((tm, tc), lambda j, i: (i, j + off))
    prev = lambda off: pl.BlockSpec((16, tc), lambda j, i: (jnp.maximum(i * (tm // 16) - 1, 0), j + off))
    nxt = lambda off: pl.BlockSpec((16, tc), lambda j, i: (jnp.minimum((i + 1) * (tm // 16), L // 16 - 1), j + off))
    const = lambda r: pl.BlockSpec((r, tc), lambda j, i: (0, j))
    return pl.pallas_call(
        body, grid=(ncb, nt),
        in_specs=[blk(0), prev(0), nxt(0), blk(ncb), nxt(ncb), blk(0), nxt(0), const(3), const(1)],
        out_specs=[pl.BlockSpec((2, tm, tc), lambda j, i: (0, i, j)), const(3), const(1)],
        out_shape=[jax.ShapeDtypeStruct((2, L, F), bf16),
                   jax.ShapeDtypeStruct((3, F), f32), jax.ShapeDtypeStruct((1, F), f32)],
        name="ffn_act_bwd", compiler_params=_cparams(("parallel", "arbitrary")))(hh, hh, hh, hh, hh, df, df, w, b)


def _merge_fwd(cfg, ya1, yb0, proj, wso, wco):
    L, D, SW, CW = cfg.L, cfg.D, cfg.SW, cfg.CW
    Ns = D // _NCHIP
    tm = _tile(L, 1024, 16)
    tn = _tile(Ns, 512, 128)
    nb = Ns // tn
    off_a = (SW + 3 * CW) // tn
    off_b = (SW + 3 * CW + D) // tn

    def body(a_ref, b_ref, wa_ref, wb_ref, ma_ref, mb_ref, m_ref, ya_ref, yb_ref):
        ya = jnp.dot(a_ref[...], wa_ref[...], preferred_element_type=f32)
        yb = jnp.dot(b_ref[...], wb_ref[...], preferred_element_type=f32)
        sa = jax.nn.sigmoid(ma_ref[...].astype(f32))
        sb = jax.nn.sigmoid(mb_ref[...].astype(f32))
        m_ref[...] = (sa * ya + sb * yb).astype(bf16)
        ya_ref[...] = ya.astype(bf16)
        yb_ref[...] = yb.astype(bf16)

    o_spec = pl.BlockSpec((tm, tn), lambda i, j: (i, j))
    o_shape = jax.ShapeDtypeStruct((L, D), bf16)
    return pl.pallas_call(
        body, grid=(L // tm, D // tn),
        in_specs=[pl.BlockSpec((tm, SW), lambda i, j: (i, 0)), pl.BlockSpec((tm, CW), lambda i, j: (i, 0)),
                  pl.BlockSpec((None, SW, tn), lambda i, j: (j // nb, 0, j % nb)),
                  pl.BlockSpec((None, CW, tn), lambda i, j: (j // nb, 0, j % nb)),
                  pl.BlockSpec((tm, tn), lambda i, j: (i, off_a + j)), pl.BlockSpec((tm, tn), lambda i, j: (i, off_b + j))],
        out_specs=[o_spec, o_spec, o_spec], out_shape=[o_shape, o_shape, o_shape],
        name="merge_fwd", compiler_params=_cparams(("parallel", "parallel")))(ya1, yb0, wso, wco, proj, proj)


def _s5_dims(cfg):
    G = cfg.SW // _SSM_GROUP
    NS = G // _SLAB_GROUPS
    SC = _SLAB_GROUPS * _SSM_GROUP
    SH = _SLAB_GROUPS * _SSM_STATE
    NST = 2 * SH * NS
    return G, NS, SC, SH, NST


def _lane_slabs(cfg, W):
    _, NS, _, SH, _ = _s5_dims(cfg)
    return [(2 * SH * s + w0, 2 * SH * s + SH + w0) for s in range(NS) for w0 in range(0, SH, W)]


def _discretize(a_re, a_im, log_dt, b_re, b_im):
    dt = jnp.exp(log_dt)[:, None]
    mag = jnp.exp(dt * a_re)
    abr = mag * jnp.cos(dt * a_im)
    abi = mag * jnp.sin(dt * a_im)
    nr = abr - 1.0
    ni = abi
    den = a_re * a_re + a_im * a_im
    fr = (nr * a_re + ni * a_im) / den
    fi = (ni * a_re - nr * a_im) / den
    bbr = fr[..., None] * b_re - fi[..., None] * b_im
    bbi = fr[..., None] * b_im + fi[..., None] * b_re
    return abr, abi, bbr, bbi


def _state_rows(cfg, re, im):
    _, NS, _, SH, _ = _s5_dims(cfg)
    return jnp.concatenate([re.reshape(NS, SH), im.reshape(NS, SH)], axis=1).reshape(-1)


def _s5_tables(cfg, abr, abi, bbr, bbi, c_re, c_im):
    G, NS, SC, SH, NST = _s5_dims(cfg)
    S = cfg.T // 8
    eye = jnp.eye(_SLAB_GROUPS, dtype=f32)
    bb = jnp.stack([bbr, bbi]).reshape(2, NS, _SLAB_GROUPS, _SSM_STATE, _SSM_GROUP)
    bs = jnp.einsum("rsgph,gq->sghrqp", bb, eye).reshape(NS, SC, 2 * SH).astype(bf16)
    cc = jnp.stack([c_re, -c_im]).reshape(2, NS, _SLAB_GROUPS, _SSM_GROUP, _SSM_STATE)
    cs = jnp.einsum("rsghp,gq->srqpgh", cc, eye).reshape(NS, 2 * SH, SC).astype(bf16)
    arep = jnp.broadcast_to(_state_rows(cfg, abr, abi)[None, :], (8, NST))
    pr, pi = abr, abi
    for _ in range(S - 1):
        pr, pi = pr * abr - pi * abi, pr * abi + pi * abr
    apow = jnp.broadcast_to(_state_rows(cfg, pr, pi)[None, :], (8, NST))
    t = jnp.arange(cfg.T)
    perm = (t % 8) * S + t // 8
    pm = jax.nn.one_hot(perm, cfg.T, dtype=bf16)
    return bs, cs, arep, apow, pm, pm.T


def _cmul_add(ar, ai, xr, xi, br, bi):
    return ar * xr - ai * xi + br, ar * xi + ai * xr + bi


def _s5_forward_chunk(cfg, W, upb, bs_ref, arep_ref, apow_ref, st, x0, cin_store):
    _, NS, SC, SH, _ = _s5_dims(cfg)
    S = cfg.T // 8
    for s in range(NS):
        st[:, 2 * SH * s:2 * SH * (s + 1)] = jnp.dot(upb[:, SC * s:SC * (s + 1)], bs_ref[s], preferred_element_type=f32)
    rows = lax.broadcasted_iota(jnp.int32, (8, W), 0)
    zero = jnp.zeros((8, W), f32)
    for rc, ic in _lane_slabs(cfg, W):
        ar = arep_ref[:, rc:rc + W]
        ai = arep_ref[:, ic:ic + W]

        def step(i, carry, rc=rc, ic=ic, ar=ar, ai=ai):
            xr, xi = carry
            r0 = pl.multiple_of(i * 8, 8)
            nr, ni = _cmul_add(ar, ai, xr, xi, st[pl.ds(r0, 8), rc:rc + W], st[pl.ds(r0, 8), ic:ic + W])
            st[pl.ds(r0, 8), rc:rc + W] = nr
            st[pl.ds(r0, 8), ic:ic + W] = ni
            return nr, ni

        er, ei = lax.fori_loop(0, S, step, (zero, zero))
        pr = apow_ref[:, rc:rc + W]
        pi = apow_ref[:, ic:ic + W]
        x0r = x0[:, rc:rc + W]
        x0i = x0[:, ic:ic + W]
        cr = jnp.where(rows == 0, x0r, 0.0)
        ci = jnp.where(rows == 0, x0i, 0.0)
        for _ in range(7):
            fr, fi = _cmul_add(pr, pi, cr, ci, er, ei)
            cr = jnp.where(rows == 0, x0r, pltpu.roll(fr, 1, axis=0))
            ci = jnp.where(rows == 0, x0i, pltpu.roll(fi, 1, axis=0))
        fr, fi = _cmul_add(pr, pi, cr, ci, er, ei)
        x0[:, rc:rc + W] = jnp.broadcast_to(fr[7:8, :], (8, W))
        x0[:, ic:ic + W] = jnp.broadcast_to(fi[7:8, :], (8, W))
        cin_store(rc, ic, cr, ci)

        def fix(i, carry, rc=rc, ic=ic, ar=ar, ai=ai):
            kr, ki = carry
            r0 = pl.multiple_of(i * 8, 8)
            nr, ni = ar * kr - ai * ki, ar * ki + ai * kr
            st[pl.ds(r0, 8), rc:rc + W] = st[pl.ds(r0, 8), rc:rc + W] + nr
            st[pl.ds(r0, 8), ic:ic + W] = st[pl.ds(r0, 8), ic:ic + W] + ni
            return nr, ni

        lax.fori_loop(0, S, fix, (cr, ci))


def _s5_fwd(cfg, proj, tabs, dskip):
    L, SW, T = cfg.L, cfg.SW, cfg.T
    G, NS, SC, SH, NST = _s5_dims(cfg)
    bs, cs, arep, apow, pm, pmt = tabs
    W = min(512, SH)
    NC = L // T

    def body(u_ref, pm_ref, pmt_ref, bs_ref, cs_ref, arep_ref, apow_ref, dskip_ref, y_ref, ya0_ref, cin_ref, st, x0):
        c = pl.program_id(0)

        @pl.when(c == 0)
        def _():
            x0[...] = jnp.zeros_like(x0)

        up = jnp.dot(pm_ref[...], u_ref[...], preferred_element_type=f32)
        upb = up.astype(bf16)

        def cin_store(rc, ic, cr, ci):
            cin_ref[0, :, rc:rc + W] = cr
            cin_ref[0, :, ic:ic + W] = ci

        _s5_forward_chunk(cfg, W, upb, bs_ref, arep_ref, apow_ref, st, x0, cin_store)
        yp = jnp.concatenate(
            [jnp.dot(st[:, 2 * SH * s:2 * SH * (s + 1)].astype(bf16), cs_ref[s], preferred_element_type=f32)
             for s in range(NS)], axis=1) + dskip_ref[...] * up
        y = jnp.dot(pmt_ref[...], yp.astype(bf16), preferred_element_type=f32)
        y_ref[...] = y.astype(bf16)
        ya0_ref[...] = _gelu(y).astype(bf16)

    const2 = lambda shape: pl.BlockSpec(shape, lambda c: (0, 0))
    const3 = lambda shape: pl.BlockSpec(shape, lambda c: (0, 0, 0))
    return pl.pallas_call(
        body, grid=(NC,),
        in_specs=[pl.BlockSpec((T, SW), lambda c: (c, 0)), const2((T, T)), const2((T, T)), const3((NS, SC, 2 * SH)),
                  const3((NS, 2 * SH, SC)), const2((8, NST)), const2((8, NST)), const2((1, SW))],
        out_specs=[pl.BlockSpec((T, SW), lambda c: (c, 0)), pl.BlockSpec((T, SW), lambda c: (c, 0)),
                   pl.BlockSpec((1, 8, NST), lambda c: (c, 0, 0))],
        out_shape=[jax.ShapeDtypeStruct((L, SW), bf16), jax.ShapeDtypeStruct((L, SW), bf16),
                   jax.ShapeDtypeStruct((NC, 8, NST), f32)],
        scratch_shapes=[pltpu.VMEM((T, NST), f32), pltpu.VMEM((8, NST), f32)],
        name="s5_fwd", compiler_params=_cparams(("arbitrary",)))(proj, pm, pmt, bs, cs, arep, apow, dskip)


def _s5_bwd(cfg, proj, dy, cin, tabs, dskip, dproj):
    L, SW, T = cfg.L, cfg.SW, cfg.T
    du_col = 3 * cfg.CW // SW
    G, NS, SC, SH, NST = _s5_dims(cfg)
    bs, cs, arep, apow, pm, pmt = tabs
    W = min(512, SH)
    S = T // 8
    NC = L // T

    def body(u_ref, dy_ref, cin_ref, pm_ref, pmt_ref, bs_ref, cs_ref, arep_ref, apow_ref, dskip_ref, dproj_ref,
             du_ref, da_ref, db_hbm, dc_hbm, dd_ref, st, gs, x0, g0, db_acc, dc_acc, sem):
        c = pl.program_id(0)

        @pl.when(c == 0)
        def _():
            g0[...] = jnp.zeros_like(g0)
            da_ref[...] = jnp.zeros_like(da_ref)
            dd_ref[...] = jnp.zeros_like(dd_ref)
            db_acc[...] = jnp.zeros_like(db_acc)
            dc_acc[...] = jnp.zeros_like(dc_acc)

        up = jnp.dot(pm_ref[...], u_ref[...], preferred_element_type=f32)
        upb = up.astype(bf16)
        dyp = jnp.dot(pm_ref[...], dy_ref[...], preferred_element_type=f32)
        dypb = dyp.astype(bf16)
        x0[...] = jnp.zeros_like(x0)
        for rc, ic in _lane_slabs(cfg, W):
            x0[:, rc:rc + W] = jnp.broadcast_to(cin_ref[0, 0:1, rc:rc + W], (8, W))
            x0[:, ic:ic + W] = jnp.broadcast_to(cin_ref[0, 0:1, ic:ic + W], (8, W))
        _s5_forward_chunk(cfg, W, upb, bs_ref, arep_ref, apow_ref, st, x0, lambda *a: None)
        for s in range(NS):
            gs[:, 2 * SH * s:2 * SH * (s + 1)] = lax.dot_general(
                dypb[:, SC * s:SC * (s + 1)], cs_ref[s], (((1,), (1,)), ((), ())), preferred_element_type=f32)
        rows = lax.broadcasted_iota(jnp.int32, (8, W), 0)
        zero = jnp.zeros((8, W), f32)
        for rc, ic in _lane_slabs(cfg, W):
            ar = arep_ref[:, rc:rc + W]
            ai = arep_ref[:, ic:ic + W]

            def rstep(k, carry, rc=rc, ic=ic, ar=ar, ai=ai):
                gr, gi = carry
                r0 = pl.multiple_of((S - 1 - k) * 8, 8)
                nr = ar * gr + ai * gi + gs[pl.ds(r0, 8), rc:rc + W]
                ni = ar * gi - ai * gr + gs[pl.ds(r0, 8), ic:ic + W]
                gs[pl.ds(r0, 8), rc:rc + W] = nr
                gs[pl.ds(r0, 8), ic:ic + W] = ni
                return nr, ni

            er, ei = lax.fori_loop(0, S, rstep, (zero, zero))
            pr = apow_ref[:, rc:rc + W]
            pi = apow_ref[:, ic:ic + W]
            g0r = g0[:, rc:rc + W]
            g0i = g0[:, ic:ic + W]
            cr = jnp.where(rows == 7, g0r, 0.0)
            ci = jnp.where(rows == 7, g0i, 0.0)
            for _ in range(7):
                fr = er + pr * cr + pi * ci
                fi = ei + pr * ci - pi * cr
                cr = jnp.where(rows == 7, g0r, pltpu.roll(fr, 7, axis=0))
                ci = jnp.where(rows == 7, g0i, pltpu.roll(fi, 7, axis=0))
            fr = er + pr * cr + pi * ci
            fi = ei + pr * ci - pi * cr
            g0[:, rc:rc + W] = jnp.broadcast_to(fr[0:1, :], (8, W))
            g0[:, ic:ic + W] = jnp.broadcast_to(fi[0:1, :], (8, W))

            def fix(k, carry, rc=rc, ic=ic, ar=ar, ai=ai):
                kr, ki, accr, acci = carry
                i = S - 1 - k
                r0 = pl.multiple_of(i * 8, 8)
                rp = pl.multiple_of((i - 1) * 8, 8)
                nr = ar * kr + ai * ki
                ni = ar * ki - ai * kr
                gr = gs[pl.ds(r0, 8), rc:rc + W] + nr
                gi = gs[pl.ds(r0, 8), ic:ic + W] + ni
                gs[pl.ds(r0, 8), rc:rc + W] = gr
                gs[pl.ds(r0, 8), ic:ic + W] = gi
                xr = st[pl.ds(rp, 8), rc:rc + W]
                xi = st[pl.ds(rp, 8), ic:ic + W]
                return nr, ni, accr + gr * xr + gi * xi, acci + gi * xr - gr * xi

            kr, ki, accr, acci = lax.fori_loop(0, S - 1, fix, (cr, ci, zero, zero))
            nr = ar * kr + ai * ki
            ni = ar * ki - ai * kr
            gr = gs[0:8, rc:rc + W] + nr
            gi = gs[0:8, ic:ic + W] + ni
            gs[0:8, rc:rc + W] = gr
            gs[0:8, ic:ic + W] = gi
            xr = cin_ref[0, :, rc:rc + W]
            xi = cin_ref[0, :, ic:ic + W]
            da_ref[:, rc:rc + W] += accr + gr * xr + gi * xi
            da_ref[:, ic:ic + W] += acci + gi * xr - gr * xi

        dups = []
        for s in range(NS):
            gsb = gs[:, 2 * SH * s:2 * SH * (s + 1)].astype(bf16)
            dups.append(lax.dot_general(gsb, bs_ref[s], (((1,), (1,)), ((), ())), preferred_element_type=f32))
            db_acc[s] += lax.dot_general(upb[:, SC * s:SC * (s + 1)], gsb, (((0,), (0,)), ((), ())),
                                         preferred_element_type=f32)
            dc_acc[s] += lax.dot_general(st[:, 2 * SH * s:2 * SH * (s + 1)].astype(bf16), dypb[:, SC * s:SC * (s + 1)],
                                         (((0,), (0,)), ((), ())), preferred_element_type=f32)
        dup = jnp.concatenate(dups, axis=1) + dskip_ref[...] * dyp
        du_ref[...] = jnp.dot(pmt_ref[...], dup.astype(bf16), preferred_element_type=f32).astype(bf16)
        dd_ref[...] += jnp.sum(dyp * up, axis=0, keepdims=True)

        @pl.when(c == NC - 1)
        def _():
            cp1 = pltpu.make_async_copy(db_acc, db_hbm, sem.at[0])
            cp2 = pltpu.make_async_copy(dc_acc, dc_hbm, sem.at[1])
            cp1.start()
            cp2.start()
            cp1.wait()
            cp2.wait()

    rev = lambda c: (NC - 1 - c, 0)
    const2 = lambda shape: pl.BlockSpec(shape, lambda c: (0, 0))
    const3 = lambda shape: pl.BlockSpec(shape, lambda c: (0, 0, 0))
    return pl.pallas_call(
        body, grid=(NC,),
        in_specs=[pl.BlockSpec((T, SW), rev), pl.BlockSpec((T, SW), rev), pl.BlockSpec((1, 8, NST), lambda c: (NC - 1 - c, 0, 0)),
                  const2((T, T)), const2((T, T)), const3((NS, SC, 2 * SH)), const3((NS, 2 * SH, SC)),
                  const2((8, NST)), const2((8, NST)), const2((1, SW)), pl.BlockSpec(memory_space=pl.ANY)],
        out_specs=[pl.BlockSpec((T, SW), lambda c: (NC - 1 - c, du_col)), const2((8, NST)),
                   pl.BlockSpec(memory_space=pl.ANY), pl.BlockSpec(memory_space=pl.ANY), const2((1, SW))],
        out_shape=[jax.ShapeDtypeStruct(dproj.shape, bf16), jax.ShapeDtypeStruct((8, NST), f32),
                   jax.ShapeDtypeStruct((NS, SC, 2 * SH), f32), jax.ShapeDtypeStruct((NS, 2 * SH, SC), f32),
                   jax.ShapeDtypeStruct((1, SW), f32)],
        scratch_shapes=[pltpu.VMEM((T, NST), f32), pltpu.VMEM((T, NST), f32), pltpu.VMEM((8, NST), f32),
                        pltpu.VMEM((8, NST), f32), pltpu.VMEM((NS, SC, 2 * SH), f32), pltpu.VMEM((NS, 2 * SH, SC), f32),
                        pltpu.SemaphoreType.DMA((2,))],
        input_output_aliases={10: 0},
        name="s5_bwd", compiler_params=_cparams(("arbitrary",)))(proj, dy, cin, pm, pmt, bs, cs, arep, apow, dskip, dproj)


def _s5_param_grads(cfg, da, db_full, dc_full):
    G, NS, SC, SH, NST = _s5_dims(cfg)
    das = da.sum(axis=0).reshape(NS, 2, SH)
    dabr = das[:, 0].reshape(G, _SSM_STATE)
    dabi = das[:, 1].reshape(G, _SSM_STATE)
    dbf = db_full.reshape(NS, _SLAB_GROUPS, _SSM_GROUP, 2, _SLAB_GROUPS, _SSM_STATE)
    dbb = jnp.einsum("sghrgp->rsgph", dbf).reshape(2, G, _SSM_STATE, _SSM_GROUP)
    dcf = dc_full.reshape(NS, 2, _SLAB_GROUPS, _SSM_STATE, _SLAB_GROUPS, _SSM_GROUP)
    dcc = jnp.einsum("srgpgh->rsghp", dcf).reshape(2, G, _SSM_GROUP, _SSM_STATE)
    return dabr, dabi, dbb[0], dbb[1], dcc[0], -dcc[1]


def _coords():
    return lax.axis_index("x"), lax.axis_index("y"), lax.axis_index("c")


def _other_chips(x, y):
    return [(1 - x, y), (x, 1 - y), (1 - x, 1 - y)]


def _allreduce8(name, v):
    R = v.shape[0]

    def body(v_ref, o_ref, sib, chips, mine, ssem, rsem):
        x, y, c = _coords()
        d2d = pltpu.make_async_remote_copy(src_ref=v_ref, dst_ref=sib, send_sem=ssem.at[0], recv_sem=rsem.at[0],
                                           device_id=(x, y, 1 - c), device_id_type=_MESH)
        d2d.start()
        d2d.wait()
        mine[...] = v_ref[...] + sib[...]
        cps = [pltpu.make_async_remote_copy(src_ref=mine, dst_ref=chips.at[j], send_sem=ssem.at[1 + j],
                                            recv_sem=rsem.at[1 + j], device_id=(*chip, c), device_id_type=_MESH)
               for j, chip in enumerate(_other_chips(x, y))]
        for cp in cps:
            cp.start()
        for cp in cps:
            cp.wait()
        o_ref[...] = (mine[...] + chips[1]) + (chips[0] + chips[2])

    vm = pl.BlockSpec(memory_space=pltpu.VMEM)
    return pl.pallas_call(
        body, in_specs=[vm], out_specs=vm, out_shape=jax.ShapeDtypeStruct((R, 128), f32),
        scratch_shapes=[pltpu.VMEM((R, 128), f32), pltpu.VMEM((3, R, 128), f32), pltpu.VMEM((R, 128), f32),
                        pltpu.SemaphoreType.DMA((4,)), pltpu.SemaphoreType.DMA((4,))],
        name=name, compiler_params=pltpu.CompilerParams(vmem_limit_bytes=_VMEM_LIMIT))(v)


def _cast_into_slot(name, w, k_idx):
    R, C = w.shape
    tr = _tile(R, 256, 16)

    def body(k_ref, w_ref, o_ref):
        o_ref[...] = w_ref[...].astype(bf16)

    gs = pltpu.PrefetchScalarGridSpec(
        num_scalar_prefetch=1, grid=(R // tr,),
        in_specs=[pl.BlockSpec((tr, C), lambda r, kr: (r, 0))],
        out_specs=pl.BlockSpec((None, tr, C), lambda r, kr: (kr[0], r, 0)))
    return pl.pallas_call(body, grid_spec=gs, out_shape=jax.ShapeDtypeStruct((_NCHIP, R, C), bf16), name=name,
                          compiler_params=_cparams(("parallel",)))(k_idx, w)


def _handshake(peers):
    barrier = pltpu.get_barrier_semaphore()
    for peer in peers:
        pl.semaphore_signal(barrier, inc=1, device_id=peer, device_id_type=_MESH)
    pl.semaphore_wait(barrier, len(peers))


def _allgather_weights(name, bufs, collective_id):
    n = len(bufs)
    refs = [jax.new_ref(b, memory_space=pltpu.MemorySpace.HBM) for b in bufs]

    @pl.kernel(mesh=plsc.ScalarSubcoreMesh(axis_name="seq", num_cores=1), name=name,
               scratch_types=(pltpu.SemaphoreType.DMA((n, 3)), pltpu.SemaphoreType.DMA((n, 3)),
                              pltpu.SemaphoreType.DMA((n, 3)), pltpu.SemaphoreType.DMA((n, 3))),
               compiler_params=pltpu.CompilerParams(collective_id=collective_id))
    def launch(ssem, rsem, fssem, frsem):
        x, y, c = _coords()
        k = 2 * x + y
        others = _other_chips(x, y)
        _handshake([(x, y, 1 - c)] + [(*chip, c) for chip in others])
        sends, fwds = [], []
        for w in range(n):
            rh = refs[w].shape[1] // 2
            mine = refs[w].at[k, pl.ds(c * rh, rh)]
            for j, chip in enumerate(others):
                cp = pltpu.make_async_remote_copy(
                    src_ref=mine, dst_ref=mine, send_sem=ssem.at[w, j], recv_sem=rsem.at[w, j],
                    device_id=(*chip, c), device_id_type=_MESH)
                cp.start()
                sends.append(cp)
        for w in range(n):
            rh = refs[w].shape[1] // 2
            for j, (ox, oy) in enumerate(others):
                landed = refs[w].at[2 * ox + oy, pl.ds(c * rh, rh)]
                pltpu.make_async_remote_copy(
                    src_ref=landed, dst_ref=landed, send_sem=ssem.at[w, j], recv_sem=rsem.at[w, j],
                    device_id=(ox, oy, c), device_id_type=_MESH).wait_recv()
                cp = pltpu.make_async_remote_copy(
                    src_ref=landed, dst_ref=landed, send_sem=fssem.at[w, j], recv_sem=frsem.at[w, j],
                    device_id=(x, y, 1 - c), device_id_type=_MESH)
                cp.start()
                fwds.append(cp)
        for w in range(n):
            rh = refs[w].shape[1] // 2
            for j, (ox, oy) in enumerate(others):
                passed = refs[w].at[2 * ox + oy, pl.ds((1 - c) * rh, rh)]
                pltpu.make_async_remote_copy(
                    src_ref=passed, dst_ref=passed, send_sem=fssem.at[w, j], recv_sem=frsem.at[w, j],
                    device_id=(x, y, 1 - c), device_id_type=_MESH).wait_recv()
        for cp in sends + fwds:
            cp.wait_send()

    launch()
    return [r[...] for r in refs]


def _sequencer_kernel(name, collective_id, sems, body):
    pl.kernel(body, mesh=plsc.ScalarSubcoreMesh(axis_name="seq", num_cores=1), name=name, scratch_types=sems,
              compiler_params=pltpu.CompilerParams(collective_id=collective_id))()


def _hbm_ref(a):
    return jax.new_ref(a, memory_space=pltpu.MemorySpace.HBM)


def _exchange_halves(name, grads, collective_id):
    n = len(grads)
    srcs = [_hbm_ref(g) for g in grads]
    dsts = [jax.empty_ref(jax.ShapeDtypeStruct((g.shape[0], g.shape[1] // 2, g.shape[2]), g.dtype),
                          memory_space=pltpu.MemorySpace.HBM) for g in grads]

    def body(ssem, rsem):
        x, y, c = _coords()
        _handshake([(x, y, 1 - c)])
        cps = []
        for w in range(n):
            rh = srcs[w].shape[1] // 2
            cp = pltpu.make_async_remote_copy(
                src_ref=srcs[w].at[:, pl.ds((1 - c) * rh, rh)], dst_ref=dsts[w], send_sem=ssem.at[w], recv_sem=rsem.at[w],
                device_id=(x, y, 1 - c), device_id_type=_MESH)
            cp.start()
            cps.append(cp)
        for cp in cps:
            cp.wait()

    _sequencer_kernel(name, collective_id, (pltpu.SemaphoreType.DMA((n,)), pltpu.SemaphoreType.DMA((n,))), body)
    return [d[...] for d in dsts]


def _scatter_shards(name, parts, collective_id):
    n = len(parts)
    srcs = [_hbm_ref(p) for p in parts]
    dsts = [jax.empty_ref(jax.ShapeDtypeStruct((3,) + p.shape[1:], p.dtype), memory_space=pltpu.MemorySpace.HBM)
            for p in parts]

    def body(ssem, rsem):
        x, y, c = _coords()
        others = _other_chips(x, y)
        _handshake([(*chip, c) for chip in others])
        cps = []
        for w in range(n):
            for j, (ox, oy) in enumerate(others):
                cp = pltpu.make_async_remote_copy(
                    src_ref=srcs[w].at[2 * ox + oy], dst_ref=dsts[w].at[j], send_sem=ssem.at[w, j], recv_sem=rsem.at[w, j],
                    device_id=(ox, oy, c), device_id_type=_MESH)
                cp.start()
                cps.append(cp)
        for cp in cps:
            cp.wait()

    _sequencer_kernel(name, collective_id, (pltpu.SemaphoreType.DMA((n, 3)), pltpu.SemaphoreType.DMA((n, 3))), body)
    return [d[...] for d in dsts]


def _join_halves(name, bufs, collective_id):
    n = len(bufs)
    refs = [_hbm_ref(b) for b in bufs]

    def body(ssem, rsem):
        x, y, c = _coords()
        _handshake([(x, y, 1 - c)])
        cps = []
        for w in range(n):
            rh = refs[w].shape[0] // 2
            mine = refs[w].at[pl.ds(c * rh, rh)]
            cp = pltpu.make_async_remote_copy(src_ref=mine, dst_ref=mine, send_sem=ssem.at[w], recv_sem=rsem.at[w],
                                              device_id=(x, y, 1 - c), device_id_type=_MESH)
            cp.start()
            cps.append(cp)
        for w in range(n):
            rh = refs[w].shape[0] // 2
            theirs = refs[w].at[pl.ds((1 - c) * rh, rh)]
            pltpu.make_async_remote_copy(src_ref=theirs, dst_ref=theirs, send_sem=ssem.at[w], recv_sem=rsem.at[w],
                                         device_id=(x, y, 1 - c), device_id_type=_MESH).wait_recv()
        for cp in cps:
            cp.wait_send()

    _sequencer_kernel(name, collective_id, (pltpu.SemaphoreType.DMA((n,)), pltpu.SemaphoreType.DMA((n,))), body)
    return [r[...] for r in refs]


def _add_own_half(name, g, t, c_idx, after):
    S, R, C = g.shape
    rh = R // 2
    tr = _tile(rh, 256, 16)
    nrb = rh // tr

    def body(c_ref, g_ref, t_ref, after_ref, o_ref):
        o_ref[...] = (g_ref[...] + t_ref[...]).astype(bf16)

    gs = pltpu.PrefetchScalarGridSpec(
        num_scalar_prefetch=1, grid=(S, nrb),
        in_specs=[pl.BlockSpec((None, tr, C), lambda s, r, cr: (s, cr[0] * nrb + r, 0)),
                  pl.BlockSpec((None, tr, C), lambda s, r, cr: (s, r, 0)),
                  pl.BlockSpec(memory_space=pl.ANY)],
        out_specs=pl.BlockSpec((None, tr, C), lambda s, r, cr: (s, r, 0)))
    return pl.pallas_call(body, grid_spec=gs, out_shape=jax.ShapeDtypeStruct((S, rh, C), bf16), name=name,
                          compiler_params=_cparams(("parallel", "parallel")))(c_idx, g, t, after)


def _add_shard_parts(name, g, t, r, kc_idx, after):
    S, R, C = g.shape
    rh = R // 2
    tr = _tile(rh, 256, 16)
    nrb = rh // tr

    def body(kc_ref, g_ref, t_ref, r_ref, after_ref, o_ref):
        own = g_ref[...] + t_ref[...]
        o_ref[...] = (own + r_ref[1].astype(f32)) + (r_ref[0].astype(f32) + r_ref[2].astype(f32))

    gs = pltpu.PrefetchScalarGridSpec(
        num_scalar_prefetch=1, grid=(nrb,),
        in_specs=[pl.BlockSpec((None, tr, C), lambda i, kc: (kc[0], kc[1] * nrb + i, 0)),
                  pl.BlockSpec((None, tr, C), lambda i, kc: (kc[0], i, 0)),
                  pl.BlockSpec((3, tr, C), lambda i, kc: (0, i, 0)),
                  pl.BlockSpec(memory_space=pl.ANY)],
        out_specs=pl.BlockSpec((tr, C), lambda i, kc: (kc[1] * nrb + i, 0)))
    return pl.pallas_call(body, grid_spec=gs, out_shape=jax.ShapeDtypeStruct((R, C), f32), name=name,
                          compiler_params=_cparams(("parallel",)))(kc_idx, g, t, r, after)


def _adamw(name, w, g, m, v):
    R, C = w.shape
    tr = _tile(R, 128, 8)

    def body(w_ref, g_ref, m_ref, v_ref, d_ref, nm_ref, nv_ref):
        gv = g_ref[...]
        nm = _ADAM_B1 * m_ref[...] + (1.0 - _ADAM_B1) * gv
        nv = _ADAM_B2 * v_ref[...] + (1.0 - _ADAM_B2) * (gv * gv)
        m_hat = nm / (1.0 - _ADAM_B1 ** _ADAM_STEP)
        v_hat = nv / (1.0 - _ADAM_B2 ** _ADAM_STEP)
        d_ref[...] = -_ADAM_LR * (m_hat / (jnp.sqrt(v_hat) + _ADAM_EPS) + _ADAM_WD * w_ref[...])
        nm_ref[...] = nm
        nv_ref[...] = nv

    spec = pl.BlockSpec((tr, C), lambda i: (i, 0))
    shape = jax.ShapeDtypeStruct((R, C), f32)
    return pl.pallas_call(body, grid=(R // tr,), in_specs=[spec] * 4, out_specs=[spec] * 3, out_shape=[shape] * 3,
                          name=name, compiler_params=_cparams(("parallel",)))(w, g, m, v)


def _pack(arrs):
    flat = jnp.concatenate([a.reshape(-1).astype(f32) for a in arrs])
    n = flat.shape[0]
    pad = (-n) % (128 * 128)
    return jnp.pad(flat, (0, pad)).reshape(-1, 128)


def _unpack(packed, shapes):
    flat = packed.reshape(-1)
    out, off = [], 0
    for s in shapes:
        n = math.prod(s)
        out.append(flat[off:off + n].reshape(s))
        off += n
    return out


_BIG = ("w_in", "w_glu", "w_ssm_out", "w_conv_out", "w_o", "w_up", "w_down")
_SMALL = ("norm_tok", "a_re", "a_im", "log_dt", "b_re", "b_im", "c_re", "c_im", "d_skip", "conv_w", "conv_b",
          "norm_ffn", "ffn_conv_w", "ffn_conv_b", "norm_final")
_WEIGHTS = ("norm_tok", "w_in", "a_re", "a_im", "log_dt", "b_re", "b_im", "c_re", "c_im", "d_skip", "w_glu",
            "w_ssm_out", "conv_w", "conv_b", "w_conv_out", "w_o", "norm_ffn", "w_up", "ffn_conv_w", "ffn_conv_b",
            "w_down", "norm_final")


def _step(cfg, x, tgt, p, m, v):
    L, D, SW, CW, F = cfg.L, cfg.D, cfg.SW, cfg.CW, cfg.F
    xi, yi, ci = _coords()
    k_idx = (2 * xi + yi).astype(jnp.int32).reshape(1)
    c_idx = ci.astype(jnp.int32).reshape(1)
    x = x.reshape(L, D)
    tgt = tgt.reshape(L, D)

    big2d = {n: p[n].reshape(p[n].shape[-2], p[n].shape[-1]) for n in _BIG}
    slots = {n: _cast_into_slot("cast_" + n, big2d[n], k_idx) for n in _BIG}
    wg = {}
    for cid, (gname, group) in enumerate((("allgather_w_in", ("w_in",)),
                                          ("allgather_w_mixer", ("w_glu", "w_ssm_out", "w_conv_out", "w_o")),
                                          ("allgather_w_up", ("w_up",)), ("allgather_w_down", ("w_down",)))):
        wg.update(zip(group, _allgather_weights(gname, [slots[n] for n in group], cid)))
    w_in, w_so, w_co, w_up = wg["w_in"], wg["w_ssm_out"], wg["w_conv_out"], wg["w_up"]
    w_glu = wg["w_glu"].reshape(SW, SW)
    w_o = wg["w_o"].reshape(D, D)
    w_down = wg["w_down"].reshape(F, D)
    kk = k_idx[0]
    cw_full = lax.dynamic_update_slice(jnp.zeros((3, CW), f32), p["conv_w"].reshape(3, CW // _NCHIP), (0, kk * (CW // _NCHIP)))
    fw_full = lax.dynamic_update_slice(jnp.zeros((3, F), f32), p["ffn_conv_w"].reshape(3, F // _NCHIP), (0, kk * (F // _NCHIP)))
    south = (ci == 0).astype(f32)
    conv_w, ffn_conv_w = _unpack(_allreduce8("allgather_conv_filters", _pack([cw_full * south, fw_full * south])),
                                 [(3, CW), (3, F)])
    conv_b = p["conv_b"].reshape(1, CW)
    ffn_conv_b = p["ffn_conv_b"].reshape(1, F)
    norm_tok = p["norm_tok"].reshape(1, D)
    norm_ffn = p["norm_ffn"].reshape(1, D)
    norm_final = p["norm_final"].reshape(1, D)
    dskip = p["d_skip"].reshape(1, SW)

    s5_in = (p["a_re"][0], p["a_im"][0], p["log_dt"][0], p["b_re"][0], p["b_im"][0])
    (abr, abi, bbr, bbi), disc_vjp = jax.vjp(_discretize, *s5_in)
    tabs = _s5_tables(cfg, abr, abi, bbr, bbi, p["c_re"][0], p["c_im"][0])

    xn1, r1 = _rms_fwd("rms_tok", x, norm_tok)
    proj = _mm_nn("in_proj", xn1, w_in, [bf16])[0]
    y_s, ya0, cin = _s5_fwd(cfg, proj, tabs, dskip)

    def tiles(*arrs):
        return lambda tm, tn: [(a, pl.BlockSpec((tm, tn), lambda i, j: (i, j))) for a in arrs]

    def glu_epi(acc, e, o):
        o[0][...] = (e[0][...].astype(f32) * jax.nn.sigmoid(acc)).astype(bf16)
        o[1][...] = acc.astype(bf16)

    ya1, z = _mm_nn("glu", ya0, w_glu, [bf16, bf16], extras_fn=tiles(ya0), epilogue=glu_epi)
    yb0 = _convb_fwd(cfg, proj, conv_w, conv_b)
    merged, ya, yb = _merge_fwd(cfg, ya1, yb0, proj, w_so, w_co)

    def res_epi(acc, e, o):
        o[0][...] = e[0][...] + acc

    h1 = _mm_nn("out_proj", merged, w_o, [f32], extras_fn=tiles(x), epilogue=res_epi)[0]
    xn2, r2 = _rms_fwd("rms_ffn", h1, norm_ffn)
    hh = _mm_nn("ffn_up", xn2, w_up, [bf16], tn=1408)[0]
    fact = _ffn_act(cfg, hh, ffn_conv_w, ffn_conv_b)
    h2 = _mm_nn("ffn_down", fact, w_down, [f32], tm=512, extras_fn=tiles(h1), epilogue=res_epi)[0]
    dh2, dh2b, g_norm_final, loss_tile = _loss_head("loss_head", h2, tgt, norm_final)

    kc_idx = jnp.concatenate([k_idx, c_idx])
    reduced, chains = {}, {}

    def rs_halves(tag, collective_id, names, gs):
        chains[tag] = dict(cid=collective_id, names=names, gs=gs,
                           sib=_exchange_halves("grad_halves_" + tag, gs, collective_id))

    def rs_shards(tag, after):
        ch = chains[tag]
        parts = [_add_own_half("grad_add_halves_" + n, g, t, c_idx, after)
                 for n, g, t in zip(ch["names"], ch["gs"], ch["sib"])]
        ch["chips"] = _scatter_shards("grad_shards_" + tag, parts, ch["cid"] + 1)

    def rs_join(tag, after):
        ch = chains[tag]
        halves = [_add_shard_parts("grad_add_chips_" + n, g, t, r, kc_idx, after)
                  for n, g, t, r in zip(ch["names"], ch["gs"], ch["sib"], ch["chips"])]
        ch["joined"] = _join_halves("grad_join_" + tag, halves, ch["cid"] + 2)

    def rs_done(tag):
        reduced.update(zip(chains[tag]["names"], chains[tag]["joined"]))

    df = _mm_nt("ffn_down_dx", dh2b, w_down, [bf16], tn=1408)[0]
    g_w_down = _mm_tn("ffn_down_dw", fact, dh2b, tm=1408, tn=512)
    rs_halves("ffn_down", 4, ["w_down"], [g_w_down.reshape(_NCHIP, F // _NCHIP, D)])
    dhh, g_ffn_conv_w, g_ffn_conv_b = _ffn_act_bwd(cfg, hh, df, ffn_conv_w, ffn_conv_b)
    rs_shards("ffn_down", dhh)
    g_w_up = _mm_tn("ffn_up_dw", xn2, dhh, shards=_NCHIP, tm=512, tn=1408, b_resident=True)
    rs_halves("ffn_up", 7, ["w_up"], [g_w_up])
    dxn2 = _ffn_up_dx(cfg, dhh, w_up)
    rs_shards("ffn_up", dxn2)
    rs_join("ffn_down", dxn2)
    dh1, dh1b, g_norm_ffn = _rms_bwd("rms_ffn_bwd", dxn2, h1, r2, norm_ffn, dh2)

    dya, dyb, dproj = _out_proj_dx(cfg, dh1b, w_o, ya, yb, proj)
    g_w_o = _mm_tn("out_proj_dw", merged, dh1b)

    def glu_bwd_epi(acc, e, o):
        a0 = e[0][...].astype(f32)
        s = jax.nn.sigmoid(e[1][...].astype(f32))
        o[0][...] = (acc * a0 * s * (1.0 - s)).astype(bf16)
        o[1][...] = (acc * s).astype(bf16)

    dz, t1 = _mm_nt("ssm_out_dx", dya, w_so, [bf16, bf16], extras_fn=tiles(ya0, z), epilogue=glu_bwd_epi)
    g_w_so = _mm_tn("ssm_out_dw", ya1, dya, shards=_NCHIP, tn=512)
    dyb0 = _mm_nt("conv_out_dx", dyb, w_co, [bf16])[0]
    g_w_co = _mm_tn("conv_out_dw", yb0, dyb, shards=_NCHIP, tn=512)
    dproj, g_conv_w, g_conv_b = _convb_bwd(cfg, proj, dyb0, conv_w, conv_b, dproj)

    def gelu_bwd_epi(acc, e, o):
        o[0][...] = ((e[0][...].astype(f32) + acc) * _gelu_grad(e[1][...].astype(f32))).astype(bf16)

    dy_s = _mm_nt("glu_dx", dz, w_glu, [bf16], extras_fn=tiles(t1, y_s), epilogue=gelu_bwd_epi)[0]
    g_w_glu = _mm_tn("glu_dw", ya0, dz)
    rs_halves("mixer", 10, ["w_o", "w_ssm_out", "w_conv_out", "w_glu"],
              [g_w_o.reshape(_NCHIP, D // _NCHIP, D), g_w_so, g_w_co, g_w_glu.reshape(_NCHIP, SW // _NCHIP, SW)])
    rs_join("ffn_up", g_w_glu)
    dproj, da_acc, db_full, dc_full, g_dskip = _s5_bwd(cfg, proj, dy_s, cin, tabs, dskip, dproj)
    rs_shards("mixer", dproj)
    g_w_in = _mm_tn("in_proj_dw", xn1, dproj, shards=_NCHIP, tn=CW,
                    b_block=lambda j: jnp.where(j == 0, 3 * CW // SW, jnp.where(j < 4, j - 1, j)))
    rs_halves("in_proj", 13, ["w_in"], [g_w_in])
    dxn1 = _in_proj_dx(cfg, dproj, w_in)
    rs_shards("in_proj", dxn1)
    rs_join("mixer", dxn1)
    dx, _, g_norm_tok = _rms_bwd("rms_tok_bwd", dxn1, x, r1, norm_tok, dh1)

    dabr, dabi, dbbr, dbbi, g_c_re, g_c_im = _s5_param_grads(cfg, da_acc, db_full, dc_full)
    g_a_re, g_a_im, g_log_dt, g_b_re, g_b_im = disc_vjp((dabr, dabi, dbbr, dbbi))

    small_g = {"norm_tok": g_norm_tok, "a_re": g_a_re, "a_im": g_a_im, "log_dt": g_log_dt, "b_re": g_b_re, "b_im": g_b_im,
               "c_re": g_c_re, "c_im": g_c_im, "d_skip": g_dskip, "conv_w": g_conv_w, "conv_b": g_conv_b,
               "norm_ffn": g_norm_ffn, "ffn_conv_w": g_ffn_conv_w, "ffn_conv_b": g_ffn_conv_b, "norm_final": g_norm_final}
    small_shapes = [small_g[n].shape for n in _SMALL]
    summed = dict(zip(_SMALL, _unpack(_allreduce8("allreduce_small_grads", _pack([small_g[n] for n in _SMALL])), small_shapes)))
    summed["conv_w"] = lax.dynamic_slice(summed["conv_w"], (0, kk * (CW // _NCHIP)), (3, CW // _NCHIP))
    summed["ffn_conv_w"] = lax.dynamic_slice(summed["ffn_conv_w"], (0, kk * (F // _NCHIP)), (3, F // _NCHIP))

    grads, deltas, new_m, new_v = {}, {}, {}, {}

    def adamw_big(names):
        for n in names:
            d_, m_, v_ = _adamw("adamw_" + n, big2d[n], reduced[n], m[n].reshape(big2d[n].shape), v[n].reshape(big2d[n].shape))
            grads[n] = reduced[n].reshape(p[n].shape)
            deltas[n], new_m[n], new_v[n] = d_.reshape(p[n].shape), m_.reshape(p[n].shape), v_.reshape(p[n].shape)

    rs_done("ffn_down")
    rs_done("ffn_up")
    adamw_big(["w_down", "w_up"])
    rs_join("in_proj", deltas["w_up"])
    rs_done("mixer")
    adamw_big(["w_o", "w_ssm_out", "w_conv_out", "w_glu"])
    shapes = [p[n].shape for n in _SMALL]
    sg = _pack([summed[n] for n in _SMALL])
    d_, m_, v_ = _adamw("adamw_small", _pack([p[n] for n in _SMALL]), sg, _pack([m[n] for n in _SMALL]),
                        _pack([v[n] for n in _SMALL]))
    for n, dd, mm, vv in zip(_SMALL, _unpack(d_, shapes), _unpack(m_, shapes), _unpack(v_, shapes)):
        grads[n] = summed[n].reshape(p[n].shape)
        deltas[n], new_m[n], new_v[n] = dd, mm, vv
    rs_done("in_proj")
    adamw_big(["w_in"])

    loss = lax.psum(loss_tile[0, 0], ("x", "y", "c"))
    return (loss, dx.reshape(1, L, D), *[grads[n] for n in _WEIGHTS], *[deltas[n] for n in _WEIGHTS],
            *[new_m[n] for n in _WEIGHTS], *[new_v[n] for n in _WEIGHTS])


def kernel(x, norm_tok, w_in, a_re, a_im, log_dt, b_re, b_im, c_re, c_im, d_skip, w_glu, w_ssm_out, conv_w, conv_b, w_conv_out, w_o, norm_ffn, w_up, ffn_conv_w, ffn_conv_b, w_down, norm_final, loss_target, m_norm_tok, m_w_in, m_a_re, m_a_im, m_log_dt, m_b_re, m_b_im, m_c_re, m_c_im, m_d_skip, m_w_glu, m_w_ssm_out, m_conv_w, m_conv_b, m_w_conv_out, m_w_o, m_norm_ffn, m_w_up, m_ffn_conv_w, m_ffn_conv_b, m_w_down, m_norm_final, v_norm_tok, v_w_in, v_a_re, v_a_im, v_log_dt, v_b_re, v_b_im, v_c_re, v_c_im, v_d_skip, v_w_glu, v_w_ssm_out, v_conv_w, v_conv_b, v_w_conv_out, v_w_o, v_norm_ffn, v_w_up, v_ffn_conv_w, v_ffn_conv_b, v_w_down, v_norm_final):
    p = dict(norm_tok=norm_tok, w_in=w_in, a_re=a_re, a_im=a_im, log_dt=log_dt, b_re=b_re, b_im=b_im, c_re=c_re,
             c_im=c_im, d_skip=d_skip, w_glu=w_glu, w_ssm_out=w_ssm_out, conv_w=conv_w, conv_b=conv_b,
             w_conv_out=w_conv_out, w_o=w_o, norm_ffn=norm_ffn, w_up=w_up, ffn_conv_w=ffn_conv_w,
             ffn_conv_b=ffn_conv_b, w_down=w_down, norm_final=norm_final)
    m = dict(norm_tok=m_norm_tok, w_in=m_w_in, a_re=m_a_re, a_im=m_a_im, log_dt=m_log_dt, b_re=m_b_re, b_im=m_b_im,
             c_re=m_c_re, c_im=m_c_im, d_skip=m_d_skip, w_glu=m_w_glu, w_ssm_out=m_w_ssm_out, conv_w=m_conv_w,
             conv_b=m_conv_b, w_conv_out=m_w_conv_out, w_o=m_w_o, norm_ffn=m_norm_ffn, w_up=m_w_up,
             ffn_conv_w=m_ffn_conv_w, ffn_conv_b=m_ffn_conv_b, w_down=m_w_down, norm_final=m_norm_final)
    v = dict(norm_tok=v_norm_tok, w_in=v_w_in, a_re=v_a_re, a_im=v_a_im, log_dt=v_log_dt, b_re=v_b_re, b_im=v_b_im,
             c_re=v_c_re, c_im=v_c_im, d_skip=v_d_skip, w_glu=v_w_glu, w_ssm_out=v_w_ssm_out, conv_w=v_conv_w,
             conv_b=v_conv_b, w_conv_out=v_w_conv_out, w_o=v_w_o, norm_ffn=v_norm_ffn, w_up=v_w_up,
             ffn_conv_w=v_ffn_conv_w, ffn_conv_b=v_ffn_conv_b, w_down=v_w_down, norm_final=v_norm_final)
    return _step(_Cfg(), x, loss_target, p, m, v)
```

```python
import functools
import math
from typing import NamedTuple

import jax
import jax.numpy as jnp
from jax import lax
from jax.experimental import pallas as pl
from jax.experimental.pallas import tpu as pltpu
from jax.experimental.pallas import tpu_sc as plsc

f32 = jnp.float32
bf16 = jnp.bfloat16
_MESH = pl.DeviceIdType.MESH

_EPS = 1e-6
_ADAM_LR = 0.001
_ADAM_B1 = 0.9
_ADAM_B2 = 0.999
_ADAM_EPS = 1e-08
_ADAM_WD = 0.01
_ADAM_STEP = 10
_SSM_GROUP = 16
_SSM_STATE = 64
_SLAB_GROUPS = 16
_NCHIP = 4
_VMEM_LIMIT = 56 * 2**20
_GELU_C = math.sqrt(2.0 / math.pi)
_GELU_A = 0.044715


class _Cfg(NamedTuple):
    L: int = 4096
    D: int = 2048
    SW: int = 1024
    CW: int = 1024
    F: int = 5632
    T: int = 256


def _tile(n, pref, align):
    t = min(n, pref)
    t -= t % align
    while t > align and n % t:
        t -= align
    assert t > 0 and n % t == 0, (n, pref, align)
    return t


def _cparams(sem):
    return pltpu.CompilerParams(dimension_semantics=sem, vmem_limit_bytes=_VMEM_LIMIT)


def _gelu(x):
    return 0.5 * x * (1.0 + jnp.tanh(_GELU_C * (x + _GELU_A * x * x * x)))


def _gelu_grad(x):
    th = jnp.tanh(_GELU_C * (x + _GELU_A * x * x * x))
    return 0.5 * (1.0 + th) + 0.5 * x * (1.0 - th * th) * _GELU_C * (1.0 + 3.0 * _GELU_A * x * x)


_NN = (((1,), (0,)), ((), ()))
_NT = (((1,), (1,)), ((), ()))
_TN = (((0,), (0,)), ((), ()))


def _whole(ref):
    return ref[...]


_ANY = pl.BlockSpec(memory_space=pl.ANY)


def _mm(name, operands, steps, *, grid, contract, outs, extras=(), epilogue=None, acc_shape=None, after=()):
    nop, ne, na = len(operands), len(extras), len(after)
    nk = len(steps)

    def body(*refs):
        op_refs = refs[:nop]
        e_refs = refs[nop:nop + ne]
        o_refs = refs[nop + ne + na:nop + ne + na + len(outs)]

        def partial(terms):
            tot = None
            for ai, av, bi, bv in terms:
                d = lax.dot_general(av(op_refs[ai]), bv(op_refs[bi]), contract, preferred_element_type=f32)
                tot = d if tot is None else tot + d
            return tot

        def finish(res):
            if epilogue is None:
                for o in o_refs:
                    o[...] = res.astype(o.dtype)
            else:
                epilogue(res, e_refs, o_refs)

        if nk == 1:
            finish(partial(steps[0][1]))
            return
        acc = refs[-1]
        kid = pl.program_id(len(grid) - 1)
        for k, terms in steps:
            def run(k=k, terms=terms):
                d = partial(terms)
                if k == 0:
                    acc[...] = d
                elif k < nk - 1:
                    acc[...] += d
                else:
                    finish(acc[...] + d)

            pl.when(kid == k)(run)

    sem = ("parallel",) * (len(grid) - (nk > 1)) + (("arbitrary",) if nk > 1 else ())
    return pl.pallas_call(
        body, grid=grid, in_specs=[o[1] for o in operands] + [e[1] for e in extras] + [_ANY] * na,
        out_specs=[o[1] for o in outs], out_shape=[o[0] for o in outs],
        scratch_shapes=[pltpu.VMEM(acc_shape, f32)] if nk > 1 else [], name=name,
        compiler_params=_cparams(sem))(*[o[0] for o in operands], *[e[0] for e in extras], *after)


def _mm_nn(name, a, w, out_dtypes, *, tm=1024, tn=1024, rows=False, extras_fn=None, epilogue=None):
    M, K = a.shape
    S, Ns = w.shape[0], w.shape[-1]
    N = Ns if rows else Ns * S
    tm, tn = _tile(M, tm, 16), _tile(Ns, tn, 128)
    nb = Ns // tn
    a_spec = pl.BlockSpec((tm, K), lambda i, j: (i, 0))
    if rows:
        b_spec = pl.BlockSpec((S, K // S, tn), lambda i, j: (0, 0, j))
        b_view = lambda r: r[...].reshape(K, tn)
    else:
        b_spec = pl.BlockSpec((None, K, tn), lambda i, j: (j // nb, 0, j % nb))
        b_view = _whole
    o_spec = pl.BlockSpec((tm, tn), lambda i, j: (i, j))
    outs = [(jax.ShapeDtypeStruct((M, N), dt), o_spec) for dt in out_dtypes]
    extras = extras_fn(tm, tn) if extras_fn is not None else ()
    return _mm(name, [(a, a_spec), (w, b_spec)], [(None, [(0, _whole, 1, b_view)])], grid=(M // tm, N // tn),
               contract=_NN, outs=outs, extras=extras, epilogue=epilogue)


def _mm_nt(name, a, w, out_dtypes, *, tm=1024, tn=1024, rows=False, extras_fn=None, epilogue=None):
    M, N = a.shape
    S, Ks, Ns = w.shape
    K = Ks * S if rows else Ks
    tm = _tile(M, tm, 16)
    a_spec = pl.BlockSpec((tm, N), lambda i, j: (i, 0))
    if rows:
        tn = K if tn >= K else _tile(Ks, tn, 128)
        if tn == K:
            b_spec = pl.BlockSpec((S, Ks, N), lambda i, j: (0, 0, 0))
            terms = [(0, _whole, 1, lambda r: r[...].reshape(K, N))]
        else:
            nbs = Ks // tn
            b_spec = pl.BlockSpec((None, tn, N), lambda i, j: (j // nbs, j % nbs, 0))
            terms = [(0, _whole, 1, _whole)]
    else:
        tn = _tile(K, tn, 128)
        assert S * Ns == N
        b_spec = pl.BlockSpec((S, tn, Ns), lambda i, j: (0, j, 0))
        terms = [(0, lambda r, s=s: r[:, s * Ns:(s + 1) * Ns], 1, lambda r, s=s: r[s]) for s in range(S)]
    o_spec = pl.BlockSpec((tm, tn), lambda i, j: (i, j))
    outs = [(jax.ShapeDtypeStruct((M, K), dt), o_spec) for dt in out_dtypes]
    extras = extras_fn(tm, tn) if extras_fn is not None else ()
    return _mm(name, [(a, a_spec), (w, b_spec)], [(None, terms)], grid=(M // tm, K // tn), contract=_NT,
               outs=outs, extras=extras, epilogue=epilogue)


def _mm_tn(name, a, b, *, shards=None, tm=1024, tn=1024, b_block=None, b_resident=False, after=()):
    M, K = a.shape
    halves = b.shape[0] if b.ndim == 3 else 1
    Nh = b.shape[-1]
    N = Nh * halves
    Ns = N // shards if shards else N
    tm, tn = _tile(K, tm, 128), _tile(math.gcd(Ns, Nh), tn, 128)
    nb, nbh = Ns // tn, Nh // tn
    ij = (lambda g0, g1: (g1, g0)) if b_resident else (lambda g0, g1: (g0, g1))
    bmap = b_block if b_block is not None else (lambda j: j)
    a_spec = pl.BlockSpec((M, tm), lambda g0, g1: (0, ij(g0, g1)[0]))
    if halves > 1:
        b_spec = pl.BlockSpec((None, M, tn), lambda g0, g1: (bmap(ij(g0, g1)[1]) // nbh, 0, bmap(ij(g0, g1)[1]) % nbh))
    else:
        b_spec = pl.BlockSpec((M, tn), lambda g0, g1: (0, bmap(ij(g0, g1)[1])))
    if shards:
        out = (jax.ShapeDtypeStruct((shards, K, Ns), f32),
               pl.BlockSpec((None, tm, tn), lambda g0, g1: (ij(g0, g1)[1] // nb, ij(g0, g1)[0], ij(g0, g1)[1] % nb)))
    else:
        out = (jax.ShapeDtypeStruct((K, N), f32), pl.BlockSpec((tm, tn), lambda g0, g1: ij(g0, g1)))
    grid = (N // tn, K // tm) if b_resident else (K // tm, N // tn)
    return _mm(name, [(a, a_spec), (b, b_spec)], [(None, [(0, _whole, 1, _whole)])], grid=grid, contract=_TN,
               outs=[out], after=after)[0]


def _in_proj_dx(cfg, dproj, w_in):
    L, D, SW, CW = cfg.L, cfg.D, cfg.SW, cfg.CW
    NP = SW + 3 * CW + 2 * D
    Ns = NP // _NCHIP
    assert SW + CW == Ns and 2 * CW == Ns and D == Ns
    tm, tn = _tile(L, 1024, 16), _tile(D, 1024, 128)
    a_spec = pl.BlockSpec((tm, NP // 2), lambda i, j, k: (i, k))
    b_spec = pl.BlockSpec((2, tn, Ns), lambda i, j, k: (k, j, 0))
    first = [(0, lambda r: r[:, 0:CW], 1, lambda r: r[0, :, SW:SW + CW]),
             (0, lambda r: r[:, CW:3 * CW], 1, lambda r: r[1]),
             (0, lambda r: r[:, 3 * CW:3 * CW + SW], 1, lambda r: r[0, :, 0:SW])]
    second = [(0, lambda r: r[:, 0:D], 1, lambda r: r[0]), (0, lambda r: r[:, D:2 * D], 1, lambda r: r[1])]
    out = (jax.ShapeDtypeStruct((L, D), bf16), pl.BlockSpec((tm, tn), lambda i, j, k: (i, j)))
    return _mm("in_proj_dx", [(dproj, a_spec), (w_in, b_spec)], [(0, first), (1, second)], grid=(L // tm, D // tn, 2),
               contract=_NT, outs=[out], acc_shape=(tm, tn))[0]


def _out_proj_dx(cfg, dh1b, w_o, ya, yb, proj):
    L, D = cfg.L, cfg.D
    NP = cfg.SW + 3 * cfg.CW + 2 * D
    assert NP == 4 * D
    tm = _tile(L, 256, 16)

    def epilogue(acc, e, o):
        sa = jax.nn.sigmoid(e[2][:, 0:D].astype(f32))
        sb = jax.nn.sigmoid(e[2][:, D:2 * D].astype(f32))
        o[0][...] = (acc * sa).astype(bf16)
        o[1][...] = (acc * sb).astype(bf16)
        o[2][:, 0:D] = (acc * e[0][...].astype(f32) * sa * (1.0 - sa)).astype(bf16)
        o[2][:, D:2 * D] = (acc * e[1][...].astype(f32) * sb * (1.0 - sb)).astype(bf16)

    row = pl.BlockSpec((tm, D), lambda i, j: (i, 0))
    half = pl.BlockSpec((tm, 2 * D), lambda i, j: (i, 1))
    return _mm("out_proj_dx", [(dh1b, row), (w_o, pl.BlockSpec(w_o.shape, lambda i, j: (0, 0, 0)))],
               [(None, [(0, _whole, 1, lambda r: r[...].reshape(D, D))])], grid=(L // tm, 1), contract=_NT,
               outs=[(jax.ShapeDtypeStruct((L, D), bf16), row), (jax.ShapeDtypeStruct((L, D), bf16), row),
                     (jax.ShapeDtypeStruct((L, NP), bf16), half)],
               extras=[(ya, row), (yb, row), (proj, half)], epilogue=epilogue)


def _ffn_up_dx(cfg, dhh, w_up):
    L, D, F = cfg.L, cfg.D, cfg.F
    Fh = F // 2
    tm, tn = _tile(L, 1024, 16), _tile(D, 512, 128)
    a_spec = pl.BlockSpec((None, tm, F), lambda i, j, k: (k, i, 0))
    b_spec = pl.BlockSpec((2, tn, Fh), lambda i, j, k: (k, j, 0))
    terms = [(0, lambda r: r[:, 0:Fh], 1, lambda r: r[0]), (0, lambda r: r[:, Fh:F], 1, lambda r: r[1])]
    out = (jax.ShapeDtypeStruct((L, D), bf16), pl.BlockSpec((tm, tn), lambda i, j, k: (i, j)))
    return _mm("ffn_up_dx", [(dhh, a_spec), (w_up, b_spec)], [(0, terms), (1, terms)], grid=(L // tm, D // tn, 2),
               contract=_NT, outs=[out], acc_shape=(tm, tn))[0]


def _rms_fwd(name, x, g):
    L, D = x.shape
    tm = _tile(L, 256, 16)

    def body(x_ref, g_ref, xn_ref, r_ref):
        xv = x_ref[...]
        r = lax.rsqrt(jnp.mean(xv * xv, axis=-1, keepdims=True) + _EPS)
        xn_ref[...] = (xv * r * g_ref[...]).astype(bf16)
        r_ref[...] = r

    return pl.pallas_call(
        body, grid=(L // tm,),
        in_specs=[pl.BlockSpec((tm, D), lambda i: (i, 0)), pl.BlockSpec((1, D), lambda i: (0, 0))],
        out_specs=[pl.BlockSpec((tm, D), lambda i: (i, 0)), pl.BlockSpec((tm, 1), lambda i: (i, 0))],
        out_shape=[jax.ShapeDtypeStruct((L, D), bf16), jax.ShapeDtypeStruct((L, 1), f32)],
        name=name, compiler_params=_cparams(("parallel",)))(x, g)


def _rms_bwd(name, dxn, h, r, g, dres, after=()):
    L, D = h.shape
    tm = _tile(L, 256, 16)

    def body(dxn_ref, h_ref, r_ref, g_ref, dres_ref, *rest):
        dh_ref, dhb_ref, dg_ref = rest[len(after):]
        i = pl.program_id(0)
        d = dxn_ref[...].astype(f32)
        hv = h_ref[...]
        rv = r_ref[...]
        dyg = d * g_ref[...]
        m = jnp.mean(dyg * hv, axis=-1, keepdims=True)
        dh = dres_ref[...] + rv * dyg - hv * (rv * rv * rv) * m
        dh_ref[...] = dh
        dhb_ref[...] = dh.astype(bf16)

        @pl.when(i == 0)
        def _():
            dg_ref[...] = jnp.zeros_like(dg_ref)

        dg_ref[...] += jnp.sum(d * hv * rv, axis=0, keepdims=True)

    row = lambda i: (i, 0)
    return pl.pallas_call(
        body, grid=(L // tm,),
        in_specs=[pl.BlockSpec((tm, D), row), pl.BlockSpec((tm, D), row), pl.BlockSpec((tm, 1), row),
                  pl.BlockSpec((1, D), lambda i: (0, 0)), pl.BlockSpec((tm, D), row)] + [_ANY] * len(after),
        out_specs=[pl.BlockSpec((tm, D), row), pl.BlockSpec((tm, D), row), pl.BlockSpec((1, D), lambda i: (0, 0))],
        out_shape=[jax.ShapeDtypeStruct((L, D), f32), jax.ShapeDtypeStruct((L, D), bf16), jax.ShapeDtypeStruct((1, D), f32)],
        name=name, compiler_params=_cparams(("arbitrary",)))(dxn, h, r, g, dres, *after)


def _loss_head(name, h2, tgt, g):
    L, D = h2.shape
    tm = _tile(L, 256, 16)

    def body(h_ref, t_ref, g_ref, dh_ref, dhb_ref, dg_ref, loss_ref):
        i = pl.program_id(0)
        hv = h_ref[...]
        gv = g_ref[...]
        r = lax.rsqrt(jnp.mean(hv * hv, axis=-1, keepdims=True) + _EPS)
        err = hv * r * gv - t_ref[...]
        dy = err * (1.0 / D)
        dyg = dy * gv
        m = jnp.mean(dyg * hv, axis=-1, keepdims=True)
        dh = r * dyg - hv * (r * r * r) * m
        dh_ref[...] = dh
        dhb_ref[...] = dh.astype(bf16)

        @pl.when(i == 0)
        def _():
            dg_ref[...] = jnp.zeros_like(dg_ref)
            loss_ref[...] = jnp.zeros_like(loss_ref)

        dg_ref[...] += jnp.sum(dy * hv * r, axis=0, keepdims=True)
        part = jnp.sum(jnp.sum(err * err, axis=-1, keepdims=True), axis=0, keepdims=True) * (0.5 / D)
        loss_ref[...] += jnp.broadcast_to(part, (8, 128))

    row = lambda i: (i, 0)
    return pl.pallas_call(
        body, grid=(L // tm,),
        in_specs=[pl.BlockSpec((tm, D), row), pl.BlockSpec((tm, D), row), pl.BlockSpec((1, D), lambda i: (0, 0))],
        out_specs=[pl.BlockSpec((tm, D), row), pl.BlockSpec((tm, D), row), pl.BlockSpec((1, D), lambda i: (0, 0)),
                   pl.BlockSpec((8, 128), lambda i: (0, 0))],
        out_shape=[jax.ShapeDtypeStruct((L, D), f32), jax.ShapeDtypeStruct((L, D), bf16),
                   jax.ShapeDtypeStruct((1, D), f32), jax.ShapeDtypeStruct((8, 128), f32)],
        name=name, compiler_params=_cparams(("arbitrary",)))(h2, tgt, g)


def _shift_down(tile, halo, k, rows8):
    tm = tile.shape[0]
    r = pltpu.roll(tile, k, axis=0)
    hh = pltpu.roll(halo, k, axis=0)
    top = jnp.where(rows8 < k, hh, r[:8])
    return jnp.concatenate([top, r[8:]], axis=0) if tm > 8 else top


def _shift_up(tile, halo, k, rows8):
    tm = tile.shape[0]
    r = pltpu.roll(tile, tm - k, axis=0)
    hh = pltpu.roll(halo, 8 - k, axis=0)
    bot = jnp.where(rows8 >= 8 - k, hh, r[tm - 8:])
    return jnp.concatenate([r[:tm - 8], bot], axis=0) if tm > 8 else bot


def _conv3(x, halo, w_ref, b_ref, rows8):
    return (w_ref[0:1, :] * _shift_down(x, halo, 2, rows8) + w_ref[1:2, :] * _shift_down(x, halo, 1, rows8)
            + w_ref[2:3, :] * x + b_ref[...])


def _convb_fwd(cfg, proj, w, b):
    L, CW = cfg.L, cfg.CW
    assert cfg.SW == CW
    tm = _tile(L, 512, 16)

    def body(v_ref, vh_ref, gb_ref, gc_ref, gch_ref, w_ref, b_ref, o_ref):
        i = pl.program_id(0)
        rows8 = lax.broadcasted_iota(jnp.int32, (8, CW), 0)
        cv = gc_ref[...].astype(f32) * v_ref[...].astype(f32)
        cvh = gch_ref[...].astype(f32)[8:] * vh_ref[...].astype(f32)[8:]
        cvh = jnp.where(i == 0, 0.0, cvh)
        cc = _conv3(cv, cvh, w_ref, b_ref, rows8)
        o_ref[...] = (gb_ref[...].astype(f32) * cc).astype(bf16)

    blk = lambda col: pl.BlockSpec((tm, CW), lambda i: (i, col))
    halo = lambda col: pl.BlockSpec((16, CW), lambda i: (jnp.maximum(i * (tm // 16) - 1, 0), col))
    return pl.pallas_call(
        body, grid=(L // tm,),
        in_specs=[blk(1), halo(1), blk(2), blk(3), halo(3),
                  pl.BlockSpec((3, CW), lambda i: (0, 0)), pl.BlockSpec((1, CW), lambda i: (0, 0))],
        out_specs=pl.BlockSpec((tm, CW), lambda i: (i, 0)),
        out_shape=jax.ShapeDtypeStruct((L, CW), bf16),
        name="convb_fwd", compiler_params=_cparams(("parallel",)))(proj, proj, proj, proj, proj, w, b)


def _convb_bwd(cfg, proj, dyb0, w, b, dproj):
    L, CW = cfg.L, cfg.CW
    tm = _tile(L, 512, 16)
    nt = L // tm

    def body(v_ref, vh_ref, gb_ref, gbn_ref, gc_ref, gch_ref, d_ref, dn_ref, w_ref, b_ref, dproj_ref,
             o_ref, dw_ref, db_ref):
        i = pl.program_id(0)
        rows8 = lax.broadcasted_iota(jnp.int32, (8, CW), 0)
        v = v_ref[...].astype(f32)
        gb = gb_ref[...].astype(f32)
        gc = gc_ref[...].astype(f32)
        d = d_ref[...].astype(f32)
        cv = gc * v
        cvh = gch_ref[...].astype(f32)[8:] * vh_ref[...].astype(f32)[8:]
        cvh = jnp.where(i == 0, 0.0, cvh)
        s2 = _shift_down(cv, cvh, 2, rows8)
        s1 = _shift_down(cv, cvh, 1, rows8)
        cc = w_ref[0:1, :] * s2 + w_ref[1:2, :] * s1 + w_ref[2:3, :] * cv + b_ref[...]
        dcc = d * gb
        dccn = dn_ref[...].astype(f32)[:8] * gbn_ref[...].astype(f32)[:8]
        dccn = jnp.where(i == nt - 1, 0.0, dccn)
        dcv = (w_ref[2:3, :] * dcc + w_ref[1:2, :] * _shift_up(dcc, dccn, 1, rows8)
               + w_ref[0:1, :] * _shift_up(dcc, dccn, 2, rows8))
        o_ref[:, 0:CW] = (dcv * gc).astype(bf16)
        o_ref[:, CW:2 * CW] = (d * cc).astype(bf16)
        o_ref[:, 2 * CW:3 * CW] = (dcv * v).astype(bf16)

        @pl.when(i == 0)
        def _():
            dw_ref[...] = jnp.zeros_like(dw_ref)
            db_ref[...] = jnp.zeros_like(db_ref)

        dw_ref[0:1, :] += jnp.sum(dcc * s2, axis=0, keepdims=True)
        dw_ref[1:2, :] += jnp.sum(dcc * s1, axis=0, keepdims=True)
        dw_ref[2:3, :] += jnp.sum(dcc * cv, axis=0, keepdims=True)
        db_ref[...] += jnp.sum(dcc, axis=0, keepdims=True)

    blk = lambda col: pl.BlockSpec((tm, CW), lambda i: (i, col))
    prev = lambda col: pl.BlockSpec((16, CW), lambda i: (jnp.maximum(i * (tm // 16) - 1, 0), col))
    nxt = lambda col: pl.BlockSpec((16, CW), lambda i: (jnp.minimum((i + 1) * (tm // 16), L // 16 - 1), col))
    const = lambda r: pl.BlockSpec((r, CW), lambda i: (0, 0))
    return pl.pallas_call(
        body, grid=(nt,),
        in_specs=[blk(1), prev(1), blk(2), nxt(2), blk(3), prev(3), blk(0), nxt(0), const(3), const(1),
                  pl.BlockSpec(memory_space=pl.ANY)],
        out_specs=[pl.BlockSpec((tm, 3 * CW), lambda i: (i, 0)), const(3), const(1)],
        out_shape=[jax.ShapeDtypeStruct(dproj.shape, bf16), jax.ShapeDtypeStruct((3, CW), f32),
                   jax.ShapeDtypeStruct((1, CW), f32)],
        input_output_aliases={10: 0},
        name="convb_bwd", compiler_params=_cparams(("arbitrary",)))(proj, proj, proj, proj, proj, proj, dyb0, dyb0, w, b,
                                                                    dproj)


def _ffn_act(cfg, hh, w, b):
    L, F = cfg.L, cfg.F
    tm = _tile(L, 512, 16)
    tc = _tile(F, 1408, 128)
    ncb = F // tc

    def body(a_ref, ah_ref, g_ref, w_ref, b_ref, o_ref):
        i = pl.program_id(0)
        rows8 = lax.broadcasted_iota(jnp.int32, (8, tc), 0)
        a = a_ref[...].astype(f32)
        ah = jnp.where(i == 0, 0.0, ah_ref[...].astype(f32)[8:])
        o_ref[...] = (_gelu(_conv3(a, ah, w_ref, b_ref, rows8)) * g_ref[...].astype(f32)).astype(bf16)

    return pl.pallas_call(
        body, grid=(L // tm, ncb),
        in_specs=[pl.BlockSpec((tm, tc), lambda i, j: (i, j)),
                  pl.BlockSpec((16, tc), lambda i, j: (jnp.maximum(i * (tm // 16) - 1, 0), j)),
                  pl.BlockSpec((tm, tc), lambda i, j: (i, j + ncb)),
                  pl.BlockSpec((3, tc), lambda i, j: (0, j)), pl.BlockSpec((1, tc), lambda i, j: (0, j))],
        out_specs=pl.BlockSpec((tm, tc), lambda i, j: (i, j)),
        out_shape=jax.ShapeDtypeStruct((L, F), bf16),
        name="ffn_act", compiler_params=_cparams(("parallel", "parallel")))(hh, hh, hh, w, b)


def _ffn_act_bwd(cfg, hh, df, w, b):
    L, F = cfg.L, cfg.F
    tm = _tile(L, 512, 16)
    tc = _tile(F, 1408, 128)
    ncb = F // tc
    nt = L // tm

    def body(a_ref, ah_ref, an_ref, g_ref, gn_ref, d_ref, dn_ref, w_ref, b_ref, dhh_ref, dw_ref, db_ref):
        i = pl.program_id(1)
        rows8 = lax.broadcasted_iota(jnp.int32, (8, tc), 0)
        a = a_ref[...].astype(f32)
        ah = jnp.where(i == 0, 0.0, ah_ref[...].astype(f32)[8:])
        s2 = _shift_down(a, ah, 2, rows8)
        s1 = _shift_down(a, ah, 1, rows8)
        act = w_ref[0:1, :] * s2 + w_ref[1:2, :] * s1 + w_ref[2:3, :] * a + b_ref[...]
        d = d_ref[...].astype(f32)
        dhh_ref[1] = (d * _gelu(act)).astype(bf16)
        dact = d * g_ref[...].astype(f32) * _gelu_grad(act)
        an = an_ref[...].astype(f32)[:8]
        actn = _conv3(an, a[tm - 8:], w_ref, b_ref, rows8)
        dactn = dn_ref[...].astype(f32)[:8] * gn_ref[...].astype(f32)[:8] * _gelu_grad(actn)
        dactn = jnp.where(i == nt - 1, 0.0, dactn)
        da = (w_ref[2:3, :] * dact + w_ref[1:2, :] * _shift_up(dact, dactn, 1, rows8)
              + w_ref[0:1, :] * _shift_up(dact, dactn, 2, rows8))
        dhh_ref[0] = da.astype(bf16)

        @pl.when(i == 0)
        def _():
            dw_ref[...] = jnp.zeros_like(dw_ref)
            db_ref[...] = jnp.zeros_like(db_ref)

        dw_ref[0:1, :] += jnp.sum(dact * s2, axis=0, keepdims=True)
        dw_ref[1:2, :] += jnp.sum(dact * s1, axis=0, keepdims=True)
        dw_ref[2:3, :] += jnp.sum(dact * a, axis=0, keepdims=True)
        db_ref[...] += jnp.sum(dact, axis=0, keepdims=True)

    blk = lambda off: pl.BlockSpec((tm, tc), lambda j, i: (i, j + off))
    prev = lambda off: pl.BlockSpec((16, tc), lambda j, i: (jnp.maximum(i * (tm // 16) - 1, 0), j + off))
    nxt = lambda off: pl.BlockSpec((16, tc), lambda j, i: (jnp.minimum((i + 1) * (tm // 16), L // 16 - 1), j + off))
    const = lambda r: pl.BlockSpec((r, tc), lambda j, i: (0, j))
    return pl.pallas_call(
        body, grid=(ncb, nt),
        in_specs=[blk(0), prev(0), nxt(0), blk(ncb), nxt(ncb), blk(0), nxt(0), const(3), const(1)],
        out_specs=[pl.BlockSpec((2, tm, tc), lambda j, i: (0, i, j)), const(3), const(1)],
        out_shape=[jax.ShapeDtypeStruct((2, L, F), bf16),
                   jax.ShapeDtypeStruct((3, F), f32), jax.ShapeDtypeStruct((1, F), f32)],
        name="ffn_act_bwd", compiler_params=_cparams(("parallel", "arbitrary")))(hh, hh, hh, hh, hh, df, df, w, b)


def _merge_fwd(cfg, ya1, yb0, proj, wso, wco):
    L, D, SW, CW = cfg.L, cfg.D, cfg.SW, cfg.CW
    Ns = D // _NCHIP
    tm = _tile(L, 1024, 16)
    tn = _tile(Ns, 512, 128)
    nb = Ns // tn
    off_a = (SW + 3 * CW) // tn
    off_b = (SW + 3 * CW + D) // tn

    def body(a_ref, b_ref, wa_ref, wb_ref, ma_ref, mb_ref, m_ref, ya_ref, yb_ref):
        ya = jnp.dot(a_ref[...], wa_ref[...], preferred_element_type=f32)
        yb = jnp.dot(b_ref[...], wb_ref[...], preferred_element_type=f32)
        sa = jax.nn.sigmoid(ma_ref[...].astype(f32))
        sb = jax.nn.sigmoid(mb_ref[...].astype(f32))
        m_ref[...] = (sa * ya + sb * yb).astype(bf16)
        ya_ref[...] = ya.astype(bf16)
        yb_ref[...] = yb.astype(bf16)

    o_spec = pl.BlockSpec((tm, tn), lambda i, j: (i, j))
    o_shape = jax.ShapeDtypeStruct((L, D), bf16)
    return pl.pallas_call(
        body, grid=(L // tm, D // tn),
        in_specs=[pl.BlockSpec((tm, SW), lambda i, j: (i, 0)), pl.BlockSpec((tm, CW), lambda i, j: (i, 0)),
                  pl.BlockSpec((None, SW, tn), lambda i, j: (j // nb, 0, j % nb)),
                  pl.BlockSpec((None, CW, tn), lambda i, j: (j // nb, 0, j % nb)),
                  pl.BlockSpec((tm, tn), lambda i, j: (i, off_a + j)), pl.BlockSpec((tm, tn), lambda i, j: (i, off_b + j))],
        out_specs=[o_spec, o_spec, o_spec], out_shape=[o_shape, o_shape, o_shape],
        name="merge_fwd", compiler_params=_cparams(("parallel", "parallel")))(ya1, yb0, wso, wco, proj, proj)


def _s5_dims(cfg):
    G = cfg.SW // _SSM_GROUP
    NS = G // _SLAB_GROUPS
    SC = _SLAB_GROUPS * _SSM_GROUP
    SH = _SLAB_GROUPS * _SSM_STATE
    NST = 2 * SH * NS
    return G, NS, SC, SH, NST


def _lane_slabs(cfg, W):
    _, NS, _, SH, _ = _s5_dims(cfg)
    return [(2 * SH * s + w0, 2 * SH * s + SH + w0) for s in range(NS) for w0 in range(0, SH, W)]


def _discretize(a_re, a_im, log_dt, b_re, b_im):
    dt = jnp.exp(log_dt)[:, None]
    mag = jnp.exp(dt * a_re)
    abr = mag * jnp.cos(dt * a_im)
    abi = mag * jnp.sin(dt * a_im)
    nr = abr - 1.0
    ni = abi
    den = a_re * a_re + a_im * a_im
    fr = (nr * a_re + ni * a_im) / den
    fi = (ni * a_re - nr * a_im) / den
    bbr = fr[..., None] * b_re - fi[..., None] * b_im
    bbi = fr[..., None] * b_im + fi[..., None] * b_re
    return abr, abi, bbr, bbi


def _state_rows(cfg, re, im):
    _, NS, _, SH, _ = _s5_dims(cfg)
    return jnp.concatenate([re.reshape(NS, SH), im.reshape(NS, SH)], axis=1).reshape(-1)


def _s5_tables(cfg, abr, abi, bbr, bbi, c_re, c_im):
    G, NS, SC, SH, NST = _s5_dims(cfg)
    S = cfg.T // 8
    eye = jnp.eye(_SLAB_GROUPS, dtype=f32)
    bb = jnp.stack([bbr, bbi]).reshape(2, NS, _SLAB_GROUPS, _SSM_STATE, _SSM_GROUP)
    bs = jnp.einsum("rsgph,gq->sghrqp", bb, eye).reshape(NS, SC, 2 * SH).astype(bf16)
    cc = jnp.stack([c_re, -c_im]).reshape(2, NS, _SLAB_GROUPS, _SSM_GROUP, _SSM_STATE)
    cs = jnp.einsum("rsghp,gq->srqpgh", cc, eye).reshape(NS, 2 * SH, SC).astype(bf16)
    arep = jnp.broadcast_to(_state_rows(cfg, abr, abi)[None, :], (8, NST))
    pr, pi = abr, abi
    for _ in range(S - 1):
        pr, pi = pr * abr - pi * abi, pr * abi + pi * abr
    apow = jnp.broadcast_to(_state_rows(cfg, pr, pi)[None, :], (8, NST))
    t = jnp.arange(cfg.T)
    perm = (t % 8) * S + t // 8
    pm = jax.nn.one_hot(perm, cfg.T, dtype=bf16)
    return bs, cs, arep, apow, pm, pm.T


def _cmul_add(ar, ai, xr, xi, br, bi):
    return ar * xr - ai * xi + br, ar * xi + ai * xr + bi


def _s5_forward_chunk(cfg, W, upb, bs_ref, arep_ref, apow_ref, st, x0, cin_store):
    _, NS, SC, SH, _ = _s5_dims(cfg)
    S = cfg.T // 8
    for s in range(NS):
        st[:, 2 * SH * s:2 * SH * (s + 1)] = jnp.dot(upb[:, SC * s:SC * (s + 1)], bs_ref[s], preferred_element_type=f32)
    rows = lax.broadcasted_iota(jnp.int32, (8, W), 0)
    zero = jnp.zeros((8, W), f32)
    for rc, ic in _lane_slabs(cfg, W):
        ar = arep_ref[:, rc:rc + W]
        ai = arep_ref[:, ic:ic + W]

        def step(i, carry, rc=rc, ic=ic, ar=ar, ai=ai):
            xr, xi = carry
            r0 = pl.multiple_of(i * 8, 8)
            nr, ni = _cmul_add(ar, ai, xr, xi, st[pl.ds(r0, 8), rc:rc + W], st[pl.ds(r0, 8), ic:ic + W])
            st[pl.ds(r0, 8), rc:rc + W] = nr
            st[pl.ds(r0, 8), ic:ic + W] = ni
            return nr, ni

        er, ei = lax.fori_loop(0, S, step, (zero, zero))
        pr = apow_ref[:, rc:rc + W]
        pi = apow_ref[:, ic:ic + W]
        x0r = x0[:, rc:rc + W]
        x0i = x0[:, ic:ic + W]
        cr = jnp.where(rows == 0, x0r, 0.0)
        ci = jnp.where(rows == 0, x0i, 0.0)
        for _ in range(7):
            fr, fi = _cmul_add(pr, pi, cr, ci, er, ei)
            cr = jnp.where(rows == 0, x0r, pltpu.roll(fr, 1, axis=0))
            ci = jnp.where(rows == 0, x0i, pltpu.roll(fi, 1, axis=0))
        fr, fi = _cmul_add(pr, pi, cr, ci, er, ei)
        x0[:, rc:rc + W] = jnp.broadcast_to(fr[7:8, :], (8, W))
        x0[:, ic:ic + W] = jnp.broadcast_to(fi[7:8, :], (8, W))
        cin_store(rc, ic, cr, ci)

        def fix(i, carry, rc=rc, ic=ic, ar=ar, ai=ai):
            kr, ki = carry
            r0 = pl.multiple_of(i * 8, 8)
            nr, ni = ar * kr - ai * ki, ar * ki + ai * kr
            st[pl.ds(r0, 8), rc:rc + W] = st[pl.ds(r0, 8), rc:rc + W] + nr
            st[pl.ds(r0, 8), ic:ic + W] = st[pl.ds(r0, 8), ic:ic + W] + ni
            return nr, ni

        lax.fori_loop(0, S, fix, (cr, ci))


def _s5_fwd(cfg, proj, tabs, dskip):
    L, SW, T = cfg.L, cfg.SW, cfg.T
    G, NS, SC, SH, NST = _s5_dims(cfg)
    bs, cs, arep, apow, pm, pmt = tabs
    W = min(512, SH)
    NC = L // T

    def body(u_ref, pm_ref, pmt_ref, bs_ref, cs_ref, arep_ref, apow_ref, dskip_ref, y_ref, ya0_ref, cin_ref, st, x0):
        c = pl.program_id(0)

        @pl.when(c == 0)
        def _():
            x0[...] = jnp.zeros_like(x0)

        up = jnp.dot(pm_ref[...], u_ref[...], preferred_element_type=f32)
        upb = up.astype(bf16)

        def cin_store(rc, ic, cr, ci):
            cin_ref[0, :, rc:rc + W] = cr
            cin_ref[0, :, ic:ic + W] = ci

        _s5_forward_chunk(cfg, W, upb, bs_ref, arep_ref, apow_ref, st, x0, cin_store)
        yp = jnp.concatenate(
            [jnp.dot(st[:, 2 * SH * s:2 * SH * (s + 1)].astype(bf16), cs_ref[s], preferred_element_type=f32)
             for s in range(NS)], axis=1) + dskip_ref[...] * up
        y = jnp.dot(pmt_ref[...], yp.astype(bf16), preferred_element_type=f32)
        y_ref[...] = y.astype(bf16)
        ya0_ref[...] = _gelu(y).astype(bf16)

    const2 = lambda shape: pl.BlockSpec(shape, lambda c: (0, 0))
    const3 = lambda shape: pl.BlockSpec(shape, lambda c: (0, 0, 0))
    return pl.pallas_call(
        body, grid=(NC,),
        in_specs=[pl.BlockSpec((T, SW), lambda c: (c, 0)), const2((T, T)), const2((T, T)), const3((NS, SC, 2 * SH)),
                  const3((NS, 2 * SH, SC)), const2((8, NST)), const2((8, NST)), const2((1, SW))],
        out_specs=[pl.BlockSpec((T, SW), lambda c: (c, 0)), pl.BlockSpec((T, SW), lambda c: (c, 0)),
                   pl.BlockSpec((1, 8, NST), lambda c: (c, 0, 0))],
        out_shape=[jax.ShapeDtypeStruct((L, SW), bf16), jax.ShapeDtypeStruct((L, SW), bf16),
                   jax.ShapeDtypeStruct((NC, 8, NST), f32)],
        scratch_shapes=[pltpu.VMEM((T, NST), f32), pltpu.VMEM((8, NST), f32)],
        name="s5_fwd", compiler_params=_cparams(("arbitrary",)))(proj, pm, pmt, bs, cs, arep, apow, dskip)


def _s5_bwd(cfg, proj, dy, cin, tabs, dskip, dproj, after=()):
    L, SW, T = cfg.L, cfg.SW, cfg.T
    du_col = 3 * cfg.CW // SW
    G, NS, SC, SH, NST = _s5_dims(cfg)
    bs, cs, arep, apow, pm, pmt = tabs
    W = min(512, SH)
    S = T // 8
    NC = L // T

    def body(u_ref, dy_ref, cin_ref, pm_ref, pmt_ref, bs_ref, cs_ref, arep_ref, apow_ref, dskip_ref, dproj_ref, *rest):
        du_ref, da_ref, db_hbm, dc_hbm, dd_ref, st, gs, x0, g0, db_acc, dc_acc, sem = rest[len(after):]
        c = pl.program_id(0)

        @pl.when(c == 0)
        def _():
            g0[...] = jnp.zeros_like(g0)
            da_ref[...] = jnp.zeros_like(da_ref)
            dd_ref[...] = jnp.zeros_like(dd_ref)
            db_acc[...] = jnp.zeros_like(db_acc)
            dc_acc[...] = jnp.zeros_like(dc_acc)

        up = jnp.dot(pm_ref[...], u_ref[...], preferred_element_type=f32)
        upb = up.astype(bf16)
        dyp = jnp.dot(pm_ref[...], dy_ref[...], preferred_element_type=f32)
        dypb = dyp.astype(bf16)
        x0[...] = jnp.zeros_like(x0)
        for rc, ic in _lane_slabs(cfg, W):
            x0[:, rc:rc + W] = jnp.broadcast_to(cin_ref[0, 0:1, rc:rc + W], (8, W))
            x0[:, ic:ic + W] = jnp.broadcast_to(cin_ref[0, 0:1, ic:ic + W], (8, W))
        _s5_forward_chunk(cfg, W, upb, bs_ref, arep_ref, apow_ref, st, x0, lambda *a: None)
        for s in range(NS):
            gs[:, 2 * SH * s:2 * SH * (s + 1)] = lax.dot_general(
                dypb[:, SC * s:SC * (s + 1)], cs_ref[s], (((1,), (1,)), ((), ())), preferred_element_type=f32)
        rows = lax.broadcasted_iota(jnp.int32, (8, W), 0)
        zero = jnp.zeros((8, W), f32)
        for rc, ic in _lane_slabs(cfg, W):
            ar = arep_ref[:, rc:rc + W]
            ai = arep_ref[:, ic:ic + W]

            def rstep(k, carry, rc=rc, ic=ic, ar=ar, ai=ai):
                gr, gi = carry
                r0 = pl.multiple_of((S - 1 - k) * 8, 8)
                nr = ar * gr + ai * gi + gs[pl.ds(r0, 8), rc:rc + W]
                ni = ar * gi - ai * gr + gs[pl.ds(r0, 8), ic:ic + W]
                gs[pl.ds(r0, 8), rc:rc + W] = nr
                gs[pl.ds(r0, 8), ic:ic + W] = ni
                return nr, ni

            er, ei = lax.fori_loop(0, S, rstep, (zero, zero))
            pr = apow_ref[:, rc:rc + W]
            pi = apow_ref[:, ic:ic + W]
            g0r = g0[:, rc:rc + W]
            g0i = g0[:, ic:ic + W]
            cr = jnp.where(rows == 7, g0r, 0.0)
            ci = jnp.where(rows == 7, g0i, 0.0)
            for _ in range(7):
                fr = er + pr * cr + pi * ci
                fi = ei + pr * ci - pi * cr
                cr = jnp.where(rows == 7, g0r, pltpu.roll(fr, 7, axis=0))
                ci = jnp.where(rows == 7, g0i, pltpu.roll(fi, 7, axis=0))
            fr = er + pr * cr + pi * ci
            fi = ei + pr * ci - pi * cr
            g0[:, rc:rc + W] = jnp.broadcast_to(fr[0:1, :], (8, W))
            g0[:, ic:ic + W] = jnp.broadcast_to(fi[0:1, :], (8, W))

            def fix(k, carry, rc=rc, ic=ic, ar=ar, ai=ai):
                kr, ki, accr, acci = carry
                i = S - 1 - k
                r0 = pl.multiple_of(i * 8, 8)
                rp = pl.multiple_of((i - 1) * 8, 8)
                nr = ar * kr + ai * ki
                ni = ar * ki - ai * kr
                gr = gs[pl.ds(r0, 8), rc:rc + W] + nr
                gi = gs[pl.ds(r0, 8), ic:ic + W] + ni
                gs[pl.ds(r0, 8), rc:rc + W] = gr
                gs[pl.ds(r0, 8), ic:ic + W] = gi
                xr = st[pl.ds(rp, 8), rc:rc + W]
                xi = st[pl.ds(rp, 8), ic:ic + W]
                return nr, ni, accr + gr * xr + gi * xi, acci + gi * xr - gr * xi

            kr, ki, accr, acci = lax.fori_loop(0, S - 1, fix, (cr, ci, zero, zero))
            nr = ar * kr + ai * ki
            ni = ar * ki - ai * kr
            gr = gs[0:8, rc:rc + W] + nr
            gi = gs[0:8, ic:ic + W] + ni
            gs[0:8, rc:rc + W] = gr
            gs[0:8, ic:ic + W] = gi
            xr = cin_ref[0, :, rc:rc + W]
            xi = cin_ref[0, :, ic:ic + W]
            da_ref[:, rc:rc + W] += accr + gr * xr + gi * xi
            da_ref[:, ic:ic + W] += acci + gi * xr - gr * xi

        dups = []
        for s in range(NS):
            gsb = gs[:, 2 * SH * s:2 * SH * (s + 1)].astype(bf16)
            dups.append(lax.dot_general(gsb, bs_ref[s], (((1,), (1,)), ((), ())), preferred_element_type=f32))
            db_acc[s] += lax.dot_general(upb[:, SC * s:SC * (s + 1)], gsb, (((0,), (0,)), ((), ())),
                                         preferred_element_type=f32)
            dc_acc[s] += lax.dot_general(st[:, 2 * SH * s:2 * SH * (s + 1)].astype(bf16), dypb[:, SC * s:SC * (s + 1)],
                                         (((0,), (0,)), ((), ())), preferred_element_type=f32)
        dup = jnp.concatenate(dups, axis=1) + dskip_ref[...] * dyp
        du_ref[...] = jnp.dot(pmt_ref[...], dup.astype(bf16), preferred_element_type=f32).astype(bf16)
        dd_ref[...] += jnp.sum(dyp * up, axis=0, keepdims=True)

        @pl.when(c == NC - 1)
        def _():
            cp1 = pltpu.make_async_copy(db_acc, db_hbm, sem.at[0])
            cp2 = pltpu.make_async_copy(dc_acc, dc_hbm, sem.at[1])
            cp1.start()
            cp2.start()
            cp1.wait()
            cp2.wait()

    rev = lambda c: (NC - 1 - c, 0)
    const2 = lambda shape: pl.BlockSpec(shape, lambda c: (0, 0))
    const3 = lambda shape: pl.BlockSpec(shape, lambda c: (0, 0, 0))
    return pl.pallas_call(
        body, grid=(NC,),
        in_specs=[pl.BlockSpec((T, SW), rev), pl.BlockSpec((T, SW), rev), pl.BlockSpec((1, 8, NST), lambda c: (NC - 1 - c, 0, 0)),
                  const2((T, T)), const2((T, T)), const3((NS, SC, 2 * SH)), const3((NS, 2 * SH, SC)),
                  const2((8, NST)), const2((8, NST)), const2((1, SW)), _ANY] + [_ANY] * len(after),
        out_specs=[pl.BlockSpec((T, SW), lambda c: (NC - 1 - c, du_col)), const2((8, NST)),
                   pl.BlockSpec(memory_space=pl.ANY), pl.BlockSpec(memory_space=pl.ANY), const2((1, SW))],
        out_shape=[jax.ShapeDtypeStruct(dproj.shape, bf16), jax.ShapeDtypeStruct((8, NST), f32),
                   jax.ShapeDtypeStruct((NS, SC, 2 * SH), f32), jax.ShapeDtypeStruct((NS, 2 * SH, SC), f32),
                   jax.ShapeDtypeStruct((1, SW), f32)],
        scratch_shapes=[pltpu.VMEM((T, NST), f32), pltpu.VMEM((T, NST), f32), pltpu.VMEM((8, NST), f32),
                        pltpu.VMEM((8, NST), f32), pltpu.VMEM((NS, SC, 2 * SH), f32), pltpu.VMEM((NS, 2 * SH, SC), f32),
                        pltpu.SemaphoreType.DMA((2,))],
        input_output_aliases={10: 0},
        name="s5_bwd", compiler_params=_cparams(("arbitrary",)))(proj, dy, cin, pm, pmt, bs, cs, arep, apow, dskip, dproj, *after)


def _s5_param_grads(cfg, da, db_full, dc_full):
    G, NS, SC, SH, NST = _s5_dims(cfg)
    das = da.sum(axis=0).reshape(NS, 2, SH)
    dabr = das[:, 0].reshape(G, _SSM_STATE)
    dabi = das[:, 1].reshape(G, _SSM_STATE)
    dbf = db_full.reshape(NS, _SLAB_GROUPS, _SSM_GROUP, 2, _SLAB_GROUPS, _SSM_STATE)
    dbb = jnp.einsum("sghrgp->rsgph", dbf).reshape(2, G, _SSM_STATE, _SSM_GROUP)
    dcf = dc_full.reshape(NS, 2, _SLAB_GROUPS, _SSM_STATE, _SLAB_GROUPS, _SSM_GROUP)
    dcc = jnp.einsum("srgpgh->rsghp", dcf).reshape(2, G, _SSM_GROUP, _SSM_STATE)
    return dabr, dabi, dbb[0], dbb[1], dcc[0], -dcc[1]


def _coords():
    return lax.axis_index("x"), lax.axis_index("y"), lax.axis_index("c")


def _other_chips(x, y):
    return [(1 - x, y), (x, 1 - y), (1 - x, 1 - y)]


def _allreduce8(name, v):
    R = v.shape[0]

    def body(v_ref, o_ref, sib, chips, mine, ssem, rsem):
        x, y, c = _coords()
        d2d = pltpu.make_async_remote_copy(src_ref=v_ref, dst_ref=sib, send_sem=ssem.at[0], recv_sem=rsem.at[0],
                                           device_id=(x, y, 1 - c), device_id_type=_MESH)
        d2d.start()
        d2d.wait()
        mine[...] = v_ref[...] + sib[...]
        cps = [pltpu.make_async_remote_copy(src_ref=mine, dst_ref=chips.at[j], send_sem=ssem.at[1 + j],
                                            recv_sem=rsem.at[1 + j], device_id=(*chip, c), device_id_type=_MESH)
               for j, chip in enumerate(_other_chips(x, y))]
        for cp in cps:
            cp.start()
        for cp in cps:
            cp.wait()
        o_ref[...] = (mine[...] + chips[1]) + (chips[0] + chips[2])

    vm = pl.BlockSpec(memory_space=pltpu.VMEM)
    return pl.pallas_call(
        body, in_specs=[vm], out_specs=vm, out_shape=jax.ShapeDtypeStruct((R, 128), f32),
        scratch_shapes=[pltpu.VMEM((R, 128), f32), pltpu.VMEM((3, R, 128), f32), pltpu.VMEM((R, 128), f32),
                        pltpu.SemaphoreType.DMA((4,)), pltpu.SemaphoreType.DMA((4,))],
        name=name, compiler_params=pltpu.CompilerParams(vmem_limit_bytes=_VMEM_LIMIT))(v)


def _cast_into_slot(name, w, k_idx):
    R, C = w.shape
    tr = _tile(R, 256, 16)

    def body(k_ref, w_ref, o_ref):
        o_ref[...] = w_ref[...].astype(bf16)

    gs = pltpu.PrefetchScalarGridSpec(
        num_scalar_prefetch=1, grid=(R // tr,),
        in_specs=[pl.BlockSpec((tr, C), lambda r, kr: (r, 0))],
        out_specs=pl.BlockSpec((None, tr, C), lambda r, kr: (kr[0], r, 0)))
    return pl.pallas_call(body, grid_spec=gs, out_shape=jax.ShapeDtypeStruct((_NCHIP, R, C), bf16), name=name,
                          compiler_params=_cparams(("parallel",)))(k_idx, w)


def _handshake(peers):
    barrier = pltpu.get_barrier_semaphore()
    for peer in peers:
        pl.semaphore_signal(barrier, inc=1, device_id=peer, device_id_type=_MESH)
    pl.semaphore_wait(barrier, len(peers))


def _allgather_weights(name, bufs, collective_id):
    n = len(bufs)
    refs = [jax.new_ref(b, memory_space=pltpu.MemorySpace.HBM) for b in bufs]

    @pl.kernel(mesh=plsc.ScalarSubcoreMesh(axis_name="seq", num_cores=1), name=name,
               scratch_types=(pltpu.SemaphoreType.DMA((n, 3)), pltpu.SemaphoreType.DMA((n, 3)),
                              pltpu.SemaphoreType.DMA((n, 3)), pltpu.SemaphoreType.DMA((n, 3))),
               compiler_params=pltpu.CompilerParams(collective_id=collective_id))
    def launch(ssem, rsem, fssem, frsem):
        x, y, c = _coords()
        k = 2 * x + y
        others = _other_chips(x, y)
        _handshake([(x, y, 1 - c)] + [(*chip, c) for chip in others])
        sends, fwds = [], []
        for w in range(n):
            rh = refs[w].shape[1] // 2
            mine = refs[w].at[k, pl.ds(c * rh, rh)]
            for j, chip in enumerate(others):
                cp = pltpu.make_async_remote_copy(
                    src_ref=mine, dst_ref=mine, send_sem=ssem.at[w, j], recv_sem=rsem.at[w, j],
                    device_id=(*chip, c), device_id_type=_MESH)
                cp.start()
                sends.append(cp)
        for w in range(n):
            rh = refs[w].shape[1] // 2
            for j, (ox, oy) in enumerate(others):
                landed = refs[w].at[2 * ox + oy, pl.ds(c * rh, rh)]
                pltpu.make_async_remote_copy(
                    src_ref=landed, dst_ref=landed, send_sem=ssem.at[w, j], recv_sem=rsem.at[w, j],
                    device_id=(ox, oy, c), device_id_type=_MESH).wait_recv()
                cp = pltpu.make_async_remote_copy(
                    src_ref=landed, dst_ref=landed, send_sem=fssem.at[w, j], recv_sem=frsem.at[w, j],
                    device_id=(x, y, 1 - c), device_id_type=_MESH)
                cp.start()
                fwds.append(cp)
        for w in range(n):
            rh = refs[w].shape[1] // 2
            for j, (ox, oy) in enumerate(others):
                passed = refs[w].at[2 * ox + oy, pl.ds((1 - c) * rh, rh)]
                pltpu.make_async_remote_copy(
                    src_ref=passed, dst_ref=passed, send_sem=fssem.at[w, j], recv_sem=frsem.at[w, j],
                    device_id=(x, y, 1 - c), device_id_type=_MESH).wait_recv()
        for cp in sends + fwds:
            cp.wait_send()

    launch()
    return [r[...] for r in refs]


def _sequencer_kernel(name, collective_id, sems, body):
    pl.kernel(body, mesh=plsc.ScalarSubcoreMesh(axis_name="seq", num_cores=1), name=name, scratch_types=sems,
              compiler_params=pltpu.CompilerParams(collective_id=collective_id))()


def _hbm_ref(a):
    return jax.new_ref(a, memory_space=pltpu.MemorySpace.HBM)


def _exchange_halves(name, grads, collective_id):
    n = len(grads)
    srcs = [_hbm_ref(g) for g in grads]
    dsts = [jax.empty_ref(jax.ShapeDtypeStruct((g.shape[0], g.shape[1] // 2, g.shape[2]), g.dtype),
                          memory_space=pltpu.MemorySpace.HBM) for g in grads]

    def body(ssem, rsem):
        x, y, c = _coords()
        _handshake([(x, y, 1 - c)])
        cps = []
        for w in range(n):
            rh = srcs[w].shape[1] // 2
            cp = pltpu.make_async_remote_copy(
                src_ref=srcs[w].at[:, pl.ds((1 - c) * rh, rh)], dst_ref=dsts[w], send_sem=ssem.at[w], recv_sem=rsem.at[w],
                device_id=(x, y, 1 - c), device_id_type=_MESH)
            cp.start()
            cps.append(cp)
        for cp in cps:
            cp.wait()

    _sequencer_kernel(name, collective_id, (pltpu.SemaphoreType.DMA((n,)), pltpu.SemaphoreType.DMA((n,))), body)
    return [d[...] for d in dsts]


def _scatter_shards(name, parts, collective_id):
    n = len(parts)
    srcs = [_hbm_ref(p) for p in parts]
    dsts = [jax.empty_ref(jax.ShapeDtypeStruct((3,) + p.shape[1:], p.dtype), memory_space=pltpu.MemorySpace.HBM)
            for p in parts]

    def body(ssem, rsem):
        x, y, c = _coords()
        others = _other_chips(x, y)
        _handshake([(*chip, c) for chip in others])
        cps = []
        for w in range(n):
            for j, (ox, oy) in enumerate(others):
                cp = pltpu.make_async_remote_copy(
                    src_ref=srcs[w].at[2 * ox + oy], dst_ref=dsts[w].at[j], send_sem=ssem.at[w, j], recv_sem=rsem.at[w, j],
                    device_id=(ox, oy, c), device_id_type=_MESH)
                cp.start()
                cps.append(cp)
        for cp in cps:
            cp.wait()

    _sequencer_kernel(name, collective_id, (pltpu.SemaphoreType.DMA((n, 3)), pltpu.SemaphoreType.DMA((n, 3))), body)
    return [d[...] for d in dsts]


def _join_halves(name, bufs, collective_id):
    n = len(bufs)
    refs = [_hbm_ref(b) for b in bufs]

    def body(ssem, rsem):
        x, y, c = _coords()
        _handshake([(x, y, 1 - c)])
        cps = []
        for w in range(n):
            rh = refs[w].shape[0] // 2
            mine = refs[w].at[pl.ds(c * rh, rh)]
            cp = pltpu.make_async_remote_copy(src_ref=mine, dst_ref=mine, send_sem=ssem.at[w], recv_sem=rsem.at[w],
                                              device_id=(x, y, 1 - c), device_id_type=_MESH)
            cp.start()
            cps.append(cp)
        for w in range(n):
            rh = refs[w].shape[0] // 2
            theirs = refs[w].at[pl.ds((1 - c) * rh, rh)]
            pltpu.make_async_remote_copy(src_ref=theirs, dst_ref=theirs, send_sem=ssem.at[w], recv_sem=rsem.at[w],
                                         device_id=(x, y, 1 - c), device_id_type=_MESH).wait_recv()
        for cp in cps:
            cp.wait_send()

    _sequencer_kernel(name, collective_id, (pltpu.SemaphoreType.DMA((n,)), pltpu.SemaphoreType.DMA((n,))), body)
    return [r[...] for r in refs]


def _add_own_half(name, g, t, c_idx, after):
    S, R, C = g.shape
    rh = R // 2
    tr = _tile(rh, 256, 16)
    nrb = rh // tr

    def body(c_ref, g_ref, t_ref, *rest):
        rest[-1][...] = (g_ref[...] + t_ref[...]).astype(bf16)

    gs = pltpu.PrefetchScalarGridSpec(
        num_scalar_prefetch=1, grid=(S, nrb),
        in_specs=[pl.BlockSpec((None, tr, C), lambda s, r, cr: (s, cr[0] * nrb + r, 0)),
                  pl.BlockSpec((None, tr, C), lambda s, r, cr: (s, r, 0))] + [_ANY] * len(after),
        out_specs=pl.BlockSpec((None, tr, C), lambda s, r, cr: (s, r, 0)))
    return pl.pallas_call(body, grid_spec=gs, out_shape=jax.ShapeDtypeStruct((S, rh, C), bf16), name=name,
                          compiler_params=_cparams(("parallel", "parallel")))(c_idx, g, t, *after)


def _add_shard_parts(name, g, t, r, kc_idx, after):
    S, R, C = g.shape
    rh = R // 2
    tr = _tile(rh, 256, 16)
    nrb = rh // tr

    def body(kc_ref, g_ref, t_ref, r_ref, *rest):
        own = g_ref[...] + t_ref[...]
        rest[-1][...] = (own + r_ref[1].astype(f32)) + (r_ref[0].astype(f32) + r_ref[2].astype(f32))

    gs = pltpu.PrefetchScalarGridSpec(
        num_scalar_prefetch=1, grid=(nrb,),
        in_specs=[pl.BlockSpec((None, tr, C), lambda i, kc: (kc[0], kc[1] * nrb + i, 0)),
                  pl.BlockSpec((None, tr, C), lambda i, kc: (kc[0], i, 0)),
                  pl.BlockSpec((3, tr, C), lambda i, kc: (0, i, 0))] + [_ANY] * len(after),
        out_specs=pl.BlockSpec((tr, C), lambda i, kc: (kc[1] * nrb + i, 0)))
    return pl.pallas_call(body, grid_spec=gs, out_shape=jax.ShapeDtypeStruct((R, C), f32), name=name,
                          compiler_params=_cparams(("parallel",)))(kc_idx, g, t, r, *after)


def _adamw(name, w, g, m, v, after=()):
    R, C = w.shape
    tr = _tile(R, 128, 8)

    def body(w_ref, g_ref, m_ref, v_ref, *rest):
        d_ref, nm_ref, nv_ref = rest[len(after):]
        gv = g_ref[...]
        nm = _ADAM_B1 * m_ref[...] + (1.0 - _ADAM_B1) * gv
        nv = _ADAM_B2 * v_ref[...] + (1.0 - _ADAM_B2) * (gv * gv)
        m_hat = nm / (1.0 - _ADAM_B1 ** _ADAM_STEP)
        v_hat = nv / (1.0 - _ADAM_B2 ** _ADAM_STEP)
        d_ref[...] = -_ADAM_LR * (m_hat / (jnp.sqrt(v_hat) + _ADAM_EPS) + _ADAM_WD * w_ref[...])
        nm_ref[...] = nm
        nv_ref[...] = nv

    spec = pl.BlockSpec((tr, C), lambda i: (i, 0))
    shape = jax.ShapeDtypeStruct((R, C), f32)
    return pl.pallas_call(body, grid=(R // tr,), in_specs=[spec] * 4 + [_ANY] * len(after), out_specs=[spec] * 3,
                          out_shape=[shape] * 3, name=name, compiler_params=_cparams(("parallel",)))(w, g, m, v, *after)


def _pack(arrs):
    flat = jnp.concatenate([a.reshape(-1).astype(f32) for a in arrs])
    n = flat.shape[0]
    pad = (-n) % (128 * 128)
    return jnp.pad(flat, (0, pad)).reshape(-1, 128)


def _unpack(packed, shapes):
    flat = packed.reshape(-1)
    out, off = [], 0
    for s in shapes:
        n = math.prod(s)
        out.append(flat[off:off + n].reshape(s))
        off += n
    return out


_BIG = ("w_in", "w_glu", "w_ssm_out", "w_conv_out", "w_o", "w_up", "w_down")
_SMALL = ("norm_tok", "a_re", "a_im", "log_dt", "b_re", "b_im", "c_re", "c_im", "d_skip", "conv_w", "conv_b",
          "norm_ffn", "ffn_conv_w", "ffn_conv_b", "norm_final")
_WEIGHTS = ("norm_tok", "w_in", "a_re", "a_im", "log_dt", "b_re", "b_im", "c_re", "c_im", "d_skip", "w_glu",
            "w_ssm_out", "conv_w", "conv_b", "w_conv_out", "w_o", "norm_ffn", "w_up", "ffn_conv_w", "ffn_conv_b",
            "w_down", "norm_final")


def _step(cfg, x, tgt, p, m, v):
    L, D, SW, CW, F = cfg.L, cfg.D, cfg.SW, cfg.CW, cfg.F
    xi, yi, ci = _coords()
    k_idx = (2 * xi + yi).astype(jnp.int32).reshape(1)
    c_idx = ci.astype(jnp.int32).reshape(1)
    x = x.reshape(L, D)
    tgt = tgt.reshape(L, D)

    big2d = {n: p[n].reshape(p[n].shape[-2], p[n].shape[-1]) for n in _BIG}
    slots = {n: _cast_into_slot("cast_" + n, big2d[n], k_idx) for n in _BIG}
    wg = {}
    for cid, (gname, group) in enumerate((("allgather_w_in", ("w_in",)),
                                          ("allgather_w_mixer", ("w_glu", "w_ssm_out", "w_conv_out", "w_o")),
                                          ("allgather_w_up", ("w_up",)), ("allgather_w_down", ("w_down",)))):
        wg.update(zip(group, _allgather_weights(gname, [slots[n] for n in group], cid)))
    w_in, w_so, w_co, w_up = wg["w_in"], wg["w_ssm_out"], wg["w_conv_out"], wg["w_up"]
    w_glu, w_o, w_down = wg["w_glu"], wg["w_o"], wg["w_down"]
    kk = k_idx[0]
    cw_full = lax.dynamic_update_slice(jnp.zeros((3, CW), f32), p["conv_w"].reshape(3, CW // _NCHIP), (0, kk * (CW // _NCHIP)))
    fw_full = lax.dynamic_update_slice(jnp.zeros((3, F), f32), p["ffn_conv_w"].reshape(3, F // _NCHIP), (0, kk * (F // _NCHIP)))
    south = (ci == 0).astype(f32)
    conv_w, ffn_conv_w = _unpack(_allreduce8("allgather_conv_filters", _pack([cw_full * south, fw_full * south])),
                                 [(3, CW), (3, F)])
    conv_b = p["conv_b"].reshape(1, CW)
    ffn_conv_b = p["ffn_conv_b"].reshape(1, F)
    norm_tok = p["norm_tok"].reshape(1, D)
    norm_ffn = p["norm_ffn"].reshape(1, D)
    norm_final = p["norm_final"].reshape(1, D)
    dskip = p["d_skip"].reshape(1, SW)

    s5_in = (p["a_re"][0], p["a_im"][0], p["log_dt"][0], p["b_re"][0], p["b_im"][0])
    (abr, abi, bbr, bbi), disc_vjp = jax.vjp(_discretize, *s5_in)
    tabs = _s5_tables(cfg, abr, abi, bbr, bbi, p["c_re"][0], p["c_im"][0])

    xn1, r1 = _rms_fwd("rms_tok", x, norm_tok)
    proj = _mm_nn("in_proj", xn1, w_in, [bf16])[0]
    y_s, ya0, cin = _s5_fwd(cfg, proj, tabs, dskip)

    def tiles(*arrs):
        return lambda tm, tn: [(a, pl.BlockSpec((tm, tn), lambda i, j: (i, j))) for a in arrs]

    def glu_epi(acc, e, o):
        o[0][...] = (e[0][...].astype(f32) * jax.nn.sigmoid(acc)).astype(bf16)
        o[1][...] = acc.astype(bf16)

    ya1, z = _mm_nn("glu", ya0, w_glu, [bf16, bf16], rows=True, extras_fn=tiles(ya0), epilogue=glu_epi)
    yb0 = _convb_fwd(cfg, proj, conv_w, conv_b)
    merged, ya, yb = _merge_fwd(cfg, ya1, yb0, proj, w_so, w_co)

    def res_epi(acc, e, o):
        o[0][...] = e[0][...] + acc

    h1 = _mm_nn("out_proj", merged, w_o, [f32], rows=True, extras_fn=tiles(x), epilogue=res_epi)[0]
    xn2, r2 = _rms_fwd("rms_ffn", h1, norm_ffn)
    hh = _mm_nn("ffn_up", xn2, w_up, [bf16], tn=1408)[0]
    fact = _ffn_act(cfg, hh, ffn_conv_w, ffn_conv_b)
    h2 = _mm_nn("ffn_down", fact, w_down, [f32], tm=512, rows=True, extras_fn=tiles(h1), epilogue=res_epi)[0]
    dh2, dh2b, g_norm_final, loss_tile = _loss_head("loss_head", h2, tgt, norm_final)

    kc_idx = jnp.concatenate([k_idx, c_idx])
    reduced, chains = {}, {}

    def rs_halves(tag, collective_id, names, gs):
        chains[tag] = dict(cid=collective_id, names=names, gs=gs,
                           sib=_exchange_halves("grad_halves_" + tag, gs, collective_id))

    def rs_shards(tag, after):
        ch = chains[tag]
        ch["parts"] = [_add_own_half("grad_add_halves_" + n, g, t, c_idx, after)
                       for n, g, t in zip(ch["names"], ch["gs"], ch["sib"])]
        ch["chips"] = _scatter_shards("grad_shards_" + tag, ch["parts"], ch["cid"] + 1)
        return ch["parts"]

    def rs_join(tag, after):
        ch = chains[tag]
        ch["halves"] = [_add_shard_parts("grad_add_chips_" + n, g, t, r, kc_idx, after)
                        for n, g, t, r in zip(ch["names"], ch["gs"], ch["sib"], ch["chips"])]
        reduced.update(zip(ch["names"], _join_halves("grad_join_" + tag, ch["halves"], ch["cid"] + 2)))
        return ch["halves"]

    df = _mm_nt("ffn_down_dx", dh2b, w_down, [bf16], tn=1408, rows=True)[0]
    g_w_down = _mm_tn("ffn_down_dw", fact, dh2b, tm=1408, tn=512)
    rs_halves("ffn_down", 4, ["w_down"], [g_w_down.reshape(_NCHIP, F // _NCHIP, D)])
    dhh, g_ffn_conv_w, g_ffn_conv_b = _ffn_act_bwd(cfg, hh, df, ffn_conv_w, ffn_conv_b)
    sent = rs_shards("ffn_down", [dhh])
    g_w_up = _mm_tn("ffn_up_dw", xn2, dhh, shards=_NCHIP, tm=512, tn=1408, b_resident=True, after=sent)
    rs_halves("ffn_up", 7, ["w_up"], [g_w_up])
    dxn2 = _ffn_up_dx(cfg, dhh, w_up)
    sent = rs_shards("ffn_up", [dxn2]) + rs_join("ffn_down", [dxn2])
    dh1, dh1b, g_norm_ffn = _rms_bwd("rms_ffn_bwd", dxn2, h1, r2, norm_ffn, dh2, after=sent)

    dya, dyb, dproj = _out_proj_dx(cfg, dh1b, w_o, ya, yb, proj)
    g_w_o = _mm_tn("out_proj_dw", merged, dh1b)

    def glu_bwd_epi(acc, e, o):
        a0 = e[0][...].astype(f32)
        s = jax.nn.sigmoid(e[1][...].astype(f32))
        o[0][...] = (acc * a0 * s * (1.0 - s)).astype(bf16)
        o[1][...] = (acc * s).astype(bf16)

    dz, t1 = _mm_nt("ssm_out_dx", dya, w_so, [bf16, bf16], extras_fn=tiles(ya0, z), epilogue=glu_bwd_epi)
    g_w_so = _mm_tn("ssm_out_dw", ya1, dya, shards=_NCHIP, tn=512)
    dyb0 = _mm_nt("conv_out_dx", dyb, w_co, [bf16])[0]
    g_w_co = _mm_tn("conv_out_dw", yb0, dyb, shards=_NCHIP, tn=512)
    dproj, g_conv_w, g_conv_b = _convb_bwd(cfg, proj, dyb0, conv_w, conv_b, dproj)

    def gelu_bwd_epi(acc, e, o):
        o[0][...] = ((e[0][...].astype(f32) + acc) * _gelu_grad(e[1][...].astype(f32))).astype(bf16)

    dy_s = _mm_nt("glu_dx", dz, w_glu, [bf16], rows=True, extras_fn=tiles(t1, y_s), epilogue=gelu_bwd_epi)[0]
    g_w_glu = _mm_tn("glu_dw", ya0, dz)
    rs_halves("mixer", 10, ["w_o", "w_ssm_out", "w_conv_out", "w_glu"],
              [g_w_o.reshape(_NCHIP, D // _NCHIP, D), g_w_so, g_w_co, g_w_glu.reshape(_NCHIP, SW // _NCHIP, SW)])
    sent = rs_join("ffn_up", [g_w_glu])
    dproj, da_acc, db_full, dc_full, g_dskip = _s5_bwd(cfg, proj, dy_s, cin, tabs, dskip, dproj, after=sent)
    sent = rs_shards("mixer", [dproj])
    g_w_in = _mm_tn("in_proj_dw", xn1, dproj, shards=_NCHIP, tn=CW, after=sent,
                    b_block=lambda j: jnp.where(j == 0, 3 * CW // SW, jnp.where(j < 4, j - 1, j)))
    rs_halves("in_proj", 13, ["w_in"], [g_w_in])
    dxn1 = _in_proj_dx(cfg, dproj, w_in)
    sent = rs_shards("in_proj", [dxn1]) + rs_join("mixer", [dxn1])
    dx, _, g_norm_tok = _rms_bwd("rms_tok_bwd", dxn1, x, r1, norm_tok, dh1, after=sent)

    dabr, dabi, dbbr, dbbi, g_c_re, g_c_im = _s5_param_grads(cfg, da_acc, db_full, dc_full)
    g_a_re, g_a_im, g_log_dt, g_b_re, g_b_im = disc_vjp((dabr, dabi, dbbr, dbbi))

    small_g = {"norm_tok": g_norm_tok, "a_re": g_a_re, "a_im": g_a_im, "log_dt": g_log_dt, "b_re": g_b_re, "b_im": g_b_im,
               "c_re": g_c_re, "c_im": g_c_im, "d_skip": g_dskip, "conv_w": g_conv_w, "conv_b": g_conv_b,
               "norm_ffn": g_norm_ffn, "ffn_conv_w": g_ffn_conv_w, "ffn_conv_b": g_ffn_conv_b, "norm_final": g_norm_final}
    small_shapes = [small_g[n].shape for n in _SMALL]
    summed = dict(zip(_SMALL, _unpack(_allreduce8("allreduce_small_grads", _pack([small_g[n] for n in _SMALL])), small_shapes)))
    summed["conv_w"] = lax.dynamic_slice(summed["conv_w"], (0, kk * (CW // _NCHIP)), (3, CW // _NCHIP))
    summed["ffn_conv_w"] = lax.dynamic_slice(summed["ffn_conv_w"], (0, kk * (F // _NCHIP)), (3, F // _NCHIP))

    grads, deltas, new_m, new_v = {}, {}, {}, {}

    def adamw_big(names, after):
        for n in names:
            d_, m_, v_ = _adamw("adamw_" + n, big2d[n], reduced[n], m[n].reshape(big2d[n].shape),
                                v[n].reshape(big2d[n].shape), after=after)
            grads[n] = reduced[n].reshape(p[n].shape)
            deltas[n], new_m[n], new_v[n] = d_.reshape(p[n].shape), m_.reshape(p[n].shape), v_.reshape(p[n].shape)
            after = [d_]
        return after

    shapes = [p[n].shape for n in _SMALL]
    sg = _pack([summed[n] for n in _SMALL])
    d_, m_, v_ = _adamw("adamw_small", _pack([p[n] for n in _SMALL]), sg, _pack([m[n] for n in _SMALL]),
                        _pack([v[n] for n in _SMALL]))
    for n, dd, mm, vv in zip(_SMALL, _unpack(d_, shapes), _unpack(m_, shapes), _unpack(v_, shapes)):
        grads[n] = summed[n].reshape(p[n].shape)
        deltas[n], new_m[n], new_v[n] = dd, mm, vv
    done = adamw_big(["w_down", "w_up", "w_o", "w_ssm_out", "w_conv_out", "w_glu"], [d_])
    rs_join("in_proj", done)
    adamw_big(["w_in"], ())

    loss = lax.psum(loss_tile[0, 0], ("x", "y", "c"))
    return (loss, dx.reshape(1, L, D), *[grads[n] for n in _WEIGHTS], *[deltas[n] for n in _WEIGHTS],
            *[new_m[n] for n in _WEIGHTS], *[new_v[n] for n in _WEIGHTS])


def kernel(x, norm_tok, w_in, a_re, a_im, log_dt, b_re, b_im, c_re, c_im, d_skip, w_glu, w_ssm_out, conv_w, conv_b, w_conv_out, w_o, norm_ffn, w_up, ffn_conv_w, ffn_conv_b, w_down, norm_final, loss_target, m_norm_tok, m_w_in, m_a_re, m_a_im, m_log_dt, m_b_re, m_b_im, m_c_re, m_c_im, m_d_skip, m_w_glu, m_w_ssm_out, m_conv_w, m_conv_b, m_w_conv_out, m_w_o, m_norm_ffn, m_w_up, m_ffn_conv_w, m_ffn_conv_b, m_w_down, m_norm_final, v_norm_tok, v_w_in, v_a_re, v_a_im, v_log_dt, v_b_re, v_b_im, v_c_re, v_c_im, v_d_skip, v_w_glu, v_w_ssm_out, v_conv_w, v_conv_b, v_w_conv_out, v_w_o, v_norm_ffn, v_w_up, v_ffn_conv_w, v_ffn_conv_b, v_w_down, v_norm_final):
    p = dict(norm_tok=norm_tok, w_in=w_in, a_re=a_re, a_im=a_im, log_dt=log_dt, b_re=b_re, b_im=b_im, c_re=c_re,
             c_im=c_im, d_skip=d_skip, w_glu=w_glu, w_ssm_out=w_ssm_out, conv_w=conv_w, conv_b=conv_b,
             w_conv_out=w_conv_out, w_o=w_o, norm_ffn=norm_ffn, w_up=w_up, ffn_conv_w=ffn_conv_w,
             ffn_conv_b=ffn_conv_b, w_down=w_down, norm_final=norm_final)
    m = dict(norm_tok=m_norm_tok, w_in=m_w_in, a_re=m_a_re, a_im=m_a_im, log_dt=m_log_dt, b_re=m_b_re, b_im=m_b_im,
             c_re=m_c_re, c_im=m_c_im, d_skip=m_d_skip, w_glu=m_w_glu, w_ssm_out=m_w_ssm_out, conv_w=m_conv_w,
             conv_b=m_conv_b, w_conv_out=m_w_conv_out, w_o=m_w_o, norm_ffn=m_norm_ffn, w_up=m_w_up,
             ffn_conv_w=m_ffn_conv_w, ffn_conv_b=m_ffn_conv_b, w_down=m_w_down, norm_final=m_norm_final)
    v = dict(norm_tok=v_norm_tok, w_in=v_w_in, a_re=v_a_re, a_im=v_a_im, log_dt=v_log_dt, b_re=v_b_re, b_im=v_b_im,
             c_re=v_c_re, c_im=v_c_im, d_skip=v_d_skip, w_glu=v_w_glu, w_ssm_out=v_w_ssm_out, conv_w=v_conv_w,
             conv_b=v_conv_b, w_conv_out=v_w_conv_out, w_o=v_w_o, norm_ffn=v_norm_ffn, w_up=v_w_up,
             ffn_conv_w=v_ffn_conv_w, ffn_conv_b=v_ffn_conv_b, w_down=v_w_down, norm_final=v_norm_final)
    return _step(_Cfg(), x, loss_target, p, m, v)
```

```python
import functools
import math
from typing import NamedTuple

import jax
import jax.numpy as jnp
from jax import lax
from jax.experimental import pallas as pl
from jax.experimental.pallas import tpu as pltpu
from jax.experimental.pallas import tpu_sc as plsc

f32 = jnp.float32
bf16 = jnp.bfloat16
_MESH = pl.DeviceIdType.MESH

_EPS = 1e-6
_ADAM_LR = 0.001
_ADAM_B1 = 0.9
_ADAM_B2 = 0.999
_ADAM_EPS = 1e-08
_ADAM_WD = 0.01
_ADAM_STEP = 10
_SSM_GROUP = 16
_SSM_STATE = 64
_SLAB_GROUPS = 16
_NCHIP = 4
_VMEM_LIMIT = 56 * 2**20
_GELU_C = math.sqrt(2.0 / math.pi)
_GELU_A = 0.044715


class _Cfg(NamedTuple):
    L: int = 4096
    D: int = 2048
    SW: int = 1024
    CW: int = 1024
    F: int = 5632
    T: int = 256


def _tile(n, pref, align):
    t = min(n, pref)
    t -= t % align
    while t > align and n % t:
        t -= align
    assert t > 0 and n % t == 0, (n, pref, align)
    return t


def _cparams(sem):
    return pltpu.CompilerParams(dimension_semantics=sem, vmem_limit_bytes=_VMEM_LIMIT)


def _gelu(x):
    return 0.5 * x * (1.0 + jnp.tanh(_GELU_C * (x + _GELU_A * x * x * x)))


def _gelu_grad(x):
    th = jnp.tanh(_GELU_C * (x + _GELU_A * x * x * x))
    return 0.5 * (1.0 + th) + 0.5 * x * (1.0 - th * th) * _GELU_C * (1.0 + 3.0 * _GELU_A * x * x)


_NN = (((1,), (0,)), ((), ()))
_NT = (((1,), (1,)), ((), ()))
_TN = (((0,), (0,)), ((), ()))


def _whole(ref):
    return ref[...]


_ANY = pl.BlockSpec(memory_space=pl.ANY)


def _mm(name, operands, steps, *, grid, contract, outs, extras=(), epilogue=None, acc_shape=None, after=()):
    nop, ne, na = len(operands), len(extras), len(after)
    nk = len(steps)

    def body(*refs):
        op_refs = refs[:nop]
        e_refs = refs[nop:nop + ne]
        o_refs = refs[nop + ne + na:nop + ne + na + len(outs)]

        def partial(terms):
            tot = None
            for ai, av, bi, bv in terms:
                d = lax.dot_general(av(op_refs[ai]), bv(op_refs[bi]), contract, preferred_element_type=f32)
                tot = d if tot is None else tot + d
            return tot

        def finish(res):
            if epilogue is None:
                for o in o_refs:
                    o[...] = res.astype(o.dtype)
            else:
                epilogue(res, e_refs, o_refs)

        if nk == 1:
            finish(partial(steps[0][1]))
            return
        acc = refs[-1]
        kid = pl.program_id(len(grid) - 1)
        for k, terms in steps:
            def run(k=k, terms=terms):
                d = partial(terms)
                if k == 0:
                    acc[...] = d
                elif k < nk - 1:
                    acc[...] += d
                else:
                    finish(acc[...] + d)

            pl.when(kid == k)(run)

    sem = ("parallel",) * (len(grid) - (nk > 1)) + (("arbitrary",) if nk > 1 else ())
    return pl.pallas_call(
        body, grid=grid, in_specs=[o[1] for o in operands] + [e[1] for e in extras] + [_ANY] * na,
        out_specs=[o[1] for o in outs], out_shape=[o[0] for o in outs],
        scratch_shapes=[pltpu.VMEM(acc_shape, f32)] if nk > 1 else [], name=name,
        compiler_params=_cparams(sem))(*[o[0] for o in operands], *[e[0] for e in extras], *after)


def _mm_nn(name, a, w, out_dtypes, *, tm=1024, tn=1024, rows=False, extras_fn=None, epilogue=None):
    M, K = a.shape
    S, Ns = w.shape[0], w.shape[-1]
    N = Ns if rows else Ns * S
    tm, tn = _tile(M, tm, 16), _tile(Ns, tn, 128)
    nb = Ns // tn
    a_spec = pl.BlockSpec((tm, K), lambda i, j: (i, 0))
    if rows:
        b_spec = pl.BlockSpec((S, K // S, tn), lambda i, j: (0, 0, j))
        b_view = lambda r: r[...].reshape(K, tn)
    else:
        b_spec = pl.BlockSpec((None, K, tn), lambda i, j: (j // nb, 0, j % nb))
        b_view = _whole
    o_spec = pl.BlockSpec((tm, tn), lambda i, j: (i, j))
    outs = [(jax.ShapeDtypeStruct((M, N), dt), o_spec) for dt in out_dtypes]
    extras = extras_fn(tm, tn) if extras_fn is not None else ()
    return _mm(name, [(a, a_spec), (w, b_spec)], [(None, [(0, _whole, 1, b_view)])], grid=(M // tm, N // tn),
               contract=_NN, outs=outs, extras=extras, epilogue=epilogue)


def _mm_nt(name, a, w, out_dtypes, *, tm=1024, tn=1024, rows=False, extras_fn=None, epilogue=None):
    M, N = a.shape
    S, Ks, Ns = w.shape
    K = Ks * S if rows else Ks
    tm = _tile(M, tm, 16)
    a_spec = pl.BlockSpec((tm, N), lambda i, j: (i, 0))
    if rows:
        tn = K if tn >= K else _tile(Ks, tn, 128)
        if tn == K:
            b_spec = pl.BlockSpec((S, Ks, N), lambda i, j: (0, 0, 0))
            terms = [(0, _whole, 1, lambda r: r[...].reshape(K, N))]
        else:
            nbs = Ks // tn
            b_spec = pl.BlockSpec((None, tn, N), lambda i, j: (j // nbs, j % nbs, 0))
            terms = [(0, _whole, 1, _whole)]
    else:
        tn = _tile(K, tn, 128)
        assert S * Ns == N
        b_spec = pl.BlockSpec((S, tn, Ns), lambda i, j: (0, j, 0))
        terms = [(0, lambda r, s=s: r[:, s * Ns:(s + 1) * Ns], 1, lambda r, s=s: r[s]) for s in range(S)]
    o_spec = pl.BlockSpec((tm, tn), lambda i, j: (i, j))
    outs = [(jax.ShapeDtypeStruct((M, K), dt), o_spec) for dt in out_dtypes]
    extras = extras_fn(tm, tn) if extras_fn is not None else ()
    return _mm(name, [(a, a_spec), (w, b_spec)], [(None, terms)], grid=(M // tm, K // tn), contract=_NT,
               outs=outs, extras=extras, epilogue=epilogue)


def _mm_tn(name, a, b, *, shards=None, tm=1024, tn=1024, b_block=None, b_resident=False, after=()):
    M, K = a.shape
    halves = b.shape[0] if b.ndim == 3 else 1
    Nh = b.shape[-1]
    N = Nh * halves
    Ns = N // shards if shards else N
    tm, tn = _tile(K, tm, 128), _tile(math.gcd(Ns, Nh), tn, 128)
    nb, nbh = Ns // tn, Nh // tn
    ij = (lambda g0, g1: (g1, g0)) if b_resident else (lambda g0, g1: (g0, g1))
    bmap = b_block if b_block is not None else (lambda j: j)
    a_spec = pl.BlockSpec((M, tm), lambda g0, g1: (0, ij(g0, g1)[0]))
    if halves > 1:
        b_spec = pl.BlockSpec((None, M, tn), lambda g0, g1: (bmap(ij(g0, g1)[1]) // nbh, 0, bmap(ij(g0, g1)[1]) % nbh))
    else:
        b_spec = pl.BlockSpec((M, tn), lambda g0, g1: (0, bmap(ij(g0, g1)[1])))
    if shards:
        out = (jax.ShapeDtypeStruct((shards, K, Ns), f32),
               pl.BlockSpec((None, tm, tn), lambda g0, g1: (ij(g0, g1)[1] // nb, ij(g0, g1)[0], ij(g0, g1)[1] % nb)))
    else:
        out = (jax.ShapeDtypeStruct((K, N), f32), pl.BlockSpec((tm, tn), lambda g0, g1: ij(g0, g1)))
    grid = (N // tn, K // tm) if b_resident else (K // tm, N // tn)
    return _mm(name, [(a, a_spec), (b, b_spec)], [(None, [(0, _whole, 1, _whole)])], grid=grid, contract=_TN,
               outs=[out], after=after)[0]


def _in_proj_dx(cfg, dproj, w_in):
    L, D, SW, CW = cfg.L, cfg.D, cfg.SW, cfg.CW
    NP = SW + 3 * CW + 2 * D
    Ns = NP // _NCHIP
    assert SW + CW == Ns and 2 * CW == Ns and D == Ns
    tm, tn = _tile(L, 1024, 16), _tile(D, 1024, 128)
    a_spec = pl.BlockSpec((tm, NP // 2), lambda i, j, k: (i, k))
    b_spec = pl.BlockSpec((2, tn, Ns), lambda i, j, k: (k, j, 0))
    first = [(0, lambda r: r[:, 0:CW], 1, lambda r: r[0, :, SW:SW + CW]),
             (0, lambda r: r[:, CW:3 * CW], 1, lambda r: r[1]),
             (0, lambda r: r[:, 3 * CW:3 * CW + SW], 1, lambda r: r[0, :, 0:SW])]
    second = [(0, lambda r: r[:, 0:D], 1, lambda r: r[0]), (0, lambda r: r[:, D:2 * D], 1, lambda r: r[1])]
    out = (jax.ShapeDtypeStruct((L, D), bf16), pl.BlockSpec((tm, tn), lambda i, j, k: (i, j)))
    return _mm("in_proj_dx", [(dproj, a_spec), (w_in, b_spec)], [(0, first), (1, second)], grid=(L // tm, D // tn, 2),
               contract=_NT, outs=[out], acc_shape=(tm, tn))[0]


def _out_proj_dx(cfg, dh1b, w_o, ya, yb, proj):
    L, D = cfg.L, cfg.D
    NP = cfg.SW + 3 * cfg.CW + 2 * D
    assert NP == 4 * D
    tm = _tile(L, 256, 16)

    def epilogue(acc, e, o):
        sa = jax.nn.sigmoid(e[2][:, 0:D].astype(f32))
        sb = jax.nn.sigmoid(e[2][:, D:2 * D].astype(f32))
        o[0][...] = (acc * sa).astype(bf16)
        o[1][...] = (acc * sb).astype(bf16)
        o[2][:, 0:D] = (acc * e[0][...].astype(f32) * sa * (1.0 - sa)).astype(bf16)
        o[2][:, D:2 * D] = (acc * e[1][...].astype(f32) * sb * (1.0 - sb)).astype(bf16)

    row = pl.BlockSpec((tm, D), lambda i, j: (i, 0))
    half = pl.BlockSpec((tm, 2 * D), lambda i, j: (i, 1))
    return _mm("out_proj_dx", [(dh1b, row), (w_o, pl.BlockSpec(w_o.shape, lambda i, j: (0, 0, 0)))],
               [(None, [(0, _whole, 1, lambda r: r[...].reshape(D, D))])], grid=(L // tm, 1), contract=_NT,
               outs=[(jax.ShapeDtypeStruct((L, D), bf16), row), (jax.ShapeDtypeStruct((L, D), bf16), row),
                     (jax.ShapeDtypeStruct((L, NP), bf16), half)],
               extras=[(ya, row), (yb, row), (proj, half)], epilogue=epilogue)


def _ffn_up_dx(cfg, dhh, w_up):
    L, D, F = cfg.L, cfg.D, cfg.F
    Fh = F // 2
    tm, tn = _tile(L, 1024, 16), _tile(D, 512, 128)
    a_spec = pl.BlockSpec((None, tm, F), lambda i, j, k: (k, i, 0))
    b_spec = pl.BlockSpec((2, tn, Fh), lambda i, j, k: (k, j, 0))
    terms = [(0, lambda r: r[:, 0:Fh], 1, lambda r: r[0]), (0, lambda r: r[:, Fh:F], 1, lambda r: r[1])]
    out = (jax.ShapeDtypeStruct((L, D), bf16), pl.BlockSpec((tm, tn), lambda i, j, k: (i, j)))
    return _mm("ffn_up_dx", [(dhh, a_spec), (w_up, b_spec)], [(0, terms), (1, terms)], grid=(L // tm, D // tn, 2),
               contract=_NT, outs=[out], acc_shape=(tm, tn))[0]


def _rms_fwd(name, x, g):
    L, D = x.shape
    tm = _tile(L, 256, 16)

    def body(x_ref, g_ref, xn_ref, r_ref):
        xv = x_ref[...]
        r = lax.rsqrt(jnp.mean(xv * xv, axis=-1, keepdims=True) + _EPS)
        xn_ref[...] = (xv * r * g_ref[...]).astype(bf16)
        r_ref[...] = r

    return pl.pallas_call(
        body, grid=(L // tm,),
        in_specs=[pl.BlockSpec((tm, D), lambda i: (i, 0)), pl.BlockSpec((1, D), lambda i: (0, 0))],
        out_specs=[pl.BlockSpec((tm, D), lambda i: (i, 0)), pl.BlockSpec((tm, 1), lambda i: (i, 0))],
        out_shape=[jax.ShapeDtypeStruct((L, D), bf16), jax.ShapeDtypeStruct((L, 1), f32)],
        name=name, compiler_params=_cparams(("parallel",)))(x, g)


def _rms_bwd(name, dxn, h, r, g, dres, after=()):
    L, D = h.shape
    tm = _tile(L, 256, 16)

    def body(dxn_ref, h_ref, r_ref, g_ref, dres_ref, *rest):
        dh_ref, dhb_ref, dg_ref = rest[len(after):]
        i = pl.program_id(0)
        d = dxn_ref[...].astype(f32)
        hv = h_ref[...]
        rv = r_ref[...]
        dyg = d * g_ref[...]
        m = jnp.mean(dyg * hv, axis=-1, keepdims=True)
        dh = dres_ref[...] + rv * dyg - hv * (rv * rv * rv) * m
        dh_ref[...] = dh
        dhb_ref[...] = dh.astype(bf16)

        @pl.when(i == 0)
        def _():
            dg_ref[...] = jnp.zeros_like(dg_ref)

        dg_ref[...] += jnp.sum(d * hv * rv, axis=0, keepdims=True)

    row = lambda i: (i, 0)
    return pl.pallas_call(
        body, grid=(L // tm,),
        in_specs=[pl.BlockSpec((tm, D), row), pl.BlockSpec((tm, D), row), pl.BlockSpec((tm, 1), row),
                  pl.BlockSpec((1, D), lambda i: (0, 0)), pl.BlockSpec((tm, D), row)] + [_ANY] * len(after),
        out_specs=[pl.BlockSpec((tm, D), row), pl.BlockSpec((tm, D), row), pl.BlockSpec((1, D), lambda i: (0, 0))],
        out_shape=[jax.ShapeDtypeStruct((L, D), f32), jax.ShapeDtypeStruct((L, D), bf16), jax.ShapeDtypeStruct((1, D), f32)],
        name=name, compiler_params=_cparams(("arbitrary",)))(dxn, h, r, g, dres, *after)


def _loss_head(name, h2, tgt, g):
    L, D = h2.shape
    tm = _tile(L, 256, 16)

    def body(h_ref, t_ref, g_ref, dh_ref, dhb_ref, dg_ref, loss_ref):
        i = pl.program_id(0)
        hv = h_ref[...]
        gv = g_ref[...]
        r = lax.rsqrt(jnp.mean(hv * hv, axis=-1, keepdims=True) + _EPS)
        err = hv * r * gv - t_ref[...]
        dy = err * (1.0 / D)
        dyg = dy * gv
        m = jnp.mean(dyg * hv, axis=-1, keepdims=True)
        dh = r * dyg - hv * (r * r * r) * m
        dh_ref[...] = dh
        dhb_ref[...] = dh.astype(bf16)

        @pl.when(i == 0)
        def _():
            dg_ref[...] = jnp.zeros_like(dg_ref)
            loss_ref[...] = jnp.zeros_like(loss_ref)

        dg_ref[...] += jnp.sum(dy * hv * r, axis=0, keepdims=True)
        part = jnp.sum(jnp.sum(err * err, axis=-1, keepdims=True), axis=0, keepdims=True) * (0.5 / D)
        loss_ref[...] += jnp.broadcast_to(part, (8, 128))

    row = lambda i: (i, 0)
    return pl.pallas_call(
        body, grid=(L // tm,),
        in_specs=[pl.BlockSpec((tm, D), row), pl.BlockSpec((tm, D), row), pl.BlockSpec((1, D), lambda i: (0, 0))],
        out_specs=[pl.BlockSpec((tm, D), row), pl.BlockSpec((tm, D), row), pl.BlockSpec((1, D), lambda i: (0, 0)),
                   pl.BlockSpec((8, 128), lambda i: (0, 0))],
        out_shape=[jax.ShapeDtypeStruct((L, D), f32), jax.ShapeDtypeStruct((L, D), bf16),
                   jax.ShapeDtypeStruct((1, D), f32), jax.ShapeDtypeStruct((8, 128), f32)],
        name=name, compiler_params=_cparams(("arbitrary",)))(h2, tgt, g)


def _shift_down(tile, halo, k, rows8):
    tm = tile.shape[0]
    r = pltpu.roll(tile, k, axis=0)
    hh = pltpu.roll(halo, k, axis=0)
    top = jnp.where(rows8 < k, hh, r[:8])
    return jnp.concatenate([top, r[8:]], axis=0) if tm > 8 else top


def _shift_up(tile, halo, k, rows8):
    tm = tile.shape[0]
    r = pltpu.roll(tile, tm - k, axis=0)
    hh = pltpu.roll(halo, 8 - k, axis=0)
    bot = jnp.where(rows8 >= 8 - k, hh, r[tm - 8:])
    return jnp.concatenate([r[:tm - 8], bot], axis=0) if tm > 8 else bot


def _conv3(x, halo, w_ref, b_ref, rows8):
    return (w_ref[0:1, :] * _shift_down(x, halo, 2, rows8) + w_ref[1:2, :] * _shift_down(x, halo, 1, rows8)
            + w_ref[2:3, :] * x + b_ref[...])


def _convb_fwd(cfg, proj, w, b):
    L, CW = cfg.L, cfg.CW
    assert cfg.SW == CW
    tm = _tile(L, 512, 16)

    def body(v_ref, vh_ref, gb_ref, gc_ref, gch_ref, w_ref, b_ref, o_ref):
        i = pl.program_id(0)
        rows8 = lax.broadcasted_iota(jnp.int32, (8, CW), 0)
        cv = gc_ref[...].astype(f32) * v_ref[...].astype(f32)
        cvh = gch_ref[...].astype(f32)[8:] * vh_ref[...].astype(f32)[8:]
        cvh = jnp.where(i == 0, 0.0, cvh)
        cc = _conv3(cv, cvh, w_ref, b_ref, rows8)
        o_ref[...] = (gb_ref[...].astype(f32) * cc).astype(bf16)

    blk = lambda col: pl.BlockSpec((tm, CW), lambda i: (i, col))
    halo = lambda col: pl.BlockSpec((16, CW), lambda i: (jnp.maximum(i * (tm // 16) - 1, 0), col))
    return pl.pallas_call(
        body, grid=(L // tm,),
        in_specs=[blk(1), halo(1), blk(2), blk(3), halo(3),
                  pl.BlockSpec((3, CW), lambda i: (0, 0)), pl.BlockSpec((1, CW), lambda i: (0, 0))],
        out_specs=pl.BlockSpec((tm, CW), lambda i: (i, 0)),
        out_shape=jax.ShapeDtypeStruct((L, CW), bf16),
        name="convb_fwd", compiler_params=_cparams(("parallel",)))(proj, proj, proj, proj, proj, w, b)


def _convb_bwd(cfg, proj, dyb0, w, b, dproj):
    L, CW = cfg.L, cfg.CW
    tm = _tile(L, 512, 16)
    nt = L // tm

    def body(v_ref, vh_ref, gb_ref, gbn_ref, gc_ref, gch_ref, d_ref, dn_ref, w_ref, b_ref, dproj_ref,
             o_ref, dw_ref, db_ref):
        i = pl.program_id(0)
        rows8 = lax.broadcasted_iota(jnp.int32, (8, CW), 0)
        v = v_ref[...].astype(f32)
        gb = gb_ref[...].astype(f32)
        gc = gc_ref[...].astype(f32)
        d = d_ref[...].astype(f32)
        cv = gc * v
        cvh = gch_ref[...].astype(f32)[8:] * vh_ref[...].astype(f32)[8:]
        cvh = jnp.where(i == 0, 0.0, cvh)
        s2 = _shift_down(cv, cvh, 2, rows8)
        s1 = _shift_down(cv, cvh, 1, rows8)
        cc = w_ref[0:1, :] * s2 + w_ref[1:2, :] * s1 + w_ref[2:3, :] * cv + b_ref[...]
        dcc = d * gb
        dccn = dn_ref[...].astype(f32)[:8] * gbn_ref[...].astype(f32)[:8]
        dccn = jnp.where(i == nt - 1, 0.0, dccn)
        dcv = (w_ref[2:3, :] * dcc + w_ref[1:2, :] * _shift_up(dcc, dccn, 1, rows8)
               + w_ref[0:1, :] * _shift_up(dcc, dccn, 2, rows8))
        o_ref[:, 0:CW] = (dcv * gc).astype(bf16)
        o_ref[:, CW:2 * CW] = (d * cc).astype(bf16)
        o_ref[:, 2 * CW:3 * CW] = (dcv * v).astype(bf16)

        @pl.when(i == 0)
        def _():
            dw_ref[...] = jnp.zeros_like(dw_ref)
            db_ref[...] = jnp.zeros_like(db_ref)

        dw_ref[0:1, :] += jnp.sum(dcc * s2, axis=0, keepdims=True)
        dw_ref[1:2, :] += jnp.sum(dcc * s1, axis=0, keepdims=True)
        dw_ref[2:3, :] += jnp.sum(dcc * cv, axis=0, keepdims=True)
        db_ref[...] += jnp.sum(dcc, axis=0, keepdims=True)

    blk = lambda col: pl.BlockSpec((tm, CW), lambda i: (i, col))
    prev = lambda col: pl.BlockSpec((16, CW), lambda i: (jnp.maximum(i * (tm // 16) - 1, 0), col))
    nxt = lambda col: pl.BlockSpec((16, CW), lambda i: (jnp.minimum((i + 1) * (tm // 16), L // 16 - 1), col))
    const = lambda r: pl.BlockSpec((r, CW), lambda i: (0, 0))
    return pl.pallas_call(
        body, grid=(nt,),
        in_specs=[blk(1), prev(1), blk(2), nxt(2), blk(3), prev(3), blk(0), nxt(0), const(3), const(1),
                  pl.BlockSpec(memory_space=pl.ANY)],
        out_specs=[pl.BlockSpec((tm, 3 * CW), lambda i: (i, 0)), const(3), const(1)],
        out_shape=[jax.ShapeDtypeStruct(dproj.shape, bf16), jax.ShapeDtypeStruct((3, CW), f32),
                   jax.ShapeDtypeStruct((1, CW), f32)],
        input_output_aliases={10: 0},
        name="convb_bwd", compiler_params=_cparams(("arbitrary",)))(proj, proj, proj, proj, proj, proj, dyb0, dyb0, w, b,
                                                                    dproj)


def _ffn_act(cfg, hh, w, b):
    L, F = cfg.L, cfg.F
    tm = _tile(L, 512, 16)
    tc = _tile(F, 1408, 128)
    ncb = F // tc

    def body(a_ref, ah_ref, g_ref, w_ref, b_ref, o_ref):
        i = pl.program_id(0)
        rows8 = lax.broadcasted_iota(jnp.int32, (8, tc), 0)
        a = a_ref[...].astype(f32)
        ah = jnp.where(i == 0, 0.0, ah_ref[...].astype(f32)[8:])
        o_ref[...] = (_gelu(_conv3(a, ah, w_ref, b_ref, rows8)) * g_ref[...].astype(f32)).astype(bf16)

    return pl.pallas_call(
        body, grid=(L // tm, ncb),
        in_specs=[pl.BlockSpec((tm, tc), lambda i, j: (i, j)),
                  pl.BlockSpec((16, tc), lambda i, j: (jnp.maximum(i * (tm // 16) - 1, 0), j)),
                  pl.BlockSpec((tm, tc), lambda i, j: (i, j + ncb)),
                  pl.BlockSpec((3, tc), lambda i, j: (0, j)), pl.BlockSpec((1, tc), lambda i, j: (0, j))],
        out_specs=pl.BlockSpec((tm, tc), lambda i, j: (i, j)),
        out_shape=jax.ShapeDtypeStruct((L, F), bf16),
        name="ffn_act", compiler_params=_cparams(("parallel", "parallel")))(hh, hh, hh, w, b)


def _ffn_act_bwd(cfg, hh, df, w, b):
    L, F = cfg.L, cfg.F
    tm = _tile(L, 512, 16)
    tc = _tile(F, 1408, 128)
    ncb = F // tc
    nt = L // tm

    def body(a_ref, ah_ref, an_ref, g_ref, gn_ref, d_ref, dn_ref, w_ref, b_ref, dhh_ref, dw_ref, db_ref):
        i = pl.program_id(1)
        rows8 = lax.broadcasted_iota(jnp.int32, (8, tc), 0)
        a = a_ref[...].astype(f32)
        ah = jnp.where(i == 0, 0.0, ah_ref[...].astype(f32)[8:])
        s2 = _shift_down(a, ah, 2, rows8)
        s1 = _shift_down(a, ah, 1, rows8)
        act = w_ref[0:1, :] * s2 + w_ref[1:2, :] * s1 + w_ref[2:3, :] * a + b_ref[...]
        d = d_ref[...].astype(f32)
        dhh_ref[1] = (d * _gelu(act)).astype(bf16)
        dact = d * g_ref[...].astype(f32) * _gelu_grad(act)
        an = an_ref[...].astype(f32)[:8]
        actn = _conv3(an, a[tm - 8:], w_ref, b_ref, rows8)
        dactn = dn_ref[...].astype(f32)[:8] * gn_ref[...].astype(f32)[:8] * _gelu_grad(actn)
        dactn = jnp.where(i == nt - 1, 0.0, dactn)
        da = (w_ref[2:3, :] * dact + w_ref[1:2, :] * _shift_up(dact, dactn, 1, rows8)
              + w_ref[0:1, :] * _shift_up(dact, dactn, 2, rows8))
        dhh_ref[0] = da.astype(bf16)

        @pl.when(i == 0)
        def _():
            dw_ref[...] = jnp.zeros_like(dw_ref)
            db_ref[...] = jnp.zeros_like(db_ref)

        dw_ref[0:1, :] += jnp.sum(dact * s2, axis=0, keepdims=True)
        dw_ref[1:2, :] += jnp.sum(dact * s1, axis=0, keepdims=True)
        dw_ref[2:3, :] += jnp.sum(dact * a, axis=0, keepdims=True)
        db_ref[...] += jnp.sum(dact, axis=0, keepdims=True)

    blk = lambda off: pl.BlockSpec((tm, tc), lambda j, i: (i, j + off))
    prev = lambda off: pl.BlockSpec((16, tc), lambda j, i: (jnp.maximum(i * (tm // 16) - 1, 0), j + off))
    nxt = lambda off: pl.BlockSpec((16, tc), lambda j, i: (jnp.minimum((i + 1) * (tm // 16), L // 16 - 1), j + off))
    const = lambda r: pl.BlockSpec((r, tc), lambda j, i: (0, j))
    return pl.pallas_call(
        body, grid=(ncb, nt),
        in_specs=[blk(0), prev(0), nxt(0), blk(ncb), nxt(ncb), blk(0), nxt(0), const(3), const(1)],
        out_specs=[pl.BlockSpec((2, tm, tc), lambda j, i: (0, i, j)), const(3), const(1)],
        out_shape=[jax.ShapeDtypeStruct((2, L, F), bf16),
                   jax.ShapeDtypeStruct((3, F), f32), jax.ShapeDtypeStruct((1, F), f32)],
        name="ffn_act_bwd", compiler_params=_cparams(("parallel", "arbitrary")))(hh, hh, hh, hh, hh, df, df, w, b)


def _merge_fwd(cfg, ya1, yb0, proj, wso, wco):
    L, D, SW, CW = cfg.L, cfg.D, cfg.SW, cfg.CW
    Ns = D // _NCHIP
    tm = _tile(L, 1024, 16)
    tn = _tile(Ns, 512, 128)
    nb = Ns // tn
    off_a = (SW + 3 * CW) // tn
    off_b = (SW + 3 * CW + D) // tn

    def body(a_ref, b_ref, wa_ref, wb_ref, ma_ref, mb_ref, m_ref, ya_ref, yb_ref):
        ya = jnp.dot(a_ref[...], wa_ref[...], preferred_element_type=f32)
        yb = jnp.dot(b_ref[...], wb_ref[...], preferred_element_type=f32)
        sa = jax.nn.sigmoid(ma_ref[...].astype(f32))
        sb = jax.nn.sigmoid(mb_ref[...].astype(f32))
        m_ref[...] = (sa * ya + sb * yb).astype(bf16)
        ya_ref[...] = ya.astype(bf16)
        yb_ref[...] = yb.astype(bf16)

    o_spec = pl.BlockSpec((tm, tn), lambda i, j: (i, j))
    o_shape = jax.ShapeDtypeStruct((L, D), bf16)
    return pl.pallas_call(
        body, grid=(L // tm, D // tn),
        in_specs=[pl.BlockSpec((tm, SW), lambda i, j: (i, 0)), pl.BlockSpec((tm, CW), lambda i, j: (i, 0)),
                  pl.BlockSpec((None, SW, tn), lambda i, j: (j // nb, 0, j % nb)),
                  pl.BlockSpec((None, CW, tn), lambda i, j: (j // nb, 0, j % nb)),
                  pl.BlockSpec((tm, tn), lambda i, j: (i, off_a + j)), pl.BlockSpec((tm, tn), lambda i, j: (i, off_b + j))],
        out_specs=[o_spec, o_spec, o_spec], out_shape=[o_shape, o_shape, o_shape],
        name="merge_fwd", compiler_params=_cparams(("parallel", "parallel")))(ya1, yb0, wso, wco, proj, proj)


def _s5_dims(cfg):
    G = cfg.SW // _SSM_GROUP
    NS = G // _SLAB_GROUPS
    SC = _SLAB_GROUPS * _SSM_GROUP
    SH = _SLAB_GROUPS * _SSM_STATE
    NST = 2 * SH * NS
    return G, NS, SC, SH, NST


def _lane_slabs(cfg, W):
    _, NS, _, SH, _ = _s5_dims(cfg)
    return [(2 * SH * s + w0, 2 * SH * s + SH + w0) for s in range(NS) for w0 in range(0, SH, W)]


def _discretize(a_re, a_im, log_dt, b_re, b_im):
    dt = jnp.exp(log_dt)[:, None]
    mag = jnp.exp(dt * a_re)
    abr = mag * jnp.cos(dt * a_im)
    abi = mag * jnp.sin(dt * a_im)
    nr = abr - 1.0
    ni = abi
    den = a_re * a_re + a_im * a_im
    fr = (nr * a_re + ni * a_im) / den
    fi = (ni * a_re - nr * a_im) / den
    bbr = fr[..., None] * b_re - fi[..., None] * b_im
    bbi = fr[..., None] * b_im + fi[..., None] * b_re
    return abr, abi, bbr, bbi


def _state_rows(cfg, re, im):
    _, NS, _, SH, _ = _s5_dims(cfg)
    return jnp.concatenate([re.reshape(NS, SH), im.reshape(NS, SH)], axis=1).reshape(-1)


def _s5_tables(cfg, abr, abi, bbr, bbi, c_re, c_im):
    G, NS, SC, SH, NST = _s5_dims(cfg)
    S = cfg.T // 8
    eye = jnp.eye(_SLAB_GROUPS, dtype=f32)
    bb = jnp.stack([bbr, bbi]).reshape(2, NS, _SLAB_GROUPS, _SSM_STATE, _SSM_GROUP)
    bs = jnp.einsum("rsgph,gq->sghrqp", bb, eye).reshape(NS, SC, 2 * SH).astype(bf16)
    cc = jnp.stack([c_re, -c_im]).reshape(2, NS, _SLAB_GROUPS, _SSM_GROUP, _SSM_STATE)
    cs = jnp.einsum("rsghp,gq->srqpgh", cc, eye).reshape(NS, 2 * SH, SC).astype(bf16)
    arep = jnp.broadcast_to(_state_rows(cfg, abr, abi)[None, :], (8, NST))
    pr, pi = abr, abi
    for _ in range(S - 1):
        pr, pi = pr * abr - pi * abi, pr * abi + pi * abr
    apow = jnp.broadcast_to(_state_rows(cfg, pr, pi)[None, :], (8, NST))
    t = jnp.arange(cfg.T)
    perm = (t % 8) * S + t // 8
    pm = jax.nn.one_hot(perm, cfg.T, dtype=bf16)
    return bs, cs, arep, apow, pm, pm.T


def _cmul_add(ar, ai, xr, xi, br, bi):
    return ar * xr - ai * xi + br, ar * xi + ai * xr + bi


def _s5_forward_chunk(cfg, W, upb, bs_ref, arep_ref, apow_ref, st, x0, cin_store):
    _, NS, SC, SH, _ = _s5_dims(cfg)
    S = cfg.T // 8
    for s in range(NS):
        st[:, 2 * SH * s:2 * SH * (s + 1)] = jnp.dot(upb[:, SC * s:SC * (s + 1)], bs_ref[s], preferred_element_type=f32)
    rows = lax.broadcasted_iota(jnp.int32, (8, W), 0)
    zero = jnp.zeros((8, W), f32)
    for rc, ic in _lane_slabs(cfg, W):
        ar = arep_ref[:, rc:rc + W]
        ai = arep_ref[:, ic:ic + W]

        def step(i, carry, rc=rc, ic=ic, ar=ar, ai=ai):
            xr, xi = carry
            r0 = pl.multiple_of(i * 8, 8)
            nr, ni = _cmul_add(ar, ai, xr, xi, st[pl.ds(r0, 8), rc:rc + W], st[pl.ds(r0, 8), ic:ic + W])
            st[pl.ds(r0, 8), rc:rc + W] = nr
            st[pl.ds(r0, 8), ic:ic + W] = ni
            return nr, ni

        er, ei = lax.fori_loop(0, S, step, (zero, zero))
        pr = apow_ref[:, rc:rc + W]
        pi = apow_ref[:, ic:ic + W]
        x0r = x0[:, rc:rc + W]
        x0i = x0[:, ic:ic + W]
        cr = jnp.where(rows == 0, x0r, 0.0)
        ci = jnp.where(rows == 0, x0i, 0.0)
        for _ in range(7):
            fr, fi = _cmul_add(pr, pi, cr, ci, er, ei)
            cr = jnp.where(rows == 0, x0r, pltpu.roll(fr, 1, axis=0))
            ci = jnp.where(rows == 0, x0i, pltpu.roll(fi, 1, axis=0))
        fr, fi = _cmul_add(pr, pi, cr, ci, er, ei)
        x0[:, rc:rc + W] = jnp.broadcast_to(fr[7:8, :], (8, W))
        x0[:, ic:ic + W] = jnp.broadcast_to(fi[7:8, :], (8, W))
        cin_store(rc, ic, cr, ci)

        def fix(i, carry, rc=rc, ic=ic, ar=ar, ai=ai):
            kr, ki = carry
            r0 = pl.multiple_of(i * 8, 8)
            nr, ni = ar * kr - ai * ki, ar * ki + ai * kr
            st[pl.ds(r0, 8), rc:rc + W] = st[pl.ds(r0, 8), rc:rc + W] + nr
            st[pl.ds(r0, 8), ic:ic + W] = st[pl.ds(r0, 8), ic:ic + W] + ni
            return nr, ni

        lax.fori_loop(0, S, fix, (cr, ci))


def _s5_fwd(cfg, proj, tabs, dskip):
    L, SW, T = cfg.L, cfg.SW, cfg.T
    G, NS, SC, SH, NST = _s5_dims(cfg)
    bs, cs, arep, apow, pm, pmt = tabs
    W = min(512, SH)
    NC = L // T

    def body(u_ref, pm_ref, pmt_ref, bs_ref, cs_ref, arep_ref, apow_ref, dskip_ref, y_ref, ya0_ref, cin_ref, st, x0):
        c = pl.program_id(0)

        @pl.when(c == 0)
        def _():
            x0[...] = jnp.zeros_like(x0)

        up = jnp.dot(pm_ref[...], u_ref[...], preferred_element_type=f32)
        upb = up.astype(bf16)

        def cin_store(rc, ic, cr, ci):
            cin_ref[0, :, rc:rc + W] = cr
            cin_ref[0, :, ic:ic + W] = ci

        _s5_forward_chunk(cfg, W, upb, bs_ref, arep_ref, apow_ref, st, x0, cin_store)
        yp = jnp.concatenate(
            [jnp.dot(st[:, 2 * SH * s:2 * SH * (s + 1)].astype(bf16), cs_ref[s], preferred_element_type=f32)
             for s in range(NS)], axis=1) + dskip_ref[...] * up
        y = jnp.dot(pmt_ref[...], yp.astype(bf16), preferred_element_type=f32)
        y_ref[...] = y.astype(bf16)
        ya0_ref[...] = _gelu(y).astype(bf16)

    const2 = lambda shape: pl.BlockSpec(shape, lambda c: (0, 0))
    const3 = lambda shape: pl.BlockSpec(shape, lambda c: (0, 0, 0))
    return pl.pallas_call(
        body, grid=(NC,),
        in_specs=[pl.BlockSpec((T, SW), lambda c: (c, 0)), const2((T, T)), const2((T, T)), const3((NS, SC, 2 * SH)),
                  const3((NS, 2 * SH, SC)), const2((8, NST)), const2((8, NST)), const2((1, SW))],
        out_specs=[pl.BlockSpec((T, SW), lambda c: (c, 0)), pl.BlockSpec((T, SW), lambda c: (c, 0)),
                   pl.BlockSpec((1, 8, NST), lambda c: (c, 0, 0))],
        out_shape=[jax.ShapeDtypeStruct((L, SW), bf16), jax.ShapeDtypeStruct((L, SW), bf16),
                   jax.ShapeDtypeStruct((NC, 8, NST), f32)],
        scratch_shapes=[pltpu.VMEM((T, NST), f32), pltpu.VMEM((8, NST), f32)],
        name="s5_fwd", compiler_params=_cparams(("arbitrary",)))(proj, pm, pmt, bs, cs, arep, apow, dskip)


def _s5_bwd(cfg, proj, dy, cin, tabs, dskip, dproj, after=()):
    L, SW, T = cfg.L, cfg.SW, cfg.T
    du_col = 3 * cfg.CW // SW
    G, NS, SC, SH, NST = _s5_dims(cfg)
    bs, cs, arep, apow, pm, pmt = tabs
    W = min(512, SH)
    S = T // 8
    NC = L // T

    def body(u_ref, dy_ref, cin_ref, pm_ref, pmt_ref, bs_ref, cs_ref, arep_ref, apow_ref, dskip_ref, dproj_ref, *rest):
        du_ref, da_ref, db_ref, dc_ref, dd_ref, st, gs, x0, g0, db_acc, dc_acc = rest[len(after):]
        c = pl.program_id(0)

        @pl.when(c == 0)
        def _():
            g0[...] = jnp.zeros_like(g0)
            da_ref[...] = jnp.zeros_like(da_ref)
            dd_ref[...] = jnp.zeros_like(dd_ref)
            db_acc[...] = jnp.zeros_like(db_acc)
            dc_acc[...] = jnp.zeros_like(dc_acc)

        up = jnp.dot(pm_ref[...], u_ref[...], preferred_element_type=f32)
        upb = up.astype(bf16)
        dyp = jnp.dot(pm_ref[...], dy_ref[...], preferred_element_type=f32)
        dypb = dyp.astype(bf16)
        x0[...] = jnp.zeros_like(x0)
        for rc, ic in _lane_slabs(cfg, W):
            x0[:, rc:rc + W] = jnp.broadcast_to(cin_ref[0, 0:1, rc:rc + W], (8, W))
            x0[:, ic:ic + W] = jnp.broadcast_to(cin_ref[0, 0:1, ic:ic + W], (8, W))
        _s5_forward_chunk(cfg, W, upb, bs_ref, arep_ref, apow_ref, st, x0, lambda *a: None)
        for s in range(NS):
            gs[:, 2 * SH * s:2 * SH * (s + 1)] = lax.dot_general(
                dypb[:, SC * s:SC * (s + 1)], cs_ref[s], (((1,), (1,)), ((), ())), preferred_element_type=f32)
        rows = lax.broadcasted_iota(jnp.int32, (8, W), 0)
        zero = jnp.zeros((8, W), f32)
        for rc, ic in _lane_slabs(cfg, W):
            ar = arep_ref[:, rc:rc + W]
            ai = arep_ref[:, ic:ic + W]

            def rstep(k, carry, rc=rc, ic=ic, ar=ar, ai=ai):
                gr, gi = carry
                r0 = pl.multiple_of((S - 1 - k) * 8, 8)
                nr = ar * gr + ai * gi + gs[pl.ds(r0, 8), rc:rc + W]
                ni = ar * gi - ai * gr + gs[pl.ds(r0, 8), ic:ic + W]
                gs[pl.ds(r0, 8), rc:rc + W] = nr
                gs[pl.ds(r0, 8), ic:ic + W] = ni
                return nr, ni

            er, ei = lax.fori_loop(0, S, rstep, (zero, zero))
            pr = apow_ref[:, rc:rc + W]
            pi = apow_ref[:, ic:ic + W]
            g0r = g0[:, rc:rc + W]
            g0i = g0[:, ic:ic + W]
            cr = jnp.where(rows == 7, g0r, 0.0)
            ci = jnp.where(rows == 7, g0i, 0.0)
            for _ in range(7):
                fr = er + pr * cr + pi * ci
                fi = ei + pr * ci - pi * cr
                cr = jnp.where(rows == 7, g0r, pltpu.roll(fr, 7, axis=0))
                ci = jnp.where(rows == 7, g0i, pltpu.roll(fi, 7, axis=0))
            fr = er + pr * cr + pi * ci
            fi = ei + pr * ci - pi * cr
            g0[:, rc:rc + W] = jnp.broadcast_to(fr[0:1, :], (8, W))
            g0[:, ic:ic + W] = jnp.broadcast_to(fi[0:1, :], (8, W))

            def fix(k, carry, rc=rc, ic=ic, ar=ar, ai=ai):
                kr, ki, accr, acci = carry
                i = S - 1 - k
                r0 = pl.multiple_of(i * 8, 8)
                rp = pl.multiple_of((i - 1) * 8, 8)
                nr = ar * kr + ai * ki
                ni = ar * ki - ai * kr
                gr = gs[pl.ds(r0, 8), rc:rc + W] + nr
                gi = gs[pl.ds(r0, 8), ic:ic + W] + ni
                gs[pl.ds(r0, 8), rc:rc + W] = gr
                gs[pl.ds(r0, 8), ic:ic + W] = gi
                xr = st[pl.ds(rp, 8), rc:rc + W]
                xi = st[pl.ds(rp, 8), ic:ic + W]
                return nr, ni, accr + gr * xr + gi * xi, acci + gi * xr - gr * xi

            kr, ki, accr, acci = lax.fori_loop(0, S - 1, fix, (cr, ci, zero, zero))
            nr = ar * kr + ai * ki
            ni = ar * ki - ai * kr
            gr = gs[0:8, rc:rc + W] + nr
            gi = gs[0:8, ic:ic + W] + ni
            gs[0:8, rc:rc + W] = gr
            gs[0:8, ic:ic + W] = gi
            xr = cin_ref[0, :, rc:rc + W]
            xi = cin_ref[0, :, ic:ic + W]
            da_ref[:, rc:rc + W] += accr + gr * xr + gi * xi
            da_ref[:, ic:ic + W] += acci + gi * xr - gr * xi

        dups = []
        for s in range(NS):
            gsb = gs[:, 2 * SH * s:2 * SH * (s + 1)].astype(bf16)
            dups.append(lax.dot_general(gsb, bs_ref[s], (((1,), (1,)), ((), ())), preferred_element_type=f32))
            db_acc[s] += lax.dot_general(upb[:, SC * s:SC * (s + 1)], gsb, (((0,), (0,)), ((), ())),
                                         preferred_element_type=f32)
            dc_acc[s] += lax.dot_general(st[:, 2 * SH * s:2 * SH * (s + 1)].astype(bf16), dypb[:, SC * s:SC * (s + 1)],
                                         (((0,), (0,)), ((), ())), preferred_element_type=f32)
        dup = jnp.concatenate(dups, axis=1) + dskip_ref[...] * dyp
        du_ref[...] = jnp.dot(pmt_ref[...], dup.astype(bf16), preferred_element_type=f32).astype(bf16)
        dd_ref[...] += jnp.sum(dyp * up, axis=0, keepdims=True)

        @pl.when(c == NC - 1)
        def _():
            PS, GH = _SSM_STATE, _SSM_GROUP
            mask_b = (lax.broadcasted_iota(jnp.int32, (SC, SH), 0) // GH
                      == lax.broadcasted_iota(jnp.int32, (SC, SH), 1) // PS)
            mask_c = (lax.broadcasted_iota(jnp.int32, (SH, SC), 0) // PS
                      == lax.broadcasted_iota(jnp.int32, (SH, SC), 1) // GH)
            for s in range(NS):
                for r in range(2):
                    xb = jnp.where(mask_b, db_acc[s, :, r * SH:(r + 1) * SH], 0.0)
                    zb = xb[:, 0:128]
                    for q in range(1, SH // 128):
                        zb = zb + xb[:, q * 128:(q + 1) * 128]
                    db_ref[s, r] = zb + pltpu.roll(zb, PS, axis=1)
                    xc = jnp.where(mask_c, dc_acc[s, r * SH:(r + 1) * SH, :], 0.0)
                    zc = xc[0:PS]
                    for q in range(1, SH // PS):
                        zc = zc + xc[q * PS:(q + 1) * PS]
                    dc_ref[s, r] = zc

    rev = lambda c: (NC - 1 - c, 0)
    const2 = lambda shape: pl.BlockSpec(shape, lambda c: (0, 0))
    const3 = lambda shape: pl.BlockSpec(shape, lambda c: (0, 0, 0))
    const4 = lambda shape: pl.BlockSpec(shape, lambda c: (0, 0, 0, 0))
    return pl.pallas_call(
        body, grid=(NC,),
        in_specs=[pl.BlockSpec((T, SW), rev), pl.BlockSpec((T, SW), rev), pl.BlockSpec((1, 8, NST), lambda c: (NC - 1 - c, 0, 0)),
                  const2((T, T)), const2((T, T)), const3((NS, SC, 2 * SH)), const3((NS, 2 * SH, SC)),
                  const2((8, NST)), const2((8, NST)), const2((1, SW)), _ANY] + [_ANY] * len(after),
        out_specs=[pl.BlockSpec((T, SW), lambda c: (NC - 1 - c, du_col)), const2((8, NST)),
                   const4((NS, 2, SC, 128)), const4((NS, 2, _SSM_STATE, SC)), const2((1, SW))],
        out_shape=[jax.ShapeDtypeStruct(dproj.shape, bf16), jax.ShapeDtypeStruct((8, NST), f32),
                   jax.ShapeDtypeStruct((NS, 2, SC, 128), f32), jax.ShapeDtypeStruct((NS, 2, _SSM_STATE, SC), f32),
                   jax.ShapeDtypeStruct((1, SW), f32)],
        scratch_shapes=[pltpu.VMEM((T, NST), f32), pltpu.VMEM((T, NST), f32), pltpu.VMEM((8, NST), f32),
                        pltpu.VMEM((8, NST), f32), pltpu.VMEM((NS, SC, 2 * SH), f32), pltpu.VMEM((NS, 2 * SH, SC), f32)],
        input_output_aliases={10: 0},
        name="s5_bwd", compiler_params=_cparams(("arbitrary",)))(proj, dy, cin, pm, pmt, bs, cs, arep, apow, dskip, dproj, *after)


def _s5_param_grads(cfg, da, db_diag, dc_diag):
    G, NS, SC, SH, NST = _s5_dims(cfg)
    das = da.sum(axis=0).reshape(NS, 2, SH)
    dabr = das[:, 0].reshape(G, _SSM_STATE)
    dabi = das[:, 1].reshape(G, _SSM_STATE)
    dbd = db_diag[..., :_SSM_STATE].reshape(NS, 2, _SLAB_GROUPS, _SSM_GROUP, _SSM_STATE)
    dbb = dbd.transpose(1, 0, 2, 4, 3).reshape(2, G, _SSM_STATE, _SSM_GROUP)
    dcd = dc_diag.reshape(NS, 2, _SSM_STATE, _SLAB_GROUPS, _SSM_GROUP)
    dcc = dcd.transpose(1, 0, 3, 4, 2).reshape(2, G, _SSM_GROUP, _SSM_STATE)
    return dabr, dabi, dbb[0], dbb[1], dcc[0], -dcc[1]


def _coords():
    return lax.axis_index("x"), lax.axis_index("y"), lax.axis_index("c")


def _other_chips(x, y):
    return [(1 - x, y), (x, 1 - y), (1 - x, 1 - y)]


def _allreduce8(name, v):
    R = v.shape[0]

    def body(v_ref, o_ref, sib, chips, mine, ssem, rsem):
        x, y, c = _coords()
        d2d = pltpu.make_async_remote_copy(src_ref=v_ref, dst_ref=sib, send_sem=ssem.at[0], recv_sem=rsem.at[0],
                                           device_id=(x, y, 1 - c), device_id_type=_MESH)
        d2d.start()
        d2d.wait()
        mine[...] = v_ref[...] + sib[...]
        cps = [pltpu.make_async_remote_copy(src_ref=mine, dst_ref=chips.at[j], send_sem=ssem.at[1 + j],
                                            recv_sem=rsem.at[1 + j], device_id=(*chip, c), device_id_type=_MESH)
               for j, chip in enumerate(_other_chips(x, y))]
        for cp in cps:
            cp.start()
        for cp in cps:
            cp.wait()
        o_ref[...] = (mine[...] + chips[1]) + (chips[0] + chips[2])

    vm = pl.BlockSpec(memory_space=pltpu.VMEM)
    return pl.pallas_call(
        body, in_specs=[vm], out_specs=vm, out_shape=jax.ShapeDtypeStruct((R, 128), f32),
        scratch_shapes=[pltpu.VMEM((R, 128), f32), pltpu.VMEM((3, R, 128), f32), pltpu.VMEM((R, 128), f32),
                        pltpu.SemaphoreType.DMA((4,)), pltpu.SemaphoreType.DMA((4,))],
        name=name, compiler_params=pltpu.CompilerParams(vmem_limit_bytes=_VMEM_LIMIT))(v)


def _cast_into_slot(name, w, k_idx):
    R, C = w.shape
    tr = _tile(R, 256, 16)

    def body(k_ref, w_ref, o_ref):
        o_ref[...] = w_ref[...].astype(bf16)

    gs = pltpu.PrefetchScalarGridSpec(
        num_scalar_prefetch=1, grid=(R // tr,),
        in_specs=[pl.BlockSpec((tr, C), lambda r, kr: (r, 0))],
        out_specs=pl.BlockSpec((None, tr, C), lambda r, kr: (kr[0], r, 0)))
    return pl.pallas_call(body, grid_spec=gs, out_shape=jax.ShapeDtypeStruct((_NCHIP, R, C), bf16), name=name,
                          compiler_params=_cparams(("parallel",)))(k_idx, w)


def _handshake(peers):
    barrier = pltpu.get_barrier_semaphore()
    for peer in peers:
        pl.semaphore_signal(barrier, inc=1, device_id=peer, device_id_type=_MESH)
    pl.semaphore_wait(barrier, len(peers))


def _allgather_weights(name, bufs, collective_id):
    n = len(bufs)
    refs = [jax.new_ref(b, memory_space=pltpu.MemorySpace.HBM) for b in bufs]

    @pl.kernel(mesh=plsc.ScalarSubcoreMesh(axis_name="seq", num_cores=1), name=name,
               scratch_types=(pltpu.SemaphoreType.DMA((n, 3)), pltpu.SemaphoreType.DMA((n, 3)),
                              pltpu.SemaphoreType.DMA((n, 3)), pltpu.SemaphoreType.DMA((n, 3))),
               compiler_params=pltpu.CompilerParams(collective_id=collective_id))
    def launch(ssem, rsem, fssem, frsem):
        x, y, c = _coords()
        k = 2 * x + y
        others = _other_chips(x, y)
        _handshake([(x, y, 1 - c)] + [(*chip, c) for chip in others])
        sends, fwds = [], []
        for w in range(n):
            rh = refs[w].shape[1] // 2
            mine = refs[w].at[k, pl.ds(c * rh, rh)]
            for j, chip in enumerate(others):
                cp = pltpu.make_async_remote_copy(
                    src_ref=mine, dst_ref=mine, send_sem=ssem.at[w, j], recv_sem=rsem.at[w, j],
                    device_id=(*chip, c), device_id_type=_MESH)
                cp.start()
                sends.append(cp)
        for w in range(n):
            rh = refs[w].shape[1] // 2
            for j, (ox, oy) in enumerate(others):
                landed = refs[w].at[2 * ox + oy, pl.ds(c * rh, rh)]
                pltpu.make_async_remote_copy(
                    src_ref=landed, dst_ref=landed, send_sem=ssem.at[w, j], recv_sem=rsem.at[w, j],
                    device_id=(ox, oy, c), device_id_type=_MESH).wait_recv()
                cp = pltpu.make_async_remote_copy(
                    src_ref=landed, dst_ref=landed, send_sem=fssem.at[w, j], recv_sem=frsem.at[w, j],
                    device_id=(x, y, 1 - c), device_id_type=_MESH)
                cp.start()
                fwds.append(cp)
        for w in range(n):
            rh = refs[w].shape[1] // 2
            for j, (ox, oy) in enumerate(others):
                passed = refs[w].at[2 * ox + oy, pl.ds((1 - c) * rh, rh)]
                pltpu.make_async_remote_copy(
                    src_ref=passed, dst_ref=passed, send_sem=fssem.at[w, j], recv_sem=frsem.at[w, j],
                    device_id=(x, y, 1 - c), device_id_type=_MESH).wait_recv()
        for cp in sends + fwds:
            cp.wait_send()

    launch()
    return [r[...] for r in refs]


def _sequencer_kernel(name, collective_id, sems, body):
    pl.kernel(body, mesh=plsc.ScalarSubcoreMesh(axis_name="seq", num_cores=1), name=name, scratch_types=sems,
              compiler_params=pltpu.CompilerParams(collective_id=collective_id))()


def _hbm_ref(a):
    return jax.new_ref(a, memory_space=pltpu.MemorySpace.HBM)


def _exchange_halves(name, grads, collective_id):
    n = len(grads)
    srcs = [_hbm_ref(g) for g in grads]
    dsts = [jax.empty_ref(jax.ShapeDtypeStruct((g.shape[0], g.shape[1] // 2, g.shape[2]), g.dtype),
                          memory_space=pltpu.MemorySpace.HBM) for g in grads]

    def body(ssem, rsem):
        x, y, c = _coords()
        _handshake([(x, y, 1 - c)])
        cps = []
        for w in range(n):
            rh = srcs[w].shape[1] // 2
            cp = pltpu.make_async_remote_copy(
                src_ref=srcs[w].at[:, pl.ds((1 - c) * rh, rh)], dst_ref=dsts[w], send_sem=ssem.at[w], recv_sem=rsem.at[w],
                device_id=(x, y, 1 - c), device_id_type=_MESH)
            cp.start()
            cps.append(cp)
        for cp in cps:
            cp.wait()

    _sequencer_kernel(name, collective_id, (pltpu.SemaphoreType.DMA((n,)), pltpu.SemaphoreType.DMA((n,))), body)
    return [d[...] for d in dsts]


def _scatter_shards(name, parts, collective_id):
    n = len(parts)
    srcs = [_hbm_ref(p) for p in parts]
    dsts = [jax.empty_ref(jax.ShapeDtypeStruct((3,) + p.shape[1:], p.dtype), memory_space=pltpu.MemorySpace.HBM)
            for p in parts]

    def body(ssem, rsem):
        x, y, c = _coords()
        others = _other_chips(x, y)
        _handshake([(*chip, c) for chip in others])
        cps = []
        for w in range(n):
            for j, (ox, oy) in enumerate(others):
                cp = pltpu.make_async_remote_copy(
                    src_ref=srcs[w].at[2 * ox + oy], dst_ref=dsts[w].at[j], send_sem=ssem.at[w, j], recv_sem=rsem.at[w, j],
                    device_id=(ox, oy, c), device_id_type=_MESH)
                cp.start()
                cps.append(cp)
        for cp in cps:
            cp.wait()

    _sequencer_kernel(name, collective_id, (pltpu.SemaphoreType.DMA((n, 3)), pltpu.SemaphoreType.DMA((n, 3))), body)
    return [d[...] for d in dsts]


def _join_halves(name, bufs, collective_id):
    n = len(bufs)
    refs = [_hbm_ref(b) for b in bufs]

    def body(ssem, rsem):
        x, y, c = _coords()
        _handshake([(x, y, 1 - c)])
        cps = []
        for w in range(n):
            rh = refs[w].shape[0] // 2
            mine = refs[w].at[pl.ds(c * rh, rh)]
            cp = pltpu.make_async_remote_copy(src_ref=mine, dst_ref=mine, send_sem=ssem.at[w], recv_sem=rsem.at[w],
                                              device_id=(x, y, 1 - c), device_id_type=_MESH)
            cp.start()
            cps.append(cp)
        for w in range(n):
            rh = refs[w].shape[0] // 2
            theirs = refs[w].at[pl.ds((1 - c) * rh, rh)]
            pltpu.make_async_remote_copy(src_ref=theirs, dst_ref=theirs, send_sem=ssem.at[w], recv_sem=rsem.at[w],
                                         device_id=(x, y, 1 - c), device_id_type=_MESH).wait_recv()
        for cp in cps:
            cp.wait_send()

    _sequencer_kernel(name, collective_id, (pltpu.SemaphoreType.DMA((n,)), pltpu.SemaphoreType.DMA((n,))), body)
    return [r[...] for r in refs]


def _add_own_half(name, g, t, c_idx, after):
    S, R, C = g.shape
    rh = R // 2
    tr = _tile(rh, 256, 16)
    nrb = rh // tr

    def body(c_ref, g_ref, t_ref, *rest):
        rest[-1][...] = (g_ref[...] + t_ref[...]).astype(bf16)

    gs = pltpu.PrefetchScalarGridSpec(
        num_scalar_prefetch=1, grid=(S, nrb),
        in_specs=[pl.BlockSpec((None, tr, C), lambda s, r, cr: (s, cr[0] * nrb + r, 0)),
                  pl.BlockSpec((None, tr, C), lambda s, r, cr: (s, r, 0))] + [_ANY] * len(after),
        out_specs=pl.BlockSpec((None, tr, C), lambda s, r, cr: (s, r, 0)))
    return pl.pallas_call(body, grid_spec=gs, out_shape=jax.ShapeDtypeStruct((S, rh, C), bf16), name=name,
                          compiler_params=_cparams(("parallel", "parallel")))(c_idx, g, t, *after)


def _add_shard_parts(name, g, t, r, kc_idx, after):
    S, R, C = g.shape
    rh = R // 2
    tr = _tile(rh, 256, 16)
    nrb = rh // tr

    def body(kc_ref, g_ref, t_ref, r_ref, *rest):
        own = g_ref[...] + t_ref[...]
        rest[-1][...] = (own + r_ref[1].astype(f32)) + (r_ref[0].astype(f32) + r_ref[2].astype(f32))

    gs = pltpu.PrefetchScalarGridSpec(
        num_scalar_prefetch=1, grid=(nrb,),
        in_specs=[pl.BlockSpec((None, tr, C), lambda i, kc: (kc[0], kc[1] * nrb + i, 0)),
                  pl.BlockSpec((None, tr, C), lambda i, kc: (kc[0], i, 0)),
                  pl.BlockSpec((3, tr, C), lambda i, kc: (0, i, 0))] + [_ANY] * len(after),
        out_specs=pl.BlockSpec((tr, C), lambda i, kc: (kc[1] * nrb + i, 0)))
    return pl.pallas_call(body, grid_spec=gs, out_shape=jax.ShapeDtypeStruct((R, C), f32), name=name,
                          compiler_params=_cparams(("parallel",)))(kc_idx, g, t, r, *after)


def _adamw_update(wv, gv, mv, vv):
    nm = _ADAM_B1 * mv + (1.0 - _ADAM_B1) * gv
    nv = _ADAM_B2 * vv + (1.0 - _ADAM_B2) * (gv * gv)
    m_hat = nm / (1.0 - _ADAM_B1 ** _ADAM_STEP)
    v_hat = nv / (1.0 - _ADAM_B2 ** _ADAM_STEP)
    return -_ADAM_LR * (m_hat / (jnp.sqrt(v_hat) + _ADAM_EPS) + _ADAM_WD * wv), nm, nv


def _adamw(name, w, g, m, v, after=()):
    R, C = w.shape
    tr = _tile(R, 128, 8)

    def body(w_ref, g_ref, m_ref, v_ref, *rest):
        go_ref, d_ref, nm_ref, nv_ref = rest[len(after):]
        gv = g_ref[...]
        go_ref[...] = gv
        d_ref[...], nm_ref[...], nv_ref[...] = _adamw_update(w_ref[...], gv, m_ref[...], v_ref[...])

    spec = pl.BlockSpec((tr, C), lambda i: (i, 0))
    shape = jax.ShapeDtypeStruct((R, C), f32)
    return pl.pallas_call(body, grid=(R // tr,), in_specs=[spec] * 4 + [_ANY] * len(after), out_specs=[spec] * 4,
                          out_shape=[shape] * 4, name=name, compiler_params=_cparams(("parallel",)))(w, g, m, v, *after)


def _adamw_whole(name, w, g, m, v):
    def body(w_ref, g_ref, m_ref, v_ref, d_ref, nm_ref, nv_ref):
        d_ref[...], nm_ref[...], nv_ref[...] = _adamw_update(w_ref[...], g_ref[...], m_ref[...], v_ref[...])

    vm = pl.BlockSpec(memory_space=pltpu.VMEM)
    return pl.pallas_call(body, in_specs=[vm] * 4, out_specs=[vm] * 3, out_shape=[jax.ShapeDtypeStruct(w.shape, f32)] * 3,
                          name=name, compiler_params=pltpu.CompilerParams(vmem_limit_bytes=_VMEM_LIMIT))(w, g, m, v)


def _pack(arrs):
    flat = jnp.concatenate([a.reshape(-1).astype(f32) for a in arrs])
    n = flat.shape[0]
    pad = (-n) % (128 * 128)
    return jnp.pad(flat, (0, pad)).reshape(-1, 128)


def _unpack(packed, shapes):
    flat = packed.reshape(-1)
    out, off = [], 0
    for s in shapes:
        n = math.prod(s)
        out.append(flat[off:off + n].reshape(s))
        off += n
    return out


_BIG = ("w_in", "w_glu", "w_ssm_out", "w_conv_out", "w_o", "w_up", "w_down")
_SMALL = ("norm_tok", "a_re", "a_im", "log_dt", "b_re", "b_im", "c_re", "c_im", "d_skip", "conv_w", "conv_b",
          "norm_ffn", "ffn_conv_w", "ffn_conv_b", "norm_final")
_WEIGHTS = ("norm_tok", "w_in", "a_re", "a_im", "log_dt", "b_re", "b_im", "c_re", "c_im", "d_skip", "w_glu",
            "w_ssm_out", "conv_w", "conv_b", "w_conv_out", "w_o", "norm_ffn", "w_up", "ffn_conv_w", "ffn_conv_b",
            "w_down", "norm_final")


def _step(cfg, x, tgt, p, m, v):
    L, D, SW, CW, F = cfg.L, cfg.D, cfg.SW, cfg.CW, cfg.F
    xi, yi, ci = _coords()
    k_idx = (2 * xi + yi).astype(jnp.int32).reshape(1)
    c_idx = ci.astype(jnp.int32).reshape(1)
    x = x.reshape(L, D)
    tgt = tgt.reshape(L, D)

    big2d = {n: p[n].reshape(p[n].shape[-2], p[n].shape[-1]) for n in _BIG}
    slots = {n: _cast_into_slot("cast_" + n, big2d[n], k_idx) for n in _BIG}
    wg = {}
    for cid, (gname, group) in enumerate((("allgather_w_in", ("w_in",)),
                                          ("allgather_w_mixer", ("w_glu", "w_ssm_out", "w_conv_out", "w_o")),
                                          ("allgather_w_up", ("w_up",)), ("allgather_w_down", ("w_down",)))):
        wg.update(zip(group, _allgather_weights(gname, [slots[n] for n in group], cid)))
    w_in, w_so, w_co, w_up = wg["w_in"], wg["w_ssm_out"], wg["w_conv_out"], wg["w_up"]
    w_glu, w_o, w_down = wg["w_glu"], wg["w_o"], wg["w_down"]
    kk = k_idx[0]
    cw_full = lax.dynamic_update_slice(jnp.zeros((3, CW), f32), p["conv_w"].reshape(3, CW // _NCHIP), (0, kk * (CW // _NCHIP)))
    fw_full = lax.dynamic_update_slice(jnp.zeros((3, F), f32), p["ffn_conv_w"].reshape(3, F // _NCHIP), (0, kk * (F // _NCHIP)))
    south = (ci == 0).astype(f32)
    conv_w, ffn_conv_w = _unpack(_allreduce8("allgather_conv_filters", _pack([cw_full * south, fw_full * south])),
                                 [(3, CW), (3, F)])
    conv_b = p["conv_b"].reshape(1, CW)
    ffn_conv_b = p["ffn_conv_b"].reshape(1, F)
    norm_tok = p["norm_tok"].reshape(1, D)
    norm_ffn = p["norm_ffn"].reshape(1, D)
    norm_final = p["norm_final"].reshape(1, D)
    dskip = p["d_skip"].reshape(1, SW)

    s5_in = (p["a_re"][0], p["a_im"][0], p["log_dt"][0], p["b_re"][0], p["b_im"][0])
    (abr, abi, bbr, bbi), disc_vjp = jax.vjp(_discretize, *s5_in)
    tabs = _s5_tables(cfg, abr, abi, bbr, bbi, p["c_re"][0], p["c_im"][0])

    xn1, r1 = _rms_fwd("rms_tok", x, norm_tok)
    proj = _mm_nn("in_proj", xn1, w_in, [bf16])[0]
    y_s, ya0, cin = _s5_fwd(cfg, proj, tabs, dskip)

    def tiles(*arrs):
        return lambda tm, tn: [(a, pl.BlockSpec((tm, tn), lambda i, j: (i, j))) for a in arrs]

    def glu_epi(acc, e, o):
        o[0][...] = (e[0][...].astype(f32) * jax.nn.sigmoid(acc)).astype(bf16)
        o[1][...] = acc.astype(bf16)

    ya1, z = _mm_nn("glu", ya0, w_glu, [bf16, bf16], rows=True, extras_fn=tiles(ya0), epilogue=glu_epi)
    yb0 = _convb_fwd(cfg, proj, conv_w, conv_b)
    merged, ya, yb = _merge_fwd(cfg, ya1, yb0, proj, w_so, w_co)

    def res_epi(acc, e, o):
        o[0][...] = e[0][...] + acc

    h1 = _mm_nn("out_proj", merged, w_o, [f32], rows=True, extras_fn=tiles(x), epilogue=res_epi)[0]
    xn2, r2 = _rms_fwd("rms_ffn", h1, norm_ffn)
    hh = _mm_nn("ffn_up", xn2, w_up, [bf16], tn=1408)[0]
    fact = _ffn_act(cfg, hh, ffn_conv_w, ffn_conv_b)
    h2 = _mm_nn("ffn_down", fact, w_down, [f32], tm=512, rows=True, extras_fn=tiles(h1), epilogue=res_epi)[0]
    dh2, dh2b, g_norm_final, loss_tile = _loss_head("loss_head", h2, tgt, norm_final)

    kc_idx = jnp.concatenate([k_idx, c_idx])
    reduced, chains = {}, {}

    def rs_halves(tag, collective_id, names, gs):
        chains[tag] = dict(cid=collective_id, names=names, gs=gs,
                           sib=_exchange_halves("grad_halves_" + tag, gs, collective_id))

    def rs_shards(tag, after):
        ch = chains[tag]
        ch["parts"] = [_add_own_half("grad_add_halves_" + n, g, t, c_idx, after)
                       for n, g, t in zip(ch["names"], ch["gs"], ch["sib"])]
        ch["chips"] = _scatter_shards("grad_shards_" + tag, ch["parts"], ch["cid"] + 1)
        return ch["parts"]

    def rs_join(tag, after):
        ch = chains[tag]
        ch["halves"] = [_add_shard_parts("grad_add_chips_" + n, g, t, r, kc_idx, after)
                        for n, g, t, r in zip(ch["names"], ch["gs"], ch["sib"], ch["chips"])]
        reduced.update(zip(ch["names"], _join_halves("grad_join_" + tag, ch["halves"], ch["cid"] + 2)))
        return ch["halves"]

    df = _mm_nt("ffn_down_dx", dh2b, w_down, [bf16], tn=1408, rows=True)[0]
    g_w_down = _mm_tn("ffn_down_dw", fact, dh2b, tm=1408, tn=512)
    rs_halves("ffn_down", 4, ["w_down"], [g_w_down.reshape(_NCHIP, F // _NCHIP, D)])
    dhh, g_ffn_conv_w, g_ffn_conv_b = _ffn_act_bwd(cfg, hh, df, ffn_conv_w, ffn_conv_b)
    sent = rs_shards("ffn_down", [dhh])
    g_w_up = _mm_tn("ffn_up_dw", xn2, dhh, shards=_NCHIP, tm=512, tn=1408, b_resident=True, after=sent)
    rs_halves("ffn_up", 7, ["w_up"], [g_w_up])
    dxn2 = _ffn_up_dx(cfg, dhh, w_up)
    sent = rs_shards("ffn_up", [dxn2]) + rs_join("ffn_down", [dxn2])
    dh1, dh1b, g_norm_ffn = _rms_bwd("rms_ffn_bwd", dxn2, h1, r2, norm_ffn, dh2, after=sent)

    dya, dyb, dproj = _out_proj_dx(cfg, dh1b, w_o, ya, yb, proj)
    g_w_o = _mm_tn("out_proj_dw", merged, dh1b)

    def glu_bwd_epi(acc, e, o):
        a0 = e[0][...].astype(f32)
        s = jax.nn.sigmoid(e[1][...].astype(f32))
        o[0][...] = (acc * a0 * s * (1.0 - s)).astype(bf16)
        o[1][...] = (acc * s).astype(bf16)

    dz, t1 = _mm_nt("ssm_out_dx", dya, w_so, [bf16, bf16], extras_fn=tiles(ya0, z), epilogue=glu_bwd_epi)
    g_w_so = _mm_tn("ssm_out_dw", ya1, dya, shards=_NCHIP, tn=512)
    dyb0 = _mm_nt("conv_out_dx", dyb, w_co, [bf16])[0]
    g_w_co = _mm_tn("conv_out_dw", yb0, dyb, shards=_NCHIP, tn=512)
    dproj, g_conv_w, g_conv_b = _convb_bwd(cfg, proj, dyb0, conv_w, conv_b, dproj)

    def gelu_bwd_epi(acc, e, o):
        o[0][...] = ((e[0][...].astype(f32) + acc) * _gelu_grad(e[1][...].astype(f32))).astype(bf16)

    dy_s = _mm_nt("glu_dx", dz, w_glu, [bf16], rows=True, extras_fn=tiles(t1, y_s), epilogue=gelu_bwd_epi)[0]
    g_w_glu = _mm_tn("glu_dw", ya0, dz)
    rs_halves("mixer", 10, ["w_o", "w_ssm_out", "w_conv_out", "w_glu"],
              [g_w_o.reshape(_NCHIP, D // _NCHIP, D), g_w_so, g_w_co, g_w_glu.reshape(_NCHIP, SW // _NCHIP, SW)])
    sent = rs_join("ffn_up", [g_w_glu])
    dproj, da_acc, db_full, dc_full, g_dskip = _s5_bwd(cfg, proj, dy_s, cin, tabs, dskip, dproj, after=sent)
    sent = rs_shards("mixer", [dproj])
    g_w_in = _mm_tn("in_proj_dw", xn1, dproj, shards=_NCHIP, tn=CW, after=sent,
                    b_block=lambda j: jnp.where(j == 0, 3 * CW // SW, jnp.where(j < 4, j - 1, j)))
    rs_halves("in_proj", 13, ["w_in"], [g_w_in])
    dxn1 = _in_proj_dx(cfg, dproj, w_in)
    sent = rs_shards("in_proj", [dxn1]) + rs_join("mixer", [dxn1])
    dx, _, g_norm_tok = _rms_bwd("rms_tok_bwd", dxn1, x, r1, norm_tok, dh1, after=sent)

    dabr, dabi, dbbr, dbbi, g_c_re, g_c_im = _s5_param_grads(cfg, da_acc, db_full, dc_full)
    g_a_re, g_a_im, g_log_dt, g_b_re, g_b_im = disc_vjp((dabr, dabi, dbbr, dbbi))

    small_g = {"norm_tok": g_norm_tok, "a_re": g_a_re, "a_im": g_a_im, "log_dt": g_log_dt, "b_re": g_b_re, "b_im": g_b_im,
               "c_re": g_c_re, "c_im": g_c_im, "d_skip": g_dskip, "conv_w": g_conv_w, "conv_b": g_conv_b,
               "norm_ffn": g_norm_ffn, "ffn_conv_w": g_ffn_conv_w, "ffn_conv_b": g_ffn_conv_b, "norm_final": g_norm_final}
    small_shapes = [small_g[n].shape for n in _SMALL]
    summed = dict(zip(_SMALL, _unpack(_allreduce8("allreduce_small_grads", _pack([small_g[n] for n in _SMALL])), small_shapes)))
    summed["conv_w"] = lax.dynamic_slice(summed["conv_w"], (0, kk * (CW // _NCHIP)), (3, CW // _NCHIP))
    summed["ffn_conv_w"] = lax.dynamic_slice(summed["ffn_conv_w"], (0, kk * (F // _NCHIP)), (3, F // _NCHIP))

    grads, deltas, new_m, new_v = {}, {}, {}, {}

    def adamw_big(names, after):
        for n in names:
            g_, d_, m_, v_ = _adamw("adamw_" + n, big2d[n], reduced[n], m[n].reshape(big2d[n].shape),
                                    v[n].reshape(big2d[n].shape), after=after)
            grads[n], deltas[n], new_m[n], new_v[n] = (a.reshape(p[n].shape) for a in (g_, d_, m_, v_))
            after = [d_]
        return after

    for n in _SMALL:
        grads[n] = summed[n].reshape(p[n].shape)
        deltas[n], new_m[n], new_v[n] = _adamw_whole("adamw_" + n, p[n], grads[n], m[n], v[n])
    done = adamw_big(["w_down", "w_up", "w_o", "w_ssm_out", "w_conv_out", "w_glu"], [deltas["norm_final"]])
    rs_join("in_proj", done)
    adamw_big(["w_in"], ())

    loss = lax.psum(loss_tile[0, 0], ("x", "y", "c"))
    return (loss, dx.reshape(1, L, D), *[grads[n] for n in _WEIGHTS], *[deltas[n] for n in _WEIGHTS],
            *[new_m[n] for n in _WEIGHTS], *[new_v[n] for n in _WEIGHTS])


def kernel(x, norm_tok, w_in, a_re, a_im, log_dt, b_re, b_im, c_re, c_im, d_skip, w_glu, w_ssm_out, conv_w, conv_b, w_conv_out, w_o, norm_ffn, w_up, ffn_conv_w, ffn_conv_b, w_down, norm_final, loss_target, m_norm_tok, m_w_in, m_a_re, m_a_im, m_log_dt, m_b_re, m_b_im, m_c_re, m_c_im, m_d_skip, m_w_glu, m_w_ssm_out, m_conv_w, m_conv_b, m_w_conv_out, m_w_o, m_norm_ffn, m_w_up, m_ffn_conv_w, m_ffn_conv_b, m_w_down, m_norm_final, v_norm_tok, v_w_in, v_a_re, v_a_im, v_log_dt, v_b_re, v_b_im, v_c_re, v_c_im, v_d_skip, v_w_glu, v_w_ssm_out, v_conv_w, v_conv_b, v_w_conv_out, v_w_o, v_norm_ffn, v_w_up, v_ffn_conv_w, v_ffn_conv_b, v_w_down, v_norm_final):
    p = dict(norm_tok=norm_tok, w_in=w_in, a_re=a_re, a_im=a_im, log_dt=log_dt, b_re=b_re, b_im=b_im, c_re=c_re,
             c_im=c_im, d_skip=d_skip, w_glu=w_glu, w_ssm_out=w_ssm_out, conv_w=conv_w, conv_b=conv_b,
             w_conv_out=w_conv_out, w_o=w_o, norm_ffn=norm_ffn, w_up=w_up, ffn_conv_w=ffn_conv_w,
             ffn_conv_b=ffn_conv_b, w_down=w_down, norm_final=norm_final)
    m = dict(norm_tok=m_norm_tok, w_in=m_w_in, a_re=m_a_re, a_im=m_a_im, log_dt=m_log_dt, b_re=m_b_re, b_im=m_b_im,
             c_re=m_c_re, c_im=m_c_im, d_skip=m_d_skip, w_glu=m_w_glu, w_ssm_out=m_w_ssm_out, conv_w=m_conv_w,
             conv_b=m_conv_b, w_conv_out=m_w_conv_out, w_o=m_w_o, norm_ffn=m_norm_ffn, w_up=m_w_up,
             ffn_conv_w=m_ffn_conv_w, ffn_conv_b=m_ffn_conv_b, w_down=m_w_down, norm_final=m_norm_final)
    v = dict(norm_tok=v_norm_tok, w_in=v_w_in, a_re=v_a_re, a_im=v_a_im, log_dt=v_log_dt, b_re=v_b_re, b_im=v_b_im,
             c_re=v_c_re, c_im=v_c_im, d_skip=v_d_skip, w_glu=v_w_glu, w_ssm_out=v_w_ssm_out, conv_w=v_conv_w,
             conv_b=v_conv_b, w_conv_out=v_w_conv_out, w_o=v_w_o, norm_ffn=v_norm_ffn, w_up=v_w_up,
             ffn_conv_w=v_ffn_conv_w, ffn_conv_b=v_ffn_conv_b, w_down=v_w_down, norm_final=v_norm_final)
    return _step(_Cfg(), x, loss_target, p, m, v)
```

```python
import functools
import math
from typing import NamedTuple

import jax
import jax.numpy as jnp
from jax import lax
from jax.experimental import pallas as pl
from jax.experimental.pallas import tpu as pltpu
from jax.experimental.pallas import tpu_sc as plsc

f32 = jnp.float32
bf16 = jnp.bfloat16
_MESH = pl.DeviceIdType.MESH

_EPS = 1e-6
_ADAM_LR = 0.001
_ADAM_B1 = 0.9
_ADAM_B2 = 0.999
_ADAM_EPS = 1e-08
_ADAM_WD = 0.01
_ADAM_STEP = 10
_SSM_GROUP = 16
_SSM_STATE = 64
_SLAB_GROUPS = 16
_NCHIP = 4
_VMEM_LIMIT = 56 * 2**20
_GELU_C = math.sqrt(2.0 / math.pi)
_GELU_A = 0.044715


class _Cfg(NamedTuple):
    L: int = 4096
    D: int = 2048
    SW: int = 1024
    CW: int = 1024
    F: int = 5632
    T: int = 256


def _tile(n, pref, align):
    t = min(n, pref)
    t -= t % align
    while t > align and n % t:
        t -= align
    assert t > 0 and n % t == 0, (n, pref, align)
    return t


def _cparams(sem):
    return pltpu.CompilerParams(dimension_semantics=sem, vmem_limit_bytes=_VMEM_LIMIT)


def _gelu(x):
    return _gelu_and_grad(x)[0]


def _gelu_grad(x):
    return _gelu_and_grad(x)[1]


def _gelu_and_grad(x):
    x2 = x * x
    th = jnp.tanh(x * (_GELU_C + (_GELU_C * _GELU_A) * x2))
    half = 0.5 + 0.5 * th
    return x * half, half + (0.5 * x) * (1.0 - th * th) * (_GELU_C + (3.0 * _GELU_C * _GELU_A) * x2)


_NN = (((1,), (0,)), ((), ()))
_NT = (((1,), (1,)), ((), ()))
_TN = (((0,), (0,)), ((), ()))


def _whole(ref):
    return ref[...]


_ANY = pl.BlockSpec(memory_space=pl.ANY)


def _mm(name, operands, steps, *, grid, contract, outs, extras=(), epilogue=None, acc_shape=None, after=()):
    nop, ne, na = len(operands), len(extras), len(after)
    nk = len(steps)

    def body(*refs):
        op_refs = refs[:nop]
        e_refs = refs[nop:nop + ne]
        o_refs = refs[nop + ne + na:nop + ne + na + len(outs)]

        def partial(terms):
            tot = None
            for ai, av, bi, bv in terms:
                d = lax.dot_general(av(op_refs[ai]), bv(op_refs[bi]), contract, preferred_element_type=f32)
                tot = d if tot is None else tot + d
            return tot

        def finish(res):
            if epilogue is None:
                for o in o_refs:
                    o[...] = res.astype(o.dtype)
            else:
                epilogue(res, e_refs, o_refs)

        if nk == 1:
            finish(partial(steps[0][1]))
            return
        acc = refs[-1]
        kid = pl.program_id(len(grid) - 1)
        for k, terms in steps:
            def run(k=k, terms=terms):
                d = partial(terms)
                if k == 0:
                    acc[...] = d
                elif k < nk - 1:
                    acc[...] += d
                else:
                    finish(acc[...] + d)

            pl.when(kid == k)(run)

    sem = ("parallel",) * (len(grid) - (nk > 1)) + (("arbitrary",) if nk > 1 else ())
    return pl.pallas_call(
        body, grid=grid, in_specs=[o[1] for o in operands] + [e[1] for e in extras] + [_ANY] * na,
        out_specs=[o[1] for o in outs], out_shape=[o[0] for o in outs],
        scratch_shapes=[pltpu.VMEM(acc_shape, f32)] if nk > 1 else [], name=name,
        compiler_params=_cparams(sem))(*[o[0] for o in operands], *[e[0] for e in extras], *after)


def _mm_nn(name, a, w, out_dtypes, *, tm=1024, tn=1024, rows=False, extras_fn=None, epilogue=None):
    M, K = a.shape
    S, Ns = w.shape[0], w.shape[-1]
    N = Ns if rows else Ns * S
    tm, tn = _tile(M, tm, 16), _tile(Ns, tn, 128)
    nb = Ns // tn
    a_spec = pl.BlockSpec((tm, K), lambda i, j: (i, 0))
    if rows:
        b_spec = pl.BlockSpec((S, K // S, tn), lambda i, j: (0, 0, j))
        b_view = lambda r: r[...].reshape(K, tn)
    else:
        b_spec = pl.BlockSpec((None, K, tn), lambda i, j: (j // nb, 0, j % nb))
        b_view = _whole
    o_spec = pl.BlockSpec((tm, tn), lambda i, j: (i, j))
    outs = [(jax.ShapeDtypeStruct((M, N), dt), o_spec) for dt in out_dtypes]
    extras = extras_fn(tm, tn) if extras_fn is not None else ()
    return _mm(name, [(a, a_spec), (w, b_spec)], [(None, [(0, _whole, 1, b_view)])], grid=(M // tm, N // tn),
               contract=_NN, outs=outs, extras=extras, epilogue=epilogue)


def _mm_nt(name, a, w, out_dtypes, *, tm=1024, tn=1024, rows=False, extras_fn=None, epilogue=None):
    M, N = a.shape
    S, Ks, Ns = w.shape
    K = Ks * S if rows else Ks
    tm = _tile(M, tm, 16)
    a_spec = pl.BlockSpec((tm, N), lambda i, j: (i, 0))
    if rows:
        tn = K if tn >= K else _tile(Ks, tn, 128)
        if tn == K:
            b_spec = pl.BlockSpec((S, Ks, N), lambda i, j: (0, 0, 0))
            terms = [(0, _whole, 1, lambda r: r[...].reshape(K, N))]
        else:
            nbs = Ks // tn
            b_spec = pl.BlockSpec((None, tn, N), lambda i, j: (j // nbs, j % nbs, 0))
            terms = [(0, _whole, 1, _whole)]
    else:
        tn = _tile(K, tn, 128)
        assert S * Ns == N
        b_spec = pl.BlockSpec((S, tn, Ns), lambda i, j: (0, j, 0))
        terms = [(0, lambda r, s=s: r[:, s * Ns:(s + 1) * Ns], 1, lambda r, s=s: r[s]) for s in range(S)]
    o_spec = pl.BlockSpec((tm, tn), lambda i, j: (i, j))
    outs = [(jax.ShapeDtypeStruct((M, K), dt), o_spec) for dt in out_dtypes]
    extras = extras_fn(tm, tn) if extras_fn is not None else ()
    return _mm(name, [(a, a_spec), (w, b_spec)], [(None, terms)], grid=(M // tm, K // tn), contract=_NT,
               outs=outs, extras=extras, epilogue=epilogue)


def _mm_tn(name, a, b, *, shards=None, tm=1024, tn=1024, b_block=None, b_resident=False, after=()):
    M, K = a.shape
    halves = b.shape[0] if b.ndim == 3 else 1
    Nh = b.shape[-1]
    N = Nh * halves
    Ns = N // shards if shards else N
    tm, tn = _tile(K, tm, 128), _tile(math.gcd(Ns, Nh), tn, 128)
    nb, nbh = Ns // tn, Nh // tn
    ij = (lambda g0, g1: (g1, g0)) if b_resident else (lambda g0, g1: (g0, g1))
    bmap = b_block if b_block is not None else (lambda j: j)
    a_spec = pl.BlockSpec((M, tm), lambda g0, g1: (0, ij(g0, g1)[0]))
    if halves > 1:
        b_spec = pl.BlockSpec((None, M, tn), lambda g0, g1: (bmap(ij(g0, g1)[1]) // nbh, 0, bmap(ij(g0, g1)[1]) % nbh))
    else:
        b_spec = pl.BlockSpec((M, tn), lambda g0, g1: (0, bmap(ij(g0, g1)[1])))
    if shards:
        out = (jax.ShapeDtypeStruct((shards, K, Ns), f32),
               pl.BlockSpec((None, tm, tn), lambda g0, g1: (ij(g0, g1)[1] // nb, ij(g0, g1)[0], ij(g0, g1)[1] % nb)))
    else:
        out = (jax.ShapeDtypeStruct((K, N), f32), pl.BlockSpec((tm, tn), lambda g0, g1: ij(g0, g1)))
    grid = (N // tn, K // tm) if b_resident else (K // tm, N // tn)
    return _mm(name, [(a, a_spec), (b, b_spec)], [(None, [(0, _whole, 1, _whole)])], grid=grid, contract=_TN,
               outs=[out], after=after)[0]


def _in_proj_dx(cfg, dproj, w_in):
    L, D, SW, CW = cfg.L, cfg.D, cfg.SW, cfg.CW
    NP = SW + 3 * CW + 2 * D
    Ns = NP // _NCHIP
    assert SW + CW == Ns and 2 * CW == Ns and D == Ns
    tm, tn = _tile(L, 1024, 16), _tile(D, 1024, 128)
    a_spec = pl.BlockSpec((tm, NP // 2), lambda i, j, k: (i, k))
    b_spec = pl.BlockSpec((2, tn, Ns), lambda i, j, k: (k, j, 0))
    first = [(0, lambda r: r[:, 0:CW], 1, lambda r: r[0, :, SW:SW + CW]),
             (0, lambda r: r[:, CW:3 * CW], 1, lambda r: r[1]),
             (0, lambda r: r[:, 3 * CW:3 * CW + SW], 1, lambda r: r[0, :, 0:SW])]
    second = [(0, lambda r: r[:, 0:D], 1, lambda r: r[0]), (0, lambda r: r[:, D:2 * D], 1, lambda r: r[1])]
    out = (jax.ShapeDtypeStruct((L, D), bf16), pl.BlockSpec((tm, tn), lambda i, j, k: (i, j)))
    return _mm("in_proj_dx", [(dproj, a_spec), (w_in, b_spec)], [(0, first), (1, second)], grid=(L // tm, D // tn, 2),
               contract=_NT, outs=[out], acc_shape=(tm, tn))[0]


def _out_proj_dx(cfg, dh1b, w_o, ya, yb, proj):
    L, D = cfg.L, cfg.D
    NP = cfg.SW + 3 * cfg.CW + 2 * D
    assert NP == 4 * D
    tm = _tile(L, 512, 16)

    def epilogue(acc, e, o):
        sa = jax.nn.sigmoid(e[2][:, 0:D].astype(f32))
        sb = jax.nn.sigmoid(e[2][:, D:2 * D].astype(f32))
        o[0][...] = (acc * sa).astype(bf16)
        o[1][...] = (acc * sb).astype(bf16)
        o[2][:, 0:D] = (acc * e[0][...].astype(f32) * sa * (1.0 - sa)).astype(bf16)
        o[2][:, D:2 * D] = (acc * e[1][...].astype(f32) * sb * (1.0 - sb)).astype(bf16)

    row = pl.BlockSpec((tm, D), lambda i, j: (i, 0))
    half = pl.BlockSpec((tm, 2 * D), lambda i, j: (i, 1))
    return _mm("out_proj_dx", [(dh1b, row), (w_o, pl.BlockSpec(w_o.shape, lambda i, j: (0, 0, 0),
                                                                pipeline_mode=pl.Buffered(1)))],
               [(None, [(0, _whole, 1, lambda r: r[...].reshape(D, D))])], grid=(L // tm, 1), contract=_NT,
               outs=[(jax.ShapeDtypeStruct((L, D), bf16), row), (jax.ShapeDtypeStruct((L, D), bf16), row),
                     (jax.ShapeDtypeStruct((L, NP), bf16), half)],
               extras=[(ya, row), (yb, row), (proj, half)], epilogue=epilogue)


def _ffn_up_dx(cfg, dhh, w_up):
    L, D, F = cfg.L, cfg.D, cfg.F
    Fh = F // 2
    tm, tn = _tile(L, 1024, 16), _tile(D, 512, 128)
    a_spec = pl.BlockSpec((None, tm, F), lambda i, j, k: (k, i, 0))
    b_spec = pl.BlockSpec((2, tn, Fh), lambda i, j, k: (k, j, 0))
    terms = [(0, lambda r: r[:, 0:Fh], 1, lambda r: r[0]), (0, lambda r: r[:, Fh:F], 1, lambda r: r[1])]
    out = (jax.ShapeDtypeStruct((L, D), bf16), pl.BlockSpec((tm, tn), lambda i, j, k: (i, j)))
    return _mm("ffn_up_dx", [(dhh, a_spec), (w_up, b_spec)], [(0, terms), (1, terms)], grid=(L // tm, D // tn, 2),
               contract=_NT, outs=[out], acc_shape=(tm, tn))[0]


def _rms_fwd(name, x, g):
    L, D = x.shape
    tm = _tile(L, 256, 16)

    def body(x_ref, g_ref, xn_ref, r_ref):
        xv = x_ref[...]
        r = lax.rsqrt(jnp.mean(xv * xv, axis=-1, keepdims=True) + _EPS)
        xn_ref[...] = (xv * r * g_ref[...]).astype(bf16)
        r_ref[...] = r

    return pl.pallas_call(
        body, grid=(L // tm,),
        in_specs=[pl.BlockSpec((tm, D), lambda i: (i, 0)), pl.BlockSpec((1, D), lambda i: (0, 0))],
        out_specs=[pl.BlockSpec((tm, D), lambda i: (i, 0)), pl.BlockSpec((tm, 1), lambda i: (i, 0))],
        out_shape=[jax.ShapeDtypeStruct((L, D), bf16), jax.ShapeDtypeStruct((L, 1), f32)],
        name=name, compiler_params=_cparams(("parallel",)))(x, g)


def _rms_bwd(name, dxn, h, r, g, dres, after=()):
    L, D = h.shape
    tm = _tile(L, 256, 16)

    def body(dxn_ref, h_ref, r_ref, g_ref, dres_ref, *rest):
        dh_ref, dhb_ref, dg_ref = rest[len(after):]
        i = pl.program_id(0)
        d = dxn_ref[...].astype(f32)
        hv = h_ref[...]
        rv = r_ref[...]
        dyg = d * g_ref[...]
        m = jnp.mean(dyg * hv, axis=-1, keepdims=True)
        dh = dres_ref[...] + rv * dyg - hv * (rv * rv * rv) * m
        dh_ref[...] = dh
        dhb_ref[...] = dh.astype(bf16)

        @pl.when(i == 0)
        def _():
            dg_ref[...] = jnp.zeros_like(dg_ref)

        dg_ref[...] += jnp.sum(d * hv * rv, axis=0, keepdims=True)

    row = lambda i: (i, 0)
    return pl.pallas_call(
        body, grid=(L // tm,),
        in_specs=[pl.BlockSpec((tm, D), row), pl.BlockSpec((tm, D), row), pl.BlockSpec((tm, 1), row),
                  pl.BlockSpec((1, D), lambda i: (0, 0)), pl.BlockSpec((tm, D), row)] + [_ANY] * len(after),
        out_specs=[pl.BlockSpec((tm, D), row), pl.BlockSpec((tm, D), row), pl.BlockSpec((1, D), lambda i: (0, 0))],
        out_shape=[jax.ShapeDtypeStruct((L, D), f32), jax.ShapeDtypeStruct((L, D), bf16), jax.ShapeDtypeStruct((1, D), f32)],
        name=name, compiler_params=_cparams(("arbitrary",)))(dxn, h, r, g, dres, *after)


def _loss_head(name, h2, tgt, g):
    L, D = h2.shape
    tm = _tile(L, 256, 16)

    def body(h_ref, t_ref, g_ref, dh_ref, dhb_ref, dg_ref, loss_ref):
        i = pl.program_id(0)
        hv = h_ref[...]
        gv = g_ref[...]
        r = lax.rsqrt(jnp.mean(hv * hv, axis=-1, keepdims=True) + _EPS)
        err = hv * r * gv - t_ref[...]
        dy = err * (1.0 / D)
        dyg = dy * gv
        m = jnp.mean(dyg * hv, axis=-1, keepdims=True)
        dh = r * dyg - hv * (r * r * r) * m
        dh_ref[...] = dh
        dhb_ref[...] = dh.astype(bf16)

        @pl.when(i == 0)
        def _():
            dg_ref[...] = jnp.zeros_like(dg_ref)
            loss_ref[...] = jnp.zeros_like(loss_ref)

        dg_ref[...] += jnp.sum(dy * hv * r, axis=0, keepdims=True)
        part = jnp.sum(jnp.sum(err * err, axis=-1, keepdims=True), axis=0, keepdims=True) * (0.5 / D)
        loss_ref[...] += jnp.broadcast_to(part, (8, 128))

    row = lambda i: (i, 0)
    return pl.pallas_call(
        body, grid=(L // tm,),
        in_specs=[pl.BlockSpec((tm, D), row), pl.BlockSpec((tm, D), row), pl.BlockSpec((1, D), lambda i: (0, 0))],
        out_specs=[pl.BlockSpec((tm, D), row), pl.BlockSpec((tm, D), row), pl.BlockSpec((1, D), lambda i: (0, 0)),
                   pl.BlockSpec((8, 128), lambda i: (0, 0))],
        out_shape=[jax.ShapeDtypeStruct((L, D), f32), jax.ShapeDtypeStruct((L, D), bf16),
                   jax.ShapeDtypeStruct((1, D), f32), jax.ShapeDtypeStruct((8, 128), f32)],
        name=name, compiler_params=_cparams(("arbitrary",)))(h2, tgt, g)


def _shift_down(tile, halo, k, rows8):
    tm = tile.shape[0]
    r = pltpu.roll(tile, k, axis=0)
    hh = pltpu.roll(halo, k, axis=0)
    top = jnp.where(rows8 < k, hh, r[:8])
    return jnp.concatenate([top, r[8:]], axis=0) if tm > 8 else top


def _shift_up(tile, halo, k, rows8):
    tm = tile.shape[0]
    r = pltpu.roll(tile, tm - k, axis=0)
    hh = pltpu.roll(halo, 8 - k, axis=0)
    bot = jnp.where(rows8 >= 8 - k, hh, r[tm - 8:])
    return jnp.concatenate([r[:tm - 8], bot], axis=0) if tm > 8 else bot


def _conv3(x, halo, w_ref, b_ref, rows8):
    return (w_ref[0:1, :] * _shift_down(x, halo, 2, rows8) + w_ref[1:2, :] * _shift_down(x, halo, 1, rows8)
            + w_ref[2:3, :] * x + b_ref[...])


def _convb_fwd(cfg, proj, w, b):
    L, CW = cfg.L, cfg.CW
    assert cfg.SW == CW
    tm = _tile(L, 512, 16)

    def body(v_ref, vh_ref, gb_ref, gc_ref, gch_ref, w_ref, b_ref, o_ref):
        i = pl.program_id(0)
        rows8 = lax.broadcasted_iota(jnp.int32, (8, CW), 0)
        cv = gc_ref[...].astype(f32) * v_ref[...].astype(f32)
        cvh = gch_ref[...].astype(f32)[8:] * vh_ref[...].astype(f32)[8:]
        cvh = jnp.where(i == 0, 0.0, cvh)
        cc = _conv3(cv, cvh, w_ref, b_ref, rows8)
        o_ref[...] = (gb_ref[...].astype(f32) * cc).astype(bf16)

    blk = lambda col: pl.BlockSpec((tm, CW), lambda i: (i, col))
    halo = lambda col: pl.BlockSpec((16, CW), lambda i: (jnp.maximum(i * (tm // 16) - 1, 0), col))
    return pl.pallas_call(
        body, grid=(L // tm,),
        in_specs=[blk(1), halo(1), blk(2), blk(3), halo(3),
                  pl.BlockSpec((3, CW), lambda i: (0, 0)), pl.BlockSpec((1, CW), lambda i: (0, 0))],
        out_specs=pl.BlockSpec((tm, CW), lambda i: (i, 0)),
        out_shape=jax.ShapeDtypeStruct((L, CW), bf16),
        name="convb_fwd", compiler_params=_cparams(("parallel",)))(proj, proj, proj, proj, proj, w, b)


def _convb_bwd(cfg, proj, dyb0, w, b, dproj):
    L, CW = cfg.L, cfg.CW
    tm = _tile(L, 512, 16)
    nt = L // tm

    def body(v_ref, vh_ref, gb_ref, gbn_ref, gc_ref, gch_ref, d_ref, dn_ref, w_ref, b_ref, dproj_ref,
             o_ref, dw_ref, db_ref):
        i = pl.program_id(0)
        rows8 = lax.broadcasted_iota(jnp.int32, (8, CW), 0)
        v = v_ref[...].astype(f32)
        gb = gb_ref[...].astype(f32)
        gc = gc_ref[...].astype(f32)
        d = d_ref[...].astype(f32)
        cv = gc * v
        cvh = gch_ref[...].astype(f32)[8:] * vh_ref[...].astype(f32)[8:]
        cvh = jnp.where(i == 0, 0.0, cvh)
        s2 = _shift_down(cv, cvh, 2, rows8)
        s1 = _shift_down(cv, cvh, 1, rows8)
        cc = w_ref[0:1, :] * s2 + w_ref[1:2, :] * s1 + w_ref[2:3, :] * cv + b_ref[...]
        dcc = d * gb
        dccn = dn_ref[...].astype(f32)[:8] * gbn_ref[...].astype(f32)[:8]
        dccn = jnp.where(i == nt - 1, 0.0, dccn)
        dcv = (w_ref[2:3, :] * dcc + w_ref[1:2, :] * _shift_up(dcc, dccn, 1, rows8)
               + w_ref[0:1, :] * _shift_up(dcc, dccn, 2, rows8))
        o_ref[:, 0:CW] = (dcv * gc).astype(bf16)
        o_ref[:, CW:2 * CW] = (d * cc).astype(bf16)
        o_ref[:, 2 * CW:3 * CW] = (dcv * v).astype(bf16)

        @pl.when(i == 0)
        def _():
            dw_ref[...] = jnp.zeros_like(dw_ref)
            db_ref[...] = jnp.zeros_like(db_ref)

        dw_ref[0:1, :] += jnp.sum(dcc * s2, axis=0, keepdims=True)
        dw_ref[1:2, :] += jnp.sum(dcc * s1, axis=0, keepdims=True)
        dw_ref[2:3, :] += jnp.sum(dcc * cv, axis=0, keepdims=True)
        db_ref[...] += jnp.sum(dcc, axis=0, keepdims=True)

    blk = lambda col: pl.BlockSpec((tm, CW), lambda i: (i, col))
    prev = lambda col: pl.BlockSpec((16, CW), lambda i: (jnp.maximum(i * (tm // 16) - 1, 0), col))
    nxt = lambda col: pl.BlockSpec((16, CW), lambda i: (jnp.minimum((i + 1) * (tm // 16), L // 16 - 1), col))
    const = lambda r: pl.BlockSpec((r, CW), lambda i: (0, 0))
    return pl.pallas_call(
        body, grid=(nt,),
        in_specs=[blk(1), prev(1), blk(2), nxt(2), blk(3), prev(3), blk(0), nxt(0), const(3), const(1),
                  pl.BlockSpec(memory_space=pl.ANY)],
        out_specs=[pl.BlockSpec((tm, 3 * CW), lambda i: (i, 0)), const(3), const(1)],
        out_shape=[jax.ShapeDtypeStruct(dproj.shape, bf16), jax.ShapeDtypeStruct((3, CW), f32),
                   jax.ShapeDtypeStruct((1, CW), f32)],
        input_output_aliases={10: 0},
        name="convb_bwd", compiler_params=_cparams(("arbitrary",)))(proj, proj, proj, proj, proj, proj, dyb0, dyb0, w, b,
                                                                    dproj)


def _ffn_act(cfg, hh, w, b):
    L, F = cfg.L, cfg.F
    tm = _tile(L, 512, 16)
    tc = _tile(F, 1408, 128)
    ncb = F // tc

    def body(a_ref, ah_ref, g_ref, w_ref, b_ref, o_ref):
        i = pl.program_id(0)
        rows8 = lax.broadcasted_iota(jnp.int32, (8, tc), 0)
        a = a_ref[...].astype(f32)
        ah = jnp.where(i == 0, 0.0, ah_ref[...].astype(f32)[8:])
        o_ref[...] = (_gelu(_conv3(a, ah, w_ref, b_ref, rows8)) * g_ref[...].astype(f32)).astype(bf16)

    return pl.pallas_call(
        body, grid=(L // tm, ncb),
        in_specs=[pl.BlockSpec((tm, tc), lambda i, j: (i, j)),
                  pl.BlockSpec((16, tc), lambda i, j: (jnp.maximum(i * (tm // 16) - 1, 0), j)),
                  pl.BlockSpec((tm, tc), lambda i, j: (i, j + ncb)),
                  pl.BlockSpec((3, tc), lambda i, j: (0, j)), pl.BlockSpec((1, tc), lambda i, j: (0, j))],
        out_specs=pl.BlockSpec((tm, tc), lambda i, j: (i, j)),
        out_shape=jax.ShapeDtypeStruct((L, F), bf16),
        name="ffn_act", compiler_params=_cparams(("parallel", "parallel")))(hh, hh, hh, w, b)


def _ffn_act_bwd(cfg, hh, df, w, b):
    L, F = cfg.L, cfg.F
    tm = _tile(L, 512, 16)
    tc = _tile(F, 1408, 128)
    ncb = F // tc
    nt = L // tm

    def body(a_ref, ah_ref, an_ref, g_ref, gn_ref, d_ref, dn_ref, w_ref, b_ref, dhh_ref, dw_ref, db_ref):
        i = pl.program_id(1)
        rows8 = lax.broadcasted_iota(jnp.int32, (8, tc), 0)
        a = a_ref[...].astype(f32)
        ah = jnp.where(i == 0, 0.0, ah_ref[...].astype(f32)[8:])
        s2 = _shift_down(a, ah, 2, rows8)
        s1 = _shift_down(a, ah, 1, rows8)
        act = w_ref[0:1, :] * s2 + w_ref[1:2, :] * s1 + w_ref[2:3, :] * a + b_ref[...]
        d = d_ref[...].astype(f32)
        gelu, gelu_d = _gelu_and_grad(act)
        dhh_ref[1] = (d * gelu).astype(bf16)
        dact = d * g_ref[...].astype(f32) * gelu_d
        an = an_ref[...].astype(f32)[:8]
        actn = _conv3(an, a[tm - 8:], w_ref, b_ref, rows8)
        dactn = dn_ref[...].astype(f32)[:8] * gn_ref[...].astype(f32)[:8] * _gelu_grad(actn)
        dactn = jnp.where(i == nt - 1, 0.0, dactn)
        da = (w_ref[2:3, :] * dact + w_ref[1:2, :] * _shift_up(dact, dactn, 1, rows8)
              + w_ref[0:1, :] * _shift_up(dact, dactn, 2, rows8))
        dhh_ref[0] = da.astype(bf16)

        @pl.when(i == 0)
        def _():
            dw_ref[...] = jnp.zeros_like(dw_ref)
            db_ref[...] = jnp.zeros_like(db_ref)

        dw_ref[0:1, :] += jnp.sum(dact * s2, axis=0, keepdims=True)
        dw_ref[1:2, :] += jnp.sum(dact * s1, axis=0, keepdims=True)
        dw_ref[2:3, :] += jnp.sum(dact * a, axis=0, keepdims=True)
        db_ref[...] += jnp.sum(dact, axis=0, keepdims=True)

    blk = lambda off: pl.BlockSpec((tm, tc), lambda j, i: (i, j + off))
    prev = lambda off: pl.BlockSpec((16, tc), lambda j, i: (jnp.maximum(i * (tm // 16) - 1, 0), j + off))
    nxt = lambda off: pl.BlockSpec((16, tc), lambda j, i: (jnp.minimum((i + 1) * (tm // 16), L // 16 - 1), j + off))
    const = lambda r: pl.BlockSpec((r, tc), lambda j, i: (0, j))
    return pl.pallas_call(
        body, grid=(ncb, nt),
        in_specs=[blk(0), prev(0), nxt(0), blk(ncb), nxt(ncb), blk(0), nxt(0), const(3), const(1)],
        out_specs=[pl.BlockSpec((2, tm, tc), lambda j, i: (0, i, j)), const(3), const(1)],
        out_shape=[jax.ShapeDtypeStruct((2, L, F), bf16),
                   jax.ShapeDtypeStruct((3, F), f32), jax.ShapeDtypeStruct((1, F), f32)],
        name="ffn_act_bwd", compiler_params=_cparams(("parallel", "arbitrary")))(hh, hh, hh, hh, hh, df, df, w, b)


def _merge_fwd(cfg, ya1, yb0, proj, wso, wco):
    L, D, SW, CW = cfg.L, cfg.D, cfg.SW, cfg.CW
    Ns = D // _NCHIP
    tm = _tile(L, 1024, 16)
    tn = _tile(Ns, 512, 128)
    nb = Ns // tn
    off_a = (SW + 3 * CW) // tn
    off_b = (SW + 3 * CW + D) // tn

    def body(a_ref, b_ref, wa_ref, wb_ref, ma_ref, mb_ref, m_ref, ya_ref, yb_ref):
        ya = jnp.dot(a_ref[...], wa_ref[...], preferred_element_type=f32)
        yb = jnp.dot(b_ref[...], wb_ref[...], preferred_element_type=f32)
        sa = jax.nn.sigmoid(ma_ref[...].astype(f32))
        sb = jax.nn.sigmoid(mb_ref[...].astype(f32))
        m_ref[...] = (sa * ya + sb * yb).astype(bf16)
        ya_ref[...] = ya.astype(bf16)
        yb_ref[...] = yb.astype(bf16)

    o_spec = pl.BlockSpec((tm, tn), lambda i, j: (i, j))
    o_shape = jax.ShapeDtypeStruct((L, D), bf16)
    return pl.pallas_call(
        body, grid=(L // tm, D // tn),
        in_specs=[pl.BlockSpec((tm, SW), lambda i, j: (i, 0)), pl.BlockSpec((tm, CW), lambda i, j: (i, 0)),
                  pl.BlockSpec((None, SW, tn), lambda i, j: (j // nb, 0, j % nb)),
                  pl.BlockSpec((None, CW, tn), lambda i, j: (j // nb, 0, j % nb)),
                  pl.BlockSpec((tm, tn), lambda i, j: (i, off_a + j)), pl.BlockSpec((tm, tn), lambda i, j: (i, off_b + j))],
        out_specs=[o_spec, o_spec, o_spec], out_shape=[o_shape, o_shape, o_shape],
        name="merge_fwd", compiler_params=_cparams(("parallel", "parallel")))(ya1, yb0, wso, wco, proj, proj)


def _s5_dims(cfg):
    G = cfg.SW // _SSM_GROUP
    NS = G // _SLAB_GROUPS
    SC = _SLAB_GROUPS * _SSM_GROUP
    SH = _SLAB_GROUPS * _SSM_STATE
    NST = 2 * SH * NS
    return G, NS, SC, SH, NST


def _lane_slabs(cfg, W):
    _, NS, _, SH, _ = _s5_dims(cfg)
    return [(2 * SH * s + w0, 2 * SH * s + SH + w0) for s in range(NS) for w0 in range(0, SH, W)]


def _discretize(a_re, a_im, log_dt, b_re, b_im):
    dt = jnp.exp(log_dt)[:, None]
    mag = jnp.exp(dt * a_re)
    abr = mag * jnp.cos(dt * a_im)
    abi = mag * jnp.sin(dt * a_im)
    nr = abr - 1.0
    ni = abi
    den = a_re * a_re + a_im * a_im
    fr = (nr * a_re + ni * a_im) / den
    fi = (ni * a_re - nr * a_im) / den
    bbr = fr[..., None] * b_re - fi[..., None] * b_im
    bbi = fr[..., None] * b_im + fi[..., None] * b_re
    return abr, abi, bbr, bbi


def _state_rows(cfg, re, im):
    _, NS, _, SH, _ = _s5_dims(cfg)
    return jnp.concatenate([re.reshape(NS, SH), im.reshape(NS, SH)], axis=1).reshape(-1)


def _s5_tables(cfg, abr, abi, bbr, bbi, c_re, c_im):
    G, NS, SC, SH, NST = _s5_dims(cfg)
    S = cfg.T // 8
    eye = jnp.eye(_SLAB_GROUPS, dtype=f32)
    bb = jnp.stack([bbr, bbi]).reshape(2, NS, _SLAB_GROUPS, _SSM_STATE, _SSM_GROUP)
    bs = jnp.einsum("rsgph,gq->sghrqp", bb, eye).reshape(NS, SC, 2 * SH).astype(bf16)
    cc = jnp.stack([c_re, -c_im]).reshape(2, NS, _SLAB_GROUPS, _SSM_GROUP, _SSM_STATE)
    cs = jnp.einsum("rsghp,gq->srqpgh", cc, eye).reshape(NS, 2 * SH, SC).astype(bf16)
    arep = jnp.broadcast_to(_state_rows(cfg, abr, abi)[None, :], (8, NST))
    pr, pi = abr, abi
    for _ in range(S - 1):
        pr, pi = pr * abr - pi * abi, pr * abi + pi * abr
    apow = jnp.broadcast_to(_state_rows(cfg, pr, pi)[None, :], (8, NST))
    t = jnp.arange(cfg.T)
    perm = (t % 8) * S + t // 8
    pm = jax.nn.one_hot(perm, cfg.T, dtype=bf16)
    return bs, cs, arep, apow, pm, pm.T


def _cmul_add(ar, ai, xr, xi, br, bi):
    return ar * xr - ai * xi + br, ar * xi + ai * xr + bi


def _s5_forward_chunk(cfg, W, upb, bs_ref, arep_ref, apow_ref, st, x0, cin_store):
    _, NS, SC, SH, _ = _s5_dims(cfg)
    S = cfg.T // 8
    for s in range(NS):
        st[:, 2 * SH * s:2 * SH * (s + 1)] = jnp.dot(upb[:, SC * s:SC * (s + 1)], bs_ref[s], preferred_element_type=f32)
    rows = lax.broadcasted_iota(jnp.int32, (8, W), 0)
    zero = jnp.zeros((8, W), f32)
    for rc, ic in _lane_slabs(cfg, W):
        ar = arep_ref[:, rc:rc + W]
        ai = arep_ref[:, ic:ic + W]

        def step(i, carry, rc=rc, ic=ic, ar=ar, ai=ai):
            xr, xi = carry
            r0 = pl.multiple_of(i * 8, 8)
            nr, ni = _cmul_add(ar, ai, xr, xi, st[pl.ds(r0, 8), rc:rc + W], st[pl.ds(r0, 8), ic:ic + W])
            st[pl.ds(r0, 8), rc:rc + W] = nr
            st[pl.ds(r0, 8), ic:ic + W] = ni
            return nr, ni

        er, ei = lax.fori_loop(0, S, step, (zero, zero))
        pr = apow_ref[:, rc:rc + W]
        pi = apow_ref[:, ic:ic + W]
        x0r = x0[:, rc:rc + W]
        x0i = x0[:, ic:ic + W]
        cr = jnp.where(rows == 0, x0r, 0.0)
        ci = jnp.where(rows == 0, x0i, 0.0)
        for _ in range(7):
            fr, fi = _cmul_add(pr, pi, cr, ci, er, ei)
            cr = jnp.where(rows == 0, x0r, pltpu.roll(fr, 1, axis=0))
            ci = jnp.where(rows == 0, x0i, pltpu.roll(fi, 1, axis=0))
        fr, fi = _cmul_add(pr, pi, cr, ci, er, ei)
        x0[:, rc:rc + W] = jnp.broadcast_to(fr[7:8, :], (8, W))
        x0[:, ic:ic + W] = jnp.broadcast_to(fi[7:8, :], (8, W))
        cin_store(rc, ic, cr, ci)

        def fix(i, carry, rc=rc, ic=ic, ar=ar, ai=ai):
            kr, ki = carry
            r0 = pl.multiple_of(i * 8, 8)
            nr, ni = ar * kr - ai * ki, ar * ki + ai * kr
            st[pl.ds(r0, 8), rc:rc + W] = st[pl.ds(r0, 8), rc:rc + W] + nr
            st[pl.ds(r0, 8), ic:ic + W] = st[pl.ds(r0, 8), ic:ic + W] + ni
            return nr, ni

        lax.fori_loop(0, S, fix, (cr, ci))


def _s5_fwd(cfg, proj, tabs, dskip):
    L, SW, T = cfg.L, cfg.SW, cfg.T
    G, NS, SC, SH, NST = _s5_dims(cfg)
    bs, cs, arep, apow, pm, pmt = tabs
    W = min(512, SH)
    NC = L // T

    def body(u_ref, pm_ref, pmt_ref, bs_ref, cs_ref, arep_ref, apow_ref, dskip_ref, y_ref, ya0_ref, cin_ref, st, x0):
        c = pl.program_id(0)

        @pl.when(c == 0)
        def _():
            x0[...] = jnp.zeros_like(x0)

        up = jnp.dot(pm_ref[...], u_ref[...], preferred_element_type=f32)
        upb = up.astype(bf16)

        def cin_store(rc, ic, cr, ci):
            cin_ref[0, :, rc:rc + W] = cr
            cin_ref[0, :, ic:ic + W] = ci

        _s5_forward_chunk(cfg, W, upb, bs_ref, arep_ref, apow_ref, st, x0, cin_store)
        yp = jnp.concatenate(
            [jnp.dot(st[:, 2 * SH * s:2 * SH * (s + 1)].astype(bf16), cs_ref[s], preferred_element_type=f32)
             for s in range(NS)], axis=1) + dskip_ref[...] * up
        y = jnp.dot(pmt_ref[...], yp.astype(bf16), preferred_element_type=f32)
        y_ref[...] = y.astype(bf16)
        ya0_ref[...] = _gelu(y).astype(bf16)

    const2 = lambda shape: pl.BlockSpec(shape, lambda c: (0, 0))
    const3 = lambda shape: pl.BlockSpec(shape, lambda c: (0, 0, 0))
    return pl.pallas_call(
        body, grid=(NC,),
        in_specs=[pl.BlockSpec((T, SW), lambda c: (c, 0)), const2((T, T)), const2((T, T)), const3((NS, SC, 2 * SH)),
                  const3((NS, 2 * SH, SC)), const2((8, NST)), const2((8, NST)), const2((1, SW))],
        out_specs=[pl.BlockSpec((T, SW), lambda c: (c, 0)), pl.BlockSpec((T, SW), lambda c: (c, 0)),
                   pl.BlockSpec((1, 8, NST), lambda c: (c, 0, 0))],
        out_shape=[jax.ShapeDtypeStruct((L, SW), bf16), jax.ShapeDtypeStruct((L, SW), bf16),
                   jax.ShapeDtypeStruct((NC, 8, NST), f32)],
        scratch_shapes=[pltpu.VMEM((T, NST), f32), pltpu.VMEM((8, NST), f32)],
        name="s5_fwd", compiler_params=_cparams(("arbitrary",)))(proj, pm, pmt, bs, cs, arep, apow, dskip)


def _s5_bwd(cfg, proj, dy, cin, tabs, dskip, dproj, after=()):
    L, SW, T = cfg.L, cfg.SW, cfg.T
    du_col = 3 * cfg.CW // SW
    G, NS, SC, SH, NST = _s5_dims(cfg)
    bs, cs, arep, apow, pm, pmt = tabs
    W = min(512, SH)
    S = T // 8
    NC = L // T

    def body(u_ref, dy_ref, cin_ref, pm_ref, pmt_ref, bs_ref, cs_ref, arep_ref, apow_ref, dskip_ref, dproj_ref, *rest):
        du_ref, da_ref, db_ref, dc_ref, dd_ref, st, gs, x0, g0, db_acc, dc_acc = rest[len(after):]
        c = pl.program_id(0)

        @pl.when(c == 0)
        def _():
            g0[...] = jnp.zeros_like(g0)
            da_ref[...] = jnp.zeros_like(da_ref)
            dd_ref[...] = jnp.zeros_like(dd_ref)
            db_acc[...] = jnp.zeros_like(db_acc)
            dc_acc[...] = jnp.zeros_like(dc_acc)

        up = jnp.dot(pm_ref[...], u_ref[...], preferred_element_type=f32)
        upb = up.astype(bf16)
        dyp = jnp.dot(pm_ref[...], dy_ref[...], preferred_element_type=f32)
        dypb = dyp.astype(bf16)
        x0[...] = jnp.zeros_like(x0)
        for rc, ic in _lane_slabs(cfg, W):
            x0[:, rc:rc + W] = jnp.broadcast_to(cin_ref[0, 0:1, rc:rc + W], (8, W))
            x0[:, ic:ic + W] = jnp.broadcast_to(cin_ref[0, 0:1, ic:ic + W], (8, W))
        _s5_forward_chunk(cfg, W, upb, bs_ref, arep_ref, apow_ref, st, x0, lambda *a: None)
        for s in range(NS):
            gs[:, 2 * SH * s:2 * SH * (s + 1)] = lax.dot_general(
                dypb[:, SC * s:SC * (s + 1)], cs_ref[s], (((1,), (1,)), ((), ())), preferred_element_type=f32)
        rows = lax.broadcasted_iota(jnp.int32, (8, W), 0)
        zero = jnp.zeros((8, W), f32)
        for rc, ic in _lane_slabs(cfg, W):
            ar = arep_ref[:, rc:rc + W]
            ai = arep_ref[:, ic:ic + W]

            def rstep(k, carry, rc=rc, ic=ic, ar=ar, ai=ai):
                gr, gi = carry
                r0 = pl.multiple_of((S - 1 - k) * 8, 8)
                nr = ar * gr + ai * gi + gs[pl.ds(r0, 8), rc:rc + W]
                ni = ar * gi - ai * gr + gs[pl.ds(r0, 8), ic:ic + W]
                gs[pl.ds(r0, 8), rc:rc + W] = nr
                gs[pl.ds(r0, 8), ic:ic + W] = ni
                return nr, ni

            er, ei = lax.fori_loop(0, S, rstep, (zero, zero))
            pr = apow_ref[:, rc:rc + W]
            pi = apow_ref[:, ic:ic + W]
            g0r = g0[:, rc:rc + W]
            g0i = g0[:, ic:ic + W]
            cr = jnp.where(rows == 7, g0r, 0.0)
            ci = jnp.where(rows == 7, g0i, 0.0)
            for _ in range(7):
                fr = er + pr * cr + pi * ci
                fi = ei + pr * ci - pi * cr
                cr = jnp.where(rows == 7, g0r, pltpu.roll(fr, 7, axis=0))
                ci = jnp.where(rows == 7, g0i, pltpu.roll(fi, 7, axis=0))
            fr = er + pr * cr + pi * ci
            fi = ei + pr * ci - pi * cr
            g0[:, rc:rc + W] = jnp.broadcast_to(fr[0:1, :], (8, W))
            g0[:, ic:ic + W] = jnp.broadcast_to(fi[0:1, :], (8, W))

            def fix(k, carry, rc=rc, ic=ic, ar=ar, ai=ai):
                kr, ki, accr, acci = carry
                i = S - 1 - k
                r0 = pl.multiple_of(i * 8, 8)
                rp = pl.multiple_of((i - 1) * 8, 8)
                nr = ar * kr + ai * ki
                ni = ar * ki - ai * kr
                gr = gs[pl.ds(r0, 8), rc:rc + W] + nr
                gi = gs[pl.ds(r0, 8), ic:ic + W] + ni
                gs[pl.ds(r0, 8), rc:rc + W] = gr
                gs[pl.ds(r0, 8), ic:ic + W] = gi
                xr = st[pl.ds(rp, 8), rc:rc + W]
                xi = st[pl.ds(rp, 8), ic:ic + W]
                return nr, ni, accr + gr * xr + gi * xi, acci + gi * xr - gr * xi

            kr, ki, accr, acci = lax.fori_loop(0, S - 1, fix, (cr, ci, zero, zero))
            nr = ar * kr + ai * ki
            ni = ar * ki - ai * kr
            gr = gs[0:8, rc:rc + W] + nr
            gi = gs[0:8, ic:ic + W] + ni
            gs[0:8, rc:rc + W] = gr
            gs[0:8, ic:ic + W] = gi
            xr = cin_ref[0, :, rc:rc + W]
            xi = cin_ref[0, :, ic:ic + W]
            da_ref[:, rc:rc + W] += accr + gr * xr + gi * xi
            da_ref[:, ic:ic + W] += acci + gi * xr - gr * xi

        dups = []
        for s in range(NS):
            gsb = gs[:, 2 * SH * s:2 * SH * (s + 1)].astype(bf16)
            dups.append(lax.dot_general(gsb, bs_ref[s], (((1,), (1,)), ((), ())), preferred_element_type=f32))
            db_acc[s] += lax.dot_general(upb[:, SC * s:SC * (s + 1)], gsb, (((0,), (0,)), ((), ())),
                                         preferred_element_type=f32)
            dc_acc[s] += lax.dot_general(st[:, 2 * SH * s:2 * SH * (s + 1)].astype(bf16), dypb[:, SC * s:SC * (s + 1)],
                                         (((0,), (0,)), ((), ())), preferred_element_type=f32)
        dup = jnp.concatenate(dups, axis=1) + dskip_ref[...] * dyp
        du_ref[...] = jnp.dot(pmt_ref[...], dup.astype(bf16), preferred_element_type=f32).astype(bf16)
        dd_ref[...] += jnp.sum(dyp * up, axis=0, keepdims=True)

        @pl.when(c == NC - 1)
        def _():
            PS, GH = _SSM_STATE, _SSM_GROUP
            mask_b = (lax.broadcasted_iota(jnp.int32, (SC, SH), 0) // GH
                      == lax.broadcasted_iota(jnp.int32, (SC, SH), 1) // PS)
            mask_c = (lax.broadcasted_iota(jnp.int32, (SH, SC), 0) // PS
                      == lax.broadcasted_iota(jnp.int32, (SH, SC), 1) // GH)
            for s in range(NS):
                for r in range(2):
                    xb = jnp.where(mask_b, db_acc[s, :, r * SH:(r + 1) * SH], 0.0)
                    zb = xb[:, 0:128]
                    for q in range(1, SH // 128):
                        zb = zb + xb[:, q * 128:(q + 1) * 128]
                    db_ref[s, r] = zb + pltpu.roll(zb, PS, axis=1)
                    xc = jnp.where(mask_c, dc_acc[s, r * SH:(r + 1) * SH, :], 0.0)
                    zc = xc[0:PS]
                    for q in range(1, SH // PS):
                        zc = zc + xc[q * PS:(q + 1) * PS]
                    dc_ref[s, r] = zc

    rev = lambda c: (NC - 1 - c, 0)
    const2 = lambda shape: pl.BlockSpec(shape, lambda c: (0, 0))
    const3 = lambda shape: pl.BlockSpec(shape, lambda c: (0, 0, 0))
    const4 = lambda shape: pl.BlockSpec(shape, lambda c: (0, 0, 0, 0))
    return pl.pallas_call(
        body, grid=(NC,),
        in_specs=[pl.BlockSpec((T, SW), rev), pl.BlockSpec((T, SW), rev), pl.BlockSpec((1, 8, NST), lambda c: (NC - 1 - c, 0, 0)),
                  const2((T, T)), const2((T, T)), const3((NS, SC, 2 * SH)), const3((NS, 2 * SH, SC)),
                  const2((8, NST)), const2((8, NST)), const2((1, SW)), _ANY] + [_ANY] * len(after),
        out_specs=[pl.BlockSpec((T, SW), lambda c: (NC - 1 - c, du_col)), const2((8, NST)),
                   const4((NS, 2, SC, 128)), const4((NS, 2, _SSM_STATE, SC)), const2((1, SW))],
        out_shape=[jax.ShapeDtypeStruct(dproj.shape, bf16), jax.ShapeDtypeStruct((8, NST), f32),
                   jax.ShapeDtypeStruct((NS, 2, SC, 128), f32), jax.ShapeDtypeStruct((NS, 2, _SSM_STATE, SC), f32),
                   jax.ShapeDtypeStruct((1, SW), f32)],
        scratch_shapes=[pltpu.VMEM((T, NST), f32), pltpu.VMEM((T, NST), f32), pltpu.VMEM((8, NST), f32),
                        pltpu.VMEM((8, NST), f32), pltpu.VMEM((NS, SC, 2 * SH), f32), pltpu.VMEM((NS, 2 * SH, SC), f32)],
        input_output_aliases={10: 0},
        name="s5_bwd", compiler_params=_cparams(("arbitrary",)))(proj, dy, cin, pm, pmt, bs, cs, arep, apow, dskip, dproj, *after)


def _s5_param_grads(cfg, da, db_diag, dc_diag):
    G, NS, SC, SH, NST = _s5_dims(cfg)
    das = da.sum(axis=0).reshape(NS, 2, SH)
    dabr = das[:, 0].reshape(G, _SSM_STATE)
    dabi = das[:, 1].reshape(G, _SSM_STATE)
    dbd = db_diag[..., :_SSM_STATE].reshape(NS, 2, _SLAB_GROUPS, _SSM_GROUP, _SSM_STATE)
    dbb = dbd.transpose(1, 0, 2, 4, 3).reshape(2, G, _SSM_STATE, _SSM_GROUP)
    dcd = dc_diag.reshape(NS, 2, _SSM_STATE, _SLAB_GROUPS, _SSM_GROUP)
    dcc = dcd.transpose(1, 0, 3, 4, 2).reshape(2, G, _SSM_GROUP, _SSM_STATE)
    return dabr, dabi, dbb[0], dbb[1], dcc[0], -dcc[1]


def _coords():
    return lax.axis_index("x"), lax.axis_index("y"), lax.axis_index("c")


def _other_chips(x, y):
    return [(1 - x, y), (x, 1 - y), (1 - x, 1 - y)]


def _allreduce8(name, v):
    R = v.shape[0]

    def body(v_ref, o_ref, sib, chips, mine, ssem, rsem):
        x, y, c = _coords()
        d2d = pltpu.make_async_remote_copy(src_ref=v_ref, dst_ref=sib, send_sem=ssem.at[0], recv_sem=rsem.at[0],
                                           device_id=(x, y, 1 - c), device_id_type=_MESH)
        d2d.start()
        d2d.wait()
        mine[...] = v_ref[...] + sib[...]
        cps = [pltpu.make_async_remote_copy(src_ref=mine, dst_ref=chips.at[j], send_sem=ssem.at[1 + j],
                                            recv_sem=rsem.at[1 + j], device_id=(*chip, c), device_id_type=_MESH)
               for j, chip in enumerate(_other_chips(x, y))]
        for cp in cps:
            cp.start()
        for cp in cps:
            cp.wait()
        o_ref[...] = (mine[...] + chips[1]) + (chips[0] + chips[2])

    vm = pl.BlockSpec(memory_space=pltpu.VMEM)
    return pl.pallas_call(
        body, in_specs=[vm], out_specs=vm, out_shape=jax.ShapeDtypeStruct((R, 128), f32),
        scratch_shapes=[pltpu.VMEM((R, 128), f32), pltpu.VMEM((3, R, 128), f32), pltpu.VMEM((R, 128), f32),
                        pltpu.SemaphoreType.DMA((4,)), pltpu.SemaphoreType.DMA((4,))],
        name=name, compiler_params=pltpu.CompilerParams(vmem_limit_bytes=_VMEM_LIMIT))(v)


def _cast_into_slot(name, w, k_idx):
    R, C = w.shape
    tr = _tile(R, 256, 16)

    def body(k_ref, w_ref, o_ref):
        o_ref[...] = w_ref[...].astype(bf16)

    gs = pltpu.PrefetchScalarGridSpec(
        num_scalar_prefetch=1, grid=(R // tr,),
        in_specs=[pl.BlockSpec((tr, C), lambda r, kr: (r, 0))],
        out_specs=pl.BlockSpec((None, tr, C), lambda r, kr: (kr[0], r, 0)))
    return pl.pallas_call(body, grid_spec=gs, out_shape=jax.ShapeDtypeStruct((_NCHIP, R, C), bf16), name=name,
                          compiler_params=_cparams(("parallel",)))(k_idx, w)


def _handshake(peers):
    barrier = pltpu.get_barrier_semaphore()
    for peer in peers:
        pl.semaphore_signal(barrier, inc=1, device_id=peer, device_id_type=_MESH)
    pl.semaphore_wait(barrier, len(peers))


def _allgather_weights(name, bufs, collective_id):
    n = len(bufs)
    refs = [jax.new_ref(b, memory_space=pltpu.MemorySpace.HBM) for b in bufs]

    @pl.kernel(mesh=plsc.ScalarSubcoreMesh(axis_name="seq", num_cores=1), name=name,
               scratch_types=(pltpu.SemaphoreType.DMA((n, 3)), pltpu.SemaphoreType.DMA((n, 3)),
                              pltpu.SemaphoreType.DMA((n, 3)), pltpu.SemaphoreType.DMA((n, 3))),
               compiler_params=pltpu.CompilerParams(collective_id=collective_id))
    def launch(ssem, rsem, fssem, frsem):
        x, y, c = _coords()
        k = 2 * x + y
        others = _other_chips(x, y)
        _handshake([(x, y, 1 - c)] + [(*chip, c) for chip in others])
        sends, fwds = [], []
        for w in range(n):
            rh = refs[w].shape[1] // 2
            mine = refs[w].at[k, pl.ds(c * rh, rh)]
            for j, chip in enumerate(others):
                cp = pltpu.make_async_remote_copy(
                    src_ref=mine, dst_ref=mine, send_sem=ssem.at[w, j], recv_sem=rsem.at[w, j],
                    device_id=(*chip, c), device_id_type=_MESH)
                cp.start()
                sends.append(cp)
        for w in range(n):
            rh = refs[w].shape[1] // 2
            for j, (ox, oy) in enumerate(others):
                landed = refs[w].at[2 * ox + oy, pl.ds(c * rh, rh)]
                pltpu.make_async_remote_copy(
                    src_ref=landed, dst_ref=landed, send_sem=ssem.at[w, j], recv_sem=rsem.at[w, j],
                    device_id=(ox, oy, c), device_id_type=_MESH).wait_recv()
                cp = pltpu.make_async_remote_copy(
                    src_ref=landed, dst_ref=landed, send_sem=fssem.at[w, j], recv_sem=frsem.at[w, j],
                    device_id=(x, y, 1 - c), device_id_type=_MESH)
                cp.start()
                fwds.append(cp)
        for w in range(n):
            rh = refs[w].shape[1] // 2
            for j, (ox, oy) in enumerate(others):
                passed = refs[w].at[2 * ox + oy, pl.ds((1 - c) * rh, rh)]
                pltpu.make_async_remote_copy(
                    src_ref=passed, dst_ref=passed, send_sem=fssem.at[w, j], recv_sem=frsem.at[w, j],
                    device_id=(x, y, 1 - c), device_id_type=_MESH).wait_recv()
        for cp in sends + fwds:
            cp.wait_send()

    launch()
    return [r[...] for r in refs]


def _sequencer_kernel(name, collective_id, sems, body):
    pl.kernel(body, mesh=plsc.ScalarSubcoreMesh(axis_name="seq", num_cores=1), name=name, scratch_types=sems,
              compiler_params=pltpu.CompilerParams(collective_id=collective_id))()


def _hbm_ref(a):
    return jax.new_ref(a, memory_space=pltpu.MemorySpace.HBM)


def _exchange_halves(name, grads, collective_id):
    n = len(grads)
    srcs = [_hbm_ref(g) for g in grads]
    dsts = [jax.empty_ref(jax.ShapeDtypeStruct((g.shape[0], g.shape[1] // 2, g.shape[2]), g.dtype),
                          memory_space=pltpu.MemorySpace.HBM) for g in grads]

    def body(ssem, rsem):
        x, y, c = _coords()
        _handshake([(x, y, 1 - c)])
        cps = []
        for w in range(n):
            rh = srcs[w].shape[1] // 2
            cp = pltpu.make_async_remote_copy(
                src_ref=srcs[w].at[:, pl.ds((1 - c) * rh, rh)], dst_ref=dsts[w], send_sem=ssem.at[w], recv_sem=rsem.at[w],
                device_id=(x, y, 1 - c), device_id_type=_MESH)
            cp.start()
            cps.append(cp)
        for cp in cps:
            cp.wait()

    _sequencer_kernel(name, collective_id, (pltpu.SemaphoreType.DMA((n,)), pltpu.SemaphoreType.DMA((n,))), body)
    return [d[...] for d in dsts]


def _scatter_shards(name, parts, collective_id):
    n = len(parts)
    srcs = [_hbm_ref(p) for p in parts]
    dsts = [jax.empty_ref(jax.ShapeDtypeStruct((3,) + p.shape[1:], p.dtype), memory_space=pltpu.MemorySpace.HBM)
            for p in parts]

    def body(ssem, rsem):
        x, y, c = _coords()
        others = _other_chips(x, y)
        _handshake([(*chip, c) for chip in others])
        cps = []
        for w in range(n):
            for j, (ox, oy) in enumerate(others):
                cp = pltpu.make_async_remote_copy(
                    src_ref=srcs[w].at[2 * ox + oy], dst_ref=dsts[w].at[j], send_sem=ssem.at[w, j], recv_sem=rsem.at[w, j],
                    device_id=(ox, oy, c), device_id_type=_MESH)
                cp.start()
                cps.append(cp)
        for cp in cps:
            cp.wait()

    _sequencer_kernel(name, collective_id, (pltpu.SemaphoreType.DMA((n, 3)), pltpu.SemaphoreType.DMA((n, 3))), body)
    return [d[...] for d in dsts]


def _join_halves(name, bufs, collective_id):
    n = len(bufs)
    refs = [_hbm_ref(b) for b in bufs]

    def body(ssem, rsem):
        x, y, c = _coords()
        _handshake([(x, y, 1 - c)])
        cps = []
        for w in range(n):
            rh = refs[w].shape[0] // 2
            mine = refs[w].at[pl.ds(c * rh, rh)]
            cp = pltpu.make_async_remote_copy(src_ref=mine, dst_ref=mine, send_sem=ssem.at[w], recv_sem=rsem.at[w],
                                              device_id=(x, y, 1 - c), device_id_type=_MESH)
            cp.start()
            cps.append(cp)
        for w in range(n):
            rh = refs[w].shape[0] // 2
            theirs = refs[w].at[pl.ds((1 - c) * rh, rh)]
            pltpu.make_async_remote_copy(src_ref=theirs, dst_ref=theirs, send_sem=ssem.at[w], recv_sem=rsem.at[w],
                                         device_id=(x, y, 1 - c), device_id_type=_MESH).wait_recv()
        for cp in cps:
            cp.wait_send()

    _sequencer_kernel(name, collective_id, (pltpu.SemaphoreType.DMA((n,)), pltpu.SemaphoreType.DMA((n,))), body)
    return [r[...] for r in refs]


def _allgather_small(name, v, collective_id):
    R = v.shape[0]
    src = _hbm_ref(v)
    dst = jax.empty_ref(jax.ShapeDtypeStruct((8, R, 128), v.dtype), memory_space=pltpu.MemorySpace.HBM)
    flips = [(dx, dy, dc) for dx in (0, 1) for dy in (0, 1) for dc in (0, 1)][1:]

    def body(lsem, ssem, rsem):
        x, y, c = _coords()
        me = 4 * x + 2 * y + c
        flip = lambda v, d: 1 - v if d else v
        peers = [(flip(x, dx), flip(y, dy), flip(c, dc)) for dx, dy, dc in flips]
        _handshake(peers)
        own = pltpu.make_async_copy(src, dst.at[me], lsem)
        own.start()
        cps = []
        for r, peer in enumerate(peers):
            cp = pltpu.make_async_remote_copy(src_ref=src, dst_ref=dst.at[me], send_sem=ssem.at[r], recv_sem=rsem.at[r],
                                              device_id=peer, device_id_type=_MESH)
            cp.start()
            cps.append(cp)
        for r, (px, py, pc) in enumerate(peers):
            theirs = dst.at[4 * px + 2 * py + pc]
            pltpu.make_async_remote_copy(src_ref=theirs, dst_ref=theirs, send_sem=ssem.at[r], recv_sem=rsem.at[r],
                                         device_id=(px, py, pc), device_id_type=_MESH).wait_recv()
        for cp in cps:
            cp.wait_send()
        own.wait()

    _sequencer_kernel(name, collective_id, (pltpu.SemaphoreType.DMA, pltpu.SemaphoreType.DMA((7,)),
                                            pltpu.SemaphoreType.DMA((7,))), body)
    return dst[...]


def _sum8(name, g8, after):
    R = g8.shape[1]
    tr = _tile(R, 512, 8)

    def body(g_ref, *rest):
        rest[-1][...] = (((g_ref[0] + g_ref[1]) + (g_ref[2] + g_ref[3]))
                         + ((g_ref[4] + g_ref[5]) + (g_ref[6] + g_ref[7])))

    return pl.pallas_call(body, grid=(R // tr,),
                          in_specs=[pl.BlockSpec((8, tr, 128), lambda i: (0, i, 0))] + [_ANY] * len(after),
                          out_specs=pl.BlockSpec((tr, 128), lambda i: (i, 0)), out_shape=jax.ShapeDtypeStruct((R, 128), f32),
                          name=name, compiler_params=_cparams(("parallel",)))(g8, *after)


def _add_own_half(name, g, t, c_idx, after):
    S, R, C = g.shape
    rh = R // 2
    tr = _tile(rh, 256, 16)
    nrb = rh // tr

    def body(c_ref, g_ref, t_ref, *rest):
        rest[-1][...] = (g_ref[...] + t_ref[...]).astype(bf16)

    gs = pltpu.PrefetchScalarGridSpec(
        num_scalar_prefetch=1, grid=(S, nrb),
        in_specs=[pl.BlockSpec((None, tr, C), lambda s, r, cr: (s, cr[0] * nrb + r, 0)),
                  pl.BlockSpec((None, tr, C), lambda s, r, cr: (s, r, 0))] + [_ANY] * len(after),
        out_specs=pl.BlockSpec((None, tr, C), lambda s, r, cr: (s, r, 0)))
    return pl.pallas_call(body, grid_spec=gs, out_shape=jax.ShapeDtypeStruct((S, rh, C), bf16), name=name,
                          compiler_params=_cparams(("parallel", "parallel")))(c_idx, g, t, *after)


def _add_shard_parts(name, g, t, r, kc_idx, after):
    S, R, C = g.shape
    rh = R // 2
    tr = _tile(rh, 256, 16)
    nrb = rh // tr

    def body(kc_ref, g_ref, t_ref, r_ref, *rest):
        own = g_ref[...] + t_ref[...]
        rest[-1][...] = (own + r_ref[1].astype(f32)) + (r_ref[0].astype(f32) + r_ref[2].astype(f32))

    gs = pltpu.PrefetchScalarGridSpec(
        num_scalar_prefetch=1, grid=(nrb,),
        in_specs=[pl.BlockSpec((None, tr, C), lambda i, kc: (kc[0], kc[1] * nrb + i, 0)),
                  pl.BlockSpec((None, tr, C), lambda i, kc: (kc[0], i, 0)),
                  pl.BlockSpec((3, tr, C), lambda i, kc: (0, i, 0))] + [_ANY] * len(after),
        out_specs=pl.BlockSpec((tr, C), lambda i, kc: (kc[1] * nrb + i, 0)))
    return pl.pallas_call(body, grid_spec=gs, out_shape=jax.ShapeDtypeStruct((R, C), f32), name=name,
                          compiler_params=_cparams(("parallel",)))(kc_idx, g, t, r, *after)


def _adamw_update(wv, gv, mv, vv):
    nm = _ADAM_B1 * mv + (1.0 - _ADAM_B1) * gv
    nv = _ADAM_B2 * vv + (1.0 - _ADAM_B2) * (gv * gv)
    m_hat = nm / (1.0 - _ADAM_B1 ** _ADAM_STEP)
    v_hat = nv / (1.0 - _ADAM_B2 ** _ADAM_STEP)
    return -_ADAM_LR * (m_hat / (jnp.sqrt(v_hat) + _ADAM_EPS) + _ADAM_WD * wv), nm, nv


def _adamw(name, w, g, m, v, after=()):
    R, C = w.shape
    tr = _tile(R, 128, 8)

    def body(w_ref, g_ref, m_ref, v_ref, *rest):
        go_ref, d_ref, nm_ref, nv_ref = rest[len(after):]
        gv = g_ref[...]
        go_ref[...] = gv
        d_ref[...], nm_ref[...], nv_ref[...] = _adamw_update(w_ref[...], gv, m_ref[...], v_ref[...])

    spec = pl.BlockSpec((tr, C), lambda i: (i, 0))
    shape = jax.ShapeDtypeStruct((R, C), f32)
    return pl.pallas_call(body, grid=(R // tr,), in_specs=[spec] * 4 + [_ANY] * len(after), out_specs=[spec] * 4,
                          out_shape=[shape] * 4, name=name, compiler_params=_cparams(("parallel",)))(w, g, m, v, *after)


def _adamw_whole(name, w, g, m, v):
    def body(w_ref, g_ref, m_ref, v_ref, d_ref, nm_ref, nv_ref):
        d_ref[...], nm_ref[...], nv_ref[...] = _adamw_update(w_ref[...], g_ref[...], m_ref[...], v_ref[...])

    vm = pl.BlockSpec(memory_space=pltpu.VMEM)
    return pl.pallas_call(body, in_specs=[vm] * 4, out_specs=[vm] * 3, out_shape=[jax.ShapeDtypeStruct(w.shape, f32)] * 3,
                          name=name, compiler_params=pltpu.CompilerParams(vmem_limit_bytes=_VMEM_LIMIT))(w, g, m, v)


def _pack(arrs):
    flat = jnp.concatenate([a.reshape(-1).astype(f32) for a in arrs])
    n = flat.shape[0]
    pad = (-n) % (128 * 128)
    return jnp.pad(flat, (0, pad)).reshape(-1, 128)


def _unpack(packed, shapes):
    flat = packed.reshape(-1)
    out, off = [], 0
    for s in shapes:
        n = math.prod(s)
        out.append(flat[off:off + n].reshape(s))
        off += n
    return out


_BIG = ("w_in", "w_glu", "w_ssm_out", "w_conv_out", "w_o", "w_up", "w_down")
_SMALL = ("norm_tok", "a_re", "a_im", "log_dt", "b_re", "b_im", "c_re", "c_im", "d_skip", "conv_w", "conv_b",
          "norm_ffn", "ffn_conv_w", "ffn_conv_b", "norm_final")
_WEIGHTS = ("norm_tok", "w_in", "a_re", "a_im", "log_dt", "b_re", "b_im", "c_re", "c_im", "d_skip", "w_glu",
            "w_ssm_out", "conv_w", "conv_b", "w_conv_out", "w_o", "norm_ffn", "w_up", "ffn_conv_w", "ffn_conv_b",
            "w_down", "norm_final")


def _step(cfg, x, tgt, p, m, v):
    L, D, SW, CW, F = cfg.L, cfg.D, cfg.SW, cfg.CW, cfg.F
    xi, yi, ci = _coords()
    k_idx = (2 * xi + yi).astype(jnp.int32).reshape(1)
    c_idx = ci.astype(jnp.int32).reshape(1)
    x = x.reshape(L, D)
    tgt = tgt.reshape(L, D)

    big2d = {n: p[n].reshape(p[n].shape[-2], p[n].shape[-1]) for n in _BIG}
    slots = {n: _cast_into_slot("cast_" + n, big2d[n], k_idx) for n in _BIG}
    wg = {}
    for cid, (gname, group) in enumerate((("allgather_w_in", ("w_in",)),
                                          ("allgather_w_mixer", ("w_glu", "w_ssm_out", "w_conv_out", "w_o")),
                                          ("allgather_w_up", ("w_up",)), ("allgather_w_down", ("w_down",)))):
        wg.update(zip(group, _allgather_weights(gname, [slots[n] for n in group], cid)))
    w_in, w_so, w_co, w_up = wg["w_in"], wg["w_ssm_out"], wg["w_conv_out"], wg["w_up"]
    w_glu, w_o, w_down = wg["w_glu"], wg["w_o"], wg["w_down"]
    kk = k_idx[0]
    cw_full = lax.dynamic_update_slice(jnp.zeros((3, CW), f32), p["conv_w"].reshape(3, CW // _NCHIP), (0, kk * (CW // _NCHIP)))
    fw_full = lax.dynamic_update_slice(jnp.zeros((3, F), f32), p["ffn_conv_w"].reshape(3, F // _NCHIP), (0, kk * (F // _NCHIP)))
    south = (ci == 0).astype(f32)
    conv_w, ffn_conv_w = _unpack(_allreduce8("allgather_conv_filters", _pack([cw_full * south, fw_full * south])),
                                 [(3, CW), (3, F)])
    conv_b = p["conv_b"].reshape(1, CW)
    ffn_conv_b = p["ffn_conv_b"].reshape(1, F)
    norm_tok = p["norm_tok"].reshape(1, D)
    norm_ffn = p["norm_ffn"].reshape(1, D)
    norm_final = p["norm_final"].reshape(1, D)
    dskip = p["d_skip"].reshape(1, SW)

    s5_in = (p["a_re"][0], p["a_im"][0], p["log_dt"][0], p["b_re"][0], p["b_im"][0])
    (abr, abi, bbr, bbi), disc_vjp = jax.vjp(_discretize, *s5_in)
    tabs = _s5_tables(cfg, abr, abi, bbr, bbi, p["c_re"][0], p["c_im"][0])

    xn1, r1 = _rms_fwd("rms_tok", x, norm_tok)
    proj = _mm_nn("in_proj", xn1, w_in, [bf16])[0]
    y_s, ya0, cin = _s5_fwd(cfg, proj, tabs, dskip)

    def tiles(*arrs):
        return lambda tm, tn: [(a, pl.BlockSpec((tm, tn), lambda i, j: (i, j))) for a in arrs]

    def glu_epi(acc, e, o):
        o[0][...] = (e[0][...].astype(f32) * jax.nn.sigmoid(acc)).astype(bf16)
        o[1][...] = acc.astype(bf16)

    ya1, z = _mm_nn("glu", ya0, w_glu, [bf16, bf16], rows=True, extras_fn=tiles(ya0), epilogue=glu_epi)
    yb0 = _convb_fwd(cfg, proj, conv_w, conv_b)
    merged, ya, yb = _merge_fwd(cfg, ya1, yb0, proj, w_so, w_co)

    def res_epi(acc, e, o):
        o[0][...] = e[0][...] + acc

    h1 = _mm_nn("out_proj", merged, w_o, [f32], rows=True, extras_fn=tiles(x), epilogue=res_epi)[0]
    xn2, r2 = _rms_fwd("rms_ffn", h1, norm_ffn)
    hh = _mm_nn("ffn_up", xn2, w_up, [bf16], tn=1408)[0]
    fact = _ffn_act(cfg, hh, ffn_conv_w, ffn_conv_b)
    h2 = _mm_nn("ffn_down", fact, w_down, [f32], tm=512, rows=True, extras_fn=tiles(h1), epilogue=res_epi)[0]
    dh2, dh2b, g_norm_final, loss_tile = _loss_head("loss_head", h2, tgt, norm_final)

    kc_idx = jnp.concatenate([k_idx, c_idx])
    reduced, chains = {}, {}

    def rs_halves(tag, collective_id, names, gs):
        chains[tag] = dict(cid=collective_id, names=names, gs=gs,
                           sib=_exchange_halves("grad_halves_" + tag, gs, collective_id))

    def rs_shards(tag, after):
        ch = chains[tag]
        ch["parts"] = [_add_own_half("grad_add_halves_" + n, g, t, c_idx, after)
                       for n, g, t in zip(ch["names"], ch["gs"], ch["sib"])]
        ch["chips"] = _scatter_shards("grad_shards_" + tag, ch["parts"], ch["cid"] + 1)
        return ch["parts"]

    def rs_join(tag, after):
        ch = chains[tag]
        ch["halves"] = [_add_shard_parts("grad_add_chips_" + n, g, t, r, kc_idx, after)
                        for n, g, t, r in zip(ch["names"], ch["gs"], ch["sib"], ch["chips"])]
        reduced.update(zip(ch["names"], _join_halves("grad_join_" + tag, ch["halves"], ch["cid"] + 2)))
        return ch["halves"]

    df = _mm_nt("ffn_down_dx", dh2b, w_down, [bf16], tn=1408, rows=True)[0]
    g_w_down = _mm_tn("ffn_down_dw", fact, dh2b, tm=1408, tn=512)
    rs_halves("ffn_down", 4, ["w_down"], [g_w_down.reshape(_NCHIP, F // _NCHIP, D)])
    dhh, g_ffn_conv_w, g_ffn_conv_b = _ffn_act_bwd(cfg, hh, df, ffn_conv_w, ffn_conv_b)
    sent = rs_shards("ffn_down", [dhh])
    g_w_up = _mm_tn("ffn_up_dw", xn2, dhh, shards=_NCHIP, tm=512, tn=1408, b_resident=True, after=sent)
    rs_halves("ffn_up", 7, ["w_up"], [g_w_up])
    dxn2 = _ffn_up_dx(cfg, dhh, w_up)
    sent = rs_shards("ffn_up", [dxn2]) + rs_join("ffn_down", [dxn2])
    dh1, dh1b, g_norm_ffn = _rms_bwd("rms_ffn_bwd", dxn2, h1, r2, norm_ffn, dh2, after=sent)

    dya, dyb, dproj = _out_proj_dx(cfg, dh1b, w_o, ya, yb, proj)
    g_w_o = _mm_tn("out_proj_dw", merged, dh1b)

    def glu_bwd_epi(acc, e, o):
        a0 = e[0][...].astype(f32)
        s = jax.nn.sigmoid(e[1][...].astype(f32))
        o[0][...] = (acc * a0 * s * (1.0 - s)).astype(bf16)
        o[1][...] = (acc * s).astype(bf16)

    dz, t1 = _mm_nt("ssm_out_dx", dya, w_so, [bf16, bf16], extras_fn=tiles(ya0, z), epilogue=glu_bwd_epi)
    g_w_so = _mm_tn("ssm_out_dw", ya1, dya, shards=_NCHIP, tn=512)
    dyb0 = _mm_nt("conv_out_dx", dyb, w_co, [bf16])[0]
    g_w_co = _mm_tn("conv_out_dw", yb0, dyb, shards=_NCHIP, tn=512)
    dproj, g_conv_w, g_conv_b = _convb_bwd(cfg, proj, dyb0, conv_w, conv_b, dproj)

    def gelu_bwd_epi(acc, e, o):
        o[0][...] = ((e[0][...].astype(f32) + acc) * _gelu_grad(e[1][...].astype(f32))).astype(bf16)

    dy_s = _mm_nt("glu_dx", dz, w_glu, [bf16], rows=True, extras_fn=tiles(t1, y_s), epilogue=gelu_bwd_epi)[0]
    g_w_glu = _mm_tn("glu_dw", ya0, dz)
    rs_halves("mixer", 10, ["w_o", "w_ssm_out", "w_conv_out", "w_glu"],
              [g_w_o.reshape(_NCHIP, D // _NCHIP, D), g_w_so, g_w_co, g_w_glu.reshape(_NCHIP, SW // _NCHIP, SW)])
    sent = rs_join("ffn_up", [g_w_glu])
    dproj, da_acc, db_full, dc_full, g_dskip = _s5_bwd(cfg, proj, dy_s, cin, tabs, dskip, dproj, after=sent)
    sent = rs_shards("mixer", [dproj])

    dabr, dabi, dbbr, dbbi, g_c_re, g_c_im = _s5_param_grads(cfg, da_acc, db_full, dc_full)
    g_a_re, g_a_im, g_log_dt, g_b_re, g_b_im = disc_vjp((dabr, dabi, dbbr, dbbi))
    small_g = {"a_re": g_a_re, "a_im": g_a_im, "log_dt": g_log_dt, "b_re": g_b_re, "b_im": g_b_im,
               "c_re": g_c_re, "c_im": g_c_im, "d_skip": g_dskip, "conv_w": g_conv_w, "conv_b": g_conv_b,
               "norm_ffn": g_norm_ffn, "ffn_conv_w": g_ffn_conv_w, "ffn_conv_b": g_ffn_conv_b, "norm_final": g_norm_final}
    early = [n for n in _SMALL if n != "norm_tok"]
    small8 = _allgather_small("allgather_small_grads", _pack([small_g[n] for n in early]), 16)

    g_w_in = _mm_tn("in_proj_dw", xn1, dproj, shards=_NCHIP, tn=CW, after=sent,
                    b_block=lambda j: jnp.where(j == 0, 3 * CW // SW, jnp.where(j < 4, j - 1, j)))
    rs_halves("in_proj", 13, ["w_in"], [g_w_in])
    dxn1 = _in_proj_dx(cfg, dproj, w_in)
    sent = rs_shards("in_proj", [dxn1]) + rs_join("mixer", [dxn1])
    dx, _, g_norm_tok = _rms_bwd("rms_tok_bwd", dxn1, x, r1, norm_tok, dh1, after=sent)

    summed = dict(zip(early, _unpack(_sum8("sum_small_grads", small8, [dx]), [small_g[n].shape for n in early])))
    summed["norm_tok"] = _unpack(_allreduce8("allreduce_norm_tok", _pack([g_norm_tok])), [g_norm_tok.shape])[0]
    summed["conv_w"] = lax.dynamic_slice(summed["conv_w"], (0, kk * (CW // _NCHIP)), (3, CW // _NCHIP))
    summed["ffn_conv_w"] = lax.dynamic_slice(summed["ffn_conv_w"], (0, kk * (F // _NCHIP)), (3, F // _NCHIP))

    grads, deltas, new_m, new_v = {}, {}, {}, {}

    def adamw_big(names, after):
        for n in names:
            g_, d_, m_, v_ = _adamw("adamw_" + n, big2d[n], reduced[n], m[n].reshape(big2d[n].shape),
                                    v[n].reshape(big2d[n].shape), after=after)
            grads[n], deltas[n], new_m[n], new_v[n] = (a.reshape(p[n].shape) for a in (g_, d_, m_, v_))
            after = [d_]
        return after

    for n in _SMALL:
        grads[n] = summed[n].reshape(p[n].shape)
        deltas[n], new_m[n], new_v[n] = _adamw_whole("adamw_" + n, p[n], grads[n], m[n], v[n])
    done = adamw_big(["w_down", "w_up", "w_o", "w_ssm_out", "w_conv_out", "w_glu"], [deltas["norm_final"]])
    rs_join("in_proj", done + [deltas[n] for n in _SMALL])
    adamw_big(["w_in"], ())

    loss = lax.psum(loss_tile[0, 0], ("x", "y", "c"))
    return (loss, dx.reshape(1, L, D), *[grads[n] for n in _WEIGHTS], *[deltas[n] for n in _WEIGHTS],
            *[new_m[n] for n in _WEIGHTS], *[new_v[n] for n in _WEIGHTS])


def kernel(x, norm_tok, w_in, a_re, a_im, log_dt, b_re, b_im, c_re, c_im, d_skip, w_glu, w_ssm_out, conv_w, conv_b, w_conv_out, w_o, norm_ffn, w_up, ffn_conv_w, ffn_conv_b, w_down, norm_final, loss_target, m_norm_tok, m_w_in, m_a_re, m_a_im, m_log_dt, m_b_re, m_b_im, m_c_re, m_c_im, m_d_skip, m_w_glu, m_w_ssm_out, m_conv_w, m_conv_b, m_w_conv_out, m_w_o, m_norm_ffn, m_w_up, m_ffn_conv_w, m_ffn_conv_b, m_w_down, m_norm_final, v_norm_tok, v_w_in, v_a_re, v_a_im, v_log_dt, v_b_re, v_b_im, v_c_re, v_c_im, v_d_skip, v_w_glu, v_w_ssm_out, v_conv_w, v_conv_b, v_w_conv_out, v_w_o, v_norm_ffn, v_w_up, v_ffn_conv_w, v_ffn_conv_b, v_w_down, v_norm_final):
    p = dict(norm_tok=norm_tok, w_in=w_in, a_re=a_re, a_im=a_im, log_dt=log_dt, b_re=b_re, b_im=b_im, c_re=c_re,
             c_im=c_im, d_skip=d_skip, w_glu=w_glu, w_ssm_out=w_ssm_out, conv_w=conv_w, conv_b=conv_b,
             w_conv_out=w_conv_out, w_o=w_o, norm_ffn=norm_ffn, w_up=w_up, ffn_conv_w=ffn_conv_w,
             ffn_conv_b=ffn_conv_b, w_down=w_down, norm_final=norm_final)
    m = dict(norm_tok=m_norm_tok, w_in=m_w_in, a_re=m_a_re, a_im=m_a_im, log_dt=m_log_dt, b_re=m_b_re, b_im=m_b_im,
             c_re=m_c_re, c_im=m_c_im, d_skip=m_d_skip, w_glu=m_w_glu, w_ssm_out=m_w_ssm_out, conv_w=m_conv_w,
             conv_b=m_conv_b, w_conv_out=m_w_conv_out, w_o=m_w_o, norm_ffn=m_norm_ffn, w_up=m_w_up,
             ffn_conv_w=m_ffn_conv_w, ffn_conv_b=m_ffn_conv_b, w_down=m_w_down, norm_final=m_norm_final)
    v = dict(norm_tok=v_norm_tok, w_in=v_w_in, a_re=v_a_re, a_im=v_a_im, log_dt=v_log_dt, b_re=v_b_re, b_im=v_b_im,
             c_re=v_c_re, c_im=v_c_im, d_skip=v_d_skip, w_glu=v_w_glu, w_ssm_out=v_w_ssm_out, conv_w=v_conv_w,
             conv_b=v_conv_b, w_conv_out=v_w_conv_out, w_o=v_w_o, norm_ffn=v_norm_ffn, w_up=v_w_up,
             ffn_conv_w=v_ffn_conv_w, ffn_conv_b=v_ffn_conv_b, w_down=v_w_down, norm_final=v_norm_final)
    return _step(_Cfg(), x, loss_target, p, m, v)
```

```python
import functools
import math
from typing import NamedTuple

import jax
import jax.numpy as jnp
from jax import lax
from jax.experimental import pallas as pl
from jax.experimental.pallas import tpu as pltpu
from jax.experimental.pallas import tpu_sc as plsc

f32 = jnp.float32
bf16 = jnp.bfloat16
_MESH = pl.DeviceIdType.MESH

_EPS = 1e-6
_ADAM_LR = 0.001
_ADAM_B1 = 0.9
_ADAM_B2 = 0.999
_ADAM_EPS = 1e-08
_ADAM_WD = 0.01
_ADAM_STEP = 10
_SSM_GROUP = 16
_SSM_STATE = 64
_SLAB_GROUPS = 16
_NCHIP = 4
_VMEM_LIMIT = 56 * 2**20
_VMEM_S5_BWD = 62 * 2**20
_GELU_C = math.sqrt(2.0 / math.pi)
_GELU_A = 0.044715


class _Cfg(NamedTuple):
    L: int = 4096
    D: int = 2048
    SW: int = 1024
    CW: int = 1024
    F: int = 5632
    T: int = 256


def _tile(n, pref, align):
    t = min(n, pref)
    t -= t % align
    while t > align and n % t:
        t -= align
    assert t > 0 and n % t == 0, (n, pref, align)
    return t


def _cparams(sem, vmem_limit=_VMEM_LIMIT):
    return pltpu.CompilerParams(dimension_semantics=sem, vmem_limit_bytes=vmem_limit)


def _gelu(x):
    return _gelu_and_grad(x)[0]


def _gelu_grad(x):
    return _gelu_and_grad(x)[1]


def _gelu_and_grad(x):
    x2 = x * x
    th = jnp.tanh(x * (_GELU_C + (_GELU_C * _GELU_A) * x2))
    half = 0.5 + 0.5 * th
    return x * half, half + (0.5 * x) * (1.0 - th * th) * (_GELU_C + (3.0 * _GELU_C * _GELU_A) * x2)


_NN = (((1,), (0,)), ((), ()))
_NT = (((1,), (1,)), ((), ()))
_TN = (((0,), (0,)), ((), ()))


def _whole(ref):
    return ref[...]


_ANY = pl.BlockSpec(memory_space=pl.ANY)


def _mm(name, operands, steps, *, grid, contract, outs, extras=(), epilogue=None, acc_shape=None, after=()):
    nop, ne, na = len(operands), len(extras), len(after)
    nk = len(steps)

    def body(*refs):
        op_refs = refs[:nop]
        e_refs = refs[nop:nop + ne]
        o_refs = refs[nop + ne + na:nop + ne + na + len(outs)]

        def partial(terms):
            tot = None
            for ai, av, bi, bv in terms:
                d = lax.dot_general(av(op_refs[ai]), bv(op_refs[bi]), contract, preferred_element_type=f32)
                tot = d if tot is None else tot + d
            return tot

        def finish(res):
            if epilogue is None:
                for o in o_refs:
                    o[...] = res.astype(o.dtype)
            else:
                epilogue(res, e_refs, o_refs)

        if nk == 1:
            finish(partial(steps[0][1]))
            return
        acc = refs[-1]
        kid = pl.program_id(len(grid) - 1)
        for k, terms in steps:
            def run(k=k, terms=terms):
                d = partial(terms)
                if k == 0:
                    acc[...] = d
                elif k < nk - 1:
                    acc[...] += d
                else:
                    finish(acc[...] + d)

            pl.when(kid == k)(run)

    sem = ("parallel",) * (len(grid) - (nk > 1)) + (("arbitrary",) if nk > 1 else ())
    return pl.pallas_call(
        body, grid=grid, in_specs=[o[1] for o in operands] + [e[1] for e in extras] + [_ANY] * na,
        out_specs=[o[1] for o in outs], out_shape=[o[0] for o in outs],
        scratch_shapes=[pltpu.VMEM(acc_shape, f32)] if nk > 1 else [], name=name,
        compiler_params=_cparams(sem))(*[o[0] for o in operands], *[e[0] for e in extras], *after)


def _mm_nn(name, a, w, out_dtypes, *, tm=1024, tn=1024, rows=False, extras_fn=None, epilogue=None):
    M, K = a.shape
    S, Ns = w.shape[0], w.shape[-1]
    N = Ns if rows else Ns * S
    tm, tn = _tile(M, tm, 16), _tile(Ns, tn, 128)
    nb = Ns // tn
    a_spec = pl.BlockSpec((tm, K), lambda i, j: (i, 0))
    if rows:
        b_spec = pl.BlockSpec((S, K // S, tn), lambda i, j: (0, 0, j))
        b_view = lambda r: r[...].reshape(K, tn)
    else:
        b_spec = pl.BlockSpec((None, K, tn), lambda i, j: (j // nb, 0, j % nb))
        b_view = _whole
    o_spec = pl.BlockSpec((tm, tn), lambda i, j: (i, j))
    outs = [(jax.ShapeDtypeStruct((M, N), dt), o_spec) for dt in out_dtypes]
    extras = extras_fn(tm, tn) if extras_fn is not None else ()
    return _mm(name, [(a, a_spec), (w, b_spec)], [(None, [(0, _whole, 1, b_view)])], grid=(M // tm, N // tn),
               contract=_NN, outs=outs, extras=extras, epilogue=epilogue)


def _mm_nt(name, a, w, out_dtypes, *, tm=1024, tn=1024, rows=False, extras_fn=None, epilogue=None):
    M, N = a.shape
    S, Ks, Ns = w.shape
    K = Ks * S if rows else Ks
    tm = _tile(M, tm, 16)
    a_spec = pl.BlockSpec((tm, N), lambda i, j: (i, 0))
    if rows:
        tn = K if tn >= K else _tile(Ks, tn, 128)
        if tn == K:
            b_spec = pl.BlockSpec((S, Ks, N), lambda i, j: (0, 0, 0))
            terms = [(0, _whole, 1, lambda r: r[...].reshape(K, N))]
        else:
            nbs = Ks // tn
            b_spec = pl.BlockSpec((None, tn, N), lambda i, j: (j // nbs, j % nbs, 0))
            terms = [(0, _whole, 1, _whole)]
    else:
        tn = _tile(K, tn, 128)
        assert S * Ns == N
        b_spec = pl.BlockSpec((S, tn, Ns), lambda i, j: (0, j, 0))
        terms = [(0, lambda r, s=s: r[:, s * Ns:(s + 1) * Ns], 1, lambda r, s=s: r[s]) for s in range(S)]
    o_spec = pl.BlockSpec((tm, tn), lambda i, j: (i, j))
    outs = [(jax.ShapeDtypeStruct((M, K), dt), o_spec) for dt in out_dtypes]
    extras = extras_fn(tm, tn) if extras_fn is not None else ()
    return _mm(name, [(a, a_spec), (w, b_spec)], [(None, terms)], grid=(M // tm, K // tn), contract=_NT,
               outs=outs, extras=extras, epilogue=epilogue)


def _mm_tn(name, a, b, *, shards=None, tm=1024, tn=1024, b_block=None, b_resident=False, after=()):
    M, K = a.shape
    halves = b.shape[0] if b.ndim == 3 else 1
    Nh = b.shape[-1]
    N = Nh * halves
    Ns = N // shards if shards else N
    tm, tn = _tile(K, tm, 128), _tile(math.gcd(Ns, Nh), tn, 128)
    nb, nbh = Ns // tn, Nh // tn
    ij = (lambda g0, g1: (g1, g0)) if b_resident else (lambda g0, g1: (g0, g1))
    bmap = b_block if b_block is not None else (lambda j: j)
    a_spec = pl.BlockSpec((M, tm), lambda g0, g1: (0, ij(g0, g1)[0]))
    if halves > 1:
        b_spec = pl.BlockSpec((None, M, tn), lambda g0, g1: (bmap(ij(g0, g1)[1]) // nbh, 0, bmap(ij(g0, g1)[1]) % nbh))
    else:
        b_spec = pl.BlockSpec((M, tn), lambda g0, g1: (0, bmap(ij(g0, g1)[1])))
    if shards:
        out = (jax.ShapeDtypeStruct((shards, K, Ns), f32),
               pl.BlockSpec((None, tm, tn), lambda g0, g1: (ij(g0, g1)[1] // nb, ij(g0, g1)[0], ij(g0, g1)[1] % nb)))
    else:
        out = (jax.ShapeDtypeStruct((K, N), f32), pl.BlockSpec((tm, tn), lambda g0, g1: ij(g0, g1)))
    grid = (N // tn, K // tm) if b_resident else (K // tm, N // tn)
    return _mm(name, [(a, a_spec), (b, b_spec)], [(None, [(0, _whole, 1, _whole)])], grid=grid, contract=_TN,
               outs=[out], after=after)[0]


def _in_proj_dx(cfg, dproj, w_in):
    L, D, SW, CW = cfg.L, cfg.D, cfg.SW, cfg.CW
    NP = SW + 3 * CW + 2 * D
    Ns = NP // _NCHIP
    assert SW + CW == Ns and 2 * CW == Ns and D == Ns
    tm, tn = _tile(L, 1024, 16), _tile(D, 1024, 128)
    a_spec = pl.BlockSpec((tm, NP // 2), lambda i, j, k: (i, k))
    b_spec = pl.BlockSpec((2, tn, Ns), lambda i, j, k: (k, j, 0))
    first = [(0, lambda r: r[:, 0:CW], 1, lambda r: r[0, :, SW:SW + CW]),
             (0, lambda r: r[:, CW:3 * CW], 1, lambda r: r[1]),
             (0, lambda r: r[:, 3 * CW:3 * CW + SW], 1, lambda r: r[0, :, 0:SW])]
    second = [(0, lambda r: r[:, 0:D], 1, lambda r: r[0]), (0, lambda r: r[:, D:2 * D], 1, lambda r: r[1])]
    out = (jax.ShapeDtypeStruct((L, D), bf16), pl.BlockSpec((tm, tn), lambda i, j, k: (i, j)))
    return _mm("in_proj_dx", [(dproj, a_spec), (w_in, b_spec)], [(0, first), (1, second)], grid=(L // tm, D // tn, 2),
               contract=_NT, outs=[out], acc_shape=(tm, tn))[0]


def _out_proj_dx(cfg, dh1b, w_o, ya, yb, proj):
    L, D = cfg.L, cfg.D
    NP = cfg.SW + 3 * cfg.CW + 2 * D
    assert NP == 4 * D
    tm = _tile(L, 512, 16)

    def epilogue(acc, e, o):
        sa = jax.nn.sigmoid(e[2][:, 0:D].astype(f32))
        sb = jax.nn.sigmoid(e[2][:, D:2 * D].astype(f32))
        o[0][...] = (acc * sa).astype(bf16)
        o[1][...] = (acc * sb).astype(bf16)
        o[2][:, 0:D] = (acc * e[0][...].astype(f32) * sa * (1.0 - sa)).astype(bf16)
        o[2][:, D:2 * D] = (acc * e[1][...].astype(f32) * sb * (1.0 - sb)).astype(bf16)

    row = pl.BlockSpec((tm, D), lambda i, j: (i, 0))
    half = pl.BlockSpec((tm, 2 * D), lambda i, j: (i, 1))
    return _mm("out_proj_dx", [(dh1b, row), (w_o, pl.BlockSpec(w_o.shape, lambda i, j: (0, 0, 0),
                                                                pipeline_mode=pl.Buffered(1)))],
               [(None, [(0, _whole, 1, lambda r: r[...].reshape(D, D))])], grid=(L // tm, 1), contract=_NT,
               outs=[(jax.ShapeDtypeStruct((L, D), bf16), row), (jax.ShapeDtypeStruct((L, D), bf16), row),
                     (jax.ShapeDtypeStruct((L, NP), bf16), half)],
               extras=[(ya, row), (yb, row), (proj, half)], epilogue=epilogue)


def _ffn_up_dx(cfg, dhh, w_up):
    L, D, F = cfg.L, cfg.D, cfg.F
    Fh = F // 2
    tm, tn = _tile(L, 1024, 16), _tile(D, 512, 128)
    a_spec = pl.BlockSpec((None, tm, F), lambda i, j, k: (k, i, 0))
    b_spec = pl.BlockSpec((2, tn, Fh), lambda i, j, k: (k, j, 0))
    terms = [(0, lambda r: r[:, 0:Fh], 1, lambda r: r[0]), (0, lambda r: r[:, Fh:F], 1, lambda r: r[1])]
    out = (jax.ShapeDtypeStruct((L, D), bf16), pl.BlockSpec((tm, tn), lambda i, j, k: (i, j)))
    return _mm("ffn_up_dx", [(dhh, a_spec), (w_up, b_spec)], [(0, terms), (1, terms)], grid=(L // tm, D // tn, 2),
               contract=_NT, outs=[out], acc_shape=(tm, tn))[0]


def _rms_fwd(name, x, g):
    L, D = x.shape
    tm = _tile(L, 256, 16)

    def body(x_ref, g_ref, xn_ref, r_ref):
        xv = x_ref[...]
        r = lax.rsqrt(jnp.mean(xv * xv, axis=-1, keepdims=True) + _EPS)
        xn_ref[...] = (xv * r * g_ref[...]).astype(bf16)
        r_ref[...] = r

    return pl.pallas_call(
        body, grid=(L // tm,),
        in_specs=[pl.BlockSpec((tm, D), lambda i: (i, 0)), pl.BlockSpec((1, D), lambda i: (0, 0))],
        out_specs=[pl.BlockSpec((tm, D), lambda i: (i, 0)), pl.BlockSpec((tm, 1), lambda i: (i, 0))],
        out_shape=[jax.ShapeDtypeStruct((L, D), bf16), jax.ShapeDtypeStruct((L, 1), f32)],
        name=name, compiler_params=_cparams(("parallel",)))(x, g)


def _rms_bwd(name, dxn, h, r, g, dres, after=()):
    L, D = h.shape
    tm = _tile(L, 256, 16)

    def body(dxn_ref, h_ref, r_ref, g_ref, dres_ref, *rest):
        dh_ref, dhb_ref, dg_ref = rest[len(after):]
        i = pl.program_id(0)
        d = dxn_ref[...].astype(f32)
        hv = h_ref[...]
        rv = r_ref[...]
        dyg = d * g_ref[...]
        m = jnp.mean(dyg * hv, axis=-1, keepdims=True)
        dh = dres_ref[...] + rv * dyg - hv * (rv * rv * rv) * m
        dh_ref[...] = dh
        dhb_ref[...] = dh.astype(bf16)

        @pl.when(i == 0)
        def _():
            dg_ref[...] = jnp.zeros_like(dg_ref)

        dg_ref[...] += jnp.sum(d * hv * rv, axis=0, keepdims=True)

    row = lambda i: (i, 0)
    return pl.pallas_call(
        body, grid=(L // tm,),
        in_specs=[pl.BlockSpec((tm, D), row), pl.BlockSpec((tm, D), row), pl.BlockSpec((tm, 1), row),
                  pl.BlockSpec((1, D), lambda i: (0, 0)), pl.BlockSpec((tm, D), row)] + [_ANY] * len(after),
        out_specs=[pl.BlockSpec((tm, D), row), pl.BlockSpec((tm, D), row), pl.BlockSpec((1, D), lambda i: (0, 0))],
        out_shape=[jax.ShapeDtypeStruct((L, D), f32), jax.ShapeDtypeStruct((L, D), bf16), jax.ShapeDtypeStruct((1, D), f32)],
        name=name, compiler_params=_cparams(("arbitrary",)))(dxn, h, r, g, dres, *after)


def _loss_head(name, h2, tgt, g):
    L, D = h2.shape
    tm = _tile(L, 256, 16)

    def body(h_ref, t_ref, g_ref, dh_ref, dhb_ref, dg_ref, loss_ref):
        i = pl.program_id(0)
        hv = h_ref[...]
        gv = g_ref[...]
        r = lax.rsqrt(jnp.mean(hv * hv, axis=-1, keepdims=True) + _EPS)
        err = hv * r * gv - t_ref[...]
        dy = err * (1.0 / D)
        dyg = dy * gv
        m = jnp.mean(dyg * hv, axis=-1, keepdims=True)
        dh = r * dyg - hv * (r * r * r) * m
        dh_ref[...] = dh
        dhb_ref[...] = dh.astype(bf16)

        @pl.when(i == 0)
        def _():
            dg_ref[...] = jnp.zeros_like(dg_ref)
            loss_ref[...] = jnp.zeros_like(loss_ref)

        dg_ref[...] += jnp.sum(dy * hv * r, axis=0, keepdims=True)
        part = jnp.sum(jnp.sum(err * err, axis=-1, keepdims=True), axis=0, keepdims=True) * (0.5 / D)
        loss_ref[...] += jnp.broadcast_to(part, (8, 128))

    row = lambda i: (i, 0)
    return pl.pallas_call(
        body, grid=(L // tm,),
        in_specs=[pl.BlockSpec((tm, D), row), pl.BlockSpec((tm, D), row), pl.BlockSpec((1, D), lambda i: (0, 0))],
        out_specs=[pl.BlockSpec((tm, D), row), pl.BlockSpec((tm, D), row), pl.BlockSpec((1, D), lambda i: (0, 0)),
                   pl.BlockSpec((8, 128), lambda i: (0, 0))],
        out_shape=[jax.ShapeDtypeStruct((L, D), f32), jax.ShapeDtypeStruct((L, D), bf16),
                   jax.ShapeDtypeStruct((1, D), f32), jax.ShapeDtypeStruct((8, 128), f32)],
        name=name, compiler_params=_cparams(("arbitrary",)))(h2, tgt, g)


def _shift_down(tile, halo, k, rows8):
    tm = tile.shape[0]
    r = pltpu.roll(tile, k, axis=0)
    hh = pltpu.roll(halo, k, axis=0)
    top = jnp.where(rows8 < k, hh, r[:8])
    return jnp.concatenate([top, r[8:]], axis=0) if tm > 8 else top


def _shift_up(tile, halo, k, rows8):
    tm = tile.shape[0]
    r = pltpu.roll(tile, tm - k, axis=0)
    hh = pltpu.roll(halo, 8 - k, axis=0)
    bot = jnp.where(rows8 >= 8 - k, hh, r[tm - 8:])
    return jnp.concatenate([r[:tm - 8], bot], axis=0) if tm > 8 else bot


def _conv3(x, halo, w_ref, b_ref, rows8):
    return (w_ref[0:1, :] * _shift_down(x, halo, 2, rows8) + w_ref[1:2, :] * _shift_down(x, halo, 1, rows8)
            + w_ref[2:3, :] * x + b_ref[...])


def _convb_fwd(cfg, proj, w, b):
    L, CW = cfg.L, cfg.CW
    assert cfg.SW == CW
    tm = _tile(L, 512, 16)

    def body(v_ref, vh_ref, gb_ref, gc_ref, gch_ref, w_ref, b_ref, o_ref):
        i = pl.program_id(0)
        rows8 = lax.broadcasted_iota(jnp.int32, (8, CW), 0)
        cv = gc_ref[...].astype(f32) * v_ref[...].astype(f32)
        cvh = gch_ref[...].astype(f32)[8:] * vh_ref[...].astype(f32)[8:]
        cvh = jnp.where(i == 0, 0.0, cvh)
        cc = _conv3(cv, cvh, w_ref, b_ref, rows8)
        o_ref[...] = (gb_ref[...].astype(f32) * cc).astype(bf16)

    blk = lambda col: pl.BlockSpec((tm, CW), lambda i: (i, col))
    halo = lambda col: pl.BlockSpec((16, CW), lambda i: (jnp.maximum(i * (tm // 16) - 1, 0), col))
    return pl.pallas_call(
        body, grid=(L // tm,),
        in_specs=[blk(1), halo(1), blk(2), blk(3), halo(3),
                  pl.BlockSpec((3, CW), lambda i: (0, 0)), pl.BlockSpec((1, CW), lambda i: (0, 0))],
        out_specs=pl.BlockSpec((tm, CW), lambda i: (i, 0)),
        out_shape=jax.ShapeDtypeStruct((L, CW), bf16),
        name="convb_fwd", compiler_params=_cparams(("parallel",)))(proj, proj, proj, proj, proj, w, b)


def _convb_bwd(cfg, proj, dyb0, w, b, dproj):
    L, CW = cfg.L, cfg.CW
    tm = _tile(L, 512, 16)
    nt = L // tm

    def body(v_ref, vh_ref, gb_ref, gbn_ref, gc_ref, gch_ref, d_ref, dn_ref, w_ref, b_ref, dproj_ref,
             o_ref, dw_ref, db_ref):
        i = pl.program_id(0)
        rows8 = lax.broadcasted_iota(jnp.int32, (8, CW), 0)
        v = v_ref[...].astype(f32)
        gb = gb_ref[...].astype(f32)
        gc = gc_ref[...].astype(f32)
        d = d_ref[...].astype(f32)
        cv = gc * v
        cvh = gch_ref[...].astype(f32)[8:] * vh_ref[...].astype(f32)[8:]
        cvh = jnp.where(i == 0, 0.0, cvh)
        s2 = _shift_down(cv, cvh, 2, rows8)
        s1 = _shift_down(cv, cvh, 1, rows8)
        cc = w_ref[0:1, :] * s2 + w_ref[1:2, :] * s1 + w_ref[2:3, :] * cv + b_ref[...]
        dcc = d * gb
        dccn = dn_ref[...].astype(f32)[:8] * gbn_ref[...].astype(f32)[:8]
        dccn = jnp.where(i == nt - 1, 0.0, dccn)
        dcv = (w_ref[2:3, :] * dcc + w_ref[1:2, :] * _shift_up(dcc, dccn, 1, rows8)
               + w_ref[0:1, :] * _shift_up(dcc, dccn, 2, rows8))
        o_ref[:, 0:CW] = (dcv * gc).astype(bf16)
        o_ref[:, CW:2 * CW] = (d * cc).astype(bf16)
        o_ref[:, 2 * CW:3 * CW] = (dcv * v).astype(bf16)

        @pl.when(i == 0)
        def _():
            dw_ref[...] = jnp.zeros_like(dw_ref)
            db_ref[...] = jnp.zeros_like(db_ref)

        dw_ref[0:1, :] += jnp.sum(dcc * s2, axis=0, keepdims=True)
        dw_ref[1:2, :] += jnp.sum(dcc * s1, axis=0, keepdims=True)
        dw_ref[2:3, :] += jnp.sum(dcc * cv, axis=0, keepdims=True)
        db_ref[...] += jnp.sum(dcc, axis=0, keepdims=True)

    blk = lambda col: pl.BlockSpec((tm, CW), lambda i: (i, col))
    prev = lambda col: pl.BlockSpec((16, CW), lambda i: (jnp.maximum(i * (tm // 16) - 1, 0), col))
    nxt = lambda col: pl.BlockSpec((16, CW), lambda i: (jnp.minimum((i + 1) * (tm // 16), L // 16 - 1), col))
    const = lambda r: pl.BlockSpec((r, CW), lambda i: (0, 0))
    return pl.pallas_call(
        body, grid=(nt,),
        in_specs=[blk(1), prev(1), blk(2), nxt(2), blk(3), prev(3), blk(0), nxt(0), const(3), const(1),
                  pl.BlockSpec(memory_space=pl.ANY)],
        out_specs=[pl.BlockSpec((tm, 3 * CW), lambda i: (i, 0)), const(3), const(1)],
        out_shape=[jax.ShapeDtypeStruct(dproj.shape, bf16), jax.ShapeDtypeStruct((3, CW), f32),
                   jax.ShapeDtypeStruct((1, CW), f32)],
        input_output_aliases={10: 0},
        name="convb_bwd", compiler_params=_cparams(("arbitrary",)))(proj, proj, proj, proj, proj, proj, dyb0, dyb0, w, b,
                                                                    dproj)


def _ffn_act(cfg, hh, w, b):
    L, F = cfg.L, cfg.F
    tm = _tile(L, 512, 16)
    tc = _tile(F, 1408, 128)
    ncb = F // tc

    def body(a_ref, ah_ref, g_ref, w_ref, b_ref, o_ref):
        i = pl.program_id(0)
        rows8 = lax.broadcasted_iota(jnp.int32, (8, tc), 0)
        a = a_ref[...].astype(f32)
        ah = jnp.where(i == 0, 0.0, ah_ref[...].astype(f32)[8:])
        o_ref[...] = (_gelu(_conv3(a, ah, w_ref, b_ref, rows8)) * g_ref[...].astype(f32)).astype(bf16)

    return pl.pallas_call(
        body, grid=(L // tm, ncb),
        in_specs=[pl.BlockSpec((tm, tc), lambda i, j: (i, j)),
                  pl.BlockSpec((16, tc), lambda i, j: (jnp.maximum(i * (tm // 16) - 1, 0), j)),
                  pl.BlockSpec((tm, tc), lambda i, j: (i, j + ncb)),
                  pl.BlockSpec((3, tc), lambda i, j: (0, j)), pl.BlockSpec((1, tc), lambda i, j: (0, j))],
        out_specs=pl.BlockSpec((tm, tc), lambda i, j: (i, j)),
        out_shape=jax.ShapeDtypeStruct((L, F), bf16),
        name="ffn_act", compiler_params=_cparams(("parallel", "parallel")))(hh, hh, hh, w, b)


def _ffn_act_bwd(cfg, hh, df, w, b):
    L, F = cfg.L, cfg.F
    tm = _tile(L, 512, 16)
    tc = _tile(F, 1408, 128)
    ncb = F // tc
    nt = L // tm

    def body(a_ref, ah_ref, an_ref, g_ref, gn_ref, d_ref, dn_ref, w_ref, b_ref, dhh_ref, dw_ref, db_ref):
        i = pl.program_id(1)
        rows8 = lax.broadcasted_iota(jnp.int32, (8, tc), 0)
        a = a_ref[...].astype(f32)
        ah = jnp.where(i == 0, 0.0, ah_ref[...].astype(f32)[8:])
        s2 = _shift_down(a, ah, 2, rows8)
        s1 = _shift_down(a, ah, 1, rows8)
        act = w_ref[0:1, :] * s2 + w_ref[1:2, :] * s1 + w_ref[2:3, :] * a + b_ref[...]
        d = d_ref[...].astype(f32)
        gelu, gelu_d = _gelu_and_grad(act)
        dhh_ref[1] = (d * gelu).astype(bf16)
        dact = d * g_ref[...].astype(f32) * gelu_d
        an = an_ref[...].astype(f32)[:8]
        actn = _conv3(an, a[tm - 8:], w_ref, b_ref, rows8)
        dactn = dn_ref[...].astype(f32)[:8] * gn_ref[...].astype(f32)[:8] * _gelu_grad(actn)
        dactn = jnp.where(i == nt - 1, 0.0, dactn)
        da = (w_ref[2:3, :] * dact + w_ref[1:2, :] * _shift_up(dact, dactn, 1, rows8)
              + w_ref[0:1, :] * _shift_up(dact, dactn, 2, rows8))
        dhh_ref[0] = da.astype(bf16)

        @pl.when(i == 0)
        def _():
            dw_ref[...] = jnp.zeros_like(dw_ref)
            db_ref[...] = jnp.zeros_like(db_ref)

        dw_ref[0:1, :] += jnp.sum(dact * s2, axis=0, keepdims=True)
        dw_ref[1:2, :] += jnp.sum(dact * s1, axis=0, keepdims=True)
        dw_ref[2:3, :] += jnp.sum(dact * a, axis=0, keepdims=True)
        db_ref[...] += jnp.sum(dact, axis=0, keepdims=True)

    blk = lambda off: pl.BlockSpec((tm, tc), lambda j, i: (i, j + off))
    prev = lambda off: pl.BlockSpec((16, tc), lambda j, i: (jnp.maximum(i * (tm // 16) - 1, 0), j + off))
    nxt = lambda off: pl.BlockSpec((16, tc), lambda j, i: (jnp.minimum((i + 1) * (tm // 16), L // 16 - 1), j + off))
    const = lambda r: pl.BlockSpec((r, tc), lambda j, i: (0, j))
    return pl.pallas_call(
        body, grid=(ncb, nt),
        in_specs=[blk(0), prev(0), nxt(0), blk(ncb), nxt(ncb), blk(0), nxt(0), const(3), const(1)],
        out_specs=[pl.BlockSpec((2, tm, tc), lambda j, i: (0, i, j)), const(3), const(1)],
        out_shape=[jax.ShapeDtypeStruct((2, L, F), bf16),
                   jax.ShapeDtypeStruct((3, F), f32), jax.ShapeDtypeStruct((1, F), f32)],
        name="ffn_act_bwd", compiler_params=_cparams(("parallel", "arbitrary")))(hh, hh, hh, hh, hh, df, df, w, b)


def _merge_fwd(cfg, ya1, yb0, proj, wso, wco):
    L, D, SW, CW = cfg.L, cfg.D, cfg.SW, cfg.CW
    Ns = D // _NCHIP
    tm = _tile(L, 1024, 16)
    tn = _tile(Ns, 512, 128)
    nb = Ns // tn
    off_a = (SW + 3 * CW) // tn
    off_b = (SW + 3 * CW + D) // tn

    def body(a_ref, b_ref, wa_ref, wb_ref, ma_ref, mb_ref, m_ref, ya_ref, yb_ref):
        ya = jnp.dot(a_ref[...], wa_ref[...], preferred_element_type=f32)
        yb = jnp.dot(b_ref[...], wb_ref[...], preferred_element_type=f32)
        sa = jax.nn.sigmoid(ma_ref[...].astype(f32))
        sb = jax.nn.sigmoid(mb_ref[...].astype(f32))
        m_ref[...] = (sa * ya + sb * yb).astype(bf16)
        ya_ref[...] = ya.astype(bf16)
        yb_ref[...] = yb.astype(bf16)

    o_spec = pl.BlockSpec((tm, tn), lambda i, j: (i, j))
    o_shape = jax.ShapeDtypeStruct((L, D), bf16)
    return pl.pallas_call(
        body, grid=(L // tm, D // tn),
        in_specs=[pl.BlockSpec((tm, SW), lambda i, j: (i, 0)), pl.BlockSpec((tm, CW), lambda i, j: (i, 0)),
                  pl.BlockSpec((None, SW, tn), lambda i, j: (j // nb, 0, j % nb)),
                  pl.BlockSpec((None, CW, tn), lambda i, j: (j // nb, 0, j % nb)),
                  pl.BlockSpec((tm, tn), lambda i, j: (i, off_a + j)), pl.BlockSpec((tm, tn), lambda i, j: (i, off_b + j))],
        out_specs=[o_spec, o_spec, o_spec], out_shape=[o_shape, o_shape, o_shape],
        name="merge_fwd", compiler_params=_cparams(("parallel", "parallel")))(ya1, yb0, wso, wco, proj, proj)


def _s5_dims(cfg):
    G = cfg.SW // _SSM_GROUP
    NS = G // _SLAB_GROUPS
    SC = _SLAB_GROUPS * _SSM_GROUP
    SH = _SLAB_GROUPS * _SSM_STATE
    NST = 2 * SH * NS
    return G, NS, SC, SH, NST


def _lane_slabs(cfg, W):
    _, NS, _, SH, _ = _s5_dims(cfg)
    return [(2 * SH * s + w0, 2 * SH * s + SH + w0) for s in range(NS) for w0 in range(0, SH, W)]


def _discretize(a_re, a_im, log_dt, b_re, b_im):
    dt = jnp.exp(log_dt)[:, None]
    mag = jnp.exp(dt * a_re)
    abr = mag * jnp.cos(dt * a_im)
    abi = mag * jnp.sin(dt * a_im)
    nr = abr - 1.0
    ni = abi
    den = a_re * a_re + a_im * a_im
    fr = (nr * a_re + ni * a_im) / den
    fi = (ni * a_re - nr * a_im) / den
    bbr = fr[..., None] * b_re - fi[..., None] * b_im
    bbi = fr[..., None] * b_im + fi[..., None] * b_re
    return abr, abi, bbr, bbi


def _state_rows(cfg, re, im):
    _, NS, _, SH, _ = _s5_dims(cfg)
    return jnp.concatenate([re.reshape(NS, SH), im.reshape(NS, SH)], axis=1).reshape(-1)


def _s5_tables(cfg, abr, abi, bbr, bbi, c_re, c_im):
    G, NS, SC, SH, NST = _s5_dims(cfg)
    S = cfg.T // 8
    eye = jnp.eye(_SLAB_GROUPS, dtype=f32)
    bb = jnp.stack([bbr, bbi]).reshape(2, NS, _SLAB_GROUPS, _SSM_STATE, _SSM_GROUP)
    bs = jnp.einsum("rsgph,gq->sghrqp", bb, eye).reshape(NS, SC, 2 * SH).astype(bf16)
    cc = jnp.stack([c_re, -c_im]).reshape(2, NS, _SLAB_GROUPS, _SSM_GROUP, _SSM_STATE)
    cs = jnp.einsum("rsghp,gq->srqpgh", cc, eye).reshape(NS, 2 * SH, SC).astype(bf16)
    arep = jnp.broadcast_to(_state_rows(cfg, abr, abi)[None, :], (8, NST))
    pr, pi = abr, abi
    for _ in range(S - 1):
        pr, pi = pr * abr - pi * abi, pr * abi + pi * abr
    apow = jnp.broadcast_to(_state_rows(cfg, pr, pi)[None, :], (8, NST))
    t = jnp.arange(cfg.T)
    perm = (t % 8) * S + t // 8
    pm = jax.nn.one_hot(perm, cfg.T, dtype=bf16)
    return bs, cs, arep, apow, pm, pm.T


def _cmul_add(ar, ai, xr, xi, br, bi):
    return ar * xr - ai * xi + br, ar * xi + ai * xr + bi


def _s5_forward_chunk(cfg, W, upb, bs_ref, arep_ref, apow_ref, st, x0, cin_store):
    _, NS, SC, SH, _ = _s5_dims(cfg)
    S = cfg.T // 8
    for s in range(NS):
        st[:, 2 * SH * s:2 * SH * (s + 1)] = jnp.dot(upb[:, SC * s:SC * (s + 1)], bs_ref[s], preferred_element_type=f32)
    rows = lax.broadcasted_iota(jnp.int32, (8, W), 0)
    zero = jnp.zeros((8, W), f32)
    for rc, ic in _lane_slabs(cfg, W):
        ar = arep_ref[:, rc:rc + W]
        ai = arep_ref[:, ic:ic + W]

        def step(i, carry, rc=rc, ic=ic, ar=ar, ai=ai):
            xr, xi = carry
            r0 = pl.multiple_of(i * 8, 8)
            nr, ni = _cmul_add(ar, ai, xr, xi, st[pl.ds(r0, 8), rc:rc + W], st[pl.ds(r0, 8), ic:ic + W])
            st[pl.ds(r0, 8), rc:rc + W] = nr
            st[pl.ds(r0, 8), ic:ic + W] = ni
            return nr, ni

        er, ei = lax.fori_loop(0, S, step, (zero, zero))
        pr = apow_ref[:, rc:rc + W]
        pi = apow_ref[:, ic:ic + W]
        x0r = x0[:, rc:rc + W]
        x0i = x0[:, ic:ic + W]
        cr = jnp.where(rows == 0, x0r, 0.0)
        ci = jnp.where(rows == 0, x0i, 0.0)
        for _ in range(7):
            fr, fi = _cmul_add(pr, pi, cr, ci, er, ei)
            cr = jnp.where(rows == 0, x0r, pltpu.roll(fr, 1, axis=0))
            ci = jnp.where(rows == 0, x0i, pltpu.roll(fi, 1, axis=0))
        fr, fi = _cmul_add(pr, pi, cr, ci, er, ei)
        x0[:, rc:rc + W] = jnp.broadcast_to(fr[7:8, :], (8, W))
        x0[:, ic:ic + W] = jnp.broadcast_to(fi[7:8, :], (8, W))
        cin_store(rc, ic, cr, ci)

        def fix(i, carry, rc=rc, ic=ic, ar=ar, ai=ai):
            kr, ki = carry
            r0 = pl.multiple_of(i * 8, 8)
            nr, ni = ar * kr - ai * ki, ar * ki + ai * kr
            st[pl.ds(r0, 8), rc:rc + W] = st[pl.ds(r0, 8), rc:rc + W] + nr
            st[pl.ds(r0, 8), ic:ic + W] = st[pl.ds(r0, 8), ic:ic + W] + ni
            return nr, ni

        lax.fori_loop(0, S, fix, (cr, ci))


def _s5_fwd(cfg, proj, tabs, dskip):
    L, SW, T = cfg.L, cfg.SW, cfg.T
    G, NS, SC, SH, NST = _s5_dims(cfg)
    bs, cs, arep, apow, pm, pmt = tabs
    W = min(512, SH)
    NC = L // T

    def body(u_ref, pm_ref, pmt_ref, bs_ref, cs_ref, arep_ref, apow_ref, dskip_ref, y_ref, ya0_ref, cin_ref, st, x0):
        c = pl.program_id(0)

        @pl.when(c == 0)
        def _():
            x0[...] = jnp.zeros_like(x0)

        up = jnp.dot(pm_ref[...], u_ref[...], preferred_element_type=f32)
        upb = up.astype(bf16)

        def cin_store(rc, ic, cr, ci):
            cin_ref[0, :, rc:rc + W] = cr
            cin_ref[0, :, ic:ic + W] = ci

        _s5_forward_chunk(cfg, W, upb, bs_ref, arep_ref, apow_ref, st, x0, cin_store)
        yp = jnp.concatenate(
            [jnp.dot(st[:, 2 * SH * s:2 * SH * (s + 1)].astype(bf16), cs_ref[s], preferred_element_type=f32)
             for s in range(NS)], axis=1) + dskip_ref[...] * up
        y = jnp.dot(pmt_ref[...], yp.astype(bf16), preferred_element_type=f32)
        y_ref[...] = y.astype(bf16)
        ya0_ref[...] = _gelu(y).astype(bf16)

    const2 = lambda shape: pl.BlockSpec(shape, lambda c: (0, 0))
    const3 = lambda shape: pl.BlockSpec(shape, lambda c: (0, 0, 0))
    return pl.pallas_call(
        body, grid=(NC,),
        in_specs=[pl.BlockSpec((T, SW), lambda c: (c, 0)), const2((T, T)), const2((T, T)), const3((NS, SC, 2 * SH)),
                  const3((NS, 2 * SH, SC)), const2((8, NST)), const2((8, NST)), const2((1, SW))],
        out_specs=[pl.BlockSpec((T, SW), lambda c: (c, 0)), pl.BlockSpec((T, SW), lambda c: (c, 0)),
                   pl.BlockSpec((1, 8, NST), lambda c: (c, 0, 0)), pl.BlockSpec((T, NST), lambda c: (c, 0))],
        out_shape=[jax.ShapeDtypeStruct((L, SW), bf16), jax.ShapeDtypeStruct((L, SW), bf16),
                   jax.ShapeDtypeStruct((NC, 8, NST), f32), jax.ShapeDtypeStruct((L, NST), f32)],
        scratch_shapes=[pltpu.VMEM((8, NST), f32)],
        name="s5_fwd", compiler_params=_cparams(("arbitrary",)))(proj, pm, pmt, bs, cs, arep, apow, dskip)


def _s5_bwd(cfg, proj, dy, cin, xs, tabs, dskip, dproj, after=()):
    L, SW, T = cfg.L, cfg.SW, cfg.T
    du_col = 3 * cfg.CW // SW
    G, NS, SC, SH, NST = _s5_dims(cfg)
    bs, cs, arep, apow, pm, pmt = tabs
    W = min(512, SH)
    S = T // 8
    NC = L // T

    def body(u_ref, dy_ref, cin_ref, st, pm_ref, pmt_ref, bs_ref, cs_ref, arep_ref, apow_ref, dskip_ref, dproj_ref, *rest):
        du_ref, da_ref, db_ref, dc_ref, dd_ref, gs, g0, db_acc, dc_acc = rest[len(after):]
        c = pl.program_id(0)

        @pl.when(c == 0)
        def _():
            g0[...] = jnp.zeros_like(g0)
            da_ref[...] = jnp.zeros_like(da_ref)
            dd_ref[...] = jnp.zeros_like(dd_ref)
            db_acc[...] = jnp.zeros_like(db_acc)
            dc_acc[...] = jnp.zeros_like(dc_acc)

        up = jnp.dot(pm_ref[...], u_ref[...], preferred_element_type=f32)
        upb = up.astype(bf16)
        dyp = jnp.dot(pm_ref[...], dy_ref[...], preferred_element_type=f32)
        dypb = dyp.astype(bf16)
        for s in range(NS):
            gs[:, 2 * SH * s:2 * SH * (s + 1)] = lax.dot_general(
                dypb[:, SC * s:SC * (s + 1)], cs_ref[s], (((1,), (1,)), ((), ())), preferred_element_type=f32)
        rows = lax.broadcasted_iota(jnp.int32, (8, W), 0)
        zero = jnp.zeros((8, W), f32)
        for rc, ic in _lane_slabs(cfg, W):
            ar = arep_ref[:, rc:rc + W]
            ai = arep_ref[:, ic:ic + W]

            def rstep(k, carry, rc=rc, ic=ic, ar=ar, ai=ai):
                gr, gi = carry
                r0 = pl.multiple_of((S - 1 - k) * 8, 8)
                nr = ar * gr + ai * gi + gs[pl.ds(r0, 8), rc:rc + W]
                ni = ar * gi - ai * gr + gs[pl.ds(r0, 8), ic:ic + W]
                gs[pl.ds(r0, 8), rc:rc + W] = nr
                gs[pl.ds(r0, 8), ic:ic + W] = ni
                return nr, ni

            er, ei = lax.fori_loop(0, S, rstep, (zero, zero))
            pr = apow_ref[:, rc:rc + W]
            pi = apow_ref[:, ic:ic + W]
            g0r = g0[:, rc:rc + W]
            g0i = g0[:, ic:ic + W]
            cr = jnp.where(rows == 7, g0r, 0.0)
            ci = jnp.where(rows == 7, g0i, 0.0)
            for _ in range(7):
                fr = er + pr * cr + pi * ci
                fi = ei + pr * ci - pi * cr
                cr = jnp.where(rows == 7, g0r, pltpu.roll(fr, 7, axis=0))
                ci = jnp.where(rows == 7, g0i, pltpu.roll(fi, 7, axis=0))
            fr = er + pr * cr + pi * ci
            fi = ei + pr * ci - pi * cr
            g0[:, rc:rc + W] = jnp.broadcast_to(fr[0:1, :], (8, W))
            g0[:, ic:ic + W] = jnp.broadcast_to(fi[0:1, :], (8, W))

            def fix(k, carry, rc=rc, ic=ic, ar=ar, ai=ai):
                kr, ki, accr, acci = carry
                i = S - 1 - k
                r0 = pl.multiple_of(i * 8, 8)
                rp = pl.multiple_of((i - 1) * 8, 8)
                nr = ar * kr + ai * ki
                ni = ar * ki - ai * kr
                gr = gs[pl.ds(r0, 8), rc:rc + W] + nr
                gi = gs[pl.ds(r0, 8), ic:ic + W] + ni
                gs[pl.ds(r0, 8), rc:rc + W] = gr
                gs[pl.ds(r0, 8), ic:ic + W] = gi
                xr = st[pl.ds(rp, 8), rc:rc + W]
                xi = st[pl.ds(rp, 8), ic:ic + W]
                return nr, ni, accr + gr * xr + gi * xi, acci + gi * xr - gr * xi

            kr, ki, accr, acci = lax.fori_loop(0, S - 1, fix, (cr, ci, zero, zero))
            nr = ar * kr + ai * ki
            ni = ar * ki - ai * kr
            gr = gs[0:8, rc:rc + W] + nr
            gi = gs[0:8, ic:ic + W] + ni
            gs[0:8, rc:rc + W] = gr
            gs[0:8, ic:ic + W] = gi
            xr = cin_ref[0, :, rc:rc + W]
            xi = cin_ref[0, :, ic:ic + W]
            da_ref[:, rc:rc + W] += accr + gr * xr + gi * xi
            da_ref[:, ic:ic + W] += acci + gi * xr - gr * xi

        dups = []
        for s in range(NS):
            gsb = gs[:, 2 * SH * s:2 * SH * (s + 1)].astype(bf16)
            dups.append(lax.dot_general(gsb, bs_ref[s], (((1,), (1,)), ((), ())), preferred_element_type=f32))
            db_acc[s] += lax.dot_general(upb[:, SC * s:SC * (s + 1)], gsb, (((0,), (0,)), ((), ())),
                                         preferred_element_type=f32)
            dc_acc[s] += lax.dot_general(st[:, 2 * SH * s:2 * SH * (s + 1)].astype(bf16), dypb[:, SC * s:SC * (s + 1)],
                                         (((0,), (0,)), ((), ())), preferred_element_type=f32)
        dup = jnp.concatenate(dups, axis=1) + dskip_ref[...] * dyp
        du_ref[...] = jnp.dot(pmt_ref[...], dup.astype(bf16), preferred_element_type=f32).astype(bf16)
        dd_ref[...] += jnp.sum(dyp * up, axis=0, keepdims=True)

        @pl.when(c == NC - 1)
        def _():
            PS, GH = _SSM_STATE, _SSM_GROUP
            mask_b = (lax.broadcasted_iota(jnp.int32, (SC, SH), 0) // GH
                      == lax.broadcasted_iota(jnp.int32, (SC, SH), 1) // PS)
            mask_c = (lax.broadcasted_iota(jnp.int32, (SH, SC), 0) // PS
                      == lax.broadcasted_iota(jnp.int32, (SH, SC), 1) // GH)
            for s in range(NS):
                for r in range(2):
                    xb = jnp.where(mask_b, db_acc[s, :, r * SH:(r + 1) * SH], 0.0)
                    zb = xb[:, 0:128]
                    for q in range(1, SH // 128):
                        zb = zb + xb[:, q * 128:(q + 1) * 128]
                    db_ref[s, r] = zb + pltpu.roll(zb, PS, axis=1)
                    xc = jnp.where(mask_c, dc_acc[s, r * SH:(r + 1) * SH, :], 0.0)
                    zc = xc[0:PS]
                    for q in range(1, SH // PS):
                        zc = zc + xc[q * PS:(q + 1) * PS]
                    dc_ref[s, r] = zc

    rev = lambda c: (NC - 1 - c, 0)
    const2 = lambda shape: pl.BlockSpec(shape, lambda c: (0, 0))
    const3 = lambda shape: pl.BlockSpec(shape, lambda c: (0, 0, 0))
    const4 = lambda shape: pl.BlockSpec(shape, lambda c: (0, 0, 0, 0))
    return pl.pallas_call(
        body, grid=(NC,),
        in_specs=[pl.BlockSpec((T, SW), rev), pl.BlockSpec((T, SW), rev), pl.BlockSpec((1, 8, NST), lambda c: (NC - 1 - c, 0, 0)),
                  pl.BlockSpec((T, NST), rev), const2((T, T)), const2((T, T)),
                  pl.BlockSpec((NS, SC, 2 * SH), lambda c: (0, 0, 0), pipeline_mode=pl.Buffered(1)),
                  pl.BlockSpec((NS, 2 * SH, SC), lambda c: (0, 0, 0), pipeline_mode=pl.Buffered(1)),
                  const2((8, NST)), const2((8, NST)), const2((1, SW)), _ANY] + [_ANY] * len(after),
        out_specs=[pl.BlockSpec((T, SW), lambda c: (NC - 1 - c, du_col)), const2((8, NST)),
                   const4((NS, 2, SC, 128)), const4((NS, 2, _SSM_STATE, SC)), const2((1, SW))],
        out_shape=[jax.ShapeDtypeStruct(dproj.shape, bf16), jax.ShapeDtypeStruct((8, NST), f32),
                   jax.ShapeDtypeStruct((NS, 2, SC, 128), f32), jax.ShapeDtypeStruct((NS, 2, _SSM_STATE, SC), f32),
                   jax.ShapeDtypeStruct((1, SW), f32)],
        scratch_shapes=[pltpu.VMEM((T, NST), f32), pltpu.VMEM((8, NST), f32),
                        pltpu.VMEM((NS, SC, 2 * SH), f32), pltpu.VMEM((NS, 2 * SH, SC), f32)],
        input_output_aliases={11: 0},
        name="s5_bwd", compiler_params=_cparams(("arbitrary",), _VMEM_S5_BWD))(proj, dy, cin, xs, pm, pmt, bs, cs, arep, apow, dskip, dproj,
                                                                  *after)


def _s5_param_grads(cfg, da, db_diag, dc_diag):
    G, NS, SC, SH, NST = _s5_dims(cfg)
    das = da.sum(axis=0).reshape(NS, 2, SH)
    dabr = das[:, 0].reshape(G, _SSM_STATE)
    dabi = das[:, 1].reshape(G, _SSM_STATE)
    dbd = db_diag[..., :_SSM_STATE].reshape(NS, 2, _SLAB_GROUPS, _SSM_GROUP, _SSM_STATE)
    dbb = dbd.transpose(1, 0, 2, 4, 3).reshape(2, G, _SSM_STATE, _SSM_GROUP)
    dcd = dc_diag.reshape(NS, 2, _SSM_STATE, _SLAB_GROUPS, _SSM_GROUP)
    dcc = dcd.transpose(1, 0, 3, 4, 2).reshape(2, G, _SSM_GROUP, _SSM_STATE)
    return dabr, dabi, dbb[0], dbb[1], dcc[0], -dcc[1]


def _coords():
    return lax.axis_index("x"), lax.axis_index("y"), lax.axis_index("c")


def _other_chips(x, y):
    return [(1 - x, y), (x, 1 - y), (1 - x, 1 - y)]


def _allreduce8(name, v):
    R = v.shape[0]

    def body(v_ref, o_ref, sib, chips, mine, ssem, rsem):
        x, y, c = _coords()
        d2d = pltpu.make_async_remote_copy(src_ref=v_ref, dst_ref=sib, send_sem=ssem.at[0], recv_sem=rsem.at[0],
                                           device_id=(x, y, 1 - c), device_id_type=_MESH)
        d2d.start()
        d2d.wait()
        mine[...] = v_ref[...] + sib[...]
        cps = [pltpu.make_async_remote_copy(src_ref=mine, dst_ref=chips.at[j], send_sem=ssem.at[1 + j],
                                            recv_sem=rsem.at[1 + j], device_id=(*chip, c), device_id_type=_MESH)
               for j, chip in enumerate(_other_chips(x, y))]
        for cp in cps:
            cp.start()
        for cp in cps:
            cp.wait()
        o_ref[...] = (mine[...] + chips[1]) + (chips[0] + chips[2])

    vm = pl.BlockSpec(memory_space=pltpu.VMEM)
    return pl.pallas_call(
        body, in_specs=[vm], out_specs=vm, out_shape=jax.ShapeDtypeStruct((R, 128), f32),
        scratch_shapes=[pltpu.VMEM((R, 128), f32), pltpu.VMEM((3, R, 128), f32), pltpu.VMEM((R, 128), f32),
                        pltpu.SemaphoreType.DMA((4,)), pltpu.SemaphoreType.DMA((4,))],
        name=name, compiler_params=pltpu.CompilerParams(vmem_limit_bytes=_VMEM_LIMIT))(v)


def _cast_into_slot(name, w, k_idx):
    R, C = w.shape
    tr = _tile(R, 256, 16)

    def body(k_ref, w_ref, o_ref):
        o_ref[...] = w_ref[...].astype(bf16)

    gs = pltpu.PrefetchScalarGridSpec(
        num_scalar_prefetch=1, grid=(R // tr,),
        in_specs=[pl.BlockSpec((tr, C), lambda r, kr: (r, 0))],
        out_specs=pl.BlockSpec((None, tr, C), lambda r, kr: (kr[0], r, 0)))
    return pl.pallas_call(body, grid_spec=gs, out_shape=jax.ShapeDtypeStruct((_NCHIP, R, C), bf16), name=name,
                          compiler_params=_cparams(("parallel",)))(k_idx, w)


def _handshake(peers):
    barrier = pltpu.get_barrier_semaphore()
    for peer in peers:
        pl.semaphore_signal(barrier, inc=1, device_id=peer, device_id_type=_MESH)
    pl.semaphore_wait(barrier, len(peers))


def _allgather_weights(name, bufs, collective_id):
    n = len(bufs)
    refs = [jax.new_ref(b, memory_space=pltpu.MemorySpace.HBM) for b in bufs]

    @pl.kernel(mesh=plsc.ScalarSubcoreMesh(axis_name="seq", num_cores=1), name=name,
               scratch_types=(pltpu.SemaphoreType.DMA((n, 3)), pltpu.SemaphoreType.DMA((n, 3)),
                              pltpu.SemaphoreType.DMA((n, 3)), pltpu.SemaphoreType.DMA((n, 3))),
               compiler_params=pltpu.CompilerParams(collective_id=collective_id))
    def launch(ssem, rsem, fssem, frsem):
        x, y, c = _coords()
        k = 2 * x + y
        others = _other_chips(x, y)
        _handshake([(x, y, 1 - c)] + [(*chip, c) for chip in others])
        sends, fwds = [], []
        for w in range(n):
            rh = refs[w].shape[1] // 2
            mine = refs[w].at[k, pl.ds(c * rh, rh)]
            for j, chip in enumerate(others):
                cp = pltpu.make_async_remote_copy(
                    src_ref=mine, dst_ref=mine, send_sem=ssem.at[w, j], recv_sem=rsem.at[w, j],
                    device_id=(*chip, c), device_id_type=_MESH)
                cp.start()
                sends.append(cp)
        for w in range(n):
            rh = refs[w].shape[1] // 2
            for j, (ox, oy) in enumerate(others):
                landed = refs[w].at[2 * ox + oy, pl.ds(c * rh, rh)]
                pltpu.make_async_remote_copy(
                    src_ref=landed, dst_ref=landed, send_sem=ssem.at[w, j], recv_sem=rsem.at[w, j],
                    device_id=(ox, oy, c), device_id_type=_MESH).wait_recv()
                cp = pltpu.make_async_remote_copy(
                    src_ref=landed, dst_ref=landed, send_sem=fssem.at[w, j], recv_sem=frsem.at[w, j],
                    device_id=(x, y, 1 - c), device_id_type=_MESH)
                cp.start()
                fwds.append(cp)
        for w in range(n):
            rh = refs[w].shape[1] // 2
            for j, (ox, oy) in enumerate(others):
                passed = refs[w].at[2 * ox + oy, pl.ds((1 - c) * rh, rh)]
                pltpu.make_async_remote_copy(
                    src_ref=passed, dst_ref=passed, send_sem=fssem.at[w, j], recv_sem=frsem.at[w, j],
                    device_id=(x, y, 1 - c), device_id_type=_MESH).wait_recv()
        for cp in sends + fwds:
            cp.wait_send()

    launch()
    return [r[...] for r in refs]


def _sequencer_kernel(name, collective_id, sems, body):
    pl.kernel(body, mesh=plsc.ScalarSubcoreMesh(axis_name="seq", num_cores=1), name=name, scratch_types=sems,
              compiler_params=pltpu.CompilerParams(collective_id=collective_id))()


def _hbm_ref(a):
    return jax.new_ref(a, memory_space=pltpu.MemorySpace.HBM)


def _exchange_halves(name, grads, collective_id):
    n = len(grads)
    srcs = [_hbm_ref(g) for g in grads]
    dsts = [jax.empty_ref(jax.ShapeDtypeStruct((g.shape[0], g.shape[1] // 2, g.shape[2]), g.dtype),
                          memory_space=pltpu.MemorySpace.HBM) for g in grads]

    def body(ssem, rsem):
        x, y, c = _coords()
        _handshake([(x, y, 1 - c)])
        cps = []
        for w in range(n):
            rh = srcs[w].shape[1] // 2
            cp = pltpu.make_async_remote_copy(
                src_ref=srcs[w].at[:, pl.ds((1 - c) * rh, rh)], dst_ref=dsts[w], send_sem=ssem.at[w], recv_sem=rsem.at[w],
                device_id=(x, y, 1 - c), device_id_type=_MESH)
            cp.start()
            cps.append(cp)
        for cp in cps:
            cp.wait()

    _sequencer_kernel(name, collective_id, (pltpu.SemaphoreType.DMA((n,)), pltpu.SemaphoreType.DMA((n,))), body)
    return [d[...] for d in dsts]


def _scatter_shards(name, parts, collective_id):
    n = len(parts)
    srcs = [_hbm_ref(p) for p in parts]
    dsts = [jax.empty_ref(jax.ShapeDtypeStruct((3,) + p.shape[1:], p.dtype), memory_space=pltpu.MemorySpace.HBM)
            for p in parts]

    def body(ssem, rsem):
        x, y, c = _coords()
        others = _other_chips(x, y)
        _handshake([(*chip, c) for chip in others])
        cps = []
        for w in range(n):
            for j, (ox, oy) in enumerate(others):
                cp = pltpu.make_async_remote_copy(
                    src_ref=srcs[w].at[2 * ox + oy], dst_ref=dsts[w].at[j], send_sem=ssem.at[w, j], recv_sem=rsem.at[w, j],
                    device_id=(ox, oy, c), device_id_type=_MESH)
                cp.start()
                cps.append(cp)
        for cp in cps:
            cp.wait()

    _sequencer_kernel(name, collective_id, (pltpu.SemaphoreType.DMA((n, 3)), pltpu.SemaphoreType.DMA((n, 3))), body)
    return [d[...] for d in dsts]


def _join_halves(name, bufs, collective_id):
    n = len(bufs)
    refs = [_hbm_ref(b) for b in bufs]

    def body(ssem, rsem):
        x, y, c = _coords()
        _handshake([(x, y, 1 - c)])
        cps = []
        for w in range(n):
            rh = refs[w].shape[0] // 2
            mine = refs[w].at[pl.ds(c * rh, rh)]
            cp = pltpu.make_async_remote_copy(src_ref=mine, dst_ref=mine, send_sem=ssem.at[w], recv_sem=rsem.at[w],
                                              device_id=(x, y, 1 - c), device_id_type=_MESH)
            cp.start()
            cps.append(cp)
        for w in range(n):
            rh = refs[w].shape[0] // 2
            theirs = refs[w].at[pl.ds((1 - c) * rh, rh)]
            pltpu.make_async_remote_copy(src_ref=theirs, dst_ref=theirs, send_sem=ssem.at[w], recv_sem=rsem.at[w],
                                         device_id=(x, y, 1 - c), device_id_type=_MESH).wait_recv()
        for cp in cps:
            cp.wait_send()

    _sequencer_kernel(name, collective_id, (pltpu.SemaphoreType.DMA((n,)), pltpu.SemaphoreType.DMA((n,))), body)
    return [r[...] for r in refs]


def _allgather_small(name, v, collective_id):
    R = v.shape[0]
    src = _hbm_ref(v)
    dst = jax.empty_ref(jax.ShapeDtypeStruct((8, R, 128), v.dtype), memory_space=pltpu.MemorySpace.HBM)
    flips = [(dx, dy, dc) for dx in (0, 1) for dy in (0, 1) for dc in (0, 1)][1:]

    def body(lsem, ssem, rsem):
        x, y, c = _coords()
        me = 4 * x + 2 * y + c
        flip = lambda v, d: 1 - v if d else v
        peers = [(flip(x, dx), flip(y, dy), flip(c, dc)) for dx, dy, dc in flips]
        _handshake(peers)
        own = pltpu.make_async_copy(src, dst.at[me], lsem)
        own.start()
        cps = []
        for r, peer in enumerate(peers):
            cp = pltpu.make_async_remote_copy(src_ref=src, dst_ref=dst.at[me], send_sem=ssem.at[r], recv_sem=rsem.at[r],
                                              device_id=peer, device_id_type=_MESH)
            cp.start()
            cps.append(cp)
        for r, (px, py, pc) in enumerate(peers):
            theirs = dst.at[4 * px + 2 * py + pc]
            pltpu.make_async_remote_copy(src_ref=theirs, dst_ref=theirs, send_sem=ssem.at[r], recv_sem=rsem.at[r],
                                         device_id=(px, py, pc), device_id_type=_MESH).wait_recv()
        for cp in cps:
            cp.wait_send()
        own.wait()

    _sequencer_kernel(name, collective_id, (pltpu.SemaphoreType.DMA, pltpu.SemaphoreType.DMA((7,)),
                                            pltpu.SemaphoreType.DMA((7,))), body)
    return dst[...]


def _sum8(name, g8, after):
    R = g8.shape[1]
    tr = _tile(R, 512, 8)

    def body(g_ref, *rest):
        rest[-1][...] = (((g_ref[0] + g_ref[1]) + (g_ref[2] + g_ref[3]))
                         + ((g_ref[4] + g_ref[5]) + (g_ref[6] + g_ref[7])))

    return pl.pallas_call(body, grid=(R // tr,),
                          in_specs=[pl.BlockSpec((8, tr, 128), lambda i: (0, i, 0))] + [_ANY] * len(after),
                          out_specs=pl.BlockSpec((tr, 128), lambda i: (i, 0)), out_shape=jax.ShapeDtypeStruct((R, 128), f32),
                          name=name, compiler_params=_cparams(("parallel",)))(g8, *after)


def _add_own_half(name, g, t, c_idx, after):
    S, R, C = g.shape
    rh = R // 2
    tr = _tile(rh, 256, 16)
    nrb = rh // tr

    def body(c_ref, g_ref, t_ref, *rest):
        rest[-1][...] = (g_ref[...] + t_ref[...]).astype(bf16)

    gs = pltpu.PrefetchScalarGridSpec(
        num_scalar_prefetch=1, grid=(S, nrb),
        in_specs=[pl.BlockSpec((None, tr, C), lambda s, r, cr: (s, cr[0] * nrb + r, 0)),
                  pl.BlockSpec((None, tr, C), lambda s, r, cr: (s, r, 0))] + [_ANY] * len(after),
        out_specs=pl.BlockSpec((None, tr, C), lambda s, r, cr: (s, r, 0)))
    return pl.pallas_call(body, grid_spec=gs, out_shape=jax.ShapeDtypeStruct((S, rh, C), bf16), name=name,
                          compiler_params=_cparams(("parallel", "parallel")))(c_idx, g, t, *after)


def _add_shard_parts(name, g, t, r, kc_idx, after):
    S, R, C = g.shape
    rh = R // 2
    tr = _tile(rh, 256, 16)
    nrb = rh // tr

    def body(kc_ref, g_ref, t_ref, r_ref, *rest):
        own = g_ref[...] + t_ref[...]
        rest[-1][...] = (own + r_ref[1].astype(f32)) + (r_ref[0].astype(f32) + r_ref[2].astype(f32))

    gs = pltpu.PrefetchScalarGridSpec(
        num_scalar_prefetch=1, grid=(nrb,),
        in_specs=[pl.BlockSpec((None, tr, C), lambda i, kc: (kc[0], kc[1] * nrb + i, 0)),
                  pl.BlockSpec((None, tr, C), lambda i, kc: (kc[0], i, 0)),
                  pl.BlockSpec((3, tr, C), lambda i, kc: (0, i, 0))] + [_ANY] * len(after),
        out_specs=pl.BlockSpec((tr, C), lambda i, kc: (kc[1] * nrb + i, 0)))
    return pl.pallas_call(body, grid_spec=gs, out_shape=jax.ShapeDtypeStruct((R, C), f32), name=name,
                          compiler_params=_cparams(("parallel",)))(kc_idx, g, t, r, *after)


def _adamw_update(wv, gv, mv, vv):
    nm = _ADAM_B1 * mv + (1.0 - _ADAM_B1) * gv
    nv = _ADAM_B2 * vv + (1.0 - _ADAM_B2) * (gv * gv)
    m_hat = nm / (1.0 - _ADAM_B1 ** _ADAM_STEP)
    v_hat = nv / (1.0 - _ADAM_B2 ** _ADAM_STEP)
    return -_ADAM_LR * (m_hat / (jnp.sqrt(v_hat) + _ADAM_EPS) + _ADAM_WD * wv), nm, nv


def _adamw(name, w, g, m, v, after=()):
    R, C = w.shape
    tr = _tile(R, 128, 8)

    def body(w_ref, g_ref, m_ref, v_ref, *rest):
        go_ref, d_ref, nm_ref, nv_ref = rest[len(after):]
        gv = g_ref[...]
        go_ref[...] = gv
        d_ref[...], nm_ref[...], nv_ref[...] = _adamw_update(w_ref[...], gv, m_ref[...], v_ref[...])

    spec = pl.BlockSpec((tr, C), lambda i: (i, 0))
    shape = jax.ShapeDtypeStruct((R, C), f32)
    return pl.pallas_call(body, grid=(R // tr,), in_specs=[spec] * 4 + [_ANY] * len(after), out_specs=[spec] * 4,
                          out_shape=[shape] * 4, name=name, compiler_params=_cparams(("parallel",)))(w, g, m, v, *after)


def _adamw_whole(name, w, g, m, v):
    def body(w_ref, g_ref, m_ref, v_ref, d_ref, nm_ref, nv_ref):
        d_ref[...], nm_ref[...], nv_ref[...] = _adamw_update(w_ref[...], g_ref[...], m_ref[...], v_ref[...])

    vm = pl.BlockSpec(memory_space=pltpu.VMEM)
    return pl.pallas_call(body, in_specs=[vm] * 4, out_specs=[vm] * 3, out_shape=[jax.ShapeDtypeStruct(w.shape, f32)] * 3,
                          name=name, compiler_params=pltpu.CompilerParams(vmem_limit_bytes=_VMEM_LIMIT))(w, g, m, v)


def _pack(arrs):
    flat = jnp.concatenate([a.reshape(-1).astype(f32) for a in arrs])
    n = flat.shape[0]
    pad = (-n) % (128 * 128)
    return jnp.pad(flat, (0, pad)).reshape(-1, 128)


def _unpack(packed, shapes):
    flat = packed.reshape(-1)
    out, off = [], 0
    for s in shapes:
        n = math.prod(s)
        out.append(flat[off:off + n].reshape(s))
        off += n
    return out


_BIG = ("w_in", "w_glu", "w_ssm_out", "w_conv_out", "w_o", "w_up", "w_down")
_SMALL = ("norm_tok", "a_re", "a_im", "log_dt", "b_re", "b_im", "c_re", "c_im", "d_skip", "conv_w", "conv_b",
          "norm_ffn", "ffn_conv_w", "ffn_conv_b", "norm_final")
_WEIGHTS = ("norm_tok", "w_in", "a_re", "a_im", "log_dt", "b_re", "b_im", "c_re", "c_im", "d_skip", "w_glu",
            "w_ssm_out", "conv_w", "conv_b", "w_conv_out", "w_o", "norm_ffn", "w_up", "ffn_conv_w", "ffn_conv_b",
            "w_down", "norm_final")


def _step(cfg, x, tgt, p, m, v):
    L, D, SW, CW, F = cfg.L, cfg.D, cfg.SW, cfg.CW, cfg.F
    xi, yi, ci = _coords()
    k_idx = (2 * xi + yi).astype(jnp.int32).reshape(1)
    c_idx = ci.astype(jnp.int32).reshape(1)
    x = x.reshape(L, D)
    tgt = tgt.reshape(L, D)

    big2d = {n: p[n].reshape(p[n].shape[-2], p[n].shape[-1]) for n in _BIG}
    slots = {n: _cast_into_slot("cast_" + n, big2d[n], k_idx) for n in _BIG}
    wg = {}
    for cid, (gname, group) in enumerate((("allgather_w_in", ("w_in",)),
                                          ("allgather_w_mixer", ("w_glu", "w_ssm_out", "w_conv_out", "w_o")),
                                          ("allgather_w_up", ("w_up",)), ("allgather_w_down", ("w_down",)))):
        wg.update(zip(group, _allgather_weights(gname, [slots[n] for n in group], cid)))
    w_in, w_so, w_co, w_up = wg["w_in"], wg["w_ssm_out"], wg["w_conv_out"], wg["w_up"]
    w_glu, w_o, w_down = wg["w_glu"], wg["w_o"], wg["w_down"]
    kk = k_idx[0]
    cw_full = lax.dynamic_update_slice(jnp.zeros((3, CW), f32), p["conv_w"].reshape(3, CW // _NCHIP), (0, kk * (CW // _NCHIP)))
    fw_full = lax.dynamic_update_slice(jnp.zeros((3, F), f32), p["ffn_conv_w"].reshape(3, F // _NCHIP), (0, kk * (F // _NCHIP)))
    south = (ci == 0).astype(f32)
    conv_w, ffn_conv_w = _unpack(_allreduce8("allgather_conv_filters", _pack([cw_full * south, fw_full * south])),
                                 [(3, CW), (3, F)])
    conv_b = p["conv_b"].reshape(1, CW)
    ffn_conv_b = p["ffn_conv_b"].reshape(1, F)
    norm_tok = p["norm_tok"].reshape(1, D)
    norm_ffn = p["norm_ffn"].reshape(1, D)
    norm_final = p["norm_final"].reshape(1, D)
    dskip = p["d_skip"].reshape(1, SW)

    s5_in = (p["a_re"][0], p["a_im"][0], p["log_dt"][0], p["b_re"][0], p["b_im"][0])
    (abr, abi, bbr, bbi), disc_vjp = jax.vjp(_discretize, *s5_in)
    tabs = _s5_tables(cfg, abr, abi, bbr, bbi, p["c_re"][0], p["c_im"][0])

    xn1, r1 = _rms_fwd("rms_tok", x, norm_tok)
    proj = _mm_nn("in_proj", xn1, w_in, [bf16])[0]
    y_s, ya0, cin, xs = _s5_fwd(cfg, proj, tabs, dskip)

    def tiles(*arrs):
        return lambda tm, tn: [(a, pl.BlockSpec((tm, tn), lambda i, j: (i, j))) for a in arrs]

    def glu_epi(acc, e, o):
        o[0][...] = (e[0][...].astype(f32) * jax.nn.sigmoid(acc)).astype(bf16)
        o[1][...] = acc.astype(bf16)

    ya1, z = _mm_nn("glu", ya0, w_glu, [bf16, bf16], rows=True, extras_fn=tiles(ya0), epilogue=glu_epi)
    yb0 = _convb_fwd(cfg, proj, conv_w, conv_b)
    merged, ya, yb = _merge_fwd(cfg, ya1, yb0, proj, w_so, w_co)

    def res_epi(acc, e, o):
        o[0][...] = e[0][...] + acc

    h1 = _mm_nn("out_proj", merged, w_o, [f32], rows=True, extras_fn=tiles(x), epilogue=res_epi)[0]
    xn2, r2 = _rms_fwd("rms_ffn", h1, norm_ffn)
    hh = _mm_nn("ffn_up", xn2, w_up, [bf16], tn=1408)[0]
    fact = _ffn_act(cfg, hh, ffn_conv_w, ffn_conv_b)
    h2 = _mm_nn("ffn_down", fact, w_down, [f32], tm=512, rows=True, extras_fn=tiles(h1), epilogue=res_epi)[0]
    dh2, dh2b, g_norm_final, loss_tile = _loss_head("loss_head", h2, tgt, norm_final)

    kc_idx = jnp.concatenate([k_idx, c_idx])
    reduced, chains = {}, {}

    def rs_halves(tag, collective_id, names, gs):
        chains[tag] = dict(cid=collective_id, names=names, gs=gs,
                           sib=_exchange_halves("grad_halves_" + tag, gs, collective_id))

    def rs_shards(tag, after):
        ch = chains[tag]
        ch["parts"] = [_add_own_half("grad_add_halves_" + n, g, t, c_idx, after)
                       for n, g, t in zip(ch["names"], ch["gs"], ch["sib"])]
        ch["chips"] = _scatter_shards("grad_shards_" + tag, ch["parts"], ch["cid"] + 1)
        return ch["parts"]

    def rs_join(tag, after):
        ch = chains[tag]
        ch["halves"] = [_add_shard_parts("grad_add_chips_" + n, g, t, r, kc_idx, after)
                        for n, g, t, r in zip(ch["names"], ch["gs"], ch["sib"], ch["chips"])]
        reduced.update(zip(ch["names"], _join_halves("grad_join_" + tag, ch["halves"], ch["cid"] + 2)))
        return ch["halves"]

    df = _mm_nt("ffn_down_dx", dh2b, w_down, [bf16], tn=1408, rows=True)[0]
    g_w_down = _mm_tn("ffn_down_dw", fact, dh2b, tm=1408, tn=512)
    rs_halves("ffn_down", 4, ["w_down"], [g_w_down.reshape(_NCHIP, F // _NCHIP, D)])
    dhh, g_ffn_conv_w, g_ffn_conv_b = _ffn_act_bwd(cfg, hh, df, ffn_conv_w, ffn_conv_b)
    sent = rs_shards("ffn_down", [dhh])
    g_w_up = _mm_tn("ffn_up_dw", xn2, dhh, shards=_NCHIP, tm=512, tn=1408, b_resident=True, after=sent)
    rs_halves("ffn_up", 7, ["w_up"], [g_w_up])
    dxn2 = _ffn_up_dx(cfg, dhh, w_up)
    sent = rs_shards("ffn_up", [dxn2]) + rs_join("ffn_down", [dxn2])
    dh1, dh1b, g_norm_ffn = _rms_bwd("rms_ffn_bwd", dxn2, h1, r2, norm_ffn, dh2, after=sent)

    dya, dyb, dproj = _out_proj_dx(cfg, dh1b, w_o, ya, yb, proj)
    g_w_o = _mm_tn("out_proj_dw", merged, dh1b)

    def glu_bwd_epi(acc, e, o):
        a0 = e[0][...].astype(f32)
        s = jax.nn.sigmoid(e[1][...].astype(f32))
        o[0][...] = (acc * a0 * s * (1.0 - s)).astype(bf16)
        o[1][...] = (acc * s).astype(bf16)

    dz, t1 = _mm_nt("ssm_out_dx", dya, w_so, [bf16, bf16], extras_fn=tiles(ya0, z), epilogue=glu_bwd_epi)
    g_w_so = _mm_tn("ssm_out_dw", ya1, dya, shards=_NCHIP, tn=512)
    dyb0 = _mm_nt("conv_out_dx", dyb, w_co, [bf16])[0]
    g_w_co = _mm_tn("conv_out_dw", yb0, dyb, shards=_NCHIP, tn=512)
    dproj, g_conv_w, g_conv_b = _convb_bwd(cfg, proj, dyb0, conv_w, conv_b, dproj)

    def gelu_bwd_epi(acc, e, o):
        o[0][...] = ((e[0][...].astype(f32) + acc) * _gelu_grad(e[1][...].astype(f32))).astype(bf16)

    dy_s = _mm_nt("glu_dx", dz, w_glu, [bf16], rows=True, extras_fn=tiles(t1, y_s), epilogue=gelu_bwd_epi)[0]
    g_w_glu = _mm_tn("glu_dw", ya0, dz)
    rs_halves("mixer", 10, ["w_o", "w_ssm_out", "w_conv_out", "w_glu"],
              [g_w_o.reshape(_NCHIP, D // _NCHIP, D), g_w_so, g_w_co, g_w_glu.reshape(_NCHIP, SW // _NCHIP, SW)])
    sent = rs_join("ffn_up", [g_w_glu])
    dproj, da_acc, db_full, dc_full, g_dskip = _s5_bwd(cfg, proj, dy_s, cin, xs, tabs, dskip, dproj, after=sent)
    sent = rs_shards("mixer", [dproj])

    dabr, dabi, dbbr, dbbi, g_c_re, g_c_im = _s5_param_grads(cfg, da_acc, db_full, dc_full)
    g_a_re, g_a_im, g_log_dt, g_b_re, g_b_im = disc_vjp((dabr, dabi, dbbr, dbbi))
    small_g = {"a_re": g_a_re, "a_im": g_a_im, "log_dt": g_log_dt, "b_re": g_b_re, "b_im": g_b_im,
               "c_re": g_c_re, "c_im": g_c_im, "d_skip": g_dskip, "conv_w": g_conv_w, "conv_b": g_conv_b,
               "norm_ffn": g_norm_ffn, "ffn_conv_w": g_ffn_conv_w, "ffn_conv_b": g_ffn_conv_b, "norm_final": g_norm_final}
    early = [n for n in _SMALL if n != "norm_tok"]
    small8 = _allgather_small("allgather_small_grads", _pack([small_g[n] for n in early]), 16)

    g_w_in = _mm_tn("in_proj_dw", xn1, dproj, shards=_NCHIP, tn=CW, after=sent,
                    b_block=lambda j: jnp.where(j == 0, 3 * CW // SW, jnp.where(j < 4, j - 1, j)))
    rs_halves("in_proj", 13, ["w_in"], [g_w_in])
    dxn1 = _in_proj_dx(cfg, dproj, w_in)
    sent = rs_shards("in_proj", [dxn1]) + rs_join("mixer", [dxn1])
    dx, _, g_norm_tok = _rms_bwd("rms_tok_bwd", dxn1, x, r1, norm_tok, dh1, after=sent)

    summed = dict(zip(early, _unpack(_sum8("sum_small_grads", small8, [dx]), [small_g[n].shape for n in early])))
    summed["norm_tok"] = _unpack(_allreduce8("allreduce_norm_tok", _pack([g_norm_tok])), [g_norm_tok.shape])[0]
    summed["conv_w"] = lax.dynamic_slice(summed["conv_w"], (0, kk * (CW // _NCHIP)), (3, CW // _NCHIP))
    summed["ffn_conv_w"] = lax.dynamic_slice(summed["ffn_conv_w"], (0, kk * (F // _NCHIP)), (3, F // _NCHIP))

    grads, deltas, new_m, new_v = {}, {}, {}, {}

    def adamw_big(names, after):
        for n in names:
            g_, d_, m_, v_ = _adamw("adamw_" + n, big2d[n], reduced[n], m[n].reshape(big2d[n].shape),
                                    v[n].reshape(big2d[n].shape), after=after)
            grads[n], deltas[n], new_m[n], new_v[n] = (a.reshape(p[n].shape) for a in (g_, d_, m_, v_))
            after = [d_]
        return after

    for n in _SMALL:
        grads[n] = summed[n].reshape(p[n].shape)
        deltas[n], new_m[n], new_v[n] = _adamw_whole("adamw_" + n, p[n], grads[n], m[n], v[n])
    done = adamw_big(["w_down", "w_up", "w_o", "w_ssm_out", "w_conv_out", "w_glu"], [deltas["norm_final"]])
    rs_join("in_proj", done + [deltas[n] for n in _SMALL])
    adamw_big(["w_in"], ())

    loss = lax.psum(loss_tile[0, 0], ("x", "y", "c"))
    return (loss, dx.reshape(1, L, D), *[grads[n] for n in _WEIGHTS], *[deltas[n] for n in _WEIGHTS],
            *[new_m[n] for n in _WEIGHTS], *[new_v[n] for n in _WEIGHTS])


def kernel(x, norm_tok, w_in, a_re, a_im, log_dt, b_re, b_im, c_re, c_im, d_skip, w_glu, w_ssm_out, conv_w, conv_b, w_conv_out, w_o, norm_ffn, w_up, ffn_conv_w, ffn_conv_b, w_down, norm_final, loss_target, m_norm_tok, m_w_in, m_a_re, m_a_im, m_log_dt, m_b_re, m_b_im, m_c_re, m_c_im, m_d_skip, m_w_glu, m_w_ssm_out, m_conv_w, m_conv_b, m_w_conv_out, m_w_o, m_norm_ffn, m_w_up, m_ffn_conv_w, m_ffn_conv_b, m_w_down, m_norm_final, v_norm_tok, v_w_in, v_a_re, v_a_im, v_log_dt, v_b_re, v_b_im, v_c_re, v_c_im, v_d_skip, v_w_glu, v_w_ssm_out, v_conv_w, v_conv_b, v_w_conv_out, v_w_o, v_norm_ffn, v_w_up, v_ffn_conv_w, v_ffn_conv_b, v_w_down, v_norm_final):
    p = dict(norm_tok=norm_tok, w_in=w_in, a_re=a_re, a_im=a_im, log_dt=log_dt, b_re=b_re, b_im=b_im, c_re=c_re,
             c_im=c_im, d_skip=d_skip, w_glu=w_glu, w_ssm_out=w_ssm_out, conv_w=conv_w, conv_b=conv_b,
             w_conv_out=w_conv_out, w_o=w_o, norm_ffn=norm_ffn, w_up=w_up, ffn_conv_w=ffn_conv_w,
             ffn_conv_b=ffn_conv_b, w_down=w_down, norm_final=norm_final)
    m = dict(norm_tok=m_norm_tok, w_in=m_w_in, a_re=m_a_re, a_im=m_a_im, log_dt=m_log_dt, b_re=m_b_re, b_im=m_b_im,
             c_re=m_c_re, c_im=m_c_im, d_skip=m_d_skip, w_glu=m_w_glu, w_ssm_out=m_w_ssm_out, conv_w=m_conv_w,
             conv_b=m_conv_b, w_conv_out=m_w_conv_out, w_o=m_w_o, norm_ffn=m_norm_ffn, w_up=m_w_up,
             ffn_conv_w=m_ffn_conv_w, ffn_conv_b=m_ffn_conv_b, w_down=m_w_down, norm_final=m_norm_final)
    v = dict(norm_tok=v_norm_tok, w_in=v_w_in, a_re=v_a_re, a_im=v_a_im, log_dt=v_log_dt, b_re=v_b_re, b_im=v_b_im,
             c_re=v_c_re, c_im=v_c_im, d_skip=v_d_skip, w_glu=v_w_glu, w_ssm_out=v_w_ssm_out, conv_w=v_conv_w,
             conv_b=v_conv_b, w_conv_out=v_w_conv_out, w_o=v_w_o, norm_ffn=v_norm_ffn, w_up=v_w_up,
             ffn_conv_w=v_ffn_conv_w, ffn_conv_b=v_ffn_conv_b, w_down=v_w_down, norm_final=v_norm_final)
    return _step(_Cfg(), x, loss_target, p, m, v)
```

```python
import functools
import math
from typing import NamedTuple

import jax
import jax.numpy as jnp
from jax import lax
from jax.experimental import pallas as pl
from jax.experimental.pallas import tpu as pltpu
from jax.experimental.pallas import tpu_sc as plsc

f32 = jnp.float32
bf16 = jnp.bfloat16
_MESH = pl.DeviceIdType.MESH

_EPS = 1e-6
_ADAM_LR = 0.001
_ADAM_B1 = 0.9
_ADAM_B2 = 0.999
_ADAM_EPS = 1e-08
_ADAM_WD = 0.01
_ADAM_STEP = 10
_SSM_GROUP = 16
_SSM_STATE = 64
_SLAB_GROUPS = 16
_NCHIP = 4
_VMEM_LIMIT = 56 * 2**20
_VMEM_S5_BWD = 62 * 2**20
_GELU_C = math.sqrt(2.0 / math.pi)
_GELU_A = 0.044715


class _Cfg(NamedTuple):
    L: int = 4096
    D: int = 2048
    SW: int = 1024
    CW: int = 1024
    F: int = 5632
    T: int = 256


def _tile(n, pref, align):
    t = min(n, pref)
    t -= t % align
    while t > align and n % t:
        t -= align
    assert t > 0 and n % t == 0, (n, pref, align)
    return t


def _cparams(sem, vmem_limit=_VMEM_LIMIT):
    return pltpu.CompilerParams(dimension_semantics=sem, vmem_limit_bytes=vmem_limit)


def _gelu(x):
    return _gelu_and_grad(x)[0]


def _gelu_grad(x):
    return _gelu_and_grad(x)[1]


def _gelu_and_grad(x):
    x2 = x * x
    th = jnp.tanh(x * (_GELU_C + (_GELU_C * _GELU_A) * x2))
    half = 0.5 + 0.5 * th
    return x * half, half + (0.5 * x) * (1.0 - th * th) * (_GELU_C + (3.0 * _GELU_C * _GELU_A) * x2)


_NN = (((1,), (0,)), ((), ()))
_NT = (((1,), (1,)), ((), ()))
_TN = (((0,), (0,)), ((), ()))


def _whole(ref):
    return ref[...]


_ANY = pl.BlockSpec(memory_space=pl.ANY)


def _mm(name, operands, steps, *, grid, contract, outs, extras=(), epilogue=None, acc_shape=None, after=()):
    nop, ne, na = len(operands), len(extras), len(after)
    nk = len(steps)

    def body(*refs):
        op_refs = refs[:nop]
        e_refs = refs[nop:nop + ne]
        o_refs = refs[nop + ne + na:nop + ne + na + len(outs)]

        def partial(terms):
            tot = None
            for ai, av, bi, bv in terms:
                d = lax.dot_general(av(op_refs[ai]), bv(op_refs[bi]), contract, preferred_element_type=f32)
                tot = d if tot is None else tot + d
            return tot

        def finish(res):
            if epilogue is None:
                for o in o_refs:
                    o[...] = res.astype(o.dtype)
            else:
                epilogue(res, e_refs, o_refs)

        if nk == 1:
            finish(partial(steps[0][1]))
            return
        acc = refs[-1]
        kid = pl.program_id(len(grid) - 1)
        for k, terms in steps:
            def run(k=k, terms=terms):
                d = partial(terms)
                if k == 0:
                    acc[...] = d
                elif k < nk - 1:
                    acc[...] += d
                else:
                    finish(acc[...] + d)

            pl.when(kid == k)(run)

    sem = ("parallel",) * (len(grid) - (nk > 1)) + (("arbitrary",) if nk > 1 else ())
    return pl.pallas_call(
        body, grid=grid, in_specs=[o[1] for o in operands] + [e[1] for e in extras] + [_ANY] * na,
        out_specs=[o[1] for o in outs], out_shape=[o[0] for o in outs],
        scratch_shapes=[pltpu.VMEM(acc_shape, f32)] if nk > 1 else [], name=name,
        compiler_params=_cparams(sem))(*[o[0] for o in operands], *[e[0] for e in extras], *after)


def _mm_nn(name, a, w, out_dtypes, *, tm=1024, tn=1024, rows=False, extras_fn=None, epilogue=None):
    M, K = a.shape
    S, Ns = w.shape[0], w.shape[-1]
    N = Ns if rows else Ns * S
    tm, tn = _tile(M, tm, 16), _tile(Ns, tn, 128)
    nb = Ns // tn
    a_spec = pl.BlockSpec((tm, K), lambda i, j: (i, 0))
    if rows:
        b_spec = pl.BlockSpec((S, K // S, tn), lambda i, j: (0, 0, j))
        b_view = lambda r: r[...].reshape(K, tn)
    else:
        b_spec = pl.BlockSpec((None, K, tn), lambda i, j: (j // nb, 0, j % nb))
        b_view = _whole
    o_spec = pl.BlockSpec((tm, tn), lambda i, j: (i, j))
    outs = [(jax.ShapeDtypeStruct((M, N), dt), o_spec) for dt in out_dtypes]
    extras = extras_fn(tm, tn) if extras_fn is not None else ()
    return _mm(name, [(a, a_spec), (w, b_spec)], [(None, [(0, _whole, 1, b_view)])], grid=(M // tm, N // tn),
               contract=_NN, outs=outs, extras=extras, epilogue=epilogue)


def _mm_nt(name, a, w, out_dtypes, *, tm=1024, tn=1024, rows=False, extras_fn=None, epilogue=None):
    M, N = a.shape
    S, Ks, Ns = w.shape
    K = Ks * S if rows else Ks
    tm = _tile(M, tm, 16)
    a_spec = pl.BlockSpec((tm, N), lambda i, j: (i, 0))
    if rows:
        tn = K if tn >= K else _tile(Ks, tn, 128)
        if tn == K:
            b_spec = pl.BlockSpec((S, Ks, N), lambda i, j: (0, 0, 0))
            terms = [(0, _whole, 1, lambda r: r[...].reshape(K, N))]
        else:
            nbs = Ks // tn
            b_spec = pl.BlockSpec((None, tn, N), lambda i, j: (j // nbs, j % nbs, 0))
            terms = [(0, _whole, 1, _whole)]
    else:
        tn = _tile(K, tn, 128)
        assert S * Ns == N
        b_spec = pl.BlockSpec((S, tn, Ns), lambda i, j: (0, j, 0))
        terms = [(0, lambda r, s=s: r[:, s * Ns:(s + 1) * Ns], 1, lambda r, s=s: r[s]) for s in range(S)]
    o_spec = pl.BlockSpec((tm, tn), lambda i, j: (i, j))
    outs = [(jax.ShapeDtypeStruct((M, K), dt), o_spec) for dt in out_dtypes]
    extras = extras_fn(tm, tn) if extras_fn is not None else ()
    return _mm(name, [(a, a_spec), (w, b_spec)], [(None, terms)], grid=(M // tm, K // tn), contract=_NT,
               outs=outs, extras=extras, epilogue=epilogue)


def _mm_tn(name, a, b, *, shards=None, tm=1024, tn=1024, b_block=None, b_resident=False, after=()):
    M, K = a.shape
    halves = b.shape[0] if b.ndim == 3 else 1
    Nh = b.shape[-1]
    N = Nh * halves
    Ns = N // shards if shards else N
    tm, tn = _tile(K, tm, 128), _tile(math.gcd(Ns, Nh), tn, 128)
    nb, nbh = Ns // tn, Nh // tn
    ij = (lambda g0, g1: (g1, g0)) if b_resident else (lambda g0, g1: (g0, g1))
    bmap = b_block if b_block is not None else (lambda j: j)
    a_spec = pl.BlockSpec((M, tm), lambda g0, g1: (0, ij(g0, g1)[0]))
    if halves > 1:
        b_spec = pl.BlockSpec((None, M, tn), lambda g0, g1: (bmap(ij(g0, g1)[1]) // nbh, 0, bmap(ij(g0, g1)[1]) % nbh))
    else:
        b_spec = pl.BlockSpec((M, tn), lambda g0, g1: (0, bmap(ij(g0, g1)[1])))
    if shards:
        out = (jax.ShapeDtypeStruct((shards, K, Ns), f32),
               pl.BlockSpec((None, tm, tn), lambda g0, g1: (ij(g0, g1)[1] // nb, ij(g0, g1)[0], ij(g0, g1)[1] % nb)))
    else:
        out = (jax.ShapeDtypeStruct((K, N), f32), pl.BlockSpec((tm, tn), lambda g0, g1: ij(g0, g1)))
    grid = (N // tn, K // tm) if b_resident else (K // tm, N // tn)
    return _mm(name, [(a, a_spec), (b, b_spec)], [(None, [(0, _whole, 1, _whole)])], grid=grid, contract=_TN,
               outs=[out], after=after)[0]


def _in_proj_dx(cfg, dproj, w_in):
    L, D, SW, CW = cfg.L, cfg.D, cfg.SW, cfg.CW
    NP = SW + 3 * CW + 2 * D
    Ns = NP // _NCHIP
    assert SW + CW == Ns and 2 * CW == Ns and D == Ns
    tm, tn = _tile(L, 1024, 16), _tile(D, 1024, 128)
    a_spec = pl.BlockSpec((tm, NP // 2), lambda i, j, k: (i, k))
    b_spec = pl.BlockSpec((2, tn, Ns), lambda i, j, k: (k, j, 0))
    first = [(0, lambda r: r[:, 0:CW], 1, lambda r: r[0, :, SW:SW + CW]),
             (0, lambda r: r[:, CW:3 * CW], 1, lambda r: r[1]),
             (0, lambda r: r[:, 3 * CW:3 * CW + SW], 1, lambda r: r[0, :, 0:SW])]
    second = [(0, lambda r: r[:, 0:D], 1, lambda r: r[0]), (0, lambda r: r[:, D:2 * D], 1, lambda r: r[1])]
    out = (jax.ShapeDtypeStruct((L, D), bf16), pl.BlockSpec((tm, tn), lambda i, j, k: (i, j)))
    return _mm("in_proj_dx", [(dproj, a_spec), (w_in, b_spec)], [(0, first), (1, second)], grid=(L // tm, D // tn, 2),
               contract=_NT, outs=[out], acc_shape=(tm, tn))[0]


def _out_proj_dx(cfg, dh1b, w_o, ya, yb, proj):
    L, D = cfg.L, cfg.D
    NP = cfg.SW + 3 * cfg.CW + 2 * D
    assert NP == 4 * D
    tm = _tile(L, 512, 16)

    def epilogue(acc, e, o):
        sa = jax.nn.sigmoid(e[2][:, 0:D].astype(f32))
        sb = jax.nn.sigmoid(e[2][:, D:2 * D].astype(f32))
        o[0][...] = (acc * sa).astype(bf16)
        o[1][...] = (acc * sb).astype(bf16)
        o[2][:, 0:D] = (acc * e[0][...].astype(f32) * sa * (1.0 - sa)).astype(bf16)
        o[2][:, D:2 * D] = (acc * e[1][...].astype(f32) * sb * (1.0 - sb)).astype(bf16)

    row = pl.BlockSpec((tm, D), lambda i, j: (i, 0))
    half = pl.BlockSpec((tm, 2 * D), lambda i, j: (i, 1))
    return _mm("out_proj_dx", [(dh1b, row), (w_o, pl.BlockSpec(w_o.shape, lambda i, j: (0, 0, 0),
                                                                pipeline_mode=pl.Buffered(1)))],
               [(None, [(0, _whole, 1, lambda r: r[...].reshape(D, D))])], grid=(L // tm, 1), contract=_NT,
               outs=[(jax.ShapeDtypeStruct((L, D), bf16), row), (jax.ShapeDtypeStruct((L, D), bf16), row),
                     (jax.ShapeDtypeStruct((L, NP), bf16), half)],
               extras=[(ya, row), (yb, row), (proj, half)], epilogue=epilogue)


def _ffn_up_dx(cfg, dhh, w_up):
    L, D, F = cfg.L, cfg.D, cfg.F
    Fh = F // 2
    tm, tn = _tile(L, 1024, 16), _tile(D, 512, 128)
    a_spec = pl.BlockSpec((None, tm, F), lambda i, j, k: (k, i, 0))
    b_spec = pl.BlockSpec((2, tn, Fh), lambda i, j, k: (k, j, 0))
    terms = [(0, lambda r: r[:, 0:Fh], 1, lambda r: r[0]), (0, lambda r: r[:, Fh:F], 1, lambda r: r[1])]
    out = (jax.ShapeDtypeStruct((L, D), bf16), pl.BlockSpec((tm, tn), lambda i, j, k: (i, j)))
    return _mm("ffn_up_dx", [(dhh, a_spec), (w_up, b_spec)], [(0, terms), (1, terms)], grid=(L // tm, D // tn, 2),
               contract=_NT, outs=[out], acc_shape=(tm, tn))[0]


def _rms_fwd(name, x, g):
    L, D = x.shape
    tm = _tile(L, 256, 16)

    def body(x_ref, g_ref, xn_ref, r_ref):
        xv = x_ref[...]
        r = lax.rsqrt(jnp.mean(xv * xv, axis=-1, keepdims=True) + _EPS)
        xn_ref[...] = (xv * r * g_ref[...]).astype(bf16)
        r_ref[...] = r

    return pl.pallas_call(
        body, grid=(L // tm,),
        in_specs=[pl.BlockSpec((tm, D), lambda i: (i, 0)), pl.BlockSpec((1, D), lambda i: (0, 0))],
        out_specs=[pl.BlockSpec((tm, D), lambda i: (i, 0)), pl.BlockSpec((tm, 1), lambda i: (i, 0))],
        out_shape=[jax.ShapeDtypeStruct((L, D), bf16), jax.ShapeDtypeStruct((L, 1), f32)],
        name=name, compiler_params=_cparams(("parallel",)))(x, g)


def _rms_bwd(name, dxn, h, r, g, dres, after=()):
    L, D = h.shape
    tm = _tile(L, 256, 16)

    def body(dxn_ref, h_ref, r_ref, g_ref, dres_ref, *rest):
        dh_ref, dhb_ref, dg_ref = rest[len(after):]
        i = pl.program_id(0)
        d = dxn_ref[...].astype(f32)
        hv = h_ref[...]
        rv = r_ref[...]
        dyg = d * g_ref[...]
        m = jnp.mean(dyg * hv, axis=-1, keepdims=True)
        dh = dres_ref[...] + rv * dyg - hv * (rv * rv * rv) * m
        dh_ref[...] = dh
        dhb_ref[...] = dh.astype(bf16)

        @pl.when(i == 0)
        def _():
            dg_ref[...] = jnp.zeros_like(dg_ref)

        dg_ref[...] += jnp.sum(d * hv * rv, axis=0, keepdims=True)

    row = lambda i: (i, 0)
    return pl.pallas_call(
        body, grid=(L // tm,),
        in_specs=[pl.BlockSpec((tm, D), row), pl.BlockSpec((tm, D), row), pl.BlockSpec((tm, 1), row),
                  pl.BlockSpec((1, D), lambda i: (0, 0)), pl.BlockSpec((tm, D), row)] + [_ANY] * len(after),
        out_specs=[pl.BlockSpec((tm, D), row), pl.BlockSpec((tm, D), row), pl.BlockSpec((1, D), lambda i: (0, 0))],
        out_shape=[jax.ShapeDtypeStruct((L, D), f32), jax.ShapeDtypeStruct((L, D), bf16), jax.ShapeDtypeStruct((1, D), f32)],
        name=name, compiler_params=_cparams(("arbitrary",)))(dxn, h, r, g, dres, *after)


def _loss_head(name, h2, tgt, g):
    L, D = h2.shape
    tm = _tile(L, 256, 16)

    def body(h_ref, t_ref, g_ref, dh_ref, dhb_ref, dg_ref, loss_ref):
        i = pl.program_id(0)
        hv = h_ref[...]
        gv = g_ref[...]
        r = lax.rsqrt(jnp.mean(hv * hv, axis=-1, keepdims=True) + _EPS)
        err = hv * r * gv - t_ref[...]
        dy = err * (1.0 / D)
        dyg = dy * gv
        m = jnp.mean(dyg * hv, axis=-1, keepdims=True)
        dh = r * dyg - hv * (r * r * r) * m
        dh_ref[...] = dh
        dhb_ref[...] = dh.astype(bf16)

        @pl.when(i == 0)
        def _():
            dg_ref[...] = jnp.zeros_like(dg_ref)
            loss_ref[...] = jnp.zeros_like(loss_ref)

        dg_ref[...] += jnp.sum(dy * hv * r, axis=0, keepdims=True)
        part = jnp.sum(jnp.sum(err * err, axis=-1, keepdims=True), axis=0, keepdims=True) * (0.5 / D)
        loss_ref[...] += jnp.broadcast_to(part, (8, 128))

    row = lambda i: (i, 0)
    return pl.pallas_call(
        body, grid=(L // tm,),
        in_specs=[pl.BlockSpec((tm, D), row), pl.BlockSpec((tm, D), row), pl.BlockSpec((1, D), lambda i: (0, 0))],
        out_specs=[pl.BlockSpec((tm, D), row), pl.BlockSpec((tm, D), row), pl.BlockSpec((1, D), lambda i: (0, 0)),
                   pl.BlockSpec((8, 128), lambda i: (0, 0))],
        out_shape=[jax.ShapeDtypeStruct((L, D), f32), jax.ShapeDtypeStruct((L, D), bf16),
                   jax.ShapeDtypeStruct((1, D), f32), jax.ShapeDtypeStruct((8, 128), f32)],
        name=name, compiler_params=_cparams(("arbitrary",)))(h2, tgt, g)


def _shift_down(tile, halo, k, rows8):
    tm = tile.shape[0]
    r = pltpu.roll(tile, k, axis=0)
    hh = pltpu.roll(halo, k, axis=0)
    top = jnp.where(rows8 < k, hh, r[:8])
    return jnp.concatenate([top, r[8:]], axis=0) if tm > 8 else top


def _shift_up(tile, halo, k, rows8):
    tm = tile.shape[0]
    r = pltpu.roll(tile, tm - k, axis=0)
    hh = pltpu.roll(halo, 8 - k, axis=0)
    bot = jnp.where(rows8 >= 8 - k, hh, r[tm - 8:])
    return jnp.concatenate([r[:tm - 8], bot], axis=0) if tm > 8 else bot


def _conv3(x, halo, w_ref, b_ref, rows8):
    return (w_ref[0:1, :] * _shift_down(x, halo, 2, rows8) + w_ref[1:2, :] * _shift_down(x, halo, 1, rows8)
            + w_ref[2:3, :] * x + b_ref[...])


def _convb_fwd(cfg, proj, w, b):
    L, CW = cfg.L, cfg.CW
    assert cfg.SW == CW
    tm = _tile(L, 512, 16)

    def body(v_ref, vh_ref, gb_ref, gc_ref, gch_ref, w_ref, b_ref, o_ref):
        i = pl.program_id(0)
        rows8 = lax.broadcasted_iota(jnp.int32, (8, CW), 0)
        cv = gc_ref[...].astype(f32) * v_ref[...].astype(f32)
        cvh = gch_ref[...].astype(f32)[8:] * vh_ref[...].astype(f32)[8:]
        cvh = jnp.where(i == 0, 0.0, cvh)
        cc = _conv3(cv, cvh, w_ref, b_ref, rows8)
        o_ref[...] = (gb_ref[...].astype(f32) * cc).astype(bf16)

    blk = lambda col: pl.BlockSpec((tm, CW), lambda i: (i, col))
    halo = lambda col: pl.BlockSpec((16, CW), lambda i: (jnp.maximum(i * (tm // 16) - 1, 0), col))
    return pl.pallas_call(
        body, grid=(L // tm,),
        in_specs=[blk(1), halo(1), blk(2), blk(3), halo(3),
                  pl.BlockSpec((3, CW), lambda i: (0, 0)), pl.BlockSpec((1, CW), lambda i: (0, 0))],
        out_specs=pl.BlockSpec((tm, CW), lambda i: (i, 0)),
        out_shape=jax.ShapeDtypeStruct((L, CW), bf16),
        name="convb_fwd", compiler_params=_cparams(("parallel",)))(proj, proj, proj, proj, proj, w, b)


def _convb_bwd(cfg, proj, dyb0, w, b, dproj):
    L, CW = cfg.L, cfg.CW
    tm = _tile(L, 512, 16)
    nt = L // tm

    def body(v_ref, vh_ref, gb_ref, gbn_ref, gc_ref, gch_ref, d_ref, dn_ref, w_ref, b_ref, dproj_ref,
             o_ref, dw_ref, db_ref):
        i = pl.program_id(0)
        rows8 = lax.broadcasted_iota(jnp.int32, (8, CW), 0)
        v = v_ref[...].astype(f32)
        gb = gb_ref[...].astype(f32)
        gc = gc_ref[...].astype(f32)
        d = d_ref[...].astype(f32)
        cv = gc * v
        cvh = gch_ref[...].astype(f32)[8:] * vh_ref[...].astype(f32)[8:]
        cvh = jnp.where(i == 0, 0.0, cvh)
        s2 = _shift_down(cv, cvh, 2, rows8)
        s1 = _shift_down(cv, cvh, 1, rows8)
        cc = w_ref[0:1, :] * s2 + w_ref[1:2, :] * s1 + w_ref[2:3, :] * cv + b_ref[...]
        dcc = d * gb
        dccn = dn_ref[...].astype(f32)[:8] * gbn_ref[...].astype(f32)[:8]
        dccn = jnp.where(i == nt - 1, 0.0, dccn)
        dcv = (w_ref[2:3, :] * dcc + w_ref[1:2, :] * _shift_up(dcc, dccn, 1, rows8)
               + w_ref[0:1, :] * _shift_up(dcc, dccn, 2, rows8))
        o_ref[:, 0:CW] = (dcv * gc).astype(bf16)
        o_ref[:, CW:2 * CW] = (d * cc).astype(bf16)
        o_ref[:, 2 * CW:3 * CW] = (dcv * v).astype(bf16)

        @pl.when(i == 0)
        def _():
            dw_ref[...] = jnp.zeros_like(dw_ref)
            db_ref[...] = jnp.zeros_like(db_ref)

        dw_ref[0:1, :] += jnp.sum(dcc * s2, axis=0, keepdims=True)
        dw_ref[1:2, :] += jnp.sum(dcc * s1, axis=0, keepdims=True)
        dw_ref[2:3, :] += jnp.sum(dcc * cv, axis=0, keepdims=True)
        db_ref[...] += jnp.sum(dcc, axis=0, keepdims=True)

    blk = lambda col: pl.BlockSpec((tm, CW), lambda i: (i, col))
    prev = lambda col: pl.BlockSpec((16, CW), lambda i: (jnp.maximum(i * (tm // 16) - 1, 0), col))
    nxt = lambda col: pl.BlockSpec((16, CW), lambda i: (jnp.minimum((i + 1) * (tm // 16), L // 16 - 1), col))
    const = lambda r: pl.BlockSpec((r, CW), lambda i: (0, 0))
    return pl.pallas_call(
        body, grid=(nt,),
        in_specs=[blk(1), prev(1), blk(2), nxt(2), blk(3), prev(3), blk(0), nxt(0), const(3), const(1),
                  pl.BlockSpec(memory_space=pl.ANY)],
        out_specs=[pl.BlockSpec((tm, 3 * CW), lambda i: (i, 0)), const(3), const(1)],
        out_shape=[jax.ShapeDtypeStruct(dproj.shape, bf16), jax.ShapeDtypeStruct((3, CW), f32),
                   jax.ShapeDtypeStruct((1, CW), f32)],
        input_output_aliases={10: 0},
        name="convb_bwd", compiler_params=_cparams(("arbitrary",)))(proj, proj, proj, proj, proj, proj, dyb0, dyb0, w, b,
                                                                    dproj)


def _ffn_act(cfg, hh, w, b):
    L, F = cfg.L, cfg.F
    tm = _tile(L, 512, 16)
    tc = _tile(F, 1408, 128)
    ncb = F // tc

    def body(a_ref, ah_ref, g_ref, w_ref, b_ref, o_ref):
        i = pl.program_id(0)
        rows8 = lax.broadcasted_iota(jnp.int32, (8, tc), 0)
        a = a_ref[...].astype(f32)
        ah = jnp.where(i == 0, 0.0, ah_ref[...].astype(f32)[8:])
        o_ref[...] = (_gelu(_conv3(a, ah, w_ref, b_ref, rows8)) * g_ref[...].astype(f32)).astype(bf16)

    return pl.pallas_call(
        body, grid=(L // tm, ncb),
        in_specs=[pl.BlockSpec((tm, tc), lambda i, j: (i, j)),
                  pl.BlockSpec((16, tc), lambda i, j: (jnp.maximum(i * (tm // 16) - 1, 0), j)),
                  pl.BlockSpec((tm, tc), lambda i, j: (i, j + ncb)),
                  pl.BlockSpec((3, tc), lambda i, j: (0, j)), pl.BlockSpec((1, tc), lambda i, j: (0, j))],
        out_specs=pl.BlockSpec((tm, tc), lambda i, j: (i, j)),
        out_shape=jax.ShapeDtypeStruct((L, F), bf16),
        name="ffn_act", compiler_params=_cparams(("parallel", "parallel")))(hh, hh, hh, w, b)


def _ffn_act_bwd(cfg, hh, df, w, b):
    L, F = cfg.L, cfg.F
    tm = _tile(L, 512, 16)
    tc = _tile(F, 1408, 128)
    ncb = F // tc
    nt = L // tm

    def body(a_ref, ah_ref, an_ref, g_ref, gn_ref, d_ref, dn_ref, w_ref, b_ref, dhh_ref, dw_ref, db_ref):
        i = pl.program_id(1)
        rows8 = lax.broadcasted_iota(jnp.int32, (8, tc), 0)
        a = a_ref[...].astype(f32)
        ah = jnp.where(i == 0, 0.0, ah_ref[...].astype(f32)[8:])
        s2 = _shift_down(a, ah, 2, rows8)
        s1 = _shift_down(a, ah, 1, rows8)
        act = w_ref[0:1, :] * s2 + w_ref[1:2, :] * s1 + w_ref[2:3, :] * a + b_ref[...]
        d = d_ref[...].astype(f32)
        gelu, gelu_d = _gelu_and_grad(act)
        dhh_ref[1] = (d * gelu).astype(bf16)
        dact = d * g_ref[...].astype(f32) * gelu_d
        an = an_ref[...].astype(f32)[:8]
        actn = _conv3(an, a[tm - 8:], w_ref, b_ref, rows8)
        dactn = dn_ref[...].astype(f32)[:8] * gn_ref[...].astype(f32)[:8] * _gelu_grad(actn)
        dactn = jnp.where(i == nt - 1, 0.0, dactn)
        da = (w_ref[2:3, :] * dact + w_ref[1:2, :] * _shift_up(dact, dactn, 1, rows8)
              + w_ref[0:1, :] * _shift_up(dact, dactn, 2, rows8))
        dhh_ref[0] = da.astype(bf16)

        @pl.when(i == 0)
        def _():
            dw_ref[...] = jnp.zeros_like(dw_ref)
            db_ref[...] = jnp.zeros_like(db_ref)

        dw_ref[0:1, :] += jnp.sum(dact * s2, axis=0, keepdims=True)
        dw_ref[1:2, :] += jnp.sum(dact * s1, axis=0, keepdims=True)
        dw_ref[2:3, :] += jnp.sum(dact * a, axis=0, keepdims=True)
        db_ref[...] += jnp.sum(dact, axis=0, keepdims=True)

    blk = lambda off: pl.BlockSpec((tm, tc), lambda j, i: (i, j + off))
    prev = lambda off: pl.BlockSpec((16, tc), lambda j, i: (jnp.maximum(i * (tm // 16) - 1, 0), j + off))
    nxt = lambda off: pl.BlockSpec((16, tc), lambda j, i: (jnp.minimum((i + 1) * (tm // 16), L // 16 - 1), j + off))
    const = lambda r: pl.BlockSpec((r, tc), lambda j, i: (0, j))
    return pl.pallas_call(
        body, grid=(ncb, nt),
        in_specs=[blk(0), prev(0), nxt(0), blk(ncb), nxt(ncb), blk(0), nxt(0), const(3), const(1)],
        out_specs=[pl.BlockSpec((2, tm, tc), lambda j, i: (0, i, j)), const(3), const(1)],
        out_shape=[jax.ShapeDtypeStruct((2, L, F), bf16),
                   jax.ShapeDtypeStruct((3, F), f32), jax.ShapeDtypeStruct((1, F), f32)],
        name="ffn_act_bwd", compiler_params=_cparams(("parallel", "arbitrary")))(hh, hh, hh, hh, hh, df, df, w, b)


def _merge_fwd(cfg, ya1, yb0, proj, wso, wco):
    L, D, SW, CW = cfg.L, cfg.D, cfg.SW, cfg.CW
    Ns = D // _NCHIP
    tm = _tile(L, 1024, 16)
    tn = _tile(Ns, 512, 128)
    nb = Ns // tn
    off_a = (SW + 3 * CW) // tn
    off_b = (SW + 3 * CW + D) // tn

    def body(a_ref, b_ref, wa_ref, wb_ref, ma_ref, mb_ref, m_ref, ya_ref, yb_ref):
        ya = jnp.dot(a_ref[...], wa_ref[...], preferred_element_type=f32)
        yb = jnp.dot(b_ref[...], wb_ref[...], preferred_element_type=f32)
        sa = jax.nn.sigmoid(ma_ref[...].astype(f32))
        sb = jax.nn.sigmoid(mb_ref[...].astype(f32))
        m_ref[...] = (sa * ya + sb * yb).astype(bf16)
        ya_ref[...] = ya.astype(bf16)
        yb_ref[...] = yb.astype(bf16)

    o_spec = pl.BlockSpec((tm, tn), lambda i, j: (i, j))
    o_shape = jax.ShapeDtypeStruct((L, D), bf16)
    return pl.pallas_call(
        body, grid=(L // tm, D // tn),
        in_specs=[pl.BlockSpec((tm, SW), lambda i, j: (i, 0)), pl.BlockSpec((tm, CW), lambda i, j: (i, 0)),
                  pl.BlockSpec((None, SW, tn), lambda i, j: (j // nb, 0, j % nb)),
                  pl.BlockSpec((None, CW, tn), lambda i, j: (j // nb, 0, j % nb)),
                  pl.BlockSpec((tm, tn), lambda i, j: (i, off_a + j)), pl.BlockSpec((tm, tn), lambda i, j: (i, off_b + j))],
        out_specs=[o_spec, o_spec, o_spec], out_shape=[o_shape, o_shape, o_shape],
        name="merge_fwd", compiler_params=_cparams(("parallel", "parallel")))(ya1, yb0, wso, wco, proj, proj)


def _s5_dims(cfg):
    G = cfg.SW // _SSM_GROUP
    NS = G // _SLAB_GROUPS
    SC = _SLAB_GROUPS * _SSM_GROUP
    SH = _SLAB_GROUPS * _SSM_STATE
    NST = 2 * SH * NS
    return G, NS, SC, SH, NST


def _lane_slabs(cfg, W):
    _, NS, _, SH, _ = _s5_dims(cfg)
    return [(2 * SH * s + w0, 2 * SH * s + SH + w0) for s in range(NS) for w0 in range(0, SH, W)]


def _discretize(a_re, a_im, log_dt, b_re, b_im):
    dt = jnp.exp(log_dt)[:, None]
    mag = jnp.exp(dt * a_re)
    abr = mag * jnp.cos(dt * a_im)
    abi = mag * jnp.sin(dt * a_im)
    nr = abr - 1.0
    ni = abi
    den = a_re * a_re + a_im * a_im
    fr = (nr * a_re + ni * a_im) / den
    fi = (ni * a_re - nr * a_im) / den
    bbr = fr[..., None] * b_re - fi[..., None] * b_im
    bbi = fr[..., None] * b_im + fi[..., None] * b_re
    return abr, abi, bbr, bbi


def _state_rows(cfg, re, im):
    _, NS, _, SH, _ = _s5_dims(cfg)
    return jnp.concatenate([re.reshape(NS, SH), im.reshape(NS, SH)], axis=1).reshape(-1)


def _s5_tables(cfg, abr, abi, bbr, bbi, c_re, c_im):
    G, NS, SC, SH, NST = _s5_dims(cfg)
    S = cfg.T // 8
    eye = jnp.eye(_SLAB_GROUPS, dtype=f32)
    bb = jnp.stack([bbr, bbi]).reshape(2, NS, _SLAB_GROUPS, _SSM_STATE, _SSM_GROUP)
    bs = jnp.einsum("rsgph,gq->sghrqp", bb, eye).reshape(NS, SC, 2 * SH).astype(bf16)
    cc = jnp.stack([c_re, -c_im]).reshape(2, NS, _SLAB_GROUPS, _SSM_GROUP, _SSM_STATE)
    cs = jnp.einsum("rsghp,gq->srqpgh", cc, eye).reshape(NS, 2 * SH, SC).astype(bf16)
    arep = jnp.broadcast_to(_state_rows(cfg, abr, abi)[None, :], (8, NST))
    pr, pi = abr, abi
    for _ in range(S - 1):
        pr, pi = pr * abr - pi * abi, pr * abi + pi * abr
    apow = jnp.broadcast_to(_state_rows(cfg, pr, pi)[None, :], (8, NST))
    t = jnp.arange(cfg.T)
    perm = (t % 8) * S + t // 8
    pm = jax.nn.one_hot(perm, cfg.T, dtype=bf16)
    return bs, cs, arep, apow, pm, pm.T


def _cmul_add(ar, ai, xr, xi, br, bi):
    return ar * xr - ai * xi + br, ar * xi + ai * xr + bi


def _s5_forward_chunk(cfg, W, upb, bs_ref, arep_ref, apow_ref, st, x0, cin_store):
    _, NS, SC, SH, _ = _s5_dims(cfg)
    S = cfg.T // 8
    for s in range(NS):
        st[:, 2 * SH * s:2 * SH * (s + 1)] = jnp.dot(upb[:, SC * s:SC * (s + 1)], bs_ref[s], preferred_element_type=f32)
    rows = lax.broadcasted_iota(jnp.int32, (8, W), 0)
    zero = jnp.zeros((8, W), f32)
    for rc, ic in _lane_slabs(cfg, W):
        ar = arep_ref[:, rc:rc + W]
        ai = arep_ref[:, ic:ic + W]

        def step(i, carry, rc=rc, ic=ic, ar=ar, ai=ai):
            xr, xi = carry
            r0 = pl.multiple_of(i * 8, 8)
            nr, ni = _cmul_add(ar, ai, xr, xi, st[pl.ds(r0, 8), rc:rc + W], st[pl.ds(r0, 8), ic:ic + W])
            st[pl.ds(r0, 8), rc:rc + W] = nr
            st[pl.ds(r0, 8), ic:ic + W] = ni
            return nr, ni

        er, ei = lax.fori_loop(0, S, step, (zero, zero))
        pr = apow_ref[:, rc:rc + W]
        pi = apow_ref[:, ic:ic + W]
        x0r = x0[:, rc:rc + W]
        x0i = x0[:, ic:ic + W]
        cr = jnp.where(rows == 0, x0r, 0.0)
        ci = jnp.where(rows == 0, x0i, 0.0)
        for _ in range(7):
            fr, fi = _cmul_add(pr, pi, cr, ci, er, ei)
            cr = jnp.where(rows == 0, x0r, pltpu.roll(fr, 1, axis=0))
            ci = jnp.where(rows == 0, x0i, pltpu.roll(fi, 1, axis=0))
        fr, fi = _cmul_add(pr, pi, cr, ci, er, ei)
        x0[:, rc:rc + W] = jnp.broadcast_to(fr[7:8, :], (8, W))
        x0[:, ic:ic + W] = jnp.broadcast_to(fi[7:8, :], (8, W))
        cin_store(rc, ic, cr, ci)

        def fix(i, carry, rc=rc, ic=ic, ar=ar, ai=ai):
            kr, ki = carry
            r0 = pl.multiple_of(i * 8, 8)
            nr, ni = ar * kr - ai * ki, ar * ki + ai * kr
            st[pl.ds(r0, 8), rc:rc + W] = st[pl.ds(r0, 8), rc:rc + W] + nr
            st[pl.ds(r0, 8), ic:ic + W] = st[pl.ds(r0, 8), ic:ic + W] + ni
            return nr, ni

        lax.fori_loop(0, S, fix, (cr, ci))


def _s5_fwd(cfg, proj, tabs, dskip):
    L, SW, T = cfg.L, cfg.SW, cfg.T
    G, NS, SC, SH, NST = _s5_dims(cfg)
    bs, cs, arep, apow, pm, pmt = tabs
    W = min(512, SH)
    NC = L // T

    def body(u_ref, pm_ref, pmt_ref, bs_ref, cs_ref, arep_ref, apow_ref, dskip_ref, y_ref, ya0_ref, cin_ref, st, x0):
        c = pl.program_id(0)

        @pl.when(c == 0)
        def _():
            x0[...] = jnp.zeros_like(x0)

        up = jnp.dot(pm_ref[...], u_ref[...], preferred_element_type=f32)
        upb = up.astype(bf16)

        def cin_store(rc, ic, cr, ci):
            cin_ref[0, :, rc:rc + W] = cr
            cin_ref[0, :, ic:ic + W] = ci

        _s5_forward_chunk(cfg, W, upb, bs_ref, arep_ref, apow_ref, st, x0, cin_store)
        yp = jnp.concatenate(
            [jnp.dot(st[:, 2 * SH * s:2 * SH * (s + 1)].astype(bf16), cs_ref[s], preferred_element_type=f32)
             for s in range(NS)], axis=1) + dskip_ref[...] * up
        y = jnp.dot(pmt_ref[...], yp.astype(bf16), preferred_element_type=f32)
        y_ref[...] = y.astype(bf16)
        ya0_ref[...] = _gelu(y).astype(bf16)

    const2 = lambda shape: pl.BlockSpec(shape, lambda c: (0, 0))
    const3 = lambda shape: pl.BlockSpec(shape, lambda c: (0, 0, 0))
    return pl.pallas_call(
        body, grid=(NC,),
        in_specs=[pl.BlockSpec((T, SW), lambda c: (c, 0)), const2((T, T)), const2((T, T)), const3((NS, SC, 2 * SH)),
                  const3((NS, 2 * SH, SC)), const2((8, NST)), const2((8, NST)), const2((1, SW))],
        out_specs=[pl.BlockSpec((T, SW), lambda c: (c, 0)), pl.BlockSpec((T, SW), lambda c: (c, 0)),
                   pl.BlockSpec((1, 8, NST), lambda c: (c, 0, 0)), pl.BlockSpec((T, NST), lambda c: (c, 0))],
        out_shape=[jax.ShapeDtypeStruct((L, SW), bf16), jax.ShapeDtypeStruct((L, SW), bf16),
                   jax.ShapeDtypeStruct((NC, 8, NST), f32), jax.ShapeDtypeStruct((L, NST), f32)],
        scratch_shapes=[pltpu.VMEM((8, NST), f32)],
        name="s5_fwd", compiler_params=_cparams(("arbitrary",)))(proj, pm, pmt, bs, cs, arep, apow, dskip)


def _s5_bwd(cfg, proj, dy, cin, xs, tabs, dskip, dproj, after=()):
    L, SW, T = cfg.L, cfg.SW, cfg.T
    du_col = 3 * cfg.CW // SW
    G, NS, SC, SH, NST = _s5_dims(cfg)
    bs, cs, arep, apow, pm, pmt = tabs
    W = min(512, SH)
    S = T // 8
    NC = L // T

    def body(u_ref, dy_ref, cin_ref, st, pm_ref, pmt_ref, bs_ref, cs_ref, arep_ref, apow_ref, dskip_ref, dproj_ref, *rest):
        du_ref, da_ref, db_ref, dc_ref, dd_ref, gs, g0, db_acc, dc_acc = rest[len(after):]
        c = pl.program_id(0)

        @pl.when(c == 0)
        def _():
            g0[...] = jnp.zeros_like(g0)
            da_ref[...] = jnp.zeros_like(da_ref)
            dd_ref[...] = jnp.zeros_like(dd_ref)
            db_acc[...] = jnp.zeros_like(db_acc)
            dc_acc[...] = jnp.zeros_like(dc_acc)

        up = jnp.dot(pm_ref[...], u_ref[...], preferred_element_type=f32)
        upb = up.astype(bf16)
        dyp = jnp.dot(pm_ref[...], dy_ref[...], preferred_element_type=f32)
        dypb = dyp.astype(bf16)
        for s in range(NS):
            gs[:, 2 * SH * s:2 * SH * (s + 1)] = lax.dot_general(
                dypb[:, SC * s:SC * (s + 1)], cs_ref[s], (((1,), (1,)), ((), ())), preferred_element_type=f32)
        rows = lax.broadcasted_iota(jnp.int32, (8, W), 0)
        zero = jnp.zeros((8, W), f32)
        for rc, ic in _lane_slabs(cfg, W):
            ar = arep_ref[:, rc:rc + W]
            ai = arep_ref[:, ic:ic + W]

            def rstep(k, carry, rc=rc, ic=ic, ar=ar, ai=ai):
                gr, gi = carry
                r0 = pl.multiple_of((S - 1 - k) * 8, 8)
                nr = ar * gr + ai * gi + gs[pl.ds(r0, 8), rc:rc + W]
                ni = ar * gi - ai * gr + gs[pl.ds(r0, 8), ic:ic + W]
                gs[pl.ds(r0, 8), rc:rc + W] = nr
                gs[pl.ds(r0, 8), ic:ic + W] = ni
                return nr, ni

            er, ei = lax.fori_loop(0, S, rstep, (zero, zero))
            pr = apow_ref[:, rc:rc + W]
            pi = apow_ref[:, ic:ic + W]
            g0r = g0[:, rc:rc + W]
            g0i = g0[:, ic:ic + W]
            cr = jnp.where(rows == 7, g0r, 0.0)
            ci = jnp.where(rows == 7, g0i, 0.0)
            for _ in range(7):
                fr = er + pr * cr + pi * ci
                fi = ei + pr * ci - pi * cr
                cr = jnp.where(rows == 7, g0r, pltpu.roll(fr, 7, axis=0))
                ci = jnp.where(rows == 7, g0i, pltpu.roll(fi, 7, axis=0))
            fr = er + pr * cr + pi * ci
            fi = ei + pr * ci - pi * cr
            g0[:, rc:rc + W] = jnp.broadcast_to(fr[0:1, :], (8, W))
            g0[:, ic:ic + W] = jnp.broadcast_to(fi[0:1, :], (8, W))

            def fix(k, carry, rc=rc, ic=ic, ar=ar, ai=ai):
                kr, ki, accr, acci = carry
                i = S - 1 - k
                r0 = pl.multiple_of(i * 8, 8)
                rp = pl.multiple_of((i - 1) * 8, 8)
                nr = ar * kr + ai * ki
                ni = ar * ki - ai * kr
                gr = gs[pl.ds(r0, 8), rc:rc + W] + nr
                gi = gs[pl.ds(r0, 8), ic:ic + W] + ni
                gs[pl.ds(r0, 8), rc:rc + W] = gr
                gs[pl.ds(r0, 8), ic:ic + W] = gi
                xr = st[pl.ds(rp, 8), rc:rc + W]
                xi = st[pl.ds(rp, 8), ic:ic + W]
                return nr, ni, accr + gr * xr + gi * xi, acci + gi * xr - gr * xi

            kr, ki, accr, acci = lax.fori_loop(0, S - 1, fix, (cr, ci, zero, zero))
            nr = ar * kr + ai * ki
            ni = ar * ki - ai * kr
            gr = gs[0:8, rc:rc + W] + nr
            gi = gs[0:8, ic:ic + W] + ni
            gs[0:8, rc:rc + W] = gr
            gs[0:8, ic:ic + W] = gi
            xr = cin_ref[0, :, rc:rc + W]
            xi = cin_ref[0, :, ic:ic + W]
            da_ref[:, rc:rc + W] += accr + gr * xr + gi * xi
            da_ref[:, ic:ic + W] += acci + gi * xr - gr * xi

        dups = []
        for s in range(NS):
            gsb = gs[:, 2 * SH * s:2 * SH * (s + 1)].astype(bf16)
            dups.append(lax.dot_general(gsb, bs_ref[s], (((1,), (1,)), ((), ())), preferred_element_type=f32))
            db_acc[s] += lax.dot_general(upb[:, SC * s:SC * (s + 1)], gsb, (((0,), (0,)), ((), ())),
                                         preferred_element_type=f32)
            dc_acc[s] += lax.dot_general(st[:, 2 * SH * s:2 * SH * (s + 1)].astype(bf16), dypb[:, SC * s:SC * (s + 1)],
                                         (((0,), (0,)), ((), ())), preferred_element_type=f32)
        dup = jnp.concatenate(dups, axis=1) + dskip_ref[...] * dyp
        du_ref[...] = jnp.dot(pmt_ref[...], dup.astype(bf16), preferred_element_type=f32).astype(bf16)
        dd_ref[...] += jnp.sum(dyp * up, axis=0, keepdims=True)

        @pl.when(c == NC - 1)
        def _():
            PS, GH = _SSM_STATE, _SSM_GROUP
            mask_b = (lax.broadcasted_iota(jnp.int32, (SC, SH), 0) // GH
                      == lax.broadcasted_iota(jnp.int32, (SC, SH), 1) // PS)
            mask_c = (lax.broadcasted_iota(jnp.int32, (SH, SC), 0) // PS
                      == lax.broadcasted_iota(jnp.int32, (SH, SC), 1) // GH)
            for s in range(NS):
                for r in range(2):
                    xb = jnp.where(mask_b, db_acc[s, :, r * SH:(r + 1) * SH], 0.0)
                    zb = xb[:, 0:128]
                    for q in range(1, SH // 128):
                        zb = zb + xb[:, q * 128:(q + 1) * 128]
                    db_ref[s, r] = zb + pltpu.roll(zb, PS, axis=1)
                    xc = jnp.where(mask_c, dc_acc[s, r * SH:(r + 1) * SH, :], 0.0)
                    zc = xc[0:PS]
                    for q in range(1, SH // PS):
                        zc = zc + xc[q * PS:(q + 1) * PS]
                    dc_ref[s, r] = zc

    rev = lambda c: (NC - 1 - c, 0)
    const2 = lambda shape: pl.BlockSpec(shape, lambda c: (0, 0))
    const3 = lambda shape: pl.BlockSpec(shape, lambda c: (0, 0, 0))
    const4 = lambda shape: pl.BlockSpec(shape, lambda c: (0, 0, 0, 0))
    return pl.pallas_call(
        body, grid=(NC,),
        in_specs=[pl.BlockSpec((T, SW), rev), pl.BlockSpec((T, SW), rev), pl.BlockSpec((1, 8, NST), lambda c: (NC - 1 - c, 0, 0)),
                  pl.BlockSpec((T, NST), rev), const2((T, T)), const2((T, T)),
                  pl.BlockSpec((NS, SC, 2 * SH), lambda c: (0, 0, 0), pipeline_mode=pl.Buffered(1)),
                  pl.BlockSpec((NS, 2 * SH, SC), lambda c: (0, 0, 0), pipeline_mode=pl.Buffered(1)),
                  const2((8, NST)), const2((8, NST)), const2((1, SW)), _ANY] + [_ANY] * len(after),
        out_specs=[pl.BlockSpec((T, SW), lambda c: (NC - 1 - c, du_col)), const2((8, NST)),
                   const4((NS, 2, SC, 128)), const4((NS, 2, _SSM_STATE, SC)), const2((1, SW))],
        out_shape=[jax.ShapeDtypeStruct(dproj.shape, bf16), jax.ShapeDtypeStruct((8, NST), f32),
                   jax.ShapeDtypeStruct((NS, 2, SC, 128), f32), jax.ShapeDtypeStruct((NS, 2, _SSM_STATE, SC), f32),
                   jax.ShapeDtypeStruct((1, SW), f32)],
        scratch_shapes=[pltpu.VMEM((T, NST), f32), pltpu.VMEM((8, NST), f32),
                        pltpu.VMEM((NS, SC, 2 * SH), f32), pltpu.VMEM((NS, 2 * SH, SC), f32)],
        input_output_aliases={11: 0},
        name="s5_bwd", compiler_params=_cparams(("arbitrary",), _VMEM_S5_BWD))(proj, dy, cin, xs, pm, pmt, bs, cs, arep, apow, dskip, dproj,
                                                                  *after)


def _s5_param_grads(cfg, da, db_diag, dc_diag):
    G, NS, SC, SH, NST = _s5_dims(cfg)
    das = da.sum(axis=0).reshape(NS, 2, SH)
    dabr = das[:, 0].reshape(G, _SSM_STATE)
    dabi = das[:, 1].reshape(G, _SSM_STATE)
    dbd = db_diag[..., :_SSM_STATE].reshape(NS, 2, _SLAB_GROUPS, _SSM_GROUP, _SSM_STATE)
    dbb = dbd.transpose(1, 0, 2, 4, 3).reshape(2, G, _SSM_STATE, _SSM_GROUP)
    dcd = dc_diag.reshape(NS, 2, _SSM_STATE, _SLAB_GROUPS, _SSM_GROUP)
    dcc = dcd.transpose(1, 0, 3, 4, 2).reshape(2, G, _SSM_GROUP, _SSM_STATE)
    return dabr, dabi, dbb[0], dbb[1], dcc[0], -dcc[1]


def _coords():
    return lax.axis_index("x"), lax.axis_index("y"), lax.axis_index("c")


def _other_chips(x, y):
    return [(1 - x, y), (x, 1 - y), (1 - x, 1 - y)]


def _allreduce8(name, v):
    R = v.shape[0]

    def body(v_ref, o_ref, sib, chips, mine, ssem, rsem):
        x, y, c = _coords()
        d2d = pltpu.make_async_remote_copy(src_ref=v_ref, dst_ref=sib, send_sem=ssem.at[0], recv_sem=rsem.at[0],
                                           device_id=(x, y, 1 - c), device_id_type=_MESH)
        d2d.start()
        d2d.wait()
        mine[...] = v_ref[...] + sib[...]
        cps = [pltpu.make_async_remote_copy(src_ref=mine, dst_ref=chips.at[j], send_sem=ssem.at[1 + j],
                                            recv_sem=rsem.at[1 + j], device_id=(*chip, c), device_id_type=_MESH)
               for j, chip in enumerate(_other_chips(x, y))]
        for cp in cps:
            cp.start()
        for cp in cps:
            cp.wait()
        o_ref[...] = (mine[...] + chips[1]) + (chips[0] + chips[2])

    vm = pl.BlockSpec(memory_space=pltpu.VMEM)
    return pl.pallas_call(
        body, in_specs=[vm], out_specs=vm, out_shape=jax.ShapeDtypeStruct((R, 128), f32),
        scratch_shapes=[pltpu.VMEM((R, 128), f32), pltpu.VMEM((3, R, 128), f32), pltpu.VMEM((R, 128), f32),
                        pltpu.SemaphoreType.DMA((4,)), pltpu.SemaphoreType.DMA((4,))],
        name=name, compiler_params=pltpu.CompilerParams(vmem_limit_bytes=_VMEM_LIMIT))(v)


def _cast_into_slot(name, w, k_idx, also_alone=False):
    R, C = w.shape
    tr = _tile(R, 256, 16)

    def body(k_ref, w_ref, *o_refs):
        for o_ref in o_refs:
            o_ref[...] = w_ref[...].astype(bf16)

    slot = (jax.ShapeDtypeStruct((_NCHIP, R, C), bf16), pl.BlockSpec((None, tr, C), lambda r, kr: (kr[0], r, 0)))
    alone = (jax.ShapeDtypeStruct((R, C), bf16), pl.BlockSpec((tr, C), lambda r, kr: (r, 0)))
    outs = [slot, alone] if also_alone else [slot]
    gs = pltpu.PrefetchScalarGridSpec(
        num_scalar_prefetch=1, grid=(R // tr,),
        in_specs=[pl.BlockSpec((tr, C), lambda r, kr: (r, 0))], out_specs=[o[1] for o in outs])
    res = pl.pallas_call(body, grid_spec=gs, out_shape=[o[0] for o in outs], name=name,
                         compiler_params=_cparams(("parallel",)))(k_idx, w)
    return res if also_alone else res[0]


def _in_proj(cfg, xn1, w_own, w_in, k_idx):
    L, D = xn1.shape
    S, _, Ns = w_in.shape
    tm, tn = _tile(L, 1024, 16), _tile(Ns, 1024, 128)
    nb = Ns // tn

    def body(k_ref, a_ref, w_ref, *rest):
        rest[-1][...] = jnp.dot(a_ref[...], w_ref[...], preferred_element_type=f32).astype(bf16)

    a_spec = pl.BlockSpec((tm, D), lambda i, j, kr: (i, 0))
    shape = jax.ShapeDtypeStruct((L, S * Ns), bf16)
    own = pltpu.PrefetchScalarGridSpec(
        num_scalar_prefetch=1, grid=(L // tm, nb),
        in_specs=[a_spec, pl.BlockSpec((D, tn), lambda i, j, kr: (0, j))],
        out_specs=pl.BlockSpec((tm, tn), lambda i, j, kr: (i, kr[0] * nb + j)))
    proj = pl.pallas_call(body, grid_spec=own, out_shape=shape, name="in_proj_own",
                          compiler_params=_cparams(("parallel", "parallel")))(k_idx, xn1, w_own)
    shard = lambda j, kr: (kr[0] + 1 + j // nb) % S
    rest = pltpu.PrefetchScalarGridSpec(
        num_scalar_prefetch=1, grid=(L // tm, (S - 1) * nb),
        in_specs=[a_spec, pl.BlockSpec((None, D, tn), lambda i, j, kr: (shard(j, kr), 0, j % nb)), _ANY],
        out_specs=pl.BlockSpec((tm, tn), lambda i, j, kr: (i, shard(j, kr) * nb + j % nb)))
    return pl.pallas_call(body, grid_spec=rest, out_shape=shape, input_output_aliases={3: 0}, name="in_proj",
                          compiler_params=_cparams(("parallel", "parallel")))(k_idx, xn1, w_in, proj)


def _handshake(peers):
    barrier = pltpu.get_barrier_semaphore()
    for peer in peers:
        pl.semaphore_signal(barrier, inc=1, device_id=peer, device_id_type=_MESH)
    pl.semaphore_wait(barrier, len(peers))


def _allgather_weights(name, bufs, collective_id):
    n = len(bufs)
    refs = [jax.new_ref(b, memory_space=pltpu.MemorySpace.HBM) for b in bufs]

    def copy(ref, sems, idx, to):
        return pltpu.make_async_remote_copy(src_ref=ref, dst_ref=ref, send_sem=sems[0].at[idx], recv_sem=sems[1].at[idx],
                                            device_id=to, device_id_type=_MESH)

    def launch(ssem, rsem, qssem, qrsem, fssem, frsem):
        x, y, c = _coords()
        k = 2 * x + y
        nbrs = [(1 - x, y), (x, 1 - y)]
        across = 2 * (1 - x) + (1 - y)
        sibling = (x, y, 1 - c)
        _handshake([sibling] + [(*chip, c) for chip in nbrs])
        started = []

        def start(cp):
            cp.start()
            started.append(cp)

        for w in range(n):
            rh = refs[w].shape[1] // 2
            for j, chip in enumerate(nbrs):
                start(copy(refs[w].at[k, pl.ds(c * rh, rh)], (ssem, rsem), (w, j), (*chip, c)))
        for w in range(n):
            rh = refs[w].shape[1] // 2
            rq = rh // 2
            for j, (ox, oy) in enumerate(nbrs):
                ko = 2 * ox + oy
                landed = refs[w].at[ko, pl.ds(c * rh, rh)]
                copy(landed, (ssem, rsem), (w, j), (ox, oy, c)).wait_recv()
                start(copy(refs[w].at[ko, pl.ds(c * rh + j * rq, rq)], (qssem, qrsem), (w, j), (*nbrs[1 - j], c)))
                start(copy(landed, (fssem, frsem), (w, j), sibling))
        for w in range(n):
            rh = refs[w].shape[1] // 2
            rq = rh // 2
            for q in range(2):
                quarter = refs[w].at[across, pl.ds(c * rh + q * rq, rq)]
                copy(quarter, (qssem, qrsem), (w, q), (*nbrs[1 - q], c)).wait_recv()
            start(copy(refs[w].at[across, pl.ds(c * rh, rh)], (fssem, frsem), (w, 2), sibling))
        for w in range(n):
            rh = refs[w].shape[1] // 2
            for j, ko in enumerate([2 * nbrs[0][0] + nbrs[0][1], 2 * nbrs[1][0] + nbrs[1][1], across]):
                copy(refs[w].at[ko, pl.ds((1 - c) * rh, rh)], (fssem, frsem), (w, j), sibling).wait_recv()
        for cp in started:
            cp.wait_send()

    _sequencer_kernel(name, collective_id,
                      (pltpu.SemaphoreType.DMA((n, 2)), pltpu.SemaphoreType.DMA((n, 2)), pltpu.SemaphoreType.DMA((n, 2)),
                       pltpu.SemaphoreType.DMA((n, 2)), pltpu.SemaphoreType.DMA((n, 3)), pltpu.SemaphoreType.DMA((n, 3))),
                      launch)
    return [r[...] for r in refs]


def _sequencer_kernel(name, collective_id, sems, body):
    pl.kernel(body, mesh=plsc.ScalarSubcoreMesh(axis_name="seq", num_cores=1), name=name, scratch_types=sems,
              compiler_params=pltpu.CompilerParams(collective_id=collective_id))()


def _hbm_ref(a):
    return jax.new_ref(a, memory_space=pltpu.MemorySpace.HBM)


def _exchange_halves(name, grads, collective_id):
    n = len(grads)
    srcs = [_hbm_ref(g) for g in grads]
    dsts = [jax.empty_ref(jax.ShapeDtypeStruct((g.shape[0], g.shape[1] // 2, g.shape[2]), g.dtype),
                          memory_space=pltpu.MemorySpace.HBM) for g in grads]

    def body(ssem, rsem):
        x, y, c = _coords()
        _handshake([(x, y, 1 - c)])
        cps = []
        for w in range(n):
            rh = srcs[w].shape[1] // 2
            cp = pltpu.make_async_remote_copy(
                src_ref=srcs[w].at[:, pl.ds((1 - c) * rh, rh)], dst_ref=dsts[w], send_sem=ssem.at[w], recv_sem=rsem.at[w],
                device_id=(x, y, 1 - c), device_id_type=_MESH)
            cp.start()
            cps.append(cp)
        for cp in cps:
            cp.wait()

    _sequencer_kernel(name, collective_id, (pltpu.SemaphoreType.DMA((n,)), pltpu.SemaphoreType.DMA((n,))), body)
    return [d[...] for d in dsts]


def _scatter_shards(name, parts, collective_id):
    n = len(parts)
    srcs = [_hbm_ref(p) for p in parts]
    dsts = [jax.empty_ref(jax.ShapeDtypeStruct((3,) + p.shape[1:], p.dtype), memory_space=pltpu.MemorySpace.HBM)
            for p in parts]

    def body(ssem, rsem):
        x, y, c = _coords()
        others = _other_chips(x, y)
        _handshake([(*chip, c) for chip in others])
        cps = []
        for w in range(n):
            for j, (ox, oy) in enumerate(others):
                cp = pltpu.make_async_remote_copy(
                    src_ref=srcs[w].at[2 * ox + oy], dst_ref=dsts[w].at[j], send_sem=ssem.at[w, j], recv_sem=rsem.at[w, j],
                    device_id=(ox, oy, c), device_id_type=_MESH)
                cp.start()
                cps.append(cp)
        for cp in cps:
            cp.wait()

    _sequencer_kernel(name, collective_id, (pltpu.SemaphoreType.DMA((n, 3)), pltpu.SemaphoreType.DMA((n, 3))), body)
    return [d[...] for d in dsts]


def _join_halves(name, bufs, collective_id):
    n = len(bufs)
    refs = [_hbm_ref(b) for b in bufs]

    def body(ssem, rsem):
        x, y, c = _coords()
        _handshake([(x, y, 1 - c)])
        cps = []
        for w in range(n):
            rh = refs[w].shape[0] // 2
            mine = refs[w].at[pl.ds(c * rh, rh)]
            cp = pltpu.make_async_remote_copy(src_ref=mine, dst_ref=mine, send_sem=ssem.at[w], recv_sem=rsem.at[w],
                                              device_id=(x, y, 1 - c), device_id_type=_MESH)
            cp.start()
            cps.append(cp)
        for w in range(n):
            rh = refs[w].shape[0] // 2
            theirs = refs[w].at[pl.ds((1 - c) * rh, rh)]
            pltpu.make_async_remote_copy(src_ref=theirs, dst_ref=theirs, send_sem=ssem.at[w], recv_sem=rsem.at[w],
                                         device_id=(x, y, 1 - c), device_id_type=_MESH).wait_recv()
        for cp in cps:
            cp.wait_send()

    _sequencer_kernel(name, collective_id, (pltpu.SemaphoreType.DMA((n,)), pltpu.SemaphoreType.DMA((n,))), body)
    return [r[...] for r in refs]


def _allgather_small(name, v, collective_id):
    R = v.shape[0]
    src = _hbm_ref(v)
    dst = jax.empty_ref(jax.ShapeDtypeStruct((8, R, 128), v.dtype), memory_space=pltpu.MemorySpace.HBM)
    flips = [(dx, dy, dc) for dx in (0, 1) for dy in (0, 1) for dc in (0, 1)][1:]

    def body(lsem, ssem, rsem):
        x, y, c = _coords()
        me = 4 * x + 2 * y + c
        flip = lambda v, d: 1 - v if d else v
        peers = [(flip(x, dx), flip(y, dy), flip(c, dc)) for dx, dy, dc in flips]
        _handshake(peers)
        own = pltpu.make_async_copy(src, dst.at[me], lsem)
        own.start()
        cps = []
        for r, peer in enumerate(peers):
            cp = pltpu.make_async_remote_copy(src_ref=src, dst_ref=dst.at[me], send_sem=ssem.at[r], recv_sem=rsem.at[r],
                                              device_id=peer, device_id_type=_MESH)
            cp.start()
            cps.append(cp)
        for r, (px, py, pc) in enumerate(peers):
            theirs = dst.at[4 * px + 2 * py + pc]
            pltpu.make_async_remote_copy(src_ref=theirs, dst_ref=theirs, send_sem=ssem.at[r], recv_sem=rsem.at[r],
                                         device_id=(px, py, pc), device_id_type=_MESH).wait_recv()
        for cp in cps:
            cp.wait_send()
        own.wait()

    _sequencer_kernel(name, collective_id, (pltpu.SemaphoreType.DMA, pltpu.SemaphoreType.DMA((7,)),
                                            pltpu.SemaphoreType.DMA((7,))), body)
    return dst[...]


def _sum8(name, g8, after):
    R = g8.shape[1]
    tr = _tile(R, 512, 8)

    def body(g_ref, *rest):
        rest[-1][...] = (((g_ref[0] + g_ref[1]) + (g_ref[2] + g_ref[3]))
                         + ((g_ref[4] + g_ref[5]) + (g_ref[6] + g_ref[7])))

    return pl.pallas_call(body, grid=(R // tr,),
                          in_specs=[pl.BlockSpec((8, tr, 128), lambda i: (0, i, 0))] + [_ANY] * len(after),
                          out_specs=pl.BlockSpec((tr, 128), lambda i: (i, 0)), out_shape=jax.ShapeDtypeStruct((R, 128), f32),
                          name=name, compiler_params=_cparams(("parallel",)))(g8, *after)


def _add_own_half(name, g, t, c_idx, after):
    S, R, C = g.shape
    rh = R // 2
    tr = _tile(rh, 256, 16)
    nrb = rh // tr

    def body(c_ref, g_ref, t_ref, *rest):
        rest[-1][...] = (g_ref[...] + t_ref[...]).astype(bf16)

    gs = pltpu.PrefetchScalarGridSpec(
        num_scalar_prefetch=1, grid=(S, nrb),
        in_specs=[pl.BlockSpec((None, tr, C), lambda s, r, cr: (s, cr[0] * nrb + r, 0)),
                  pl.BlockSpec((None, tr, C), lambda s, r, cr: (s, r, 0))] + [_ANY] * len(after),
        out_specs=pl.BlockSpec((None, tr, C), lambda s, r, cr: (s, r, 0)))
    return pl.pallas_call(body, grid_spec=gs, out_shape=jax.ShapeDtypeStruct((S, rh, C), bf16), name=name,
                          compiler_params=_cparams(("parallel", "parallel")))(c_idx, g, t, *after)


def _add_shard_parts(name, g, t, r, kc_idx, after):
    S, R, C = g.shape
    rh = R // 2
    tr = _tile(rh, 256, 16)
    nrb = rh // tr

    def body(kc_ref, g_ref, t_ref, r_ref, *rest):
        own = g_ref[...] + t_ref[...]
        rest[-1][...] = (own + r_ref[1].astype(f32)) + (r_ref[0].astype(f32) + r_ref[2].astype(f32))

    gs = pltpu.PrefetchScalarGridSpec(
        num_scalar_prefetch=1, grid=(nrb,),
        in_specs=[pl.BlockSpec((None, tr, C), lambda i, kc: (kc[0], kc[1] * nrb + i, 0)),
                  pl.BlockSpec((None, tr, C), lambda i, kc: (kc[0], i, 0)),
                  pl.BlockSpec((3, tr, C), lambda i, kc: (0, i, 0))] + [_ANY] * len(after),
        out_specs=pl.BlockSpec((tr, C), lambda i, kc: (kc[1] * nrb + i, 0)))
    return pl.pallas_call(body, grid_spec=gs, out_shape=jax.ShapeDtypeStruct((R, C), f32), name=name,
                          compiler_params=_cparams(("parallel",)))(kc_idx, g, t, r, *after)


def _adamw_update(wv, gv, mv, vv):
    nm = _ADAM_B1 * mv + (1.0 - _ADAM_B1) * gv
    nv = _ADAM_B2 * vv + (1.0 - _ADAM_B2) * (gv * gv)
    m_hat = nm / (1.0 - _ADAM_B1 ** _ADAM_STEP)
    v_hat = nv / (1.0 - _ADAM_B2 ** _ADAM_STEP)
    return -_ADAM_LR * (m_hat / (jnp.sqrt(v_hat) + _ADAM_EPS) + _ADAM_WD * wv), nm, nv


def _adamw(name, w, g, m, v, after=()):
    R, C = w.shape
    tr = _tile(R, 128, 8)

    def body(w_ref, g_ref, m_ref, v_ref, *rest):
        go_ref, d_ref, nm_ref, nv_ref = rest[len(after):]
        gv = g_ref[...]
        go_ref[...] = gv
        d_ref[...], nm_ref[...], nv_ref[...] = _adamw_update(w_ref[...], gv, m_ref[...], v_ref[...])

    spec = pl.BlockSpec((tr, C), lambda i: (i, 0))
    shape = jax.ShapeDtypeStruct((R, C), f32)
    return pl.pallas_call(body, grid=(R // tr,), in_specs=[spec] * 4 + [_ANY] * len(after), out_specs=[spec] * 4,
                          out_shape=[shape] * 4, name=name, compiler_params=_cparams(("parallel",)))(w, g, m, v, *after)


def _adamw_whole(name, w, g, m, v):
    def body(w_ref, g_ref, m_ref, v_ref, d_ref, nm_ref, nv_ref):
        d_ref[...], nm_ref[...], nv_ref[...] = _adamw_update(w_ref[...], g_ref[...], m_ref[...], v_ref[...])

    vm = pl.BlockSpec(memory_space=pltpu.VMEM)
    return pl.pallas_call(body, in_specs=[vm] * 4, out_specs=[vm] * 3, out_shape=[jax.ShapeDtypeStruct(w.shape, f32)] * 3,
                          name=name, compiler_params=pltpu.CompilerParams(vmem_limit_bytes=_VMEM_LIMIT))(w, g, m, v)


def _pack(arrs):
    flat = jnp.concatenate([a.reshape(-1).astype(f32) for a in arrs])
    n = flat.shape[0]
    pad = (-n) % (128 * 128)
    return jnp.pad(flat, (0, pad)).reshape(-1, 128)


def _unpack(packed, shapes):
    flat = packed.reshape(-1)
    out, off = [], 0
    for s in shapes:
        n = math.prod(s)
        out.append(flat[off:off + n].reshape(s))
        off += n
    return out


_BIG = ("w_in", "w_glu", "w_ssm_out", "w_conv_out", "w_o", "w_up", "w_down")
_SMALL = ("norm_tok", "a_re", "a_im", "log_dt", "b_re", "b_im", "c_re", "c_im", "d_skip", "conv_w", "conv_b",
          "norm_ffn", "ffn_conv_w", "ffn_conv_b", "norm_final")
_WEIGHTS = ("norm_tok", "w_in", "a_re", "a_im", "log_dt", "b_re", "b_im", "c_re", "c_im", "d_skip", "w_glu",
            "w_ssm_out", "conv_w", "conv_b", "w_conv_out", "w_o", "norm_ffn", "w_up", "ffn_conv_w", "ffn_conv_b",
            "w_down", "norm_final")


def _step(cfg, x, tgt, p, m, v):
    L, D, SW, CW, F = cfg.L, cfg.D, cfg.SW, cfg.CW, cfg.F
    xi, yi, ci = _coords()
    k_idx = (2 * xi + yi).astype(jnp.int32).reshape(1)
    c_idx = ci.astype(jnp.int32).reshape(1)
    x = x.reshape(L, D)
    tgt = tgt.reshape(L, D)

    big2d = {n: p[n].reshape(p[n].shape[-2], p[n].shape[-1]) for n in _BIG}
    slots = {n: _cast_into_slot("cast_" + n, big2d[n], k_idx) for n in _BIG if n != "w_in"}
    slots["w_in"], w_in_own = _cast_into_slot("cast_w_in", big2d["w_in"], k_idx, also_alone=True)
    wg = {}
    for cid, (gname, group) in enumerate((("allgather_w_in", ("w_in",)),
                                          ("allgather_w_mixer", ("w_glu", "w_ssm_out", "w_conv_out", "w_o")),
                                          ("allgather_w_up", ("w_up",)), ("allgather_w_down", ("w_down",)))):
        wg.update(zip(group, _allgather_weights(gname, [slots[n] for n in group], cid)))
    w_in, w_so, w_co, w_up = wg["w_in"], wg["w_ssm_out"], wg["w_conv_out"], wg["w_up"]
    w_glu, w_o, w_down = wg["w_glu"], wg["w_o"], wg["w_down"]
    kk = k_idx[0]
    cw_full = lax.dynamic_update_slice(jnp.zeros((3, CW), f32), p["conv_w"].reshape(3, CW // _NCHIP), (0, kk * (CW // _NCHIP)))
    fw_full = lax.dynamic_update_slice(jnp.zeros((3, F), f32), p["ffn_conv_w"].reshape(3, F // _NCHIP), (0, kk * (F // _NCHIP)))
    south = (ci == 0).astype(f32)
    conv_w, ffn_conv_w = _unpack(_allreduce8("allgather_conv_filters", _pack([cw_full * south, fw_full * south])),
                                 [(3, CW), (3, F)])
    conv_b = p["conv_b"].reshape(1, CW)
    ffn_conv_b = p["ffn_conv_b"].reshape(1, F)
    norm_tok = p["norm_tok"].reshape(1, D)
    norm_ffn = p["norm_ffn"].reshape(1, D)
    norm_final = p["norm_final"].reshape(1, D)
    dskip = p["d_skip"].reshape(1, SW)

    s5_in = (p["a_re"][0], p["a_im"][0], p["log_dt"][0], p["b_re"][0], p["b_im"][0])
    (abr, abi, bbr, bbi), disc_vjp = jax.vjp(_discretize, *s5_in)
    tabs = _s5_tables(cfg, abr, abi, bbr, bbi, p["c_re"][0], p["c_im"][0])

    xn1, r1 = _rms_fwd("rms_tok", x, norm_tok)
    proj = _in_proj(cfg, xn1, w_in_own, w_in, k_idx)
    y_s, ya0, cin, xs = _s5_fwd(cfg, proj, tabs, dskip)

    def tiles(*arrs):
        return lambda tm, tn: [(a, pl.BlockSpec((tm, tn), lambda i, j: (i, j))) for a in arrs]

    def glu_epi(acc, e, o):
        o[0][...] = (e[0][...].astype(f32) * jax.nn.sigmoid(acc)).astype(bf16)
        o[1][...] = acc.astype(bf16)

    ya1, z = _mm_nn("glu", ya0, w_glu, [bf16, bf16], rows=True, extras_fn=tiles(ya0), epilogue=glu_epi)
    yb0 = _convb_fwd(cfg, proj, conv_w, conv_b)
    merged, ya, yb = _merge_fwd(cfg, ya1, yb0, proj, w_so, w_co)

    def res_epi(acc, e, o):
        o[0][...] = e[0][...] + acc

    h1 = _mm_nn("out_proj", merged, w_o, [f32], rows=True, extras_fn=tiles(x), epilogue=res_epi)[0]
    xn2, r2 = _rms_fwd("rms_ffn", h1, norm_ffn)
    hh = _mm_nn("ffn_up", xn2, w_up, [bf16], tn=1408)[0]
    fact = _ffn_act(cfg, hh, ffn_conv_w, ffn_conv_b)
    h2 = _mm_nn("ffn_down", fact, w_down, [f32], tm=512, rows=True, extras_fn=tiles(h1), epilogue=res_epi)[0]
    dh2, dh2b, g_norm_final, loss_tile = _loss_head("loss_head", h2, tgt, norm_final)

    kc_idx = jnp.concatenate([k_idx, c_idx])
    reduced, chains = {}, {}

    def rs_halves(tag, collective_id, names, gs):
        chains[tag] = dict(cid=collective_id, names=names, gs=gs,
                           sib=_exchange_halves("grad_halves_" + tag, gs, collective_id))

    def rs_shards(tag, after):
        ch = chains[tag]
        ch["parts"] = [_add_own_half("grad_add_halves_" + n, g, t, c_idx, after)
                       for n, g, t in zip(ch["names"], ch["gs"], ch["sib"])]
        ch["chips"] = _scatter_shards("grad_shards_" + tag, ch["parts"], ch["cid"] + 1)
        return ch["parts"]

    def rs_join(tag, after):
        ch = chains[tag]
        ch["halves"] = [_add_shard_parts("grad_add_chips_" + n, g, t, r, kc_idx, after)
                        for n, g, t, r in zip(ch["names"], ch["gs"], ch["sib"], ch["chips"])]
        reduced.update(zip(ch["names"], _join_halves("grad_join_" + tag, ch["halves"], ch["cid"] + 2)))
        return ch["halves"]

    df = _mm_nt("ffn_down_dx", dh2b, w_down, [bf16], tn=1408, rows=True)[0]
    g_w_down = _mm_tn("ffn_down_dw", fact, dh2b, tm=1408, tn=512)
    rs_halves("ffn_down", 4, ["w_down"], [g_w_down.reshape(_NCHIP, F // _NCHIP, D)])
    dhh, g_ffn_conv_w, g_ffn_conv_b = _ffn_act_bwd(cfg, hh, df, ffn_conv_w, ffn_conv_b)
    sent = rs_shards("ffn_down", [dhh])
    g_w_up = _mm_tn("ffn_up_dw", xn2, dhh, shards=_NCHIP, tm=512, tn=1408, b_resident=True, after=sent)
    rs_halves("ffn_up", 7, ["w_up"], [g_w_up])
    dxn2 = _ffn_up_dx(cfg, dhh, w_up)
    sent = rs_shards("ffn_up", [dxn2]) + rs_join("ffn_down", [dxn2])
    dh1, dh1b, g_norm_ffn = _rms_bwd("rms_ffn_bwd", dxn2, h1, r2, norm_ffn, dh2, after=sent)

    dya, dyb, dproj = _out_proj_dx(cfg, dh1b, w_o, ya, yb, proj)
    g_w_o = _mm_tn("out_proj_dw", merged, dh1b)

    def glu_bwd_epi(acc, e, o):
        a0 = e[0][...].astype(f32)
        s = jax.nn.sigmoid(e[1][...].astype(f32))
        o[0][...] = (acc * a0 * s * (1.0 - s)).astype(bf16)
        o[1][...] = (acc * s).astype(bf16)

    dz, t1 = _mm_nt("ssm_out_dx", dya, w_so, [bf16, bf16], extras_fn=tiles(ya0, z), epilogue=glu_bwd_epi)
    g_w_so = _mm_tn("ssm_out_dw", ya1, dya, shards=_NCHIP, tn=512)
    dyb0 = _mm_nt("conv_out_dx", dyb, w_co, [bf16])[0]
    g_w_co = _mm_tn("conv_out_dw", yb0, dyb, shards=_NCHIP, tn=512)
    dproj, g_conv_w, g_conv_b = _convb_bwd(cfg, proj, dyb0, conv_w, conv_b, dproj)

    def gelu_bwd_epi(acc, e, o):
        o[0][...] = ((e[0][...].astype(f32) + acc) * _gelu_grad(e[1][...].astype(f32))).astype(bf16)

    dy_s = _mm_nt("glu_dx", dz, w_glu, [bf16], rows=True, extras_fn=tiles(t1, y_s), epilogue=gelu_bwd_epi)[0]
    g_w_glu = _mm_tn("glu_dw", ya0, dz)
    rs_halves("mixer", 10, ["w_o", "w_ssm_out", "w_conv_out", "w_glu"],
              [g_w_o.reshape(_NCHIP, D // _NCHIP, D), g_w_so, g_w_co, g_w_glu.reshape(_NCHIP, SW // _NCHIP, SW)])
    sent = rs_join("ffn_up", [g_w_glu])
    dproj, da_acc, db_full, dc_full, g_dskip = _s5_bwd(cfg, proj, dy_s, cin, xs, tabs, dskip, dproj, after=sent)
    sent = rs_shards("mixer", [dproj])

    dabr, dabi, dbbr, dbbi, g_c_re, g_c_im = _s5_param_grads(cfg, da_acc, db_full, dc_full)
    g_a_re, g_a_im, g_log_dt, g_b_re, g_b_im = disc_vjp((dabr, dabi, dbbr, dbbi))
    small_g = {"a_re": g_a_re, "a_im": g_a_im, "log_dt": g_log_dt, "b_re": g_b_re, "b_im": g_b_im,
               "c_re": g_c_re, "c_im": g_c_im, "d_skip": g_dskip, "conv_w": g_conv_w, "conv_b": g_conv_b,
               "norm_ffn": g_norm_ffn, "ffn_conv_w": g_ffn_conv_w, "ffn_conv_b": g_ffn_conv_b, "norm_final": g_norm_final}
    early = [n for n in _SMALL if n != "norm_tok"]
    small8 = _allgather_small("allgather_small_grads", _pack([small_g[n] for n in early]), 16)

    g_w_in = _mm_tn("in_proj_dw", xn1, dproj, shards=_NCHIP, tn=CW, after=sent,
                    b_block=lambda j: jnp.where(j == 0, 3 * CW // SW, jnp.where(j < 4, j - 1, j)))
    rs_halves("in_proj", 13, ["w_in"], [g_w_in])
    dxn1 = _in_proj_dx(cfg, dproj, w_in)
    sent = rs_shards("in_proj", [dxn1]) + rs_join("mixer", [dxn1])
    dx, _, g_norm_tok = _rms_bwd("rms_tok_bwd", dxn1, x, r1, norm_tok, dh1, after=sent)

    summed = dict(zip(early, _unpack(_sum8("sum_small_grads", small8, [dx]), [small_g[n].shape for n in early])))
    summed["norm_tok"] = _unpack(_allreduce8("allreduce_norm_tok", _pack([g_norm_tok])), [g_norm_tok.shape])[0]
    summed["conv_w"] = lax.dynamic_slice(summed["conv_w"], (0, kk * (CW // _NCHIP)), (3, CW // _NCHIP))
    summed["ffn_conv_w"] = lax.dynamic_slice(summed["ffn_conv_w"], (0, kk * (F // _NCHIP)), (3, F // _NCHIP))

    grads, deltas, new_m, new_v = {}, {}, {}, {}

    def adamw_big(names, after):
        for n in names:
            g_, d_, m_, v_ = _adamw("adamw_" + n, big2d[n], reduced[n], m[n].reshape(big2d[n].shape),
                                    v[n].reshape(big2d[n].shape), after=after)
            grads[n], deltas[n], new_m[n], new_v[n] = (a.reshape(p[n].shape) for a in (g_, d_, m_, v_))
            after = [d_]
        return after

    for n in _SMALL:
        grads[n] = summed[n].reshape(p[n].shape)
        deltas[n], new_m[n], new_v[n] = _adamw_whole("adamw_" + n, p[n], grads[n], m[n], v[n])
    done = adamw_big(["w_down", "w_up", "w_o", "w_ssm_out", "w_conv_out", "w_glu"], [deltas["norm_final"]])
    rs_join("in_proj", done + [deltas[n] for n in _SMALL])
    adamw_big(["w_in"], ())

    loss = lax.psum(loss_tile[0, 0], ("x", "y", "c"))
    return (loss, dx.reshape(1, L, D), *[grads[n] for n in _WEIGHTS], *[deltas[n] for n in _WEIGHTS],
            *[new_m[n] for n in _WEIGHTS], *[new_v[n] for n in _WEIGHTS])


def kernel(x, norm_tok, w_in, a_re, a_im, log_dt, b_re, b_im, c_re, c_im, d_skip, w_glu, w_ssm_out, conv_w, conv_b, w_conv_out, w_o, norm_ffn, w_up, ffn_conv_w, ffn_conv_b, w_down, norm_final, loss_target, m_norm_tok, m_w_in, m_a_re, m_a_im, m_log_dt, m_b_re, m_b_im, m_c_re, m_c_im, m_d_skip, m_w_glu, m_w_ssm_out, m_conv_w, m_conv_b, m_w_conv_out, m_w_o, m_norm_ffn, m_w_up, m_ffn_conv_w, m_ffn_conv_b, m_w_down, m_norm_final, v_norm_tok, v_w_in, v_a_re, v_a_im, v_log_dt, v_b_re, v_b_im, v_c_re, v_c_im, v_d_skip, v_w_glu, v_w_ssm_out, v_conv_w, v_conv_b, v_w_conv_out, v_w_o, v_norm_ffn, v_w_up, v_ffn_conv_w, v_ffn_conv_b, v_w_down, v_norm_final):
    p = dict(norm_tok=norm_tok, w_in=w_in, a_re=a_re, a_im=a_im, log_dt=log_dt, b_re=b_re, b_im=b_im, c_re=c_re,
             c_im=c_im, d_skip=d_skip, w_glu=w_glu, w_ssm_out=w_ssm_out, conv_w=conv_w, conv_b=conv_b,
             w_conv_out=w_conv_out, w_o=w_o, norm_ffn=norm_ffn, w_up=w_up, ffn_conv_w=ffn_conv_w,
             ffn_conv_b=ffn_conv_b, w_down=w_down, norm_final=norm_final)
    m = dict(norm_tok=m_norm_tok, w_in=m_w_in, a_re=m_a_re, a_im=m_a_im, log_dt=m_log_dt, b_re=m_b_re, b_im=m_b_im,
             c_re=m_c_re, c_im=m_c_im, d_skip=m_d_skip, w_glu=m_w_glu, w_ssm_out=m_w_ssm_out, conv_w=m_conv_w,
             conv_b=m_conv_b, w_conv_out=m_w_conv_out, w_o=m_w_o, norm_ffn=m_norm_ffn, w_up=m_w_up,
             ffn_conv_w=m_ffn_conv_w, ffn_conv_b=m_ffn_conv_b, w_down=m_w_down, norm_final=m_norm_final)
    v = dict(norm_tok=v_norm_tok, w_in=v_w_in, a_re=v_a_re, a_im=v_a_im, log_dt=v_log_dt, b_re=v_b_re, b_im=v_b_im,
             c_re=v_c_re, c_im=v_c_im, d_skip=v_d_skip, w_glu=v_w_glu, w_ssm_out=v_w_ssm_out, conv_w=v_conv_w,
             conv_b=v_conv_b, w_conv_out=v_w_conv_out, w_o=v_w_o, norm_ffn=v_norm_ffn, w_up=v_w_up,
             ffn_conv_w=v_ffn_conv_w, ffn_conv_b=v_ffn_conv_b, w_down=v_w_down, norm_final=v_norm_final)
    return _step(_Cfg(), x, loss_target, p, m, v)
```

```python
import functools
import math
from typing import NamedTuple

import jax
import jax.numpy as jnp
from jax import lax
from jax.experimental import pallas as pl
from jax.experimental.pallas import tpu as pltpu
from jax.experimental.pallas import tpu_sc as plsc

f32 = jnp.float32
bf16 = jnp.bfloat16
_MESH = pl.DeviceIdType.MESH

_EPS = 1e-6
_ADAM_LR = 0.001
_ADAM_B1 = 0.9
_ADAM_B2 = 0.999
_ADAM_EPS = 1e-08
_ADAM_WD = 0.01
_ADAM_STEP = 10
_SSM_GROUP = 16
_SSM_STATE = 64
_SLAB_GROUPS = 16
_NCHIP = 4
_VMEM_LIMIT = 56 * 2**20
_VMEM_S5_BWD = 62 * 2**20
_GELU_C = math.sqrt(2.0 / math.pi)
_GELU_A = 0.044715


class _Cfg(NamedTuple):
    L: int = 4096
    D: int = 2048
    SW: int = 1024
    CW: int = 1024
    F: int = 5632
    T: int = 256


def _tile(n, pref, align):
    t = min(n, pref)
    t -= t % align
    while t > align and n % t:
        t -= align
    assert t > 0 and n % t == 0, (n, pref, align)
    return t


def _cparams(sem, vmem_limit=_VMEM_LIMIT):
    return pltpu.CompilerParams(dimension_semantics=sem, vmem_limit_bytes=vmem_limit)


def _gelu(x):
    return _gelu_and_grad(x)[0]


def _gelu_grad(x):
    return _gelu_and_grad(x)[1]


def _gelu_and_grad(x):
    x2 = x * x
    th = jnp.tanh(x * (_GELU_C + (_GELU_C * _GELU_A) * x2))
    half = 0.5 + 0.5 * th
    return x * half, half + (0.5 * x) * (1.0 - th * th) * (_GELU_C + (3.0 * _GELU_C * _GELU_A) * x2)


_NN = (((1,), (0,)), ((), ()))
_NT = (((1,), (1,)), ((), ()))
_TN = (((0,), (0,)), ((), ()))


def _whole(ref):
    return ref[...]


_ANY = pl.BlockSpec(memory_space=pl.ANY)


def _mm(name, operands, steps, *, grid, contract, outs, extras=(), epilogue=None, acc_shape=None, after=()):
    nop, ne, na = len(operands), len(extras), len(after)
    nk = len(steps)

    def body(*refs):
        op_refs = refs[:nop]
        e_refs = refs[nop:nop + ne]
        o_refs = refs[nop + ne + na:nop + ne + na + len(outs)]

        def partial(terms):
            tot = None
            for ai, av, bi, bv in terms:
                d = lax.dot_general(av(op_refs[ai]), bv(op_refs[bi]), contract, preferred_element_type=f32)
                tot = d if tot is None else tot + d
            return tot

        def finish(res):
            if epilogue is None:
                for o in o_refs:
                    o[...] = res.astype(o.dtype)
            else:
                epilogue(res, e_refs, o_refs)

        if nk == 1:
            finish(partial(steps[0][1]))
            return
        acc = refs[-1]
        kid = pl.program_id(len(grid) - 1)
        for k, terms in steps:
            def run(k=k, terms=terms):
                d = partial(terms)
                if k == 0:
                    acc[...] = d
                elif k < nk - 1:
                    acc[...] += d
                else:
                    finish(acc[...] + d)

            pl.when(kid == k)(run)

    sem = ("parallel",) * (len(grid) - (nk > 1)) + (("arbitrary",) if nk > 1 else ())
    return pl.pallas_call(
        body, grid=grid, in_specs=[o[1] for o in operands] + [e[1] for e in extras] + [_ANY] * na,
        out_specs=[o[1] for o in outs], out_shape=[o[0] for o in outs],
        scratch_shapes=[pltpu.VMEM(acc_shape, f32)] if nk > 1 else [], name=name,
        compiler_params=_cparams(sem))(*[o[0] for o in operands], *[e[0] for e in extras], *after)


def _mm_nn(name, a, w, out_dtypes, *, tm=1024, tn=1024, rows=False, extras_fn=None, epilogue=None):
    M, K = a.shape
    S, Ns = w.shape[0], w.shape[-1]
    N = Ns if rows else Ns * S
    tm, tn = _tile(M, tm, 16), _tile(Ns, tn, 128)
    nb = Ns // tn
    a_spec = pl.BlockSpec((tm, K), lambda i, j: (i, 0))
    if rows:
        b_spec = pl.BlockSpec((S, K // S, tn), lambda i, j: (0, 0, j))
        b_view = lambda r: r[...].reshape(K, tn)
    else:
        b_spec = pl.BlockSpec((None, K, tn), lambda i, j: (j // nb, 0, j % nb))
        b_view = _whole
    o_spec = pl.BlockSpec((tm, tn), lambda i, j: (i, j))
    outs = [(jax.ShapeDtypeStruct((M, N), dt), o_spec) for dt in out_dtypes]
    extras = extras_fn(tm, tn) if extras_fn is not None else ()
    return _mm(name, [(a, a_spec), (w, b_spec)], [(None, [(0, _whole, 1, b_view)])], grid=(M // tm, N // tn),
               contract=_NN, outs=outs, extras=extras, epilogue=epilogue)


def _mm_nt(name, a, w, out_dtypes, *, tm=1024, tn=1024, rows=False, extras_fn=None, epilogue=None):
    M, N = a.shape
    S, Ks, Ns = w.shape
    K = Ks * S if rows else Ks
    tm = _tile(M, tm, 16)
    a_spec = pl.BlockSpec((tm, N), lambda i, j: (i, 0))
    if rows:
        tn = K if tn >= K else _tile(Ks, tn, 128)
        if tn == K:
            b_spec = pl.BlockSpec((S, Ks, N), lambda i, j: (0, 0, 0))
            terms = [(0, _whole, 1, lambda r: r[...].reshape(K, N))]
        else:
            nbs = Ks // tn
            b_spec = pl.BlockSpec((None, tn, N), lambda i, j: (j // nbs, j % nbs, 0))
            terms = [(0, _whole, 1, _whole)]
    else:
        tn = _tile(K, tn, 128)
        assert S * Ns == N
        b_spec = pl.BlockSpec((S, tn, Ns), lambda i, j: (0, j, 0))
        terms = [(0, lambda r, s=s: r[:, s * Ns:(s + 1) * Ns], 1, lambda r, s=s: r[s]) for s in range(S)]
    o_spec = pl.BlockSpec((tm, tn), lambda i, j: (i, j))
    outs = [(jax.ShapeDtypeStruct((M, K), dt), o_spec) for dt in out_dtypes]
    extras = extras_fn(tm, tn) if extras_fn is not None else ()
    return _mm(name, [(a, a_spec), (w, b_spec)], [(None, terms)], grid=(M // tm, K // tn), contract=_NT,
               outs=outs, extras=extras, epilogue=epilogue)


def _mm_tn(name, a, b, *, shards=None, tm=1024, tn=1024, b_block=None, b_resident=False, after=()):
    M, K = a.shape
    halves = b.shape[0] if b.ndim == 3 else 1
    Nh = b.shape[-1]
    N = Nh * halves
    Ns = N // shards if shards else N
    tm, tn = _tile(K, tm, 128), _tile(math.gcd(Ns, Nh), tn, 128)
    nb, nbh = Ns // tn, Nh // tn
    ij = (lambda g0, g1: (g1, g0)) if b_resident else (lambda g0, g1: (g0, g1))
    bmap = b_block if b_block is not None else (lambda j: j)
    a_spec = pl.BlockSpec((M, tm), lambda g0, g1: (0, ij(g0, g1)[0]))
    if halves > 1:
        b_spec = pl.BlockSpec((None, M, tn), lambda g0, g1: (bmap(ij(g0, g1)[1]) // nbh, 0, bmap(ij(g0, g1)[1]) % nbh))
    else:
        b_spec = pl.BlockSpec((M, tn), lambda g0, g1: (0, bmap(ij(g0, g1)[1])))
    if shards:
        out = (jax.ShapeDtypeStruct((shards, K, Ns), f32),
               pl.BlockSpec((None, tm, tn), lambda g0, g1: (ij(g0, g1)[1] // nb, ij(g0, g1)[0], ij(g0, g1)[1] % nb)))
    else:
        out = (jax.ShapeDtypeStruct((K, N), f32), pl.BlockSpec((tm, tn), lambda g0, g1: ij(g0, g1)))
    grid = (N // tn, K // tm) if b_resident else (K // tm, N // tn)
    return _mm(name, [(a, a_spec), (b, b_spec)], [(None, [(0, _whole, 1, _whole)])], grid=grid, contract=_TN,
               outs=[out], after=after)[0]


def _in_proj_dx(cfg, dproj, w_in):
    L, D, SW, CW = cfg.L, cfg.D, cfg.SW, cfg.CW
    NP = SW + 3 * CW + 2 * D
    Ns = NP // _NCHIP
    assert SW + CW == Ns and 2 * CW == Ns and D == Ns
    tm, tn = _tile(L, 1024, 16), _tile(D, 1024, 128)
    a_spec = pl.BlockSpec((tm, NP // 2), lambda i, j, k: (i, k))
    b_spec = pl.BlockSpec((2, tn, Ns), lambda i, j, k: (k, j, 0))
    first = [(0, lambda r: r[:, 0:CW], 1, lambda r: r[0, :, SW:SW + CW]),
             (0, lambda r: r[:, CW:3 * CW], 1, lambda r: r[1]),
             (0, lambda r: r[:, 3 * CW:3 * CW + SW], 1, lambda r: r[0, :, 0:SW])]
    second = [(0, lambda r: r[:, 0:D], 1, lambda r: r[0]), (0, lambda r: r[:, D:2 * D], 1, lambda r: r[1])]
    out = (jax.ShapeDtypeStruct((L, D), bf16), pl.BlockSpec((tm, tn), lambda i, j, k: (i, j)))
    return _mm("in_proj_dx", [(dproj, a_spec), (w_in, b_spec)], [(0, first), (1, second)], grid=(L // tm, D // tn, 2),
               contract=_NT, outs=[out], acc_shape=(tm, tn))[0]


def _out_proj_dx(cfg, dh1b, w_o, ya, yb, proj):
    L, D = cfg.L, cfg.D
    NP = cfg.SW + 3 * cfg.CW + 2 * D
    assert NP == 4 * D
    tm = _tile(L, 512, 16)

    def epilogue(acc, e, o):
        sa = jax.nn.sigmoid(e[2][:, 0:D].astype(f32))
        sb = jax.nn.sigmoid(e[2][:, D:2 * D].astype(f32))
        o[0][...] = (acc * sa).astype(bf16)
        o[1][...] = (acc * sb).astype(bf16)
        o[2][:, 0:D] = (acc * e[0][...].astype(f32) * sa * (1.0 - sa)).astype(bf16)
        o[2][:, D:2 * D] = (acc * e[1][...].astype(f32) * sb * (1.0 - sb)).astype(bf16)

    row = pl.BlockSpec((tm, D), lambda i, j: (i, 0))
    half = pl.BlockSpec((tm, 2 * D), lambda i, j: (i, 1))
    return _mm("out_proj_dx", [(dh1b, row), (w_o, pl.BlockSpec(w_o.shape, lambda i, j: (0, 0, 0),
                                                                pipeline_mode=pl.Buffered(1)))],
               [(None, [(0, _whole, 1, lambda r: r[...].reshape(D, D))])], grid=(L // tm, 1), contract=_NT,
               outs=[(jax.ShapeDtypeStruct((L, D), bf16), row), (jax.ShapeDtypeStruct((L, D), bf16), row),
                     (jax.ShapeDtypeStruct((L, NP), bf16), half)],
               extras=[(ya, row), (yb, row), (proj, half)], epilogue=epilogue)


def _ffn_up_dx(cfg, dhh, w_up):
    L, D, F = cfg.L, cfg.D, cfg.F
    Fh = F // 2
    tm, tn = _tile(L, 1024, 16), _tile(D, 512, 128)
    a_spec = pl.BlockSpec((None, tm, F), lambda i, j, k: (k, i, 0))
    b_spec = pl.BlockSpec((2, tn, Fh), lambda i, j, k: (k, j, 0))
    terms = [(0, lambda r: r[:, 0:Fh], 1, lambda r: r[0]), (0, lambda r: r[:, Fh:F], 1, lambda r: r[1])]
    out = (jax.ShapeDtypeStruct((L, D), bf16), pl.BlockSpec((tm, tn), lambda i, j, k: (i, j)))
    return _mm("ffn_up_dx", [(dhh, a_spec), (w_up, b_spec)], [(0, terms), (1, terms)], grid=(L // tm, D // tn, 2),
               contract=_NT, outs=[out], acc_shape=(tm, tn))[0]


def _rms_fwd(name, x, g):
    L, D = x.shape
    tm = _tile(L, 256, 16)

    def body(x_ref, g_ref, xn_ref, r_ref):
        xv = x_ref[...]
        r = lax.rsqrt(jnp.mean(xv * xv, axis=-1, keepdims=True) + _EPS)
        xn_ref[...] = (xv * r * g_ref[...]).astype(bf16)
        r_ref[...] = r

    return pl.pallas_call(
        body, grid=(L // tm,),
        in_specs=[pl.BlockSpec((tm, D), lambda i: (i, 0)), pl.BlockSpec((1, D), lambda i: (0, 0))],
        out_specs=[pl.BlockSpec((tm, D), lambda i: (i, 0)), pl.BlockSpec((tm, 1), lambda i: (i, 0))],
        out_shape=[jax.ShapeDtypeStruct((L, D), bf16), jax.ShapeDtypeStruct((L, 1), f32)],
        name=name, compiler_params=_cparams(("parallel",)))(x, g)


def _rms_bwd(name, dxn, h, r, g, dres, after=()):
    L, D = h.shape
    tm = _tile(L, 256, 16)

    def body(dxn_ref, h_ref, r_ref, g_ref, dres_ref, *rest):
        dh_ref, dhb_ref, dg_ref = rest[len(after):]
        i = pl.program_id(0)
        d = dxn_ref[...].astype(f32)
        hv = h_ref[...]
        rv = r_ref[...]
        dyg = d * g_ref[...]
        m = jnp.mean(dyg * hv, axis=-1, keepdims=True)
        dh = dres_ref[...] + rv * dyg - hv * (rv * rv * rv) * m
        dh_ref[...] = dh
        dhb_ref[...] = dh.astype(bf16)

        @pl.when(i == 0)
        def _():
            dg_ref[...] = jnp.zeros_like(dg_ref)

        dg_ref[...] += jnp.sum(d * hv * rv, axis=0, keepdims=True)

    row = lambda i: (i, 0)
    return pl.pallas_call(
        body, grid=(L // tm,),
        in_specs=[pl.BlockSpec((tm, D), row), pl.BlockSpec((tm, D), row), pl.BlockSpec((tm, 1), row),
                  pl.BlockSpec((1, D), lambda i: (0, 0)), pl.BlockSpec((tm, D), row)] + [_ANY] * len(after),
        out_specs=[pl.BlockSpec((tm, D), row), pl.BlockSpec((tm, D), row), pl.BlockSpec((1, D), lambda i: (0, 0))],
        out_shape=[jax.ShapeDtypeStruct((L, D), f32), jax.ShapeDtypeStruct((L, D), bf16), jax.ShapeDtypeStruct((1, D), f32)],
        name=name, compiler_params=_cparams(("arbitrary",)))(dxn, h, r, g, dres, *after)


def _loss_head(name, h2, tgt, g):
    L, D = h2.shape
    tm = _tile(L, 256, 16)

    def body(h_ref, t_ref, g_ref, dh_ref, dhb_ref, dg_ref, loss_ref):
        i = pl.program_id(0)
        hv = h_ref[...]
        gv = g_ref[...]
        r = lax.rsqrt(jnp.mean(hv * hv, axis=-1, keepdims=True) + _EPS)
        err = hv * r * gv - t_ref[...]
        dy = err * (1.0 / D)
        dyg = dy * gv
        m = jnp.mean(dyg * hv, axis=-1, keepdims=True)
        dh = r * dyg - hv * (r * r * r) * m
        dh_ref[...] = dh
        dhb_ref[...] = dh.astype(bf16)

        @pl.when(i == 0)
        def _():
            dg_ref[...] = jnp.zeros_like(dg_ref)
            loss_ref[...] = jnp.zeros_like(loss_ref)

        dg_ref[...] += jnp.sum(dy * hv * r, axis=0, keepdims=True)
        part = jnp.sum(jnp.sum(err * err, axis=-1, keepdims=True), axis=0, keepdims=True) * (0.5 / D)
        loss_ref[...] += jnp.broadcast_to(part, (8, 128))

    row = lambda i: (i, 0)
    return pl.pallas_call(
        body, grid=(L // tm,),
        in_specs=[pl.BlockSpec((tm, D), row), pl.BlockSpec((tm, D), row), pl.BlockSpec((1, D), lambda i: (0, 0))],
        out_specs=[pl.BlockSpec((tm, D), row), pl.BlockSpec((tm, D), row), pl.BlockSpec((1, D), lambda i: (0, 0)),
                   pl.BlockSpec((8, 128), lambda i: (0, 0))],
        out_shape=[jax.ShapeDtypeStruct((L, D), f32), jax.ShapeDtypeStruct((L, D), bf16),
                   jax.ShapeDtypeStruct((1, D), f32), jax.ShapeDtypeStruct((8, 128), f32)],
        name=name, compiler_params=_cparams(("arbitrary",)))(h2, tgt, g)


def _shift_down(tile, halo, k, rows8):
    tm = tile.shape[0]
    r = pltpu.roll(tile, k, axis=0)
    hh = pltpu.roll(halo, k, axis=0)
    top = jnp.where(rows8 < k, hh, r[:8])
    return jnp.concatenate([top, r[8:]], axis=0) if tm > 8 else top


def _shift_up(tile, halo, k, rows8):
    tm = tile.shape[0]
    r = pltpu.roll(tile, tm - k, axis=0)
    hh = pltpu.roll(halo, 8 - k, axis=0)
    bot = jnp.where(rows8 >= 8 - k, hh, r[tm - 8:])
    return jnp.concatenate([r[:tm - 8], bot], axis=0) if tm > 8 else bot


def _conv3(x, halo, w_ref, b_ref, rows8):
    return (w_ref[0:1, :] * _shift_down(x, halo, 2, rows8) + w_ref[1:2, :] * _shift_down(x, halo, 1, rows8)
            + w_ref[2:3, :] * x + b_ref[...])


def _convb_fwd(cfg, proj, w, b):
    L, CW = cfg.L, cfg.CW
    assert cfg.SW == CW
    tm = _tile(L, 512, 16)

    def body(v_ref, vh_ref, gb_ref, gc_ref, gch_ref, w_ref, b_ref, o_ref):
        i = pl.program_id(0)
        rows8 = lax.broadcasted_iota(jnp.int32, (8, CW), 0)
        cv = gc_ref[...].astype(f32) * v_ref[...].astype(f32)
        cvh = gch_ref[...].astype(f32)[8:] * vh_ref[...].astype(f32)[8:]
        cvh = jnp.where(i == 0, 0.0, cvh)
        cc = _conv3(cv, cvh, w_ref, b_ref, rows8)
        o_ref[...] = (gb_ref[...].astype(f32) * cc).astype(bf16)

    blk = lambda col: pl.BlockSpec((tm, CW), lambda i: (i, col))
    halo = lambda col: pl.BlockSpec((16, CW), lambda i: (jnp.maximum(i * (tm // 16) - 1, 0), col))
    return pl.pallas_call(
        body, grid=(L // tm,),
        in_specs=[blk(1), halo(1), blk(2), blk(3), halo(3),
                  pl.BlockSpec((3, CW), lambda i: (0, 0)), pl.BlockSpec((1, CW), lambda i: (0, 0))],
        out_specs=pl.BlockSpec((tm, CW), lambda i: (i, 0)),
        out_shape=jax.ShapeDtypeStruct((L, CW), bf16),
        name="convb_fwd", compiler_params=_cparams(("parallel",)))(proj, proj, proj, proj, proj, w, b)


def _convb_bwd(cfg, proj, dyb0, w, b, dproj):
    L, CW = cfg.L, cfg.CW
    tm = _tile(L, 512, 16)
    nt = L // tm

    def body(v_ref, vh_ref, gb_ref, gbn_ref, gc_ref, gch_ref, d_ref, dn_ref, w_ref, b_ref, dproj_ref,
             o_ref, dw_ref, db_ref):
        i = pl.program_id(0)
        rows8 = lax.broadcasted_iota(jnp.int32, (8, CW), 0)
        v = v_ref[...].astype(f32)
        gb = gb_ref[...].astype(f32)
        gc = gc_ref[...].astype(f32)
        d = d_ref[...].astype(f32)
        cv = gc * v
        cvh = gch_ref[...].astype(f32)[8:] * vh_ref[...].astype(f32)[8:]
        cvh = jnp.where(i == 0, 0.0, cvh)
        s2 = _shift_down(cv, cvh, 2, rows8)
        s1 = _shift_down(cv, cvh, 1, rows8)
        cc = w_ref[0:1, :] * s2 + w_ref[1:2, :] * s1 + w_ref[2:3, :] * cv + b_ref[...]
        dcc = d * gb
        dccn = dn_ref[...].astype(f32)[:8] * gbn_ref[...].astype(f32)[:8]
        dccn = jnp.where(i == nt - 1, 0.0, dccn)
        dcv = (w_ref[2:3, :] * dcc + w_ref[1:2, :] * _shift_up(dcc, dccn, 1, rows8)
               + w_ref[0:1, :] * _shift_up(dcc, dccn, 2, rows8))
        o_ref[:, 0:CW] = (dcv * gc).astype(bf16)
        o_ref[:, CW:2 * CW] = (d * cc).astype(bf16)
        o_ref[:, 2 * CW:3 * CW] = (dcv * v).astype(bf16)

        @pl.when(i == 0)
        def _():
            dw_ref[...] = jnp.zeros_like(dw_ref)
            db_ref[...] = jnp.zeros_like(db_ref)

        dw_ref[0:1, :] += jnp.sum(dcc * s2, axis=0, keepdims=True)
        dw_ref[1:2, :] += jnp.sum(dcc * s1, axis=0, keepdims=True)
        dw_ref[2:3, :] += jnp.sum(dcc * cv, axis=0, keepdims=True)
        db_ref[...] += jnp.sum(dcc, axis=0, keepdims=True)

    blk = lambda col: pl.BlockSpec((tm, CW), lambda i: (i, col))
    prev = lambda col: pl.BlockSpec((16, CW), lambda i: (jnp.maximum(i * (tm // 16) - 1, 0), col))
    nxt = lambda col: pl.BlockSpec((16, CW), lambda i: (jnp.minimum((i + 1) * (tm // 16), L // 16 - 1), col))
    const = lambda r: pl.BlockSpec((r, CW), lambda i: (0, 0))
    return pl.pallas_call(
        body, grid=(nt,),
        in_specs=[blk(1), prev(1), blk(2), nxt(2), blk(3), prev(3), blk(0), nxt(0), const(3), const(1),
                  pl.BlockSpec(memory_space=pl.ANY)],
        out_specs=[pl.BlockSpec((tm, 3 * CW), lambda i: (i, 0)), const(3), const(1)],
        out_shape=[jax.ShapeDtypeStruct(dproj.shape, bf16), jax.ShapeDtypeStruct((3, CW), f32),
                   jax.ShapeDtypeStruct((1, CW), f32)],
        input_output_aliases={10: 0},
        name="convb_bwd", compiler_params=_cparams(("arbitrary",)))(proj, proj, proj, proj, proj, proj, dyb0, dyb0, w, b,
                                                                    dproj)


def _ffn_act(cfg, hh, w, b):
    L, F = cfg.L, cfg.F
    tm = _tile(L, 512, 16)
    tc = _tile(F, 1408, 128)
    ncb = F // tc

    def body(a_ref, ah_ref, g_ref, w_ref, b_ref, o_ref, act_ref):
        i = pl.program_id(0)
        rows8 = lax.broadcasted_iota(jnp.int32, (8, tc), 0)
        a = a_ref[...].astype(f32)
        ah = jnp.where(i == 0, 0.0, ah_ref[...].astype(f32)[8:])
        act = _conv3(a, ah, w_ref, b_ref, rows8)
        act_ref[...] = act.astype(bf16)
        o_ref[...] = (_gelu(act) * g_ref[...].astype(f32)).astype(bf16)

    tile = pl.BlockSpec((tm, tc), lambda i, j: (i, j))
    return pl.pallas_call(
        body, grid=(L // tm, ncb),
        in_specs=[tile, pl.BlockSpec((16, tc), lambda i, j: (jnp.maximum(i * (tm // 16) - 1, 0), j)),
                  pl.BlockSpec((tm, tc), lambda i, j: (i, j + ncb)),
                  pl.BlockSpec((3, tc), lambda i, j: (0, j)), pl.BlockSpec((1, tc), lambda i, j: (0, j))],
        out_specs=[tile, tile], out_shape=[jax.ShapeDtypeStruct((L, F), bf16)] * 2,
        name="ffn_act", compiler_params=_cparams(("parallel", "parallel")))(hh, hh, hh, w, b)


def _ffn_act_bwd(cfg, hh, act, df, w):
    L, F = cfg.L, cfg.F
    tm = _tile(L, 512, 16)
    tc = _tile(F, 1408, 128)
    ncb = F // tc
    nt = L // tm

    def body(a_ref, c_ref, cn_ref, g_ref, gn_ref, d_ref, dn_ref, w_ref, dhh_ref, dw_ref, db_ref):
        i = pl.program_id(1)
        rows8 = lax.broadcasted_iota(jnp.int32, (8, tc), 0)
        a = a_ref[...].astype(f32)
        d = d_ref[...].astype(f32)
        gelu, gelu_d = _gelu_and_grad(c_ref[...].astype(f32))
        dhh_ref[1] = (d * gelu).astype(bf16)
        dact = d * g_ref[...].astype(f32) * gelu_d
        dactn = dn_ref[...].astype(f32)[:8] * gn_ref[...].astype(f32)[:8] * _gelu_grad(cn_ref[...].astype(f32)[:8])
        dactn = jnp.where(i == nt - 1, 0.0, dactn)
        up1 = _shift_up(dact, dactn, 1, rows8)
        up2 = _shift_up(dact, dactn, 2, rows8)
        dhh_ref[0] = (w_ref[2:3, :] * dact + w_ref[1:2, :] * up1 + w_ref[0:1, :] * up2).astype(bf16)

        @pl.when(i == 0)
        def _():
            dw_ref[...] = jnp.zeros_like(dw_ref)
            db_ref[...] = jnp.zeros_like(db_ref)

        dw_ref[0:1, :] += jnp.sum(up2 * a, axis=0, keepdims=True)
        dw_ref[1:2, :] += jnp.sum(up1 * a, axis=0, keepdims=True)
        dw_ref[2:3, :] += jnp.sum(dact * a, axis=0, keepdims=True)
        db_ref[...] += jnp.sum(dact, axis=0, keepdims=True)

    blk = lambda off: pl.BlockSpec((tm, tc), lambda j, i: (i, j + off))
    nxt = lambda off: pl.BlockSpec((16, tc), lambda j, i: (jnp.minimum((i + 1) * (tm // 16), L // 16 - 1), j + off))
    const = lambda r: pl.BlockSpec((r, tc), lambda j, i: (0, j))
    return pl.pallas_call(
        body, grid=(ncb, nt),
        in_specs=[blk(0), blk(0), nxt(0), blk(ncb), nxt(ncb), blk(0), nxt(0), const(3)],
        out_specs=[pl.BlockSpec((2, tm, tc), lambda j, i: (0, i, j)), const(3), const(1)],
        out_shape=[jax.ShapeDtypeStruct((2, L, F), bf16),
                   jax.ShapeDtypeStruct((3, F), f32), jax.ShapeDtypeStruct((1, F), f32)],
        name="ffn_act_bwd", compiler_params=_cparams(("parallel", "arbitrary")))(hh, act, act, hh, hh, df, df, w)


def _merge_fwd(cfg, ya1, yb0, proj, wso, wco):
    L, D, SW, CW = cfg.L, cfg.D, cfg.SW, cfg.CW
    Ns = D // _NCHIP
    tm = _tile(L, 1024, 16)
    tn = _tile(Ns, 512, 128)
    nb = Ns // tn
    off_a = (SW + 3 * CW) // tn
    off_b = (SW + 3 * CW + D) // tn

    def body(a_ref, b_ref, wa_ref, wb_ref, ma_ref, mb_ref, m_ref, ya_ref, yb_ref):
        ya = jnp.dot(a_ref[...], wa_ref[...], preferred_element_type=f32)
        yb = jnp.dot(b_ref[...], wb_ref[...], preferred_element_type=f32)
        sa = jax.nn.sigmoid(ma_ref[...].astype(f32))
        sb = jax.nn.sigmoid(mb_ref[...].astype(f32))
        m_ref[...] = (sa * ya + sb * yb).astype(bf16)
        ya_ref[...] = ya.astype(bf16)
        yb_ref[...] = yb.astype(bf16)

    o_spec = pl.BlockSpec((tm, tn), lambda i, j: (i, j))
    o_shape = jax.ShapeDtypeStruct((L, D), bf16)
    return pl.pallas_call(
        body, grid=(L // tm, D // tn),
        in_specs=[pl.BlockSpec((tm, SW), lambda i, j: (i, 0)), pl.BlockSpec((tm, CW), lambda i, j: (i, 0)),
                  pl.BlockSpec((None, SW, tn), lambda i, j: (j // nb, 0, j % nb)),
                  pl.BlockSpec((None, CW, tn), lambda i, j: (j // nb, 0, j % nb)),
                  pl.BlockSpec((tm, tn), lambda i, j: (i, off_a + j)), pl.BlockSpec((tm, tn), lambda i, j: (i, off_b + j))],
        out_specs=[o_spec, o_spec, o_spec], out_shape=[o_shape, o_shape, o_shape],
        name="merge_fwd", compiler_params=_cparams(("parallel", "parallel")))(ya1, yb0, wso, wco, proj, proj)


def _s5_dims(cfg):
    G = cfg.SW // _SSM_GROUP
    NS = G // _SLAB_GROUPS
    SC = _SLAB_GROUPS * _SSM_GROUP
    SH = _SLAB_GROUPS * _SSM_STATE
    NST = 2 * SH * NS
    return G, NS, SC, SH, NST


def _lane_slabs(cfg, W):
    _, NS, _, SH, _ = _s5_dims(cfg)
    return [(2 * SH * s + w0, 2 * SH * s + SH + w0) for s in range(NS) for w0 in range(0, SH, W)]


def _discretize(a_re, a_im, log_dt, b_re, b_im):
    dt = jnp.exp(log_dt)[:, None]
    mag = jnp.exp(dt * a_re)
    abr = mag * jnp.cos(dt * a_im)
    abi = mag * jnp.sin(dt * a_im)
    nr = abr - 1.0
    ni = abi
    den = a_re * a_re + a_im * a_im
    fr = (nr * a_re + ni * a_im) / den
    fi = (ni * a_re - nr * a_im) / den
    bbr = fr[..., None] * b_re - fi[..., None] * b_im
    bbi = fr[..., None] * b_im + fi[..., None] * b_re
    return abr, abi, bbr, bbi


def _state_rows(cfg, re, im):
    _, NS, _, SH, _ = _s5_dims(cfg)
    return jnp.concatenate([re.reshape(NS, SH), im.reshape(NS, SH)], axis=1).reshape(-1)


def _s5_tables(cfg, abr, abi, bbr, bbi, c_re, c_im):
    G, NS, SC, SH, NST = _s5_dims(cfg)
    S = cfg.T // 8
    eye = jnp.eye(_SLAB_GROUPS, dtype=f32)
    bb = jnp.stack([bbr, bbi]).reshape(2, NS, _SLAB_GROUPS, _SSM_STATE, _SSM_GROUP)
    bs = jnp.einsum("rsgph,gq->sghrqp", bb, eye).reshape(NS, SC, 2 * SH).astype(bf16)
    cc = jnp.stack([c_re, -c_im]).reshape(2, NS, _SLAB_GROUPS, _SSM_GROUP, _SSM_STATE)
    cs = jnp.einsum("rsghp,gq->srqpgh", cc, eye).reshape(NS, 2 * SH, SC).astype(bf16)
    arep = jnp.broadcast_to(_state_rows(cfg, abr, abi)[None, :], (8, NST))
    pr, pi = abr, abi
    for _ in range(S - 1):
        pr, pi = pr * abr - pi * abi, pr * abi + pi * abr
    apow = jnp.broadcast_to(_state_rows(cfg, pr, pi)[None, :], (8, NST))
    t = jnp.arange(cfg.T)
    perm = (t % 8) * S + t // 8
    pm = jax.nn.one_hot(perm, cfg.T, dtype=bf16)
    return bs, cs, arep, apow, pm, pm.T


def _cmul_add(ar, ai, xr, xi, br, bi):
    return ar * xr - ai * xi + br, ar * xi + ai * xr + bi


def _s5_forward_chunk(cfg, W, upb, bs_ref, arep_ref, apow_ref, st, x0, cin_store):
    _, NS, SC, SH, _ = _s5_dims(cfg)
    S = cfg.T // 8
    for s in range(NS):
        st[:, 2 * SH * s:2 * SH * (s + 1)] = jnp.dot(upb[:, SC * s:SC * (s + 1)], bs_ref[s], preferred_element_type=f32)
    rows = lax.broadcasted_iota(jnp.int32, (8, W), 0)
    zero = jnp.zeros((8, W), f32)
    for rc, ic in _lane_slabs(cfg, W):
        ar = arep_ref[:, rc:rc + W]
        ai = arep_ref[:, ic:ic + W]

        def step(i, carry, rc=rc, ic=ic, ar=ar, ai=ai):
            xr, xi = carry
            r0 = pl.multiple_of(i * 8, 8)
            nr, ni = _cmul_add(ar, ai, xr, xi, st[pl.ds(r0, 8), rc:rc + W], st[pl.ds(r0, 8), ic:ic + W])
            st[pl.ds(r0, 8), rc:rc + W] = nr
            st[pl.ds(r0, 8), ic:ic + W] = ni
            return nr, ni

        er, ei = lax.fori_loop(0, S, step, (zero, zero))
        pr = apow_ref[:, rc:rc + W]
        pi = apow_ref[:, ic:ic + W]
        x0r = x0[:, rc:rc + W]
        x0i = x0[:, ic:ic + W]
        cr = jnp.where(rows == 0, x0r, 0.0)
        ci = jnp.where(rows == 0, x0i, 0.0)
        for _ in range(7):
            fr, fi = _cmul_add(pr, pi, cr, ci, er, ei)
            cr = jnp.where(rows == 0, x0r, pltpu.roll(fr, 1, axis=0))
            ci = jnp.where(rows == 0, x0i, pltpu.roll(fi, 1, axis=0))
        fr, fi = _cmul_add(pr, pi, cr, ci, er, ei)
        x0[:, rc:rc + W] = jnp.broadcast_to(fr[7:8, :], (8, W))
        x0[:, ic:ic + W] = jnp.broadcast_to(fi[7:8, :], (8, W))
        cin_store(rc, ic, cr, ci)

        def fix(i, carry, rc=rc, ic=ic, ar=ar, ai=ai):
            kr, ki = carry
            r0 = pl.multiple_of(i * 8, 8)
            nr, ni = ar * kr - ai * ki, ar * ki + ai * kr
            st[pl.ds(r0, 8), rc:rc + W] = st[pl.ds(r0, 8), rc:rc + W] + nr
            st[pl.ds(r0, 8), ic:ic + W] = st[pl.ds(r0, 8), ic:ic + W] + ni
            return nr, ni

        lax.fori_loop(0, S, fix, (cr, ci))


def _s5_fwd(cfg, proj, tabs, dskip):
    L, SW, T = cfg.L, cfg.SW, cfg.T
    G, NS, SC, SH, NST = _s5_dims(cfg)
    bs, cs, arep, apow, pm, pmt = tabs
    W = min(512, SH)
    NC = L // T

    def body(u_ref, pm_ref, pmt_ref, bs_ref, cs_ref, arep_ref, apow_ref, dskip_ref, y_ref, ya0_ref, cin_ref, st, x0):
        c = pl.program_id(0)

        @pl.when(c == 0)
        def _():
            x0[...] = jnp.zeros_like(x0)

        up = jnp.dot(pm_ref[...], u_ref[...], preferred_element_type=f32)
        upb = up.astype(bf16)

        def cin_store(rc, ic, cr, ci):
            cin_ref[0, :, rc:rc + W] = cr
            cin_ref[0, :, ic:ic + W] = ci

        _s5_forward_chunk(cfg, W, upb, bs_ref, arep_ref, apow_ref, st, x0, cin_store)
        yp = jnp.concatenate(
            [jnp.dot(st[:, 2 * SH * s:2 * SH * (s + 1)].astype(bf16), cs_ref[s], preferred_element_type=f32)
             for s in range(NS)], axis=1) + dskip_ref[...] * up
        y = jnp.dot(pmt_ref[...], yp.astype(bf16), preferred_element_type=f32)
        y_ref[...] = y.astype(bf16)
        ya0_ref[...] = _gelu(y).astype(bf16)

    const2 = lambda shape: pl.BlockSpec(shape, lambda c: (0, 0))
    const3 = lambda shape: pl.BlockSpec(shape, lambda c: (0, 0, 0))
    return pl.pallas_call(
        body, grid=(NC,),
        in_specs=[pl.BlockSpec((T, SW), lambda c: (c, 0)), const2((T, T)), const2((T, T)), const3((NS, SC, 2 * SH)),
                  const3((NS, 2 * SH, SC)), const2((8, NST)), const2((8, NST)), const2((1, SW))],
        out_specs=[pl.BlockSpec((T, SW), lambda c: (c, 0)), pl.BlockSpec((T, SW), lambda c: (c, 0)),
                   pl.BlockSpec((1, 8, NST), lambda c: (c, 0, 0)), pl.BlockSpec((T, NST), lambda c: (c, 0))],
        out_shape=[jax.ShapeDtypeStruct((L, SW), bf16), jax.ShapeDtypeStruct((L, SW), bf16),
                   jax.ShapeDtypeStruct((NC, 8, NST), f32), jax.ShapeDtypeStruct((L, NST), f32)],
        scratch_shapes=[pltpu.VMEM((8, NST), f32)],
        name="s5_fwd", compiler_params=_cparams(("arbitrary",)))(proj, pm, pmt, bs, cs, arep, apow, dskip)


def _s5_bwd(cfg, proj, dy, cin, xs, tabs, dskip, dproj, after=()):
    L, SW, T = cfg.L, cfg.SW, cfg.T
    du_col = 3 * cfg.CW // SW
    G, NS, SC, SH, NST = _s5_dims(cfg)
    bs, cs, arep, apow, pm, pmt = tabs
    W = min(512, SH)
    S = T // 8
    NC = L // T

    def body(u_ref, dy_ref, cin_ref, st, pm_ref, pmt_ref, bs_ref, cs_ref, arep_ref, apow_ref, dskip_ref, dproj_ref, *rest):
        du_ref, da_ref, db_ref, dc_ref, dd_ref, gs, g0, db_acc, dc_acc = rest[len(after):]
        c = pl.program_id(0)

        @pl.when(c == 0)
        def _():
            g0[...] = jnp.zeros_like(g0)
            da_ref[...] = jnp.zeros_like(da_ref)
            dd_ref[...] = jnp.zeros_like(dd_ref)
            db_acc[...] = jnp.zeros_like(db_acc)
            dc_acc[...] = jnp.zeros_like(dc_acc)

        up = jnp.dot(pm_ref[...], u_ref[...], preferred_element_type=f32)
        upb = up.astype(bf16)
        dyp = jnp.dot(pm_ref[...], dy_ref[...], preferred_element_type=f32)
        dypb = dyp.astype(bf16)
        for s in range(NS):
            gs[:, 2 * SH * s:2 * SH * (s + 1)] = lax.dot_general(
                dypb[:, SC * s:SC * (s + 1)], cs_ref[s], (((1,), (1,)), ((), ())), preferred_element_type=f32)
        rows = lax.broadcasted_iota(jnp.int32, (8, W), 0)
        zero = jnp.zeros((8, W), f32)
        for rc, ic in _lane_slabs(cfg, W):
            ar = arep_ref[:, rc:rc + W]
            ai = arep_ref[:, ic:ic + W]

            def rstep(k, carry, rc=rc, ic=ic, ar=ar, ai=ai):
                gr, gi = carry
                r0 = pl.multiple_of((S - 1 - k) * 8, 8)
                nr = ar * gr + ai * gi + gs[pl.ds(r0, 8), rc:rc + W]
                ni = ar * gi - ai * gr + gs[pl.ds(r0, 8), ic:ic + W]
                gs[pl.ds(r0, 8), rc:rc + W] = nr
                gs[pl.ds(r0, 8), ic:ic + W] = ni
                return nr, ni

            er, ei = lax.fori_loop(0, S, rstep, (zero, zero))
            pr = apow_ref[:, rc:rc + W]
            pi = apow_ref[:, ic:ic + W]
            g0r = g0[:, rc:rc + W]
            g0i = g0[:, ic:ic + W]
            cr = jnp.where(rows == 7, g0r, 0.0)
            ci = jnp.where(rows == 7, g0i, 0.0)
            for _ in range(7):
                fr = er + pr * cr + pi * ci
                fi = ei + pr * ci - pi * cr
                cr = jnp.where(rows == 7, g0r, pltpu.roll(fr, 7, axis=0))
                ci = jnp.where(rows == 7, g0i, pltpu.roll(fi, 7, axis=0))
            fr = er + pr * cr + pi * ci
            fi = ei + pr * ci - pi * cr
            g0[:, rc:rc + W] = jnp.broadcast_to(fr[0:1, :], (8, W))
            g0[:, ic:ic + W] = jnp.broadcast_to(fi[0:1, :], (8, W))

            def fix(k, carry, rc=rc, ic=ic, ar=ar, ai=ai):
                kr, ki, accr, acci = carry
                i = S - 1 - k
                r0 = pl.multiple_of(i * 8, 8)
                rp = pl.multiple_of((i - 1) * 8, 8)
                nr = ar * kr + ai * ki
                ni = ar * ki - ai * kr
                gr = gs[pl.ds(r0, 8), rc:rc + W] + nr
                gi = gs[pl.ds(r0, 8), ic:ic + W] + ni
                gs[pl.ds(r0, 8), rc:rc + W] = gr
                gs[pl.ds(r0, 8), ic:ic + W] = gi
                xr = st[pl.ds(rp, 8), rc:rc + W]
                xi = st[pl.ds(rp, 8), ic:ic + W]
                return nr, ni, accr + gr * xr + gi * xi, acci + gi * xr - gr * xi

            kr, ki, accr, acci = lax.fori_loop(0, S - 1, fix, (cr, ci, zero, zero))
            nr = ar * kr + ai * ki
            ni = ar * ki - ai * kr
            gr = gs[0:8, rc:rc + W] + nr
            gi = gs[0:8, ic:ic + W] + ni
            gs[0:8, rc:rc + W] = gr
            gs[0:8, ic:ic + W] = gi
            xr = cin_ref[0, :, rc:rc + W]
            xi = cin_ref[0, :, ic:ic + W]
            da_ref[:, rc:rc + W] += accr + gr * xr + gi * xi
            da_ref[:, ic:ic + W] += acci + gi * xr - gr * xi

        dups = []
        for s in range(NS):
            gsb = gs[:, 2 * SH * s:2 * SH * (s + 1)].astype(bf16)
            dups.append(lax.dot_general(gsb, bs_ref[s], (((1,), (1,)), ((), ())), preferred_element_type=f32))
            db_acc[s] += lax.dot_general(upb[:, SC * s:SC * (s + 1)], gsb, (((0,), (0,)), ((), ())),
                                         preferred_element_type=f32)
            dc_acc[s] += lax.dot_general(st[:, 2 * SH * s:2 * SH * (s + 1)].astype(bf16), dypb[:, SC * s:SC * (s + 1)],
                                         (((0,), (0,)), ((), ())), preferred_element_type=f32)
        dup = jnp.concatenate(dups, axis=1) + dskip_ref[...] * dyp
        du_ref[...] = jnp.dot(pmt_ref[...], dup.astype(bf16), preferred_element_type=f32).astype(bf16)
        dd_ref[...] += jnp.sum(dyp * up, axis=0, keepdims=True)

        @pl.when(c == NC - 1)
        def _():
            PS, GH = _SSM_STATE, _SSM_GROUP
            mask_b = (lax.broadcasted_iota(jnp.int32, (SC, SH), 0) // GH
                      == lax.broadcasted_iota(jnp.int32, (SC, SH), 1) // PS)
            mask_c = (lax.broadcasted_iota(jnp.int32, (SH, SC), 0) // PS
                      == lax.broadcasted_iota(jnp.int32, (SH, SC), 1) // GH)
            for s in range(NS):
                for r in range(2):
                    xb = jnp.where(mask_b, db_acc[s, :, r * SH:(r + 1) * SH], 0.0)
                    zb = xb[:, 0:128]
                    for q in range(1, SH // 128):
                        zb = zb + xb[:, q * 128:(q + 1) * 128]
                    db_ref[s, r] = zb + pltpu.roll(zb, PS, axis=1)
                    xc = jnp.where(mask_c, dc_acc[s, r * SH:(r + 1) * SH, :], 0.0)
                    zc = xc[0:PS]
                    for q in range(1, SH // PS):
                        zc = zc + xc[q * PS:(q + 1) * PS]
                    dc_ref[s, r] = zc

    rev = lambda c: (NC - 1 - c, 0)
    const2 = lambda shape: pl.BlockSpec(shape, lambda c: (0, 0))
    const3 = lambda shape: pl.BlockSpec(shape, lambda c: (0, 0, 0))
    const4 = lambda shape: pl.BlockSpec(shape, lambda c: (0, 0, 0, 0))
    return pl.pallas_call(
        body, grid=(NC,),
        in_specs=[pl.BlockSpec((T, SW), rev), pl.BlockSpec((T, SW), rev), pl.BlockSpec((1, 8, NST), lambda c: (NC - 1 - c, 0, 0)),
                  pl.BlockSpec((T, NST), rev), const2((T, T)), const2((T, T)),
                  pl.BlockSpec((NS, SC, 2 * SH), lambda c: (0, 0, 0), pipeline_mode=pl.Buffered(1)),
                  pl.BlockSpec((NS, 2 * SH, SC), lambda c: (0, 0, 0), pipeline_mode=pl.Buffered(1)),
                  const2((8, NST)), const2((8, NST)), const2((1, SW)), _ANY] + [_ANY] * len(after),
        out_specs=[pl.BlockSpec((T, SW), lambda c: (NC - 1 - c, du_col)), const2((8, NST)),
                   const4((NS, 2, SC, 128)), const4((NS, 2, _SSM_STATE, SC)), const2((1, SW))],
        out_shape=[jax.ShapeDtypeStruct(dproj.shape, bf16), jax.ShapeDtypeStruct((8, NST), f32),
                   jax.ShapeDtypeStruct((NS, 2, SC, 128), f32), jax.ShapeDtypeStruct((NS, 2, _SSM_STATE, SC), f32),
                   jax.ShapeDtypeStruct((1, SW), f32)],
        scratch_shapes=[pltpu.VMEM((T, NST), f32), pltpu.VMEM((8, NST), f32),
                        pltpu.VMEM((NS, SC, 2 * SH), f32), pltpu.VMEM((NS, 2 * SH, SC), f32)],
        input_output_aliases={11: 0},
        name="s5_bwd", compiler_params=_cparams(("arbitrary",), _VMEM_S5_BWD))(proj, dy, cin, xs, pm, pmt, bs, cs, arep, apow, dskip, dproj,
                                                                  *after)


def _s5_param_grads(cfg, da, db_diag, dc_diag):
    G, NS, SC, SH, NST = _s5_dims(cfg)
    das = da.sum(axis=0).reshape(NS, 2, SH)
    dabr = das[:, 0].reshape(G, _SSM_STATE)
    dabi = das[:, 1].reshape(G, _SSM_STATE)
    dbd = db_diag[..., :_SSM_STATE].reshape(NS, 2, _SLAB_GROUPS, _SSM_GROUP, _SSM_STATE)
    dbb = dbd.transpose(1, 0, 2, 4, 3).reshape(2, G, _SSM_STATE, _SSM_GROUP)
    dcd = dc_diag.reshape(NS, 2, _SSM_STATE, _SLAB_GROUPS, _SSM_GROUP)
    dcc = dcd.transpose(1, 0, 3, 4, 2).reshape(2, G, _SSM_GROUP, _SSM_STATE)
    return dabr, dabi, dbb[0], dbb[1], dcc[0], -dcc[1]


def _coords():
    return lax.axis_index("x"), lax.axis_index("y"), lax.axis_index("c")


def _other_chips(x, y):
    return [(1 - x, y), (x, 1 - y), (1 - x, 1 - y)]


def _allreduce8(name, v):
    R = v.shape[0]

    def body(v_ref, o_ref, sib, chips, mine, ssem, rsem):
        x, y, c = _coords()
        d2d = pltpu.make_async_remote_copy(src_ref=v_ref, dst_ref=sib, send_sem=ssem.at[0], recv_sem=rsem.at[0],
                                           device_id=(x, y, 1 - c), device_id_type=_MESH)
        d2d.start()
        d2d.wait()
        mine[...] = v_ref[...] + sib[...]
        cps = [pltpu.make_async_remote_copy(src_ref=mine, dst_ref=chips.at[j], send_sem=ssem.at[1 + j],
                                            recv_sem=rsem.at[1 + j], device_id=(*chip, c), device_id_type=_MESH)
               for j, chip in enumerate(_other_chips(x, y))]
        for cp in cps:
            cp.start()
        for cp in cps:
            cp.wait()
        o_ref[...] = (mine[...] + chips[1]) + (chips[0] + chips[2])

    vm = pl.BlockSpec(memory_space=pltpu.VMEM)
    return pl.pallas_call(
        body, in_specs=[vm], out_specs=vm, out_shape=jax.ShapeDtypeStruct((R, 128), f32),
        scratch_shapes=[pltpu.VMEM((R, 128), f32), pltpu.VMEM((3, R, 128), f32), pltpu.VMEM((R, 128), f32),
                        pltpu.SemaphoreType.DMA((4,)), pltpu.SemaphoreType.DMA((4,))],
        name=name, compiler_params=pltpu.CompilerParams(vmem_limit_bytes=_VMEM_LIMIT))(v)


def _cast_into_slot(name, w, k_idx, also_alone=False):
    R, C = w.shape
    tr = _tile(R, 256, 16)

    def body(k_ref, w_ref, *o_refs):
        for o_ref in o_refs:
            o_ref[...] = w_ref[...].astype(bf16)

    slot = (jax.ShapeDtypeStruct((_NCHIP, R, C), bf16), pl.BlockSpec((None, tr, C), lambda r, kr: (kr[0], r, 0)))
    alone = (jax.ShapeDtypeStruct((R, C), bf16), pl.BlockSpec((tr, C), lambda r, kr: (r, 0)))
    outs = [slot, alone] if also_alone else [slot]
    gs = pltpu.PrefetchScalarGridSpec(
        num_scalar_prefetch=1, grid=(R // tr,),
        in_specs=[pl.BlockSpec((tr, C), lambda r, kr: (r, 0))], out_specs=[o[1] for o in outs])
    res = pl.pallas_call(body, grid_spec=gs, out_shape=[o[0] for o in outs], name=name,
                         compiler_params=_cparams(("parallel",)))(k_idx, w)
    return res if also_alone else res[0]


def _in_proj(cfg, xn1, w_own, w_in, k_idx):
    L, D = xn1.shape
    S, _, Ns = w_in.shape
    tm, tn = _tile(L, 1024, 16), _tile(Ns, 1024, 128)
    nb = Ns // tn

    def body(k_ref, a_ref, w_ref, *rest):
        rest[-1][...] = jnp.dot(a_ref[...], w_ref[...], preferred_element_type=f32).astype(bf16)

    a_spec = pl.BlockSpec((tm, D), lambda i, j, kr: (i, 0))
    shape = jax.ShapeDtypeStruct((L, S * Ns), bf16)
    own = pltpu.PrefetchScalarGridSpec(
        num_scalar_prefetch=1, grid=(L // tm, nb),
        in_specs=[a_spec, pl.BlockSpec((D, tn), lambda i, j, kr: (0, j))],
        out_specs=pl.BlockSpec((tm, tn), lambda i, j, kr: (i, kr[0] * nb + j)))
    proj = pl.pallas_call(body, grid_spec=own, out_shape=shape, name="in_proj_own",
                          compiler_params=_cparams(("parallel", "parallel")))(k_idx, xn1, w_own)
    shard = lambda j, kr: (kr[0] + 1 + j // nb) % S
    rest = pltpu.PrefetchScalarGridSpec(
        num_scalar_prefetch=1, grid=(L // tm, (S - 1) * nb),
        in_specs=[a_spec, pl.BlockSpec((None, D, tn), lambda i, j, kr: (shard(j, kr), 0, j % nb)), _ANY],
        out_specs=pl.BlockSpec((tm, tn), lambda i, j, kr: (i, shard(j, kr) * nb + j % nb)))
    return pl.pallas_call(body, grid_spec=rest, out_shape=shape, input_output_aliases={3: 0}, name="in_proj",
                          compiler_params=_cparams(("parallel", "parallel")))(k_idx, xn1, w_in, proj)


def _handshake(peers):
    barrier = pltpu.get_barrier_semaphore()
    for peer in peers:
        pl.semaphore_signal(barrier, inc=1, device_id=peer, device_id_type=_MESH)
    pl.semaphore_wait(barrier, len(peers))


def _allgather_weights(name, bufs, collective_id):
    n = len(bufs)
    refs = [jax.new_ref(b, memory_space=pltpu.MemorySpace.HBM) for b in bufs]

    def copy(ref, sems, idx, to):
        return pltpu.make_async_remote_copy(src_ref=ref, dst_ref=ref, send_sem=sems[0].at[idx], recv_sem=sems[1].at[idx],
                                            device_id=to, device_id_type=_MESH)

    def launch(ssem, rsem, qssem, qrsem, fssem, frsem):
        x, y, c = _coords()
        k = 2 * x + y
        nbrs = [(1 - x, y), (x, 1 - y)]
        across = 2 * (1 - x) + (1 - y)
        sibling = (x, y, 1 - c)
        _handshake([sibling] + [(*chip, c) for chip in nbrs])
        started = []

        def start(cp):
            cp.start()
            started.append(cp)

        for w in range(n):
            rh = refs[w].shape[1] // 2
            for j, chip in enumerate(nbrs):
                start(copy(refs[w].at[k, pl.ds(c * rh, rh)], (ssem, rsem), (w, j), (*chip, c)))
        for w in range(n):
            rh = refs[w].shape[1] // 2
            rq = rh // 2
            for j, (ox, oy) in enumerate(nbrs):
                ko = 2 * ox + oy
                landed = refs[w].at[ko, pl.ds(c * rh, rh)]
                copy(landed, (ssem, rsem), (w, j), (ox, oy, c)).wait_recv()
                start(copy(refs[w].at[ko, pl.ds(c * rh + j * rq, rq)], (qssem, qrsem), (w, j), (*nbrs[1 - j], c)))
                start(copy(landed, (fssem, frsem), (w, j), sibling))
        for w in range(n):
            rh = refs[w].shape[1] // 2
            rq = rh // 2
            for q in range(2):
                quarter = refs[w].at[across, pl.ds(c * rh + q * rq, rq)]
                copy(quarter, (qssem, qrsem), (w, q), (*nbrs[1 - q], c)).wait_recv()
            start(copy(refs[w].at[across, pl.ds(c * rh, rh)], (fssem, frsem), (w, 2), sibling))
        for w in range(n):
            rh = refs[w].shape[1] // 2
            for j, ko in enumerate([2 * nbrs[0][0] + nbrs[0][1], 2 * nbrs[1][0] + nbrs[1][1], across]):
                copy(refs[w].at[ko, pl.ds((1 - c) * rh, rh)], (fssem, frsem), (w, j), sibling).wait_recv()
        for cp in started:
            cp.wait_send()

    _sequencer_kernel(name, collective_id,
                      (pltpu.SemaphoreType.DMA((n, 2)), pltpu.SemaphoreType.DMA((n, 2)), pltpu.SemaphoreType.DMA((n, 2)),
                       pltpu.SemaphoreType.DMA((n, 2)), pltpu.SemaphoreType.DMA((n, 3)), pltpu.SemaphoreType.DMA((n, 3))),
                      launch)
    return [r[...] for r in refs]


def _sequencer_kernel(name, collective_id, sems, body):
    pl.kernel(body, mesh=plsc.ScalarSubcoreMesh(axis_name="seq", num_cores=1), name=name, scratch_types=sems,
              compiler_params=pltpu.CompilerParams(collective_id=collective_id))()


def _hbm_ref(a):
    return jax.new_ref(a, memory_space=pltpu.MemorySpace.HBM)


def _exchange_halves(name, grads, collective_id):
    n = len(grads)
    srcs = [_hbm_ref(g) for g in grads]
    dsts = [jax.empty_ref(jax.ShapeDtypeStruct((g.shape[0], g.shape[1] // 2, g.shape[2]), g.dtype),
                          memory_space=pltpu.MemorySpace.HBM) for g in grads]

    def body(ssem, rsem):
        x, y, c = _coords()
        _handshake([(x, y, 1 - c)])
        cps = []
        for w in range(n):
            rh = srcs[w].shape[1] // 2
            cp = pltpu.make_async_remote_copy(
                src_ref=srcs[w].at[:, pl.ds((1 - c) * rh, rh)], dst_ref=dsts[w], send_sem=ssem.at[w], recv_sem=rsem.at[w],
                device_id=(x, y, 1 - c), device_id_type=_MESH)
            cp.start()
            cps.append(cp)
        for cp in cps:
            cp.wait()

    _sequencer_kernel(name, collective_id, (pltpu.SemaphoreType.DMA((n,)), pltpu.SemaphoreType.DMA((n,))), body)
    return [d[...] for d in dsts]


def _scatter_shards(name, parts, collective_id):
    n = len(parts)
    srcs = [_hbm_ref(p) for p in parts]
    dsts = [jax.empty_ref(jax.ShapeDtypeStruct((3,) + p.shape[1:], p.dtype), memory_space=pltpu.MemorySpace.HBM)
            for p in parts]

    def body(ssem, rsem):
        x, y, c = _coords()
        k = 2 * x + y
        others = _other_chips(x, y)
        _handshake([(*chip, c) for chip in others])
        cps = []
        for w in range(n):
            for j, (ox, oy) in enumerate(others):
                cp = pltpu.make_async_remote_copy(
                    src_ref=srcs[w].at[(2 * ox + oy - k + 3) % 4], dst_ref=dsts[w].at[j], send_sem=ssem.at[w, j],
                    recv_sem=rsem.at[w, j],
                    device_id=(ox, oy, c), device_id_type=_MESH)
                cp.start()
                cps.append(cp)
        for cp in cps:
            cp.wait()

    _sequencer_kernel(name, collective_id, (pltpu.SemaphoreType.DMA((n, 3)), pltpu.SemaphoreType.DMA((n, 3))), body)
    return [d[...] for d in dsts]


def _join_halves(name, bufs, collective_id):
    n = len(bufs)
    refs = [_hbm_ref(b) for b in bufs]

    def body(ssem, rsem):
        x, y, c = _coords()
        _handshake([(x, y, 1 - c)])
        cps = []
        for w in range(n):
            rh = refs[w].shape[0] // 2
            mine = refs[w].at[pl.ds(c * rh, rh)]
            cp = pltpu.make_async_remote_copy(src_ref=mine, dst_ref=mine, send_sem=ssem.at[w], recv_sem=rsem.at[w],
                                              device_id=(x, y, 1 - c), device_id_type=_MESH)
            cp.start()
            cps.append(cp)
        for w in range(n):
            rh = refs[w].shape[0] // 2
            theirs = refs[w].at[pl.ds((1 - c) * rh, rh)]
            pltpu.make_async_remote_copy(src_ref=theirs, dst_ref=theirs, send_sem=ssem.at[w], recv_sem=rsem.at[w],
                                         device_id=(x, y, 1 - c), device_id_type=_MESH).wait_recv()
        for cp in cps:
            cp.wait_send()

    _sequencer_kernel(name, collective_id, (pltpu.SemaphoreType.DMA((n,)), pltpu.SemaphoreType.DMA((n,))), body)
    return [r[...] for r in refs]


def _allgather_small(name, v, collective_id):
    R = v.shape[0]
    src = _hbm_ref(v)
    dst = jax.empty_ref(jax.ShapeDtypeStruct((8, R, 128), v.dtype), memory_space=pltpu.MemorySpace.HBM)
    flips = [(dx, dy, dc) for dx in (0, 1) for dy in (0, 1) for dc in (0, 1)][1:]

    def body(lsem, ssem, rsem):
        x, y, c = _coords()
        me = 4 * x + 2 * y + c
        flip = lambda v, d: 1 - v if d else v
        peers = [(flip(x, dx), flip(y, dy), flip(c, dc)) for dx, dy, dc in flips]
        _handshake(peers)
        own = pltpu.make_async_copy(src, dst.at[me], lsem)
        own.start()
        cps = []
        for r, peer in enumerate(peers):
            cp = pltpu.make_async_remote_copy(src_ref=src, dst_ref=dst.at[me], send_sem=ssem.at[r], recv_sem=rsem.at[r],
                                              device_id=peer, device_id_type=_MESH)
            cp.start()
            cps.append(cp)
        for r, (px, py, pc) in enumerate(peers):
            theirs = dst.at[4 * px + 2 * py + pc]
            pltpu.make_async_remote_copy(src_ref=theirs, dst_ref=theirs, send_sem=ssem.at[r], recv_sem=rsem.at[r],
                                         device_id=(px, py, pc), device_id_type=_MESH).wait_recv()
        for cp in cps:
            cp.wait_send()
        own.wait()

    _sequencer_kernel(name, collective_id, (pltpu.SemaphoreType.DMA, pltpu.SemaphoreType.DMA((7,)),
                                            pltpu.SemaphoreType.DMA((7,))), body)
    return dst[...]


def _sum8(name, g8, after):
    R = g8.shape[1]
    tr = _tile(R, 512, 8)

    def body(g_ref, *rest):
        rest[-1][...] = (((g_ref[0] + g_ref[1]) + (g_ref[2] + g_ref[3]))
                         + ((g_ref[4] + g_ref[5]) + (g_ref[6] + g_ref[7])))

    return pl.pallas_call(body, grid=(R // tr,),
                          in_specs=[pl.BlockSpec((8, tr, 128), lambda i: (0, i, 0))] + [_ANY] * len(after),
                          out_specs=pl.BlockSpec((tr, 128), lambda i: (i, 0)), out_shape=jax.ShapeDtypeStruct((R, 128), f32),
                          name=name, compiler_params=_cparams(("parallel",)))(g8, *after)


def _add_own_half(name, g, t, kc_idx, after):
    S, R, C = g.shape
    rh = R // 2
    tr = _tile(rh, 256, 16)
    nrb = rh // tr
    shard = lambda s, kc: (kc[0] + 1 + s) % S

    def body(kc_ref, g_ref, t_ref, *rest):
        rest[-1][...] = (g_ref[...] + t_ref[...]).astype(bf16)

    gs = pltpu.PrefetchScalarGridSpec(
        num_scalar_prefetch=1, grid=(S - 1, nrb),
        in_specs=[pl.BlockSpec((None, tr, C), lambda s, r, kc: (shard(s, kc), kc[1] * nrb + r, 0)),
                  pl.BlockSpec((None, tr, C), lambda s, r, kc: (shard(s, kc), r, 0))] + [_ANY] * len(after),
        out_specs=pl.BlockSpec((None, tr, C), lambda s, r, kc: (s, r, 0)))
    return pl.pallas_call(body, grid_spec=gs, out_shape=jax.ShapeDtypeStruct((S - 1, rh, C), bf16), name=name,
                          compiler_params=_cparams(("parallel", "parallel")))(kc_idx, g, t, *after)


def _add_shard_parts(name, g, t, r, kc_idx, after):
    S, R, C = g.shape
    rh = R // 2
    tr = _tile(rh, 256, 16)
    nrb = rh // tr

    def body(kc_ref, g_ref, t_ref, r_ref, *rest):
        own = g_ref[...] + t_ref[...]
        rest[-1][...] = (own + r_ref[1].astype(f32)) + (r_ref[0].astype(f32) + r_ref[2].astype(f32))

    gs = pltpu.PrefetchScalarGridSpec(
        num_scalar_prefetch=1, grid=(nrb,),
        in_specs=[pl.BlockSpec((None, tr, C), lambda i, kc: (kc[0], kc[1] * nrb + i, 0)),
                  pl.BlockSpec((None, tr, C), lambda i, kc: (kc[0], i, 0)),
                  pl.BlockSpec((3, tr, C), lambda i, kc: (0, i, 0))] + [_ANY] * len(after),
        out_specs=pl.BlockSpec((tr, C), lambda i, kc: (kc[1] * nrb + i, 0)))
    return pl.pallas_call(body, grid_spec=gs, out_shape=jax.ShapeDtypeStruct((R, C), f32), name=name,
                          compiler_params=_cparams(("parallel",)))(kc_idx, g, t, r, *after)


def _adamw_update(wv, gv, mv, vv):
    nm = _ADAM_B1 * mv + (1.0 - _ADAM_B1) * gv
    nv = _ADAM_B2 * vv + (1.0 - _ADAM_B2) * (gv * gv)
    m_hat = nm / (1.0 - _ADAM_B1 ** _ADAM_STEP)
    v_hat = nv / (1.0 - _ADAM_B2 ** _ADAM_STEP)
    return -_ADAM_LR * (m_hat / (jnp.sqrt(v_hat) + _ADAM_EPS) + _ADAM_WD * wv), nm, nv


def _adamw(name, w, g, m, v, after=()):
    R, C = w.shape
    tr = _tile(R, 128, 8)

    def body(w_ref, g_ref, m_ref, v_ref, *rest):
        go_ref, d_ref, nm_ref, nv_ref = rest[len(after):]
        gv = g_ref[...]
        go_ref[...] = gv
        d_ref[...], nm_ref[...], nv_ref[...] = _adamw_update(w_ref[...], gv, m_ref[...], v_ref[...])

    spec = pl.BlockSpec((tr, C), lambda i: (i, 0))
    shape = jax.ShapeDtypeStruct((R, C), f32)
    return pl.pallas_call(body, grid=(R // tr,), in_specs=[spec] * 4 + [_ANY] * len(after), out_specs=[spec] * 4,
                          out_shape=[shape] * 4, name=name, compiler_params=_cparams(("parallel",)))(w, g, m, v, *after)


def _adamw_whole(name, w, g, m, v):
    def body(w_ref, g_ref, m_ref, v_ref, d_ref, nm_ref, nv_ref):
        d_ref[...], nm_ref[...], nv_ref[...] = _adamw_update(w_ref[...], g_ref[...], m_ref[...], v_ref[...])

    vm = pl.BlockSpec(memory_space=pltpu.VMEM)
    return pl.pallas_call(body, in_specs=[vm] * 4, out_specs=[vm] * 3, out_shape=[jax.ShapeDtypeStruct(w.shape, f32)] * 3,
                          name=name, compiler_params=pltpu.CompilerParams(vmem_limit_bytes=_VMEM_LIMIT))(w, g, m, v)


def _pack(arrs):
    flat = jnp.concatenate([a.reshape(-1).astype(f32) for a in arrs])
    n = flat.shape[0]
    pad = (-n) % (128 * 128)
    return jnp.pad(flat, (0, pad)).reshape(-1, 128)


def _unpack(packed, shapes):
    flat = packed.reshape(-1)
    out, off = [], 0
    for s in shapes:
        n = math.prod(s)
        out.append(flat[off:off + n].reshape(s))
        off += n
    return out


_BIG = ("w_in", "w_glu", "w_ssm_out", "w_conv_out", "w_o", "w_up", "w_down")
_SMALL = ("norm_tok", "a_re", "a_im", "log_dt", "b_re", "b_im", "c_re", "c_im", "d_skip", "conv_w", "conv_b",
          "norm_ffn", "ffn_conv_w", "ffn_conv_b", "norm_final")
_WEIGHTS = ("norm_tok", "w_in", "a_re", "a_im", "log_dt", "b_re", "b_im", "c_re", "c_im", "d_skip", "w_glu",
            "w_ssm_out", "conv_w", "conv_b", "w_conv_out", "w_o", "norm_ffn", "w_up", "ffn_conv_w", "ffn_conv_b",
            "w_down", "norm_final")


def _step(cfg, x, tgt, p, m, v):
    L, D, SW, CW, F = cfg.L, cfg.D, cfg.SW, cfg.CW, cfg.F
    xi, yi, ci = _coords()
    k_idx = (2 * xi + yi).astype(jnp.int32).reshape(1)
    c_idx = ci.astype(jnp.int32).reshape(1)
    x = x.reshape(L, D)
    tgt = tgt.reshape(L, D)

    big2d = {n: p[n].reshape(p[n].shape[-2], p[n].shape[-1]) for n in _BIG}
    slots = {n: _cast_into_slot("cast_" + n, big2d[n], k_idx) for n in _BIG if n != "w_in"}
    slots["w_in"], w_in_own = _cast_into_slot("cast_w_in", big2d["w_in"], k_idx, also_alone=True)
    wg = {}
    for cid, (gname, group) in enumerate((("allgather_w_in", ("w_in",)),
                                          ("allgather_w_mixer", ("w_glu", "w_ssm_out", "w_conv_out", "w_o")),
                                          ("allgather_w_up", ("w_up",)), ("allgather_w_down", ("w_down",)))):
        wg.update(zip(group, _allgather_weights(gname, [slots[n] for n in group], cid)))
    w_in, w_so, w_co, w_up = wg["w_in"], wg["w_ssm_out"], wg["w_conv_out"], wg["w_up"]
    w_glu, w_o, w_down = wg["w_glu"], wg["w_o"], wg["w_down"]
    kk = k_idx[0]
    cw_full = lax.dynamic_update_slice(jnp.zeros((3, CW), f32), p["conv_w"].reshape(3, CW // _NCHIP), (0, kk * (CW // _NCHIP)))
    fw_full = lax.dynamic_update_slice(jnp.zeros((3, F), f32), p["ffn_conv_w"].reshape(3, F // _NCHIP), (0, kk * (F // _NCHIP)))
    south = (ci == 0).astype(f32)
    filters8 = _allgather_small("allgather_conv_filters", _pack([cw_full * south, fw_full * south]), 17)
    conv_b = p["conv_b"].reshape(1, CW)
    ffn_conv_b = p["ffn_conv_b"].reshape(1, F)
    norm_tok = p["norm_tok"].reshape(1, D)
    norm_ffn = p["norm_ffn"].reshape(1, D)
    norm_final = p["norm_final"].reshape(1, D)
    dskip = p["d_skip"].reshape(1, SW)

    s5_in = (p["a_re"][0], p["a_im"][0], p["log_dt"][0], p["b_re"][0], p["b_im"][0])
    (abr, abi, bbr, bbi), disc_vjp = jax.vjp(_discretize, *s5_in)
    tabs = _s5_tables(cfg, abr, abi, bbr, bbi, p["c_re"][0], p["c_im"][0])

    xn1, r1 = _rms_fwd("rms_tok", x, norm_tok)
    proj = _in_proj(cfg, xn1, w_in_own, w_in, k_idx)
    conv_w, ffn_conv_w = _unpack(_sum8("sum_conv_filters", filters8, [proj]), [(3, CW), (3, F)])
    y_s, ya0, cin, xs = _s5_fwd(cfg, proj, tabs, dskip)

    def tiles(*arrs):
        return lambda tm, tn: [(a, pl.BlockSpec((tm, tn), lambda i, j: (i, j))) for a in arrs]

    def glu_epi(acc, e, o):
        o[0][...] = (e[0][...].astype(f32) * jax.nn.sigmoid(acc)).astype(bf16)
        o[1][...] = acc.astype(bf16)

    ya1, z = _mm_nn("glu", ya0, w_glu, [bf16, bf16], rows=True, extras_fn=tiles(ya0), epilogue=glu_epi)
    yb0 = _convb_fwd(cfg, proj, conv_w, conv_b)
    merged, ya, yb = _merge_fwd(cfg, ya1, yb0, proj, w_so, w_co)

    def res_epi(acc, e, o):
        o[0][...] = e[0][...] + acc

    h1 = _mm_nn("out_proj", merged, w_o, [f32], rows=True, extras_fn=tiles(x), epilogue=res_epi)[0]
    xn2, r2 = _rms_fwd("rms_ffn", h1, norm_ffn)
    hh = _mm_nn("ffn_up", xn2, w_up, [bf16], tn=1408)[0]
    fact, ffn_pre = _ffn_act(cfg, hh, ffn_conv_w, ffn_conv_b)
    h2 = _mm_nn("ffn_down", fact, w_down, [f32], tm=512, rows=True, extras_fn=tiles(h1), epilogue=res_epi)[0]
    dh2, dh2b, g_norm_final, loss_tile = _loss_head("loss_head", h2, tgt, norm_final)

    kc_idx = jnp.concatenate([k_idx, c_idx])
    reduced, chains = {}, {}

    def rs_halves(tag, collective_id, names, gs):
        chains[tag] = dict(cid=collective_id, names=names, gs=gs,
                           sib=_exchange_halves("grad_halves_" + tag, gs, collective_id))

    def rs_shards(tag, after):
        ch = chains[tag]
        ch["parts"] = [_add_own_half("grad_add_halves_" + n, g, t, kc_idx, after)
                       for n, g, t in zip(ch["names"], ch["gs"], ch["sib"])]
        ch["chips"] = _scatter_shards("grad_shards_" + tag, ch["parts"], ch["cid"] + 1)
        return ch["parts"]

    def rs_join(tag, after):
        ch = chains[tag]
        ch["halves"] = [_add_shard_parts("grad_add_chips_" + n, g, t, r, kc_idx, after)
                        for n, g, t, r in zip(ch["names"], ch["gs"], ch["sib"], ch["chips"])]
        reduced.update(zip(ch["names"], _join_halves("grad_join_" + tag, ch["halves"], ch["cid"] + 2)))
        return ch["halves"]

    df = _mm_nt("ffn_down_dx", dh2b, w_down, [bf16], tn=1408, rows=True)[0]
    g_w_down = _mm_tn("ffn_down_dw", fact, dh2b, tm=1408, tn=512)
    rs_halves("ffn_down", 4, ["w_down"], [g_w_down.reshape(_NCHIP, F // _NCHIP, D)])
    dhh, g_ffn_conv_w, g_ffn_conv_b = _ffn_act_bwd(cfg, hh, ffn_pre, df, ffn_conv_w)
    sent = rs_shards("ffn_down", [dhh])
    g_w_up = _mm_tn("ffn_up_dw", xn2, dhh, shards=_NCHIP, tm=512, tn=1408, b_resident=True, after=sent)
    rs_halves("ffn_up", 7, ["w_up"], [g_w_up])
    dxn2 = _ffn_up_dx(cfg, dhh, w_up)
    sent = rs_shards("ffn_up", [dxn2]) + rs_join("ffn_down", [dxn2])
    dh1, dh1b, g_norm_ffn = _rms_bwd("rms_ffn_bwd", dxn2, h1, r2, norm_ffn, dh2, after=sent)

    dya, dyb, dproj = _out_proj_dx(cfg, dh1b, w_o, ya, yb, proj)
    g_w_o = _mm_tn("out_proj_dw", merged, dh1b)

    def glu_bwd_epi(acc, e, o):
        a0 = e[0][...].astype(f32)
        s = jax.nn.sigmoid(e[1][...].astype(f32))
        o[0][...] = (acc * a0 * s * (1.0 - s)).astype(bf16)
        o[1][...] = (acc * s).astype(bf16)

    dz, t1 = _mm_nt("ssm_out_dx", dya, w_so, [bf16, bf16], extras_fn=tiles(ya0, z), epilogue=glu_bwd_epi)
    g_w_so = _mm_tn("ssm_out_dw", ya1, dya, shards=_NCHIP, tn=512)
    dyb0 = _mm_nt("conv_out_dx", dyb, w_co, [bf16])[0]
    g_w_co = _mm_tn("conv_out_dw", yb0, dyb, shards=_NCHIP, tn=512)
    dproj, g_conv_w, g_conv_b = _convb_bwd(cfg, proj, dyb0, conv_w, conv_b, dproj)

    def gelu_bwd_epi(acc, e, o):
        o[0][...] = ((e[0][...].astype(f32) + acc) * _gelu_grad(e[1][...].astype(f32))).astype(bf16)

    dy_s = _mm_nt("glu_dx", dz, w_glu, [bf16], rows=True, extras_fn=tiles(t1, y_s), epilogue=gelu_bwd_epi)[0]
    g_w_glu = _mm_tn("glu_dw", ya0, dz)
    rs_halves("mixer", 10, ["w_o", "w_ssm_out", "w_conv_out", "w_glu"],
              [g_w_o.reshape(_NCHIP, D // _NCHIP, D), g_w_so, g_w_co, g_w_glu.reshape(_NCHIP, SW // _NCHIP, SW)])
    sent = rs_join("ffn_up", [g_w_glu])
    dproj, da_acc, db_full, dc_full, g_dskip = _s5_bwd(cfg, proj, dy_s, cin, xs, tabs, dskip, dproj, after=sent)
    sent = rs_shards("mixer", [dproj])

    dabr, dabi, dbbr, dbbi, g_c_re, g_c_im = _s5_param_grads(cfg, da_acc, db_full, dc_full)
    g_a_re, g_a_im, g_log_dt, g_b_re, g_b_im = disc_vjp((dabr, dabi, dbbr, dbbi))
    small_g = {"a_re": g_a_re, "a_im": g_a_im, "log_dt": g_log_dt, "b_re": g_b_re, "b_im": g_b_im,
               "c_re": g_c_re, "c_im": g_c_im, "d_skip": g_dskip, "conv_w": g_conv_w, "conv_b": g_conv_b,
               "norm_ffn": g_norm_ffn, "ffn_conv_w": g_ffn_conv_w, "ffn_conv_b": g_ffn_conv_b, "norm_final": g_norm_final}
    early = [n for n in _SMALL if n != "norm_tok"]
    small8 = _allgather_small("allgather_small_grads", _pack([small_g[n] for n in early]), 16)

    g_w_in = _mm_tn("in_proj_dw", xn1, dproj, shards=_NCHIP, tn=CW, after=sent,
                    b_block=lambda j: jnp.where(j == 0, 3 * CW // SW, jnp.where(j < 4, j - 1, j)))
    rs_halves("in_proj", 13, ["w_in"], [g_w_in])
    dxn1 = _in_proj_dx(cfg, dproj, w_in)
    sent = rs_shards("in_proj", [dxn1]) + rs_join("mixer", [dxn1])
    dx, _, g_norm_tok = _rms_bwd("rms_tok_bwd", dxn1, x, r1, norm_tok, dh1, after=sent)

    summed = dict(zip(early, _unpack(_sum8("sum_small_grads", small8, [dx]), [small_g[n].shape for n in early])))
    summed["norm_tok"] = _unpack(_allreduce8("allreduce_norm_tok", _pack([g_norm_tok])), [g_norm_tok.shape])[0]
    summed["conv_w"] = lax.dynamic_slice(summed["conv_w"], (0, kk * (CW // _NCHIP)), (3, CW // _NCHIP))
    summed["ffn_conv_w"] = lax.dynamic_slice(summed["ffn_conv_w"], (0, kk * (F // _NCHIP)), (3, F // _NCHIP))

    grads, deltas, new_m, new_v = {}, {}, {}, {}

    def adamw_big(names, after):
        for n in names:
            g_, d_, m_, v_ = _adamw("adamw_" + n, big2d[n], reduced[n], m[n].reshape(big2d[n].shape),
                                    v[n].reshape(big2d[n].shape), after=after)
            grads[n], deltas[n], new_m[n], new_v[n] = (a.reshape(p[n].shape) for a in (g_, d_, m_, v_))
            after = [d_]
        return after

    for n in _SMALL:
        grads[n] = summed[n].reshape(p[n].shape)
        deltas[n], new_m[n], new_v[n] = _adamw_whole("adamw_" + n, p[n], grads[n], m[n], v[n])
    done = adamw_big(["w_down", "w_up", "w_o", "w_ssm_out", "w_conv_out", "w_glu"], [deltas["norm_final"]])
    rs_join("in_proj", done + [deltas[n] for n in _SMALL])
    adamw_big(["w_in"], ())

    loss = lax.psum(loss_tile[0, 0], ("x", "y", "c"))
    return (loss, dx.reshape(1, L, D), *[grads[n] for n in _WEIGHTS], *[deltas[n] for n in _WEIGHTS],
            *[new_m[n] for n in _WEIGHTS], *[new_v[n] for n in _WEIGHTS])


def kernel(x, norm_tok, w_in, a_re, a_im, log_dt, b_re, b_im, c_re, c_im, d_skip, w_glu, w_ssm_out, conv_w, conv_b, w_conv_out, w_o, norm_ffn, w_up, ffn_conv_w, ffn_conv_b, w_down, norm_final, loss_target, m_norm_tok, m_w_in, m_a_re, m_a_im, m_log_dt, m_b_re, m_b_im, m_c_re, m_c_im, m_d_skip, m_w_glu, m_w_ssm_out, m_conv_w, m_conv_b, m_w_conv_out, m_w_o, m_norm_ffn, m_w_up, m_ffn_conv_w, m_ffn_conv_b, m_w_down, m_norm_final, v_norm_tok, v_w_in, v_a_re, v_a_im, v_log_dt, v_b_re, v_b_im, v_c_re, v_c_im, v_d_skip, v_w_glu, v_w_ssm_out, v_conv_w, v_conv_b, v_w_conv_out, v_w_o, v_norm_ffn, v_w_up, v_ffn_conv_w, v_ffn_conv_b, v_w_down, v_norm_final):
    p = dict(norm_tok=norm_tok, w_in=w_in, a_re=a_re, a_im=a_im, log_dt=log_dt, b_re=b_re, b_im=b_im, c_re=c_re,
             c_im=c_im, d_skip=d_skip, w_glu=w_glu, w_ssm_out=w_ssm_out, conv_w=conv_w, conv_b=conv_b,
             w_conv_out=w_conv_out, w_o=w_o, norm_ffn=norm_ffn, w_up=w_up, ffn_conv_w=ffn_conv_w,
             ffn_conv_b=ffn_conv_b, w_down=w_down, norm_final=norm_final)
    m = dict(norm_tok=m_norm_tok, w_in=m_w_in, a_re=m_a_re, a_im=m_a_im, log_dt=m_log_dt, b_re=m_b_re, b_im=m_b_im,
             c_re=m_c_re, c_im=m_c_im, d_skip=m_d_skip, w_glu=m_w_glu, w_ssm_out=m_w_ssm_out, conv_w=m_conv_w,
             conv_b=m_conv_b, w_conv_out=m_w_conv_out, w_o=m_w_o, norm_ffn=m_norm_ffn, w_up=m_w_up,
             ffn_conv_w=m_ffn_conv_w, ffn_conv_b=m_ffn_conv_b, w_down=m_w_down, norm_final=m_norm_final)
    v = dict(norm_tok=v_norm_tok, w_in=v_w_in, a_re=v_a_re, a_im=v_a_im, log_dt=v_log_dt, b_re=v_b_re, b_im=v_b_im,
             c_re=v_c_re, c_im=v_c_im, d_skip=v_d_skip, w_glu=v_w_glu, w_ssm_out=v_w_ssm_out, conv_w=v_conv_w,
             conv_b=v_conv_b, w_conv_out=v_w_conv_out, w_o=v_w_o, norm_ffn=v_norm_ffn, w_up=v_w_up,
             ffn_conv_w=v_ffn_conv_w, ffn_conv_b=v_ffn_conv_b, w_down=v_w_down, norm_final=v_norm_final)
    return _step(_Cfg(), x, loss_target, p, m, v)
```

```python
import functools
import math
from typing import NamedTuple

import jax
import jax.numpy as jnp
from jax import lax
from jax.experimental import pallas as pl
from jax.experimental.pallas import tpu as pltpu
from jax.experimental.pallas import tpu_sc as plsc

f32 = jnp.float32
bf16 = jnp.bfloat16
_MESH = pl.DeviceIdType.MESH

_EPS = 1e-6
_ADAM_LR = 0.001
_ADAM_B1 = 0.9
_ADAM_B2 = 0.999
_ADAM_EPS = 1e-08
_ADAM_WD = 0.01
_ADAM_STEP = 10
_SSM_GROUP = 16
_SSM_STATE = 64
_SLAB_GROUPS = 16
_NCHIP = 4
_VMEM_LIMIT = 56 * 2**20
_VMEM_S5_BWD = 62 * 2**20
_GELU_C = math.sqrt(2.0 / math.pi)
_GELU_A = 0.044715


class _Cfg(NamedTuple):
    L: int = 4096
    D: int = 2048
    SW: int = 1024
    CW: int = 1024
    F: int = 5632
    T: int = 256


def _tile(n, pref, align):
    t = min(n, pref)
    t -= t % align
    while t > align and n % t:
        t -= align
    assert t > 0 and n % t == 0, (n, pref, align)
    return t


def _cparams(sem, vmem_limit=_VMEM_LIMIT):
    return pltpu.CompilerParams(dimension_semantics=sem, vmem_limit_bytes=vmem_limit)


def _gelu(x):
    return _gelu_and_grad(x)[0]


def _gelu_grad(x):
    return _gelu_and_grad(x)[1]


def _gelu_and_grad(x):
    x2 = x * x
    th = jnp.tanh(x * (_GELU_C + (_GELU_C * _GELU_A) * x2))
    half = 0.5 + 0.5 * th
    return x * half, half + (0.5 * x) * (1.0 - th * th) * (_GELU_C + (3.0 * _GELU_C * _GELU_A) * x2)


_NN = (((1,), (0,)), ((), ()))
_NT = (((1,), (1,)), ((), ()))
_TN = (((0,), (0,)), ((), ()))


def _whole(ref):
    return ref[...]


_ANY = pl.BlockSpec(memory_space=pl.ANY)


def _mm(name, operands, steps, *, grid, contract, outs, extras=(), epilogue=None, acc_shape=None, after=()):
    nop, ne, na = len(operands), len(extras), len(after)
    nk = len(steps)

    def body(*refs):
        op_refs = refs[:nop]
        e_refs = refs[nop:nop + ne]
        o_refs = refs[nop + ne + na:nop + ne + na + len(outs)]

        def partial(terms):
            tot = None
            for ai, av, bi, bv in terms:
                d = lax.dot_general(av(op_refs[ai]), bv(op_refs[bi]), contract, preferred_element_type=f32)
                tot = d if tot is None else tot + d
            return tot

        def finish(res):
            if epilogue is None:
                for o in o_refs:
                    o[...] = res.astype(o.dtype)
            else:
                epilogue(res, e_refs, o_refs)

        if nk == 1:
            finish(partial(steps[0][1]))
            return
        acc = refs[-1]
        kid = pl.program_id(len(grid) - 1)
        for k, terms in steps:
            def run(k=k, terms=terms):
                d = partial(terms)
                if k == 0:
                    acc[...] = d
                elif k < nk - 1:
                    acc[...] += d
                else:
                    finish(acc[...] + d)

            pl.when(kid == k)(run)

    sem = ("parallel",) * (len(grid) - (nk > 1)) + (("arbitrary",) if nk > 1 else ())
    return pl.pallas_call(
        body, grid=grid, in_specs=[o[1] for o in operands] + [e[1] for e in extras] + [_ANY] * na,
        out_specs=[o[1] for o in outs], out_shape=[o[0] for o in outs],
        scratch_shapes=[pltpu.VMEM(acc_shape, f32)] if nk > 1 else [], name=name,
        compiler_params=_cparams(sem))(*[o[0] for o in operands], *[e[0] for e in extras], *after)


def _mm_nn(name, a, w, out_dtypes, *, tm=1024, tn=1024, rows=False, extras_fn=None, epilogue=None):
    M, K = a.shape
    S, Ns = w.shape[0], w.shape[-1]
    N = Ns if rows else Ns * S
    tm, tn = _tile(M, tm, 16), _tile(Ns, tn, 128)
    nb = Ns // tn
    a_spec = pl.BlockSpec((tm, K), lambda i, j: (i, 0))
    if rows:
        b_spec = pl.BlockSpec((S, K // S, tn), lambda i, j: (0, 0, j))
        b_view = lambda r: r[...].reshape(K, tn)
    else:
        b_spec = pl.BlockSpec((None, K, tn), lambda i, j: (j // nb, 0, j % nb))
        b_view = _whole
    o_spec = pl.BlockSpec((tm, tn), lambda i, j: (i, j))
    outs = [(jax.ShapeDtypeStruct((M, N), dt), o_spec) for dt in out_dtypes]
    extras = extras_fn(tm, tn) if extras_fn is not None else ()
    return _mm(name, [(a, a_spec), (w, b_spec)], [(None, [(0, _whole, 1, b_view)])], grid=(M // tm, N // tn),
               contract=_NN, outs=outs, extras=extras, epilogue=epilogue)


def _mm_nt(name, a, w, out_dtypes, *, tm=1024, tn=1024, rows=False, extras_fn=None, epilogue=None):
    M, N = a.shape
    S, Ks, Ns = w.shape
    K = Ks * S if rows else Ks
    tm = _tile(M, tm, 16)
    a_spec = pl.BlockSpec((tm, N), lambda i, j: (i, 0))
    if rows:
        whole_shards = tn > Ks and tn % Ks == 0 and K % tn == 0
        tn = K if tn >= K else (tn if whole_shards else _tile(Ks, tn, 128))
        if tn == K or whole_shards:
            b_spec = pl.BlockSpec((tn // Ks, Ks, N), lambda i, j: (j, 0, 0))
            terms = [(0, _whole, 1, lambda r: r[...].reshape(tn, N))]
        else:
            nbs = Ks // tn
            b_spec = pl.BlockSpec((None, tn, N), lambda i, j: (j // nbs, j % nbs, 0))
            terms = [(0, _whole, 1, _whole)]
    else:
        tn = _tile(K, tn, 128)
        assert S * Ns == N
        b_spec = pl.BlockSpec((S, tn, Ns), lambda i, j: (0, j, 0))
        terms = [(0, lambda r, s=s: r[:, s * Ns:(s + 1) * Ns], 1, lambda r, s=s: r[s]) for s in range(S)]
    o_spec = pl.BlockSpec((tm, tn), lambda i, j: (i, j))
    outs = [(jax.ShapeDtypeStruct((M, K), dt), o_spec) for dt in out_dtypes]
    extras = extras_fn(tm, tn) if extras_fn is not None else ()
    return _mm(name, [(a, a_spec), (w, b_spec)], [(None, terms)], grid=(M // tm, K // tn), contract=_NT,
               outs=outs, extras=extras, epilogue=epilogue)


def _mm_tn(name, a, b, *, shards=None, tm=1024, tn=1024, b_block=None, b_resident=False, after=()):
    M, K = a.shape
    halves = b.shape[0] if b.ndim == 3 else 1
    Nh = b.shape[-1]
    N = Nh * halves
    Ns = N // shards if shards else N
    tm, tn = _tile(K, tm, 128), _tile(math.gcd(Ns, Nh), tn, 128)
    nb, nbh = Ns // tn, Nh // tn
    ij = (lambda g0, g1: (g1, g0)) if b_resident else (lambda g0, g1: (g0, g1))
    bmap = b_block if b_block is not None else (lambda j: j)
    a_spec = pl.BlockSpec((M, tm), lambda g0, g1: (0, ij(g0, g1)[0]))
    if halves > 1:
        b_spec = pl.BlockSpec((None, M, tn), lambda g0, g1: (bmap(ij(g0, g1)[1]) // nbh, 0, bmap(ij(g0, g1)[1]) % nbh))
    else:
        b_spec = pl.BlockSpec((M, tn), lambda g0, g1: (0, bmap(ij(g0, g1)[1])))
    if shards:
        out = (jax.ShapeDtypeStruct((shards, K, Ns), f32),
               pl.BlockSpec((None, tm, tn), lambda g0, g1: (ij(g0, g1)[1] // nb, ij(g0, g1)[0], ij(g0, g1)[1] % nb)))
    else:
        out = (jax.ShapeDtypeStruct((K, N), f32), pl.BlockSpec((tm, tn), lambda g0, g1: ij(g0, g1)))
    grid = (N // tn, K // tm) if b_resident else (K // tm, N // tn)
    return _mm(name, [(a, a_spec), (b, b_spec)], [(None, [(0, _whole, 1, _whole)])], grid=grid, contract=_TN,
               outs=[out], after=after)[0]


def _in_proj_dx(cfg, dproj, w_in):
    L, D, SW, CW = cfg.L, cfg.D, cfg.SW, cfg.CW
    NP = SW + 3 * CW + 2 * D
    Ns = NP // _NCHIP
    assert SW + CW == Ns and 2 * CW == Ns and D == Ns
    tm, tn = _tile(L, 1024, 16), _tile(D, 1024, 128)
    a_spec = pl.BlockSpec((tm, NP // 2), lambda i, j, k: (i, k))
    b_spec = pl.BlockSpec((2, tn, Ns), lambda i, j, k: (k, j, 0))
    first = [(0, lambda r: r[:, 0:CW], 1, lambda r: r[0, :, SW:SW + CW]),
             (0, lambda r: r[:, CW:3 * CW], 1, lambda r: r[1]),
             (0, lambda r: r[:, 3 * CW:3 * CW + SW], 1, lambda r: r[0, :, 0:SW])]
    second = [(0, lambda r: r[:, 0:D], 1, lambda r: r[0]), (0, lambda r: r[:, D:2 * D], 1, lambda r: r[1])]
    out = (jax.ShapeDtypeStruct((L, D), bf16), pl.BlockSpec((tm, tn), lambda i, j, k: (i, j)))
    return _mm("in_proj_dx", [(dproj, a_spec), (w_in, b_spec)], [(0, first), (1, second)], grid=(L // tm, D // tn, 2),
               contract=_NT, outs=[out], acc_shape=(tm, tn))[0]


def _out_proj_dx(cfg, dh1b, w_o, ya, yb, proj):
    L, D = cfg.L, cfg.D
    NP = cfg.SW + 3 * cfg.CW + 2 * D
    assert NP == 4 * D
    tm = _tile(L, 512, 16)

    def epilogue(acc, e, o):
        sa = jax.nn.sigmoid(e[2][:, 0:D].astype(f32))
        sb = jax.nn.sigmoid(e[2][:, D:2 * D].astype(f32))
        o[0][...] = (acc * sa).astype(bf16)
        o[1][...] = (acc * sb).astype(bf16)
        o[2][:, 0:D] = (acc * e[0][...].astype(f32) * sa * (1.0 - sa)).astype(bf16)
        o[2][:, D:2 * D] = (acc * e[1][...].astype(f32) * sb * (1.0 - sb)).astype(bf16)

    row = pl.BlockSpec((tm, D), lambda i, j: (i, 0))
    half = pl.BlockSpec((tm, 2 * D), lambda i, j: (i, 1))
    return _mm("out_proj_dx", [(dh1b, row), (w_o, pl.BlockSpec(w_o.shape, lambda i, j: (0, 0, 0),
                                                                pipeline_mode=pl.Buffered(1)))],
               [(None, [(0, _whole, 1, lambda r: r[...].reshape(D, D))])], grid=(L // tm, 1), contract=_NT,
               outs=[(jax.ShapeDtypeStruct((L, D), bf16), row), (jax.ShapeDtypeStruct((L, D), bf16), row),
                     (jax.ShapeDtypeStruct((L, NP), bf16), half)],
               extras=[(ya, row), (yb, row), (proj, half)], epilogue=epilogue)


def _ffn_up_dx(cfg, dhh, w_up):
    L, D, F = cfg.L, cfg.D, cfg.F
    Fh = F // 2
    tm, tn = _tile(L, 1024, 16), _tile(D, 512, 128)
    a_spec = pl.BlockSpec((None, tm, F), lambda i, j, k: (k, i, 0))
    b_spec = pl.BlockSpec((2, tn, Fh), lambda i, j, k: (k, j, 0))
    terms = [(0, lambda r: r[:, 0:Fh], 1, lambda r: r[0]), (0, lambda r: r[:, Fh:F], 1, lambda r: r[1])]
    out = (jax.ShapeDtypeStruct((L, D), bf16), pl.BlockSpec((tm, tn), lambda i, j, k: (i, j)))
    return _mm("ffn_up_dx", [(dhh, a_spec), (w_up, b_spec)], [(0, terms), (1, terms)], grid=(L // tm, D // tn, 2),
               contract=_NT, outs=[out], acc_shape=(tm, tn))[0]


def _rms_fwd(name, x, g):
    L, D = x.shape
    tm = _tile(L, 512, 16)

    def body(x_ref, g_ref, xn_ref, r_ref):
        xv = x_ref[...]
        r = lax.rsqrt(jnp.mean(xv * xv, axis=-1, keepdims=True) + _EPS)
        xn_ref[...] = (xv * r * g_ref[...]).astype(bf16)
        r_ref[...] = r

    return pl.pallas_call(
        body, grid=(L // tm,),
        in_specs=[pl.BlockSpec((tm, D), lambda i: (i, 0)), pl.BlockSpec((1, D), lambda i: (0, 0))],
        out_specs=[pl.BlockSpec((tm, D), lambda i: (i, 0)), pl.BlockSpec((tm, 1), lambda i: (i, 0))],
        out_shape=[jax.ShapeDtypeStruct((L, D), bf16), jax.ShapeDtypeStruct((L, 1), f32)],
        name=name, compiler_params=_cparams(("parallel",)))(x, g)


def _rms_bwd(name, dxn, h, r, g, dres, after=()):
    L, D = h.shape
    tm = _tile(L, 512, 16)

    def body(dxn_ref, h_ref, r_ref, g_ref, dres_ref, *rest):
        dh_ref, dhb_ref, dg_ref = rest[len(after):]
        i = pl.program_id(0)
        d = dxn_ref[...].astype(f32)
        hv = h_ref[...]
        rv = r_ref[...]
        dyg = d * g_ref[...]
        m = jnp.mean(dyg * hv, axis=-1, keepdims=True)
        dh = dres_ref[...] + rv * dyg - hv * (rv * rv * rv) * m
        dh_ref[...] = dh
        dhb_ref[...] = dh.astype(bf16)

        @pl.when(i == 0)
        def _():
            dg_ref[...] = jnp.zeros_like(dg_ref)

        dg_ref[...] += jnp.sum(d * hv * rv, axis=0, keepdims=True)

    row = lambda i: (i, 0)
    return pl.pallas_call(
        body, grid=(L // tm,),
        in_specs=[pl.BlockSpec((tm, D), row), pl.BlockSpec((tm, D), row), pl.BlockSpec((tm, 1), row),
                  pl.BlockSpec((1, D), lambda i: (0, 0)), pl.BlockSpec((tm, D), row)] + [_ANY] * len(after),
        out_specs=[pl.BlockSpec((tm, D), row), pl.BlockSpec((tm, D), row), pl.BlockSpec((1, D), lambda i: (0, 0))],
        out_shape=[jax.ShapeDtypeStruct((L, D), f32), jax.ShapeDtypeStruct((L, D), bf16), jax.ShapeDtypeStruct((1, D), f32)],
        name=name, compiler_params=_cparams(("arbitrary",)))(dxn, h, r, g, dres, *after)


def _loss_head(name, h2, tgt, g):
    L, D = h2.shape
    tm = _tile(L, 512, 16)

    def body(h_ref, t_ref, g_ref, dh_ref, dhb_ref, dg_ref, loss_ref):
        i = pl.program_id(0)
        hv = h_ref[...]
        gv = g_ref[...]
        r = lax.rsqrt(jnp.mean(hv * hv, axis=-1, keepdims=True) + _EPS)
        err = hv * r * gv - t_ref[...]
        dy = err * (1.0 / D)
        dyg = dy * gv
        m = jnp.mean(dyg * hv, axis=-1, keepdims=True)
        dh = r * dyg - hv * (r * r * r) * m
        dh_ref[...] = dh
        dhb_ref[...] = dh.astype(bf16)

        @pl.when(i == 0)
        def _():
            dg_ref[...] = jnp.zeros_like(dg_ref)
            loss_ref[...] = jnp.zeros_like(loss_ref)

        dg_ref[...] += jnp.sum(dy * hv * r, axis=0, keepdims=True)
        part = jnp.sum(jnp.sum(err * err, axis=-1, keepdims=True), axis=0, keepdims=True) * (0.5 / D)
        loss_ref[...] += jnp.broadcast_to(part, (8, 128))

    row = lambda i: (i, 0)
    return pl.pallas_call(
        body, grid=(L // tm,),
        in_specs=[pl.BlockSpec((tm, D), row), pl.BlockSpec((tm, D), row), pl.BlockSpec((1, D), lambda i: (0, 0))],
        out_specs=[pl.BlockSpec((tm, D), row), pl.BlockSpec((tm, D), row), pl.BlockSpec((1, D), lambda i: (0, 0)),
                   pl.BlockSpec((8, 128), lambda i: (0, 0))],
        out_shape=[jax.ShapeDtypeStruct((L, D), f32), jax.ShapeDtypeStruct((L, D), bf16),
                   jax.ShapeDtypeStruct((1, D), f32), jax.ShapeDtypeStruct((8, 128), f32)],
        name=name, compiler_params=_cparams(("arbitrary",)))(h2, tgt, g)


def _shift_down(tile, halo, k, rows8):
    tm = tile.shape[0]
    r = pltpu.roll(tile, k, axis=0)
    hh = pltpu.roll(halo, k, axis=0)
    top = jnp.where(rows8 < k, hh, r[:8])
    return jnp.concatenate([top, r[8:]], axis=0) if tm > 8 else top


def _shift_up(tile, halo, k, rows8):
    tm = tile.shape[0]
    r = pltpu.roll(tile, tm - k, axis=0)
    hh = pltpu.roll(halo, 8 - k, axis=0)
    bot = jnp.where(rows8 >= 8 - k, hh, r[tm - 8:])
    return jnp.concatenate([r[:tm - 8], bot], axis=0) if tm > 8 else bot


def _conv3(x, halo, w_ref, b_ref, rows8):
    return (w_ref[0:1, :] * _shift_down(x, halo, 2, rows8) + w_ref[1:2, :] * _shift_down(x, halo, 1, rows8)
            + w_ref[2:3, :] * x + b_ref[...])


def _convb_fwd(cfg, proj, w, b):
    L, CW = cfg.L, cfg.CW
    assert cfg.SW == CW
    tm = _tile(L, 512, 16)

    def body(v_ref, vh_ref, gb_ref, gc_ref, gch_ref, w_ref, b_ref, o_ref):
        i = pl.program_id(0)
        rows8 = lax.broadcasted_iota(jnp.int32, (8, CW), 0)
        cv = gc_ref[...].astype(f32) * v_ref[...].astype(f32)
        cvh = gch_ref[...].astype(f32)[8:] * vh_ref[...].astype(f32)[8:]
        cvh = jnp.where(i == 0, 0.0, cvh)
        cc = _conv3(cv, cvh, w_ref, b_ref, rows8)
        o_ref[...] = (gb_ref[...].astype(f32) * cc).astype(bf16)

    blk = lambda col: pl.BlockSpec((tm, CW), lambda i: (i, col))
    halo = lambda col: pl.BlockSpec((16, CW), lambda i: (jnp.maximum(i * (tm // 16) - 1, 0), col))
    return pl.pallas_call(
        body, grid=(L // tm,),
        in_specs=[blk(1), halo(1), blk(2), blk(3), halo(3),
                  pl.BlockSpec((3, CW), lambda i: (0, 0)), pl.BlockSpec((1, CW), lambda i: (0, 0))],
        out_specs=pl.BlockSpec((tm, CW), lambda i: (i, 0)),
        out_shape=jax.ShapeDtypeStruct((L, CW), bf16),
        name="convb_fwd", compiler_params=_cparams(("parallel",)))(proj, proj, proj, proj, proj, w, b)


def _convb_bwd(cfg, proj, dyb0, w, b, dproj):
    L, CW = cfg.L, cfg.CW
    tm = _tile(L, 512, 16)
    nt = L // tm

    def body(v_ref, vh_ref, gb_ref, gbn_ref, gc_ref, gch_ref, d_ref, dn_ref, w_ref, b_ref, dproj_ref,
             o_ref, dw_ref, db_ref):
        i = pl.program_id(0)
        rows8 = lax.broadcasted_iota(jnp.int32, (8, CW), 0)
        v = v_ref[...].astype(f32)
        gb = gb_ref[...].astype(f32)
        gc = gc_ref[...].astype(f32)
        d = d_ref[...].astype(f32)
        cv = gc * v
        cvh = gch_ref[...].astype(f32)[8:] * vh_ref[...].astype(f32)[8:]
        cvh = jnp.where(i == 0, 0.0, cvh)
        s2 = _shift_down(cv, cvh, 2, rows8)
        s1 = _shift_down(cv, cvh, 1, rows8)
        cc = w_ref[0:1, :] * s2 + w_ref[1:2, :] * s1 + w_ref[2:3, :] * cv + b_ref[...]
        dcc = d * gb
        dccn = dn_ref[...].astype(f32)[:8] * gbn_ref[...].astype(f32)[:8]
        dccn = jnp.where(i == nt - 1, 0.0, dccn)
        dcv = (w_ref[2:3, :] * dcc + w_ref[1:2, :] * _shift_up(dcc, dccn, 1, rows8)
               + w_ref[0:1, :] * _shift_up(dcc, dccn, 2, rows8))
        o_ref[:, 0:CW] = (dcv * gc).astype(bf16)
        o_ref[:, CW:2 * CW] = (d * cc).astype(bf16)
        o_ref[:, 2 * CW:3 * CW] = (dcv * v).astype(bf16)

        @pl.when(i == 0)
        def _():
            dw_ref[...] = jnp.zeros_like(dw_ref)
            db_ref[...] = jnp.zeros_like(db_ref)

        dw_ref[0:1, :] += jnp.sum(dcc * s2, axis=0, keepdims=True)
        dw_ref[1:2, :] += jnp.sum(dcc * s1, axis=0, keepdims=True)
        dw_ref[2:3, :] += jnp.sum(dcc * cv, axis=0, keepdims=True)
        db_ref[...] += jnp.sum(dcc, axis=0, keepdims=True)

    blk = lambda col: pl.BlockSpec((tm, CW), lambda i: (i, col))
    prev = lambda col: pl.BlockSpec((16, CW), lambda i: (jnp.maximum(i * (tm // 16) - 1, 0), col))
    nxt = lambda col: pl.BlockSpec((16, CW), lambda i: (jnp.minimum((i + 1) * (tm // 16), L // 16 - 1), col))
    const = lambda r: pl.BlockSpec((r, CW), lambda i: (0, 0))
    return pl.pallas_call(
        body, grid=(nt,),
        in_specs=[blk(1), prev(1), blk(2), nxt(2), blk(3), prev(3), blk(0), nxt(0), const(3), const(1),
                  pl.BlockSpec(memory_space=pl.ANY)],
        out_specs=[pl.BlockSpec((tm, 3 * CW), lambda i: (i, 0)), const(3), const(1)],
        out_shape=[jax.ShapeDtypeStruct(dproj.shape, bf16), jax.ShapeDtypeStruct((3, CW), f32),
                   jax.ShapeDtypeStruct((1, CW), f32)],
        input_output_aliases={10: 0},
        name="convb_bwd", compiler_params=_cparams(("arbitrary",)))(proj, proj, proj, proj, proj, proj, dyb0, dyb0, w, b,
                                                                    dproj)


def _ffn_act(cfg, hh, w, b):
    L, F = cfg.L, cfg.F
    tm = _tile(L, 512, 16)
    tc = _tile(F, 1408, 128)
    ncb = F // tc

    def body(a_ref, ah_ref, g_ref, w_ref, b_ref, o_ref, act_ref):
        i = pl.program_id(0)
        rows8 = lax.broadcasted_iota(jnp.int32, (8, tc), 0)
        a = a_ref[...].astype(f32)
        ah = jnp.where(i == 0, 0.0, ah_ref[...].astype(f32)[8:])
        act = _conv3(a, ah, w_ref, b_ref, rows8)
        act_ref[...] = act.astype(bf16)
        o_ref[...] = (_gelu(act) * g_ref[...].astype(f32)).astype(bf16)

    tile = pl.BlockSpec((tm, tc), lambda i, j: (i, j))
    return pl.pallas_call(
        body, grid=(L // tm, ncb),
        in_specs=[tile, pl.BlockSpec((16, tc), lambda i, j: (jnp.maximum(i * (tm // 16) - 1, 0), j)),
                  pl.BlockSpec((tm, tc), lambda i, j: (i, j + ncb)),
                  pl.BlockSpec((3, tc), lambda i, j: (0, j)), pl.BlockSpec((1, tc), lambda i, j: (0, j))],
        out_specs=[tile, tile], out_shape=[jax.ShapeDtypeStruct((L, F), bf16)] * 2,
        name="ffn_act", compiler_params=_cparams(("parallel", "parallel")))(hh, hh, hh, w, b)


def _ffn_act_bwd(cfg, hh, act, df, w):
    L, F = cfg.L, cfg.F
    tm = _tile(L, 512, 16)
    tc = _tile(F, 1408, 128)
    ncb = F // tc
    nt = L // tm

    def body(a_ref, c_ref, cn_ref, g_ref, gn_ref, d_ref, dn_ref, w_ref, dhh_ref, dw_ref, db_ref):
        i = pl.program_id(1)
        rows8 = lax.broadcasted_iota(jnp.int32, (8, tc), 0)
        a = a_ref[...].astype(f32)
        d = d_ref[...].astype(f32)
        gelu, gelu_d = _gelu_and_grad(c_ref[...].astype(f32))
        dhh_ref[1] = (d * gelu).astype(bf16)
        dact = d * g_ref[...].astype(f32) * gelu_d
        dactn = dn_ref[...].astype(f32)[:8] * gn_ref[...].astype(f32)[:8] * _gelu_grad(cn_ref[...].astype(f32)[:8])
        dactn = jnp.where(i == nt - 1, 0.0, dactn)
        up1 = _shift_up(dact, dactn, 1, rows8)
        up2 = _shift_up(dact, dactn, 2, rows8)
        dhh_ref[0] = (w_ref[2:3, :] * dact + w_ref[1:2, :] * up1 + w_ref[0:1, :] * up2).astype(bf16)

        @pl.when(i == 0)
        def _():
            dw_ref[...] = jnp.zeros_like(dw_ref)
            db_ref[...] = jnp.zeros_like(db_ref)

        dw_ref[0:1, :] += jnp.sum(up2 * a, axis=0, keepdims=True)
        dw_ref[1:2, :] += jnp.sum(up1 * a, axis=0, keepdims=True)
        dw_ref[2:3, :] += jnp.sum(dact * a, axis=0, keepdims=True)
        db_ref[...] += jnp.sum(dact, axis=0, keepdims=True)

    blk = lambda off: pl.BlockSpec((tm, tc), lambda j, i: (i, j + off))
    nxt = lambda off: pl.BlockSpec((16, tc), lambda j, i: (jnp.minimum((i + 1) * (tm // 16), L // 16 - 1), j + off))
    const = lambda r: pl.BlockSpec((r, tc), lambda j, i: (0, j))
    return pl.pallas_call(
        body, grid=(ncb, nt),
        in_specs=[blk(0), blk(0), nxt(0), blk(ncb), nxt(ncb), blk(0), nxt(0), const(3)],
        out_specs=[pl.BlockSpec((2, tm, tc), lambda j, i: (0, i, j)), const(3), const(1)],
        out_shape=[jax.ShapeDtypeStruct((2, L, F), bf16),
                   jax.ShapeDtypeStruct((3, F), f32), jax.ShapeDtypeStruct((1, F), f32)],
        name="ffn_act_bwd", compiler_params=_cparams(("parallel", "arbitrary")))(hh, act, act, hh, hh, df, df, w)


def _merge_fwd(cfg, ya1, yb0, proj, wso, wco):
    L, D, SW, CW = cfg.L, cfg.D, cfg.SW, cfg.CW
    Ns = D // _NCHIP
    tm = _tile(L, 1024, 16)
    tn = _tile(Ns, 512, 128)
    nb = Ns // tn
    off_a = (SW + 3 * CW) // tn
    off_b = (SW + 3 * CW + D) // tn

    def body(a_ref, b_ref, wa_ref, wb_ref, ma_ref, mb_ref, m_ref, ya_ref, yb_ref):
        ya = jnp.dot(a_ref[...], wa_ref[...], preferred_element_type=f32)
        yb = jnp.dot(b_ref[...], wb_ref[...], preferred_element_type=f32)
        sa = jax.nn.sigmoid(ma_ref[...].astype(f32))
        sb = jax.nn.sigmoid(mb_ref[...].astype(f32))
        m_ref[...] = (sa * ya + sb * yb).astype(bf16)
        ya_ref[...] = ya.astype(bf16)
        yb_ref[...] = yb.astype(bf16)

    o_spec = pl.BlockSpec((tm, tn), lambda i, j: (i, j))
    o_shape = jax.ShapeDtypeStruct((L, D), bf16)
    return pl.pallas_call(
        body, grid=(L // tm, D // tn),
        in_specs=[pl.BlockSpec((tm, SW), lambda i, j: (i, 0)), pl.BlockSpec((tm, CW), lambda i, j: (i, 0)),
                  pl.BlockSpec((None, SW, tn), lambda i, j: (j // nb, 0, j % nb)),
                  pl.BlockSpec((None, CW, tn), lambda i, j: (j // nb, 0, j % nb)),
                  pl.BlockSpec((tm, tn), lambda i, j: (i, off_a + j)), pl.BlockSpec((tm, tn), lambda i, j: (i, off_b + j))],
        out_specs=[o_spec, o_spec, o_spec], out_shape=[o_shape, o_shape, o_shape],
        name="merge_fwd", compiler_params=_cparams(("parallel", "parallel")))(ya1, yb0, wso, wco, proj, proj)


def _s5_dims(cfg):
    G = cfg.SW // _SSM_GROUP
    NS = G // _SLAB_GROUPS
    SC = _SLAB_GROUPS * _SSM_GROUP
    SH = _SLAB_GROUPS * _SSM_STATE
    NST = 2 * SH * NS
    return G, NS, SC, SH, NST


def _lane_slabs(cfg, W):
    _, NS, _, SH, _ = _s5_dims(cfg)
    return [(2 * SH * s + w0, 2 * SH * s + SH + w0) for s in range(NS) for w0 in range(0, SH, W)]


def _discretize(a_re, a_im, log_dt, b_re, b_im):
    dt = jnp.exp(log_dt)[:, None]
    mag = jnp.exp(dt * a_re)
    abr = mag * jnp.cos(dt * a_im)
    abi = mag * jnp.sin(dt * a_im)
    nr = abr - 1.0
    ni = abi
    den = a_re * a_re + a_im * a_im
    fr = (nr * a_re + ni * a_im) / den
    fi = (ni * a_re - nr * a_im) / den
    bbr = fr[..., None] * b_re - fi[..., None] * b_im
    bbi = fr[..., None] * b_im + fi[..., None] * b_re
    return abr, abi, bbr, bbi


def _state_rows(cfg, re, im):
    _, NS, _, SH, _ = _s5_dims(cfg)
    return jnp.concatenate([re.reshape(NS, SH), im.reshape(NS, SH)], axis=1).reshape(-1)


def _s5_tables(cfg, abr, abi, bbr, bbi, c_re, c_im):
    G, NS, SC, SH, NST = _s5_dims(cfg)
    S = cfg.T // 8
    eye = jnp.eye(_SLAB_GROUPS, dtype=f32)
    bb = jnp.stack([bbr, bbi]).reshape(2, NS, _SLAB_GROUPS, _SSM_STATE, _SSM_GROUP)
    bs = jnp.einsum("rsgph,gq->sghrqp", bb, eye).reshape(NS, SC, 2 * SH).astype(bf16)
    cc = jnp.stack([c_re, -c_im]).reshape(2, NS, _SLAB_GROUPS, _SSM_GROUP, _SSM_STATE)
    cs = jnp.einsum("rsghp,gq->srqpgh", cc, eye).reshape(NS, 2 * SH, SC).astype(bf16)
    arep = jnp.broadcast_to(_state_rows(cfg, abr, abi)[None, :], (8, NST))
    pr, pi = abr, abi
    for _ in range(S - 1):
        pr, pi = pr * abr - pi * abi, pr * abi + pi * abr
    apow = jnp.broadcast_to(_state_rows(cfg, pr, pi)[None, :], (8, NST))
    t = jnp.arange(cfg.T)
    perm = (t % 8) * S + t // 8
    pm = jax.nn.one_hot(perm, cfg.T, dtype=bf16)
    return bs, cs, arep, apow, pm, pm.T


def _cmul_add(ar, ai, xr, xi, br, bi):
    return ar * xr - ai * xi + br, ar * xi + ai * xr + bi


def _s5_forward_chunk(cfg, W, upb, bs_ref, arep_ref, apow_ref, st, x0, cin_store):
    _, NS, SC, SH, _ = _s5_dims(cfg)
    S = cfg.T // 8
    for s in range(NS):
        st[:, 2 * SH * s:2 * SH * (s + 1)] = jnp.dot(upb[:, SC * s:SC * (s + 1)], bs_ref[s], preferred_element_type=f32)
    rows = lax.broadcasted_iota(jnp.int32, (8, W), 0)
    zero = jnp.zeros((8, W), f32)
    for rc, ic in _lane_slabs(cfg, W):
        ar = arep_ref[:, rc:rc + W]
        ai = arep_ref[:, ic:ic + W]

        def step(i, carry, rc=rc, ic=ic, ar=ar, ai=ai):
            xr, xi = carry
            r0 = pl.multiple_of(i * 8, 8)
            nr, ni = _cmul_add(ar, ai, xr, xi, st[pl.ds(r0, 8), rc:rc + W], st[pl.ds(r0, 8), ic:ic + W])
            st[pl.ds(r0, 8), rc:rc + W] = nr
            st[pl.ds(r0, 8), ic:ic + W] = ni
            return nr, ni

        er, ei = lax.fori_loop(0, S, step, (zero, zero))
        pr = apow_ref[:, rc:rc + W]
        pi = apow_ref[:, ic:ic + W]
        x0r = x0[:, rc:rc + W]
        x0i = x0[:, ic:ic + W]
        cr = jnp.where(rows == 0, x0r, 0.0)
        ci = jnp.where(rows == 0, x0i, 0.0)
        for _ in range(7):
            fr, fi = _cmul_add(pr, pi, cr, ci, er, ei)
            cr = jnp.where(rows == 0, x0r, pltpu.roll(fr, 1, axis=0))
            ci = jnp.where(rows == 0, x0i, pltpu.roll(fi, 1, axis=0))
        fr, fi = _cmul_add(pr, pi, cr, ci, er, ei)
        x0[:, rc:rc + W] = jnp.broadcast_to(fr[7:8, :], (8, W))
        x0[:, ic:ic + W] = jnp.broadcast_to(fi[7:8, :], (8, W))
        cin_store(rc, ic, cr, ci)

        def fix(i, carry, rc=rc, ic=ic, ar=ar, ai=ai):
            kr, ki = carry
            r0 = pl.multiple_of(i * 8, 8)
            nr, ni = ar * kr - ai * ki, ar * ki + ai * kr
            st[pl.ds(r0, 8), rc:rc + W] = st[pl.ds(r0, 8), rc:rc + W] + nr
            st[pl.ds(r0, 8), ic:ic + W] = st[pl.ds(r0, 8), ic:ic + W] + ni
            return nr, ni

        lax.fori_loop(0, S, fix, (cr, ci))


def _s5_fwd(cfg, proj, tabs, dskip):
    L, SW, T = cfg.L, cfg.SW, cfg.T
    G, NS, SC, SH, NST = _s5_dims(cfg)
    bs, cs, arep, apow, pm, pmt = tabs
    W = min(512, SH)
    NC = L // T

    def body(u_ref, pm_ref, pmt_ref, bs_ref, cs_ref, arep_ref, apow_ref, dskip_ref, y_ref, ya0_ref, cin_ref, st, x0):
        c = pl.program_id(0)

        @pl.when(c == 0)
        def _():
            x0[...] = jnp.zeros_like(x0)

        up = jnp.dot(pm_ref[...], u_ref[...], preferred_element_type=f32)
        upb = up.astype(bf16)

        def cin_store(rc, ic, cr, ci):
            cin_ref[0, :, rc:rc + W] = cr
            cin_ref[0, :, ic:ic + W] = ci

        _s5_forward_chunk(cfg, W, upb, bs_ref, arep_ref, apow_ref, st, x0, cin_store)
        yp = jnp.concatenate(
            [jnp.dot(st[:, 2 * SH * s:2 * SH * (s + 1)].astype(bf16), cs_ref[s], preferred_element_type=f32)
             for s in range(NS)], axis=1) + dskip_ref[...] * up
        y = jnp.dot(pmt_ref[...], yp.astype(bf16), preferred_element_type=f32)
        y_ref[...] = y.astype(bf16)
        ya0_ref[...] = _gelu(y).astype(bf16)

    const2 = lambda shape: pl.BlockSpec(shape, lambda c: (0, 0))
    const3 = lambda shape: pl.BlockSpec(shape, lambda c: (0, 0, 0))
    return pl.pallas_call(
        body, grid=(NC,),
        in_specs=[pl.BlockSpec((T, SW), lambda c: (c, 0)), const2((T, T)), const2((T, T)), const3((NS, SC, 2 * SH)),
                  const3((NS, 2 * SH, SC)), const2((8, NST)), const2((8, NST)), const2((1, SW))],
        out_specs=[pl.BlockSpec((T, SW), lambda c: (c, 0)), pl.BlockSpec((T, SW), lambda c: (c, 0)),
                   pl.BlockSpec((1, 8, NST), lambda c: (c, 0, 0)), pl.BlockSpec((T, NST), lambda c: (c, 0))],
        out_shape=[jax.ShapeDtypeStruct((L, SW), bf16), jax.ShapeDtypeStruct((L, SW), bf16),
                   jax.ShapeDtypeStruct((NC, 8, NST), f32), jax.ShapeDtypeStruct((L, NST), f32)],
        scratch_shapes=[pltpu.VMEM((8, NST), f32)],
        name="s5_fwd", compiler_params=_cparams(("arbitrary",)))(proj, pm, pmt, bs, cs, arep, apow, dskip)


def _s5_bwd(cfg, proj, dy, cin, xs, tabs, dskip, dproj, after=()):
    L, SW, T = cfg.L, cfg.SW, cfg.T
    du_col = 3 * cfg.CW // SW
    G, NS, SC, SH, NST = _s5_dims(cfg)
    bs, cs, arep, apow, pm, pmt = tabs
    W = min(512, SH)
    S = T // 8
    NC = L // T

    def body(u_ref, dy_ref, cin_ref, st, pm_ref, pmt_ref, bs_ref, cs_ref, arep_ref, apow_ref, dskip_ref, dproj_ref, *rest):
        du_ref, da_ref, db_ref, dc_ref, dd_ref, gs, g0, db_acc, dc_acc = rest[len(after):]
        c = pl.program_id(0)

        @pl.when(c == 0)
        def _():
            g0[...] = jnp.zeros_like(g0)
            da_ref[...] = jnp.zeros_like(da_ref)
            dd_ref[...] = jnp.zeros_like(dd_ref)
            db_acc[...] = jnp.zeros_like(db_acc)
            dc_acc[...] = jnp.zeros_like(dc_acc)

        up = jnp.dot(pm_ref[...], u_ref[...], preferred_element_type=f32)
        upb = up.astype(bf16)
        dyp = jnp.dot(pm_ref[...], dy_ref[...], preferred_element_type=f32)
        dypb = dyp.astype(bf16)
        for s in range(NS):
            gs[:, 2 * SH * s:2 * SH * (s + 1)] = lax.dot_general(
                dypb[:, SC * s:SC * (s + 1)], cs_ref[s], (((1,), (1,)), ((), ())), preferred_element_type=f32)
        rows = lax.broadcasted_iota(jnp.int32, (8, W), 0)
        zero = jnp.zeros((8, W), f32)
        for rc, ic in _lane_slabs(cfg, W):
            ar = arep_ref[:, rc:rc + W]
            ai = arep_ref[:, ic:ic + W]

            def rstep(k, carry, rc=rc, ic=ic, ar=ar, ai=ai):
                gr, gi = carry
                r0 = pl.multiple_of((S - 1 - k) * 8, 8)
                nr = ar * gr + ai * gi + gs[pl.ds(r0, 8), rc:rc + W]
                ni = ar * gi - ai * gr + gs[pl.ds(r0, 8), ic:ic + W]
                gs[pl.ds(r0, 8), rc:rc + W] = nr
                gs[pl.ds(r0, 8), ic:ic + W] = ni
                return nr, ni

            er, ei = lax.fori_loop(0, S, rstep, (zero, zero))
            pr = apow_ref[:, rc:rc + W]
            pi = apow_ref[:, ic:ic + W]
            g0r = g0[:, rc:rc + W]
            g0i = g0[:, ic:ic + W]
            cr = jnp.where(rows == 7, g0r, 0.0)
            ci = jnp.where(rows == 7, g0i, 0.0)
            for _ in range(7):
                fr = er + pr * cr + pi * ci
                fi = ei + pr * ci - pi * cr
                cr = jnp.where(rows == 7, g0r, pltpu.roll(fr, 7, axis=0))
                ci = jnp.where(rows == 7, g0i, pltpu.roll(fi, 7, axis=0))
            fr = er + pr * cr + pi * ci
            fi = ei + pr * ci - pi * cr
            g0[:, rc:rc + W] = jnp.broadcast_to(fr[0:1, :], (8, W))
            g0[:, ic:ic + W] = jnp.broadcast_to(fi[0:1, :], (8, W))

            def fix(k, carry, rc=rc, ic=ic, ar=ar, ai=ai):
                kr, ki, accr, acci = carry
                i = S - 1 - k
                r0 = pl.multiple_of(i * 8, 8)
                rp = pl.multiple_of((i - 1) * 8, 8)
                nr = ar * kr + ai * ki
                ni = ar * ki - ai * kr
                gr = gs[pl.ds(r0, 8), rc:rc + W] + nr
                gi = gs[pl.ds(r0, 8), ic:ic + W] + ni
                gs[pl.ds(r0, 8), rc:rc + W] = gr
                gs[pl.ds(r0, 8), ic:ic + W] = gi
                xr = st[pl.ds(rp, 8), rc:rc + W]
                xi = st[pl.ds(rp, 8), ic:ic + W]
                return nr, ni, accr + gr * xr + gi * xi, acci + gi * xr - gr * xi

            kr, ki, accr, acci = lax.fori_loop(0, S - 1, fix, (cr, ci, zero, zero))
            nr = ar * kr + ai * ki
            ni = ar * ki - ai * kr
            gr = gs[0:8, rc:rc + W] + nr
            gi = gs[0:8, ic:ic + W] + ni
            gs[0:8, rc:rc + W] = gr
            gs[0:8, ic:ic + W] = gi
            xr = cin_ref[0, :, rc:rc + W]
            xi = cin_ref[0, :, ic:ic + W]
            da_ref[:, rc:rc + W] += accr + gr * xr + gi * xi
            da_ref[:, ic:ic + W] += acci + gi * xr - gr * xi

        dups = []
        for s in range(NS):
            gsb = gs[:, 2 * SH * s:2 * SH * (s + 1)].astype(bf16)
            dups.append(lax.dot_general(gsb, bs_ref[s], (((1,), (1,)), ((), ())), preferred_element_type=f32))
            db_acc[s] += lax.dot_general(upb[:, SC * s:SC * (s + 1)], gsb, (((0,), (0,)), ((), ())),
                                         preferred_element_type=f32)
            dc_acc[s] += lax.dot_general(st[:, 2 * SH * s:2 * SH * (s + 1)].astype(bf16), dypb[:, SC * s:SC * (s + 1)],
                                         (((0,), (0,)), ((), ())), preferred_element_type=f32)
        dup = jnp.concatenate(dups, axis=1) + dskip_ref[...] * dyp
        du_ref[...] = jnp.dot(pmt_ref[...], dup.astype(bf16), preferred_element_type=f32).astype(bf16)
        dd_ref[...] += jnp.sum(dyp * up, axis=0, keepdims=True)

        @pl.when(c == NC - 1)
        def _():
            PS, GH = _SSM_STATE, _SSM_GROUP
            mask_b = (lax.broadcasted_iota(jnp.int32, (SC, SH), 0) // GH
                      == lax.broadcasted_iota(jnp.int32, (SC, SH), 1) // PS)
            mask_c = (lax.broadcasted_iota(jnp.int32, (SH, SC), 0) // PS
                      == lax.broadcasted_iota(jnp.int32, (SH, SC), 1) // GH)
            for s in range(NS):
                for r in range(2):
                    xb = jnp.where(mask_b, db_acc[s, :, r * SH:(r + 1) * SH], 0.0)
                    zb = xb[:, 0:128]
                    for q in range(1, SH // 128):
                        zb = zb + xb[:, q * 128:(q + 1) * 128]
                    db_ref[s, r] = zb + pltpu.roll(zb, PS, axis=1)
                    xc = jnp.where(mask_c, dc_acc[s, r * SH:(r + 1) * SH, :], 0.0)
                    zc = xc[0:PS]
                    for q in range(1, SH // PS):
                        zc = zc + xc[q * PS:(q + 1) * PS]
                    dc_ref[s, r] = zc

    rev = lambda c: (NC - 1 - c, 0)
    const2 = lambda shape: pl.BlockSpec(shape, lambda c: (0, 0))
    const3 = lambda shape: pl.BlockSpec(shape, lambda c: (0, 0, 0))
    const4 = lambda shape: pl.BlockSpec(shape, lambda c: (0, 0, 0, 0))
    return pl.pallas_call(
        body, grid=(NC,),
        in_specs=[pl.BlockSpec((T, SW), rev), pl.BlockSpec((T, SW), rev), pl.BlockSpec((1, 8, NST), lambda c: (NC - 1 - c, 0, 0)),
                  pl.BlockSpec((T, NST), rev), const2((T, T)), const2((T, T)),
                  pl.BlockSpec((NS, SC, 2 * SH), lambda c: (0, 0, 0), pipeline_mode=pl.Buffered(1)),
                  pl.BlockSpec((NS, 2 * SH, SC), lambda c: (0, 0, 0), pipeline_mode=pl.Buffered(1)),
                  const2((8, NST)), const2((8, NST)), const2((1, SW)), _ANY] + [_ANY] * len(after),
        out_specs=[pl.BlockSpec((T, SW), lambda c: (NC - 1 - c, du_col)), const2((8, NST)),
                   const4((NS, 2, SC, 128)), const4((NS, 2, _SSM_STATE, SC)), const2((1, SW))],
        out_shape=[jax.ShapeDtypeStruct(dproj.shape, bf16), jax.ShapeDtypeStruct((8, NST), f32),
                   jax.ShapeDtypeStruct((NS, 2, SC, 128), f32), jax.ShapeDtypeStruct((NS, 2, _SSM_STATE, SC), f32),
                   jax.ShapeDtypeStruct((1, SW), f32)],
        scratch_shapes=[pltpu.VMEM((T, NST), f32), pltpu.VMEM((8, NST), f32),
                        pltpu.VMEM((NS, SC, 2 * SH), f32), pltpu.VMEM((NS, 2 * SH, SC), f32)],
        input_output_aliases={11: 0},
        name="s5_bwd", compiler_params=_cparams(("arbitrary",), _VMEM_S5_BWD))(proj, dy, cin, xs, pm, pmt, bs, cs, arep, apow, dskip, dproj,
                                                                  *after)


def _s5_param_grads(cfg, da, db_diag, dc_diag):
    G, NS, SC, SH, NST = _s5_dims(cfg)
    das = da.sum(axis=0).reshape(NS, 2, SH)
    dabr = das[:, 0].reshape(G, _SSM_STATE)
    dabi = das[:, 1].reshape(G, _SSM_STATE)
    dbd = db_diag[..., :_SSM_STATE].reshape(NS, 2, _SLAB_GROUPS, _SSM_GROUP, _SSM_STATE)
    dbb = dbd.transpose(1, 0, 2, 4, 3).reshape(2, G, _SSM_STATE, _SSM_GROUP)
    dcd = dc_diag.reshape(NS, 2, _SSM_STATE, _SLAB_GROUPS, _SSM_GROUP)
    dcc = dcd.transpose(1, 0, 3, 4, 2).reshape(2, G, _SSM_GROUP, _SSM_STATE)
    return dabr, dabi, dbb[0], dbb[1], dcc[0], -dcc[1]


def _coords():
    return lax.axis_index("x"), lax.axis_index("y"), lax.axis_index("c")


def _other_chips(x, y):
    return [(1 - x, y), (x, 1 - y), (1 - x, 1 - y)]


def _allreduce8(name, v):
    R = v.shape[0]

    def body(v_ref, o_ref, sib, chips, mine, ssem, rsem):
        x, y, c = _coords()
        d2d = pltpu.make_async_remote_copy(src_ref=v_ref, dst_ref=sib, send_sem=ssem.at[0], recv_sem=rsem.at[0],
                                           device_id=(x, y, 1 - c), device_id_type=_MESH)
        d2d.start()
        d2d.wait()
        mine[...] = v_ref[...] + sib[...]
        cps = [pltpu.make_async_remote_copy(src_ref=mine, dst_ref=chips.at[j], send_sem=ssem.at[1 + j],
                                            recv_sem=rsem.at[1 + j], device_id=(*chip, c), device_id_type=_MESH)
               for j, chip in enumerate(_other_chips(x, y))]
        for cp in cps:
            cp.start()
        for cp in cps:
            cp.wait()
        o_ref[...] = (mine[...] + chips[1]) + (chips[0] + chips[2])

    vm = pl.BlockSpec(memory_space=pltpu.VMEM)
    return pl.pallas_call(
        body, in_specs=[vm], out_specs=vm, out_shape=jax.ShapeDtypeStruct((R, 128), f32),
        scratch_shapes=[pltpu.VMEM((R, 128), f32), pltpu.VMEM((3, R, 128), f32), pltpu.VMEM((R, 128), f32),
                        pltpu.SemaphoreType.DMA((4,)), pltpu.SemaphoreType.DMA((4,))],
        name=name, compiler_params=pltpu.CompilerParams(vmem_limit_bytes=_VMEM_LIMIT))(v)


def _cast_into_slot(name, w, k_idx, also_alone=False):
    R, C = w.shape
    tr = _tile(R, 256, 16)

    def body(k_ref, w_ref, *o_refs):
        for o_ref in o_refs:
            o_ref[...] = w_ref[...].astype(bf16)

    slot = (jax.ShapeDtypeStruct((_NCHIP, R, C), bf16), pl.BlockSpec((None, tr, C), lambda r, kr: (kr[0], r, 0)))
    alone = (jax.ShapeDtypeStruct((R, C), bf16), pl.BlockSpec((tr, C), lambda r, kr: (r, 0)))
    outs = [slot, alone] if also_alone else [slot]
    gs = pltpu.PrefetchScalarGridSpec(
        num_scalar_prefetch=1, grid=(R // tr,),
        in_specs=[pl.BlockSpec((tr, C), lambda r, kr: (r, 0))], out_specs=[o[1] for o in outs])
    res = pl.pallas_call(body, grid_spec=gs, out_shape=[o[0] for o in outs], name=name,
                         compiler_params=_cparams(("parallel",)))(k_idx, w)
    return res if also_alone else res[0]


def _in_proj(cfg, xn1, w_own, w_in, k_idx):
    L, D = xn1.shape
    S, _, Ns = w_in.shape
    tm, tn = _tile(L, 1024, 16), _tile(Ns, 1024, 128)
    nb = Ns // tn

    def body(k_ref, a_ref, w_ref, *rest):
        rest[-1][...] = jnp.dot(a_ref[...], w_ref[...], preferred_element_type=f32).astype(bf16)

    a_spec = pl.BlockSpec((tm, D), lambda i, j, kr: (i, 0))
    shape = jax.ShapeDtypeStruct((L, S * Ns), bf16)
    own = pltpu.PrefetchScalarGridSpec(
        num_scalar_prefetch=1, grid=(L // tm, nb),
        in_specs=[a_spec, pl.BlockSpec((D, tn), lambda i, j, kr: (0, j))],
        out_specs=pl.BlockSpec((tm, tn), lambda i, j, kr: (i, kr[0] * nb + j)))
    proj = pl.pallas_call(body, grid_spec=own, out_shape=shape, name="in_proj_own",
                          compiler_params=_cparams(("parallel", "parallel")))(k_idx, xn1, w_own)
    shard = lambda j, kr: (kr[0] + 1 + j // nb) % S
    rest = pltpu.PrefetchScalarGridSpec(
        num_scalar_prefetch=1, grid=(L // tm, (S - 1) * nb),
        in_specs=[a_spec, pl.BlockSpec((None, D, tn), lambda i, j, kr: (shard(j, kr), 0, j % nb)), _ANY],
        out_specs=pl.BlockSpec((tm, tn), lambda i, j, kr: (i, shard(j, kr) * nb + j % nb)))
    return pl.pallas_call(body, grid_spec=rest, out_shape=shape, input_output_aliases={3: 0}, name="in_proj",
                          compiler_params=_cparams(("parallel", "parallel")))(k_idx, xn1, w_in, proj)


def _handshake(peers):
    barrier = pltpu.get_barrier_semaphore()
    for peer in peers:
        pl.semaphore_signal(barrier, inc=1, device_id=peer, device_id_type=_MESH)
    pl.semaphore_wait(barrier, len(peers))


def _allgather_weights(name, bufs, collective_id):
    n = len(bufs)
    refs = [jax.new_ref(b, memory_space=pltpu.MemorySpace.HBM) for b in bufs]

    def copy(ref, sems, idx, to):
        return pltpu.make_async_remote_copy(src_ref=ref, dst_ref=ref, send_sem=sems[0].at[idx], recv_sem=sems[1].at[idx],
                                            device_id=to, device_id_type=_MESH)

    def launch(ssem, rsem, qssem, qrsem, fssem, frsem):
        x, y, c = _coords()
        k = 2 * x + y
        nbrs = [(1 - x, y), (x, 1 - y)]
        across = 2 * (1 - x) + (1 - y)
        sibling = (x, y, 1 - c)
        _handshake([sibling] + [(*chip, c) for chip in nbrs])
        started = []

        def start(cp):
            cp.start()
            started.append(cp)

        for w in range(n):
            rh = refs[w].shape[1] // 2
            for j, chip in enumerate(nbrs):
                start(copy(refs[w].at[k, pl.ds(c * rh, rh)], (ssem, rsem), (w, j), (*chip, c)))
        for w in range(n):
            rh = refs[w].shape[1] // 2
            rq = rh // 2
            for j, (ox, oy) in enumerate(nbrs):
                ko = 2 * ox + oy
                landed = refs[w].at[ko, pl.ds(c * rh, rh)]
                copy(landed, (ssem, rsem), (w, j), (ox, oy, c)).wait_recv()
                start(copy(refs[w].at[ko, pl.ds(c * rh + j * rq, rq)], (qssem, qrsem), (w, j), (*nbrs[1 - j], c)))
                start(copy(landed, (fssem, frsem), (w, j), sibling))
        for w in range(n):
            rh = refs[w].shape[1] // 2
            rq = rh // 2
            for q in range(2):
                quarter = refs[w].at[across, pl.ds(c * rh + q * rq, rq)]
                copy(quarter, (qssem, qrsem), (w, q), (*nbrs[1 - q], c)).wait_recv()
            start(copy(refs[w].at[across, pl.ds(c * rh, rh)], (fssem, frsem), (w, 2), sibling))
        for w in range(n):
            rh = refs[w].shape[1] // 2
            for j, ko in enumerate([2 * nbrs[0][0] + nbrs[0][1], 2 * nbrs[1][0] + nbrs[1][1], across]):
                copy(refs[w].at[ko, pl.ds((1 - c) * rh, rh)], (fssem, frsem), (w, j), sibling).wait_recv()
        for cp in started:
            cp.wait_send()

    _sequencer_kernel(name, collective_id,
                      (pltpu.SemaphoreType.DMA((n, 2)), pltpu.SemaphoreType.DMA((n, 2)), pltpu.SemaphoreType.DMA((n, 2)),
                       pltpu.SemaphoreType.DMA((n, 2)), pltpu.SemaphoreType.DMA((n, 3)), pltpu.SemaphoreType.DMA((n, 3))),
                      launch)
    return [r[...] for r in refs]


def _sequencer_kernel(name, collective_id, sems, body):
    pl.kernel(body, mesh=plsc.ScalarSubcoreMesh(axis_name="seq", num_cores=1), name=name, scratch_types=sems,
              compiler_params=pltpu.CompilerParams(collective_id=collective_id))()


def _hbm_ref(a):
    return jax.new_ref(a, memory_space=pltpu.MemorySpace.HBM)


def _exchange_halves(name, grads, collective_id):
    n = len(grads)
    srcs = [_hbm_ref(g) for g in grads]
    dsts = [jax.empty_ref(jax.ShapeDtypeStruct((g.shape[0], g.shape[1] // 2, g.shape[2]), g.dtype),
                          memory_space=pltpu.MemorySpace.HBM) for g in grads]

    def body(ssem, rsem):
        x, y, c = _coords()
        _handshake([(x, y, 1 - c)])
        cps = []
        for w in range(n):
            rh = srcs[w].shape[1] // 2
            cp = pltpu.make_async_remote_copy(
                src_ref=srcs[w].at[:, pl.ds((1 - c) * rh, rh)], dst_ref=dsts[w], send_sem=ssem.at[w], recv_sem=rsem.at[w],
                device_id=(x, y, 1 - c), device_id_type=_MESH)
            cp.start()
            cps.append(cp)
        for cp in cps:
            cp.wait()

    _sequencer_kernel(name, collective_id, (pltpu.SemaphoreType.DMA((n,)), pltpu.SemaphoreType.DMA((n,))), body)
    return [d[...] for d in dsts]


def _scatter_shards(name, parts, collective_id):
    n = len(parts)
    srcs = [_hbm_ref(p) for p in parts]
    dsts = [jax.empty_ref(jax.ShapeDtypeStruct((3,) + p.shape[1:], p.dtype), memory_space=pltpu.MemorySpace.HBM)
            for p in parts]

    def body(ssem, rsem):
        x, y, c = _coords()
        k = 2 * x + y
        others = _other_chips(x, y)
        _handshake([(*chip, c) for chip in others])
        cps = []
        for w in range(n):
            for j, (ox, oy) in enumerate(others):
                cp = pltpu.make_async_remote_copy(
                    src_ref=srcs[w].at[(2 * ox + oy - k + 3) % 4], dst_ref=dsts[w].at[j], send_sem=ssem.at[w, j],
                    recv_sem=rsem.at[w, j],
                    device_id=(ox, oy, c), device_id_type=_MESH)
                cp.start()
                cps.append(cp)
        for cp in cps:
            cp.wait()

    _sequencer_kernel(name, collective_id, (pltpu.SemaphoreType.DMA((n, 3)), pltpu.SemaphoreType.DMA((n, 3))), body)
    return [d[...] for d in dsts]


def _join_halves(name, bufs, collective_id):
    n = len(bufs)
    refs = [_hbm_ref(b) for b in bufs]

    def body(ssem, rsem):
        x, y, c = _coords()
        _handshake([(x, y, 1 - c)])
        cps = []
        for w in range(n):
            rh = refs[w].shape[0] // 2
            mine = refs[w].at[pl.ds(c * rh, rh)]
            cp = pltpu.make_async_remote_copy(src_ref=mine, dst_ref=mine, send_sem=ssem.at[w], recv_sem=rsem.at[w],
                                              device_id=(x, y, 1 - c), device_id_type=_MESH)
            cp.start()
            cps.append(cp)
        for w in range(n):
            rh = refs[w].shape[0] // 2
            theirs = refs[w].at[pl.ds((1 - c) * rh, rh)]
            pltpu.make_async_remote_copy(src_ref=theirs, dst_ref=theirs, send_sem=ssem.at[w], recv_sem=rsem.at[w],
                                         device_id=(x, y, 1 - c), device_id_type=_MESH).wait_recv()
        for cp in cps:
            cp.wait_send()

    _sequencer_kernel(name, collective_id, (pltpu.SemaphoreType.DMA((n,)), pltpu.SemaphoreType.DMA((n,))), body)
    return [r[...] for r in refs]


def _allgather_small(name, v, collective_id):
    R = v.shape[0]
    src = _hbm_ref(v)
    dst = jax.empty_ref(jax.ShapeDtypeStruct((8, R, 128), v.dtype), memory_space=pltpu.MemorySpace.HBM)
    flips = [(dx, dy, dc) for dx in (0, 1) for dy in (0, 1) for dc in (0, 1)][1:]

    def body(lsem, ssem, rsem):
        x, y, c = _coords()
        me = 4 * x + 2 * y + c
        flip = lambda v, d: 1 - v if d else v
        peers = [(flip(x, dx), flip(y, dy), flip(c, dc)) for dx, dy, dc in flips]
        _handshake(peers)
        own = pltpu.make_async_copy(src, dst.at[me], lsem)
        own.start()
        cps = []
        for r, peer in enumerate(peers):
            cp = pltpu.make_async_remote_copy(src_ref=src, dst_ref=dst.at[me], send_sem=ssem.at[r], recv_sem=rsem.at[r],
                                              device_id=peer, device_id_type=_MESH)
            cp.start()
            cps.append(cp)
        for r, (px, py, pc) in enumerate(peers):
            theirs = dst.at[4 * px + 2 * py + pc]
            pltpu.make_async_remote_copy(src_ref=theirs, dst_ref=theirs, send_sem=ssem.at[r], recv_sem=rsem.at[r],
                                         device_id=(px, py, pc), device_id_type=_MESH).wait_recv()
        for cp in cps:
            cp.wait_send()
        own.wait()

    _sequencer_kernel(name, collective_id, (pltpu.SemaphoreType.DMA, pltpu.SemaphoreType.DMA((7,)),
                                            pltpu.SemaphoreType.DMA((7,))), body)
    return dst[...]


def _sum8(name, g8, after):
    R = g8.shape[1]
    tr = _tile(R, 512, 8)

    def body(g_ref, *rest):
        rest[-1][...] = (((g_ref[0] + g_ref[1]) + (g_ref[2] + g_ref[3]))
                         + ((g_ref[4] + g_ref[5]) + (g_ref[6] + g_ref[7])))

    return pl.pallas_call(body, grid=(R // tr,),
                          in_specs=[pl.BlockSpec((8, tr, 128), lambda i: (0, i, 0))] + [_ANY] * len(after),
                          out_specs=pl.BlockSpec((tr, 128), lambda i: (i, 0)), out_shape=jax.ShapeDtypeStruct((R, 128), f32),
                          name=name, compiler_params=_cparams(("parallel",)))(g8, *after)


def _add_own_half(name, g, t, kc_idx, after):
    S, R, C = g.shape
    rh = R // 2
    tr = _tile(rh, 512, 16)
    nrb = rh // tr
    shard = lambda s, kc: (kc[0] + 1 + s) % S

    def body(kc_ref, g_ref, t_ref, *rest):
        rest[-1][...] = (g_ref[...] + t_ref[...]).astype(bf16)

    gs = pltpu.PrefetchScalarGridSpec(
        num_scalar_prefetch=1, grid=(S - 1, nrb),
        in_specs=[pl.BlockSpec((None, tr, C), lambda s, r, kc: (shard(s, kc), kc[1] * nrb + r, 0)),
                  pl.BlockSpec((None, tr, C), lambda s, r, kc: (shard(s, kc), r, 0))] + [_ANY] * len(after),
        out_specs=pl.BlockSpec((None, tr, C), lambda s, r, kc: (s, r, 0)))
    return pl.pallas_call(body, grid_spec=gs, out_shape=jax.ShapeDtypeStruct((S - 1, rh, C), bf16), name=name,
                          compiler_params=_cparams(("parallel", "parallel")))(kc_idx, g, t, *after)


def _add_shard_parts(name, g, t, r, kc_idx, after):
    S, R, C = g.shape
    rh = R // 2
    tr = _tile(rh, 256, 16)
    nrb = rh // tr

    def body(kc_ref, g_ref, t_ref, r_ref, *rest):
        own = g_ref[...] + t_ref[...]
        rest[-1][...] = (own + r_ref[1].astype(f32)) + (r_ref[0].astype(f32) + r_ref[2].astype(f32))

    gs = pltpu.PrefetchScalarGridSpec(
        num_scalar_prefetch=1, grid=(nrb,),
        in_specs=[pl.BlockSpec((None, tr, C), lambda i, kc: (kc[0], kc[1] * nrb + i, 0)),
                  pl.BlockSpec((None, tr, C), lambda i, kc: (kc[0], i, 0)),
                  pl.BlockSpec((3, tr, C), lambda i, kc: (0, i, 0))] + [_ANY] * len(after),
        out_specs=pl.BlockSpec((tr, C), lambda i, kc: (kc[1] * nrb + i, 0)))
    return pl.pallas_call(body, grid_spec=gs, out_shape=jax.ShapeDtypeStruct((R, C), f32), name=name,
                          compiler_params=_cparams(("parallel",)))(kc_idx, g, t, r, *after)


def _adamw_update(wv, gv, mv, vv):
    nm = _ADAM_B1 * mv + (1.0 - _ADAM_B1) * gv
    nv = _ADAM_B2 * vv + (1.0 - _ADAM_B2) * (gv * gv)
    m_hat = nm / (1.0 - _ADAM_B1 ** _ADAM_STEP)
    v_hat = nv / (1.0 - _ADAM_B2 ** _ADAM_STEP)
    return -_ADAM_LR * (m_hat / (jnp.sqrt(v_hat) + _ADAM_EPS) + _ADAM_WD * wv), nm, nv


def _adamw(name, w, g, m, v, after=()):
    R, C = w.shape
    tr = _tile(R, 256, 8)

    def body(w_ref, g_ref, m_ref, v_ref, *rest):
        go_ref, d_ref, nm_ref, nv_ref = rest[len(after):]
        gv = g_ref[...]
        go_ref[...] = gv
        d_ref[...], nm_ref[...], nv_ref[...] = _adamw_update(w_ref[...], gv, m_ref[...], v_ref[...])

    spec = pl.BlockSpec((tr, C), lambda i: (i, 0))
    shape = jax.ShapeDtypeStruct((R, C), f32)
    return pl.pallas_call(body, grid=(R // tr,), in_specs=[spec] * 4 + [_ANY] * len(after), out_specs=[spec] * 4,
                          out_shape=[shape] * 4, name=name, compiler_params=_cparams(("parallel",)))(w, g, m, v, *after)


def _adamw_whole(name, w, g, m, v):
    def body(w_ref, g_ref, m_ref, v_ref, d_ref, nm_ref, nv_ref):
        d_ref[...], nm_ref[...], nv_ref[...] = _adamw_update(w_ref[...], g_ref[...], m_ref[...], v_ref[...])

    vm = pl.BlockSpec(memory_space=pltpu.VMEM)
    return pl.pallas_call(body, in_specs=[vm] * 4, out_specs=[vm] * 3, out_shape=[jax.ShapeDtypeStruct(w.shape, f32)] * 3,
                          name=name, compiler_params=pltpu.CompilerParams(vmem_limit_bytes=_VMEM_LIMIT))(w, g, m, v)


def _pack(arrs):
    flat = jnp.concatenate([a.reshape(-1).astype(f32) for a in arrs])
    n = flat.shape[0]
    pad = (-n) % (128 * 128)
    return jnp.pad(flat, (0, pad)).reshape(-1, 128)


def _unpack(packed, shapes):
    flat = packed.reshape(-1)
    out, off = [], 0
    for s in shapes:
        n = math.prod(s)
        out.append(flat[off:off + n].reshape(s))
        off += n
    return out


_BIG = ("w_in", "w_glu", "w_ssm_out", "w_conv_out", "w_o", "w_up", "w_down")
_SMALL = ("norm_tok", "a_re", "a_im", "log_dt", "b_re", "b_im", "c_re", "c_im", "d_skip", "conv_w", "conv_b",
          "norm_ffn", "ffn_conv_w", "ffn_conv_b", "norm_final")
_WEIGHTS = ("norm_tok", "w_in", "a_re", "a_im", "log_dt", "b_re", "b_im", "c_re", "c_im", "d_skip", "w_glu",
            "w_ssm_out", "conv_w", "conv_b", "w_conv_out", "w_o", "norm_ffn", "w_up", "ffn_conv_w", "ffn_conv_b",
            "w_down", "norm_final")


def _step(cfg, x, tgt, p, m, v):
    L, D, SW, CW, F = cfg.L, cfg.D, cfg.SW, cfg.CW, cfg.F
    xi, yi, ci = _coords()
    k_idx = (2 * xi + yi).astype(jnp.int32).reshape(1)
    c_idx = ci.astype(jnp.int32).reshape(1)
    x = x.reshape(L, D)
    tgt = tgt.reshape(L, D)

    big2d = {n: p[n].reshape(p[n].shape[-2], p[n].shape[-1]) for n in _BIG}
    slots = {n: _cast_into_slot("cast_" + n, big2d[n], k_idx) for n in _BIG if n != "w_in"}
    slots["w_in"], w_in_own = _cast_into_slot("cast_w_in", big2d["w_in"], k_idx, also_alone=True)
    wg = {}
    for cid, (gname, group) in enumerate((("allgather_w_in", ("w_in",)),
                                          ("allgather_w_mixer", ("w_glu", "w_ssm_out", "w_conv_out", "w_o")),
                                          ("allgather_w_up", ("w_up",)), ("allgather_w_down", ("w_down",)))):
        wg.update(zip(group, _allgather_weights(gname, [slots[n] for n in group], cid)))
    w_in, w_so, w_co, w_up = wg["w_in"], wg["w_ssm_out"], wg["w_conv_out"], wg["w_up"]
    w_glu, w_o, w_down = wg["w_glu"], wg["w_o"], wg["w_down"]
    kk = k_idx[0]
    cw_full = lax.dynamic_update_slice(jnp.zeros((3, CW), f32), p["conv_w"].reshape(3, CW // _NCHIP), (0, kk * (CW // _NCHIP)))
    fw_full = lax.dynamic_update_slice(jnp.zeros((3, F), f32), p["ffn_conv_w"].reshape(3, F // _NCHIP), (0, kk * (F // _NCHIP)))
    south = (ci == 0).astype(f32)
    filters8 = _allgather_small("allgather_conv_filters", _pack([cw_full * south, fw_full * south]), 17)
    conv_b = p["conv_b"].reshape(1, CW)
    ffn_conv_b = p["ffn_conv_b"].reshape(1, F)
    norm_tok = p["norm_tok"].reshape(1, D)
    norm_ffn = p["norm_ffn"].reshape(1, D)
    norm_final = p["norm_final"].reshape(1, D)
    dskip = p["d_skip"].reshape(1, SW)

    s5_in = (p["a_re"][0], p["a_im"][0], p["log_dt"][0], p["b_re"][0], p["b_im"][0])
    (abr, abi, bbr, bbi), disc_vjp = jax.vjp(_discretize, *s5_in)
    tabs = _s5_tables(cfg, abr, abi, bbr, bbi, p["c_re"][0], p["c_im"][0])

    xn1, r1 = _rms_fwd("rms_tok", x, norm_tok)
    proj = _in_proj(cfg, xn1, w_in_own, w_in, k_idx)
    conv_w, ffn_conv_w = _unpack(_sum8("sum_conv_filters", filters8, [proj]), [(3, CW), (3, F)])
    y_s, ya0, cin, xs = _s5_fwd(cfg, proj, tabs, dskip)

    def tiles(*arrs):
        return lambda tm, tn: [(a, pl.BlockSpec((tm, tn), lambda i, j: (i, j))) for a in arrs]

    def glu_epi(acc, e, o):
        o[0][...] = (e[0][...].astype(f32) * jax.nn.sigmoid(acc)).astype(bf16)
        o[1][...] = acc.astype(bf16)

    ya1, z = _mm_nn("glu", ya0, w_glu, [bf16, bf16], rows=True, extras_fn=tiles(ya0), epilogue=glu_epi)
    yb0 = _convb_fwd(cfg, proj, conv_w, conv_b)
    merged, ya, yb = _merge_fwd(cfg, ya1, yb0, proj, w_so, w_co)

    def res_epi(acc, e, o):
        o[0][...] = e[0][...] + acc

    h1 = _mm_nn("out_proj", merged, w_o, [f32], rows=True, extras_fn=tiles(x), epilogue=res_epi)[0]
    xn2, r2 = _rms_fwd("rms_ffn", h1, norm_ffn)
    hh = _mm_nn("ffn_up", xn2, w_up, [bf16], tn=2816)[0]
    fact, ffn_pre = _ffn_act(cfg, hh, ffn_conv_w, ffn_conv_b)
    h2 = _mm_nn("ffn_down", fact, w_down, [f32], tm=512, rows=True, extras_fn=tiles(h1), epilogue=res_epi)[0]
    dh2, dh2b, g_norm_final, loss_tile = _loss_head("loss_head", h2, tgt, norm_final)

    kc_idx = jnp.concatenate([k_idx, c_idx])
    reduced, chains = {}, {}

    def rs_halves(tag, collective_id, names, gs):
        chains[tag] = dict(cid=collective_id, names=names, gs=gs,
                           sib=_exchange_halves("grad_halves_" + tag, gs, collective_id))

    def rs_shards(tag, after):
        ch = chains[tag]
        ch["parts"] = [_add_own_half("grad_add_halves_" + n, g, t, kc_idx, after)
                       for n, g, t in zip(ch["names"], ch["gs"], ch["sib"])]
        ch["chips"] = _scatter_shards("grad_shards_" + tag, ch["parts"], ch["cid"] + 1)
        return ch["parts"]

    def rs_join(tag, after):
        ch = chains[tag]
        ch["halves"] = [_add_shard_parts("grad_add_chips_" + n, g, t, r, kc_idx, after)
                        for n, g, t, r in zip(ch["names"], ch["gs"], ch["sib"], ch["chips"])]
        reduced.update(zip(ch["names"], _join_halves("grad_join_" + tag, ch["halves"], ch["cid"] + 2)))
        return ch["halves"]

    df = _mm_nt("ffn_down_dx", dh2b, w_down, [bf16], tn=2816, rows=True)[0]
    g_w_down = _mm_tn("ffn_down_dw", fact, dh2b, tm=1408, tn=512)
    rs_halves("ffn_down", 4, ["w_down"], [g_w_down.reshape(_NCHIP, F // _NCHIP, D)])
    dhh, g_ffn_conv_w, g_ffn_conv_b = _ffn_act_bwd(cfg, hh, ffn_pre, df, ffn_conv_w)
    sent = rs_shards("ffn_down", [dhh])
    g_w_up = _mm_tn("ffn_up_dw", xn2, dhh, shards=_NCHIP, tm=512, tn=1408, b_resident=True, after=sent)
    rs_halves("ffn_up", 7, ["w_up"], [g_w_up])
    dxn2 = _ffn_up_dx(cfg, dhh, w_up)
    sent = rs_shards("ffn_up", [dxn2]) + rs_join("ffn_down", [dxn2])
    dh1, dh1b, g_norm_ffn = _rms_bwd("rms_ffn_bwd", dxn2, h1, r2, norm_ffn, dh2, after=sent)

    dya, dyb, dproj = _out_proj_dx(cfg, dh1b, w_o, ya, yb, proj)
    g_w_o = _mm_tn("out_proj_dw", merged, dh1b)

    def glu_bwd_epi(acc, e, o):
        a0 = e[0][...].astype(f32)
        s = jax.nn.sigmoid(e[1][...].astype(f32))
        o[0][...] = (acc * a0 * s * (1.0 - s)).astype(bf16)
        o[1][...] = (acc * s).astype(bf16)

    dz, t1 = _mm_nt("ssm_out_dx", dya, w_so, [bf16, bf16], extras_fn=tiles(ya0, z), epilogue=glu_bwd_epi)
    g_w_so = _mm_tn("ssm_out_dw", ya1, dya, shards=_NCHIP, tn=512)
    dyb0 = _mm_nt("conv_out_dx", dyb, w_co, [bf16])[0]
    g_w_co = _mm_tn("conv_out_dw", yb0, dyb, shards=_NCHIP, tn=512)
    dproj, g_conv_w, g_conv_b = _convb_bwd(cfg, proj, dyb0, conv_w, conv_b, dproj)

    def gelu_bwd_epi(acc, e, o):
        o[0][...] = ((e[0][...].astype(f32) + acc) * _gelu_grad(e[1][...].astype(f32))).astype(bf16)

    dy_s = _mm_nt("glu_dx", dz, w_glu, [bf16], rows=True, extras_fn=tiles(t1, y_s), epilogue=gelu_bwd_epi)[0]
    g_w_glu = _mm_tn("glu_dw", ya0, dz)
    rs_halves("mixer", 10, ["w_o", "w_ssm_out", "w_conv_out", "w_glu"],
              [g_w_o.reshape(_NCHIP, D // _NCHIP, D), g_w_so, g_w_co, g_w_glu.reshape(_NCHIP, SW // _NCHIP, SW)])
    sent = rs_join("ffn_up", [g_w_glu])
    dproj, da_acc, db_full, dc_full, g_dskip = _s5_bwd(cfg, proj, dy_s, cin, xs, tabs, dskip, dproj, after=sent)
    sent = rs_shards("mixer", [dproj])

    dabr, dabi, dbbr, dbbi, g_c_re, g_c_im = _s5_param_grads(cfg, da_acc, db_full, dc_full)
    g_a_re, g_a_im, g_log_dt, g_b_re, g_b_im = disc_vjp((dabr, dabi, dbbr, dbbi))
    small_g = {"a_re": g_a_re, "a_im": g_a_im, "log_dt": g_log_dt, "b_re": g_b_re, "b_im": g_b_im,
               "c_re": g_c_re, "c_im": g_c_im, "d_skip": g_dskip, "conv_w": g_conv_w, "conv_b": g_conv_b,
               "norm_ffn": g_norm_ffn, "ffn_conv_w": g_ffn_conv_w, "ffn_conv_b": g_ffn_conv_b, "norm_final": g_norm_final}
    early = [n for n in _SMALL if n != "norm_tok"]
    small8 = _allgather_small("allgather_small_grads", _pack([small_g[n] for n in early]), 16)

    g_w_in = _mm_tn("in_proj_dw", xn1, dproj, shards=_NCHIP, tn=CW, after=sent,
                    b_block=lambda j: jnp.where(j == 0, 3 * CW // SW, jnp.where(j < 4, j - 1, j)))
    rs_halves("in_proj", 13, ["w_in"], [g_w_in])
    dxn1 = _in_proj_dx(cfg, dproj, w_in)
    sent = rs_shards("in_proj", [dxn1]) + rs_join("mixer", [dxn1])
    dx, _, g_norm_tok = _rms_bwd("rms_tok_bwd", dxn1, x, r1, norm_tok, dh1, after=sent)

    summed = dict(zip(early, _unpack(_sum8("sum_small_grads", small8, [dx]), [small_g[n].shape for n in early])))
    summed["norm_tok"] = _unpack(_allreduce8("allreduce_norm_tok", _pack([g_norm_tok])), [g_norm_tok.shape])[0]
    summed["conv_w"] = lax.dynamic_slice(summed["conv_w"], (0, kk * (CW // _NCHIP)), (3, CW // _NCHIP))
    summed["ffn_conv_w"] = lax.dynamic_slice(summed["ffn_conv_w"], (0, kk * (F // _NCHIP)), (3, F // _NCHIP))

    grads, deltas, new_m, new_v = {}, {}, {}, {}

    def adamw_big(names, after):
        for n in names:
            g_, d_, m_, v_ = _adamw("adamw_" + n, big2d[n], reduced[n], m[n].reshape(big2d[n].shape),
                                    v[n].reshape(big2d[n].shape), after=after)
            grads[n], deltas[n], new_m[n], new_v[n] = (a.reshape(p[n].shape) for a in (g_, d_, m_, v_))
            after = [d_]
        return after

    for n in _SMALL:
        grads[n] = summed[n].reshape(p[n].shape)
        deltas[n], new_m[n], new_v[n] = _adamw_whole("adamw_" + n, p[n], grads[n], m[n], v[n])
    done = adamw_big(["w_down", "w_up", "w_o", "w_ssm_out", "w_conv_out", "w_glu"], [deltas["norm_final"]])
    rs_join("in_proj", done + [deltas[n] for n in _SMALL])
    adamw_big(["w_in"], ())

    loss = lax.psum(loss_tile[0, 0], ("x", "y", "c"))
    return (loss, dx.reshape(1, L, D), *[grads[n] for n in _WEIGHTS], *[deltas[n] for n in _WEIGHTS],
            *[new_m[n] for n in _WEIGHTS], *[new_v[n] for n in _WEIGHTS])


def kernel(x, norm_tok, w_in, a_re, a_im, log_dt, b_re, b_im, c_re, c_im, d_skip, w_glu, w_ssm_out, conv_w, conv_b, w_conv_out, w_o, norm_ffn, w_up, ffn_conv_w, ffn_conv_b, w_down, norm_final, loss_target, m_norm_tok, m_w_in, m_a_re, m_a_im, m_log_dt, m_b_re, m_b_im, m_c_re, m_c_im, m_d_skip, m_w_glu, m_w_ssm_out, m_conv_w, m_conv_b, m_w_conv_out, m_w_o, m_norm_ffn, m_w_up, m_ffn_conv_w, m_ffn_conv_b, m_w_down, m_norm_final, v_norm_tok, v_w_in, v_a_re, v_a_im, v_log_dt, v_b_re, v_b_im, v_c_re, v_c_im, v_d_skip, v_w_glu, v_w_ssm_out, v_conv_w, v_conv_b, v_w_conv_out, v_w_o, v_norm_ffn, v_w_up, v_ffn_conv_w, v_ffn_conv_b, v_w_down, v_norm_final):
    p = dict(norm_tok=norm_tok, w_in=w_in, a_re=a_re, a_im=a_im, log_dt=log_dt, b_re=b_re, b_im=b_im, c_re=c_re,
             c_im=c_im, d_skip=d_skip, w_glu=w_glu, w_ssm_out=w_ssm_out, conv_w=conv_w, conv_b=conv_b,
             w_conv_out=w_conv_out, w_o=w_o, norm_ffn=norm_ffn, w_up=w_up, ffn_conv_w=ffn_conv_w,
             ffn_conv_b=ffn_conv_b, w_down=w_down, norm_final=norm_final)
    m = dict(norm_tok=m_norm_tok, w_in=m_w_in, a_re=m_a_re, a_im=m_a_im, log_dt=m_log_dt, b_re=m_b_re, b_im=m_b_im,
             c_re=m_c_re, c_im=m_c_im, d_skip=m_d_skip, w_glu=m_w_glu, w_ssm_out=m_w_ssm_out, conv_w=m_conv_w,
             conv_b=m_conv_b, w_conv_out=m_w_conv_out, w_o=m_w_o, norm_ffn=m_norm_ffn, w_up=m_w_up,
             ffn_conv_w=m_ffn_conv_w, ffn_conv_b=m_ffn_conv_b, w_down=m_w_down, norm_final=m_norm_final)
    v = dict(norm_tok=v_norm_tok, w_in=v_w_in, a_re=v_a_re, a_im=v_a_im, log_dt=v_log_dt, b_re=v_b_re, b_im=v_b_im,
             c_re=v_c_re, c_im=v_c_im, d_skip=v_d_skip, w_glu=v_w_glu, w_ssm_out=v_w_ssm_out, conv_w=v_conv_w,
             conv_b=v_conv_b, w_conv_out=v_w_conv_out, w_o=v_w_o, norm_ffn=v_norm_ffn, w_up=v_w_up,
             ffn_conv_w=v_ffn_conv_w, ffn_conv_b=v_ffn_conv_b, w_down=v_w_down, norm_final=v_norm_final)
    return _step(_Cfg(), x, loss_target, p, m, v)
```

```python
import functools
import math
from typing import NamedTuple

import jax
import jax.numpy as jnp
from jax import lax
from jax.experimental import pallas as pl
from jax.experimental.pallas import tpu as pltpu
from jax.experimental.pallas import tpu_sc as plsc

f32 = jnp.float32
bf16 = jnp.bfloat16
_MESH = pl.DeviceIdType.MESH

_EPS = 1e-6
_ADAM_LR = 0.001
_ADAM_B1 = 0.9
_ADAM_B2 = 0.999
_ADAM_EPS = 1e-08
_ADAM_WD = 0.01
_ADAM_STEP = 10
_SSM_GROUP = 16
_SSM_STATE = 64
_SLAB_GROUPS = 16
_NCHIP = 4
_VMEM_LIMIT = 56 * 2**20
_VMEM_S5_BWD = 62 * 2**20
_GELU_C = math.sqrt(2.0 / math.pi)
_GELU_A = 0.044715


class _Cfg(NamedTuple):
    L: int = 4096
    D: int = 2048
    SW: int = 1024
    CW: int = 1024
    F: int = 5632
    T: int = 256


def _tile(n, pref, align):
    t = min(n, pref)
    t -= t % align
    while t > align and n % t:
        t -= align
    assert t > 0 and n % t == 0, (n, pref, align)
    return t


def _cparams(sem, vmem_limit=_VMEM_LIMIT):
    return pltpu.CompilerParams(dimension_semantics=sem, vmem_limit_bytes=vmem_limit)


def _gelu(x):
    return _gelu_and_grad(x)[0]


def _gelu_grad(x):
    return _gelu_and_grad(x)[1]


def _gelu_and_grad(x):
    x2 = x * x
    th = jnp.tanh(x * (_GELU_C + (_GELU_C * _GELU_A) * x2))
    half = 0.5 + 0.5 * th
    return x * half, half + (0.5 * x) * (1.0 - th * th) * (_GELU_C + (3.0 * _GELU_C * _GELU_A) * x2)


_NN = (((1,), (0,)), ((), ()))
_NT = (((1,), (1,)), ((), ()))
_TN = (((0,), (0,)), ((), ()))


def _whole(ref):
    return ref[...]


_ANY = pl.BlockSpec(memory_space=pl.ANY)


def _mm(name, operands, steps, *, grid, contract, outs, extras=(), epilogue=None, acc_shape=None, after=()):
    nop, ne, na = len(operands), len(extras), len(after)
    nk = len(steps)

    def body(*refs):
        op_refs = refs[:nop]
        e_refs = refs[nop:nop + ne]
        o_refs = refs[nop + ne + na:nop + ne + na + len(outs)]

        def partial(terms):
            tot = None
            for ai, av, bi, bv in terms:
                d = lax.dot_general(av(op_refs[ai]), bv(op_refs[bi]), contract, preferred_element_type=f32)
                tot = d if tot is None else tot + d
            return tot

        def finish(res):
            if epilogue is None:
                for o in o_refs:
                    o[...] = res.astype(o.dtype)
            else:
                epilogue(res, e_refs, o_refs)

        if nk == 1:
            finish(partial(steps[0][1]))
            return
        acc = refs[-1]
        kid = pl.program_id(len(grid) - 1)
        for k, terms in steps:
            def run(k=k, terms=terms):
                d = partial(terms)
                if k == 0:
                    acc[...] = d
                elif k < nk - 1:
                    acc[...] += d
                else:
                    finish(acc[...] + d)

            pl.when(kid == k)(run)

    sem = ("parallel",) * (len(grid) - (nk > 1)) + (("arbitrary",) if nk > 1 else ())
    return pl.pallas_call(
        body, grid=grid, in_specs=[o[1] for o in operands] + [e[1] for e in extras] + [_ANY] * na,
        out_specs=[o[1] for o in outs], out_shape=[o[0] for o in outs],
        scratch_shapes=[pltpu.VMEM(acc_shape, f32)] if nk > 1 else [], name=name,
        compiler_params=_cparams(sem))(*[o[0] for o in operands], *[e[0] for e in extras], *after)


def _mm_nn(name, a, w, out_dtypes, *, tm=1024, tn=1024, rows=False, extras_fn=None, epilogue=None):
    M, K = a.shape
    S, Ns = w.shape[0], w.shape[-1]
    N = Ns if rows else Ns * S
    tm, tn = _tile(M, tm, 16), _tile(Ns, tn, 128)
    nb = Ns // tn
    a_spec = pl.BlockSpec((tm, K), lambda i, j: (i, 0))
    if rows:
        b_spec = pl.BlockSpec((S, K // S, tn), lambda i, j: (0, 0, j))
        b_view = lambda r: r[...].reshape(K, tn)
    else:
        b_spec = pl.BlockSpec((None, K, tn), lambda i, j: (j // nb, 0, j % nb))
        b_view = _whole
    o_spec = pl.BlockSpec((tm, tn), lambda i, j: (i, j))
    outs = [(jax.ShapeDtypeStruct((M, N), dt), o_spec) for dt in out_dtypes]
    extras = extras_fn(tm, tn) if extras_fn is not None else ()
    return _mm(name, [(a, a_spec), (w, b_spec)], [(None, [(0, _whole, 1, b_view)])], grid=(M // tm, N // tn),
               contract=_NN, outs=outs, extras=extras, epilogue=epilogue)


def _mm_nt(name, a, w, out_dtypes, *, tm=1024, tn=1024, rows=False, extras_fn=None, epilogue=None):
    M, N = a.shape
    S, Ks, Ns = w.shape
    K = Ks * S if rows else Ks
    tm = _tile(M, tm, 16)
    a_spec = pl.BlockSpec((tm, N), lambda i, j: (i, 0))
    if rows:
        whole_shards = tn > Ks and tn % Ks == 0 and K % tn == 0
        tn = K if tn >= K else (tn if whole_shards else _tile(Ks, tn, 128))
        if tn == K or whole_shards:
            b_spec = pl.BlockSpec((tn // Ks, Ks, N), lambda i, j: (j, 0, 0))
            terms = [(0, _whole, 1, lambda r: r[...].reshape(tn, N))]
        else:
            nbs = Ks // tn
            b_spec = pl.BlockSpec((None, tn, N), lambda i, j: (j // nbs, j % nbs, 0))
            terms = [(0, _whole, 1, _whole)]
    else:
        tn = _tile(K, tn, 128)
        assert S * Ns == N
        b_spec = pl.BlockSpec((S, tn, Ns), lambda i, j: (0, j, 0))
        terms = [(0, lambda r, s=s: r[:, s * Ns:(s + 1) * Ns], 1, lambda r, s=s: r[s]) for s in range(S)]
    o_spec = pl.BlockSpec((tm, tn), lambda i, j: (i, j))
    outs = [(jax.ShapeDtypeStruct((M, K), dt), o_spec) for dt in out_dtypes]
    extras = extras_fn(tm, tn) if extras_fn is not None else ()
    return _mm(name, [(a, a_spec), (w, b_spec)], [(None, terms)], grid=(M // tm, K // tn), contract=_NT,
               outs=outs, extras=extras, epilogue=epilogue)


def _mm_tn(name, a, b, *, shards=None, tm=1024, tn=1024, b_block=None, b_resident=False, after=()):
    M, K = a.shape
    halves = b.shape[0] if b.ndim == 3 else 1
    Nh = b.shape[-1]
    N = Nh * halves
    Ns = N // shards if shards else N
    tm, tn = _tile(K, tm, 128), _tile(math.gcd(Ns, Nh), tn, 128)
    nb, nbh = Ns // tn, Nh // tn
    ij = (lambda g0, g1: (g1, g0)) if b_resident else (lambda g0, g1: (g0, g1))
    bmap = b_block if b_block is not None else (lambda j: j)
    a_spec = pl.BlockSpec((M, tm), lambda g0, g1: (0, ij(g0, g1)[0]))
    if halves > 1:
        b_spec = pl.BlockSpec((None, M, tn), lambda g0, g1: (bmap(ij(g0, g1)[1]) // nbh, 0, bmap(ij(g0, g1)[1]) % nbh))
    else:
        b_spec = pl.BlockSpec((M, tn), lambda g0, g1: (0, bmap(ij(g0, g1)[1])))
    if shards:
        out = (jax.ShapeDtypeStruct((shards, K, Ns), f32),
               pl.BlockSpec((None, tm, tn), lambda g0, g1: (ij(g0, g1)[1] // nb, ij(g0, g1)[0], ij(g0, g1)[1] % nb)))
    else:
        out = (jax.ShapeDtypeStruct((K, N), f32), pl.BlockSpec((tm, tn), lambda g0, g1: ij(g0, g1)))
    grid = (N // tn, K // tm) if b_resident else (K // tm, N // tn)
    return _mm(name, [(a, a_spec), (b, b_spec)], [(None, [(0, _whole, 1, _whole)])], grid=grid, contract=_TN,
               outs=[out], after=after)[0]


def _in_proj_dx(cfg, dproj, w_in):
    L, D, SW, CW = cfg.L, cfg.D, cfg.SW, cfg.CW
    NP = SW + 3 * CW + 2 * D
    Ns = NP // _NCHIP
    assert SW + CW == Ns and 2 * CW == Ns and D == Ns
    tm, tn = _tile(L, 1024, 16), _tile(D, 1024, 128)
    a_spec = pl.BlockSpec((tm, NP // 2), lambda i, j, k: (i, k))
    b_spec = pl.BlockSpec((2, tn, Ns), lambda i, j, k: (k, j, 0))
    first = [(0, lambda r: r[:, 0:CW], 1, lambda r: r[0, :, SW:SW + CW]),
             (0, lambda r: r[:, CW:3 * CW], 1, lambda r: r[1]),
             (0, lambda r: r[:, 3 * CW:3 * CW + SW], 1, lambda r: r[0, :, 0:SW])]
    second = [(0, lambda r: r[:, 0:D], 1, lambda r: r[0]), (0, lambda r: r[:, D:2 * D], 1, lambda r: r[1])]
    out = (jax.ShapeDtypeStruct((L, D), bf16), pl.BlockSpec((tm, tn), lambda i, j, k: (i, j)))
    return _mm("in_proj_dx", [(dproj, a_spec), (w_in, b_spec)], [(0, first), (1, second)], grid=(L // tm, D // tn, 2),
               contract=_NT, outs=[out], acc_shape=(tm, tn))[0]


def _out_proj_dx(cfg, dh1b, w_o, ya, yb, proj):
    L, D = cfg.L, cfg.D
    NP = cfg.SW + 3 * cfg.CW + 2 * D
    assert NP == 4 * D
    tm = _tile(L, 512, 16)

    def epilogue(acc, e, o):
        sa = jax.nn.sigmoid(e[2][:, 0:D].astype(f32))
        sb = jax.nn.sigmoid(e[2][:, D:2 * D].astype(f32))
        o[0][...] = (acc * sa).astype(bf16)
        o[1][...] = (acc * sb).astype(bf16)
        o[2][:, 0:D] = (acc * e[0][...].astype(f32) * sa * (1.0 - sa)).astype(bf16)
        o[2][:, D:2 * D] = (acc * e[1][...].astype(f32) * sb * (1.0 - sb)).astype(bf16)

    row = pl.BlockSpec((tm, D), lambda i, j: (i, 0))
    half = pl.BlockSpec((tm, 2 * D), lambda i, j: (i, 1))
    return _mm("out_proj_dx", [(dh1b, row), (w_o, pl.BlockSpec(w_o.shape, lambda i, j: (0, 0, 0),
                                                                pipeline_mode=pl.Buffered(1)))],
               [(None, [(0, _whole, 1, lambda r: r[...].reshape(D, D))])], grid=(L // tm, 1), contract=_NT,
               outs=[(jax.ShapeDtypeStruct((L, D), bf16), row), (jax.ShapeDtypeStruct((L, D), bf16), row),
                     (jax.ShapeDtypeStruct((L, NP), bf16), half)],
               extras=[(ya, row), (yb, row), (proj, half)], epilogue=epilogue)


def _ffn_up_dx(cfg, dhh, w_up):
    L, D, F = cfg.L, cfg.D, cfg.F
    Fh = F // 2
    tm, tn = _tile(L, 1024, 16), _tile(D, 512, 128)
    a_spec = pl.BlockSpec((None, tm, F), lambda i, j, k: (k, i, 0))
    b_spec = pl.BlockSpec((2, tn, Fh), lambda i, j, k: (k, j, 0))
    terms = [(0, lambda r: r[:, 0:Fh], 1, lambda r: r[0]), (0, lambda r: r[:, Fh:F], 1, lambda r: r[1])]
    out = (jax.ShapeDtypeStruct((L, D), bf16), pl.BlockSpec((tm, tn), lambda i, j, k: (i, j)))
    return _mm("ffn_up_dx", [(dhh, a_spec), (w_up, b_spec)], [(0, terms), (1, terms)], grid=(L // tm, D // tn, 2),
               contract=_NT, outs=[out], acc_shape=(tm, tn))[0]


def _rms_fwd(name, x, g):
    L, D = x.shape
    tm = _tile(L, 512, 16)

    def body(x_ref, g_ref, xn_ref, r_ref):
        xv = x_ref[...]
        r = lax.rsqrt(jnp.mean(xv * xv, axis=-1, keepdims=True) + _EPS)
        xn_ref[...] = (xv * r * g_ref[...]).astype(bf16)
        r_ref[...] = r

    return pl.pallas_call(
        body, grid=(L // tm,),
        in_specs=[pl.BlockSpec((tm, D), lambda i: (i, 0)), pl.BlockSpec((1, D), lambda i: (0, 0))],
        out_specs=[pl.BlockSpec((tm, D), lambda i: (i, 0)), pl.BlockSpec((tm, 1), lambda i: (i, 0))],
        out_shape=[jax.ShapeDtypeStruct((L, D), bf16), jax.ShapeDtypeStruct((L, 1), f32)],
        name=name, compiler_params=_cparams(("parallel",)))(x, g)


def _rms_bwd(name, dxn, h, r, g, dres, after=()):
    L, D = h.shape
    tm = _tile(L, 512, 16)

    def body(dxn_ref, h_ref, r_ref, g_ref, dres_ref, *rest):
        dh_ref, dhb_ref, dg_ref = rest[len(after):]
        i = pl.program_id(0)
        d = dxn_ref[...].astype(f32)
        hv = h_ref[...]
        rv = r_ref[...]
        dyg = d * g_ref[...]
        m = jnp.mean(dyg * hv, axis=-1, keepdims=True)
        dh = dres_ref[...] + rv * dyg - hv * (rv * rv * rv) * m
        dh_ref[...] = dh
        dhb_ref[...] = dh.astype(bf16)

        @pl.when(i == 0)
        def _():
            dg_ref[...] = jnp.zeros_like(dg_ref)

        dg_ref[...] += jnp.sum(d * hv * rv, axis=0, keepdims=True)

    row = lambda i: (i, 0)
    return pl.pallas_call(
        body, grid=(L // tm,),
        in_specs=[pl.BlockSpec((tm, D), row), pl.BlockSpec((tm, D), row), pl.BlockSpec((tm, 1), row),
                  pl.BlockSpec((1, D), lambda i: (0, 0)), pl.BlockSpec((tm, D), row)] + [_ANY] * len(after),
        out_specs=[pl.BlockSpec((tm, D), row), pl.BlockSpec((tm, D), row), pl.BlockSpec((1, D), lambda i: (0, 0))],
        out_shape=[jax.ShapeDtypeStruct((L, D), f32), jax.ShapeDtypeStruct((L, D), bf16), jax.ShapeDtypeStruct((1, D), f32)],
        name=name, compiler_params=_cparams(("arbitrary",)))(dxn, h, r, g, dres, *after)


def _loss_head(name, h2, tgt, g):
    L, D = h2.shape
    tm = _tile(L, 512, 16)

    def body(h_ref, t_ref, g_ref, dh_ref, dhb_ref, dg_ref, loss_ref):
        i = pl.program_id(0)
        hv = h_ref[...]
        gv = g_ref[...]
        r = lax.rsqrt(jnp.mean(hv * hv, axis=-1, keepdims=True) + _EPS)
        err = hv * r * gv - t_ref[...]
        dy = err * (1.0 / D)
        dyg = dy * gv
        m = jnp.mean(dyg * hv, axis=-1, keepdims=True)
        dh = r * dyg - hv * (r * r * r) * m
        dh_ref[...] = dh
        dhb_ref[...] = dh.astype(bf16)

        @pl.when(i == 0)
        def _():
            dg_ref[...] = jnp.zeros_like(dg_ref)
            loss_ref[...] = jnp.zeros_like(loss_ref)

        dg_ref[...] += jnp.sum(dy * hv * r, axis=0, keepdims=True)
        part = jnp.sum(jnp.sum(err * err, axis=-1, keepdims=True), axis=0, keepdims=True) * (0.5 / D)
        loss_ref[...] += jnp.broadcast_to(part, (8, 128))

    row = lambda i: (i, 0)
    return pl.pallas_call(
        body, grid=(L // tm,),
        in_specs=[pl.BlockSpec((tm, D), row), pl.BlockSpec((tm, D), row), pl.BlockSpec((1, D), lambda i: (0, 0))],
        out_specs=[pl.BlockSpec((tm, D), row), pl.BlockSpec((tm, D), row), pl.BlockSpec((1, D), lambda i: (0, 0)),
                   pl.BlockSpec((8, 128), lambda i: (0, 0))],
        out_shape=[jax.ShapeDtypeStruct((L, D), f32), jax.ShapeDtypeStruct((L, D), bf16),
                   jax.ShapeDtypeStruct((1, D), f32), jax.ShapeDtypeStruct((8, 128), f32)],
        name=name, compiler_params=_cparams(("arbitrary",)))(h2, tgt, g)


def _shift_down(tile, halo, k, rows8):
    tm = tile.shape[0]
    r = pltpu.roll(tile, k, axis=0)
    hh = pltpu.roll(halo, k, axis=0)
    top = jnp.where(rows8 < k, hh, r[:8])
    return jnp.concatenate([top, r[8:]], axis=0) if tm > 8 else top


def _shift_up(tile, halo, k, rows8):
    tm = tile.shape[0]
    r = pltpu.roll(tile, tm - k, axis=0)
    hh = pltpu.roll(halo, 8 - k, axis=0)
    bot = jnp.where(rows8 >= 8 - k, hh, r[tm - 8:])
    return jnp.concatenate([r[:tm - 8], bot], axis=0) if tm > 8 else bot


def _conv3(x, halo, w_ref, b_ref, rows8):
    return (w_ref[0:1, :] * _shift_down(x, halo, 2, rows8) + w_ref[1:2, :] * _shift_down(x, halo, 1, rows8)
            + w_ref[2:3, :] * x + b_ref[...])


def _convb_fwd(cfg, proj, w, b):
    L, CW = cfg.L, cfg.CW
    assert cfg.SW == CW
    tm = _tile(L, 512, 16)

    def body(v_ref, vh_ref, gb_ref, gc_ref, gch_ref, w_ref, b_ref, o_ref):
        i = pl.program_id(0)
        rows8 = lax.broadcasted_iota(jnp.int32, (8, CW), 0)
        cv = gc_ref[...].astype(f32) * v_ref[...].astype(f32)
        cvh = gch_ref[...].astype(f32)[8:] * vh_ref[...].astype(f32)[8:]
        cvh = jnp.where(i == 0, 0.0, cvh)
        cc = _conv3(cv, cvh, w_ref, b_ref, rows8)
        o_ref[...] = (gb_ref[...].astype(f32) * cc).astype(bf16)

    blk = lambda col: pl.BlockSpec((tm, CW), lambda i: (i, col))
    halo = lambda col: pl.BlockSpec((16, CW), lambda i: (jnp.maximum(i * (tm // 16) - 1, 0), col))
    return pl.pallas_call(
        body, grid=(L // tm,),
        in_specs=[blk(1), halo(1), blk(2), blk(3), halo(3),
                  pl.BlockSpec((3, CW), lambda i: (0, 0)), pl.BlockSpec((1, CW), lambda i: (0, 0))],
        out_specs=pl.BlockSpec((tm, CW), lambda i: (i, 0)),
        out_shape=jax.ShapeDtypeStruct((L, CW), bf16),
        name="convb_fwd", compiler_params=_cparams(("parallel",)))(proj, proj, proj, proj, proj, w, b)


def _convb_bwd(cfg, proj, dyb0, w, b, dproj):
    L, CW = cfg.L, cfg.CW
    tm = _tile(L, 512, 16)
    nt = L // tm

    def body(v_ref, vh_ref, gb_ref, gbn_ref, gc_ref, gch_ref, d_ref, dn_ref, w_ref, b_ref, dproj_ref,
             o_ref, dw_ref, db_ref):
        i = pl.program_id(0)
        rows8 = lax.broadcasted_iota(jnp.int32, (8, CW), 0)
        v = v_ref[...].astype(f32)
        gb = gb_ref[...].astype(f32)
        gc = gc_ref[...].astype(f32)
        d = d_ref[...].astype(f32)
        cv = gc * v
        cvh = gch_ref[...].astype(f32)[8:] * vh_ref[...].astype(f32)[8:]
        cvh = jnp.where(i == 0, 0.0, cvh)
        s2 = _shift_down(cv, cvh, 2, rows8)
        s1 = _shift_down(cv, cvh, 1, rows8)
        cc = w_ref[0:1, :] * s2 + w_ref[1:2, :] * s1 + w_ref[2:3, :] * cv + b_ref[...]
        dcc = d * gb
        dccn = dn_ref[...].astype(f32)[:8] * gbn_ref[...].astype(f32)[:8]
        dccn = jnp.where(i == nt - 1, 0.0, dccn)
        dcv = (w_ref[2:3, :] * dcc + w_ref[1:2, :] * _shift_up(dcc, dccn, 1, rows8)
               + w_ref[0:1, :] * _shift_up(dcc, dccn, 2, rows8))
        o_ref[:, 0:CW] = (dcv * gc).astype(bf16)
        o_ref[:, CW:2 * CW] = (d * cc).astype(bf16)
        o_ref[:, 2 * CW:3 * CW] = (dcv * v).astype(bf16)

        @pl.when(i == 0)
        def _():
            dw_ref[...] = jnp.zeros_like(dw_ref)
            db_ref[...] = jnp.zeros_like(db_ref)

        dw_ref[0:1, :] += jnp.sum(dcc * s2, axis=0, keepdims=True)
        dw_ref[1:2, :] += jnp.sum(dcc * s1, axis=0, keepdims=True)
        dw_ref[2:3, :] += jnp.sum(dcc * cv, axis=0, keepdims=True)
        db_ref[...] += jnp.sum(dcc, axis=0, keepdims=True)

    blk = lambda col: pl.BlockSpec((tm, CW), lambda i: (i, col))
    prev = lambda col: pl.BlockSpec((16, CW), lambda i: (jnp.maximum(i * (tm // 16) - 1, 0), col))
    nxt = lambda col: pl.BlockSpec((16, CW), lambda i: (jnp.minimum((i + 1) * (tm // 16), L // 16 - 1), col))
    const = lambda r: pl.BlockSpec((r, CW), lambda i: (0, 0))
    return pl.pallas_call(
        body, grid=(nt,),
        in_specs=[blk(1), prev(1), blk(2), nxt(2), blk(3), prev(3), blk(0), nxt(0), const(3), const(1),
                  pl.BlockSpec(memory_space=pl.ANY)],
        out_specs=[pl.BlockSpec((tm, 3 * CW), lambda i: (i, 0)), const(3), const(1)],
        out_shape=[jax.ShapeDtypeStruct(dproj.shape, bf16), jax.ShapeDtypeStruct((3, CW), f32),
                   jax.ShapeDtypeStruct((1, CW), f32)],
        input_output_aliases={10: 0},
        name="convb_bwd", compiler_params=_cparams(("arbitrary",)))(proj, proj, proj, proj, proj, proj, dyb0, dyb0, w, b,
                                                                    dproj)


def _ffn_act(cfg, hh, w, b):
    L, F = cfg.L, cfg.F
    tm = _tile(L, 512, 16)
    tc = _tile(F, 1408, 128)
    ncb = F // tc

    def body(a_ref, ah_ref, g_ref, w_ref, b_ref, o_ref, act_ref):
        i = pl.program_id(0)
        rows8 = lax.broadcasted_iota(jnp.int32, (8, tc), 0)
        a = a_ref[...].astype(f32)
        ah = jnp.where(i == 0, 0.0, ah_ref[...].astype(f32)[8:])
        act = _conv3(a, ah, w_ref, b_ref, rows8)
        act_ref[...] = act.astype(bf16)
        o_ref[...] = (_gelu(act) * g_ref[...].astype(f32)).astype(bf16)

    tile = pl.BlockSpec((tm, tc), lambda i, j: (i, j))
    return pl.pallas_call(
        body, grid=(L // tm, ncb),
        in_specs=[tile, pl.BlockSpec((16, tc), lambda i, j: (jnp.maximum(i * (tm // 16) - 1, 0), j)),
                  pl.BlockSpec((tm, tc), lambda i, j: (i, j + ncb)),
                  pl.BlockSpec((3, tc), lambda i, j: (0, j)), pl.BlockSpec((1, tc), lambda i, j: (0, j))],
        out_specs=[tile, tile], out_shape=[jax.ShapeDtypeStruct((L, F), bf16)] * 2,
        name="ffn_act", compiler_params=_cparams(("parallel", "parallel")))(hh, hh, hh, w, b)


def _ffn_down_dx_act_bwd(cfg, dh2b, w_down, hh, act, w):
    L, D, F = cfg.L, cfg.D, cfg.F
    S, Ks, _ = w_down.shape
    tm = _tile(L, 512, 16)
    tc = _tile(Ks, 1408, 128)
    ncb = F // tc
    nbs = Ks // tc
    nt = L // tm

    def body(dh_ref, wd_ref, a_ref, c_ref, g_ref, w_ref, dhh_ref, dw_ref, db_ref, below):
        i = pl.program_id(1)
        rows8 = lax.broadcasted_iota(jnp.int32, (8, tc), 0)
        d = lax.dot_general(dh_ref[...], wd_ref[...], _NT, preferred_element_type=f32)
        a = a_ref[...].astype(f32)
        gelu, gelu_d = _gelu_and_grad(c_ref[...].astype(f32))
        dhh_ref[1] = (d * gelu).astype(bf16)
        dact = d * g_ref[...].astype(f32) * gelu_d

        @pl.when(i == 0)
        def _():
            below[...] = jnp.zeros_like(below)
            dw_ref[...] = jnp.zeros_like(dw_ref)
            db_ref[...] = jnp.zeros_like(db_ref)

        dactn = below[...]
        up1 = _shift_up(dact, dactn, 1, rows8)
        up2 = _shift_up(dact, dactn, 2, rows8)
        below[...] = dact[:8]
        dhh_ref[0] = (w_ref[2:3, :] * dact + w_ref[1:2, :] * up1 + w_ref[0:1, :] * up2).astype(bf16)
        dw_ref[0:1, :] += jnp.sum(up2 * a, axis=0, keepdims=True)
        dw_ref[1:2, :] += jnp.sum(up1 * a, axis=0, keepdims=True)
        dw_ref[2:3, :] += jnp.sum(dact * a, axis=0, keepdims=True)
        db_ref[...] += jnp.sum(dact, axis=0, keepdims=True)

    up = lambda i: nt - 1 - i
    blk = lambda off: pl.BlockSpec((tm, tc), lambda j, i: (up(i), j + off))
    const = lambda r: pl.BlockSpec((r, tc), lambda j, i: (0, j))
    return pl.pallas_call(
        body, grid=(ncb, nt),
        in_specs=[pl.BlockSpec((tm, D), lambda j, i: (up(i), 0)),
                  pl.BlockSpec((None, tc, D), lambda j, i: (j // nbs, j % nbs, 0)),
                  blk(0), blk(0), blk(ncb), const(3)],
        out_specs=[pl.BlockSpec((2, tm, tc), lambda j, i: (0, up(i), j)), const(3), const(1)],
        out_shape=[jax.ShapeDtypeStruct((2, L, F), bf16),
                   jax.ShapeDtypeStruct((3, F), f32), jax.ShapeDtypeStruct((1, F), f32)],
        scratch_shapes=[pltpu.VMEM((8, tc), f32)],
        name="ffn_down_dx_act_bwd", compiler_params=_cparams(("parallel", "arbitrary")))(dh2b, w_down, hh, act, hh, w)


def _merge_fwd(cfg, ya1, yb0, proj, wso, wco):
    L, D, SW, CW = cfg.L, cfg.D, cfg.SW, cfg.CW
    Ns = D // _NCHIP
    tm = _tile(L, 1024, 16)
    tn = _tile(Ns, 512, 128)
    nb = Ns // tn
    off_a = (SW + 3 * CW) // tn
    off_b = (SW + 3 * CW + D) // tn

    def body(a_ref, b_ref, wa_ref, wb_ref, ma_ref, mb_ref, m_ref, ya_ref, yb_ref):
        ya = jnp.dot(a_ref[...], wa_ref[...], preferred_element_type=f32)
        yb = jnp.dot(b_ref[...], wb_ref[...], preferred_element_type=f32)
        sa = jax.nn.sigmoid(ma_ref[...].astype(f32))
        sb = jax.nn.sigmoid(mb_ref[...].astype(f32))
        m_ref[...] = (sa * ya + sb * yb).astype(bf16)
        ya_ref[...] = ya.astype(bf16)
        yb_ref[...] = yb.astype(bf16)

    o_spec = pl.BlockSpec((tm, tn), lambda i, j: (i, j))
    o_shape = jax.ShapeDtypeStruct((L, D), bf16)
    return pl.pallas_call(
        body, grid=(L // tm, D // tn),
        in_specs=[pl.BlockSpec((tm, SW), lambda i, j: (i, 0)), pl.BlockSpec((tm, CW), lambda i, j: (i, 0)),
                  pl.BlockSpec((None, SW, tn), lambda i, j: (j // nb, 0, j % nb)),
                  pl.BlockSpec((None, CW, tn), lambda i, j: (j // nb, 0, j % nb)),
                  pl.BlockSpec((tm, tn), lambda i, j: (i, off_a + j)), pl.BlockSpec((tm, tn), lambda i, j: (i, off_b + j))],
        out_specs=[o_spec, o_spec, o_spec], out_shape=[o_shape, o_shape, o_shape],
        name="merge_fwd", compiler_params=_cparams(("parallel", "parallel")))(ya1, yb0, wso, wco, proj, proj)


def _s5_dims(cfg):
    G = cfg.SW // _SSM_GROUP
    NS = G // _SLAB_GROUPS
    SC = _SLAB_GROUPS * _SSM_GROUP
    SH = _SLAB_GROUPS * _SSM_STATE
    NST = 2 * SH * NS
    return G, NS, SC, SH, NST


def _lane_slabs(cfg, W):
    _, NS, _, SH, _ = _s5_dims(cfg)
    return [(2 * SH * s + w0, 2 * SH * s + SH + w0) for s in range(NS) for w0 in range(0, SH, W)]


def _discretize(a_re, a_im, log_dt, b_re, b_im):
    dt = jnp.exp(log_dt)[:, None]
    mag = jnp.exp(dt * a_re)
    abr = mag * jnp.cos(dt * a_im)
    abi = mag * jnp.sin(dt * a_im)
    nr = abr - 1.0
    ni = abi
    den = a_re * a_re + a_im * a_im
    fr = (nr * a_re + ni * a_im) / den
    fi = (ni * a_re - nr * a_im) / den
    bbr = fr[..., None] * b_re - fi[..., None] * b_im
    bbi = fr[..., None] * b_im + fi[..., None] * b_re
    return abr, abi, bbr, bbi


def _state_rows(cfg, re, im):
    _, NS, _, SH, _ = _s5_dims(cfg)
    return jnp.concatenate([re.reshape(NS, SH), im.reshape(NS, SH)], axis=1).reshape(-1)


def _s5_tables(cfg, a_re, a_im, log_dt, abr, abi, bbr, bbi, c_re, c_im):
    G, NS, SC, SH, NST = _s5_dims(cfg)
    S = cfg.T // 8
    eye = jnp.eye(_SLAB_GROUPS, dtype=bf16)
    bb = jnp.stack([bbr, bbi]).reshape(2, NS, _SLAB_GROUPS, _SSM_STATE, _SSM_GROUP).astype(bf16)
    bs = (bb.transpose(1, 2, 4, 0, 3)[:, :, :, :, None, :] * eye[None, :, None, None, :, None]).reshape(NS, SC, 2 * SH)
    cc = jnp.stack([c_re, -c_im]).reshape(2, NS, _SLAB_GROUPS, _SSM_GROUP, _SSM_STATE).astype(bf16)
    cs = (cc.transpose(1, 0, 4, 2, 3)[:, :, None, :, :, :] * eye[None, None, :, None, :, None]).reshape(NS, 2 * SH, SC)
    arep = jnp.broadcast_to(_state_rows(cfg, abr, abi)[None, :], (8, NST))
    sdt = S * jnp.exp(log_dt)[:, None]
    mag = jnp.exp(sdt * a_re)
    apow = jnp.broadcast_to(_state_rows(cfg, mag * jnp.cos(sdt * a_im), mag * jnp.sin(sdt * a_im))[None, :], (8, NST))
    t = jnp.arange(cfg.T)
    perm = (t % 8) * S + t // 8
    pm = jax.nn.one_hot(perm, cfg.T, dtype=bf16)
    return bs, cs, arep, apow, pm, pm.T


def _cmul_add(ar, ai, xr, xi, br, bi):
    return ar * xr - ai * xi + br, ar * xi + ai * xr + bi


def _s5_forward_chunk(cfg, W, upb, bs_ref, arep_ref, apow_ref, st, x0, cin_store):
    _, NS, SC, SH, _ = _s5_dims(cfg)
    S = cfg.T // 8
    for s in range(NS):
        st[:, 2 * SH * s:2 * SH * (s + 1)] = jnp.dot(upb[:, SC * s:SC * (s + 1)], bs_ref[s], preferred_element_type=f32)
    rows = lax.broadcasted_iota(jnp.int32, (8, W), 0)
    zero = jnp.zeros((8, W), f32)
    for rc, ic in _lane_slabs(cfg, W):
        ar = arep_ref[:, rc:rc + W]
        ai = arep_ref[:, ic:ic + W]

        def step(i, carry, rc=rc, ic=ic, ar=ar, ai=ai):
            xr, xi = carry
            r0 = pl.multiple_of(i * 8, 8)
            nr, ni = _cmul_add(ar, ai, xr, xi, st[pl.ds(r0, 8), rc:rc + W], st[pl.ds(r0, 8), ic:ic + W])
            st[pl.ds(r0, 8), rc:rc + W] = nr
            st[pl.ds(r0, 8), ic:ic + W] = ni
            return nr, ni

        er, ei = lax.fori_loop(0, S, step, (zero, zero))
        pr = apow_ref[:, rc:rc + W]
        pi = apow_ref[:, ic:ic + W]
        x0r = x0[:, rc:rc + W]
        x0i = x0[:, ic:ic + W]
        cr = jnp.where(rows == 0, x0r, 0.0)
        ci = jnp.where(rows == 0, x0i, 0.0)
        for _ in range(7):
            fr, fi = _cmul_add(pr, pi, cr, ci, er, ei)
            cr = jnp.where(rows == 0, x0r, pltpu.roll(fr, 1, axis=0))
            ci = jnp.where(rows == 0, x0i, pltpu.roll(fi, 1, axis=0))
        fr, fi = _cmul_add(pr, pi, cr, ci, er, ei)
        x0[:, rc:rc + W] = jnp.broadcast_to(fr[7:8, :], (8, W))
        x0[:, ic:ic + W] = jnp.broadcast_to(fi[7:8, :], (8, W))
        cin_store(rc, ic, cr, ci)

        def fix(i, carry, rc=rc, ic=ic, ar=ar, ai=ai):
            kr, ki = carry
            r0 = pl.multiple_of(i * 8, 8)
            nr, ni = ar * kr - ai * ki, ar * ki + ai * kr
            st[pl.ds(r0, 8), rc:rc + W] = st[pl.ds(r0, 8), rc:rc + W] + nr
            st[pl.ds(r0, 8), ic:ic + W] = st[pl.ds(r0, 8), ic:ic + W] + ni
            return nr, ni

        lax.fori_loop(0, S, fix, (cr, ci))


def _s5_fwd(cfg, proj, tabs, dskip):
    L, SW, T = cfg.L, cfg.SW, cfg.T
    G, NS, SC, SH, NST = _s5_dims(cfg)
    bs, cs, arep, apow, pm, pmt = tabs
    W = min(512, SH)
    NC = L // T

    def body(u_ref, pm_ref, pmt_ref, bs_ref, cs_ref, arep_ref, apow_ref, dskip_ref, y_ref, ya0_ref, cin_ref, st, x0):
        c = pl.program_id(0)

        @pl.when(c == 0)
        def _():
            x0[...] = jnp.zeros_like(x0)

        up = jnp.dot(pm_ref[...], u_ref[...], preferred_element_type=f32)
        upb = up.astype(bf16)

        def cin_store(rc, ic, cr, ci):
            cin_ref[0, :, rc:rc + W] = cr
            cin_ref[0, :, ic:ic + W] = ci

        _s5_forward_chunk(cfg, W, upb, bs_ref, arep_ref, apow_ref, st, x0, cin_store)
        yp = jnp.concatenate(
            [jnp.dot(st[:, 2 * SH * s:2 * SH * (s + 1)].astype(bf16), cs_ref[s], preferred_element_type=f32)
             for s in range(NS)], axis=1) + dskip_ref[...] * up
        y = jnp.dot(pmt_ref[...], yp.astype(bf16), preferred_element_type=f32)
        y_ref[...] = y.astype(bf16)
        ya0_ref[...] = _gelu(y).astype(bf16)

    const2 = lambda shape: pl.BlockSpec(shape, lambda c: (0, 0))
    const3 = lambda shape: pl.BlockSpec(shape, lambda c: (0, 0, 0))
    return pl.pallas_call(
        body, grid=(NC,),
        in_specs=[pl.BlockSpec((T, SW), lambda c: (c, 0)), const2((T, T)), const2((T, T)), const3((NS, SC, 2 * SH)),
                  const3((NS, 2 * SH, SC)), const2((8, NST)), const2((8, NST)), const2((1, SW))],
        out_specs=[pl.BlockSpec((T, SW), lambda c: (c, 0)), pl.BlockSpec((T, SW), lambda c: (c, 0)),
                   pl.BlockSpec((1, 8, NST), lambda c: (c, 0, 0)), pl.BlockSpec((T, NST), lambda c: (c, 0))],
        out_shape=[jax.ShapeDtypeStruct((L, SW), bf16), jax.ShapeDtypeStruct((L, SW), bf16),
                   jax.ShapeDtypeStruct((NC, 8, NST), f32), jax.ShapeDtypeStruct((L, NST), f32)],
        scratch_shapes=[pltpu.VMEM((8, NST), f32)],
        name="s5_fwd", compiler_params=_cparams(("arbitrary",)))(proj, pm, pmt, bs, cs, arep, apow, dskip)


def _s5_bwd(cfg, proj, dy, cin, xs, tabs, dskip, dproj, after=()):
    L, SW, T = cfg.L, cfg.SW, cfg.T
    du_col = 3 * cfg.CW // SW
    G, NS, SC, SH, NST = _s5_dims(cfg)
    bs, cs, arep, apow, pm, pmt = tabs
    W = min(512, SH)
    S = T // 8
    NC = L // T

    def body(u_ref, dy_ref, cin_ref, st, pm_ref, pmt_ref, bs_ref, cs_ref, arep_ref, apow_ref, dskip_ref, dproj_ref, *rest):
        du_ref, da_ref, db_ref, dc_ref, dd_ref, gs, g0, db_acc, dc_acc = rest[len(after):]
        c = pl.program_id(0)

        @pl.when(c == 0)
        def _():
            g0[...] = jnp.zeros_like(g0)
            da_ref[...] = jnp.zeros_like(da_ref)
            dd_ref[...] = jnp.zeros_like(dd_ref)
            db_acc[...] = jnp.zeros_like(db_acc)
            dc_acc[...] = jnp.zeros_like(dc_acc)

        up = jnp.dot(pm_ref[...], u_ref[...], preferred_element_type=f32)
        upb = up.astype(bf16)
        dyp = jnp.dot(pm_ref[...], dy_ref[...], preferred_element_type=f32)
        dypb = dyp.astype(bf16)
        for s in range(NS):
            gs[:, 2 * SH * s:2 * SH * (s + 1)] = lax.dot_general(
                dypb[:, SC * s:SC * (s + 1)], cs_ref[s], (((1,), (1,)), ((), ())), preferred_element_type=f32)
        rows = lax.broadcasted_iota(jnp.int32, (8, W), 0)
        zero = jnp.zeros((8, W), f32)
        for rc, ic in _lane_slabs(cfg, W):
            ar = arep_ref[:, rc:rc + W]
            ai = arep_ref[:, ic:ic + W]

            def rstep(k, carry, rc=rc, ic=ic, ar=ar, ai=ai):
                gr, gi = carry
                r0 = pl.multiple_of((S - 1 - k) * 8, 8)
                nr = ar * gr + ai * gi + gs[pl.ds(r0, 8), rc:rc + W]
                ni = ar * gi - ai * gr + gs[pl.ds(r0, 8), ic:ic + W]
                gs[pl.ds(r0, 8), rc:rc + W] = nr
                gs[pl.ds(r0, 8), ic:ic + W] = ni
                return nr, ni

            er, ei = lax.fori_loop(0, S, rstep, (zero, zero))
            pr = apow_ref[:, rc:rc + W]
            pi = apow_ref[:, ic:ic + W]
            g0r = g0[:, rc:rc + W]
            g0i = g0[:, ic:ic + W]
            cr = jnp.where(rows == 7, g0r, 0.0)
            ci = jnp.where(rows == 7, g0i, 0.0)
            for _ in range(7):
                fr = er + pr * cr + pi * ci
                fi = ei + pr * ci - pi * cr
                cr = jnp.where(rows == 7, g0r, pltpu.roll(fr, 7, axis=0))
                ci = jnp.where(rows == 7, g0i, pltpu.roll(fi, 7, axis=0))
            fr = er + pr * cr + pi * ci
            fi = ei + pr * ci - pi * cr
            g0[:, rc:rc + W] = jnp.broadcast_to(fr[0:1, :], (8, W))
            g0[:, ic:ic + W] = jnp.broadcast_to(fi[0:1, :], (8, W))

            def fix(k, carry, rc=rc, ic=ic, ar=ar, ai=ai):
                kr, ki, accr, acci = carry
                i = S - 1 - k
                r0 = pl.multiple_of(i * 8, 8)
                rp = pl.multiple_of((i - 1) * 8, 8)
                nr = ar * kr + ai * ki
                ni = ar * ki - ai * kr
                gr = gs[pl.ds(r0, 8), rc:rc + W] + nr
                gi = gs[pl.ds(r0, 8), ic:ic + W] + ni
                gs[pl.ds(r0, 8), rc:rc + W] = gr
                gs[pl.ds(r0, 8), ic:ic + W] = gi
                xr = st[pl.ds(rp, 8), rc:rc + W]
                xi = st[pl.ds(rp, 8), ic:ic + W]
                return nr, ni, accr + gr * xr + gi * xi, acci + gi * xr - gr * xi

            kr, ki, accr, acci = lax.fori_loop(0, S - 1, fix, (cr, ci, zero, zero))
            nr = ar * kr + ai * ki
            ni = ar * ki - ai * kr
            gr = gs[0:8, rc:rc + W] + nr
            gi = gs[0:8, ic:ic + W] + ni
            gs[0:8, rc:rc + W] = gr
            gs[0:8, ic:ic + W] = gi
            xr = cin_ref[0, :, rc:rc + W]
            xi = cin_ref[0, :, ic:ic + W]
            da_ref[:, rc:rc + W] += accr + gr * xr + gi * xi
            da_ref[:, ic:ic + W] += acci + gi * xr - gr * xi

        dups = []
        for s in range(NS):
            gsb = gs[:, 2 * SH * s:2 * SH * (s + 1)].astype(bf16)
            dups.append(lax.dot_general(gsb, bs_ref[s], (((1,), (1,)), ((), ())), preferred_element_type=f32))
            db_acc[s] += lax.dot_general(upb[:, SC * s:SC * (s + 1)], gsb, (((0,), (0,)), ((), ())),
                                         preferred_element_type=f32)
            dc_acc[s] += lax.dot_general(st[:, 2 * SH * s:2 * SH * (s + 1)].astype(bf16), dypb[:, SC * s:SC * (s + 1)],
                                         (((0,), (0,)), ((), ())), preferred_element_type=f32)
        dup = jnp.concatenate(dups, axis=1) + dskip_ref[...] * dyp
        du_ref[...] = jnp.dot(pmt_ref[...], dup.astype(bf16), preferred_element_type=f32).astype(bf16)
        dd_ref[...] += jnp.sum(dyp * up, axis=0, keepdims=True)

        @pl.when(c == NC - 1)
        def _():
            PS, GH = _SSM_STATE, _SSM_GROUP
            mask_b = (lax.broadcasted_iota(jnp.int32, (SC, SH), 0) // GH
                      == lax.broadcasted_iota(jnp.int32, (SC, SH), 1) // PS)
            mask_c = (lax.broadcasted_iota(jnp.int32, (SH, SC), 0) // PS
                      == lax.broadcasted_iota(jnp.int32, (SH, SC), 1) // GH)
            for s in range(NS):
                for r in range(2):
                    xb = jnp.where(mask_b, db_acc[s, :, r * SH:(r + 1) * SH], 0.0)
                    zb = xb[:, 0:128]
                    for q in range(1, SH // 128):
                        zb = zb + xb[:, q * 128:(q + 1) * 128]
                    db_ref[s, r] = zb + pltpu.roll(zb, PS, axis=1)
                    xc = jnp.where(mask_c, dc_acc[s, r * SH:(r + 1) * SH, :], 0.0)
                    zc = xc[0:PS]
                    for q in range(1, SH // PS):
                        zc = zc + xc[q * PS:(q + 1) * PS]
                    dc_ref[s, r] = zc

    rev = lambda c: (NC - 1 - c, 0)
    const2 = lambda shape: pl.BlockSpec(shape, lambda c: (0, 0))
    const3 = lambda shape: pl.BlockSpec(shape, lambda c: (0, 0, 0))
    const4 = lambda shape: pl.BlockSpec(shape, lambda c: (0, 0, 0, 0))
    return pl.pallas_call(
        body, grid=(NC,),
        in_specs=[pl.BlockSpec((T, SW), rev), pl.BlockSpec((T, SW), rev), pl.BlockSpec((1, 8, NST), lambda c: (NC - 1 - c, 0, 0)),
                  pl.BlockSpec((T, NST), rev), const2((T, T)), const2((T, T)),
                  pl.BlockSpec((NS, SC, 2 * SH), lambda c: (0, 0, 0), pipeline_mode=pl.Buffered(1)),
                  pl.BlockSpec((NS, 2 * SH, SC), lambda c: (0, 0, 0), pipeline_mode=pl.Buffered(1)),
                  const2((8, NST)), const2((8, NST)), const2((1, SW)), _ANY] + [_ANY] * len(after),
        out_specs=[pl.BlockSpec((T, SW), lambda c: (NC - 1 - c, du_col)), const2((8, NST)),
                   const4((NS, 2, SC, 128)), const4((NS, 2, _SSM_STATE, SC)), const2((1, SW))],
        out_shape=[jax.ShapeDtypeStruct(dproj.shape, bf16), jax.ShapeDtypeStruct((8, NST), f32),
                   jax.ShapeDtypeStruct((NS, 2, SC, 128), f32), jax.ShapeDtypeStruct((NS, 2, _SSM_STATE, SC), f32),
                   jax.ShapeDtypeStruct((1, SW), f32)],
        scratch_shapes=[pltpu.VMEM((T, NST), f32), pltpu.VMEM((8, NST), f32),
                        pltpu.VMEM((NS, SC, 2 * SH), f32), pltpu.VMEM((NS, 2 * SH, SC), f32)],
        input_output_aliases={11: 0},
        name="s5_bwd", compiler_params=_cparams(("arbitrary",), _VMEM_S5_BWD))(proj, dy, cin, xs, pm, pmt, bs, cs, arep, apow, dskip, dproj,
                                                                  *after)


def _s5_param_grads(cfg, da, db_diag, dc_diag):
    G, NS, SC, SH, NST = _s5_dims(cfg)
    das = da.sum(axis=0).reshape(NS, 2, SH)
    dabr = das[:, 0].reshape(G, _SSM_STATE)
    dabi = das[:, 1].reshape(G, _SSM_STATE)
    dbd = db_diag[..., :_SSM_STATE].reshape(NS, 2, _SLAB_GROUPS, _SSM_GROUP, _SSM_STATE)
    dbb = dbd.transpose(1, 0, 2, 4, 3).reshape(2, G, _SSM_STATE, _SSM_GROUP)
    dcd = dc_diag.reshape(NS, 2, _SSM_STATE, _SLAB_GROUPS, _SSM_GROUP)
    dcc = dcd.transpose(1, 0, 3, 4, 2).reshape(2, G, _SSM_GROUP, _SSM_STATE)
    return dabr, dabi, dbb[0], dbb[1], dcc[0], -dcc[1]


def _coords():
    return lax.axis_index("x"), lax.axis_index("y"), lax.axis_index("c")


def _other_chips(x, y):
    return [(1 - x, y), (x, 1 - y), (1 - x, 1 - y)]


def _allreduce8(name, v):
    R = v.shape[0]

    def body(v_ref, o_ref, sib, chips, mine, ssem, rsem):
        x, y, c = _coords()
        d2d = pltpu.make_async_remote_copy(src_ref=v_ref, dst_ref=sib, send_sem=ssem.at[0], recv_sem=rsem.at[0],
                                           device_id=(x, y, 1 - c), device_id_type=_MESH)
        d2d.start()
        d2d.wait()
        mine[...] = v_ref[...] + sib[...]
        cps = [pltpu.make_async_remote_copy(src_ref=mine, dst_ref=chips.at[j], send_sem=ssem.at[1 + j],
                                            recv_sem=rsem.at[1 + j], device_id=(*chip, c), device_id_type=_MESH)
               for j, chip in enumerate(_other_chips(x, y))]
        for cp in cps:
            cp.start()
        for cp in cps:
            cp.wait()
        o_ref[...] = (mine[...] + chips[1]) + (chips[0] + chips[2])

    vm = pl.BlockSpec(memory_space=pltpu.VMEM)
    return pl.pallas_call(
        body, in_specs=[vm], out_specs=vm, out_shape=jax.ShapeDtypeStruct((R, 128), f32),
        scratch_shapes=[pltpu.VMEM((R, 128), f32), pltpu.VMEM((3, R, 128), f32), pltpu.VMEM((R, 128), f32),
                        pltpu.SemaphoreType.DMA((4,)), pltpu.SemaphoreType.DMA((4,))],
        name=name, compiler_params=pltpu.CompilerParams(vmem_limit_bytes=_VMEM_LIMIT))(v)


def _cast_into_slot(name, w, k_idx, also_alone=False):
    R, C = w.shape
    tr = _tile(R, 256, 16)

    def body(k_ref, w_ref, *o_refs):
        for o_ref in o_refs:
            o_ref[...] = w_ref[...].astype(bf16)

    slot = (jax.ShapeDtypeStruct((_NCHIP, R, C), bf16), pl.BlockSpec((None, tr, C), lambda r, kr: (kr[0], r, 0)))
    alone = (jax.ShapeDtypeStruct((R, C), bf16), pl.BlockSpec((tr, C), lambda r, kr: (r, 0)))
    outs = [slot, alone] if also_alone else [slot]
    gs = pltpu.PrefetchScalarGridSpec(
        num_scalar_prefetch=1, grid=(R // tr,),
        in_specs=[pl.BlockSpec((tr, C), lambda r, kr: (r, 0))], out_specs=[o[1] for o in outs])
    res = pl.pallas_call(body, grid_spec=gs, out_shape=[o[0] for o in outs], name=name,
                         compiler_params=_cparams(("parallel",)))(k_idx, w)
    return res if also_alone else res[0]


def _in_proj(cfg, xn1, w_own, w_in, k_idx):
    L, D = xn1.shape
    S, _, Ns = w_in.shape
    tm, tn = _tile(L, 1024, 16), _tile(Ns, 1024, 128)
    nb = Ns // tn

    def body(k_ref, a_ref, w_ref, *rest):
        rest[-1][...] = jnp.dot(a_ref[...], w_ref[...], preferred_element_type=f32).astype(bf16)

    a_spec = pl.BlockSpec((tm, D), lambda i, j, kr: (i, 0))
    shape = jax.ShapeDtypeStruct((L, S * Ns), bf16)
    own = pltpu.PrefetchScalarGridSpec(
        num_scalar_prefetch=1, grid=(L // tm, nb),
        in_specs=[a_spec, pl.BlockSpec((D, tn), lambda i, j, kr: (0, j))],
        out_specs=pl.BlockSpec((tm, tn), lambda i, j, kr: (i, kr[0] * nb + j)))
    proj = pl.pallas_call(body, grid_spec=own, out_shape=shape, name="in_proj_own",
                          compiler_params=_cparams(("parallel", "parallel")))(k_idx, xn1, w_own)
    shard = lambda j, kr: (kr[0] + 1 + j // nb) % S
    rest = pltpu.PrefetchScalarGridSpec(
        num_scalar_prefetch=1, grid=(L // tm, (S - 1) * nb),
        in_specs=[a_spec, pl.BlockSpec((None, D, tn), lambda i, j, kr: (shard(j, kr), 0, j % nb)), _ANY],
        out_specs=pl.BlockSpec((tm, tn), lambda i, j, kr: (i, shard(j, kr) * nb + j % nb)))
    return pl.pallas_call(body, grid_spec=rest, out_shape=shape, input_output_aliases={3: 0}, name="in_proj",
                          compiler_params=_cparams(("parallel", "parallel")))(k_idx, xn1, w_in, proj)


def _handshake(peers):
    barrier = pltpu.get_barrier_semaphore()
    for peer in peers:
        pl.semaphore_signal(barrier, inc=1, device_id=peer, device_id_type=_MESH)
    pl.semaphore_wait(barrier, len(peers))


def _allgather_weights(name, bufs, collective_id):
    n = len(bufs)
    refs = [jax.new_ref(b, memory_space=pltpu.MemorySpace.HBM) for b in bufs]

    def copy(ref, sems, idx, to):
        return pltpu.make_async_remote_copy(src_ref=ref, dst_ref=ref, send_sem=sems[0].at[idx], recv_sem=sems[1].at[idx],
                                            device_id=to, device_id_type=_MESH)

    def launch(ssem, rsem, qssem, qrsem, fssem, frsem):
        x, y, c = _coords()
        k = 2 * x + y
        nbrs = [(1 - x, y), (x, 1 - y)]
        across = 2 * (1 - x) + (1 - y)
        sibling = (x, y, 1 - c)
        _handshake([sibling] + [(*chip, c) for chip in nbrs])
        started = []

        def start(cp):
            cp.start()
            started.append(cp)

        for w in range(n):
            rh = refs[w].shape[1] // 2
            for j, chip in enumerate(nbrs):
                start(copy(refs[w].at[k, pl.ds(c * rh, rh)], (ssem, rsem), (w, j), (*chip, c)))
        for w in range(n):
            rh = refs[w].shape[1] // 2
            rq = rh // 2
            for j, (ox, oy) in enumerate(nbrs):
                ko = 2 * ox + oy
                landed = refs[w].at[ko, pl.ds(c * rh, rh)]
                copy(landed, (ssem, rsem), (w, j), (ox, oy, c)).wait_recv()
                start(copy(refs[w].at[ko, pl.ds(c * rh + j * rq, rq)], (qssem, qrsem), (w, j), (*nbrs[1 - j], c)))
                start(copy(landed, (fssem, frsem), (w, j), sibling))
        for w in range(n):
            rh = refs[w].shape[1] // 2
            rq = rh // 2
            for q in range(2):
                quarter = refs[w].at[across, pl.ds(c * rh + q * rq, rq)]
                copy(quarter, (qssem, qrsem), (w, q), (*nbrs[1 - q], c)).wait_recv()
            start(copy(refs[w].at[across, pl.ds(c * rh, rh)], (fssem, frsem), (w, 2), sibling))
        for w in range(n):
            rh = refs[w].shape[1] // 2
            for j, ko in enumerate([2 * nbrs[0][0] + nbrs[0][1], 2 * nbrs[1][0] + nbrs[1][1], across]):
                copy(refs[w].at[ko, pl.ds((1 - c) * rh, rh)], (fssem, frsem), (w, j), sibling).wait_recv()
        for cp in started:
            cp.wait_send()

    _sequencer_kernel(name, collective_id,
                      (pltpu.SemaphoreType.DMA((n, 2)), pltpu.SemaphoreType.DMA((n, 2)), pltpu.SemaphoreType.DMA((n, 2)),
                       pltpu.SemaphoreType.DMA((n, 2)), pltpu.SemaphoreType.DMA((n, 3)), pltpu.SemaphoreType.DMA((n, 3))),
                      launch)
    return [r[...] for r in refs]


def _sequencer_kernel(name, collective_id, sems, body):
    pl.kernel(body, mesh=plsc.ScalarSubcoreMesh(axis_name="seq", num_cores=1), name=name, scratch_types=sems,
              compiler_params=pltpu.CompilerParams(collective_id=collective_id))()


def _hbm_ref(a):
    return jax.new_ref(a, memory_space=pltpu.MemorySpace.HBM)


def _exchange_halves(name, grads, collective_id):
    n = len(grads)
    srcs = [_hbm_ref(g) for g in grads]
    dsts = [jax.empty_ref(jax.ShapeDtypeStruct((g.shape[0], g.shape[1] // 2, g.shape[2]), g.dtype),
                          memory_space=pltpu.MemorySpace.HBM) for g in grads]

    def body(ssem, rsem):
        x, y, c = _coords()
        _handshake([(x, y, 1 - c)])
        cps = []
        for w in range(n):
            rh = srcs[w].shape[1] // 2
            cp = pltpu.make_async_remote_copy(
                src_ref=srcs[w].at[:, pl.ds((1 - c) * rh, rh)], dst_ref=dsts[w], send_sem=ssem.at[w], recv_sem=rsem.at[w],
                device_id=(x, y, 1 - c), device_id_type=_MESH)
            cp.start()
            cps.append(cp)
        for cp in cps:
            cp.wait()

    _sequencer_kernel(name, collective_id, (pltpu.SemaphoreType.DMA((n,)), pltpu.SemaphoreType.DMA((n,))), body)
    return [d[...] for d in dsts]


def _scatter_shards(name, parts, collective_id):
    n = len(parts)
    srcs = [_hbm_ref(p) for p in parts]
    dsts = [jax.empty_ref(jax.ShapeDtypeStruct((3,) + p.shape[1:], p.dtype), memory_space=pltpu.MemorySpace.HBM)
            for p in parts]

    def body(ssem, rsem):
        x, y, c = _coords()
        k = 2 * x + y
        others = _other_chips(x, y)
        _handshake([(*chip, c) for chip in others])
        cps = []
        for w in range(n):
            for j, (ox, oy) in enumerate(others):
                cp = pltpu.make_async_remote_copy(
                    src_ref=srcs[w].at[(2 * ox + oy - k + 3) % 4], dst_ref=dsts[w].at[j], send_sem=ssem.at[w, j],
                    recv_sem=rsem.at[w, j],
                    device_id=(ox, oy, c), device_id_type=_MESH)
                cp.start()
                cps.append(cp)
        for cp in cps:
            cp.wait()

    _sequencer_kernel(name, collective_id, (pltpu.SemaphoreType.DMA((n, 3)), pltpu.SemaphoreType.DMA((n, 3))), body)
    return [d[...] for d in dsts]


def _join_halves(name, bufs, collective_id):
    n = len(bufs)
    refs = [_hbm_ref(b) for b in bufs]

    def body(ssem, rsem):
        x, y, c = _coords()
        _handshake([(x, y, 1 - c)])
        cps = []
        for w in range(n):
            rh = refs[w].shape[0] // 2
            mine = refs[w].at[pl.ds(c * rh, rh)]
            cp = pltpu.make_async_remote_copy(src_ref=mine, dst_ref=mine, send_sem=ssem.at[w], recv_sem=rsem.at[w],
                                              device_id=(x, y, 1 - c), device_id_type=_MESH)
            cp.start()
            cps.append(cp)
        for w in range(n):
            rh = refs[w].shape[0] // 2
            theirs = refs[w].at[pl.ds((1 - c) * rh, rh)]
            pltpu.make_async_remote_copy(src_ref=theirs, dst_ref=theirs, send_sem=ssem.at[w], recv_sem=rsem.at[w],
                                         device_id=(x, y, 1 - c), device_id_type=_MESH).wait_recv()
        for cp in cps:
            cp.wait_send()

    _sequencer_kernel(name, collective_id, (pltpu.SemaphoreType.DMA((n,)), pltpu.SemaphoreType.DMA((n,))), body)
    return [r[...] for r in refs]


def _allgather_small(name, v, collective_id):
    R = v.shape[0]
    src = _hbm_ref(v)
    dst = jax.empty_ref(jax.ShapeDtypeStruct((8, R, 128), v.dtype), memory_space=pltpu.MemorySpace.HBM)
    flips = [(dx, dy, dc) for dx in (0, 1) for dy in (0, 1) for dc in (0, 1)][1:]

    def body(lsem, ssem, rsem):
        x, y, c = _coords()
        me = 4 * x + 2 * y + c
        flip = lambda v, d: 1 - v if d else v
        peers = [(flip(x, dx), flip(y, dy), flip(c, dc)) for dx, dy, dc in flips]
        _handshake(peers)
        own = pltpu.make_async_copy(src, dst.at[me], lsem)
        own.start()
        cps = []
        for r, peer in enumerate(peers):
            cp = pltpu.make_async_remote_copy(src_ref=src, dst_ref=dst.at[me], send_sem=ssem.at[r], recv_sem=rsem.at[r],
                                              device_id=peer, device_id_type=_MESH)
            cp.start()
            cps.append(cp)
        for r, (px, py, pc) in enumerate(peers):
            theirs = dst.at[4 * px + 2 * py + pc]
            pltpu.make_async_remote_copy(src_ref=theirs, dst_ref=theirs, send_sem=ssem.at[r], recv_sem=rsem.at[r],
                                         device_id=(px, py, pc), device_id_type=_MESH).wait_recv()
        for cp in cps:
            cp.wait_send()
        own.wait()

    _sequencer_kernel(name, collective_id, (pltpu.SemaphoreType.DMA, pltpu.SemaphoreType.DMA((7,)),
                                            pltpu.SemaphoreType.DMA((7,))), body)
    return dst[...]


def _sum8(name, g8, after):
    R = g8.shape[1]
    tr = _tile(R, 512, 8)

    def body(g_ref, *rest):
        rest[-1][...] = (((g_ref[0] + g_ref[1]) + (g_ref[2] + g_ref[3]))
                         + ((g_ref[4] + g_ref[5]) + (g_ref[6] + g_ref[7])))

    return pl.pallas_call(body, grid=(R // tr,),
                          in_specs=[pl.BlockSpec((8, tr, 128), lambda i: (0, i, 0))] + [_ANY] * len(after),
                          out_specs=pl.BlockSpec((tr, 128), lambda i: (i, 0)), out_shape=jax.ShapeDtypeStruct((R, 128), f32),
                          name=name, compiler_params=_cparams(("parallel",)))(g8, *after)


def _add_own_half(name, g, t, kc_idx, after):
    S, R, C = g.shape
    rh = R // 2
    tr = _tile(rh, 512, 16)
    nrb = rh // tr
    shard = lambda s, kc: (kc[0] + 1 + s) % S

    def body(kc_ref, g_ref, t_ref, *rest):
        rest[-1][...] = (g_ref[...] + t_ref[...]).astype(bf16)

    gs = pltpu.PrefetchScalarGridSpec(
        num_scalar_prefetch=1, grid=(S - 1, nrb),
        in_specs=[pl.BlockSpec((None, tr, C), lambda s, r, kc: (shard(s, kc), kc[1] * nrb + r, 0)),
                  pl.BlockSpec((None, tr, C), lambda s, r, kc: (shard(s, kc), r, 0))] + [_ANY] * len(after),
        out_specs=pl.BlockSpec((None, tr, C), lambda s, r, kc: (s, r, 0)))
    return pl.pallas_call(body, grid_spec=gs, out_shape=jax.ShapeDtypeStruct((S - 1, rh, C), bf16), name=name,
                          compiler_params=_cparams(("parallel", "parallel")))(kc_idx, g, t, *after)


def _add_shard_parts(name, g, t, r, kc_idx, after):
    S, R, C = g.shape
    rh = R // 2
    tr = _tile(rh, 256, 16)
    nrb = rh // tr

    def body(kc_ref, g_ref, t_ref, r_ref, *rest):
        own = g_ref[...] + t_ref[...]
        rest[-1][...] = (own + r_ref[1].astype(f32)) + (r_ref[0].astype(f32) + r_ref[2].astype(f32))

    gs = pltpu.PrefetchScalarGridSpec(
        num_scalar_prefetch=1, grid=(nrb,),
        in_specs=[pl.BlockSpec((None, tr, C), lambda i, kc: (kc[0], kc[1] * nrb + i, 0)),
                  pl.BlockSpec((None, tr, C), lambda i, kc: (kc[0], i, 0)),
                  pl.BlockSpec((3, tr, C), lambda i, kc: (0, i, 0))] + [_ANY] * len(after),
        out_specs=pl.BlockSpec((tr, C), lambda i, kc: (kc[1] * nrb + i, 0)))
    return pl.pallas_call(body, grid_spec=gs, out_shape=jax.ShapeDtypeStruct((R, C), f32), name=name,
                          compiler_params=_cparams(("parallel",)))(kc_idx, g, t, r, *after)


def _adamw_update(wv, gv, mv, vv):
    nm = _ADAM_B1 * mv + (1.0 - _ADAM_B1) * gv
    nv = _ADAM_B2 * vv + (1.0 - _ADAM_B2) * (gv * gv)
    m_hat = nm / (1.0 - _ADAM_B1 ** _ADAM_STEP)
    v_hat = nv / (1.0 - _ADAM_B2 ** _ADAM_STEP)
    return -_ADAM_LR * (m_hat / (jnp.sqrt(v_hat) + _ADAM_EPS) + _ADAM_WD * wv), nm, nv


def _adamw(name, w, g, m, v, after=()):
    R, C = w.shape
    tr = _tile(R, 256, 8)

    def body(w_ref, g_ref, m_ref, v_ref, *rest):
        go_ref, d_ref, nm_ref, nv_ref = rest[len(after):]
        gv = g_ref[...]
        go_ref[...] = gv
        d_ref[...], nm_ref[...], nv_ref[...] = _adamw_update(w_ref[...], gv, m_ref[...], v_ref[...])

    spec = pl.BlockSpec((tr, C), lambda i: (i, 0))
    shape = jax.ShapeDtypeStruct((R, C), f32)
    return pl.pallas_call(body, grid=(R // tr,), in_specs=[spec] * 4 + [_ANY] * len(after), out_specs=[spec] * 4,
                          out_shape=[shape] * 4, name=name, compiler_params=_cparams(("parallel",)))(w, g, m, v, *after)


def _adamw_whole(name, w, g, m, v):
    def body(w_ref, g_ref, m_ref, v_ref, d_ref, nm_ref, nv_ref):
        d_ref[...], nm_ref[...], nv_ref[...] = _adamw_update(w_ref[...], g_ref[...], m_ref[...], v_ref[...])

    vm = pl.BlockSpec(memory_space=pltpu.VMEM)
    return pl.pallas_call(body, in_specs=[vm] * 4, out_specs=[vm] * 3, out_shape=[jax.ShapeDtypeStruct(w.shape, f32)] * 3,
                          name=name, compiler_params=pltpu.CompilerParams(vmem_limit_bytes=_VMEM_LIMIT))(w, g, m, v)


def _pack(arrs):
    flat = jnp.concatenate([a.reshape(-1).astype(f32) for a in arrs])
    n = flat.shape[0]
    pad = (-n) % (128 * 128)
    return jnp.pad(flat, (0, pad)).reshape(-1, 128)


def _unpack(packed, shapes):
    flat = packed.reshape(-1)
    out, off = [], 0
    for s in shapes:
        n = math.prod(s)
        out.append(flat[off:off + n].reshape(s))
        off += n
    return out


_BIG = ("w_in", "w_glu", "w_ssm_out", "w_conv_out", "w_o", "w_up", "w_down")
_SMALL = ("norm_tok", "a_re", "a_im", "log_dt", "b_re", "b_im", "c_re", "c_im", "d_skip", "conv_w", "conv_b",
          "norm_ffn", "ffn_conv_w", "ffn_conv_b", "norm_final")
_WEIGHTS = ("norm_tok", "w_in", "a_re", "a_im", "log_dt", "b_re", "b_im", "c_re", "c_im", "d_skip", "w_glu",
            "w_ssm_out", "conv_w", "conv_b", "w_conv_out", "w_o", "norm_ffn", "w_up", "ffn_conv_w", "ffn_conv_b",
            "w_down", "norm_final")


def _step(cfg, x, tgt, p, m, v):
    L, D, SW, CW, F = cfg.L, cfg.D, cfg.SW, cfg.CW, cfg.F
    xi, yi, ci = _coords()
    k_idx = (2 * xi + yi).astype(jnp.int32).reshape(1)
    c_idx = ci.astype(jnp.int32).reshape(1)
    x = x.reshape(L, D)
    tgt = tgt.reshape(L, D)

    big2d = {n: p[n].reshape(p[n].shape[-2], p[n].shape[-1]) for n in _BIG}
    slots = {n: _cast_into_slot("cast_" + n, big2d[n], k_idx) for n in _BIG if n != "w_in"}
    slots["w_in"], w_in_own = _cast_into_slot("cast_w_in", big2d["w_in"], k_idx, also_alone=True)
    wg = {}
    for cid, (gname, group) in enumerate((("allgather_w_in", ("w_in",)),
                                          ("allgather_w_mixer", ("w_glu", "w_ssm_out", "w_conv_out", "w_o")),
                                          ("allgather_w_up", ("w_up",)), ("allgather_w_down", ("w_down",)))):
        wg.update(zip(group, _allgather_weights(gname, [slots[n] for n in group], cid)))
    w_in, w_so, w_co, w_up = wg["w_in"], wg["w_ssm_out"], wg["w_conv_out"], wg["w_up"]
    w_glu, w_o, w_down = wg["w_glu"], wg["w_o"], wg["w_down"]
    kk = k_idx[0]
    cw_full = lax.dynamic_update_slice(jnp.zeros((3, CW), f32), p["conv_w"].reshape(3, CW // _NCHIP), (0, kk * (CW // _NCHIP)))
    fw_full = lax.dynamic_update_slice(jnp.zeros((3, F), f32), p["ffn_conv_w"].reshape(3, F // _NCHIP), (0, kk * (F // _NCHIP)))
    south = (ci == 0).astype(f32)
    filters8 = _allgather_small("allgather_conv_filters", _pack([cw_full * south, fw_full * south]), 17)
    conv_b = p["conv_b"].reshape(1, CW)
    ffn_conv_b = p["ffn_conv_b"].reshape(1, F)
    norm_tok = p["norm_tok"].reshape(1, D)
    norm_ffn = p["norm_ffn"].reshape(1, D)
    norm_final = p["norm_final"].reshape(1, D)
    dskip = p["d_skip"].reshape(1, SW)

    s5_in = (p["a_re"][0], p["a_im"][0], p["log_dt"][0], p["b_re"][0], p["b_im"][0])
    (abr, abi, bbr, bbi), disc_vjp = jax.vjp(_discretize, *s5_in)
    tabs = _s5_tables(cfg, *s5_in[:3], abr, abi, bbr, bbi, p["c_re"][0], p["c_im"][0])

    xn1, r1 = _rms_fwd("rms_tok", x, norm_tok)
    proj = _in_proj(cfg, xn1, w_in_own, w_in, k_idx)
    conv_w, ffn_conv_w = _unpack(_sum8("sum_conv_filters", filters8, [proj]), [(3, CW), (3, F)])
    y_s, ya0, cin, xs = _s5_fwd(cfg, proj, tabs, dskip)

    def tiles(*arrs):
        return lambda tm, tn: [(a, pl.BlockSpec((tm, tn), lambda i, j: (i, j))) for a in arrs]

    def glu_epi(acc, e, o):
        o[0][...] = (e[0][...].astype(f32) * jax.nn.sigmoid(acc)).astype(bf16)
        o[1][...] = acc.astype(bf16)

    ya1, z = _mm_nn("glu", ya0, w_glu, [bf16, bf16], rows=True, extras_fn=tiles(ya0), epilogue=glu_epi)
    yb0 = _convb_fwd(cfg, proj, conv_w, conv_b)
    merged, ya, yb = _merge_fwd(cfg, ya1, yb0, proj, w_so, w_co)

    def res_epi(acc, e, o):
        o[0][...] = e[0][...] + acc

    h1 = _mm_nn("out_proj", merged, w_o, [f32], rows=True, extras_fn=tiles(x), epilogue=res_epi)[0]
    xn2, r2 = _rms_fwd("rms_ffn", h1, norm_ffn)
    hh = _mm_nn("ffn_up", xn2, w_up, [bf16], tn=2816)[0]
    fact, ffn_pre = _ffn_act(cfg, hh, ffn_conv_w, ffn_conv_b)
    h2 = _mm_nn("ffn_down", fact, w_down, [f32], tm=512, rows=True, extras_fn=tiles(h1), epilogue=res_epi)[0]
    dh2, dh2b, g_norm_final, loss_tile = _loss_head("loss_head", h2, tgt, norm_final)

    kc_idx = jnp.concatenate([k_idx, c_idx])
    reduced, chains = {}, {}

    def rs_halves(tag, collective_id, names, gs):
        chains[tag] = dict(cid=collective_id, names=names, gs=gs,
                           sib=_exchange_halves("grad_halves_" + tag, gs, collective_id))

    def rs_shards(tag, after):
        ch = chains[tag]
        ch["parts"] = [_add_own_half("grad_add_halves_" + n, g, t, kc_idx, after)
                       for n, g, t in zip(ch["names"], ch["gs"], ch["sib"])]
        ch["chips"] = _scatter_shards("grad_shards_" + tag, ch["parts"], ch["cid"] + 1)
        return ch["parts"]

    def rs_join(tag, after):
        ch = chains[tag]
        ch["halves"] = [_add_shard_parts("grad_add_chips_" + n, g, t, r, kc_idx, after)
                        for n, g, t, r in zip(ch["names"], ch["gs"], ch["sib"], ch["chips"])]
        reduced.update(zip(ch["names"], _join_halves("grad_join_" + tag, ch["halves"], ch["cid"] + 2)))
        return ch["halves"]

    g_w_down = _mm_tn("ffn_down_dw", fact, dh2b, tm=1408, tn=512)
    rs_halves("ffn_down", 4, ["w_down"], [g_w_down.reshape(_NCHIP, F // _NCHIP, D)])
    dhh, g_ffn_conv_w, g_ffn_conv_b = _ffn_down_dx_act_bwd(cfg, dh2b, w_down, hh, ffn_pre, ffn_conv_w)
    sent = rs_shards("ffn_down", [dhh])
    g_w_up = _mm_tn("ffn_up_dw", xn2, dhh, shards=_NCHIP, tm=512, tn=1408, b_resident=True, after=sent)
    rs_halves("ffn_up", 7, ["w_up"], [g_w_up])
    dxn2 = _ffn_up_dx(cfg, dhh, w_up)
    sent = rs_shards("ffn_up", [dxn2]) + rs_join("ffn_down", [dxn2])
    dh1, dh1b, g_norm_ffn = _rms_bwd("rms_ffn_bwd", dxn2, h1, r2, norm_ffn, dh2, after=sent)

    dya, dyb, dproj = _out_proj_dx(cfg, dh1b, w_o, ya, yb, proj)
    g_w_o = _mm_tn("out_proj_dw", merged, dh1b)

    def glu_bwd_epi(acc, e, o):
        a0 = e[0][...].astype(f32)
        s = jax.nn.sigmoid(e[1][...].astype(f32))
        o[0][...] = (acc * a0 * s * (1.0 - s)).astype(bf16)
        o[1][...] = (acc * s).astype(bf16)

    dz, t1 = _mm_nt("ssm_out_dx", dya, w_so, [bf16, bf16], extras_fn=tiles(ya0, z), epilogue=glu_bwd_epi)
    g_w_so = _mm_tn("ssm_out_dw", ya1, dya, shards=_NCHIP, tn=512)
    dyb0 = _mm_nt("conv_out_dx", dyb, w_co, [bf16])[0]
    g_w_co = _mm_tn("conv_out_dw", yb0, dyb, shards=_NCHIP, tn=512)
    dproj, g_conv_w, g_conv_b = _convb_bwd(cfg, proj, dyb0, conv_w, conv_b, dproj)

    def gelu_bwd_epi(acc, e, o):
        o[0][...] = ((e[0][...].astype(f32) + acc) * _gelu_grad(e[1][...].astype(f32))).astype(bf16)

    dy_s = _mm_nt("glu_dx", dz, w_glu, [bf16], rows=True, extras_fn=tiles(t1, y_s), epilogue=gelu_bwd_epi)[0]
    g_w_glu = _mm_tn("glu_dw", ya0, dz)
    rs_halves("mixer", 10, ["w_o", "w_ssm_out", "w_conv_out", "w_glu"],
              [g_w_o.reshape(_NCHIP, D // _NCHIP, D), g_w_so, g_w_co, g_w_glu.reshape(_NCHIP, SW // _NCHIP, SW)])
    sent = rs_join("ffn_up", [g_w_glu])
    dproj, da_acc, db_full, dc_full, g_dskip = _s5_bwd(cfg, proj, dy_s, cin, xs, tabs, dskip, dproj, after=sent)
    sent = rs_shards("mixer", [dproj])

    dabr, dabi, dbbr, dbbi, g_c_re, g_c_im = _s5_param_grads(cfg, da_acc, db_full, dc_full)
    g_a_re, g_a_im, g_log_dt, g_b_re, g_b_im = disc_vjp((dabr, dabi, dbbr, dbbi))
    small_g = {"a_re": g_a_re, "a_im": g_a_im, "log_dt": g_log_dt, "b_re": g_b_re, "b_im": g_b_im,
               "c_re": g_c_re, "c_im": g_c_im, "d_skip": g_dskip, "conv_w": g_conv_w, "conv_b": g_conv_b,
               "norm_ffn": g_norm_ffn, "ffn_conv_w": g_ffn_conv_w, "ffn_conv_b": g_ffn_conv_b, "norm_final": g_norm_final}
    early = [n for n in _SMALL if n != "norm_tok"]
    small8 = _allgather_small("allgather_small_grads", _pack([small_g[n] for n in early]), 16)

    g_w_in = _mm_tn("in_proj_dw", xn1, dproj, shards=_NCHIP, tn=CW, after=sent,
                    b_block=lambda j: jnp.where(j == 0, 3 * CW // SW, jnp.where(j < 4, j - 1, j)))
    rs_halves("in_proj", 13, ["w_in"], [g_w_in])
    dxn1 = _in_proj_dx(cfg, dproj, w_in)
    sent = rs_shards("in_proj", [dxn1]) + rs_join("mixer", [dxn1])
    dx, _, g_norm_tok = _rms_bwd("rms_tok_bwd", dxn1, x, r1, norm_tok, dh1, after=sent)

    summed = dict(zip(early, _unpack(_sum8("sum_small_grads", small8, [dx]), [small_g[n].shape for n in early])))
    summed["norm_tok"] = _unpack(_allreduce8("allreduce_norm_tok", _pack([g_norm_tok])), [g_norm_tok.shape])[0]
    summed["conv_w"] = lax.dynamic_slice(summed["conv_w"], (0, kk * (CW // _NCHIP)), (3, CW // _NCHIP))
    summed["ffn_conv_w"] = lax.dynamic_slice(summed["ffn_conv_w"], (0, kk * (F // _NCHIP)), (3, F // _NCHIP))

    grads, deltas, new_m, new_v = {}, {}, {}, {}

    def adamw_big(names, after):
        for n in names:
            g_, d_, m_, v_ = _adamw("adamw_" + n, big2d[n], reduced[n], m[n].reshape(big2d[n].shape),
                                    v[n].reshape(big2d[n].shape), after=after)
            grads[n], deltas[n], new_m[n], new_v[n] = (a.reshape(p[n].shape) for a in (g_, d_, m_, v_))
            after = [d_]
        return after

    for n in _SMALL:
        grads[n] = summed[n].reshape(p[n].shape)
        deltas[n], new_m[n], new_v[n] = _adamw_whole("adamw_" + n, p[n], grads[n], m[n], v[n])
    done = adamw_big(["w_down", "w_up", "w_o", "w_ssm_out", "w_conv_out", "w_glu"], [deltas["norm_final"]])
    rs_join("in_proj", done + [deltas[n] for n in _SMALL])
    adamw_big(["w_in"], ())

    loss = lax.psum(loss_tile[0, 0], ("x", "y", "c"))
    return (loss, dx.reshape(1, L, D), *[grads[n] for n in _WEIGHTS], *[deltas[n] for n in _WEIGHTS],
            *[new_m[n] for n in _WEIGHTS], *[new_v[n] for n in _WEIGHTS])


def kernel(x, norm_tok, w_in, a_re, a_im, log_dt, b_re, b_im, c_re, c_im, d_skip, w_glu, w_ssm_out, conv_w, conv_b, w_conv_out, w_o, norm_ffn, w_up, ffn_conv_w, ffn_conv_b, w_down, norm_final, loss_target, m_norm_tok, m_w_in, m_a_re, m_a_im, m_log_dt, m_b_re, m_b_im, m_c_re, m_c_im, m_d_skip, m_w_glu, m_w_ssm_out, m_conv_w, m_conv_b, m_w_conv_out, m_w_o, m_norm_ffn, m_w_up, m_ffn_conv_w, m_ffn_conv_b, m_w_down, m_norm_final, v_norm_tok, v_w_in, v_a_re, v_a_im, v_log_dt, v_b_re, v_b_im, v_c_re, v_c_im, v_d_skip, v_w_glu, v_w_ssm_out, v_conv_w, v_conv_b, v_w_conv_out, v_w_o, v_norm_ffn, v_w_up, v_ffn_conv_w, v_ffn_conv_b, v_w_down, v_norm_final):
    p = dict(norm_tok=norm_tok, w_in=w_in, a_re=a_re, a_im=a_im, log_dt=log_dt, b_re=b_re, b_im=b_im, c_re=c_re,
             c_im=c_im, d_skip=d_skip, w_glu=w_glu, w_ssm_out=w_ssm_out, conv_w=conv_w, conv_b=conv_b,
             w_conv_out=w_conv_out, w_o=w_o, norm_ffn=norm_ffn, w_up=w_up, ffn_conv_w=ffn_conv_w,
             ffn_conv_b=ffn_conv_b, w_down=w_down, norm_final=norm_final)
    m = dict(norm_tok=m_norm_tok, w_in=m_w_in, a_re=m_a_re, a_im=m_a_im, log_dt=m_log_dt, b_re=m_b_re, b_im=m_b_im,
             c_re=m_c_re, c_im=m_c_im, d_skip=m_d_skip, w_glu=m_w_glu, w_ssm_out=m_w_ssm_out, conv_w=m_conv_w,
             conv_b=m_conv_b, w_conv_out=m_w_conv_out, w_o=m_w_o, norm_ffn=m_norm_ffn, w_up=m_w_up,
             ffn_conv_w=m_ffn_conv_w, ffn_conv_b=m_ffn_conv_b, w_down=m_w_down, norm_final=m_norm_final)
    v = dict(norm_tok=v_norm_tok, w_in=v_w_in, a_re=v_a_re, a_im=v_a_im, log_dt=v_log_dt, b_re=v_b_re, b_im=v_b_im,
             c_re=v_c_re, c_im=v_c_im, d_skip=v_d_skip, w_glu=v_w_glu, w_ssm_out=v_w_ssm_out, conv_w=v_conv_w,
             conv_b=v_conv_b, w_conv_out=v_w_conv_out, w_o=v_w_o, norm_ffn=v_norm_ffn, w_up=v_w_up,
             ffn_conv_w=v_ffn_conv_w, ffn_conv_b=v_ffn_conv_b, w_down=v_w_down, norm_final=v_norm_final)
    return _step(_Cfg(), x, loss_target, p, m, v)
```

```python
import functools
import math
from typing import NamedTuple

import jax
import jax.numpy as jnp
from jax import lax
from jax.experimental import pallas as pl
from jax.experimental.pallas import tpu as pltpu
from jax.experimental.pallas import tpu_sc as plsc

f32 = jnp.float32
bf16 = jnp.bfloat16
_MESH = pl.DeviceIdType.MESH

_EPS = 1e-6
_ADAM_LR = 0.001
_ADAM_B1 = 0.9
_ADAM_B2 = 0.999
_ADAM_EPS = 1e-08
_ADAM_WD = 0.01
_ADAM_STEP = 10
_SSM_GROUP = 16
_SSM_STATE = 64
_SLAB_GROUPS = 16
_NCHIP = 4
_VMEM_LIMIT = 56 * 2**20
_VMEM_S5_BWD = 62 * 2**20
_GELU_C = math.sqrt(2.0 / math.pi)
_GELU_A = 0.044715


class _Cfg(NamedTuple):
    L: int = 4096
    D: int = 2048
    SW: int = 1024
    CW: int = 1024
    F: int = 5632
    T: int = 256


def _tile(n, pref, align):
    t = min(n, pref)
    t -= t % align
    while t > align and n % t:
        t -= align
    assert t > 0 and n % t == 0, (n, pref, align)
    return t


def _cparams(sem, vmem_limit=_VMEM_LIMIT):
    return pltpu.CompilerParams(dimension_semantics=sem, vmem_limit_bytes=vmem_limit)


def _gelu(x):
    return _gelu_and_grad(x)[0]


def _gelu_grad(x):
    return _gelu_and_grad(x)[1]


def _gelu_and_grad(x):
    x2 = x * x
    th = jnp.tanh(x * (_GELU_C + (_GELU_C * _GELU_A) * x2))
    half = 0.5 + 0.5 * th
    return x * half, half + (0.5 * x) * (1.0 - th * th) * (_GELU_C + (3.0 * _GELU_C * _GELU_A) * x2)


_NN = (((1,), (0,)), ((), ()))
_NT = (((1,), (1,)), ((), ()))
_TN = (((0,), (0,)), ((), ()))


def _whole(ref):
    return ref[...]


_ANY = pl.BlockSpec(memory_space=pl.ANY)


def _mm(name, operands, steps, *, grid, contract, outs, extras=(), epilogue=None, acc_shape=None, after=()):
    nop, ne, na = len(operands), len(extras), len(after)
    nk = len(steps)

    def body(*refs):
        op_refs = refs[:nop]
        e_refs = refs[nop:nop + ne]
        o_refs = refs[nop + ne + na:nop + ne + na + len(outs)]

        def partial(terms):
            tot = None
            for ai, av, bi, bv in terms:
                d = lax.dot_general(av(op_refs[ai]), bv(op_refs[bi]), contract, preferred_element_type=f32)
                tot = d if tot is None else tot + d
            return tot

        def finish(res):
            if epilogue is None:
                for o in o_refs:
                    o[...] = res.astype(o.dtype)
            else:
                epilogue(res, e_refs, o_refs)

        if nk == 1:
            finish(partial(steps[0][1]))
            return
        acc = refs[-1]
        kid = pl.program_id(len(grid) - 1)
        for k, terms in steps:
            def run(k=k, terms=terms):
                d = partial(terms)
                if k == 0:
                    acc[...] = d
                elif k < nk - 1:
                    acc[...] += d
                else:
                    finish(acc[...] + d)

            pl.when(kid == k)(run)

    sem = ("parallel",) * (len(grid) - (nk > 1)) + (("arbitrary",) if nk > 1 else ())
    return pl.pallas_call(
        body, grid=grid, in_specs=[o[1] for o in operands] + [e[1] for e in extras] + [_ANY] * na,
        out_specs=[o[1] for o in outs], out_shape=[o[0] for o in outs],
        scratch_shapes=[pltpu.VMEM(acc_shape, f32)] if nk > 1 else [], name=name,
        compiler_params=_cparams(sem))(*[o[0] for o in operands], *[e[0] for e in extras], *after)


def _mm_nn(name, a, w, out_dtypes, *, tm=1024, tn=1024, rows=False, extras_fn=None, epilogue=None):
    M, K = a.shape
    S, Ns = w.shape[0], w.shape[-1]
    N = Ns if rows else Ns * S
    tm, tn = _tile(M, tm, 16), _tile(Ns, tn, 128)
    nb = Ns // tn
    a_spec = pl.BlockSpec((tm, K), lambda i, j: (i, 0))
    if rows:
        b_spec = pl.BlockSpec((S, K // S, tn), lambda i, j: (0, 0, j))
        b_view = lambda r: r[...].reshape(K, tn)
    else:
        b_spec = pl.BlockSpec((None, K, tn), lambda i, j: (j // nb, 0, j % nb))
        b_view = _whole
    o_spec = pl.BlockSpec((tm, tn), lambda i, j: (i, j))
    outs = [(jax.ShapeDtypeStruct((M, N), dt), o_spec) for dt in out_dtypes]
    extras = extras_fn(tm, tn) if extras_fn is not None else ()
    return _mm(name, [(a, a_spec), (w, b_spec)], [(None, [(0, _whole, 1, b_view)])], grid=(M // tm, N // tn),
               contract=_NN, outs=outs, extras=extras, epilogue=epilogue)


def _mm_nt(name, a, w, out_dtypes, *, tm=1024, tn=1024, rows=False, extras_fn=None, epilogue=None):
    M, N = a.shape
    S, Ks, Ns = w.shape
    K = Ks * S if rows else Ks
    tm = _tile(M, tm, 16)
    a_spec = pl.BlockSpec((tm, N), lambda i, j: (i, 0))
    if rows:
        whole_shards = tn > Ks and tn % Ks == 0 and K % tn == 0
        tn = K if tn >= K else (tn if whole_shards else _tile(Ks, tn, 128))
        if tn == K or whole_shards:
            b_spec = pl.BlockSpec((tn // Ks, Ks, N), lambda i, j: (j, 0, 0))
            terms = [(0, _whole, 1, lambda r: r[...].reshape(tn, N))]
        else:
            nbs = Ks // tn
            b_spec = pl.BlockSpec((None, tn, N), lambda i, j: (j // nbs, j % nbs, 0))
            terms = [(0, _whole, 1, _whole)]
    else:
        tn = _tile(K, tn, 128)
        assert S * Ns == N
        b_spec = pl.BlockSpec((S, tn, Ns), lambda i, j: (0, j, 0))
        terms = [(0, lambda r, s=s: r[:, s * Ns:(s + 1) * Ns], 1, lambda r, s=s: r[s]) for s in range(S)]
    o_spec = pl.BlockSpec((tm, tn), lambda i, j: (i, j))
    outs = [(jax.ShapeDtypeStruct((M, K), dt), o_spec) for dt in out_dtypes]
    extras = extras_fn(tm, tn) if extras_fn is not None else ()
    return _mm(name, [(a, a_spec), (w, b_spec)], [(None, terms)], grid=(M // tm, K // tn), contract=_NT,
               outs=outs, extras=extras, epilogue=epilogue)


def _mm_tn(name, a, b, *, shards=None, tm=1024, tn=1024, b_block=None, b_resident=False, after=()):
    M, K = a.shape
    halves = b.shape[0] if b.ndim == 3 else 1
    Nh = b.shape[-1]
    N = Nh * halves
    Ns = N // shards if shards else N
    tm, tn = _tile(K, tm, 128), _tile(math.gcd(Ns, Nh), tn, 128)
    nb, nbh = Ns // tn, Nh // tn
    ij = (lambda g0, g1: (g1, g0)) if b_resident else (lambda g0, g1: (g0, g1))
    bmap = b_block if b_block is not None else (lambda j: j)
    a_spec = pl.BlockSpec((M, tm), lambda g0, g1: (0, ij(g0, g1)[0]))
    if halves > 1:
        b_spec = pl.BlockSpec((None, M, tn), lambda g0, g1: (bmap(ij(g0, g1)[1]) // nbh, 0, bmap(ij(g0, g1)[1]) % nbh))
    else:
        b_spec = pl.BlockSpec((M, tn), lambda g0, g1: (0, bmap(ij(g0, g1)[1])))
    if shards:
        out = (jax.ShapeDtypeStruct((shards, K, Ns), f32),
               pl.BlockSpec((None, tm, tn), lambda g0, g1: (ij(g0, g1)[1] // nb, ij(g0, g1)[0], ij(g0, g1)[1] % nb)))
    else:
        out = (jax.ShapeDtypeStruct((K, N), f32), pl.BlockSpec((tm, tn), lambda g0, g1: ij(g0, g1)))
    grid = (N // tn, K // tm) if b_resident else (K // tm, N // tn)
    return _mm(name, [(a, a_spec), (b, b_spec)], [(None, [(0, _whole, 1, _whole)])], grid=grid, contract=_TN,
               outs=[out], after=after)[0]


def _in_proj_dx(cfg, dproj, w_in):
    L, D, SW, CW = cfg.L, cfg.D, cfg.SW, cfg.CW
    NP = SW + 3 * CW + 2 * D
    Ns = NP // _NCHIP
    assert SW + CW == Ns and 2 * CW == Ns and D == Ns
    tm, tn = _tile(L, 1024, 16), _tile(D, 1024, 128)
    a_spec = pl.BlockSpec((tm, NP // 2), lambda i, j, k: (i, k))
    b_spec = pl.BlockSpec((2, tn, Ns), lambda i, j, k: (k, j, 0))
    first = [(0, lambda r: r[:, 0:CW], 1, lambda r: r[0, :, SW:SW + CW]),
             (0, lambda r: r[:, CW:3 * CW], 1, lambda r: r[1]),
             (0, lambda r: r[:, 3 * CW:3 * CW + SW], 1, lambda r: r[0, :, 0:SW])]
    second = [(0, lambda r: r[:, 0:D], 1, lambda r: r[0]), (0, lambda r: r[:, D:2 * D], 1, lambda r: r[1])]
    out = (jax.ShapeDtypeStruct((L, D), bf16), pl.BlockSpec((tm, tn), lambda i, j, k: (i, j)))
    return _mm("in_proj_dx", [(dproj, a_spec), (w_in, b_spec)], [(0, first), (1, second)], grid=(L // tm, D // tn, 2),
               contract=_NT, outs=[out], acc_shape=(tm, tn))[0]


def _out_proj_dx(cfg, dh1b, w_o, ya, yb, proj):
    L, D = cfg.L, cfg.D
    NP = cfg.SW + 3 * cfg.CW + 2 * D
    assert NP == 4 * D
    tm = _tile(L, 512, 16)

    def epilogue(acc, e, o):
        sa = jax.nn.sigmoid(e[2][:, 0:D].astype(f32))
        sb = jax.nn.sigmoid(e[2][:, D:2 * D].astype(f32))
        o[0][...] = (acc * sa).astype(bf16)
        o[1][...] = (acc * sb).astype(bf16)
        o[2][:, 0:D] = (acc * e[0][...].astype(f32) * sa * (1.0 - sa)).astype(bf16)
        o[2][:, D:2 * D] = (acc * e[1][...].astype(f32) * sb * (1.0 - sb)).astype(bf16)

    row = pl.BlockSpec((tm, D), lambda i, j: (i, 0))
    half = pl.BlockSpec((tm, 2 * D), lambda i, j: (i, 1))
    return _mm("out_proj_dx", [(dh1b, row), (w_o, pl.BlockSpec(w_o.shape, lambda i, j: (0, 0, 0),
                                                                pipeline_mode=pl.Buffered(1)))],
               [(None, [(0, _whole, 1, lambda r: r[...].reshape(D, D))])], grid=(L // tm, 1), contract=_NT,
               outs=[(jax.ShapeDtypeStruct((L, D), bf16), row), (jax.ShapeDtypeStruct((L, D), bf16), row),
                     (jax.ShapeDtypeStruct((L, NP), bf16), half)],
               extras=[(ya, row), (yb, row), (proj, half)], epilogue=epilogue)


def _ffn_up_dx(cfg, dhh, w_up):
    L, D, F = cfg.L, cfg.D, cfg.F
    Fh = F // 2
    tm, tn = _tile(L, 1024, 16), _tile(D, 512, 128)
    a_spec = pl.BlockSpec((None, tm, F), lambda i, j, k: (k, i, 0))
    b_spec = pl.BlockSpec((2, tn, Fh), lambda i, j, k: (k, j, 0))
    terms = [(0, lambda r: r[:, 0:Fh], 1, lambda r: r[0]), (0, lambda r: r[:, Fh:F], 1, lambda r: r[1])]
    out = (jax.ShapeDtypeStruct((L, D), bf16), pl.BlockSpec((tm, tn), lambda i, j, k: (i, j)))
    return _mm("ffn_up_dx", [(dhh, a_spec), (w_up, b_spec)], [(0, terms), (1, terms)], grid=(L // tm, D // tn, 2),
               contract=_NT, outs=[out], acc_shape=(tm, tn))[0]


def _rms_fwd(name, x, g):
    L, D = x.shape
    tm = _tile(L, 512, 16)

    def body(x_ref, g_ref, xn_ref, r_ref):
        xv = x_ref[...]
        r = lax.rsqrt(jnp.mean(xv * xv, axis=-1, keepdims=True) + _EPS)
        xn_ref[...] = (xv * r * g_ref[...]).astype(bf16)
        r_ref[...] = r

    return pl.pallas_call(
        body, grid=(L // tm,),
        in_specs=[pl.BlockSpec((tm, D), lambda i: (i, 0)), pl.BlockSpec((1, D), lambda i: (0, 0))],
        out_specs=[pl.BlockSpec((tm, D), lambda i: (i, 0)), pl.BlockSpec((tm, 1), lambda i: (i, 0))],
        out_shape=[jax.ShapeDtypeStruct((L, D), bf16), jax.ShapeDtypeStruct((L, 1), f32)],
        name=name, compiler_params=_cparams(("parallel",)))(x, g)


def _rms_bwd(name, dxn, h, r, g, dres, after=()):
    L, D = h.shape
    tm = _tile(L, 512, 16)

    def body(dxn_ref, h_ref, r_ref, g_ref, dres_ref, *rest):
        dh_ref, dhb_ref, dg_ref = rest[len(after):]
        i = pl.program_id(0)
        d = dxn_ref[...].astype(f32)
        hv = h_ref[...]
        rv = r_ref[...]
        dyg = d * g_ref[...]
        m = jnp.mean(dyg * hv, axis=-1, keepdims=True)
        dh = dres_ref[...] + rv * dyg - hv * (rv * rv * rv) * m
        dh_ref[...] = dh
        dhb_ref[...] = dh.astype(bf16)

        @pl.when(i == 0)
        def _():
            dg_ref[...] = jnp.zeros_like(dg_ref)

        dg_ref[...] += jnp.sum(d * hv * rv, axis=0, keepdims=True)

    row = lambda i: (i, 0)
    return pl.pallas_call(
        body, grid=(L // tm,),
        in_specs=[pl.BlockSpec((tm, D), row), pl.BlockSpec((tm, D), row), pl.BlockSpec((tm, 1), row),
                  pl.BlockSpec((1, D), lambda i: (0, 0)), pl.BlockSpec((tm, D), row)] + [_ANY] * len(after),
        out_specs=[pl.BlockSpec((tm, D), row), pl.BlockSpec((tm, D), row), pl.BlockSpec((1, D), lambda i: (0, 0))],
        out_shape=[jax.ShapeDtypeStruct((L, D), f32), jax.ShapeDtypeStruct((L, D), bf16), jax.ShapeDtypeStruct((1, D), f32)],
        name=name, compiler_params=_cparams(("arbitrary",)))(dxn, h, r, g, dres, *after)


def _loss_head(name, h2, tgt, g):
    L, D = h2.shape
    tm = _tile(L, 512, 16)

    def body(h_ref, t_ref, g_ref, dh_ref, dhb_ref, dg_ref, loss_ref):
        i = pl.program_id(0)
        hv = h_ref[...]
        gv = g_ref[...]
        r = lax.rsqrt(jnp.mean(hv * hv, axis=-1, keepdims=True) + _EPS)
        err = hv * r * gv - t_ref[...]
        dy = err * (1.0 / D)
        dyg = dy * gv
        m = jnp.mean(dyg * hv, axis=-1, keepdims=True)
        dh = r * dyg - hv * (r * r * r) * m
        dh_ref[...] = dh
        dhb_ref[...] = dh.astype(bf16)

        @pl.when(i == 0)
        def _():
            dg_ref[...] = jnp.zeros_like(dg_ref)
            loss_ref[...] = jnp.zeros_like(loss_ref)

        dg_ref[...] += jnp.sum(dy * hv * r, axis=0, keepdims=True)
        part = jnp.sum(jnp.sum(err * err, axis=-1, keepdims=True), axis=0, keepdims=True) * (0.5 / D)
        loss_ref[...] += jnp.broadcast_to(part, (8, 128))

    row = lambda i: (i, 0)
    return pl.pallas_call(
        body, grid=(L // tm,),
        in_specs=[pl.BlockSpec((tm, D), row), pl.BlockSpec((tm, D), row), pl.BlockSpec((1, D), lambda i: (0, 0))],
        out_specs=[pl.BlockSpec((tm, D), row), pl.BlockSpec((tm, D), row), pl.BlockSpec((1, D), lambda i: (0, 0)),
                   pl.BlockSpec((8, 128), lambda i: (0, 0))],
        out_shape=[jax.ShapeDtypeStruct((L, D), f32), jax.ShapeDtypeStruct((L, D), bf16),
                   jax.ShapeDtypeStruct((1, D), f32), jax.ShapeDtypeStruct((8, 128), f32)],
        name=name, compiler_params=_cparams(("arbitrary",)))(h2, tgt, g)


def _shift_down(tile, halo, k, rows8):
    tm = tile.shape[0]
    r = pltpu.roll(tile, k, axis=0)
    hh = pltpu.roll(halo, k, axis=0)
    top = jnp.where(rows8 < k, hh, r[:8])
    return jnp.concatenate([top, r[8:]], axis=0) if tm > 8 else top


def _shift_up(tile, halo, k, rows8):
    tm = tile.shape[0]
    r = pltpu.roll(tile, tm - k, axis=0)
    hh = pltpu.roll(halo, 8 - k, axis=0)
    bot = jnp.where(rows8 >= 8 - k, hh, r[tm - 8:])
    return jnp.concatenate([r[:tm - 8], bot], axis=0) if tm > 8 else bot


def _conv3(x, halo, w_ref, b_ref, rows8):
    return (w_ref[0:1, :] * _shift_down(x, halo, 2, rows8) + w_ref[1:2, :] * _shift_down(x, halo, 1, rows8)
            + w_ref[2:3, :] * x + b_ref[...])


def _convb_fwd(cfg, proj, w, b):
    L, CW = cfg.L, cfg.CW
    assert cfg.SW == CW
    tm = _tile(L, 512, 16)

    def body(v_ref, vh_ref, gb_ref, gc_ref, gch_ref, w_ref, b_ref, o_ref):
        i = pl.program_id(0)
        rows8 = lax.broadcasted_iota(jnp.int32, (8, CW), 0)
        cv = gc_ref[...].astype(f32) * v_ref[...].astype(f32)
        cvh = gch_ref[...].astype(f32)[8:] * vh_ref[...].astype(f32)[8:]
        cvh = jnp.where(i == 0, 0.0, cvh)
        cc = _conv3(cv, cvh, w_ref, b_ref, rows8)
        o_ref[...] = (gb_ref[...].astype(f32) * cc).astype(bf16)

    blk = lambda col: pl.BlockSpec((tm, CW), lambda i: (i, col))
    halo = lambda col: pl.BlockSpec((16, CW), lambda i: (jnp.maximum(i * (tm // 16) - 1, 0), col))
    return pl.pallas_call(
        body, grid=(L // tm,),
        in_specs=[blk(1), halo(1), blk(2), blk(3), halo(3),
                  pl.BlockSpec((3, CW), lambda i: (0, 0)), pl.BlockSpec((1, CW), lambda i: (0, 0))],
        out_specs=pl.BlockSpec((tm, CW), lambda i: (i, 0)),
        out_shape=jax.ShapeDtypeStruct((L, CW), bf16),
        name="convb_fwd", compiler_params=_cparams(("parallel",)))(proj, proj, proj, proj, proj, w, b)


def _convb_bwd(cfg, proj, dyb0, w, b, dproj):
    L, CW = cfg.L, cfg.CW
    tm = _tile(L, 512, 16)
    nt = L // tm

    def body(v_ref, vh_ref, gb_ref, gbn_ref, gc_ref, gch_ref, d_ref, dn_ref, w_ref, b_ref, dproj_ref,
             o_ref, dw_ref, db_ref):
        i = pl.program_id(0)
        rows8 = lax.broadcasted_iota(jnp.int32, (8, CW), 0)
        v = v_ref[...].astype(f32)
        gb = gb_ref[...].astype(f32)
        gc = gc_ref[...].astype(f32)
        d = d_ref[...].astype(f32)
        cv = gc * v
        cvh = gch_ref[...].astype(f32)[8:] * vh_ref[...].astype(f32)[8:]
        cvh = jnp.where(i == 0, 0.0, cvh)
        s2 = _shift_down(cv, cvh, 2, rows8)
        s1 = _shift_down(cv, cvh, 1, rows8)
        cc = w_ref[0:1, :] * s2 + w_ref[1:2, :] * s1 + w_ref[2:3, :] * cv + b_ref[...]
        dcc = d * gb
        dccn = dn_ref[...].astype(f32)[:8] * gbn_ref[...].astype(f32)[:8]
        dccn = jnp.where(i == nt - 1, 0.0, dccn)
        dcv = (w_ref[2:3, :] * dcc + w_ref[1:2, :] * _shift_up(dcc, dccn, 1, rows8)
               + w_ref[0:1, :] * _shift_up(dcc, dccn, 2, rows8))
        o_ref[:, 0:CW] = (dcv * gc).astype(bf16)
        o_ref[:, CW:2 * CW] = (d * cc).astype(bf16)
        o_ref[:, 2 * CW:3 * CW] = (dcv * v).astype(bf16)

        @pl.when(i == 0)
        def _():
            dw_ref[...] = jnp.zeros_like(dw_ref)
            db_ref[...] = jnp.zeros_like(db_ref)

        dw_ref[0:1, :] += jnp.sum(dcc * s2, axis=0, keepdims=True)
        dw_ref[1:2, :] += jnp.sum(dcc * s1, axis=0, keepdims=True)
        dw_ref[2:3, :] += jnp.sum(dcc * cv, axis=0, keepdims=True)
        db_ref[...] += jnp.sum(dcc, axis=0, keepdims=True)

    blk = lambda col: pl.BlockSpec((tm, CW), lambda i: (i, col))
    prev = lambda col: pl.BlockSpec((16, CW), lambda i: (jnp.maximum(i * (tm // 16) - 1, 0), col))
    nxt = lambda col: pl.BlockSpec((16, CW), lambda i: (jnp.minimum((i + 1) * (tm // 16), L // 16 - 1), col))
    const = lambda r: pl.BlockSpec((r, CW), lambda i: (0, 0))
    return pl.pallas_call(
        body, grid=(nt,),
        in_specs=[blk(1), prev(1), blk(2), nxt(2), blk(3), prev(3), blk(0), nxt(0), const(3), const(1),
                  pl.BlockSpec(memory_space=pl.ANY)],
        out_specs=[pl.BlockSpec((tm, 3 * CW), lambda i: (i, 0)), const(3), const(1)],
        out_shape=[jax.ShapeDtypeStruct(dproj.shape, bf16), jax.ShapeDtypeStruct((3, CW), f32),
                   jax.ShapeDtypeStruct((1, CW), f32)],
        input_output_aliases={10: 0},
        name="convb_bwd", compiler_params=_cparams(("arbitrary",)))(proj, proj, proj, proj, proj, proj, dyb0, dyb0, w, b,
                                                                    dproj)


def _ffn_act(cfg, hh, w, b):
    L, F = cfg.L, cfg.F
    tm = _tile(L, 512, 16)
    tc = _tile(F, 1408, 128)
    ncb = F // tc

    def body(a_ref, ah_ref, g_ref, w_ref, b_ref, o_ref, act_ref):
        i = pl.program_id(0)
        rows8 = lax.broadcasted_iota(jnp.int32, (8, tc), 0)
        a = a_ref[...].astype(f32)
        ah = jnp.where(i == 0, 0.0, ah_ref[...].astype(f32)[8:])
        act = _conv3(a, ah, w_ref, b_ref, rows8)
        act_ref[...] = act.astype(bf16)
        o_ref[...] = (_gelu(act) * g_ref[...].astype(f32)).astype(bf16)

    tile = pl.BlockSpec((tm, tc), lambda i, j: (i, j))
    return pl.pallas_call(
        body, grid=(L // tm, ncb),
        in_specs=[tile, pl.BlockSpec((16, tc), lambda i, j: (jnp.maximum(i * (tm // 16) - 1, 0), j)),
                  pl.BlockSpec((tm, tc), lambda i, j: (i, j + ncb)),
                  pl.BlockSpec((3, tc), lambda i, j: (0, j)), pl.BlockSpec((1, tc), lambda i, j: (0, j))],
        out_specs=[tile, tile], out_shape=[jax.ShapeDtypeStruct((L, F), bf16)] * 2,
        name="ffn_act", compiler_params=_cparams(("parallel", "parallel")))(hh, hh, hh, w, b)


def _ffn_down_dx_act_bwd(cfg, dh2b, w_down, hh, act, w):
    L, D, F = cfg.L, cfg.D, cfg.F
    S, Ks, _ = w_down.shape
    tm = _tile(L, 512, 16)
    tc = _tile(Ks, 1408, 128)
    ncb = F // tc
    nbs = Ks // tc
    nt = L // tm

    slabs = [(c0, min(256, tc - c0)) for c0 in range(0, tc, 256)]

    def body(dh_ref, wd_ref, a_ref, c_ref, g_ref, w_ref, dhh_ref, dw_ref, db_ref, below):
        i = pl.program_id(1)

        @pl.when(i == 0)
        def _():
            below[...] = jnp.zeros_like(below)
            dw_ref[...] = jnp.zeros_like(dw_ref)
            db_ref[...] = jnp.zeros_like(db_ref)

        dh = dh_ref[...]
        for c0, cw in slabs:
            cols = slice(c0, c0 + cw)
            rows8 = lax.broadcasted_iota(jnp.int32, (8, cw), 0)
            d = lax.dot_general(dh, wd_ref[c0:c0 + cw, :], _NT, preferred_element_type=f32)
            a = a_ref[:, cols].astype(f32)
            gelu, gelu_d = _gelu_and_grad(c_ref[:, cols].astype(f32))
            dhh_ref[1, :, cols] = (d * gelu).astype(bf16)
            dact = d * g_ref[:, cols].astype(f32) * gelu_d
            dactn = below[:, cols]
            up1 = _shift_up(dact, dactn, 1, rows8)
            up2 = _shift_up(dact, dactn, 2, rows8)
            below[:, cols] = dact[:8]
            dhh_ref[0, :, cols] = (w_ref[2:3, cols] * dact + w_ref[1:2, cols] * up1 + w_ref[0:1, cols] * up2).astype(bf16)
            dw_ref[0:1, cols] += jnp.sum(up2 * a, axis=0, keepdims=True)
            dw_ref[1:2, cols] += jnp.sum(up1 * a, axis=0, keepdims=True)
            dw_ref[2:3, cols] += jnp.sum(dact * a, axis=0, keepdims=True)
            db_ref[:, cols] += jnp.sum(dact, axis=0, keepdims=True)

    up = lambda i: nt - 1 - i
    blk = lambda off: pl.BlockSpec((tm, tc), lambda j, i: (up(i), j + off))
    const = lambda r: pl.BlockSpec((r, tc), lambda j, i: (0, j))
    return pl.pallas_call(
        body, grid=(ncb, nt),
        in_specs=[pl.BlockSpec((tm, D), lambda j, i: (up(i), 0)),
                  pl.BlockSpec((None, tc, D), lambda j, i: (j // nbs, j % nbs, 0)),
                  blk(0), blk(0), blk(ncb), const(3)],
        out_specs=[pl.BlockSpec((2, tm, tc), lambda j, i: (0, up(i), j)), const(3), const(1)],
        out_shape=[jax.ShapeDtypeStruct((2, L, F), bf16),
                   jax.ShapeDtypeStruct((3, F), f32), jax.ShapeDtypeStruct((1, F), f32)],
        scratch_shapes=[pltpu.VMEM((8, tc), f32)],
        name="ffn_down_dx_act_bwd", compiler_params=_cparams(("parallel", "arbitrary")))(dh2b, w_down, hh, act, hh, w)


def _merge_fwd(cfg, ya1, yb0, proj, wso, wco):
    L, D, SW, CW = cfg.L, cfg.D, cfg.SW, cfg.CW
    Ns = D // _NCHIP
    tm = _tile(L, 1024, 16)
    tn = _tile(Ns, 512, 128)
    nb = Ns // tn
    off_a = (SW + 3 * CW) // tn
    off_b = (SW + 3 * CW + D) // tn

    def body(a_ref, b_ref, wa_ref, wb_ref, ma_ref, mb_ref, m_ref, ya_ref, yb_ref):
        ya = jnp.dot(a_ref[...], wa_ref[...], preferred_element_type=f32)
        yb = jnp.dot(b_ref[...], wb_ref[...], preferred_element_type=f32)
        sa = jax.nn.sigmoid(ma_ref[...].astype(f32))
        sb = jax.nn.sigmoid(mb_ref[...].astype(f32))
        m_ref[...] = (sa * ya + sb * yb).astype(bf16)
        ya_ref[...] = ya.astype(bf16)
        yb_ref[...] = yb.astype(bf16)

    o_spec = pl.BlockSpec((tm, tn), lambda i, j: (i, j))
    o_shape = jax.ShapeDtypeStruct((L, D), bf16)
    return pl.pallas_call(
        body, grid=(L // tm, D // tn),
        in_specs=[pl.BlockSpec((tm, SW), lambda i, j: (i, 0)), pl.BlockSpec((tm, CW), lambda i, j: (i, 0)),
                  pl.BlockSpec((None, SW, tn), lambda i, j: (j // nb, 0, j % nb)),
                  pl.BlockSpec((None, CW, tn), lambda i, j: (j // nb, 0, j % nb)),
                  pl.BlockSpec((tm, tn), lambda i, j: (i, off_a + j)), pl.BlockSpec((tm, tn), lambda i, j: (i, off_b + j))],
        out_specs=[o_spec, o_spec, o_spec], out_shape=[o_shape, o_shape, o_shape],
        name="merge_fwd", compiler_params=_cparams(("parallel", "parallel")))(ya1, yb0, wso, wco, proj, proj)


def _s5_dims(cfg):
    G = cfg.SW // _SSM_GROUP
    NS = G // _SLAB_GROUPS
    SC = _SLAB_GROUPS * _SSM_GROUP
    SH = _SLAB_GROUPS * _SSM_STATE
    NST = 2 * SH * NS
    return G, NS, SC, SH, NST


def _lane_slabs(cfg, W):
    _, NS, _, SH, _ = _s5_dims(cfg)
    return [(2 * SH * s + w0, 2 * SH * s + SH + w0) for s in range(NS) for w0 in range(0, SH, W)]


def _discretize(a_re, a_im, log_dt, b_re, b_im):
    dt = jnp.exp(log_dt)[:, None]
    mag = jnp.exp(dt * a_re)
    abr = mag * jnp.cos(dt * a_im)
    abi = mag * jnp.sin(dt * a_im)
    nr = abr - 1.0
    ni = abi
    den = a_re * a_re + a_im * a_im
    fr = (nr * a_re + ni * a_im) / den
    fi = (ni * a_re - nr * a_im) / den
    bbr = fr[..., None] * b_re - fi[..., None] * b_im
    bbi = fr[..., None] * b_im + fi[..., None] * b_re
    return abr, abi, bbr, bbi


def _state_rows(cfg, re, im):
    _, NS, _, SH, _ = _s5_dims(cfg)
    return jnp.concatenate([re.reshape(NS, SH), im.reshape(NS, SH)], axis=1).reshape(-1)


def _s5_tables(cfg, a_re, a_im, log_dt, abr, abi, bbr, bbi, c_re, c_im):
    G, NS, SC, SH, NST = _s5_dims(cfg)
    S = cfg.T // 8
    eye = jnp.eye(_SLAB_GROUPS, dtype=bf16)
    bb = jnp.stack([bbr, bbi]).reshape(2, NS, _SLAB_GROUPS, _SSM_STATE, _SSM_GROUP).astype(bf16)
    bs = (bb.transpose(1, 2, 4, 0, 3)[:, :, :, :, None, :] * eye[None, :, None, None, :, None]).reshape(NS, SC, 2 * SH)
    cc = jnp.stack([c_re, -c_im]).reshape(2, NS, _SLAB_GROUPS, _SSM_GROUP, _SSM_STATE).astype(bf16)
    cs = (cc.transpose(1, 0, 4, 2, 3)[:, :, None, :, :, :] * eye[None, None, :, None, :, None]).reshape(NS, 2 * SH, SC)
    arep = jnp.broadcast_to(_state_rows(cfg, abr, abi)[None, :], (8, NST))
    sdt = S * jnp.exp(log_dt)[:, None]
    mag = jnp.exp(sdt * a_re)
    apow = jnp.broadcast_to(_state_rows(cfg, mag * jnp.cos(sdt * a_im), mag * jnp.sin(sdt * a_im))[None, :], (8, NST))
    t = jnp.arange(cfg.T)
    perm = (t % 8) * S + t // 8
    pm = jax.nn.one_hot(perm, cfg.T, dtype=bf16)
    return bs, cs, arep, apow, pm, pm.T


def _cmul_add(ar, ai, xr, xi, br, bi):
    return ar * xr - ai * xi + br, ar * xi + ai * xr + bi


def _s5_forward_chunk(cfg, W, upb, bs_ref, arep_ref, apow_ref, st, x0, cin_store):
    _, NS, SC, SH, _ = _s5_dims(cfg)
    S = cfg.T // 8
    for s in range(NS):
        st[:, 2 * SH * s:2 * SH * (s + 1)] = jnp.dot(upb[:, SC * s:SC * (s + 1)], bs_ref[s], preferred_element_type=f32)
    rows = lax.broadcasted_iota(jnp.int32, (8, W), 0)
    zero = jnp.zeros((8, W), f32)
    for rc, ic in _lane_slabs(cfg, W):
        ar = arep_ref[:, rc:rc + W]
        ai = arep_ref[:, ic:ic + W]

        def step(i, carry, rc=rc, ic=ic, ar=ar, ai=ai):
            xr, xi = carry
            r0 = pl.multiple_of(i * 8, 8)
            nr, ni = _cmul_add(ar, ai, xr, xi, st[pl.ds(r0, 8), rc:rc + W], st[pl.ds(r0, 8), ic:ic + W])
            st[pl.ds(r0, 8), rc:rc + W] = nr
            st[pl.ds(r0, 8), ic:ic + W] = ni
            return nr, ni

        er, ei = lax.fori_loop(0, S, step, (zero, zero))
        pr = apow_ref[:, rc:rc + W]
        pi = apow_ref[:, ic:ic + W]
        x0r = x0[:, rc:rc + W]
        x0i = x0[:, ic:ic + W]
        cr = jnp.where(rows == 0, x0r, 0.0)
        ci = jnp.where(rows == 0, x0i, 0.0)
        for _ in range(7):
            fr, fi = _cmul_add(pr, pi, cr, ci, er, ei)
            cr = jnp.where(rows == 0, x0r, pltpu.roll(fr, 1, axis=0))
            ci = jnp.where(rows == 0, x0i, pltpu.roll(fi, 1, axis=0))
        fr, fi = _cmul_add(pr, pi, cr, ci, er, ei)
        x0[:, rc:rc + W] = jnp.broadcast_to(fr[7:8, :], (8, W))
        x0[:, ic:ic + W] = jnp.broadcast_to(fi[7:8, :], (8, W))
        cin_store(rc, ic, cr, ci)

        def fix(i, carry, rc=rc, ic=ic, ar=ar, ai=ai):
            kr, ki = carry
            r0 = pl.multiple_of(i * 8, 8)
            nr, ni = ar * kr - ai * ki, ar * ki + ai * kr
            st[pl.ds(r0, 8), rc:rc + W] = st[pl.ds(r0, 8), rc:rc + W] + nr
            st[pl.ds(r0, 8), ic:ic + W] = st[pl.ds(r0, 8), ic:ic + W] + ni
            return nr, ni

        lax.fori_loop(0, S, fix, (cr, ci))


def _s5_fwd(cfg, proj, tabs, dskip):
    L, SW, T = cfg.L, cfg.SW, cfg.T
    G, NS, SC, SH, NST = _s5_dims(cfg)
    bs, cs, arep, apow, pm, pmt = tabs
    W = min(512, SH)
    NC = L // T

    def body(u_ref, pm_ref, pmt_ref, bs_ref, cs_ref, arep_ref, apow_ref, dskip_ref, y_ref, ya0_ref, cin_ref, st, x0):
        c = pl.program_id(0)

        @pl.when(c == 0)
        def _():
            x0[...] = jnp.zeros_like(x0)

        up = jnp.dot(pm_ref[...], u_ref[...], preferred_element_type=f32)
        upb = up.astype(bf16)

        def cin_store(rc, ic, cr, ci):
            cin_ref[0, :, rc:rc + W] = cr
            cin_ref[0, :, ic:ic + W] = ci

        _s5_forward_chunk(cfg, W, upb, bs_ref, arep_ref, apow_ref, st, x0, cin_store)
        yp = jnp.concatenate(
            [jnp.dot(st[:, 2 * SH * s:2 * SH * (s + 1)].astype(bf16), cs_ref[s], preferred_element_type=f32)
             for s in range(NS)], axis=1) + dskip_ref[...] * up
        y = jnp.dot(pmt_ref[...], yp.astype(bf16), preferred_element_type=f32)
        y_ref[...] = y.astype(bf16)
        ya0_ref[...] = _gelu(y).astype(bf16)

    const2 = lambda shape: pl.BlockSpec(shape, lambda c: (0, 0))
    const3 = lambda shape: pl.BlockSpec(shape, lambda c: (0, 0, 0))
    return pl.pallas_call(
        body, grid=(NC,),
        in_specs=[pl.BlockSpec((T, SW), lambda c: (c, 0)), const2((T, T)), const2((T, T)), const3((NS, SC, 2 * SH)),
                  const3((NS, 2 * SH, SC)), const2((8, NST)), const2((8, NST)), const2((1, SW))],
        out_specs=[pl.BlockSpec((T, SW), lambda c: (c, 0)), pl.BlockSpec((T, SW), lambda c: (c, 0)),
                   pl.BlockSpec((1, 8, NST), lambda c: (c, 0, 0)), pl.BlockSpec((T, NST), lambda c: (c, 0))],
        out_shape=[jax.ShapeDtypeStruct((L, SW), bf16), jax.ShapeDtypeStruct((L, SW), bf16),
                   jax.ShapeDtypeStruct((NC, 8, NST), f32), jax.ShapeDtypeStruct((L, NST), f32)],
        scratch_shapes=[pltpu.VMEM((8, NST), f32)],
        name="s5_fwd", compiler_params=_cparams(("arbitrary",)))(proj, pm, pmt, bs, cs, arep, apow, dskip)


def _s5_bwd(cfg, proj, dy, cin, xs, tabs, dskip, dproj, after=()):
    L, SW, T = cfg.L, cfg.SW, cfg.T
    du_col = 3 * cfg.CW // SW
    G, NS, SC, SH, NST = _s5_dims(cfg)
    bs, cs, arep, apow, pm, pmt = tabs
    W = min(512, SH)
    S = T // 8
    NC = L // T

    def body(u_ref, dy_ref, cin_ref, st, pm_ref, pmt_ref, bs_ref, cs_ref, arep_ref, apow_ref, dskip_ref, dproj_ref, *rest):
        du_ref, da_ref, db_ref, dc_ref, dd_ref, gs, g0, db_acc, dc_acc = rest[len(after):]
        c = pl.program_id(0)

        @pl.when(c == 0)
        def _():
            g0[...] = jnp.zeros_like(g0)
            da_ref[...] = jnp.zeros_like(da_ref)
            dd_ref[...] = jnp.zeros_like(dd_ref)
            db_acc[...] = jnp.zeros_like(db_acc)
            dc_acc[...] = jnp.zeros_like(dc_acc)

        up = jnp.dot(pm_ref[...], u_ref[...], preferred_element_type=f32)
        upb = up.astype(bf16)
        dyp = jnp.dot(pm_ref[...], dy_ref[...], preferred_element_type=f32)
        dypb = dyp.astype(bf16)
        for s in range(NS):
            gs[:, 2 * SH * s:2 * SH * (s + 1)] = lax.dot_general(
                dypb[:, SC * s:SC * (s + 1)], cs_ref[s], (((1,), (1,)), ((), ())), preferred_element_type=f32)
        rows = lax.broadcasted_iota(jnp.int32, (8, W), 0)
        zero = jnp.zeros((8, W), f32)
        for rc, ic in _lane_slabs(cfg, W):
            ar = arep_ref[:, rc:rc + W]
            ai = arep_ref[:, ic:ic + W]

            def rstep(k, carry, rc=rc, ic=ic, ar=ar, ai=ai):
                gr, gi = carry
                r0 = pl.multiple_of((S - 1 - k) * 8, 8)
                nr = ar * gr + ai * gi + gs[pl.ds(r0, 8), rc:rc + W]
                ni = ar * gi - ai * gr + gs[pl.ds(r0, 8), ic:ic + W]
                gs[pl.ds(r0, 8), rc:rc + W] = nr
                gs[pl.ds(r0, 8), ic:ic + W] = ni
                return nr, ni

            er, ei = lax.fori_loop(0, S, rstep, (zero, zero))
            pr = apow_ref[:, rc:rc + W]
            pi = apow_ref[:, ic:ic + W]
            g0r = g0[:, rc:rc + W]
            g0i = g0[:, ic:ic + W]
            cr = jnp.where(rows == 7, g0r, 0.0)
            ci = jnp.where(rows == 7, g0i, 0.0)
            for _ in range(7):
                fr = er + pr * cr + pi * ci
                fi = ei + pr * ci - pi * cr
                cr = jnp.where(rows == 7, g0r, pltpu.roll(fr, 7, axis=0))
                ci = jnp.where(rows == 7, g0i, pltpu.roll(fi, 7, axis=0))
            fr = er + pr * cr + pi * ci
            fi = ei + pr * ci - pi * cr
            g0[:, rc:rc + W] = jnp.broadcast_to(fr[0:1, :], (8, W))
            g0[:, ic:ic + W] = jnp.broadcast_to(fi[0:1, :], (8, W))

            def fix(k, carry, rc=rc, ic=ic, ar=ar, ai=ai):
                kr, ki, accr, acci = carry
                i = S - 1 - k
                r0 = pl.multiple_of(i * 8, 8)
                rp = pl.multiple_of((i - 1) * 8, 8)
                nr = ar * kr + ai * ki
                ni = ar * ki - ai * kr
                gr = gs[pl.ds(r0, 8), rc:rc + W] + nr
                gi = gs[pl.ds(r0, 8), ic:ic + W] + ni
                gs[pl.ds(r0, 8), rc:rc + W] = gr
                gs[pl.ds(r0, 8), ic:ic + W] = gi
                xr = st[pl.ds(rp, 8), rc:rc + W]
                xi = st[pl.ds(rp, 8), ic:ic + W]
                return nr, ni, accr + gr * xr + gi * xi, acci + gi * xr - gr * xi

            kr, ki, accr, acci = lax.fori_loop(0, S - 1, fix, (cr, ci, zero, zero))
            nr = ar * kr + ai * ki
            ni = ar * ki - ai * kr
            gr = gs[0:8, rc:rc + W] + nr
            gi = gs[0:8, ic:ic + W] + ni
            gs[0:8, rc:rc + W] = gr
            gs[0:8, ic:ic + W] = gi
            xr = cin_ref[0, :, rc:rc + W]
            xi = cin_ref[0, :, ic:ic + W]
            da_ref[:, rc:rc + W] += accr + gr * xr + gi * xi
            da_ref[:, ic:ic + W] += acci + gi * xr - gr * xi

        dups = []
        for s in range(NS):
            gsb = gs[:, 2 * SH * s:2 * SH * (s + 1)].astype(bf16)
            dups.append(lax.dot_general(gsb, bs_ref[s], (((1,), (1,)), ((), ())), preferred_element_type=f32))
            db_acc[s] += lax.dot_general(upb[:, SC * s:SC * (s + 1)], gsb, (((0,), (0,)), ((), ())),
                                         preferred_element_type=f32)
            dc_acc[s] += lax.dot_general(st[:, 2 * SH * s:2 * SH * (s + 1)].astype(bf16), dypb[:, SC * s:SC * (s + 1)],
                                         (((0,), (0,)), ((), ())), preferred_element_type=f32)
        dup = jnp.concatenate(dups, axis=1) + dskip_ref[...] * dyp
        du_ref[...] = jnp.dot(pmt_ref[...], dup.astype(bf16), preferred_element_type=f32).astype(bf16)
        dd_ref[...] += jnp.sum(dyp * up, axis=0, keepdims=True)

        @pl.when(c == NC - 1)
        def _():
            PS, GH = _SSM_STATE, _SSM_GROUP
            mask_b = (lax.broadcasted_iota(jnp.int32, (SC, SH), 0) // GH
                      == lax.broadcasted_iota(jnp.int32, (SC, SH), 1) // PS)
            mask_c = (lax.broadcasted_iota(jnp.int32, (SH, SC), 0) // PS
                      == lax.broadcasted_iota(jnp.int32, (SH, SC), 1) // GH)
            for s in range(NS):
                for r in range(2):
                    xb = jnp.where(mask_b, db_acc[s, :, r * SH:(r + 1) * SH], 0.0)
                    zb = xb[:, 0:128]
                    for q in range(1, SH // 128):
                        zb = zb + xb[:, q * 128:(q + 1) * 128]
                    db_ref[s, r] = zb + pltpu.roll(zb, PS, axis=1)
                    xc = jnp.where(mask_c, dc_acc[s, r * SH:(r + 1) * SH, :], 0.0)
                    zc = xc[0:PS]
                    for q in range(1, SH // PS):
                        zc = zc + xc[q * PS:(q + 1) * PS]
                    dc_ref[s, r] = zc

    rev = lambda c: (NC - 1 - c, 0)
    const2 = lambda shape: pl.BlockSpec(shape, lambda c: (0, 0))
    const3 = lambda shape: pl.BlockSpec(shape, lambda c: (0, 0, 0))
    const4 = lambda shape: pl.BlockSpec(shape, lambda c: (0, 0, 0, 0))
    return pl.pallas_call(
        body, grid=(NC,),
        in_specs=[pl.BlockSpec((T, SW), rev), pl.BlockSpec((T, SW), rev), pl.BlockSpec((1, 8, NST), lambda c: (NC - 1 - c, 0, 0)),
                  pl.BlockSpec((T, NST), rev), const2((T, T)), const2((T, T)),
                  pl.BlockSpec((NS, SC, 2 * SH), lambda c: (0, 0, 0), pipeline_mode=pl.Buffered(1)),
                  pl.BlockSpec((NS, 2 * SH, SC), lambda c: (0, 0, 0), pipeline_mode=pl.Buffered(1)),
                  const2((8, NST)), const2((8, NST)), const2((1, SW)), _ANY] + [_ANY] * len(after),
        out_specs=[pl.BlockSpec((T, SW), lambda c: (NC - 1 - c, du_col)), const2((8, NST)),
                   const4((NS, 2, SC, 128)), const4((NS, 2, _SSM_STATE, SC)), const2((1, SW))],
        out_shape=[jax.ShapeDtypeStruct(dproj.shape, bf16), jax.ShapeDtypeStruct((8, NST), f32),
                   jax.ShapeDtypeStruct((NS, 2, SC, 128), f32), jax.ShapeDtypeStruct((NS, 2, _SSM_STATE, SC), f32),
                   jax.ShapeDtypeStruct((1, SW), f32)],
        scratch_shapes=[pltpu.VMEM((T, NST), f32), pltpu.VMEM((8, NST), f32),
                        pltpu.VMEM((NS, SC, 2 * SH), f32), pltpu.VMEM((NS, 2 * SH, SC), f32)],
        input_output_aliases={11: 0},
        name="s5_bwd", compiler_params=_cparams(("arbitrary",), _VMEM_S5_BWD))(proj, dy, cin, xs, pm, pmt, bs, cs, arep, apow, dskip, dproj,
                                                                  *after)


def _s5_param_grads(cfg, da, db_diag, dc_diag):
    G, NS, SC, SH, NST = _s5_dims(cfg)
    das = da.sum(axis=0).reshape(NS, 2, SH)
    dabr = das[:, 0].reshape(G, _SSM_STATE)
    dabi = das[:, 1].reshape(G, _SSM_STATE)
    dbd = db_diag[..., :_SSM_STATE].reshape(NS, 2, _SLAB_GROUPS, _SSM_GROUP, _SSM_STATE)
    dbb = dbd.transpose(1, 0, 2, 4, 3).reshape(2, G, _SSM_STATE, _SSM_GROUP)
    dcd = dc_diag.reshape(NS, 2, _SSM_STATE, _SLAB_GROUPS, _SSM_GROUP)
    dcc = dcd.transpose(1, 0, 3, 4, 2).reshape(2, G, _SSM_GROUP, _SSM_STATE)
    return dabr, dabi, dbb[0], dbb[1], dcc[0], -dcc[1]


def _coords():
    return lax.axis_index("x"), lax.axis_index("y"), lax.axis_index("c")


def _other_chips(x, y):
    return [(1 - x, y), (x, 1 - y), (1 - x, 1 - y)]


def _allreduce8(name, v):
    R = v.shape[0]

    def body(v_ref, o_ref, sib, chips, mine, ssem, rsem):
        x, y, c = _coords()
        d2d = pltpu.make_async_remote_copy(src_ref=v_ref, dst_ref=sib, send_sem=ssem.at[0], recv_sem=rsem.at[0],
                                           device_id=(x, y, 1 - c), device_id_type=_MESH)
        d2d.start()
        d2d.wait()
        mine[...] = v_ref[...] + sib[...]
        cps = [pltpu.make_async_remote_copy(src_ref=mine, dst_ref=chips.at[j], send_sem=ssem.at[1 + j],
                                            recv_sem=rsem.at[1 + j], device_id=(*chip, c), device_id_type=_MESH)
               for j, chip in enumerate(_other_chips(x, y))]
        for cp in cps:
            cp.start()
        for cp in cps:
            cp.wait()
        o_ref[...] = (mine[...] + chips[1]) + (chips[0] + chips[2])

    vm = pl.BlockSpec(memory_space=pltpu.VMEM)
    return pl.pallas_call(
        body, in_specs=[vm], out_specs=vm, out_shape=jax.ShapeDtypeStruct((R, 128), f32),
        scratch_shapes=[pltpu.VMEM((R, 128), f32), pltpu.VMEM((3, R, 128), f32), pltpu.VMEM((R, 128), f32),
                        pltpu.SemaphoreType.DMA((4,)), pltpu.SemaphoreType.DMA((4,))],
        name=name, compiler_params=pltpu.CompilerParams(vmem_limit_bytes=_VMEM_LIMIT))(v)


def _cast_into_slot(name, w, k_idx, also_alone=False):
    R, C = w.shape
    tr = _tile(R, 256, 16)

    def body(k_ref, w_ref, *o_refs):
        for o_ref in o_refs:
            o_ref[...] = w_ref[...].astype(bf16)

    slot = (jax.ShapeDtypeStruct((_NCHIP, R, C), bf16), pl.BlockSpec((None, tr, C), lambda r, kr: (kr[0], r, 0)))
    alone = (jax.ShapeDtypeStruct((R, C), bf16), pl.BlockSpec((tr, C), lambda r, kr: (r, 0)))
    outs = [slot, alone] if also_alone else [slot]
    gs = pltpu.PrefetchScalarGridSpec(
        num_scalar_prefetch=1, grid=(R // tr,),
        in_specs=[pl.BlockSpec((tr, C), lambda r, kr: (r, 0))], out_specs=[o[1] for o in outs])
    res = pl.pallas_call(body, grid_spec=gs, out_shape=[o[0] for o in outs], name=name,
                         compiler_params=_cparams(("parallel",)))(k_idx, w)
    return res if also_alone else res[0]


def _in_proj(cfg, xn1, w_own, w_in, k_idx):
    L, D = xn1.shape
    S, _, Ns = w_in.shape
    tm, tn = _tile(L, 1024, 16), _tile(Ns, 1024, 128)
    nb = Ns // tn

    def body(k_ref, a_ref, w_ref, *rest):
        rest[-1][...] = jnp.dot(a_ref[...], w_ref[...], preferred_element_type=f32).astype(bf16)

    a_spec = pl.BlockSpec((tm, D), lambda i, j, kr: (i, 0))
    shape = jax.ShapeDtypeStruct((L, S * Ns), bf16)
    own = pltpu.PrefetchScalarGridSpec(
        num_scalar_prefetch=1, grid=(L // tm, nb),
        in_specs=[a_spec, pl.BlockSpec((D, tn), lambda i, j, kr: (0, j))],
        out_specs=pl.BlockSpec((tm, tn), lambda i, j, kr: (i, kr[0] * nb + j)))
    proj = pl.pallas_call(body, grid_spec=own, out_shape=shape, name="in_proj_own",
                          compiler_params=_cparams(("parallel", "parallel")))(k_idx, xn1, w_own)
    shard = lambda j, kr: (kr[0] + 1 + j // nb) % S
    rest = pltpu.PrefetchScalarGridSpec(
        num_scalar_prefetch=1, grid=(L // tm, (S - 1) * nb),
        in_specs=[a_spec, pl.BlockSpec((None, D, tn), lambda i, j, kr: (shard(j, kr), 0, j % nb)), _ANY],
        out_specs=pl.BlockSpec((tm, tn), lambda i, j, kr: (i, shard(j, kr) * nb + j % nb)))
    return pl.pallas_call(body, grid_spec=rest, out_shape=shape, input_output_aliases={3: 0}, name="in_proj",
                          compiler_params=_cparams(("parallel", "parallel")))(k_idx, xn1, w_in, proj)


def _handshake(peers):
    barrier = pltpu.get_barrier_semaphore()
    for peer in peers:
        pl.semaphore_signal(barrier, inc=1, device_id=peer, device_id_type=_MESH)
    pl.semaphore_wait(barrier, len(peers))


def _allgather_weights(name, bufs, collective_id):
    n = len(bufs)
    refs = [jax.new_ref(b, memory_space=pltpu.MemorySpace.HBM) for b in bufs]

    def copy(ref, sems, idx, to):
        return pltpu.make_async_remote_copy(src_ref=ref, dst_ref=ref, send_sem=sems[0].at[idx], recv_sem=sems[1].at[idx],
                                            device_id=to, device_id_type=_MESH)

    def launch(ssem, rsem, qssem, qrsem, fssem, frsem):
        x, y, c = _coords()
        k = 2 * x + y
        nbrs = [(1 - x, y), (x, 1 - y)]
        across = 2 * (1 - x) + (1 - y)
        sibling = (x, y, 1 - c)
        _handshake([sibling] + [(*chip, c) for chip in nbrs])
        started = []

        def start(cp):
            cp.start()
            started.append(cp)

        for w in range(n):
            rh = refs[w].shape[1] // 2
            for j, chip in enumerate(nbrs):
                start(copy(refs[w].at[k, pl.ds(c * rh, rh)], (ssem, rsem), (w, j), (*chip, c)))
        for w in range(n):
            rh = refs[w].shape[1] // 2
            rq = rh // 2
            for j, (ox, oy) in enumerate(nbrs):
                ko = 2 * ox + oy
                landed = refs[w].at[ko, pl.ds(c * rh, rh)]
                copy(landed, (ssem, rsem), (w, j), (ox, oy, c)).wait_recv()
                start(copy(refs[w].at[ko, pl.ds(c * rh + j * rq, rq)], (qssem, qrsem), (w, j), (*nbrs[1 - j], c)))
                start(copy(landed, (fssem, frsem), (w, j), sibling))
        for w in range(n):
            rh = refs[w].shape[1] // 2
            rq = rh // 2
            for q in range(2):
                quarter = refs[w].at[across, pl.ds(c * rh + q * rq, rq)]
                copy(quarter, (qssem, qrsem), (w, q), (*nbrs[1 - q], c)).wait_recv()
            start(copy(refs[w].at[across, pl.ds(c * rh, rh)], (fssem, frsem), (w, 2), sibling))
        for w in range(n):
            rh = refs[w].shape[1] // 2
            for j, ko in enumerate([2 * nbrs[0][0] + nbrs[0][1], 2 * nbrs[1][0] + nbrs[1][1], across]):
                copy(refs[w].at[ko, pl.ds((1 - c) * rh, rh)], (fssem, frsem), (w, j), sibling).wait_recv()
        for cp in started:
            cp.wait_send()

    _sequencer_kernel(name, collective_id,
                      (pltpu.SemaphoreType.DMA((n, 2)), pltpu.SemaphoreType.DMA((n, 2)), pltpu.SemaphoreType.DMA((n, 2)),
                       pltpu.SemaphoreType.DMA((n, 2)), pltpu.SemaphoreType.DMA((n, 3)), pltpu.SemaphoreType.DMA((n, 3))),
                      launch)
    return [r[...] for r in refs]


def _sequencer_kernel(name, collective_id, sems, body):
    pl.kernel(body, mesh=plsc.ScalarSubcoreMesh(axis_name="seq", num_cores=1), name=name, scratch_types=sems,
              compiler_params=pltpu.CompilerParams(collective_id=collective_id))()


def _hbm_ref(a):
    return jax.new_ref(a, memory_space=pltpu.MemorySpace.HBM)


def _exchange_halves(name, grads, collective_id):
    n = len(grads)
    srcs = [_hbm_ref(g) for g in grads]
    dsts = [jax.empty_ref(jax.ShapeDtypeStruct((g.shape[0], g.shape[1] // 2, g.shape[2]), g.dtype),
                          memory_space=pltpu.MemorySpace.HBM) for g in grads]

    def body(ssem, rsem):
        x, y, c = _coords()
        _handshake([(x, y, 1 - c)])
        cps = []
        for w in range(n):
            rh = srcs[w].shape[1] // 2
            cp = pltpu.make_async_remote_copy(
                src_ref=srcs[w].at[:, pl.ds((1 - c) * rh, rh)], dst_ref=dsts[w], send_sem=ssem.at[w], recv_sem=rsem.at[w],
                device_id=(x, y, 1 - c), device_id_type=_MESH)
            cp.start()
            cps.append(cp)
        for cp in cps:
            cp.wait()

    _sequencer_kernel(name, collective_id, (pltpu.SemaphoreType.DMA((n,)), pltpu.SemaphoreType.DMA((n,))), body)
    return [d[...] for d in dsts]


def _scatter_shards(name, parts, collective_id):
    n = len(parts)
    srcs = [_hbm_ref(p) for p in parts]
    dsts = [jax.empty_ref(jax.ShapeDtypeStruct((3,) + p.shape[1:], p.dtype), memory_space=pltpu.MemorySpace.HBM)
            for p in parts]

    def body(ssem, rsem):
        x, y, c = _coords()
        k = 2 * x + y
        others = _other_chips(x, y)
        _handshake([(*chip, c) for chip in others])
        cps = []
        for w in range(n):
            for j, (ox, oy) in enumerate(others):
                cp = pltpu.make_async_remote_copy(
                    src_ref=srcs[w].at[(2 * ox + oy - k + 3) % 4], dst_ref=dsts[w].at[j], send_sem=ssem.at[w, j],
                    recv_sem=rsem.at[w, j],
                    device_id=(ox, oy, c), device_id_type=_MESH)
                cp.start()
                cps.append(cp)
        for cp in cps:
            cp.wait()

    _sequencer_kernel(name, collective_id, (pltpu.SemaphoreType.DMA((n, 3)), pltpu.SemaphoreType.DMA((n, 3))), body)
    return [d[...] for d in dsts]


def _join_halves(name, bufs, collective_id):
    n = len(bufs)
    refs = [_hbm_ref(b) for b in bufs]

    def body(ssem, rsem):
        x, y, c = _coords()
        _handshake([(x, y, 1 - c)])
        cps = []
        for w in range(n):
            rh = refs[w].shape[0] // 2
            mine = refs[w].at[pl.ds(c * rh, rh)]
            cp = pltpu.make_async_remote_copy(src_ref=mine, dst_ref=mine, send_sem=ssem.at[w], recv_sem=rsem.at[w],
                                              device_id=(x, y, 1 - c), device_id_type=_MESH)
            cp.start()
            cps.append(cp)
        for w in range(n):
            rh = refs[w].shape[0] // 2
            theirs = refs[w].at[pl.ds((1 - c) * rh, rh)]
            pltpu.make_async_remote_copy(src_ref=theirs, dst_ref=theirs, send_sem=ssem.at[w], recv_sem=rsem.at[w],
                                         device_id=(x, y, 1 - c), device_id_type=_MESH).wait_recv()
        for cp in cps:
            cp.wait_send()

    _sequencer_kernel(name, collective_id, (pltpu.SemaphoreType.DMA((n,)), pltpu.SemaphoreType.DMA((n,))), body)
    return [r[...] for r in refs]


def _allgather_small(name, v, collective_id):
    R = v.shape[0]
    src = _hbm_ref(v)
    dst = jax.empty_ref(jax.ShapeDtypeStruct((8, R, 128), v.dtype), memory_space=pltpu.MemorySpace.HBM)
    flips = [(dx, dy, dc) for dx in (0, 1) for dy in (0, 1) for dc in (0, 1)][1:]

    def body(lsem, ssem, rsem):
        x, y, c = _coords()
        me = 4 * x + 2 * y + c
        flip = lambda v, d: 1 - v if d else v
        peers = [(flip(x, dx), flip(y, dy), flip(c, dc)) for dx, dy, dc in flips]
        _handshake(peers)
        own = pltpu.make_async_copy(src, dst.at[me], lsem)
        own.start()
        cps = []
        for r, peer in enumerate(peers):
            cp = pltpu.make_async_remote_copy(src_ref=src, dst_ref=dst.at[me], send_sem=ssem.at[r], recv_sem=rsem.at[r],
                                              device_id=peer, device_id_type=_MESH)
            cp.start()
            cps.append(cp)
        for r, (px, py, pc) in enumerate(peers):
            theirs = dst.at[4 * px + 2 * py + pc]
            pltpu.make_async_remote_copy(src_ref=theirs, dst_ref=theirs, send_sem=ssem.at[r], recv_sem=rsem.at[r],
                                         device_id=(px, py, pc), device_id_type=_MESH).wait_recv()
        for cp in cps:
            cp.wait_send()
        own.wait()

    _sequencer_kernel(name, collective_id, (pltpu.SemaphoreType.DMA, pltpu.SemaphoreType.DMA((7,)),
                                            pltpu.SemaphoreType.DMA((7,))), body)
    return dst[...]


def _sum8(name, g8, after):
    R = g8.shape[1]
    tr = _tile(R, 512, 8)

    def body(g_ref, *rest):
        rest[-1][...] = (((g_ref[0] + g_ref[1]) + (g_ref[2] + g_ref[3]))
                         + ((g_ref[4] + g_ref[5]) + (g_ref[6] + g_ref[7])))

    return pl.pallas_call(body, grid=(R // tr,),
                          in_specs=[pl.BlockSpec((8, tr, 128), lambda i: (0, i, 0))] + [_ANY] * len(after),
                          out_specs=pl.BlockSpec((tr, 128), lambda i: (i, 0)), out_shape=jax.ShapeDtypeStruct((R, 128), f32),
                          name=name, compiler_params=_cparams(("parallel",)))(g8, *after)


def _add_own_half(name, g, t, kc_idx, after):
    S, R, C = g.shape
    rh = R // 2
    tr = _tile(rh, 512, 16)
    nrb = rh // tr
    shard = lambda s, kc: (kc[0] + 1 + s) % S

    def body(kc_ref, g_ref, t_ref, *rest):
        rest[-1][...] = (g_ref[...] + t_ref[...]).astype(bf16)

    gs = pltpu.PrefetchScalarGridSpec(
        num_scalar_prefetch=1, grid=(S - 1, nrb),
        in_specs=[pl.BlockSpec((None, tr, C), lambda s, r, kc: (shard(s, kc), kc[1] * nrb + r, 0)),
                  pl.BlockSpec((None, tr, C), lambda s, r, kc: (shard(s, kc), r, 0))] + [_ANY] * len(after),
        out_specs=pl.BlockSpec((None, tr, C), lambda s, r, kc: (s, r, 0)))
    return pl.pallas_call(body, grid_spec=gs, out_shape=jax.ShapeDtypeStruct((S - 1, rh, C), bf16), name=name,
                          compiler_params=_cparams(("parallel", "parallel")))(kc_idx, g, t, *after)


def _add_shard_parts(name, g, t, r, kc_idx, after):
    S, R, C = g.shape
    rh = R // 2
    tr = _tile(rh, 256, 16)
    nrb = rh // tr

    def body(kc_ref, g_ref, t_ref, r_ref, *rest):
        own = g_ref[...] + t_ref[...]
        rest[-1][...] = (own + r_ref[1].astype(f32)) + (r_ref[0].astype(f32) + r_ref[2].astype(f32))

    gs = pltpu.PrefetchScalarGridSpec(
        num_scalar_prefetch=1, grid=(nrb,),
        in_specs=[pl.BlockSpec((None, tr, C), lambda i, kc: (kc[0], kc[1] * nrb + i, 0)),
                  pl.BlockSpec((None, tr, C), lambda i, kc: (kc[0], i, 0)),
                  pl.BlockSpec((3, tr, C), lambda i, kc: (0, i, 0))] + [_ANY] * len(after),
        out_specs=pl.BlockSpec((tr, C), lambda i, kc: (kc[1] * nrb + i, 0)))
    return pl.pallas_call(body, grid_spec=gs, out_shape=jax.ShapeDtypeStruct((R, C), f32), name=name,
                          compiler_params=_cparams(("parallel",)))(kc_idx, g, t, r, *after)


def _adamw_update(wv, gv, mv, vv):
    nm = _ADAM_B1 * mv + (1.0 - _ADAM_B1) * gv
    nv = _ADAM_B2 * vv + (1.0 - _ADAM_B2) * (gv * gv)
    m_hat = nm / (1.0 - _ADAM_B1 ** _ADAM_STEP)
    v_hat = nv / (1.0 - _ADAM_B2 ** _ADAM_STEP)
    return -_ADAM_LR * (m_hat / (jnp.sqrt(v_hat) + _ADAM_EPS) + _ADAM_WD * wv), nm, nv


def _adamw(name, w, g, m, v, after=()):
    R, C = w.shape
    tr = _tile(R, 256, 8)

    def body(w_ref, g_ref, m_ref, v_ref, *rest):
        go_ref, d_ref, nm_ref, nv_ref = rest[len(after):]
        gv = g_ref[...]
        go_ref[...] = gv
        d_ref[...], nm_ref[...], nv_ref[...] = _adamw_update(w_ref[...], gv, m_ref[...], v_ref[...])

    spec = pl.BlockSpec((tr, C), lambda i: (i, 0))
    shape = jax.ShapeDtypeStruct((R, C), f32)
    return pl.pallas_call(body, grid=(R // tr,), in_specs=[spec] * 4 + [_ANY] * len(after), out_specs=[spec] * 4,
                          out_shape=[shape] * 4, name=name, compiler_params=_cparams(("parallel",)))(w, g, m, v, *after)


def _adamw_whole(name, w, g, m, v):
    def body(w_ref, g_ref, m_ref, v_ref, d_ref, nm_ref, nv_ref):
        d_ref[...], nm_ref[...], nv_ref[...] = _adamw_update(w_ref[...], g_ref[...], m_ref[...], v_ref[...])

    vm = pl.BlockSpec(memory_space=pltpu.VMEM)
    return pl.pallas_call(body, in_specs=[vm] * 4, out_specs=[vm] * 3, out_shape=[jax.ShapeDtypeStruct(w.shape, f32)] * 3,
                          name=name, compiler_params=pltpu.CompilerParams(vmem_limit_bytes=_VMEM_LIMIT))(w, g, m, v)


def _pack(arrs):
    flat = jnp.concatenate([a.reshape(-1).astype(f32) for a in arrs])
    n = flat.shape[0]
    pad = (-n) % (128 * 128)
    return jnp.pad(flat, (0, pad)).reshape(-1, 128)


def _unpack(packed, shapes):
    flat = packed.reshape(-1)
    out, off = [], 0
    for s in shapes:
        n = math.prod(s)
        out.append(flat[off:off + n].reshape(s))
        off += n
    return out


_BIG = ("w_in", "w_glu", "w_ssm_out", "w_conv_out", "w_o", "w_up", "w_down")
_SMALL = ("norm_tok", "a_re", "a_im", "log_dt", "b_re", "b_im", "c_re", "c_im", "d_skip", "conv_w", "conv_b",
          "norm_ffn", "ffn_conv_w", "ffn_conv_b", "norm_final")
_WEIGHTS = ("norm_tok", "w_in", "a_re", "a_im", "log_dt", "b_re", "b_im", "c_re", "c_im", "d_skip", "w_glu",
            "w_ssm_out", "conv_w", "conv_b", "w_conv_out", "w_o", "norm_ffn", "w_up", "ffn_conv_w", "ffn_conv_b",
            "w_down", "norm_final")


def _step(cfg, x, tgt, p, m, v):
    L, D, SW, CW, F = cfg.L, cfg.D, cfg.SW, cfg.CW, cfg.F
    xi, yi, ci = _coords()
    k_idx = (2 * xi + yi).astype(jnp.int32).reshape(1)
    c_idx = ci.astype(jnp.int32).reshape(1)
    x = x.reshape(L, D)
    tgt = tgt.reshape(L, D)

    big2d = {n: p[n].reshape(p[n].shape[-2], p[n].shape[-1]) for n in _BIG}
    slots = {n: _cast_into_slot("cast_" + n, big2d[n], k_idx) for n in _BIG if n != "w_in"}
    slots["w_in"], w_in_own = _cast_into_slot("cast_w_in", big2d["w_in"], k_idx, also_alone=True)
    wg = {}
    for cid, (gname, group) in enumerate((("allgather_w_in", ("w_in",)),
                                          ("allgather_w_mixer", ("w_glu", "w_ssm_out", "w_conv_out", "w_o")),
                                          ("allgather_w_up", ("w_up",)), ("allgather_w_down", ("w_down",)))):
        wg.update(zip(group, _allgather_weights(gname, [slots[n] for n in group], cid)))
    w_in, w_so, w_co, w_up = wg["w_in"], wg["w_ssm_out"], wg["w_conv_out"], wg["w_up"]
    w_glu, w_o, w_down = wg["w_glu"], wg["w_o"], wg["w_down"]
    kk = k_idx[0]
    cw_full = lax.dynamic_update_slice(jnp.zeros((3, CW), f32), p["conv_w"].reshape(3, CW // _NCHIP), (0, kk * (CW // _NCHIP)))
    fw_full = lax.dynamic_update_slice(jnp.zeros((3, F), f32), p["ffn_conv_w"].reshape(3, F // _NCHIP), (0, kk * (F // _NCHIP)))
    south = (ci == 0).astype(f32)
    filters8 = _allgather_small("allgather_conv_filters", _pack([cw_full * south, fw_full * south]), 17)
    conv_b = p["conv_b"].reshape(1, CW)
    ffn_conv_b = p["ffn_conv_b"].reshape(1, F)
    norm_tok = p["norm_tok"].reshape(1, D)
    norm_ffn = p["norm_ffn"].reshape(1, D)
    norm_final = p["norm_final"].reshape(1, D)
    dskip = p["d_skip"].reshape(1, SW)

    s5_in = (p["a_re"][0], p["a_im"][0], p["log_dt"][0], p["b_re"][0], p["b_im"][0])
    (abr, abi, bbr, bbi), disc_vjp = jax.vjp(_discretize, *s5_in)
    tabs = _s5_tables(cfg, *s5_in[:3], abr, abi, bbr, bbi, p["c_re"][0], p["c_im"][0])

    xn1, r1 = _rms_fwd("rms_tok", x, norm_tok)
    proj = _in_proj(cfg, xn1, w_in_own, w_in, k_idx)
    conv_w, ffn_conv_w = _unpack(_sum8("sum_conv_filters", filters8, [proj]), [(3, CW), (3, F)])
    y_s, ya0, cin, xs = _s5_fwd(cfg, proj, tabs, dskip)

    def tiles(*arrs):
        return lambda tm, tn: [(a, pl.BlockSpec((tm, tn), lambda i, j: (i, j))) for a in arrs]

    def glu_epi(acc, e, o):
        o[0][...] = (e[0][...].astype(f32) * jax.nn.sigmoid(acc)).astype(bf16)
        o[1][...] = acc.astype(bf16)

    ya1, z = _mm_nn("glu", ya0, w_glu, [bf16, bf16], rows=True, extras_fn=tiles(ya0), epilogue=glu_epi)
    yb0 = _convb_fwd(cfg, proj, conv_w, conv_b)
    merged, ya, yb = _merge_fwd(cfg, ya1, yb0, proj, w_so, w_co)

    def res_epi(acc, e, o):
        o[0][...] = e[0][...] + acc

    h1 = _mm_nn("out_proj", merged, w_o, [f32], rows=True, extras_fn=tiles(x), epilogue=res_epi)[0]
    xn2, r2 = _rms_fwd("rms_ffn", h1, norm_ffn)
    hh = _mm_nn("ffn_up", xn2, w_up, [bf16], tn=2816)[0]
    fact, ffn_pre = _ffn_act(cfg, hh, ffn_conv_w, ffn_conv_b)
    h2 = _mm_nn("ffn_down", fact, w_down, [f32], tm=512, rows=True, extras_fn=tiles(h1), epilogue=res_epi)[0]
    dh2, dh2b, g_norm_final, loss_tile = _loss_head("loss_head", h2, tgt, norm_final)

    kc_idx = jnp.concatenate([k_idx, c_idx])
    reduced, chains = {}, {}

    def rs_halves(tag, collective_id, names, gs):
        chains[tag] = dict(cid=collective_id, names=names, gs=gs,
                           sib=_exchange_halves("grad_halves_" + tag, gs, collective_id))

    def rs_shards(tag, after):
        ch = chains[tag]
        ch["parts"] = [_add_own_half("grad_add_halves_" + n, g, t, kc_idx, after)
                       for n, g, t in zip(ch["names"], ch["gs"], ch["sib"])]
        ch["chips"] = _scatter_shards("grad_shards_" + tag, ch["parts"], ch["cid"] + 1)
        return ch["parts"]

    def rs_join(tag, after):
        ch = chains[tag]
        ch["halves"] = [_add_shard_parts("grad_add_chips_" + n, g, t, r, kc_idx, after)
                        for n, g, t, r in zip(ch["names"], ch["gs"], ch["sib"], ch["chips"])]
        reduced.update(zip(ch["names"], _join_halves("grad_join_" + tag, ch["halves"], ch["cid"] + 2)))
        return ch["halves"]

    g_w_down = _mm_tn("ffn_down_dw", fact, dh2b, tm=1408, tn=512)
    rs_halves("ffn_down", 4, ["w_down"], [g_w_down.reshape(_NCHIP, F // _NCHIP, D)])
    dhh, g_ffn_conv_w, g_ffn_conv_b = _ffn_down_dx_act_bwd(cfg, dh2b, w_down, hh, ffn_pre, ffn_conv_w)
    sent = rs_shards("ffn_down", [dhh])
    g_w_up = _mm_tn("ffn_up_dw", xn2, dhh, shards=_NCHIP, tm=512, tn=1408, b_resident=True, after=sent)
    rs_halves("ffn_up", 7, ["w_up"], [g_w_up])
    dxn2 = _ffn_up_dx(cfg, dhh, w_up)
    sent = rs_shards("ffn_up", [dxn2]) + rs_join("ffn_down", [dxn2])
    dh1, dh1b, g_norm_ffn = _rms_bwd("rms_ffn_bwd", dxn2, h1, r2, norm_ffn, dh2, after=sent)

    dya, dyb, dproj = _out_proj_dx(cfg, dh1b, w_o, ya, yb, proj)
    g_w_o = _mm_tn("out_proj_dw", merged, dh1b)

    def glu_bwd_epi(acc, e, o):
        a0 = e[0][...].astype(f32)
        s = jax.nn.sigmoid(e[1][...].astype(f32))
        o[0][...] = (acc * a0 * s * (1.0 - s)).astype(bf16)
        o[1][...] = (acc * s).astype(bf16)

    dz, t1 = _mm_nt("ssm_out_dx", dya, w_so, [bf16, bf16], extras_fn=tiles(ya0, z), epilogue=glu_bwd_epi)
    g_w_so = _mm_tn("ssm_out_dw", ya1, dya, shards=_NCHIP, tn=512)
    dyb0 = _mm_nt("conv_out_dx", dyb, w_co, [bf16])[0]
    g_w_co = _mm_tn("conv_out_dw", yb0, dyb, shards=_NCHIP, tn=512)
    dproj, g_conv_w, g_conv_b = _convb_bwd(cfg, proj, dyb0, conv_w, conv_b, dproj)

    def gelu_bwd_epi(acc, e, o):
        o[0][...] = ((e[0][...].astype(f32) + acc) * _gelu_grad(e[1][...].astype(f32))).astype(bf16)

    dy_s = _mm_nt("glu_dx", dz, w_glu, [bf16], rows=True, extras_fn=tiles(t1, y_s), epilogue=gelu_bwd_epi)[0]
    g_w_glu = _mm_tn("glu_dw", ya0, dz)
    rs_halves("mixer", 10, ["w_o", "w_ssm_out", "w_conv_out", "w_glu"],
              [g_w_o.reshape(_NCHIP, D // _NCHIP, D), g_w_so, g_w_co, g_w_glu.reshape(_NCHIP, SW // _NCHIP, SW)])
    sent = rs_join("ffn_up", [g_w_glu])
    dproj, da_acc, db_full, dc_full, g_dskip = _s5_bwd(cfg, proj, dy_s, cin, xs, tabs, dskip, dproj, after=sent)
    sent = rs_shards("mixer", [dproj])

    dabr, dabi, dbbr, dbbi, g_c_re, g_c_im = _s5_param_grads(cfg, da_acc, db_full, dc_full)
    g_a_re, g_a_im, g_log_dt, g_b_re, g_b_im = disc_vjp((dabr, dabi, dbbr, dbbi))
    small_g = {"a_re": g_a_re, "a_im": g_a_im, "log_dt": g_log_dt, "b_re": g_b_re, "b_im": g_b_im,
               "c_re": g_c_re, "c_im": g_c_im, "d_skip": g_dskip, "conv_w": g_conv_w, "conv_b": g_conv_b,
               "norm_ffn": g_norm_ffn, "ffn_conv_w": g_ffn_conv_w, "ffn_conv_b": g_ffn_conv_b, "norm_final": g_norm_final}
    early = [n for n in _SMALL if n != "norm_tok"]
    small8 = _allgather_small("allgather_small_grads", _pack([small_g[n] for n in early]), 16)

    g_w_in = _mm_tn("in_proj_dw", xn1, dproj, shards=_NCHIP, tn=CW, after=sent,
                    b_block=lambda j: jnp.where(j == 0, 3 * CW // SW, jnp.where(j < 4, j - 1, j)))
    rs_halves("in_proj", 13, ["w_in"], [g_w_in])
    dxn1 = _in_proj_dx(cfg, dproj, w_in)
    sent = rs_shards("in_proj", [dxn1]) + rs_join("mixer", [dxn1])
    dx, _, g_norm_tok = _rms_bwd("rms_tok_bwd", dxn1, x, r1, norm_tok, dh1, after=sent)

    summed = dict(zip(early, _unpack(_sum8("sum_small_grads", small8, [dx]), [small_g[n].shape for n in early])))
    summed["norm_tok"] = _unpack(_allreduce8("allreduce_norm_tok", _pack([g_norm_tok])), [g_norm_tok.shape])[0]
    summed["conv_w"] = lax.dynamic_slice(summed["conv_w"], (0, kk * (CW // _NCHIP)), (3, CW // _NCHIP))
    summed["ffn_conv_w"] = lax.dynamic_slice(summed["ffn_conv_w"], (0, kk * (F // _NCHIP)), (3, F // _NCHIP))

    grads, deltas, new_m, new_v = {}, {}, {}, {}

    def adamw_big(names, after):
        for n in names:
            g_, d_, m_, v_ = _adamw("adamw_" + n, big2d[n], reduced[n], m[n].reshape(big2d[n].shape),
                                    v[n].reshape(big2d[n].shape), after=after)
            grads[n], deltas[n], new_m[n], new_v[n] = (a.reshape(p[n].shape) for a in (g_, d_, m_, v_))
            after = [d_]
        return after

    for n in _SMALL:
        grads[n] = summed[n].reshape(p[n].shape)
        deltas[n], new_m[n], new_v[n] = _adamw_whole("adamw_" + n, p[n], grads[n], m[n], v[n])
    done = adamw_big(["w_down", "w_up", "w_o", "w_ssm_out", "w_conv_out", "w_glu"], [deltas["norm_final"]])
    rs_join("in_proj", done + [deltas[n] for n in _SMALL])
    adamw_big(["w_in"], ())

    loss = lax.psum(loss_tile[0, 0], ("x", "y", "c"))
    return (loss, dx.reshape(1, L, D), *[grads[n] for n in _WEIGHTS], *[deltas[n] for n in _WEIGHTS],
            *[new_m[n] for n in _WEIGHTS], *[new_v[n] for n in _WEIGHTS])


def kernel(x, norm_tok, w_in, a_re, a_im, log_dt, b_re, b_im, c_re, c_im, d_skip, w_glu, w_ssm_out, conv_w, conv_b, w_conv_out, w_o, norm_ffn, w_up, ffn_conv_w, ffn_conv_b, w_down, norm_final, loss_target, m_norm_tok, m_w_in, m_a_re, m_a_im, m_log_dt, m_b_re, m_b_im, m_c_re, m_c_im, m_d_skip, m_w_glu, m_w_ssm_out, m_conv_w, m_conv_b, m_w_conv_out, m_w_o, m_norm_ffn, m_w_up, m_ffn_conv_w, m_ffn_conv_b, m_w_down, m_norm_final, v_norm_tok, v_w_in, v_a_re, v_a_im, v_log_dt, v_b_re, v_b_im, v_c_re, v_c_im, v_d_skip, v_w_glu, v_w_ssm_out, v_conv_w, v_conv_b, v_w_conv_out, v_w_o, v_norm_ffn, v_w_up, v_ffn_conv_w, v_ffn_conv_b, v_w_down, v_norm_final):
    p = dict(norm_tok=norm_tok, w_in=w_in, a_re=a_re, a_im=a_im, log_dt=log_dt, b_re=b_re, b_im=b_im, c_re=c_re,
             c_im=c_im, d_skip=d_skip, w_glu=w_glu, w_ssm_out=w_ssm_out, conv_w=conv_w, conv_b=conv_b,
             w_conv_out=w_conv_out, w_o=w_o, norm_ffn=norm_ffn, w_up=w_up, ffn_conv_w=ffn_conv_w,
             ffn_conv_b=ffn_conv_b, w_down=w_down, norm_final=norm_final)
    m = dict(norm_tok=m_norm_tok, w_in=m_w_in, a_re=m_a_re, a_im=m_a_im, log_dt=m_log_dt, b_re=m_b_re, b_im=m_b_im,
             c_re=m_c_re, c_im=m_c_im, d_skip=m_d_skip, w_glu=m_w_glu, w_ssm_out=m_w_ssm_out, conv_w=m_conv_w,
             conv_b=m_conv_b, w_conv_out=m_w_conv_out, w_o=m_w_o, norm_ffn=m_norm_ffn, w_up=m_w_up,
             ffn_conv_w=m_ffn_conv_w, ffn_conv_b=m_ffn_conv_b, w_down=m_w_down, norm_final=m_norm_final)
    v = dict(norm_tok=v_norm_tok, w_in=v_w_in, a_re=v_a_re, a_im=v_a_im, log_dt=v_log_dt, b_re=v_b_re, b_im=v_b_im,
             c_re=v_c_re, c_im=v_c_im, d_skip=v_d_skip, w_glu=v_w_glu, w_ssm_out=v_w_ssm_out, conv_w=v_conv_w,
             conv_b=v_conv_b, w_conv_out=v_w_conv_out, w_o=v_w_o, norm_ffn=v_norm_ffn, w_up=v_w_up,
             ffn_conv_w=v_ffn_conv_w, ffn_conv_b=v_ffn_conv_b, w_down=v_w_down, norm_final=v_norm_final)
    return _step(_Cfg(), x, loss_target, p, m, v)
```

```python
import math
from typing import NamedTuple

import jax
import jax.numpy as jnp
from jax import lax
from jax.experimental import pallas as pl
from jax.experimental.pallas import tpu as pltpu
from jax.experimental.pallas import tpu_sc as plsc

f32 = jnp.float32
bf16 = jnp.bfloat16
_MESH = pl.DeviceIdType.MESH

_EPS = 1e-6
_ADAM_LR = 0.001
_ADAM_B1 = 0.9
_ADAM_B2 = 0.999
_ADAM_EPS = 1e-08
_ADAM_WD = 0.01
_ADAM_STEP = 10
_SSM_GROUP = 16
_SSM_STATE = 64
_SLAB_GROUPS = 16
_NCHIP = 4
_VMEM_LIMIT = 56 * 2**20
_VMEM_S5_BWD = 62 * 2**20
_GELU_C = math.sqrt(2.0 / math.pi)
_GELU_A = 0.044715


class _Cfg(NamedTuple):
    L: int = 4096
    D: int = 2048
    SW: int = 1024
    CW: int = 1024
    F: int = 5632
    T: int = 256


def _tile(n, pref, align):
    t = min(n, pref)
    t -= t % align
    while t > align and n % t:
        t -= align
    assert t > 0 and n % t == 0, (n, pref, align)
    return t


def _cparams(sem, vmem_limit=_VMEM_LIMIT):
    return pltpu.CompilerParams(dimension_semantics=sem, vmem_limit_bytes=vmem_limit)


def _gelu(x):
    return _gelu_and_grad(x)[0]


def _gelu_grad(x):
    return _gelu_and_grad(x)[1]


def _gelu_and_grad(x):
    x2 = x * x
    th = jnp.tanh(x * (_GELU_C + (_GELU_C * _GELU_A) * x2))
    half = 0.5 + 0.5 * th
    return x * half, half + (0.5 * x) * (1.0 - th * th) * (_GELU_C + (3.0 * _GELU_C * _GELU_A) * x2)


_NN = (((1,), (0,)), ((), ()))
_NT = (((1,), (1,)), ((), ()))
_TN = (((0,), (0,)), ((), ()))


def _whole(ref):
    return ref[...]


_ANY = pl.BlockSpec(memory_space=pl.ANY)


def _mm(name, operands, steps, *, grid, contract, outs, extras=(), epilogue=None, acc_shape=None, after=()):
    nop, ne, na = len(operands), len(extras), len(after)
    nk = len(steps)

    def body(*refs):
        op_refs = refs[:nop]
        e_refs = refs[nop:nop + ne]
        o_refs = refs[nop + ne + na:nop + ne + na + len(outs)]

        def partial(terms):
            tot = None
            for ai, av, bi, bv in terms:
                d = lax.dot_general(av(op_refs[ai]), bv(op_refs[bi]), contract, preferred_element_type=f32)
                tot = d if tot is None else tot + d
            return tot

        def finish(res):
            if epilogue is None:
                for o in o_refs:
                    o[...] = res.astype(o.dtype)
            else:
                epilogue(res, e_refs, o_refs)

        if nk == 1:
            finish(partial(steps[0][1]))
            return
        acc = refs[-1]
        kid = pl.program_id(len(grid) - 1)
        for k, terms in steps:
            def run(k=k, terms=terms):
                d = partial(terms)
                if k == 0:
                    acc[...] = d
                elif k < nk - 1:
                    acc[...] += d
                else:
                    finish(acc[...] + d)

            pl.when(kid == k)(run)

    sem = ("parallel",) * (len(grid) - (nk > 1)) + (("arbitrary",) if nk > 1 else ())
    return pl.pallas_call(
        body, grid=grid, in_specs=[o[1] for o in operands] + [e[1] for e in extras] + [_ANY] * na,
        out_specs=[o[1] for o in outs], out_shape=[o[0] for o in outs],
        scratch_shapes=[pltpu.VMEM(acc_shape, f32)] if nk > 1 else [], name=name,
        compiler_params=_cparams(sem))(*[o[0] for o in operands], *[e[0] for e in extras], *after)


def _mm_nn(name, a, w, out_dtypes, *, tm=1024, tn=1024, rows=False, extras_fn=None, epilogue=None):
    M, K = a.shape
    S, Ns = w.shape[0], w.shape[-1]
    N = Ns if rows else Ns * S
    tm, tn = _tile(M, tm, 16), _tile(Ns, tn, 128)
    nb = Ns // tn
    a_spec = pl.BlockSpec((tm, K), lambda i, j: (i, 0))
    if rows:
        b_spec = pl.BlockSpec((S, K // S, tn), lambda i, j: (0, 0, j))
        b_view = lambda r: r[...].reshape(K, tn)
    else:
        b_spec = pl.BlockSpec((None, K, tn), lambda i, j: (j // nb, 0, j % nb))
        b_view = _whole
    o_spec = pl.BlockSpec((tm, tn), lambda i, j: (i, j))
    outs = [(jax.ShapeDtypeStruct((M, N), dt), o_spec) for dt in out_dtypes]
    extras = extras_fn(tm, tn) if extras_fn is not None else ()
    return _mm(name, [(a, a_spec), (w, b_spec)], [(None, [(0, _whole, 1, b_view)])], grid=(M // tm, N // tn),
               contract=_NN, outs=outs, extras=extras, epilogue=epilogue)


def _mm_nt(name, a, w, out_dtypes, *, tm=1024, tn=1024, rows=False, extras_fn=None, epilogue=None):
    M, N = a.shape
    S, Ks, Ns = w.shape
    K = Ks * S if rows else Ks
    tm = _tile(M, tm, 16)
    a_spec = pl.BlockSpec((tm, N), lambda i, j: (i, 0))
    if rows:
        whole_shards = tn > Ks and tn % Ks == 0 and K % tn == 0
        tn = K if tn >= K else (tn if whole_shards else _tile(Ks, tn, 128))
        if tn == K or whole_shards:
            b_spec = pl.BlockSpec((tn // Ks, Ks, N), lambda i, j: (j, 0, 0))
            terms = [(0, _whole, 1, lambda r: r[...].reshape(tn, N))]
        else:
            nbs = Ks // tn
            b_spec = pl.BlockSpec((None, tn, N), lambda i, j: (j // nbs, j % nbs, 0))
            terms = [(0, _whole, 1, _whole)]
    else:
        tn = _tile(K, tn, 128)
        assert S * Ns == N
        b_spec = pl.BlockSpec((S, tn, Ns), lambda i, j: (0, j, 0))
        terms = [(0, lambda r, s=s: r[:, s * Ns:(s + 1) * Ns], 1, lambda r, s=s: r[s]) for s in range(S)]
    o_spec = pl.BlockSpec((tm, tn), lambda i, j: (i, j))
    outs = [(jax.ShapeDtypeStruct((M, K), dt), o_spec) for dt in out_dtypes]
    extras = extras_fn(tm, tn) if extras_fn is not None else ()
    return _mm(name, [(a, a_spec), (w, b_spec)], [(None, terms)], grid=(M // tm, K // tn), contract=_NT,
               outs=outs, extras=extras, epilogue=epilogue)


def _mm_tn(name, a, b, *, shards=None, tm=1024, tn=1024, b_block=None, b_resident=False, after=()):
    M, K = a.shape
    halves = b.shape[0] if b.ndim == 3 else 1
    Nh = b.shape[-1]
    N = Nh * halves
    Ns = N // shards if shards else N
    tm, tn = _tile(K, tm, 128), _tile(math.gcd(Ns, Nh), tn, 128)
    nb, nbh = Ns // tn, Nh // tn
    ij = (lambda g0, g1: (g1, g0)) if b_resident else (lambda g0, g1: (g0, g1))
    bmap = b_block if b_block is not None else (lambda j: j)
    a_spec = pl.BlockSpec((M, tm), lambda g0, g1: (0, ij(g0, g1)[0]))
    if halves > 1:
        b_spec = pl.BlockSpec((None, M, tn), lambda g0, g1: (bmap(ij(g0, g1)[1]) // nbh, 0, bmap(ij(g0, g1)[1]) % nbh))
    else:
        b_spec = pl.BlockSpec((M, tn), lambda g0, g1: (0, bmap(ij(g0, g1)[1])))
    if shards:
        out = (jax.ShapeDtypeStruct((shards, K, Ns), f32),
               pl.BlockSpec((None, tm, tn), lambda g0, g1: (ij(g0, g1)[1] // nb, ij(g0, g1)[0], ij(g0, g1)[1] % nb)))
    else:
        out = (jax.ShapeDtypeStruct((K, N), f32), pl.BlockSpec((tm, tn), lambda g0, g1: ij(g0, g1)))
    grid = (N // tn, K // tm) if b_resident else (K // tm, N // tn)
    return _mm(name, [(a, a_spec), (b, b_spec)], [(None, [(0, _whole, 1, _whole)])], grid=grid, contract=_TN,
               outs=[out], after=after)[0]


def _in_proj_dx(cfg, dproj, w_in):
    L, D, SW, CW = cfg.L, cfg.D, cfg.SW, cfg.CW
    NP = SW + 3 * CW + 2 * D
    Ns = NP // _NCHIP
    assert SW + CW == Ns and 2 * CW == Ns and D == Ns
    tm, tn = _tile(L, 1024, 16), _tile(D, 1024, 128)
    a_spec = pl.BlockSpec((tm, NP // 2), lambda i, j, k: (i, k))
    b_spec = pl.BlockSpec((2, tn, Ns), lambda i, j, k: (k, j, 0))
    first = [(0, lambda r: r[:, 0:CW], 1, lambda r: r[0, :, SW:SW + CW]),
             (0, lambda r: r[:, CW:3 * CW], 1, lambda r: r[1]),
             (0, lambda r: r[:, 3 * CW:3 * CW + SW], 1, lambda r: r[0, :, 0:SW])]
    second = [(0, lambda r: r[:, 0:D], 1, lambda r: r[0]), (0, lambda r: r[:, D:2 * D], 1, lambda r: r[1])]
    out = (jax.ShapeDtypeStruct((L, D), bf16), pl.BlockSpec((tm, tn), lambda i, j, k: (i, j)))
    return _mm("in_proj_dx", [(dproj, a_spec), (w_in, b_spec)], [(0, first), (1, second)], grid=(L // tm, D // tn, 2),
               contract=_NT, outs=[out], acc_shape=(tm, tn))[0]


def _out_proj_dx(cfg, dh1b, w_o, ya, yb, proj):
    L, D = cfg.L, cfg.D
    NP = cfg.SW + 3 * cfg.CW + 2 * D
    assert NP == 4 * D
    tm = _tile(L, 512, 16)

    def epilogue(acc, e, o):
        sa = jax.nn.sigmoid(e[2][:, 0:D].astype(f32))
        sb = jax.nn.sigmoid(e[2][:, D:2 * D].astype(f32))
        o[0][...] = (acc * sa).astype(bf16)
        o[1][...] = (acc * sb).astype(bf16)
        o[2][:, 0:D] = (acc * e[0][...].astype(f32) * sa * (1.0 - sa)).astype(bf16)
        o[2][:, D:2 * D] = (acc * e[1][...].astype(f32) * sb * (1.0 - sb)).astype(bf16)

    row = pl.BlockSpec((tm, D), lambda i, j: (i, 0))
    half = pl.BlockSpec((tm, 2 * D), lambda i, j: (i, 1))
    return _mm("out_proj_dx", [(dh1b, row), (w_o, pl.BlockSpec(w_o.shape, lambda i, j: (0, 0, 0),
                                                                pipeline_mode=pl.Buffered(1)))],
               [(None, [(0, _whole, 1, lambda r: r[...].reshape(D, D))])], grid=(L // tm, 1), contract=_NT,
               outs=[(jax.ShapeDtypeStruct((L, D), bf16), row), (jax.ShapeDtypeStruct((L, D), bf16), row),
                     (jax.ShapeDtypeStruct((L, NP), bf16), half)],
               extras=[(ya, row), (yb, row), (proj, half)], epilogue=epilogue)


def _ffn_up_dx(cfg, dhh, w_up):
    L, D, F = cfg.L, cfg.D, cfg.F
    Fh = F // 2
    tm, tn = _tile(L, 1024, 16), _tile(D, 512, 128)
    a_spec = pl.BlockSpec((None, tm, F), lambda i, j, k: (k, i, 0))
    b_spec = pl.BlockSpec((2, tn, Fh), lambda i, j, k: (k, j, 0))
    terms = [(0, lambda r: r[:, 0:Fh], 1, lambda r: r[0]), (0, lambda r: r[:, Fh:F], 1, lambda r: r[1])]
    out = (jax.ShapeDtypeStruct((L, D), bf16), pl.BlockSpec((tm, tn), lambda i, j, k: (i, j)))
    return _mm("ffn_up_dx", [(dhh, a_spec), (w_up, b_spec)], [(0, terms), (1, terms)], grid=(L // tm, D // tn, 2),
               contract=_NT, outs=[out], acc_shape=(tm, tn))[0]


def _rms_fwd(name, x, g):
    L, D = x.shape
    tm = _tile(L, 512, 16)

    def body(x_ref, g_ref, xn_ref, r_ref):
        xv = x_ref[...]
        r = lax.rsqrt(jnp.mean(xv * xv, axis=-1, keepdims=True) + _EPS)
        xn_ref[...] = (xv * r * g_ref[...]).astype(bf16)
        r_ref[...] = r

    return pl.pallas_call(
        body, grid=(L // tm,),
        in_specs=[pl.BlockSpec((tm, D), lambda i: (i, 0)), pl.BlockSpec((1, D), lambda i: (0, 0))],
        out_specs=[pl.BlockSpec((tm, D), lambda i: (i, 0)), pl.BlockSpec((tm, 1), lambda i: (i, 0))],
        out_shape=[jax.ShapeDtypeStruct((L, D), bf16), jax.ShapeDtypeStruct((L, 1), f32)],
        name=name, compiler_params=_cparams(("parallel",)))(x, g)


def _rms_bwd(name, dxn, h, r, g, dres, after=()):
    L, D = h.shape
    tm = _tile(L, 512, 16)

    def body(dxn_ref, h_ref, r_ref, g_ref, dres_ref, *rest):
        dh_ref, dhb_ref, dg_ref = rest[len(after):]
        i = pl.program_id(0)
        d = dxn_ref[...].astype(f32)
        hv = h_ref[...]
        rv = r_ref[...]
        dyg = d * g_ref[...]
        m = jnp.mean(dyg * hv, axis=-1, keepdims=True)
        dh = dres_ref[...] + rv * dyg - hv * (rv * rv * rv) * m
        dh_ref[...] = dh
        dhb_ref[...] = dh.astype(bf16)

        @pl.when(i == 0)
        def _():
            dg_ref[...] = jnp.zeros_like(dg_ref)

        dg_ref[...] += jnp.sum(d * hv * rv, axis=0, keepdims=True)

    row = lambda i: (i, 0)
    return pl.pallas_call(
        body, grid=(L // tm,),
        in_specs=[pl.BlockSpec((tm, D), row), pl.BlockSpec((tm, D), row), pl.BlockSpec((tm, 1), row),
                  pl.BlockSpec((1, D), lambda i: (0, 0)), pl.BlockSpec((tm, D), row)] + [_ANY] * len(after),
        out_specs=[pl.BlockSpec((tm, D), row), pl.BlockSpec((tm, D), row), pl.BlockSpec((1, D), lambda i: (0, 0))],
        out_shape=[jax.ShapeDtypeStruct((L, D), f32), jax.ShapeDtypeStruct((L, D), bf16), jax.ShapeDtypeStruct((1, D), f32)],
        name=name, compiler_params=_cparams(("arbitrary",)))(dxn, h, r, g, dres, *after)


def _loss_head(name, h2, tgt, g):
    L, D = h2.shape
    tm = _tile(L, 512, 16)

    def body(h_ref, t_ref, g_ref, dh_ref, dhb_ref, dg_ref, loss_ref):
        i = pl.program_id(0)
        hv = h_ref[...]
        gv = g_ref[...]
        r = lax.rsqrt(jnp.mean(hv * hv, axis=-1, keepdims=True) + _EPS)
        err = hv * r * gv - t_ref[...]
        dy = err * (1.0 / D)
        dyg = dy * gv
        m = jnp.mean(dyg * hv, axis=-1, keepdims=True)
        dh = r * dyg - hv * (r * r * r) * m
        dh_ref[...] = dh
        dhb_ref[...] = dh.astype(bf16)

        @pl.when(i == 0)
        def _():
            dg_ref[...] = jnp.zeros_like(dg_ref)
            loss_ref[...] = jnp.zeros_like(loss_ref)

        dg_ref[...] += jnp.sum(dy * hv * r, axis=0, keepdims=True)
        part = jnp.sum(jnp.sum(err * err, axis=-1, keepdims=True), axis=0, keepdims=True) * (0.5 / D)
        loss_ref[...] += jnp.broadcast_to(part, (8, 128))

    row = lambda i: (i, 0)
    return pl.pallas_call(
        body, grid=(L // tm,),
        in_specs=[pl.BlockSpec((tm, D), row), pl.BlockSpec((tm, D), row), pl.BlockSpec((1, D), lambda i: (0, 0))],
        out_specs=[pl.BlockSpec((tm, D), row), pl.BlockSpec((tm, D), row), pl.BlockSpec((1, D), lambda i: (0, 0)),
                   pl.BlockSpec((8, 128), lambda i: (0, 0))],
        out_shape=[jax.ShapeDtypeStruct((L, D), f32), jax.ShapeDtypeStruct((L, D), bf16),
                   jax.ShapeDtypeStruct((1, D), f32), jax.ShapeDtypeStruct((8, 128), f32)],
        name=name, compiler_params=_cparams(("arbitrary",)))(h2, tgt, g)


def _shift_down(tile, halo, k, rows8):
    tm = tile.shape[0]
    r = pltpu.roll(tile, k, axis=0)
    hh = pltpu.roll(halo, k, axis=0)
    top = jnp.where(rows8 < k, hh, r[:8])
    return jnp.concatenate([top, r[8:]], axis=0) if tm > 8 else top


def _shift_up(tile, halo, k, rows8):
    tm = tile.shape[0]
    r = pltpu.roll(tile, tm - k, axis=0)
    hh = pltpu.roll(halo, 8 - k, axis=0)
    bot = jnp.where(rows8 >= 8 - k, hh, r[tm - 8:])
    return jnp.concatenate([r[:tm - 8], bot], axis=0) if tm > 8 else bot


def _conv3(x, halo, w_ref, b_ref, rows8):
    return (w_ref[0:1, :] * _shift_down(x, halo, 2, rows8) + w_ref[1:2, :] * _shift_down(x, halo, 1, rows8)
            + w_ref[2:3, :] * x + b_ref[...])


def _convb_fwd(cfg, proj, w, b):
    L, CW = cfg.L, cfg.CW
    assert cfg.SW == CW
    tm = _tile(L, 512, 16)

    def body(v_ref, vh_ref, gb_ref, gc_ref, gch_ref, w_ref, b_ref, o_ref):
        i = pl.program_id(0)
        rows8 = lax.broadcasted_iota(jnp.int32, (8, CW), 0)
        cv = gc_ref[...].astype(f32) * v_ref[...].astype(f32)
        cvh = gch_ref[...].astype(f32)[8:] * vh_ref[...].astype(f32)[8:]
        cvh = jnp.where(i == 0, 0.0, cvh)
        cc = _conv3(cv, cvh, w_ref, b_ref, rows8)
        o_ref[...] = (gb_ref[...].astype(f32) * cc).astype(bf16)

    blk = lambda col: pl.BlockSpec((tm, CW), lambda i: (i, col))
    halo = lambda col: pl.BlockSpec((16, CW), lambda i: (jnp.maximum(i * (tm // 16) - 1, 0), col))
    return pl.pallas_call(
        body, grid=(L // tm,),
        in_specs=[blk(1), halo(1), blk(2), blk(3), halo(3),
                  pl.BlockSpec((3, CW), lambda i: (0, 0)), pl.BlockSpec((1, CW), lambda i: (0, 0))],
        out_specs=pl.BlockSpec((tm, CW), lambda i: (i, 0)),
        out_shape=jax.ShapeDtypeStruct((L, CW), bf16),
        name="convb_fwd", compiler_params=_cparams(("parallel",)))(proj, proj, proj, proj, proj, w, b)


def _convb_bwd(cfg, proj, dyb0, w, b, dproj):
    L, CW = cfg.L, cfg.CW
    tm = _tile(L, 512, 16)
    nt = L // tm

    def body(v_ref, vh_ref, gb_ref, gbn_ref, gc_ref, gch_ref, d_ref, dn_ref, w_ref, b_ref, dproj_ref,
             o_ref, dw_ref, db_ref):
        i = pl.program_id(0)
        rows8 = lax.broadcasted_iota(jnp.int32, (8, CW), 0)
        v = v_ref[...].astype(f32)
        gb = gb_ref[...].astype(f32)
        gc = gc_ref[...].astype(f32)
        d = d_ref[...].astype(f32)
        cv = gc * v
        cvh = gch_ref[...].astype(f32)[8:] * vh_ref[...].astype(f32)[8:]
        cvh = jnp.where(i == 0, 0.0, cvh)
        s2 = _shift_down(cv, cvh, 2, rows8)
        s1 = _shift_down(cv, cvh, 1, rows8)
        cc = w_ref[0:1, :] * s2 + w_ref[1:2, :] * s1 + w_ref[2:3, :] * cv + b_ref[...]
        dcc = d * gb
        dccn = dn_ref[...].astype(f32)[:8] * gbn_ref[...].astype(f32)[:8]
        dccn = jnp.where(i == nt - 1, 0.0, dccn)
        dcv = (w_ref[2:3, :] * dcc + w_ref[1:2, :] * _shift_up(dcc, dccn, 1, rows8)
               + w_ref[0:1, :] * _shift_up(dcc, dccn, 2, rows8))
        o_ref[:, 0:CW] = (dcv * gc).astype(bf16)
        o_ref[:, CW:2 * CW] = (d * cc).astype(bf16)
        o_ref[:, 2 * CW:3 * CW] = (dcv * v).astype(bf16)

        @pl.when(i == 0)
        def _():
            dw_ref[...] = jnp.zeros_like(dw_ref)
            db_ref[...] = jnp.zeros_like(db_ref)

        dw_ref[0:1, :] += jnp.sum(dcc * s2, axis=0, keepdims=True)
        dw_ref[1:2, :] += jnp.sum(dcc * s1, axis=0, keepdims=True)
        dw_ref[2:3, :] += jnp.sum(dcc * cv, axis=0, keepdims=True)
        db_ref[...] += jnp.sum(dcc, axis=0, keepdims=True)

    blk = lambda col: pl.BlockSpec((tm, CW), lambda i: (i, col))
    prev = lambda col: pl.BlockSpec((16, CW), lambda i: (jnp.maximum(i * (tm // 16) - 1, 0), col))
    nxt = lambda col: pl.BlockSpec((16, CW), lambda i: (jnp.minimum((i + 1) * (tm // 16), L // 16 - 1), col))
    const = lambda r: pl.BlockSpec((r, CW), lambda i: (0, 0))
    return pl.pallas_call(
        body, grid=(nt,),
        in_specs=[blk(1), prev(1), blk(2), nxt(2), blk(3), prev(3), blk(0), nxt(0), const(3), const(1),
                  pl.BlockSpec(memory_space=pl.ANY)],
        out_specs=[pl.BlockSpec((tm, 3 * CW), lambda i: (i, 0)), const(3), const(1)],
        out_shape=[jax.ShapeDtypeStruct(dproj.shape, bf16), jax.ShapeDtypeStruct((3, CW), f32),
                   jax.ShapeDtypeStruct((1, CW), f32)],
        input_output_aliases={10: 0},
        name="convb_bwd", compiler_params=_cparams(("arbitrary",)))(proj, proj, proj, proj, proj, proj, dyb0, dyb0, w, b,
                                                                    dproj)


def _ffn_act(cfg, hh, w, b):
    L, F = cfg.L, cfg.F
    tm = _tile(L, 512, 16)
    tc = _tile(F, 1408, 128)
    ncb = F // tc

    def body(a_ref, ah_ref, g_ref, w_ref, b_ref, o_ref, act_ref):
        i = pl.program_id(0)
        rows8 = lax.broadcasted_iota(jnp.int32, (8, tc), 0)
        a = a_ref[...].astype(f32)
        ah = jnp.where(i == 0, 0.0, ah_ref[...].astype(f32)[8:])
        act = _conv3(a, ah, w_ref, b_ref, rows8)
        act_ref[...] = act.astype(bf16)
        o_ref[...] = (_gelu(act) * g_ref[...].astype(f32)).astype(bf16)

    tile = pl.BlockSpec((tm, tc), lambda i, j: (i, j))
    return pl.pallas_call(
        body, grid=(L // tm, ncb),
        in_specs=[tile, pl.BlockSpec((16, tc), lambda i, j: (jnp.maximum(i * (tm // 16) - 1, 0), j)),
                  pl.BlockSpec((tm, tc), lambda i, j: (i, j + ncb)),
                  pl.BlockSpec((3, tc), lambda i, j: (0, j)), pl.BlockSpec((1, tc), lambda i, j: (0, j))],
        out_specs=[tile, tile], out_shape=[jax.ShapeDtypeStruct((L, F), bf16)] * 2,
        name="ffn_act", compiler_params=_cparams(("parallel", "parallel")))(hh, hh, hh, w, b)


def _ffn_down_dx_act_bwd(cfg, dh2b, w_down, hh, act, w):
    L, D, F = cfg.L, cfg.D, cfg.F
    S, Ks, _ = w_down.shape
    tm = _tile(L, 512, 16)
    tc = _tile(Ks, 1408, 128)
    ncb = F // tc
    nbs = Ks // tc
    nt = L // tm

    slabs = [(c0, min(256, tc - c0)) for c0 in range(0, tc, 256)]

    def body(dh_ref, wd_ref, a_ref, c_ref, g_ref, w_ref, dhh_ref, dw_ref, db_ref, below):
        i = pl.program_id(1)

        @pl.when(i == 0)
        def _():
            below[...] = jnp.zeros_like(below)
            dw_ref[...] = jnp.zeros_like(dw_ref)
            db_ref[...] = jnp.zeros_like(db_ref)

        dh = dh_ref[...]
        for c0, cw in slabs:
            cols = slice(c0, c0 + cw)
            rows8 = lax.broadcasted_iota(jnp.int32, (8, cw), 0)
            d = lax.dot_general(dh, wd_ref[c0:c0 + cw, :], _NT, preferred_element_type=f32)
            a = a_ref[:, cols].astype(f32)
            gelu, gelu_d = _gelu_and_grad(c_ref[:, cols].astype(f32))
            dhh_ref[1, :, cols] = (d * gelu).astype(bf16)
            dact = d * g_ref[:, cols].astype(f32) * gelu_d
            dactn = below[:, cols]
            up1 = _shift_up(dact, dactn, 1, rows8)
            up2 = _shift_up(dact, dactn, 2, rows8)
            below[:, cols] = dact[:8]
            dhh_ref[0, :, cols] = (w_ref[2:3, cols] * dact + w_ref[1:2, cols] * up1 + w_ref[0:1, cols] * up2).astype(bf16)
            dw_ref[0:1, cols] += jnp.sum(up2 * a, axis=0, keepdims=True)
            dw_ref[1:2, cols] += jnp.sum(up1 * a, axis=0, keepdims=True)
            dw_ref[2:3, cols] += jnp.sum(dact * a, axis=0, keepdims=True)
            db_ref[:, cols] += jnp.sum(dact, axis=0, keepdims=True)

    up = lambda i: nt - 1 - i
    blk = lambda off: pl.BlockSpec((tm, tc), lambda j, i: (up(i), j + off))
    const = lambda r: pl.BlockSpec((r, tc), lambda j, i: (0, j))
    return pl.pallas_call(
        body, grid=(ncb, nt),
        in_specs=[pl.BlockSpec((tm, D), lambda j, i: (up(i), 0)),
                  pl.BlockSpec((None, tc, D), lambda j, i: (j // nbs, j % nbs, 0)),
                  blk(0), blk(0), blk(ncb), const(3)],
        out_specs=[pl.BlockSpec((2, tm, tc), lambda j, i: (0, up(i), j)), const(3), const(1)],
        out_shape=[jax.ShapeDtypeStruct((2, L, F), bf16),
                   jax.ShapeDtypeStruct((3, F), f32), jax.ShapeDtypeStruct((1, F), f32)],
        scratch_shapes=[pltpu.VMEM((8, tc), f32)],
        name="ffn_down_dx_act_bwd", compiler_params=_cparams(("parallel", "arbitrary")))(dh2b, w_down, hh, act, hh, w)


def _merge_fwd(cfg, ya1, yb0, proj, wso, wco):
    L, D, SW, CW = cfg.L, cfg.D, cfg.SW, cfg.CW
    Ns = D // _NCHIP
    tm = _tile(L, 1024, 16)
    tn = _tile(Ns, 512, 128)
    nb = Ns // tn
    off_a = (SW + 3 * CW) // tn
    off_b = (SW + 3 * CW + D) // tn

    def body(a_ref, b_ref, wa_ref, wb_ref, ma_ref, mb_ref, m_ref, ya_ref, yb_ref):
        ya = jnp.dot(a_ref[...], wa_ref[...], preferred_element_type=f32)
        yb = jnp.dot(b_ref[...], wb_ref[...], preferred_element_type=f32)
        sa = jax.nn.sigmoid(ma_ref[...].astype(f32))
        sb = jax.nn.sigmoid(mb_ref[...].astype(f32))
        m_ref[...] = (sa * ya + sb * yb).astype(bf16)
        ya_ref[...] = ya.astype(bf16)
        yb_ref[...] = yb.astype(bf16)

    o_spec = pl.BlockSpec((tm, tn), lambda i, j: (i, j))
    o_shape = jax.ShapeDtypeStruct((L, D), bf16)
    return pl.pallas_call(
        body, grid=(L // tm, D // tn),
        in_specs=[pl.BlockSpec((tm, SW), lambda i, j: (i, 0)), pl.BlockSpec((tm, CW), lambda i, j: (i, 0)),
                  pl.BlockSpec((None, SW, tn), lambda i, j: (j // nb, 0, j % nb)),
                  pl.BlockSpec((None, CW, tn), lambda i, j: (j // nb, 0, j % nb)),
                  pl.BlockSpec((tm, tn), lambda i, j: (i, off_a + j)), pl.BlockSpec((tm, tn), lambda i, j: (i, off_b + j))],
        out_specs=[o_spec, o_spec, o_spec], out_shape=[o_shape, o_shape, o_shape],
        name="merge_fwd", compiler_params=_cparams(("parallel", "parallel")))(ya1, yb0, wso, wco, proj, proj)


def _s5_dims(cfg):
    G = cfg.SW // _SSM_GROUP
    NS = G // _SLAB_GROUPS
    SC = _SLAB_GROUPS * _SSM_GROUP
    SH = _SLAB_GROUPS * _SSM_STATE
    NST = 2 * SH * NS
    return G, NS, SC, SH, NST


def _lane_slabs(cfg, W):
    _, NS, _, SH, _ = _s5_dims(cfg)
    return [(2 * SH * s + w0, 2 * SH * s + SH + w0) for s in range(NS) for w0 in range(0, SH, W)]


def _discretize(a_re, a_im, log_dt, b_re, b_im):
    dt = jnp.exp(log_dt)[:, None]
    mag = jnp.exp(dt * a_re)
    abr = mag * jnp.cos(dt * a_im)
    abi = mag * jnp.sin(dt * a_im)
    nr = abr - 1.0
    ni = abi
    den = a_re * a_re + a_im * a_im
    fr = (nr * a_re + ni * a_im) / den
    fi = (ni * a_re - nr * a_im) / den
    bbr = fr[..., None] * b_re - fi[..., None] * b_im
    bbi = fr[..., None] * b_im + fi[..., None] * b_re
    return abr, abi, bbr, bbi


def _state_rows(cfg, re, im):
    _, NS, _, SH, _ = _s5_dims(cfg)
    return jnp.concatenate([re.reshape(NS, SH), im.reshape(NS, SH)], axis=1).reshape(-1)


def _s5_tables(cfg, a_re, a_im, log_dt, abr, abi, bbr, bbi, c_re, c_im):
    G, NS, SC, SH, NST = _s5_dims(cfg)
    S = cfg.T // 8
    eye = jnp.eye(_SLAB_GROUPS, dtype=bf16)
    bb = jnp.stack([bbr, bbi]).reshape(2, NS, _SLAB_GROUPS, _SSM_STATE, _SSM_GROUP).astype(bf16)
    bs = (bb.transpose(1, 2, 4, 0, 3)[:, :, :, :, None, :] * eye[None, :, None, None, :, None]).reshape(NS, SC, 2 * SH)
    cc = jnp.stack([c_re, -c_im]).reshape(2, NS, _SLAB_GROUPS, _SSM_GROUP, _SSM_STATE).astype(bf16)
    cs = (cc.transpose(1, 0, 4, 2, 3)[:, :, None, :, :, :] * eye[None, None, :, None, :, None]).reshape(NS, 2 * SH, SC)
    arep = jnp.broadcast_to(_state_rows(cfg, abr, abi)[None, :], (8, NST))
    sdt = S * jnp.exp(log_dt)[:, None]
    mag = jnp.exp(sdt * a_re)
    apow = jnp.broadcast_to(_state_rows(cfg, mag * jnp.cos(sdt * a_im), mag * jnp.sin(sdt * a_im))[None, :], (8, NST))
    t = jnp.arange(cfg.T)
    perm = (t % 8) * S + t // 8
    pm = jax.nn.one_hot(perm, cfg.T, dtype=bf16)
    return bs, cs, arep, apow, pm, pm.T


def _cmul_add(ar, ai, xr, xi, br, bi):
    return ar * xr - ai * xi + br, ar * xi + ai * xr + bi


def _s5_forward_chunk(cfg, W, upb, bs_ref, arep_ref, apow_ref, st, x0, cin_store):
    _, NS, SC, SH, _ = _s5_dims(cfg)
    S = cfg.T // 8
    for s in range(NS):
        st[:, 2 * SH * s:2 * SH * (s + 1)] = jnp.dot(upb[:, SC * s:SC * (s + 1)], bs_ref[s], preferred_element_type=f32)
    rows = lax.broadcasted_iota(jnp.int32, (8, W), 0)
    zero = jnp.zeros((8, W), f32)
    for rc, ic in _lane_slabs(cfg, W):
        ar = arep_ref[:, rc:rc + W]
        ai = arep_ref[:, ic:ic + W]

        def step(i, carry, rc=rc, ic=ic, ar=ar, ai=ai):
            xr, xi = carry
            r0 = pl.multiple_of(i * 8, 8)
            nr, ni = _cmul_add(ar, ai, xr, xi, st[pl.ds(r0, 8), rc:rc + W], st[pl.ds(r0, 8), ic:ic + W])
            st[pl.ds(r0, 8), rc:rc + W] = nr
            st[pl.ds(r0, 8), ic:ic + W] = ni
            return nr, ni

        er, ei = lax.fori_loop(0, S, step, (zero, zero))
        pr = apow_ref[:, rc:rc + W]
        pi = apow_ref[:, ic:ic + W]
        x0r = x0[:, rc:rc + W]
        x0i = x0[:, ic:ic + W]
        cr = jnp.where(rows == 0, x0r, 0.0)
        ci = jnp.where(rows == 0, x0i, 0.0)
        for _ in range(7):
            fr, fi = _cmul_add(pr, pi, cr, ci, er, ei)
            cr = jnp.where(rows == 0, x0r, pltpu.roll(fr, 1, axis=0))
            ci = jnp.where(rows == 0, x0i, pltpu.roll(fi, 1, axis=0))
        fr, fi = _cmul_add(pr, pi, cr, ci, er, ei)
        x0[:, rc:rc + W] = jnp.broadcast_to(fr[7:8, :], (8, W))
        x0[:, ic:ic + W] = jnp.broadcast_to(fi[7:8, :], (8, W))
        cin_store(rc, ic, cr, ci)

        def fix(i, carry, rc=rc, ic=ic, ar=ar, ai=ai):
            kr, ki = carry
            r0 = pl.multiple_of(i * 8, 8)
            nr, ni = ar * kr - ai * ki, ar * ki + ai * kr
            st[pl.ds(r0, 8), rc:rc + W] = st[pl.ds(r0, 8), rc:rc + W] + nr
            st[pl.ds(r0, 8), ic:ic + W] = st[pl.ds(r0, 8), ic:ic + W] + ni
            return nr, ni

        lax.fori_loop(0, S, fix, (cr, ci))


def _s5_fwd(cfg, proj, tabs, dskip):
    L, SW, T = cfg.L, cfg.SW, cfg.T
    G, NS, SC, SH, NST = _s5_dims(cfg)
    bs, cs, arep, apow, pm, pmt = tabs
    W = min(512, SH)
    NC = L // T

    def body(u_ref, pm_ref, pmt_ref, bs_ref, cs_ref, arep_ref, apow_ref, dskip_ref, y_ref, ya0_ref, cin_ref, st, x0):
        c = pl.program_id(0)

        @pl.when(c == 0)
        def _():
            x0[...] = jnp.zeros_like(x0)

        up = jnp.dot(pm_ref[...], u_ref[...], preferred_element_type=f32)
        upb = up.astype(bf16)

        def cin_store(rc, ic, cr, ci):
            cin_ref[0, :, rc:rc + W] = cr
            cin_ref[0, :, ic:ic + W] = ci

        _s5_forward_chunk(cfg, W, upb, bs_ref, arep_ref, apow_ref, st, x0, cin_store)
        yp = jnp.concatenate(
            [jnp.dot(st[:, 2 * SH * s:2 * SH * (s + 1)].astype(bf16), cs_ref[s], preferred_element_type=f32)
             for s in range(NS)], axis=1) + dskip_ref[...] * up
        y = jnp.dot(pmt_ref[...], yp.astype(bf16), preferred_element_type=f32)
        y_ref[...] = y.astype(bf16)
        ya0_ref[...] = _gelu(y).astype(bf16)

    const2 = lambda shape: pl.BlockSpec(shape, lambda c: (0, 0))
    const3 = lambda shape: pl.BlockSpec(shape, lambda c: (0, 0, 0))
    return pl.pallas_call(
        body, grid=(NC,),
        in_specs=[pl.BlockSpec((T, SW), lambda c: (c, 0)), const2((T, T)), const2((T, T)), const3((NS, SC, 2 * SH)),
                  const3((NS, 2 * SH, SC)), const2((8, NST)), const2((8, NST)), const2((1, SW))],
        out_specs=[pl.BlockSpec((T, SW), lambda c: (c, 0)), pl.BlockSpec((T, SW), lambda c: (c, 0)),
                   pl.BlockSpec((1, 8, NST), lambda c: (c, 0, 0)), pl.BlockSpec((T, NST), lambda c: (c, 0))],
        out_shape=[jax.ShapeDtypeStruct((L, SW), bf16), jax.ShapeDtypeStruct((L, SW), bf16),
                   jax.ShapeDtypeStruct((NC, 8, NST), f32), jax.ShapeDtypeStruct((L, NST), f32)],
        scratch_shapes=[pltpu.VMEM((8, NST), f32)],
        name="s5_fwd", compiler_params=_cparams(("arbitrary",)))(proj, pm, pmt, bs, cs, arep, apow, dskip)


def _s5_bwd(cfg, proj, dy, cin, xs, tabs, dskip, dproj, after=()):
    L, SW, T = cfg.L, cfg.SW, cfg.T
    du_col = 3 * cfg.CW // SW
    G, NS, SC, SH, NST = _s5_dims(cfg)
    bs, cs, arep, apow, pm, pmt = tabs
    W = min(512, SH)
    S = T // 8
    NC = L // T

    def body(u_ref, dy_ref, cin_ref, st, pm_ref, pmt_ref, bs_ref, cs_ref, arep_ref, apow_ref, dskip_ref, dproj_ref, *rest):
        du_ref, da_ref, db_ref, dc_ref, dd_ref, gs, g0, db_acc, dc_acc = rest[len(after):]
        c = pl.program_id(0)

        @pl.when(c == 0)
        def _():
            g0[...] = jnp.zeros_like(g0)
            da_ref[...] = jnp.zeros_like(da_ref)
            dd_ref[...] = jnp.zeros_like(dd_ref)
            db_acc[...] = jnp.zeros_like(db_acc)
            dc_acc[...] = jnp.zeros_like(dc_acc)

        up = jnp.dot(pm_ref[...], u_ref[...], preferred_element_type=f32)
        upb = up.astype(bf16)
        dyp = jnp.dot(pm_ref[...], dy_ref[...], preferred_element_type=f32)
        dypb = dyp.astype(bf16)
        for s in range(NS):
            gs[:, 2 * SH * s:2 * SH * (s + 1)] = lax.dot_general(
                dypb[:, SC * s:SC * (s + 1)], cs_ref[s], (((1,), (1,)), ((), ())), preferred_element_type=f32)
        rows = lax.broadcasted_iota(jnp.int32, (8, W), 0)
        zero = jnp.zeros((8, W), f32)
        for rc, ic in _lane_slabs(cfg, W):
            ar = arep_ref[:, rc:rc + W]
            ai = arep_ref[:, ic:ic + W]

            def rstep(k, carry, rc=rc, ic=ic, ar=ar, ai=ai):
                gr, gi = carry
                r0 = pl.multiple_of((S - 1 - k) * 8, 8)
                nr = ar * gr + ai * gi + gs[pl.ds(r0, 8), rc:rc + W]
                ni = ar * gi - ai * gr + gs[pl.ds(r0, 8), ic:ic + W]
                gs[pl.ds(r0, 8), rc:rc + W] = nr
                gs[pl.ds(r0, 8), ic:ic + W] = ni
                return nr, ni

            er, ei = lax.fori_loop(0, S, rstep, (zero, zero))
            pr = apow_ref[:, rc:rc + W]
            pi = apow_ref[:, ic:ic + W]
            g0r = g0[:, rc:rc + W]
            g0i = g0[:, ic:ic + W]
            cr = jnp.where(rows == 7, g0r, 0.0)
            ci = jnp.where(rows == 7, g0i, 0.0)
            for _ in range(7):
                fr = er + pr * cr + pi * ci
                fi = ei + pr * ci - pi * cr
                cr = jnp.where(rows == 7, g0r, pltpu.roll(fr, 7, axis=0))
                ci = jnp.where(rows == 7, g0i, pltpu.roll(fi, 7, axis=0))
            fr = er + pr * cr + pi * ci
            fi = ei + pr * ci - pi * cr
            g0[:, rc:rc + W] = jnp.broadcast_to(fr[0:1, :], (8, W))
            g0[:, ic:ic + W] = jnp.broadcast_to(fi[0:1, :], (8, W))

            def fix(k, carry, rc=rc, ic=ic, ar=ar, ai=ai):
                kr, ki, accr, acci = carry
                i = S - 1 - k
                r0 = pl.multiple_of(i * 8, 8)
                rp = pl.multiple_of((i - 1) * 8, 8)
                nr = ar * kr + ai * ki
                ni = ar * ki - ai * kr
                gr = gs[pl.ds(r0, 8), rc:rc + W] + nr
                gi = gs[pl.ds(r0, 8), ic:ic + W] + ni
                gs[pl.ds(r0, 8), rc:rc + W] = gr
                gs[pl.ds(r0, 8), ic:ic + W] = gi
                xr = st[pl.ds(rp, 8), rc:rc + W]
                xi = st[pl.ds(rp, 8), ic:ic + W]
                return nr, ni, accr + gr * xr + gi * xi, acci + gi * xr - gr * xi

            kr, ki, accr, acci = lax.fori_loop(0, S - 1, fix, (cr, ci, zero, zero))
            nr = ar * kr + ai * ki
            ni = ar * ki - ai * kr
            gr = gs[0:8, rc:rc + W] + nr
            gi = gs[0:8, ic:ic + W] + ni
            gs[0:8, rc:rc + W] = gr
            gs[0:8, ic:ic + W] = gi
            xr = cin_ref[0, :, rc:rc + W]
            xi = cin_ref[0, :, ic:ic + W]
            da_ref[:, rc:rc + W] += accr + gr * xr + gi * xi
            da_ref[:, ic:ic + W] += acci + gi * xr - gr * xi

        dups = []
        for s in range(NS):
            gsb = gs[:, 2 * SH * s:2 * SH * (s + 1)].astype(bf16)
            dups.append(lax.dot_general(gsb, bs_ref[s], (((1,), (1,)), ((), ())), preferred_element_type=f32))
            db_acc[s] += lax.dot_general(upb[:, SC * s:SC * (s + 1)], gsb, (((0,), (0,)), ((), ())),
                                         preferred_element_type=f32)
            dc_acc[s] += lax.dot_general(st[:, 2 * SH * s:2 * SH * (s + 1)].astype(bf16), dypb[:, SC * s:SC * (s + 1)],
                                         (((0,), (0,)), ((), ())), preferred_element_type=f32)
        dup = jnp.concatenate(dups, axis=1) + dskip_ref[...] * dyp
        du_ref[...] = jnp.dot(pmt_ref[...], dup.astype(bf16), preferred_element_type=f32).astype(bf16)
        dd_ref[...] += jnp.sum(dyp * up, axis=0, keepdims=True)

        @pl.when(c == NC - 1)
        def _():
            PS, GH = _SSM_STATE, _SSM_GROUP
            mask_b = (lax.broadcasted_iota(jnp.int32, (SC, SH), 0) // GH
                      == lax.broadcasted_iota(jnp.int32, (SC, SH), 1) // PS)
            mask_c = (lax.broadcasted_iota(jnp.int32, (SH, SC), 0) // PS
                      == lax.broadcasted_iota(jnp.int32, (SH, SC), 1) // GH)
            for s in range(NS):
                for r in range(2):
                    xb = jnp.where(mask_b, db_acc[s, :, r * SH:(r + 1) * SH], 0.0)
                    zb = xb[:, 0:128]
                    for q in range(1, SH // 128):
                        zb = zb + xb[:, q * 128:(q + 1) * 128]
                    db_ref[s, r] = zb + pltpu.roll(zb, PS, axis=1)
                    xc = jnp.where(mask_c, dc_acc[s, r * SH:(r + 1) * SH, :], 0.0)
                    zc = xc[0:PS]
                    for q in range(1, SH // PS):
                        zc = zc + xc[q * PS:(q + 1) * PS]
                    dc_ref[s, r] = zc

    rev = lambda c: (NC - 1 - c, 0)
    const2 = lambda shape: pl.BlockSpec(shape, lambda c: (0, 0))
    const3 = lambda shape: pl.BlockSpec(shape, lambda c: (0, 0, 0))
    const4 = lambda shape: pl.BlockSpec(shape, lambda c: (0, 0, 0, 0))
    return pl.pallas_call(
        body, grid=(NC,),
        in_specs=[pl.BlockSpec((T, SW), rev), pl.BlockSpec((T, SW), rev), pl.BlockSpec((1, 8, NST), lambda c: (NC - 1 - c, 0, 0)),
                  pl.BlockSpec((T, NST), rev), const2((T, T)), const2((T, T)),
                  pl.BlockSpec((NS, SC, 2 * SH), lambda c: (0, 0, 0), pipeline_mode=pl.Buffered(1)),
                  pl.BlockSpec((NS, 2 * SH, SC), lambda c: (0, 0, 0), pipeline_mode=pl.Buffered(1)),
                  const2((8, NST)), const2((8, NST)), const2((1, SW)), _ANY] + [_ANY] * len(after),
        out_specs=[pl.BlockSpec((T, SW), lambda c: (NC - 1 - c, du_col)), const2((8, NST)),
                   const4((NS, 2, SC, 128)), const4((NS, 2, _SSM_STATE, SC)), const2((1, SW))],
        out_shape=[jax.ShapeDtypeStruct(dproj.shape, bf16), jax.ShapeDtypeStruct((8, NST), f32),
                   jax.ShapeDtypeStruct((NS, 2, SC, 128), f32), jax.ShapeDtypeStruct((NS, 2, _SSM_STATE, SC), f32),
                   jax.ShapeDtypeStruct((1, SW), f32)],
        scratch_shapes=[pltpu.VMEM((T, NST), f32), pltpu.VMEM((8, NST), f32),
                        pltpu.VMEM((NS, SC, 2 * SH), f32), pltpu.VMEM((NS, 2 * SH, SC), f32)],
        input_output_aliases={11: 0},
        name="s5_bwd", compiler_params=_cparams(("arbitrary",), _VMEM_S5_BWD))(proj, dy, cin, xs, pm, pmt, bs, cs, arep, apow, dskip, dproj,
                                                                  *after)


def _s5_param_grads(cfg, da, db_diag, dc_diag):
    G, NS, SC, SH, NST = _s5_dims(cfg)
    das = da.sum(axis=0).reshape(NS, 2, SH)
    dabr = das[:, 0].reshape(G, _SSM_STATE)
    dabi = das[:, 1].reshape(G, _SSM_STATE)
    dbd = db_diag[..., :_SSM_STATE].reshape(NS, 2, _SLAB_GROUPS, _SSM_GROUP, _SSM_STATE)
    dbb = dbd.transpose(1, 0, 2, 4, 3).reshape(2, G, _SSM_STATE, _SSM_GROUP)
    dcd = dc_diag.reshape(NS, 2, _SSM_STATE, _SLAB_GROUPS, _SSM_GROUP)
    dcc = dcd.transpose(1, 0, 3, 4, 2).reshape(2, G, _SSM_GROUP, _SSM_STATE)
    return dabr, dabi, dbb[0], dbb[1], dcc[0], -dcc[1]


def _coords():
    return lax.axis_index("x"), lax.axis_index("y"), lax.axis_index("c")


def _other_chips(x, y):
    return [(1 - x, y), (x, 1 - y), (1 - x, 1 - y)]


def _allreduce8(name, v):
    R = v.shape[0]

    def body(v_ref, o_ref, sib, chips, mine, ssem, rsem):
        x, y, c = _coords()
        d2d = pltpu.make_async_remote_copy(src_ref=v_ref, dst_ref=sib, send_sem=ssem.at[0], recv_sem=rsem.at[0],
                                           device_id=(x, y, 1 - c), device_id_type=_MESH)
        d2d.start()
        d2d.wait()
        mine[...] = v_ref[...] + sib[...]
        cps = [pltpu.make_async_remote_copy(src_ref=mine, dst_ref=chips.at[j], send_sem=ssem.at[1 + j],
                                            recv_sem=rsem.at[1 + j], device_id=(*chip, c), device_id_type=_MESH)
               for j, chip in enumerate(_other_chips(x, y))]
        for cp in cps:
            cp.start()
        for cp in cps:
            cp.wait()
        o_ref[...] = (mine[...] + chips[1]) + (chips[0] + chips[2])

    vm = pl.BlockSpec(memory_space=pltpu.VMEM)
    return pl.pallas_call(
        body, in_specs=[vm], out_specs=vm, out_shape=jax.ShapeDtypeStruct((R, 128), f32),
        scratch_shapes=[pltpu.VMEM((R, 128), f32), pltpu.VMEM((3, R, 128), f32), pltpu.VMEM((R, 128), f32),
                        pltpu.SemaphoreType.DMA((4,)), pltpu.SemaphoreType.DMA((4,))],
        name=name, compiler_params=pltpu.CompilerParams(vmem_limit_bytes=_VMEM_LIMIT))(v)


def _cast_into_slot(name, w, k_idx, also_alone=False):
    R, C = w.shape
    tr = _tile(R, 256, 16)

    def body(k_ref, w_ref, *o_refs):
        for o_ref in o_refs:
            o_ref[...] = w_ref[...].astype(bf16)

    slot = (jax.ShapeDtypeStruct((_NCHIP, R, C), bf16), pl.BlockSpec((None, tr, C), lambda r, kr: (kr[0], r, 0)))
    alone = (jax.ShapeDtypeStruct((R, C), bf16), pl.BlockSpec((tr, C), lambda r, kr: (r, 0)))
    outs = [slot, alone] if also_alone else [slot]
    gs = pltpu.PrefetchScalarGridSpec(
        num_scalar_prefetch=1, grid=(R // tr,),
        in_specs=[pl.BlockSpec((tr, C), lambda r, kr: (r, 0))], out_specs=[o[1] for o in outs])
    res = pl.pallas_call(body, grid_spec=gs, out_shape=[o[0] for o in outs], name=name,
                         compiler_params=_cparams(("parallel",)))(k_idx, w)
    return res if also_alone else res[0]


def _in_proj(cfg, xn1, w_own, w_in, k_idx):
    L, D = xn1.shape
    S, _, Ns = w_in.shape
    tm, tn = _tile(L, 1024, 16), _tile(Ns, 1024, 128)
    nb = Ns // tn

    def body(k_ref, a_ref, w_ref, *rest):
        rest[-1][...] = jnp.dot(a_ref[...], w_ref[...], preferred_element_type=f32).astype(bf16)

    a_spec = pl.BlockSpec((tm, D), lambda i, j, kr: (i, 0))
    shape = jax.ShapeDtypeStruct((L, S * Ns), bf16)
    own = pltpu.PrefetchScalarGridSpec(
        num_scalar_prefetch=1, grid=(L // tm, nb),
        in_specs=[a_spec, pl.BlockSpec((D, tn), lambda i, j, kr: (0, j))],
        out_specs=pl.BlockSpec((tm, tn), lambda i, j, kr: (i, kr[0] * nb + j)))
    proj = pl.pallas_call(body, grid_spec=own, out_shape=shape, name="in_proj_own",
                          compiler_params=_cparams(("parallel", "parallel")))(k_idx, xn1, w_own)
    shard = lambda j, kr: (kr[0] + 1 + j // nb) % S
    rest = pltpu.PrefetchScalarGridSpec(
        num_scalar_prefetch=1, grid=(L // tm, (S - 1) * nb),
        in_specs=[a_spec, pl.BlockSpec((None, D, tn), lambda i, j, kr: (shard(j, kr), 0, j % nb)), _ANY],
        out_specs=pl.BlockSpec((tm, tn), lambda i, j, kr: (i, shard(j, kr) * nb + j % nb)))
    return pl.pallas_call(body, grid_spec=rest, out_shape=shape, input_output_aliases={3: 0}, name="in_proj",
                          compiler_params=_cparams(("parallel", "parallel")))(k_idx, xn1, w_in, proj)


def _handshake(peers):
    barrier = pltpu.get_barrier_semaphore()
    for peer in peers:
        pl.semaphore_signal(barrier, inc=1, device_id=peer, device_id_type=_MESH)
    pl.semaphore_wait(barrier, len(peers))


def _allgather_weights(name, bufs, collective_id):
    n = len(bufs)
    refs = [jax.new_ref(b, memory_space=pltpu.MemorySpace.HBM) for b in bufs]

    def copy(ref, sems, idx, to):
        return pltpu.make_async_remote_copy(src_ref=ref, dst_ref=ref, send_sem=sems[0].at[idx], recv_sem=sems[1].at[idx],
                                            device_id=to, device_id_type=_MESH)

    def launch(ssem, rsem, qssem, qrsem, fssem, frsem):
        x, y, c = _coords()
        k = 2 * x + y
        nbrs = [(1 - x, y), (x, 1 - y)]
        across = 2 * (1 - x) + (1 - y)
        sibling = (x, y, 1 - c)
        _handshake([sibling] + [(*chip, c) for chip in nbrs])
        started = []

        def start(cp):
            cp.start()
            started.append(cp)

        for w in range(n):
            rh = refs[w].shape[1] // 2
            for j, chip in enumerate(nbrs):
                start(copy(refs[w].at[k, pl.ds(c * rh, rh)], (ssem, rsem), (w, j), (*chip, c)))
        for w in range(n):
            rh = refs[w].shape[1] // 2
            rq = rh // 2
            for j, (ox, oy) in enumerate(nbrs):
                ko = 2 * ox + oy
                landed = refs[w].at[ko, pl.ds(c * rh, rh)]
                copy(landed, (ssem, rsem), (w, j), (ox, oy, c)).wait_recv()
                start(copy(refs[w].at[ko, pl.ds(c * rh + j * rq, rq)], (qssem, qrsem), (w, j), (*nbrs[1 - j], c)))
                start(copy(landed, (fssem, frsem), (w, j), sibling))
        for w in range(n):
            rh = refs[w].shape[1] // 2
            rq = rh // 2
            for q in range(2):
                quarter = refs[w].at[across, pl.ds(c * rh + q * rq, rq)]
                copy(quarter, (qssem, qrsem), (w, q), (*nbrs[1 - q], c)).wait_recv()
            start(copy(refs[w].at[across, pl.ds(c * rh, rh)], (fssem, frsem), (w, 2), sibling))
        for w in range(n):
            rh = refs[w].shape[1] // 2
            for j, ko in enumerate([2 * nbrs[0][0] + nbrs[0][1], 2 * nbrs[1][0] + nbrs[1][1], across]):
                copy(refs[w].at[ko, pl.ds((1 - c) * rh, rh)], (fssem, frsem), (w, j), sibling).wait_recv()
        for cp in started:
            cp.wait_send()

    _sequencer_kernel(name, collective_id,
                      (pltpu.SemaphoreType.DMA((n, 2)), pltpu.SemaphoreType.DMA((n, 2)), pltpu.SemaphoreType.DMA((n, 2)),
                       pltpu.SemaphoreType.DMA((n, 2)), pltpu.SemaphoreType.DMA((n, 3)), pltpu.SemaphoreType.DMA((n, 3))),
                      launch)
    return [r[...] for r in refs]


def _sequencer_kernel(name, collective_id, sems, body):
    pl.kernel(body, mesh=plsc.ScalarSubcoreMesh(axis_name="seq", num_cores=1), name=name, scratch_types=sems,
              compiler_params=pltpu.CompilerParams(collective_id=collective_id))()


def _hbm_ref(a):
    return jax.new_ref(a, memory_space=pltpu.MemorySpace.HBM)


def _exchange_halves(name, grads, collective_id):
    n = len(grads)
    srcs = [_hbm_ref(g) for g in grads]
    dsts = [jax.empty_ref(jax.ShapeDtypeStruct((g.shape[0], g.shape[1] // 2, g.shape[2]), g.dtype),
                          memory_space=pltpu.MemorySpace.HBM) for g in grads]

    def body(ssem, rsem):
        x, y, c = _coords()
        _handshake([(x, y, 1 - c)])
        cps = []
        for w in range(n):
            rh = srcs[w].shape[1] // 2
            cp = pltpu.make_async_remote_copy(
                src_ref=srcs[w].at[:, pl.ds((1 - c) * rh, rh)], dst_ref=dsts[w], send_sem=ssem.at[w], recv_sem=rsem.at[w],
                device_id=(x, y, 1 - c), device_id_type=_MESH)
            cp.start()
            cps.append(cp)
        for cp in cps:
            cp.wait()

    _sequencer_kernel(name, collective_id, (pltpu.SemaphoreType.DMA((n,)), pltpu.SemaphoreType.DMA((n,))), body)
    return [d[...] for d in dsts]


def _scatter_shards(name, parts, collective_id):
    n = len(parts)
    srcs = [_hbm_ref(p) for p in parts]
    dsts = [jax.empty_ref(jax.ShapeDtypeStruct((3,) + p.shape[1:], p.dtype), memory_space=pltpu.MemorySpace.HBM)
            for p in parts]

    def body(ssem, rsem):
        x, y, c = _coords()
        k = 2 * x + y
        others = _other_chips(x, y)
        _handshake([(*chip, c) for chip in others])
        cps = []
        for w in range(n):
            for j, (ox, oy) in enumerate(others):
                cp = pltpu.make_async_remote_copy(
                    src_ref=srcs[w].at[(2 * ox + oy - k + 3) % 4], dst_ref=dsts[w].at[j], send_sem=ssem.at[w, j],
                    recv_sem=rsem.at[w, j],
                    device_id=(ox, oy, c), device_id_type=_MESH)
                cp.start()
                cps.append(cp)
        for cp in cps:
            cp.wait()

    _sequencer_kernel(name, collective_id, (pltpu.SemaphoreType.DMA((n, 3)), pltpu.SemaphoreType.DMA((n, 3))), body)
    return [d[...] for d in dsts]


def _join_halves(name, bufs, collective_id):
    n = len(bufs)
    refs = [_hbm_ref(b) for b in bufs]

    def body(ssem, rsem):
        x, y, c = _coords()
        _handshake([(x, y, 1 - c)])
        cps = []
        for w in range(n):
            rh = refs[w].shape[0] // 2
            mine = refs[w].at[pl.ds(c * rh, rh)]
            cp = pltpu.make_async_remote_copy(src_ref=mine, dst_ref=mine, send_sem=ssem.at[w], recv_sem=rsem.at[w],
                                              device_id=(x, y, 1 - c), device_id_type=_MESH)
            cp.start()
            cps.append(cp)
        for w in range(n):
            rh = refs[w].shape[0] // 2
            theirs = refs[w].at[pl.ds((1 - c) * rh, rh)]
            pltpu.make_async_remote_copy(src_ref=theirs, dst_ref=theirs, send_sem=ssem.at[w], recv_sem=rsem.at[w],
                                         device_id=(x, y, 1 - c), device_id_type=_MESH).wait_recv()
        for cp in cps:
            cp.wait_send()

    _sequencer_kernel(name, collective_id, (pltpu.SemaphoreType.DMA((n,)), pltpu.SemaphoreType.DMA((n,))), body)
    return [r[...] for r in refs]


def _allgather_small(name, v, collective_id):
    R = v.shape[0]
    src = _hbm_ref(v)
    dst = jax.empty_ref(jax.ShapeDtypeStruct((8, R, 128), v.dtype), memory_space=pltpu.MemorySpace.HBM)
    flips = [(dx, dy, dc) for dx in (0, 1) for dy in (0, 1) for dc in (0, 1)][1:]

    def body(lsem, ssem, rsem):
        x, y, c = _coords()
        me = 4 * x + 2 * y + c
        flip = lambda v, d: 1 - v if d else v
        peers = [(flip(x, dx), flip(y, dy), flip(c, dc)) for dx, dy, dc in flips]
        _handshake(peers)
        own = pltpu.make_async_copy(src, dst.at[me], lsem)
        own.start()
        cps = []
        for r, peer in enumerate(peers):
            cp = pltpu.make_async_remote_copy(src_ref=src, dst_ref=dst.at[me], send_sem=ssem.at[r], recv_sem=rsem.at[r],
                                              device_id=peer, device_id_type=_MESH)
            cp.start()
            cps.append(cp)
        for r, (px, py, pc) in enumerate(peers):
            theirs = dst.at[4 * px + 2 * py + pc]
            pltpu.make_async_remote_copy(src_ref=theirs, dst_ref=theirs, send_sem=ssem.at[r], recv_sem=rsem.at[r],
                                         device_id=(px, py, pc), device_id_type=_MESH).wait_recv()
        for cp in cps:
            cp.wait_send()
        own.wait()

    _sequencer_kernel(name, collective_id, (pltpu.SemaphoreType.DMA, pltpu.SemaphoreType.DMA((7,)),
                                            pltpu.SemaphoreType.DMA((7,))), body)
    return dst[...]


def _sum8(name, g8, after):
    R = g8.shape[1]
    tr = _tile(R, 512, 8)

    def body(g_ref, *rest):
        rest[-1][...] = (((g_ref[0] + g_ref[1]) + (g_ref[2] + g_ref[3]))
                         + ((g_ref[4] + g_ref[5]) + (g_ref[6] + g_ref[7])))

    return pl.pallas_call(body, grid=(R // tr,),
                          in_specs=[pl.BlockSpec((8, tr, 128), lambda i: (0, i, 0))] + [_ANY] * len(after),
                          out_specs=pl.BlockSpec((tr, 128), lambda i: (i, 0)), out_shape=jax.ShapeDtypeStruct((R, 128), f32),
                          name=name, compiler_params=_cparams(("parallel",)))(g8, *after)


def _add_own_half(name, g, t, kc_idx, after):
    S, R, C = g.shape
    rh = R // 2
    tr = _tile(rh, 512, 16)
    nrb = rh // tr
    shard = lambda s, kc: (kc[0] + 1 + s) % S

    def body(kc_ref, g_ref, t_ref, *rest):
        rest[-1][...] = (g_ref[...] + t_ref[...]).astype(bf16)

    gs = pltpu.PrefetchScalarGridSpec(
        num_scalar_prefetch=1, grid=(S - 1, nrb),
        in_specs=[pl.BlockSpec((None, tr, C), lambda s, r, kc: (shard(s, kc), kc[1] * nrb + r, 0)),
                  pl.BlockSpec((None, tr, C), lambda s, r, kc: (shard(s, kc), r, 0))] + [_ANY] * len(after),
        out_specs=pl.BlockSpec((None, tr, C), lambda s, r, kc: (s, r, 0)))
    return pl.pallas_call(body, grid_spec=gs, out_shape=jax.ShapeDtypeStruct((S - 1, rh, C), bf16), name=name,
                          compiler_params=_cparams(("parallel", "parallel")))(kc_idx, g, t, *after)


def _add_shard_parts(name, g, t, r, kc_idx, after):
    S, R, C = g.shape
    rh = R // 2
    tr = _tile(rh, 256, 16)
    nrb = rh // tr

    def body(kc_ref, g_ref, t_ref, r_ref, *rest):
        own = g_ref[...] + t_ref[...]
        rest[-1][...] = (own + r_ref[1].astype(f32)) + (r_ref[0].astype(f32) + r_ref[2].astype(f32))

    gs = pltpu.PrefetchScalarGridSpec(
        num_scalar_prefetch=1, grid=(nrb,),
        in_specs=[pl.BlockSpec((None, tr, C), lambda i, kc: (kc[0], kc[1] * nrb + i, 0)),
                  pl.BlockSpec((None, tr, C), lambda i, kc: (kc[0], i, 0)),
                  pl.BlockSpec((3, tr, C), lambda i, kc: (0, i, 0))] + [_ANY] * len(after),
        out_specs=pl.BlockSpec((tr, C), lambda i, kc: (kc[1] * nrb + i, 0)))
    return pl.pallas_call(body, grid_spec=gs, out_shape=jax.ShapeDtypeStruct((R, C), f32), name=name,
                          compiler_params=_cparams(("parallel",)))(kc_idx, g, t, r, *after)


def _adamw_update(wv, gv, mv, vv):
    nm = _ADAM_B1 * mv + (1.0 - _ADAM_B1) * gv
    nv = _ADAM_B2 * vv + (1.0 - _ADAM_B2) * (gv * gv)
    m_hat = nm / (1.0 - _ADAM_B1 ** _ADAM_STEP)
    v_hat = nv / (1.0 - _ADAM_B2 ** _ADAM_STEP)
    return -_ADAM_LR * (m_hat / (jnp.sqrt(v_hat) + _ADAM_EPS) + _ADAM_WD * wv), nm, nv


def _adamw(name, w, g, m, v, after=()):
    R, C = w.shape
    tr = _tile(R, 256, 8)

    def body(w_ref, g_ref, m_ref, v_ref, *rest):
        go_ref, d_ref, nm_ref, nv_ref = rest[len(after):]
        gv = g_ref[...]
        go_ref[...] = gv
        d_ref[...], nm_ref[...], nv_ref[...] = _adamw_update(w_ref[...], gv, m_ref[...], v_ref[...])

    spec = pl.BlockSpec((tr, C), lambda i: (i, 0))
    shape = jax.ShapeDtypeStruct((R, C), f32)
    return pl.pallas_call(body, grid=(R // tr,), in_specs=[spec] * 4 + [_ANY] * len(after), out_specs=[spec] * 4,
                          out_shape=[shape] * 4, name=name, compiler_params=_cparams(("parallel",)))(w, g, m, v, *after)


def _adamw_whole(name, w, g, m, v):
    def body(w_ref, g_ref, m_ref, v_ref, d_ref, nm_ref, nv_ref):
        d_ref[...], nm_ref[...], nv_ref[...] = _adamw_update(w_ref[...], g_ref[...], m_ref[...], v_ref[...])

    vm = pl.BlockSpec(memory_space=pltpu.VMEM)
    return pl.pallas_call(body, in_specs=[vm] * 4, out_specs=[vm] * 3, out_shape=[jax.ShapeDtypeStruct(w.shape, f32)] * 3,
                          name=name, compiler_params=pltpu.CompilerParams(vmem_limit_bytes=_VMEM_LIMIT))(w, g, m, v)


def _pack(arrs):
    flat = jnp.concatenate([a.reshape(-1).astype(f32) for a in arrs])
    n = flat.shape[0]
    pad = (-n) % (128 * 128)
    return jnp.pad(flat, (0, pad)).reshape(-1, 128)


def _unpack(packed, shapes):
    flat = packed.reshape(-1)
    out, off = [], 0
    for s in shapes:
        n = math.prod(s)
        out.append(flat[off:off + n].reshape(s))
        off += n
    return out


_BIG = ("w_in", "w_glu", "w_ssm_out", "w_conv_out", "w_o", "w_up", "w_down")
_SMALL = ("norm_tok", "a_re", "a_im", "log_dt", "b_re", "b_im", "c_re", "c_im", "d_skip", "conv_w", "conv_b",
          "norm_ffn", "ffn_conv_w", "ffn_conv_b", "norm_final")
_WEIGHTS = ("norm_tok", "w_in", "a_re", "a_im", "log_dt", "b_re", "b_im", "c_re", "c_im", "d_skip", "w_glu",
            "w_ssm_out", "conv_w", "conv_b", "w_conv_out", "w_o", "norm_ffn", "w_up", "ffn_conv_w", "ffn_conv_b",
            "w_down", "norm_final")


def _step(cfg, x, tgt, p, m, v):
    L, D, SW, CW, F = cfg.L, cfg.D, cfg.SW, cfg.CW, cfg.F
    xi, yi, ci = _coords()
    k_idx = (2 * xi + yi).astype(jnp.int32).reshape(1)
    c_idx = ci.astype(jnp.int32).reshape(1)
    x = x.reshape(L, D)
    tgt = tgt.reshape(L, D)

    big2d = {n: p[n].reshape(p[n].shape[-2], p[n].shape[-1]) for n in _BIG}
    slots = {n: _cast_into_slot("cast_" + n, big2d[n], k_idx) for n in _BIG if n != "w_in"}
    slots["w_in"], w_in_own = _cast_into_slot("cast_w_in", big2d["w_in"], k_idx, also_alone=True)
    wg = {}
    for cid, (gname, group) in enumerate((("allgather_w_in", ("w_in",)),
                                          ("allgather_w_mixer", ("w_glu", "w_ssm_out", "w_conv_out", "w_o")),
                                          ("allgather_w_up", ("w_up",)), ("allgather_w_down", ("w_down",)))):
        wg.update(zip(group, _allgather_weights(gname, [slots[n] for n in group], cid)))
    w_in, w_so, w_co, w_up = wg["w_in"], wg["w_ssm_out"], wg["w_conv_out"], wg["w_up"]
    w_glu, w_o, w_down = wg["w_glu"], wg["w_o"], wg["w_down"]
    kk = k_idx[0]
    cw_full = lax.dynamic_update_slice(jnp.zeros((3, CW), f32), p["conv_w"].reshape(3, CW // _NCHIP), (0, kk * (CW // _NCHIP)))
    fw_full = lax.dynamic_update_slice(jnp.zeros((3, F), f32), p["ffn_conv_w"].reshape(3, F // _NCHIP), (0, kk * (F // _NCHIP)))
    south = (ci == 0).astype(f32)
    filters8 = _allgather_small("allgather_conv_filters", _pack([cw_full * south, fw_full * south]), 17)
    conv_b = p["conv_b"].reshape(1, CW)
    ffn_conv_b = p["ffn_conv_b"].reshape(1, F)
    norm_tok = p["norm_tok"].reshape(1, D)
    norm_ffn = p["norm_ffn"].reshape(1, D)
    norm_final = p["norm_final"].reshape(1, D)
    dskip = p["d_skip"].reshape(1, SW)

    s5_in = (p["a_re"][0], p["a_im"][0], p["log_dt"][0], p["b_re"][0], p["b_im"][0])
    (abr, abi, bbr, bbi), disc_vjp = jax.vjp(_discretize, *s5_in)
    tabs = _s5_tables(cfg, *s5_in[:3], abr, abi, bbr, bbi, p["c_re"][0], p["c_im"][0])

    xn1, r1 = _rms_fwd("rms_tok", x, norm_tok)
    proj = _in_proj(cfg, xn1, w_in_own, w_in, k_idx)
    conv_w, ffn_conv_w = _unpack(_sum8("sum_conv_filters", filters8, [proj]), [(3, CW), (3, F)])
    y_s, ya0, cin, xs = _s5_fwd(cfg, proj, tabs, dskip)

    def tiles(*arrs):
        return lambda tm, tn: [(a, pl.BlockSpec((tm, tn), lambda i, j: (i, j))) for a in arrs]

    def glu_epi(acc, e, o):
        o[0][...] = (e[0][...].astype(f32) * jax.nn.sigmoid(acc)).astype(bf16)
        o[1][...] = acc.astype(bf16)

    ya1, z = _mm_nn("glu", ya0, w_glu, [bf16, bf16], rows=True, extras_fn=tiles(ya0), epilogue=glu_epi)
    yb0 = _convb_fwd(cfg, proj, conv_w, conv_b)
    merged, ya, yb = _merge_fwd(cfg, ya1, yb0, proj, w_so, w_co)

    def res_epi(acc, e, o):
        o[0][...] = e[0][...] + acc

    h1 = _mm_nn("out_proj", merged, w_o, [f32], tm=512, tn=D, rows=True, extras_fn=tiles(x), epilogue=res_epi)[0]
    xn2, r2 = _rms_fwd("rms_ffn", h1, norm_ffn)
    hh = _mm_nn("ffn_up", xn2, w_up, [bf16], tn=2816)[0]
    fact, ffn_pre = _ffn_act(cfg, hh, ffn_conv_w, ffn_conv_b)
    h2 = _mm_nn("ffn_down", fact, w_down, [f32], tm=512, rows=True, extras_fn=tiles(h1), epilogue=res_epi)[0]
    dh2, dh2b, g_norm_final, loss_tile = _loss_head("loss_head", h2, tgt, norm_final)

    kc_idx = jnp.concatenate([k_idx, c_idx])
    reduced, chains = {}, {}

    def rs_halves(tag, collective_id, names, gs):
        chains[tag] = dict(cid=collective_id, names=names, gs=gs,
                           sib=_exchange_halves("grad_halves_" + tag, gs, collective_id))

    def rs_shards(tag, after):
        ch = chains[tag]
        ch["parts"] = [_add_own_half("grad_add_halves_" + n, g, t, kc_idx, after)
                       for n, g, t in zip(ch["names"], ch["gs"], ch["sib"])]
        ch["chips"] = _scatter_shards("grad_shards_" + tag, ch["parts"], ch["cid"] + 1)
        return ch["parts"]

    def rs_join(tag, after):
        ch = chains[tag]
        ch["halves"] = [_add_shard_parts("grad_add_chips_" + n, g, t, r, kc_idx, after)
                        for n, g, t, r in zip(ch["names"], ch["gs"], ch["sib"], ch["chips"])]
        reduced.update(zip(ch["names"], _join_halves("grad_join_" + tag, ch["halves"], ch["cid"] + 2)))
        return ch["halves"]

    g_w_down = _mm_tn("ffn_down_dw", fact, dh2b, tm=1408, tn=512)
    rs_halves("ffn_down", 4, ["w_down"], [g_w_down.reshape(_NCHIP, F // _NCHIP, D)])
    dhh, g_ffn_conv_w, g_ffn_conv_b = _ffn_down_dx_act_bwd(cfg, dh2b, w_down, hh, ffn_pre, ffn_conv_w)
    sent = rs_shards("ffn_down", [dhh])
    g_w_up = _mm_tn("ffn_up_dw", xn2, dhh, shards=_NCHIP, tm=512, tn=1408, b_resident=True, after=sent)
    rs_halves("ffn_up", 7, ["w_up"], [g_w_up])
    dxn2 = _ffn_up_dx(cfg, dhh, w_up)
    sent = rs_shards("ffn_up", [dxn2]) + rs_join("ffn_down", [dxn2])
    dh1, dh1b, g_norm_ffn = _rms_bwd("rms_ffn_bwd", dxn2, h1, r2, norm_ffn, dh2, after=sent)

    dya, dyb, dproj = _out_proj_dx(cfg, dh1b, w_o, ya, yb, proj)
    g_w_o = _mm_tn("out_proj_dw", merged, dh1b)

    def glu_bwd_epi(acc, e, o):
        a0 = e[0][...].astype(f32)
        s = jax.nn.sigmoid(e[1][...].astype(f32))
        o[0][...] = (acc * a0 * s * (1.0 - s)).astype(bf16)
        o[1][...] = (acc * s).astype(bf16)

    dz, t1 = _mm_nt("ssm_out_dx", dya, w_so, [bf16, bf16], extras_fn=tiles(ya0, z), epilogue=glu_bwd_epi)
    g_w_so = _mm_tn("ssm_out_dw", ya1, dya, shards=_NCHIP, tn=512)
    dyb0 = _mm_nt("conv_out_dx", dyb, w_co, [bf16])[0]
    g_w_co = _mm_tn("conv_out_dw", yb0, dyb, shards=_NCHIP, tn=512)
    dproj, g_conv_w, g_conv_b = _convb_bwd(cfg, proj, dyb0, conv_w, conv_b, dproj)

    def gelu_bwd_epi(acc, e, o):
        o[0][...] = ((e[0][...].astype(f32) + acc) * _gelu_grad(e[1][...].astype(f32))).astype(bf16)

    dy_s = _mm_nt("glu_dx", dz, w_glu, [bf16], rows=True, extras_fn=tiles(t1, y_s), epilogue=gelu_bwd_epi)[0]
    g_w_glu = _mm_tn("glu_dw", ya0, dz)
    rs_halves("mixer", 10, ["w_o", "w_ssm_out", "w_conv_out", "w_glu"],
              [g_w_o.reshape(_NCHIP, D // _NCHIP, D), g_w_so, g_w_co, g_w_glu.reshape(_NCHIP, SW // _NCHIP, SW)])
    sent = rs_join("ffn_up", [g_w_glu])
    dproj, da_acc, db_full, dc_full, g_dskip = _s5_bwd(cfg, proj, dy_s, cin, xs, tabs, dskip, dproj, after=sent)
    sent = rs_shards("mixer", [dproj])

    dabr, dabi, dbbr, dbbi, g_c_re, g_c_im = _s5_param_grads(cfg, da_acc, db_full, dc_full)
    g_a_re, g_a_im, g_log_dt, g_b_re, g_b_im = disc_vjp((dabr, dabi, dbbr, dbbi))
    small_g = {"a_re": g_a_re, "a_im": g_a_im, "log_dt": g_log_dt, "b_re": g_b_re, "b_im": g_b_im,
               "c_re": g_c_re, "c_im": g_c_im, "d_skip": g_dskip, "conv_w": g_conv_w, "conv_b": g_conv_b,
               "norm_ffn": g_norm_ffn, "ffn_conv_w": g_ffn_conv_w, "ffn_conv_b": g_ffn_conv_b, "norm_final": g_norm_final}
    early = [n for n in _SMALL if n != "norm_tok"]
    small8 = _allgather_small("allgather_small_grads", _pack([small_g[n] for n in early]), 16)

    g_w_in = _mm_tn("in_proj_dw", xn1, dproj, shards=_NCHIP, tn=CW, after=sent,
                    b_block=lambda j: jnp.where(j == 0, 3 * CW // SW, jnp.where(j < 4, j - 1, j)))
    rs_halves("in_proj", 13, ["w_in"], [g_w_in])
    dxn1 = _in_proj_dx(cfg, dproj, w_in)
    sent = rs_shards("in_proj", [dxn1]) + rs_join("mixer", [dxn1])
    dx, _, g_norm_tok = _rms_bwd("rms_tok_bwd", dxn1, x, r1, norm_tok, dh1, after=sent)

    summed = dict(zip(early, _unpack(_sum8("sum_small_grads", small8, [dx]), [small_g[n].shape for n in early])))
    summed["norm_tok"] = _unpack(_allreduce8("allreduce_norm_tok", _pack([g_norm_tok])), [g_norm_tok.shape])[0]
    summed["conv_w"] = lax.dynamic_slice(summed["conv_w"], (0, kk * (CW // _NCHIP)), (3, CW // _NCHIP))
    summed["ffn_conv_w"] = lax.dynamic_slice(summed["ffn_conv_w"], (0, kk * (F // _NCHIP)), (3, F // _NCHIP))

    grads, deltas, new_m, new_v = {}, {}, {}, {}

    def adamw_big(names, after):
        for n in names:
            g_, d_, m_, v_ = _adamw("adamw_" + n, big2d[n], reduced[n], m[n].reshape(big2d[n].shape),
                                    v[n].reshape(big2d[n].shape), after=after)
            grads[n], deltas[n], new_m[n], new_v[n] = (a.reshape(p[n].shape) for a in (g_, d_, m_, v_))
            after = [d_]
        return after

    for n in _SMALL:
        grads[n] = summed[n].reshape(p[n].shape)
        deltas[n], new_m[n], new_v[n] = _adamw_whole("adamw_" + n, p[n], grads[n], m[n], v[n])
    done = adamw_big(["w_down", "w_up", "w_o", "w_ssm_out", "w_conv_out", "w_glu"], [deltas["norm_final"]])
    rs_join("in_proj", done + [deltas[n] for n in _SMALL])
    adamw_big(["w_in"], ())

    loss = lax.psum(loss_tile[0, 0], ("x", "y", "c"))
    return (loss, dx.reshape(1, L, D), *[grads[n] for n in _WEIGHTS], *[deltas[n] for n in _WEIGHTS],
            *[new_m[n] for n in _WEIGHTS], *[new_v[n] for n in _WEIGHTS])


def kernel(x, norm_tok, w_in, a_re, a_im, log_dt, b_re, b_im, c_re, c_im, d_skip, w_glu, w_ssm_out, conv_w, conv_b, w_conv_out, w_o, norm_ffn, w_up, ffn_conv_w, ffn_conv_b, w_down, norm_final, loss_target, m_norm_tok, m_w_in, m_a_re, m_a_im, m_log_dt, m_b_re, m_b_im, m_c_re, m_c_im, m_d_skip, m_w_glu, m_w_ssm_out, m_conv_w, m_conv_b, m_w_conv_out, m_w_o, m_norm_ffn, m_w_up, m_ffn_conv_w, m_ffn_conv_b, m_w_down, m_norm_final, v_norm_tok, v_w_in, v_a_re, v_a_im, v_log_dt, v_b_re, v_b_im, v_c_re, v_c_im, v_d_skip, v_w_glu, v_w_ssm_out, v_conv_w, v_conv_b, v_w_conv_out, v_w_o, v_norm_ffn, v_w_up, v_ffn_conv_w, v_ffn_conv_b, v_w_down, v_norm_final):
    p = dict(norm_tok=norm_tok, w_in=w_in, a_re=a_re, a_im=a_im, log_dt=log_dt, b_re=b_re, b_im=b_im, c_re=c_re,
             c_im=c_im, d_skip=d_skip, w_glu=w_glu, w_ssm_out=w_ssm_out, conv_w=conv_w, conv_b=conv_b,
             w_conv_out=w_conv_out, w_o=w_o, norm_ffn=norm_ffn, w_up=w_up, ffn_conv_w=ffn_conv_w,
             ffn_conv_b=ffn_conv_b, w_down=w_down, norm_final=norm_final)
    m = dict(norm_tok=m_norm_tok, w_in=m_w_in, a_re=m_a_re, a_im=m_a_im, log_dt=m_log_dt, b_re=m_b_re, b_im=m_b_im,
             c_re=m_c_re, c_im=m_c_im, d_skip=m_d_skip, w_glu=m_w_glu, w_ssm_out=m_w_ssm_out, conv_w=m_conv_w,
             conv_b=m_conv_b, w_conv_out=m_w_conv_out, w_o=m_w_o, norm_ffn=m_norm_ffn, w_up=m_w_up,
             ffn_conv_w=m_ffn_conv_w, ffn_conv_b=m_ffn_conv_b, w_down=m_w_down, norm_final=m_norm_final)
    v = dict(norm_tok=v_norm_tok, w_in=v_w_in, a_re=v_a_re, a_im=v_a_im, log_dt=v_log_dt, b_re=v_b_re, b_im=v_b_im,
             c_re=v_c_re, c_im=v_c_im, d_skip=v_d_skip, w_glu=v_w_glu, w_ssm_out=v_w_ssm_out, conv_w=v_conv_w,
             conv_b=v_conv_b, w_conv_out=v_w_conv_out, w_o=v_w_o, norm_ffn=v_norm_ffn, w_up=v_w_up,
             ffn_conv_w=v_ffn_conv_w, ffn_conv_b=v_ffn_conv_b, w_down=v_w_down, norm_final=v_norm_final)
    return _step(_Cfg(), x, loss_target, p, m, v)
```

```python
import math
from typing import NamedTuple

import jax
import jax.numpy as jnp
from jax import lax
from jax.experimental import pallas as pl
from jax.experimental.pallas import tpu as pltpu
from jax.experimental.pallas import tpu_sc as plsc

f32 = jnp.float32
bf16 = jnp.bfloat16
_MESH = pl.DeviceIdType.MESH

_EPS = 1e-6
_ADAM_LR = 0.001
_ADAM_B1 = 0.9
_ADAM_B2 = 0.999
_ADAM_EPS = 1e-08
_ADAM_WD = 0.01
_ADAM_STEP = 10
_SSM_GROUP = 16
_SSM_STATE = 64
_SLAB_GROUPS = 16
_NCHIP = 4
_VMEM_LIMIT = 56 * 2**20
_VMEM_S5_BWD = 62 * 2**20
_GELU_C = math.sqrt(2.0 / math.pi)
_GELU_A = 0.044715


class _Cfg(NamedTuple):
    L: int = 4096
    D: int = 2048
    SW: int = 1024
    CW: int = 1024
    F: int = 5632
    T: int = 256


def _tile(n, pref, align):
    t = min(n, pref)
    t -= t % align
    while t > align and n % t:
        t -= align
    assert t > 0 and n % t == 0, (n, pref, align)
    return t


def _cparams(sem, vmem_limit=_VMEM_LIMIT):
    return pltpu.CompilerParams(dimension_semantics=sem, vmem_limit_bytes=vmem_limit)


def _gelu(x):
    return _gelu_and_grad(x)[0]


def _gelu_grad(x):
    return _gelu_and_grad(x)[1]


def _gelu_and_grad(x):
    x2 = x * x
    th = jnp.tanh(x * (_GELU_C + (_GELU_C * _GELU_A) * x2))
    half = 0.5 + 0.5 * th
    return x * half, half + (0.5 * x) * (1.0 - th * th) * (_GELU_C + (3.0 * _GELU_C * _GELU_A) * x2)


_NN = (((1,), (0,)), ((), ()))
_NT = (((1,), (1,)), ((), ()))
_TN = (((0,), (0,)), ((), ()))


def _whole(ref):
    return ref[...]


_ANY = pl.BlockSpec(memory_space=pl.ANY)


def _mm(name, operands, steps, *, grid, contract, outs, extras=(), epilogue=None, acc_shape=None, after=()):
    nop, ne, na = len(operands), len(extras), len(after)
    nk = len(steps)

    def body(*refs):
        op_refs = refs[:nop]
        e_refs = refs[nop:nop + ne]
        o_refs = refs[nop + ne + na:nop + ne + na + len(outs)]

        def partial(terms):
            tot = None
            for ai, av, bi, bv in terms:
                d = lax.dot_general(av(op_refs[ai]), bv(op_refs[bi]), contract, preferred_element_type=f32)
                tot = d if tot is None else tot + d
            return tot

        def finish(res):
            if epilogue is None:
                for o in o_refs:
                    o[...] = res.astype(o.dtype)
            else:
                epilogue(res, e_refs, o_refs)

        if nk == 1:
            finish(partial(steps[0][1]))
            return
        acc = refs[-1]
        kid = pl.program_id(len(grid) - 1)
        for k, terms in steps:
            def run(k=k, terms=terms):
                d = partial(terms)
                if k == 0:
                    acc[...] = d
                elif k < nk - 1:
                    acc[...] += d
                else:
                    finish(acc[...] + d)

            pl.when(kid == k)(run)

    sem = ("parallel",) * (len(grid) - (nk > 1)) + (("arbitrary",) if nk > 1 else ())
    return pl.pallas_call(
        body, grid=grid, in_specs=[o[1] for o in operands] + [e[1] for e in extras] + [_ANY] * na,
        out_specs=[o[1] for o in outs], out_shape=[o[0] for o in outs],
        scratch_shapes=[pltpu.VMEM(acc_shape, f32)] if nk > 1 else [], name=name,
        compiler_params=_cparams(sem))(*[o[0] for o in operands], *[e[0] for e in extras], *after)


def _mm_nn(name, a, w, out_dtypes, *, tm=1024, tn=1024, rows=False, extras_fn=None, epilogue=None):
    M, K = a.shape
    S, Ns = w.shape[0], w.shape[-1]
    N = Ns if rows else Ns * S
    tm, tn = _tile(M, tm, 16), _tile(Ns, tn, 128)
    nb = Ns // tn
    a_spec = pl.BlockSpec((tm, K), lambda i, j: (i, 0))
    if rows:
        b_spec = pl.BlockSpec((S, K // S, tn), lambda i, j: (0, 0, j))
        b_view = lambda r: r[...].reshape(K, tn)
    else:
        b_spec = pl.BlockSpec((None, K, tn), lambda i, j: (j // nb, 0, j % nb))
        b_view = _whole
    o_spec = pl.BlockSpec((tm, tn), lambda i, j: (i, j))
    outs = [(jax.ShapeDtypeStruct((M, N), dt), o_spec) for dt in out_dtypes]
    extras = extras_fn(tm, tn) if extras_fn is not None else ()
    return _mm(name, [(a, a_spec), (w, b_spec)], [(None, [(0, _whole, 1, b_view)])], grid=(M // tm, N // tn),
               contract=_NN, outs=outs, extras=extras, epilogue=epilogue)


def _mm_nt(name, a, w, out_dtypes, *, tm=1024, tn=1024, rows=False, extras_fn=None, epilogue=None):
    M, N = a.shape
    S, Ks, Ns = w.shape
    K = Ks * S if rows else Ks
    tm = _tile(M, tm, 16)
    a_spec = pl.BlockSpec((tm, N), lambda i, j: (i, 0))
    if rows:
        whole_shards = tn > Ks and tn % Ks == 0 and K % tn == 0
        tn = K if tn >= K else (tn if whole_shards else _tile(Ks, tn, 128))
        if tn == K or whole_shards:
            b_spec = pl.BlockSpec((tn // Ks, Ks, N), lambda i, j: (j, 0, 0))
            terms = [(0, _whole, 1, lambda r: r[...].reshape(tn, N))]
        else:
            nbs = Ks // tn
            b_spec = pl.BlockSpec((None, tn, N), lambda i, j: (j // nbs, j % nbs, 0))
            terms = [(0, _whole, 1, _whole)]
    else:
        tn = _tile(K, tn, 128)
        assert S * Ns == N
        b_spec = pl.BlockSpec((S, tn, Ns), lambda i, j: (0, j, 0))
        terms = [(0, lambda r, s=s: r[:, s * Ns:(s + 1) * Ns], 1, lambda r, s=s: r[s]) for s in range(S)]
    o_spec = pl.BlockSpec((tm, tn), lambda i, j: (i, j))
    outs = [(jax.ShapeDtypeStruct((M, K), dt), o_spec) for dt in out_dtypes]
    extras = extras_fn(tm, tn) if extras_fn is not None else ()
    return _mm(name, [(a, a_spec), (w, b_spec)], [(None, terms)], grid=(M // tm, K // tn), contract=_NT,
               outs=outs, extras=extras, epilogue=epilogue)


def _mm_tn(name, a, b, *, shards=None, tm=1024, tn=1024, b_block=None, b_resident=False, after=()):
    M, K = a.shape
    halves = b.shape[0] if b.ndim == 3 else 1
    Nh = b.shape[-1]
    N = Nh * halves
    Ns = N // shards if shards else N
    tm, tn = _tile(K, tm, 128), _tile(math.gcd(Ns, Nh), tn, 128)
    nb, nbh = Ns // tn, Nh // tn
    ij = (lambda g0, g1: (g1, g0)) if b_resident else (lambda g0, g1: (g0, g1))
    bmap = b_block if b_block is not None else (lambda j: j)
    a_spec = pl.BlockSpec((M, tm), lambda g0, g1: (0, ij(g0, g1)[0]))
    if halves > 1:
        b_spec = pl.BlockSpec((None, M, tn), lambda g0, g1: (bmap(ij(g0, g1)[1]) // nbh, 0, bmap(ij(g0, g1)[1]) % nbh))
    else:
        b_spec = pl.BlockSpec((M, tn), lambda g0, g1: (0, bmap(ij(g0, g1)[1])))
    if shards:
        out = (jax.ShapeDtypeStruct((shards, K, Ns), f32),
               pl.BlockSpec((None, tm, tn), lambda g0, g1: (ij(g0, g1)[1] // nb, ij(g0, g1)[0], ij(g0, g1)[1] % nb)))
    else:
        out = (jax.ShapeDtypeStruct((K, N), f32), pl.BlockSpec((tm, tn), lambda g0, g1: ij(g0, g1)))
    grid = (N // tn, K // tm) if b_resident else (K // tm, N // tn)
    return _mm(name, [(a, a_spec), (b, b_spec)], [(None, [(0, _whole, 1, _whole)])], grid=grid, contract=_TN,
               outs=[out], after=after)[0]


def _in_proj_dx(cfg, dproj, w_in):
    L, D, SW, CW = cfg.L, cfg.D, cfg.SW, cfg.CW
    NP = SW + 3 * CW + 2 * D
    Ns = NP // _NCHIP
    assert SW + CW == Ns and 2 * CW == Ns and D == Ns
    tm, tn = _tile(L, 1024, 16), _tile(D, 1024, 128)
    a_spec = pl.BlockSpec((tm, NP // 2), lambda i, j, k: (i, k))
    b_spec = pl.BlockSpec((2, tn, Ns), lambda i, j, k: (k, j, 0))
    first = [(0, lambda r: r[:, 0:CW], 1, lambda r: r[0, :, SW:SW + CW]),
             (0, lambda r: r[:, CW:3 * CW], 1, lambda r: r[1]),
             (0, lambda r: r[:, 3 * CW:3 * CW + SW], 1, lambda r: r[0, :, 0:SW])]
    second = [(0, lambda r: r[:, 0:D], 1, lambda r: r[0]), (0, lambda r: r[:, D:2 * D], 1, lambda r: r[1])]
    out = (jax.ShapeDtypeStruct((L, D), bf16), pl.BlockSpec((tm, tn), lambda i, j, k: (i, j)))
    return _mm("in_proj_dx", [(dproj, a_spec), (w_in, b_spec)], [(0, first), (1, second)], grid=(L // tm, D // tn, 2),
               contract=_NT, outs=[out], acc_shape=(tm, tn))[0]


def _out_proj_dx(cfg, dh1b, w_o, ya, yb, proj):
    L, D = cfg.L, cfg.D
    NP = cfg.SW + 3 * cfg.CW + 2 * D
    assert NP == 4 * D
    tm = _tile(L, 512, 16)

    def epilogue(acc, e, o):
        sa = jax.nn.sigmoid(e[2][:, 0:D].astype(f32))
        sb = jax.nn.sigmoid(e[2][:, D:2 * D].astype(f32))
        o[0][...] = (acc * sa).astype(bf16)
        o[1][...] = (acc * sb).astype(bf16)
        o[2][:, 0:D] = (acc * e[0][...].astype(f32) * sa * (1.0 - sa)).astype(bf16)
        o[2][:, D:2 * D] = (acc * e[1][...].astype(f32) * sb * (1.0 - sb)).astype(bf16)

    row = pl.BlockSpec((tm, D), lambda i, j: (i, 0))
    half = pl.BlockSpec((tm, 2 * D), lambda i, j: (i, 1))
    return _mm("out_proj_dx", [(dh1b, row), (w_o, pl.BlockSpec(w_o.shape, lambda i, j: (0, 0, 0),
                                                                pipeline_mode=pl.Buffered(1)))],
               [(None, [(0, _whole, 1, lambda r: r[...].reshape(D, D))])], grid=(L // tm, 1), contract=_NT,
               outs=[(jax.ShapeDtypeStruct((L, D), bf16), row), (jax.ShapeDtypeStruct((L, D), bf16), row),
                     (jax.ShapeDtypeStruct((L, NP), bf16), half)],
               extras=[(ya, row), (yb, row), (proj, half)], epilogue=epilogue)


def _ffn_up_dx(cfg, dhh, w_up):
    L, D, F = cfg.L, cfg.D, cfg.F
    Fh = F // 2
    tm, tn = _tile(L, 1024, 16), _tile(D, 512, 128)
    a_spec = pl.BlockSpec((None, tm, F), lambda i, j, k: (k, i, 0))
    b_spec = pl.BlockSpec((2, tn, Fh), lambda i, j, k: (k, j, 0))
    terms = [(0, lambda r: r[:, 0:Fh], 1, lambda r: r[0]), (0, lambda r: r[:, Fh:F], 1, lambda r: r[1])]
    out = (jax.ShapeDtypeStruct((L, D), bf16), pl.BlockSpec((tm, tn), lambda i, j, k: (i, j)))
    return _mm("ffn_up_dx", [(dhh, a_spec), (w_up, b_spec)], [(0, terms), (1, terms)], grid=(L // tm, D // tn, 2),
               contract=_NT, outs=[out], acc_shape=(tm, tn))[0]


def _rms_fwd(name, x, g):
    L, D = x.shape
    tm = _tile(L, 512, 16)

    def body(x_ref, g_ref, xn_ref, r_ref):
        xv = x_ref[...]
        r = lax.rsqrt(jnp.mean(xv * xv, axis=-1, keepdims=True) + _EPS)
        xn_ref[...] = (xv * r * g_ref[...]).astype(bf16)
        r_ref[...] = r

    return pl.pallas_call(
        body, grid=(L // tm,),
        in_specs=[pl.BlockSpec((tm, D), lambda i: (i, 0)), pl.BlockSpec((1, D), lambda i: (0, 0))],
        out_specs=[pl.BlockSpec((tm, D), lambda i: (i, 0)), pl.BlockSpec((tm, 1), lambda i: (i, 0))],
        out_shape=[jax.ShapeDtypeStruct((L, D), bf16), jax.ShapeDtypeStruct((L, 1), f32)],
        name=name, compiler_params=_cparams(("parallel",)))(x, g)


def _rms_bwd(name, dxn, h, r, g, dres, after=()):
    L, D = h.shape
    tm = _tile(L, 512, 16)

    def body(dxn_ref, h_ref, r_ref, g_ref, dres_ref, *rest):
        dh_ref, dhb_ref, dg_ref = rest[len(after):]
        i = pl.program_id(0)
        d = dxn_ref[...].astype(f32)
        hv = h_ref[...]
        rv = r_ref[...]
        dyg = d * g_ref[...]
        m = jnp.mean(dyg * hv, axis=-1, keepdims=True)
        dh = dres_ref[...] + rv * dyg - hv * (rv * rv * rv) * m
        dh_ref[...] = dh
        dhb_ref[...] = dh.astype(bf16)

        @pl.when(i == 0)
        def _():
            dg_ref[...] = jnp.zeros_like(dg_ref)

        dg_ref[...] += jnp.sum(d * hv * rv, axis=0, keepdims=True)

    row = lambda i: (i, 0)
    return pl.pallas_call(
        body, grid=(L // tm,),
        in_specs=[pl.BlockSpec((tm, D), row), pl.BlockSpec((tm, D), row), pl.BlockSpec((tm, 1), row),
                  pl.BlockSpec((1, D), lambda i: (0, 0)), pl.BlockSpec((tm, D), row)] + [_ANY] * len(after),
        out_specs=[pl.BlockSpec((tm, D), row), pl.BlockSpec((tm, D), row), pl.BlockSpec((1, D), lambda i: (0, 0))],
        out_shape=[jax.ShapeDtypeStruct((L, D), f32), jax.ShapeDtypeStruct((L, D), bf16), jax.ShapeDtypeStruct((1, D), f32)],
        name=name, compiler_params=_cparams(("arbitrary",)))(dxn, h, r, g, dres, *after)


def _loss_head(name, h2, tgt, g):
    L, D = h2.shape
    tm = _tile(L, 512, 16)

    def body(h_ref, t_ref, g_ref, dh_ref, dhb_ref, dg_ref, loss_ref):
        i = pl.program_id(0)
        hv = h_ref[...]
        gv = g_ref[...]
        r = lax.rsqrt(jnp.mean(hv * hv, axis=-1, keepdims=True) + _EPS)
        err = hv * r * gv - t_ref[...]
        dy = err * (1.0 / D)
        dyg = dy * gv
        m = jnp.mean(dyg * hv, axis=-1, keepdims=True)
        dh = r * dyg - hv * (r * r * r) * m
        dh_ref[...] = dh
        dhb_ref[...] = dh.astype(bf16)

        @pl.when(i == 0)
        def _():
            dg_ref[...] = jnp.zeros_like(dg_ref)
            loss_ref[...] = jnp.zeros_like(loss_ref)

        dg_ref[...] += jnp.sum(dy * hv * r, axis=0, keepdims=True)
        part = jnp.sum(jnp.sum(err * err, axis=-1, keepdims=True), axis=0, keepdims=True) * (0.5 / D)
        loss_ref[...] += jnp.broadcast_to(part, (8, 128))

    row = lambda i: (i, 0)
    return pl.pallas_call(
        body, grid=(L // tm,),
        in_specs=[pl.BlockSpec((tm, D), row), pl.BlockSpec((tm, D), row), pl.BlockSpec((1, D), lambda i: (0, 0))],
        out_specs=[pl.BlockSpec((tm, D), row), pl.BlockSpec((tm, D), row), pl.BlockSpec((1, D), lambda i: (0, 0)),
                   pl.BlockSpec((8, 128), lambda i: (0, 0))],
        out_shape=[jax.ShapeDtypeStruct((L, D), f32), jax.ShapeDtypeStruct((L, D), bf16),
                   jax.ShapeDtypeStruct((1, D), f32), jax.ShapeDtypeStruct((8, 128), f32)],
        name=name, compiler_params=_cparams(("arbitrary",)))(h2, tgt, g)


def _shift_down(tile, halo, k, rows8):
    tm = tile.shape[0]
    r = pltpu.roll(tile, k, axis=0)
    hh = pltpu.roll(halo, k, axis=0)
    top = jnp.where(rows8 < k, hh, r[:8])
    return jnp.concatenate([top, r[8:]], axis=0) if tm > 8 else top


def _shift_up(tile, halo, k, rows8):
    tm = tile.shape[0]
    r = pltpu.roll(tile, tm - k, axis=0)
    hh = pltpu.roll(halo, 8 - k, axis=0)
    bot = jnp.where(rows8 >= 8 - k, hh, r[tm - 8:])
    return jnp.concatenate([r[:tm - 8], bot], axis=0) if tm > 8 else bot


def _conv3(x, halo, w_ref, b_ref, rows8):
    return (w_ref[0:1, :] * _shift_down(x, halo, 2, rows8) + w_ref[1:2, :] * _shift_down(x, halo, 1, rows8)
            + w_ref[2:3, :] * x + b_ref[...])


def _convb_fwd(cfg, proj, w, b):
    L, CW = cfg.L, cfg.CW
    assert cfg.SW == CW
    tm = _tile(L, 512, 16)

    def body(v_ref, vh_ref, gb_ref, gc_ref, gch_ref, w_ref, b_ref, o_ref):
        i = pl.program_id(0)
        rows8 = lax.broadcasted_iota(jnp.int32, (8, CW), 0)
        cv = gc_ref[...].astype(f32) * v_ref[...].astype(f32)
        cvh = gch_ref[...].astype(f32)[8:] * vh_ref[...].astype(f32)[8:]
        cvh = jnp.where(i == 0, 0.0, cvh)
        cc = _conv3(cv, cvh, w_ref, b_ref, rows8)
        o_ref[...] = (gb_ref[...].astype(f32) * cc).astype(bf16)

    blk = lambda col: pl.BlockSpec((tm, CW), lambda i: (i, col))
    halo = lambda col: pl.BlockSpec((16, CW), lambda i: (jnp.maximum(i * (tm // 16) - 1, 0), col))
    return pl.pallas_call(
        body, grid=(L // tm,),
        in_specs=[blk(1), halo(1), blk(2), blk(3), halo(3),
                  pl.BlockSpec((3, CW), lambda i: (0, 0)), pl.BlockSpec((1, CW), lambda i: (0, 0))],
        out_specs=pl.BlockSpec((tm, CW), lambda i: (i, 0)),
        out_shape=jax.ShapeDtypeStruct((L, CW), bf16),
        name="convb_fwd", compiler_params=_cparams(("parallel",)))(proj, proj, proj, proj, proj, w, b)


def _convb_bwd(cfg, proj, dyb0, w, b, dproj):
    L, CW = cfg.L, cfg.CW
    tm = _tile(L, 512, 16)
    nt = L // tm

    def body(v_ref, vh_ref, gb_ref, gbn_ref, gc_ref, gch_ref, d_ref, dn_ref, w_ref, b_ref, dproj_ref,
             o_ref, dw_ref, db_ref):
        i = pl.program_id(0)
        rows8 = lax.broadcasted_iota(jnp.int32, (8, CW), 0)
        v = v_ref[...].astype(f32)
        gb = gb_ref[...].astype(f32)
        gc = gc_ref[...].astype(f32)
        d = d_ref[...].astype(f32)
        cv = gc * v
        cvh = gch_ref[...].astype(f32)[8:] * vh_ref[...].astype(f32)[8:]
        cvh = jnp.where(i == 0, 0.0, cvh)
        s2 = _shift_down(cv, cvh, 2, rows8)
        s1 = _shift_down(cv, cvh, 1, rows8)
        cc = w_ref[0:1, :] * s2 + w_ref[1:2, :] * s1 + w_ref[2:3, :] * cv + b_ref[...]
        dcc = d * gb
        dccn = dn_ref[...].astype(f32)[:8] * gbn_ref[...].astype(f32)[:8]
        dccn = jnp.where(i == nt - 1, 0.0, dccn)
        dcv = (w_ref[2:3, :] * dcc + w_ref[1:2, :] * _shift_up(dcc, dccn, 1, rows8)
               + w_ref[0:1, :] * _shift_up(dcc, dccn, 2, rows8))
        o_ref[:, 0:CW] = (dcv * gc).astype(bf16)
        o_ref[:, CW:2 * CW] = (d * cc).astype(bf16)
        o_ref[:, 2 * CW:3 * CW] = (dcv * v).astype(bf16)

        @pl.when(i == 0)
        def _():
            dw_ref[...] = jnp.zeros_like(dw_ref)
            db_ref[...] = jnp.zeros_like(db_ref)

        dw_ref[0:1, :] += jnp.sum(dcc * s2, axis=0, keepdims=True)
        dw_ref[1:2, :] += jnp.sum(dcc * s1, axis=0, keepdims=True)
        dw_ref[2:3, :] += jnp.sum(dcc * cv, axis=0, keepdims=True)
        db_ref[...] += jnp.sum(dcc, axis=0, keepdims=True)

    blk = lambda col: pl.BlockSpec((tm, CW), lambda i: (i, col))
    prev = lambda col: pl.BlockSpec((16, CW), lambda i: (jnp.maximum(i * (tm // 16) - 1, 0), col))
    nxt = lambda col: pl.BlockSpec((16, CW), lambda i: (jnp.minimum((i + 1) * (tm // 16), L // 16 - 1), col))
    const = lambda r: pl.BlockSpec((r, CW), lambda i: (0, 0))
    return pl.pallas_call(
        body, grid=(nt,),
        in_specs=[blk(1), prev(1), blk(2), nxt(2), blk(3), prev(3), blk(0), nxt(0), const(3), const(1),
                  pl.BlockSpec(memory_space=pl.ANY)],
        out_specs=[pl.BlockSpec((tm, 3 * CW), lambda i: (i, 0)), const(3), const(1)],
        out_shape=[jax.ShapeDtypeStruct(dproj.shape, bf16), jax.ShapeDtypeStruct((3, CW), f32),
                   jax.ShapeDtypeStruct((1, CW), f32)],
        input_output_aliases={10: 0},
        name="convb_bwd", compiler_params=_cparams(("arbitrary",)))(proj, proj, proj, proj, proj, proj, dyb0, dyb0, w, b,
                                                                    dproj)


def _ffn_act(cfg, hh, w, b):
    L, F = cfg.L, cfg.F
    tm = _tile(L, 512, 16)
    tc = _tile(F, 1408, 128)
    ncb = F // tc

    def body(a_ref, ah_ref, g_ref, w_ref, b_ref, o_ref, act_ref):
        i = pl.program_id(0)
        rows8 = lax.broadcasted_iota(jnp.int32, (8, tc), 0)
        a = a_ref[...].astype(f32)
        ah = jnp.where(i == 0, 0.0, ah_ref[...].astype(f32)[8:])
        act = _conv3(a, ah, w_ref, b_ref, rows8)
        act_ref[...] = act.astype(bf16)
        o_ref[...] = (_gelu(act) * g_ref[...].astype(f32)).astype(bf16)

    tile = pl.BlockSpec((tm, tc), lambda i, j: (i, j))
    return pl.pallas_call(
        body, grid=(L // tm, ncb),
        in_specs=[tile, pl.BlockSpec((16, tc), lambda i, j: (jnp.maximum(i * (tm // 16) - 1, 0), j)),
                  pl.BlockSpec((tm, tc), lambda i, j: (i, j + ncb)),
                  pl.BlockSpec((3, tc), lambda i, j: (0, j)), pl.BlockSpec((1, tc), lambda i, j: (0, j))],
        out_specs=[tile, tile], out_shape=[jax.ShapeDtypeStruct((L, F), bf16)] * 2,
        name="ffn_act", compiler_params=_cparams(("parallel", "parallel")))(hh, hh, hh, w, b)


def _ffn_down_dx_act_bwd(cfg, dh2b, w_down, hh, act, w):
    L, D, F = cfg.L, cfg.D, cfg.F
    S, Ks, _ = w_down.shape
    tm = _tile(L, 512, 16)
    tc = _tile(Ks, 1408, 128)
    ncb = F // tc
    nbs = Ks // tc
    nt = L // tm

    slabs = [(c0, min(256, tc - c0)) for c0 in range(0, tc, 256)]

    def body(dh_ref, wd_ref, a_ref, c_ref, g_ref, w_ref, dhh_ref, dw_ref, db_ref, below):
        i = pl.program_id(1)

        @pl.when(i == 0)
        def _():
            below[...] = jnp.zeros_like(below)
            dw_ref[...] = jnp.zeros_like(dw_ref)
            db_ref[...] = jnp.zeros_like(db_ref)

        dh = dh_ref[...]
        for c0, cw in slabs:
            cols = slice(c0, c0 + cw)
            rows8 = lax.broadcasted_iota(jnp.int32, (8, cw), 0)
            d = lax.dot_general(dh, wd_ref[c0:c0 + cw, :], _NT, preferred_element_type=f32)
            a = a_ref[:, cols].astype(f32)
            gelu, gelu_d = _gelu_and_grad(c_ref[:, cols].astype(f32))
            dhh_ref[1, :, cols] = (d * gelu).astype(bf16)
            dact = d * g_ref[:, cols].astype(f32) * gelu_d
            dactn = below[:, cols]
            up1 = _shift_up(dact, dactn, 1, rows8)
            up2 = _shift_up(dact, dactn, 2, rows8)
            below[:, cols] = dact[:8]
            dhh_ref[0, :, cols] = (w_ref[2:3, cols] * dact + w_ref[1:2, cols] * up1 + w_ref[0:1, cols] * up2).astype(bf16)
            dw_ref[0:1, cols] += jnp.sum(up2 * a, axis=0, keepdims=True)
            dw_ref[1:2, cols] += jnp.sum(up1 * a, axis=0, keepdims=True)
            dw_ref[2:3, cols] += jnp.sum(dact * a, axis=0, keepdims=True)
            db_ref[:, cols] += jnp.sum(dact, axis=0, keepdims=True)

    up = lambda i: nt - 1 - i
    blk = lambda off: pl.BlockSpec((tm, tc), lambda j, i: (up(i), j + off))
    const = lambda r: pl.BlockSpec((r, tc), lambda j, i: (0, j))
    return pl.pallas_call(
        body, grid=(ncb, nt),
        in_specs=[pl.BlockSpec((tm, D), lambda j, i: (up(i), 0)),
                  pl.BlockSpec((None, tc, D), lambda j, i: (j // nbs, j % nbs, 0)),
                  blk(0), blk(0), blk(ncb), const(3)],
        out_specs=[pl.BlockSpec((2, tm, tc), lambda j, i: (0, up(i), j)), const(3), const(1)],
        out_shape=[jax.ShapeDtypeStruct((2, L, F), bf16),
                   jax.ShapeDtypeStruct((3, F), f32), jax.ShapeDtypeStruct((1, F), f32)],
        scratch_shapes=[pltpu.VMEM((8, tc), f32)],
        name="ffn_down_dx_act_bwd", compiler_params=_cparams(("parallel", "arbitrary")))(dh2b, w_down, hh, act, hh, w)


def _merge_fwd(cfg, ya1, yb0, proj, wso, wco):
    L, D, SW, CW = cfg.L, cfg.D, cfg.SW, cfg.CW
    Ns = D // _NCHIP
    tm = _tile(L, 1024, 16)
    tn = _tile(Ns, 512, 128)
    nb = Ns // tn
    off_a = (SW + 3 * CW) // tn
    off_b = (SW + 3 * CW + D) // tn

    def body(a_ref, b_ref, wa_ref, wb_ref, ma_ref, mb_ref, m_ref, ya_ref, yb_ref):
        ya = jnp.dot(a_ref[...], wa_ref[...], preferred_element_type=f32)
        yb = jnp.dot(b_ref[...], wb_ref[...], preferred_element_type=f32)
        sa = jax.nn.sigmoid(ma_ref[...].astype(f32))
        sb = jax.nn.sigmoid(mb_ref[...].astype(f32))
        m_ref[...] = (sa * ya + sb * yb).astype(bf16)
        ya_ref[...] = ya.astype(bf16)
        yb_ref[...] = yb.astype(bf16)

    o_spec = pl.BlockSpec((tm, tn), lambda i, j: (i, j))
    o_shape = jax.ShapeDtypeStruct((L, D), bf16)
    return pl.pallas_call(
        body, grid=(L // tm, D // tn),
        in_specs=[pl.BlockSpec((tm, SW), lambda i, j: (i, 0)), pl.BlockSpec((tm, CW), lambda i, j: (i, 0)),
                  pl.BlockSpec((None, SW, tn), lambda i, j: (j // nb, 0, j % nb)),
                  pl.BlockSpec((None, CW, tn), lambda i, j: (j // nb, 0, j % nb)),
                  pl.BlockSpec((tm, tn), lambda i, j: (i, off_a + j)), pl.BlockSpec((tm, tn), lambda i, j: (i, off_b + j))],
        out_specs=[o_spec, o_spec, o_spec], out_shape=[o_shape, o_shape, o_shape],
        name="merge_fwd", compiler_params=_cparams(("parallel", "parallel")))(ya1, yb0, wso, wco, proj, proj)


def _s5_dims(cfg):
    G = cfg.SW // _SSM_GROUP
    NS = G // _SLAB_GROUPS
    SC = _SLAB_GROUPS * _SSM_GROUP
    SH = _SLAB_GROUPS * _SSM_STATE
    NST = 2 * SH * NS
    return G, NS, SC, SH, NST


def _lane_slabs(cfg, W):
    _, NS, _, SH, _ = _s5_dims(cfg)
    return [(2 * SH * s + w0, 2 * SH * s + SH + w0) for s in range(NS) for w0 in range(0, SH, W)]


def _discretize(a_re, a_im, log_dt, b_re, b_im):
    dt = jnp.exp(log_dt)[:, None]
    mag = jnp.exp(dt * a_re)
    abr = mag * jnp.cos(dt * a_im)
    abi = mag * jnp.sin(dt * a_im)
    nr = abr - 1.0
    ni = abi
    den = a_re * a_re + a_im * a_im
    fr = (nr * a_re + ni * a_im) / den
    fi = (ni * a_re - nr * a_im) / den
    bbr = fr[..., None] * b_re - fi[..., None] * b_im
    bbi = fr[..., None] * b_im + fi[..., None] * b_re
    return abr, abi, bbr, bbi


def _state_rows(cfg, re, im):
    _, NS, _, SH, _ = _s5_dims(cfg)
    return jnp.concatenate([re.reshape(NS, SH), im.reshape(NS, SH)], axis=1).reshape(-1)


def _s5_tables(cfg, a_re, a_im, log_dt, abr, abi, bbr, bbi, c_re, c_im):
    G, NS, SC, SH, NST = _s5_dims(cfg)
    S = cfg.T // 8
    eye = jnp.eye(_SLAB_GROUPS, dtype=bf16)
    bb = jnp.stack([bbr, bbi]).reshape(2, NS, _SLAB_GROUPS, _SSM_STATE, _SSM_GROUP).astype(bf16)
    bs = (bb.transpose(1, 2, 4, 0, 3)[:, :, :, :, None, :] * eye[None, :, None, None, :, None]).reshape(NS, SC, 2 * SH)
    cc = jnp.stack([c_re, -c_im]).reshape(2, NS, _SLAB_GROUPS, _SSM_GROUP, _SSM_STATE).astype(bf16)
    cs = (cc.transpose(1, 0, 4, 2, 3)[:, :, None, :, :, :] * eye[None, None, :, None, :, None]).reshape(NS, 2 * SH, SC)
    arep = jnp.broadcast_to(_state_rows(cfg, abr, abi)[None, :], (8, NST))
    sdt = S * jnp.exp(log_dt)[:, None]
    mag = jnp.exp(sdt * a_re)
    apow = jnp.broadcast_to(_state_rows(cfg, mag * jnp.cos(sdt * a_im), mag * jnp.sin(sdt * a_im))[None, :], (8, NST))
    t = jnp.arange(cfg.T)
    perm = (t % 8) * S + t // 8
    pm = jax.nn.one_hot(perm, cfg.T, dtype=bf16)
    return bs, cs, arep, apow, pm, pm.T


def _cmul_add(ar, ai, xr, xi, br, bi):
    return ar * xr - ai * xi + br, ar * xi + ai * xr + bi


def _s5_forward_chunk(cfg, W, upb, bs_ref, arep_ref, apow_ref, st, x0, cin_store):
    _, NS, SC, SH, _ = _s5_dims(cfg)
    S = cfg.T // 8
    for s in range(NS):
        st[:, 2 * SH * s:2 * SH * (s + 1)] = jnp.dot(upb[:, SC * s:SC * (s + 1)], bs_ref[s], preferred_element_type=f32)
    rows = lax.broadcasted_iota(jnp.int32, (8, W), 0)
    zero = jnp.zeros((8, W), f32)
    for rc, ic in _lane_slabs(cfg, W):
        ar = arep_ref[:, rc:rc + W]
        ai = arep_ref[:, ic:ic + W]

        def step(i, carry, rc=rc, ic=ic, ar=ar, ai=ai):
            xr, xi = carry
            r0 = pl.multiple_of(i * 8, 8)
            nr, ni = _cmul_add(ar, ai, xr, xi, st[pl.ds(r0, 8), rc:rc + W], st[pl.ds(r0, 8), ic:ic + W])
            st[pl.ds(r0, 8), rc:rc + W] = nr
            st[pl.ds(r0, 8), ic:ic + W] = ni
            return nr, ni

        er, ei = lax.fori_loop(0, S, step, (zero, zero))
        pr = apow_ref[:, rc:rc + W]
        pi = apow_ref[:, ic:ic + W]
        x0r = x0[:, rc:rc + W]
        x0i = x0[:, ic:ic + W]
        cr = jnp.where(rows == 0, x0r, 0.0)
        ci = jnp.where(rows == 0, x0i, 0.0)
        for _ in range(7):
            fr, fi = _cmul_add(pr, pi, cr, ci, er, ei)
            cr = jnp.where(rows == 0, x0r, pltpu.roll(fr, 1, axis=0))
            ci = jnp.where(rows == 0, x0i, pltpu.roll(fi, 1, axis=0))
        fr, fi = _cmul_add(pr, pi, cr, ci, er, ei)
        x0[:, rc:rc + W] = jnp.broadcast_to(fr[7:8, :], (8, W))
        x0[:, ic:ic + W] = jnp.broadcast_to(fi[7:8, :], (8, W))
        cin_store(rc, ic, cr, ci)

        def fix(i, carry, rc=rc, ic=ic, ar=ar, ai=ai):
            kr, ki = carry
            r0 = pl.multiple_of(i * 8, 8)
            nr, ni = ar * kr - ai * ki, ar * ki + ai * kr
            st[pl.ds(r0, 8), rc:rc + W] = st[pl.ds(r0, 8), rc:rc + W] + nr
            st[pl.ds(r0, 8), ic:ic + W] = st[pl.ds(r0, 8), ic:ic + W] + ni
            return nr, ni

        lax.fori_loop(0, S, fix, (cr, ci))


def _s5_fwd(cfg, proj, tabs, dskip):
    L, SW, T = cfg.L, cfg.SW, cfg.T
    G, NS, SC, SH, NST = _s5_dims(cfg)
    bs, cs, arep, apow, pm, pmt = tabs
    W = min(512, SH)
    NC = L // T

    def body(u_ref, pm_ref, pmt_ref, bs_ref, cs_ref, arep_ref, apow_ref, dskip_ref, y_ref, ya0_ref, cin_ref, st, x0):
        c = pl.program_id(0)

        @pl.when(c == 0)
        def _():
            x0[...] = jnp.zeros_like(x0)

        up = jnp.dot(pm_ref[...], u_ref[...], preferred_element_type=f32)
        upb = up.astype(bf16)

        def cin_store(rc, ic, cr, ci):
            cin_ref[0, :, rc:rc + W] = cr
            cin_ref[0, :, ic:ic + W] = ci

        _s5_forward_chunk(cfg, W, upb, bs_ref, arep_ref, apow_ref, st, x0, cin_store)
        yp = jnp.concatenate(
            [jnp.dot(st[:, 2 * SH * s:2 * SH * (s + 1)].astype(bf16), cs_ref[s], preferred_element_type=f32)
             for s in range(NS)], axis=1) + dskip_ref[...] * up
        y = jnp.dot(pmt_ref[...], yp.astype(bf16), preferred_element_type=f32)
        y_ref[...] = y.astype(bf16)
        ya0_ref[...] = _gelu(y).astype(bf16)

    const2 = lambda shape: pl.BlockSpec(shape, lambda c: (0, 0))
    const3 = lambda shape: pl.BlockSpec(shape, lambda c: (0, 0, 0))
    return pl.pallas_call(
        body, grid=(NC,),
        in_specs=[pl.BlockSpec((T, SW), lambda c: (c, 0)), const2((T, T)), const2((T, T)), const3((NS, SC, 2 * SH)),
                  const3((NS, 2 * SH, SC)), const2((8, NST)), const2((8, NST)), const2((1, SW))],
        out_specs=[pl.BlockSpec((T, SW), lambda c: (c, 0)), pl.BlockSpec((T, SW), lambda c: (c, 0)),
                   pl.BlockSpec((1, 8, NST), lambda c: (c, 0, 0)), pl.BlockSpec((T, NST), lambda c: (c, 0))],
        out_shape=[jax.ShapeDtypeStruct((L, SW), bf16), jax.ShapeDtypeStruct((L, SW), bf16),
                   jax.ShapeDtypeStruct((NC, 8, NST), f32), jax.ShapeDtypeStruct((L, NST), f32)],
        scratch_shapes=[pltpu.VMEM((8, NST), f32)],
        name="s5_fwd", compiler_params=_cparams(("arbitrary",)))(proj, pm, pmt, bs, cs, arep, apow, dskip)


def _s5_bwd(cfg, proj, dy, cin, xs, tabs, dskip, dproj, after=()):
    L, SW, T = cfg.L, cfg.SW, cfg.T
    du_col = 3 * cfg.CW // SW
    G, NS, SC, SH, NST = _s5_dims(cfg)
    bs, cs, arep, apow, pm, pmt = tabs
    W = min(512, SH)
    S = T // 8
    NC = L // T

    def body(u_ref, dy_ref, cin_ref, st, pm_ref, pmt_ref, bs_ref, cs_ref, arep_ref, apow_ref, dskip_ref, dproj_ref, *rest):
        du_ref, da_ref, db_ref, dc_ref, dd_ref, gs, g0, db_acc, dc_acc = rest[len(after):]
        c = pl.program_id(0)

        @pl.when(c == 0)
        def _():
            g0[...] = jnp.zeros_like(g0)
            da_ref[...] = jnp.zeros_like(da_ref)
            dd_ref[...] = jnp.zeros_like(dd_ref)
            db_acc[...] = jnp.zeros_like(db_acc)
            dc_acc[...] = jnp.zeros_like(dc_acc)

        up = jnp.dot(pm_ref[...], u_ref[...], preferred_element_type=f32)
        upb = up.astype(bf16)
        dyp = jnp.dot(pm_ref[...], dy_ref[...], preferred_element_type=f32)
        dypb = dyp.astype(bf16)
        for s in range(NS):
            gs[:, 2 * SH * s:2 * SH * (s + 1)] = lax.dot_general(
                dypb[:, SC * s:SC * (s + 1)], cs_ref[s], (((1,), (1,)), ((), ())), preferred_element_type=f32)
        rows = lax.broadcasted_iota(jnp.int32, (8, W), 0)
        zero = jnp.zeros((8, W), f32)
        for rc, ic in _lane_slabs(cfg, W):
            ar = arep_ref[:, rc:rc + W]
            ai = arep_ref[:, ic:ic + W]

            def rstep(k, carry, rc=rc, ic=ic, ar=ar, ai=ai):
                gr, gi = carry
                r0 = pl.multiple_of((S - 1 - k) * 8, 8)
                nr = ar * gr + ai * gi + gs[pl.ds(r0, 8), rc:rc + W]
                ni = ar * gi - ai * gr + gs[pl.ds(r0, 8), ic:ic + W]
                gs[pl.ds(r0, 8), rc:rc + W] = nr
                gs[pl.ds(r0, 8), ic:ic + W] = ni
                return nr, ni

            er, ei = lax.fori_loop(0, S, rstep, (zero, zero))
            pr = apow_ref[:, rc:rc + W]
            pi = apow_ref[:, ic:ic + W]
            g0r = g0[:, rc:rc + W]
            g0i = g0[:, ic:ic + W]
            cr = jnp.where(rows == 7, g0r, 0.0)
            ci = jnp.where(rows == 7, g0i, 0.0)
            for _ in range(7):
                fr = er + pr * cr + pi * ci
                fi = ei + pr * ci - pi * cr
                cr = jnp.where(rows == 7, g0r, pltpu.roll(fr, 7, axis=0))
                ci = jnp.where(rows == 7, g0i, pltpu.roll(fi, 7, axis=0))
            fr = er + pr * cr + pi * ci
            fi = ei + pr * ci - pi * cr
            g0[:, rc:rc + W] = jnp.broadcast_to(fr[0:1, :], (8, W))
            g0[:, ic:ic + W] = jnp.broadcast_to(fi[0:1, :], (8, W))

            def fix(k, carry, rc=rc, ic=ic, ar=ar, ai=ai):
                kr, ki, accr, acci = carry
                i = S - 1 - k
                r0 = pl.multiple_of(i * 8, 8)
                rp = pl.multiple_of((i - 1) * 8, 8)
                nr = ar * kr + ai * ki
                ni = ar * ki - ai * kr
                gr = gs[pl.ds(r0, 8), rc:rc + W] + nr
                gi = gs[pl.ds(r0, 8), ic:ic + W] + ni
                gs[pl.ds(r0, 8), rc:rc + W] = gr
                gs[pl.ds(r0, 8), ic:ic + W] = gi
                xr = st[pl.ds(rp, 8), rc:rc + W]
                xi = st[pl.ds(rp, 8), ic:ic + W]
                return nr, ni, accr + gr * xr + gi * xi, acci + gi * xr - gr * xi

            kr, ki, accr, acci = lax.fori_loop(0, S - 1, fix, (cr, ci, zero, zero))
            nr = ar * kr + ai * ki
            ni = ar * ki - ai * kr
            gr = gs[0:8, rc:rc + W] + nr
            gi = gs[0:8, ic:ic + W] + ni
            gs[0:8, rc:rc + W] = gr
            gs[0:8, ic:ic + W] = gi
            xr = cin_ref[0, :, rc:rc + W]
            xi = cin_ref[0, :, ic:ic + W]
            da_ref[:, rc:rc + W] += accr + gr * xr + gi * xi
            da_ref[:, ic:ic + W] += acci + gi * xr - gr * xi

        dups = []
        for s in range(NS):
            gsb = gs[:, 2 * SH * s:2 * SH * (s + 1)].astype(bf16)
            dups.append(lax.dot_general(gsb, bs_ref[s], (((1,), (1,)), ((), ())), preferred_element_type=f32))
            db_acc[s] += lax.dot_general(upb[:, SC * s:SC * (s + 1)], gsb, (((0,), (0,)), ((), ())),
                                         preferred_element_type=f32)
            dc_acc[s] += lax.dot_general(st[:, 2 * SH * s:2 * SH * (s + 1)].astype(bf16), dypb[:, SC * s:SC * (s + 1)],
                                         (((0,), (0,)), ((), ())), preferred_element_type=f32)
        dup = jnp.concatenate(dups, axis=1) + dskip_ref[...] * dyp
        du_ref[...] = jnp.dot(pmt_ref[...], dup.astype(bf16), preferred_element_type=f32).astype(bf16)
        dd_ref[...] += jnp.sum(dyp * up, axis=0, keepdims=True)

        @pl.when(c == NC - 1)
        def _():
            PS, GH = _SSM_STATE, _SSM_GROUP
            mask_b = (lax.broadcasted_iota(jnp.int32, (SC, SH), 0) // GH
                      == lax.broadcasted_iota(jnp.int32, (SC, SH), 1) // PS)
            mask_c = (lax.broadcasted_iota(jnp.int32, (SH, SC), 0) // PS
                      == lax.broadcasted_iota(jnp.int32, (SH, SC), 1) // GH)
            for s in range(NS):
                for r in range(2):
                    xb = jnp.where(mask_b, db_acc[s, :, r * SH:(r + 1) * SH], 0.0)
                    zb = xb[:, 0:128]
                    for q in range(1, SH // 128):
                        zb = zb + xb[:, q * 128:(q + 1) * 128]
                    db_ref[s, r] = zb + pltpu.roll(zb, PS, axis=1)
                    xc = jnp.where(mask_c, dc_acc[s, r * SH:(r + 1) * SH, :], 0.0)
                    zc = xc[0:PS]
                    for q in range(1, SH // PS):
                        zc = zc + xc[q * PS:(q + 1) * PS]
                    dc_ref[s, r] = zc

    rev = lambda c: (NC - 1 - c, 0)
    const2 = lambda shape: pl.BlockSpec(shape, lambda c: (0, 0))
    const3 = lambda shape: pl.BlockSpec(shape, lambda c: (0, 0, 0))
    const4 = lambda shape: pl.BlockSpec(shape, lambda c: (0, 0, 0, 0))
    return pl.pallas_call(
        body, grid=(NC,),
        in_specs=[pl.BlockSpec((T, SW), rev), pl.BlockSpec((T, SW), rev), pl.BlockSpec((1, 8, NST), lambda c: (NC - 1 - c, 0, 0)),
                  pl.BlockSpec((T, NST), rev), const2((T, T)), const2((T, T)),
                  pl.BlockSpec((NS, SC, 2 * SH), lambda c: (0, 0, 0), pipeline_mode=pl.Buffered(1)),
                  pl.BlockSpec((NS, 2 * SH, SC), lambda c: (0, 0, 0), pipeline_mode=pl.Buffered(1)),
                  const2((8, NST)), const2((8, NST)), const2((1, SW)), _ANY] + [_ANY] * len(after),
        out_specs=[pl.BlockSpec((T, SW), lambda c: (NC - 1 - c, du_col)), const2((8, NST)),
                   const4((NS, 2, SC, 128)), const4((NS, 2, _SSM_STATE, SC)), const2((1, SW))],
        out_shape=[jax.ShapeDtypeStruct(dproj.shape, bf16), jax.ShapeDtypeStruct((8, NST), f32),
                   jax.ShapeDtypeStruct((NS, 2, SC, 128), f32), jax.ShapeDtypeStruct((NS, 2, _SSM_STATE, SC), f32),
                   jax.ShapeDtypeStruct((1, SW), f32)],
        scratch_shapes=[pltpu.VMEM((T, NST), f32), pltpu.VMEM((8, NST), f32),
                        pltpu.VMEM((NS, SC, 2 * SH), f32), pltpu.VMEM((NS, 2 * SH, SC), f32)],
        input_output_aliases={11: 0},
        name="s5_bwd", compiler_params=_cparams(("arbitrary",), _VMEM_S5_BWD))(proj, dy, cin, xs, pm, pmt, bs, cs, arep, apow, dskip, dproj,
                                                                  *after)


def _s5_param_grads(cfg, da, db_diag, dc_diag):
    G, NS, SC, SH, NST = _s5_dims(cfg)
    das = da.sum(axis=0).reshape(NS, 2, SH)
    dabr = das[:, 0].reshape(G, _SSM_STATE)
    dabi = das[:, 1].reshape(G, _SSM_STATE)
    dbd = db_diag[..., :_SSM_STATE].reshape(NS, 2, _SLAB_GROUPS, _SSM_GROUP, _SSM_STATE)
    dbb = dbd.transpose(1, 0, 2, 4, 3).reshape(2, G, _SSM_STATE, _SSM_GROUP)
    dcd = dc_diag.reshape(NS, 2, _SSM_STATE, _SLAB_GROUPS, _SSM_GROUP)
    dcc = dcd.transpose(1, 0, 3, 4, 2).reshape(2, G, _SSM_GROUP, _SSM_STATE)
    return dabr, dabi, dbb[0], dbb[1], dcc[0], -dcc[1]


def _coords():
    return lax.axis_index("x"), lax.axis_index("y"), lax.axis_index("c")


def _other_chips(x, y):
    return [(1 - x, y), (x, 1 - y), (1 - x, 1 - y)]


def _allreduce8(name, v):
    R = v.shape[0]

    def body(v_ref, o_ref, sib, chips, mine, ssem, rsem):
        x, y, c = _coords()
        d2d = pltpu.make_async_remote_copy(src_ref=v_ref, dst_ref=sib, send_sem=ssem.at[0], recv_sem=rsem.at[0],
                                           device_id=(x, y, 1 - c), device_id_type=_MESH)
        d2d.start()
        d2d.wait()
        mine[...] = v_ref[...] + sib[...]
        cps = [pltpu.make_async_remote_copy(src_ref=mine, dst_ref=chips.at[j], send_sem=ssem.at[1 + j],
                                            recv_sem=rsem.at[1 + j], device_id=(*chip, c), device_id_type=_MESH)
               for j, chip in enumerate(_other_chips(x, y))]
        for cp in cps:
            cp.start()
        for cp in cps:
            cp.wait()
        o_ref[...] = (mine[...] + chips[1]) + (chips[0] + chips[2])

    vm = pl.BlockSpec(memory_space=pltpu.VMEM)
    return pl.pallas_call(
        body, in_specs=[vm], out_specs=vm, out_shape=jax.ShapeDtypeStruct((R, 128), f32),
        scratch_shapes=[pltpu.VMEM((R, 128), f32), pltpu.VMEM((3, R, 128), f32), pltpu.VMEM((R, 128), f32),
                        pltpu.SemaphoreType.DMA((4,)), pltpu.SemaphoreType.DMA((4,))],
        name=name, compiler_params=pltpu.CompilerParams(vmem_limit_bytes=_VMEM_LIMIT))(v)


def _cast_into_slot(name, w, k_idx, also_alone=False):
    R, C = w.shape
    tr = _tile(R, 256, 16)

    def body(k_ref, w_ref, *o_refs):
        for o_ref in o_refs:
            o_ref[...] = w_ref[...].astype(bf16)

    slot = (jax.ShapeDtypeStruct((_NCHIP, R, C), bf16), pl.BlockSpec((None, tr, C), lambda r, kr: (kr[0], r, 0)))
    alone = (jax.ShapeDtypeStruct((R, C), bf16), pl.BlockSpec((tr, C), lambda r, kr: (r, 0)))
    outs = [slot, alone] if also_alone else [slot]
    gs = pltpu.PrefetchScalarGridSpec(
        num_scalar_prefetch=1, grid=(R // tr,),
        in_specs=[pl.BlockSpec((tr, C), lambda r, kr: (r, 0))], out_specs=[o[1] for o in outs])
    res = pl.pallas_call(body, grid_spec=gs, out_shape=[o[0] for o in outs], name=name,
                         compiler_params=_cparams(("parallel",)))(k_idx, w)
    return res if also_alone else res[0]


def _in_proj(cfg, xn1, w_own, w_in, k_idx):
    L, D = xn1.shape
    S, _, Ns = w_in.shape
    tm, tn = _tile(L, 1024, 16), _tile(Ns, 1024, 128)
    nb = Ns // tn

    def body(k_ref, a_ref, w_ref, *rest):
        rest[-1][...] = jnp.dot(a_ref[...], w_ref[...], preferred_element_type=f32).astype(bf16)

    a_spec = pl.BlockSpec((tm, D), lambda i, j, kr: (i, 0))
    shape = jax.ShapeDtypeStruct((L, S * Ns), bf16)
    own = pltpu.PrefetchScalarGridSpec(
        num_scalar_prefetch=1, grid=(L // tm, nb),
        in_specs=[a_spec, pl.BlockSpec((D, tn), lambda i, j, kr: (0, j))],
        out_specs=pl.BlockSpec((tm, tn), lambda i, j, kr: (i, kr[0] * nb + j)))
    proj = pl.pallas_call(body, grid_spec=own, out_shape=shape, name="in_proj_own",
                          compiler_params=_cparams(("parallel", "parallel")))(k_idx, xn1, w_own)
    shard = lambda j, kr: (kr[0] + 1 + j // nb) % S
    rest = pltpu.PrefetchScalarGridSpec(
        num_scalar_prefetch=1, grid=(L // tm, (S - 1) * nb),
        in_specs=[a_spec, pl.BlockSpec((None, D, tn), lambda i, j, kr: (shard(j, kr), 0, j % nb)), _ANY],
        out_specs=pl.BlockSpec((tm, tn), lambda i, j, kr: (i, shard(j, kr) * nb + j % nb)))
    return pl.pallas_call(body, grid_spec=rest, out_shape=shape, input_output_aliases={3: 0}, name="in_proj",
                          compiler_params=_cparams(("parallel", "parallel")))(k_idx, xn1, w_in, proj)


def _handshake(peers):
    barrier = pltpu.get_barrier_semaphore()
    for peer in peers:
        pl.semaphore_signal(barrier, inc=1, device_id=peer, device_id_type=_MESH)
    pl.semaphore_wait(barrier, len(peers))


def _allgather_weights(name, bufs, collective_id):
    n = len(bufs)
    refs = [jax.new_ref(b, memory_space=pltpu.MemorySpace.HBM) for b in bufs]

    def copy(ref, sems, idx, to):
        return pltpu.make_async_remote_copy(src_ref=ref, dst_ref=ref, send_sem=sems[0].at[idx], recv_sem=sems[1].at[idx],
                                            device_id=to, device_id_type=_MESH)

    def launch(ssem, rsem, qssem, qrsem, fssem, frsem):
        x, y, c = _coords()
        k = 2 * x + y
        nbrs = [(1 - x, y), (x, 1 - y)]
        across = 2 * (1 - x) + (1 - y)
        sibling = (x, y, 1 - c)
        _handshake([sibling] + [(*chip, c) for chip in nbrs])
        started = []

        def start(cp):
            cp.start()
            started.append(cp)

        for w in range(n):
            rh = refs[w].shape[1] // 2
            for j, chip in enumerate(nbrs):
                start(copy(refs[w].at[k, pl.ds(c * rh, rh)], (ssem, rsem), (w, j), (*chip, c)))
        for w in range(n):
            rh = refs[w].shape[1] // 2
            rq = rh // 2
            for j, (ox, oy) in enumerate(nbrs):
                ko = 2 * ox + oy
                landed = refs[w].at[ko, pl.ds(c * rh, rh)]
                copy(landed, (ssem, rsem), (w, j), (ox, oy, c)).wait_recv()
                start(copy(refs[w].at[ko, pl.ds(c * rh + j * rq, rq)], (qssem, qrsem), (w, j), (*nbrs[1 - j], c)))
                start(copy(landed, (fssem, frsem), (w, j), sibling))
        for w in range(n):
            rh = refs[w].shape[1] // 2
            rq = rh // 2
            for q in range(2):
                quarter = refs[w].at[across, pl.ds(c * rh + q * rq, rq)]
                copy(quarter, (qssem, qrsem), (w, q), (*nbrs[1 - q], c)).wait_recv()
            start(copy(refs[w].at[across, pl.ds(c * rh, rh)], (fssem, frsem), (w, 2), sibling))
        for w in range(n):
            rh = refs[w].shape[1] // 2
            for j, ko in enumerate([2 * nbrs[0][0] + nbrs[0][1], 2 * nbrs[1][0] + nbrs[1][1], across]):
                copy(refs[w].at[ko, pl.ds((1 - c) * rh, rh)], (fssem, frsem), (w, j), sibling).wait_recv()
        for cp in started:
            cp.wait_send()

    _sequencer_kernel(name, collective_id,
                      (pltpu.SemaphoreType.DMA((n, 2)), pltpu.SemaphoreType.DMA((n, 2)), pltpu.SemaphoreType.DMA((n, 2)),
                       pltpu.SemaphoreType.DMA((n, 2)), pltpu.SemaphoreType.DMA((n, 3)), pltpu.SemaphoreType.DMA((n, 3))),
                      launch)
    return [r[...] for r in refs]


def _sequencer_kernel(name, collective_id, sems, body):
    pl.kernel(body, mesh=plsc.ScalarSubcoreMesh(axis_name="seq", num_cores=1), name=name, scratch_types=sems,
              compiler_params=pltpu.CompilerParams(collective_id=collective_id))()


def _hbm_ref(a):
    return jax.new_ref(a, memory_space=pltpu.MemorySpace.HBM)


def _exchange_halves(name, grads, collective_id):
    n = len(grads)
    srcs = [_hbm_ref(g) for g in grads]
    dsts = [jax.empty_ref(jax.ShapeDtypeStruct((g.shape[0], g.shape[1] // 2, g.shape[2]), g.dtype),
                          memory_space=pltpu.MemorySpace.HBM) for g in grads]

    def body(ssem, rsem):
        x, y, c = _coords()
        _handshake([(x, y, 1 - c)])
        cps = []
        for w in range(n):
            rh = srcs[w].shape[1] // 2
            cp = pltpu.make_async_remote_copy(
                src_ref=srcs[w].at[:, pl.ds((1 - c) * rh, rh)], dst_ref=dsts[w], send_sem=ssem.at[w], recv_sem=rsem.at[w],
                device_id=(x, y, 1 - c), device_id_type=_MESH)
            cp.start()
            cps.append(cp)
        for cp in cps:
            cp.wait()

    _sequencer_kernel(name, collective_id, (pltpu.SemaphoreType.DMA((n,)), pltpu.SemaphoreType.DMA((n,))), body)
    return [d[...] for d in dsts]


def _scatter_shards(name, parts, collective_id):
    n = len(parts)
    srcs = [_hbm_ref(p) for p in parts]
    dsts = [jax.empty_ref(jax.ShapeDtypeStruct((3,) + p.shape[1:], p.dtype), memory_space=pltpu.MemorySpace.HBM)
            for p in parts]

    def body(ssem, rsem):
        x, y, c = _coords()
        k = 2 * x + y
        others = _other_chips(x, y)
        _handshake([(*chip, c) for chip in others])
        cps = []
        for w in range(n):
            for j, (ox, oy) in enumerate(others):
                cp = pltpu.make_async_remote_copy(
                    src_ref=srcs[w].at[(2 * ox + oy - k + 3) % 4], dst_ref=dsts[w].at[j], send_sem=ssem.at[w, j],
                    recv_sem=rsem.at[w, j],
                    device_id=(ox, oy, c), device_id_type=_MESH)
                cp.start()
                cps.append(cp)
        for cp in cps:
            cp.wait()

    _sequencer_kernel(name, collective_id, (pltpu.SemaphoreType.DMA((n, 3)), pltpu.SemaphoreType.DMA((n, 3))), body)
    return [d[...] for d in dsts]


def _join_halves(name, bufs, collective_id):
    n = len(bufs)
    refs = [_hbm_ref(b) for b in bufs]

    def body(ssem, rsem):
        x, y, c = _coords()
        _handshake([(x, y, 1 - c)])
        cps = []
        for w in range(n):
            rh = refs[w].shape[0] // 2
            mine = refs[w].at[pl.ds(c * rh, rh)]
            cp = pltpu.make_async_remote_copy(src_ref=mine, dst_ref=mine, send_sem=ssem.at[w], recv_sem=rsem.at[w],
                                              device_id=(x, y, 1 - c), device_id_type=_MESH)
            cp.start()
            cps.append(cp)
        for w in range(n):
            rh = refs[w].shape[0] // 2
            theirs = refs[w].at[pl.ds((1 - c) * rh, rh)]
            pltpu.make_async_remote_copy(src_ref=theirs, dst_ref=theirs, send_sem=ssem.at[w], recv_sem=rsem.at[w],
                                         device_id=(x, y, 1 - c), device_id_type=_MESH).wait_recv()
        for cp in cps:
            cp.wait_send()

    _sequencer_kernel(name, collective_id, (pltpu.SemaphoreType.DMA((n,)), pltpu.SemaphoreType.DMA((n,))), body)
    return [r[...] for r in refs]


def _allgather_small(name, v, collective_id):
    R = v.shape[0]
    src = _hbm_ref(v)
    dst = jax.empty_ref(jax.ShapeDtypeStruct((8, R, 128), v.dtype), memory_space=pltpu.MemorySpace.HBM)
    flips = [(dx, dy, dc) for dx in (0, 1) for dy in (0, 1) for dc in (0, 1)][1:]

    def body(lsem, ssem, rsem):
        x, y, c = _coords()
        me = 4 * x + 2 * y + c
        flip = lambda v, d: 1 - v if d else v
        peers = [(flip(x, dx), flip(y, dy), flip(c, dc)) for dx, dy, dc in flips]
        _handshake(peers)
        own = pltpu.make_async_copy(src, dst.at[me], lsem)
        own.start()
        cps = []
        for r, peer in enumerate(peers):
            cp = pltpu.make_async_remote_copy(src_ref=src, dst_ref=dst.at[me], send_sem=ssem.at[r], recv_sem=rsem.at[r],
                                              device_id=peer, device_id_type=_MESH)
            cp.start()
            cps.append(cp)
        for r, (px, py, pc) in enumerate(peers):
            theirs = dst.at[4 * px + 2 * py + pc]
            pltpu.make_async_remote_copy(src_ref=theirs, dst_ref=theirs, send_sem=ssem.at[r], recv_sem=rsem.at[r],
                                         device_id=(px, py, pc), device_id_type=_MESH).wait_recv()
        for cp in cps:
            cp.wait_send()
        own.wait()

    _sequencer_kernel(name, collective_id, (pltpu.SemaphoreType.DMA, pltpu.SemaphoreType.DMA((7,)),
                                            pltpu.SemaphoreType.DMA((7,))), body)
    return dst[...]


def _sum8(name, g8, after):
    R = g8.shape[1]
    tr = _tile(R, 512, 8)

    def body(g_ref, *rest):
        rest[-1][...] = (((g_ref[0] + g_ref[1]) + (g_ref[2] + g_ref[3]))
                         + ((g_ref[4] + g_ref[5]) + (g_ref[6] + g_ref[7])))

    return pl.pallas_call(body, grid=(R // tr,),
                          in_specs=[pl.BlockSpec((8, tr, 128), lambda i: (0, i, 0))] + [_ANY] * len(after),
                          out_specs=pl.BlockSpec((tr, 128), lambda i: (i, 0)), out_shape=jax.ShapeDtypeStruct((R, 128), f32),
                          name=name, compiler_params=_cparams(("parallel",)))(g8, *after)


def _add_own_half(name, g, t, kc_idx, after):
    S, R, C = g.shape
    rh = R // 2
    tr = _tile(rh, 512, 16)
    nrb = rh // tr
    shard = lambda s, kc: (kc[0] + 1 + s) % S

    def body(kc_ref, g_ref, t_ref, *rest):
        rest[-1][...] = (g_ref[...] + t_ref[...]).astype(bf16)

    gs = pltpu.PrefetchScalarGridSpec(
        num_scalar_prefetch=1, grid=(S - 1, nrb),
        in_specs=[pl.BlockSpec((None, tr, C), lambda s, r, kc: (shard(s, kc), kc[1] * nrb + r, 0)),
                  pl.BlockSpec((None, tr, C), lambda s, r, kc: (shard(s, kc), r, 0))] + [_ANY] * len(after),
        out_specs=pl.BlockSpec((None, tr, C), lambda s, r, kc: (s, r, 0)))
    return pl.pallas_call(body, grid_spec=gs, out_shape=jax.ShapeDtypeStruct((S - 1, rh, C), bf16), name=name,
                          compiler_params=_cparams(("parallel", "parallel")))(kc_idx, g, t, *after)


def _add_shard_parts(name, g, t, r, kc_idx, after):
    S, R, C = g.shape
    rh = R // 2
    tr = _tile(rh, 256, 16)
    nrb = rh // tr

    def body(kc_ref, g_ref, t_ref, r_ref, *rest):
        own = g_ref[...] + t_ref[...]
        rest[-1][...] = (own + r_ref[1].astype(f32)) + (r_ref[0].astype(f32) + r_ref[2].astype(f32))

    gs = pltpu.PrefetchScalarGridSpec(
        num_scalar_prefetch=1, grid=(nrb,),
        in_specs=[pl.BlockSpec((None, tr, C), lambda i, kc: (kc[0], kc[1] * nrb + i, 0)),
                  pl.BlockSpec((None, tr, C), lambda i, kc: (kc[0], i, 0)),
                  pl.BlockSpec((3, tr, C), lambda i, kc: (0, i, 0))] + [_ANY] * len(after),
        out_specs=pl.BlockSpec((tr, C), lambda i, kc: (kc[1] * nrb + i, 0)))
    return pl.pallas_call(body, grid_spec=gs, out_shape=jax.ShapeDtypeStruct((R, C), f32), name=name,
                          compiler_params=_cparams(("parallel",)))(kc_idx, g, t, r, *after)


def _adamw_update(wv, gv, mv, vv):
    nm = _ADAM_B1 * mv + (1.0 - _ADAM_B1) * gv
    nv = _ADAM_B2 * vv + (1.0 - _ADAM_B2) * (gv * gv)
    m_hat = nm / (1.0 - _ADAM_B1 ** _ADAM_STEP)
    v_hat = nv / (1.0 - _ADAM_B2 ** _ADAM_STEP)
    return -_ADAM_LR * (m_hat / (jnp.sqrt(v_hat) + _ADAM_EPS) + _ADAM_WD * wv), nm, nv


def _adamw(name, w, g, m, v, after=()):
    R, C = w.shape
    tr = _tile(R, 256, 8)

    def body(w_ref, g_ref, m_ref, v_ref, *rest):
        go_ref, d_ref, nm_ref, nv_ref = rest[len(after):]
        gv = g_ref[...]
        go_ref[...] = gv
        d_ref[...], nm_ref[...], nv_ref[...] = _adamw_update(w_ref[...], gv, m_ref[...], v_ref[...])

    spec = pl.BlockSpec((tr, C), lambda i: (i, 0))
    shape = jax.ShapeDtypeStruct((R, C), f32)
    return pl.pallas_call(body, grid=(R // tr,), in_specs=[spec] * 4 + [_ANY] * len(after), out_specs=[spec] * 4,
                          out_shape=[shape] * 4, name=name, compiler_params=_cparams(("parallel",)))(w, g, m, v, *after)


def _adamw_whole(name, ws, gs, ms, vs):
    n = len(ws)

    def body(*refs):
        ins, outs = refs[:4 * n], refs[4 * n:]
        for i in range(n):
            w_ref, g_ref, m_ref, v_ref = (ins[j * n + i] for j in range(4))
            d, nm, nv = _adamw_update(w_ref[...], g_ref[...], m_ref[...], v_ref[...])
            outs[i][...], outs[n + i][...], outs[2 * n + i][...] = d, nm, nv

    vm = pl.BlockSpec(memory_space=pltpu.VMEM)
    res = pl.pallas_call(body, in_specs=[vm] * (4 * n), out_specs=[vm] * (3 * n),
                         out_shape=[jax.ShapeDtypeStruct(w.shape, f32) for w in ws] * 3, name=name,
                         compiler_params=pltpu.CompilerParams(vmem_limit_bytes=_VMEM_LIMIT))(*ws, *gs, *ms, *vs)
    return res[:n], res[n:2 * n], res[2 * n:]


def _pack(arrs):
    flat = jnp.concatenate([a.reshape(-1).astype(f32) for a in arrs])
    n = flat.shape[0]
    pad = (-n) % (128 * 128)
    return jnp.pad(flat, (0, pad)).reshape(-1, 128)


def _unpack(packed, shapes):
    flat = packed.reshape(-1)
    out, off = [], 0
    for s in shapes:
        n = math.prod(s)
        out.append(flat[off:off + n].reshape(s))
        off += n
    return out


_BIG = ("w_in", "w_glu", "w_ssm_out", "w_conv_out", "w_o", "w_up", "w_down")
_SMALL = ("norm_tok", "a_re", "a_im", "log_dt", "b_re", "b_im", "c_re", "c_im", "d_skip", "conv_w", "conv_b",
          "norm_ffn", "ffn_conv_w", "ffn_conv_b", "norm_final")
_WEIGHTS = ("norm_tok", "w_in", "a_re", "a_im", "log_dt", "b_re", "b_im", "c_re", "c_im", "d_skip", "w_glu",
            "w_ssm_out", "conv_w", "conv_b", "w_conv_out", "w_o", "norm_ffn", "w_up", "ffn_conv_w", "ffn_conv_b",
            "w_down", "norm_final")


def _step(cfg, x, tgt, p, m, v):
    L, D, SW, CW, F = cfg.L, cfg.D, cfg.SW, cfg.CW, cfg.F
    xi, yi, ci = _coords()
    k_idx = (2 * xi + yi).astype(jnp.int32).reshape(1)
    c_idx = ci.astype(jnp.int32).reshape(1)
    x = x.reshape(L, D)
    tgt = tgt.reshape(L, D)

    big2d = {n: p[n].reshape(p[n].shape[-2], p[n].shape[-1]) for n in _BIG}
    slots = {n: _cast_into_slot("cast_" + n, big2d[n], k_idx) for n in _BIG if n != "w_in"}
    slots["w_in"], w_in_own = _cast_into_slot("cast_w_in", big2d["w_in"], k_idx, also_alone=True)
    wg = {}
    for cid, (gname, group) in enumerate((("allgather_w_in", ("w_in",)),
                                          ("allgather_w_mixer", ("w_glu", "w_ssm_out", "w_conv_out", "w_o")),
                                          ("allgather_w_up", ("w_up",)), ("allgather_w_down", ("w_down",)))):
        wg.update(zip(group, _allgather_weights(gname, [slots[n] for n in group], cid)))
    w_in, w_so, w_co, w_up = wg["w_in"], wg["w_ssm_out"], wg["w_conv_out"], wg["w_up"]
    w_glu, w_o, w_down = wg["w_glu"], wg["w_o"], wg["w_down"]
    kk = k_idx[0]
    cw_full = lax.dynamic_update_slice(jnp.zeros((3, CW), f32), p["conv_w"].reshape(3, CW // _NCHIP), (0, kk * (CW // _NCHIP)))
    fw_full = lax.dynamic_update_slice(jnp.zeros((3, F), f32), p["ffn_conv_w"].reshape(3, F // _NCHIP), (0, kk * (F // _NCHIP)))
    south = (ci == 0).astype(f32)
    filters8 = _allgather_small("allgather_conv_filters", _pack([cw_full * south, fw_full * south]), 17)
    conv_b = p["conv_b"].reshape(1, CW)
    ffn_conv_b = p["ffn_conv_b"].reshape(1, F)
    norm_tok = p["norm_tok"].reshape(1, D)
    norm_ffn = p["norm_ffn"].reshape(1, D)
    norm_final = p["norm_final"].reshape(1, D)
    dskip = p["d_skip"].reshape(1, SW)

    s5_in = (p["a_re"][0], p["a_im"][0], p["log_dt"][0], p["b_re"][0], p["b_im"][0])
    (abr, abi, bbr, bbi), disc_vjp = jax.vjp(_discretize, *s5_in)
    tabs = _s5_tables(cfg, *s5_in[:3], abr, abi, bbr, bbi, p["c_re"][0], p["c_im"][0])

    xn1, r1 = _rms_fwd("rms_tok", x, norm_tok)
    proj = _in_proj(cfg, xn1, w_in_own, w_in, k_idx)
    conv_w, ffn_conv_w = _unpack(_sum8("sum_conv_filters", filters8, [proj]), [(3, CW), (3, F)])
    y_s, ya0, cin, xs = _s5_fwd(cfg, proj, tabs, dskip)

    def tiles(*arrs):
        return lambda tm, tn: [(a, pl.BlockSpec((tm, tn), lambda i, j: (i, j))) for a in arrs]

    def glu_epi(acc, e, o):
        o[0][...] = (e[0][...].astype(f32) * jax.nn.sigmoid(acc)).astype(bf16)
        o[1][...] = acc.astype(bf16)

    ya1, z = _mm_nn("glu", ya0, w_glu, [bf16, bf16], rows=True, extras_fn=tiles(ya0), epilogue=glu_epi)
    yb0 = _convb_fwd(cfg, proj, conv_w, conv_b)
    merged, ya, yb = _merge_fwd(cfg, ya1, yb0, proj, w_so, w_co)

    def res_epi(acc, e, o):
        o[0][...] = e[0][...] + acc

    h1 = _mm_nn("out_proj", merged, w_o, [f32], tm=512, tn=D, rows=True, extras_fn=tiles(x), epilogue=res_epi)[0]
    xn2, r2 = _rms_fwd("rms_ffn", h1, norm_ffn)
    hh = _mm_nn("ffn_up", xn2, w_up, [bf16], tn=2816)[0]
    fact, ffn_pre = _ffn_act(cfg, hh, ffn_conv_w, ffn_conv_b)
    h2 = _mm_nn("ffn_down", fact, w_down, [f32], tm=512, rows=True, extras_fn=tiles(h1), epilogue=res_epi)[0]
    dh2, dh2b, g_norm_final, loss_tile = _loss_head("loss_head", h2, tgt, norm_final)

    kc_idx = jnp.concatenate([k_idx, c_idx])
    reduced, chains = {}, {}

    def rs_halves(tag, collective_id, names, gs):
        chains[tag] = dict(cid=collective_id, names=names, gs=gs,
                           sib=_exchange_halves("grad_halves_" + tag, gs, collective_id))

    def rs_shards(tag, after):
        ch = chains[tag]
        ch["parts"] = [_add_own_half("grad_add_halves_" + n, g, t, kc_idx, after)
                       for n, g, t in zip(ch["names"], ch["gs"], ch["sib"])]
        ch["chips"] = _scatter_shards("grad_shards_" + tag, ch["parts"], ch["cid"] + 1)
        return ch["parts"]

    def rs_join(tag, after):
        ch = chains[tag]
        ch["halves"] = [_add_shard_parts("grad_add_chips_" + n, g, t, r, kc_idx, after)
                        for n, g, t, r in zip(ch["names"], ch["gs"], ch["sib"], ch["chips"])]
        reduced.update(zip(ch["names"], _join_halves("grad_join_" + tag, ch["halves"], ch["cid"] + 2)))
        return ch["halves"]

    g_w_down = _mm_tn("ffn_down_dw", fact, dh2b, tm=1408, tn=512)
    rs_halves("ffn_down", 4, ["w_down"], [g_w_down.reshape(_NCHIP, F // _NCHIP, D)])
    dhh, g_ffn_conv_w, g_ffn_conv_b = _ffn_down_dx_act_bwd(cfg, dh2b, w_down, hh, ffn_pre, ffn_conv_w)
    sent = rs_shards("ffn_down", [dhh])
    g_w_up = _mm_tn("ffn_up_dw", xn2, dhh, shards=_NCHIP, tm=512, tn=1408, b_resident=True, after=sent)
    rs_halves("ffn_up", 7, ["w_up"], [g_w_up])
    dxn2 = _ffn_up_dx(cfg, dhh, w_up)
    sent = rs_shards("ffn_up", [dxn2]) + rs_join("ffn_down", [dxn2])
    dh1, dh1b, g_norm_ffn = _rms_bwd("rms_ffn_bwd", dxn2, h1, r2, norm_ffn, dh2, after=sent)

    dya, dyb, dproj = _out_proj_dx(cfg, dh1b, w_o, ya, yb, proj)
    g_w_o = _mm_tn("out_proj_dw", merged, dh1b)

    def glu_bwd_epi(acc, e, o):
        a0 = e[0][...].astype(f32)
        s = jax.nn.sigmoid(e[1][...].astype(f32))
        o[0][...] = (acc * a0 * s * (1.0 - s)).astype(bf16)
        o[1][...] = (acc * s).astype(bf16)

    dz, t1 = _mm_nt("ssm_out_dx", dya, w_so, [bf16, bf16], extras_fn=tiles(ya0, z), epilogue=glu_bwd_epi)
    g_w_so = _mm_tn("ssm_out_dw", ya1, dya, shards=_NCHIP, tn=512)
    dyb0 = _mm_nt("conv_out_dx", dyb, w_co, [bf16])[0]
    g_w_co = _mm_tn("conv_out_dw", yb0, dyb, shards=_NCHIP, tn=512)
    dproj, g_conv_w, g_conv_b = _convb_bwd(cfg, proj, dyb0, conv_w, conv_b, dproj)

    def gelu_bwd_epi(acc, e, o):
        o[0][...] = ((e[0][...].astype(f32) + acc) * _gelu_grad(e[1][...].astype(f32))).astype(bf16)

    dy_s = _mm_nt("glu_dx", dz, w_glu, [bf16], rows=True, extras_fn=tiles(t1, y_s), epilogue=gelu_bwd_epi)[0]
    g_w_glu = _mm_tn("glu_dw", ya0, dz)
    rs_halves("mixer", 10, ["w_o", "w_ssm_out", "w_conv_out", "w_glu"],
              [g_w_o.reshape(_NCHIP, D // _NCHIP, D), g_w_so, g_w_co, g_w_glu.reshape(_NCHIP, SW // _NCHIP, SW)])
    sent = rs_join("ffn_up", [g_w_glu])
    dproj, da_acc, db_full, dc_full, g_dskip = _s5_bwd(cfg, proj, dy_s, cin, xs, tabs, dskip, dproj, after=sent)
    sent = rs_shards("mixer", [dproj])

    dabr, dabi, dbbr, dbbi, g_c_re, g_c_im = _s5_param_grads(cfg, da_acc, db_full, dc_full)
    g_a_re, g_a_im, g_log_dt, g_b_re, g_b_im = disc_vjp((dabr, dabi, dbbr, dbbi))
    small_g = {"a_re": g_a_re, "a_im": g_a_im, "log_dt": g_log_dt, "b_re": g_b_re, "b_im": g_b_im,
               "c_re": g_c_re, "c_im": g_c_im, "d_skip": g_dskip, "conv_w": g_conv_w, "conv_b": g_conv_b,
               "norm_ffn": g_norm_ffn, "ffn_conv_w": g_ffn_conv_w, "ffn_conv_b": g_ffn_conv_b, "norm_final": g_norm_final}
    early = [n for n in _SMALL if n != "norm_tok"]
    small8 = _allgather_small("allgather_small_grads", _pack([small_g[n] for n in early]), 16)

    g_w_in = _mm_tn("in_proj_dw", xn1, dproj, shards=_NCHIP, tn=CW, after=sent,
                    b_block=lambda j: jnp.where(j == 0, 3 * CW // SW, jnp.where(j < 4, j - 1, j)))
    rs_halves("in_proj", 13, ["w_in"], [g_w_in])
    dxn1 = _in_proj_dx(cfg, dproj, w_in)
    sent = rs_shards("in_proj", [dxn1]) + rs_join("mixer", [dxn1])
    dx, _, g_norm_tok = _rms_bwd("rms_tok_bwd", dxn1, x, r1, norm_tok, dh1, after=sent)

    summed = dict(zip(early, _unpack(_sum8("sum_small_grads", small8, [dx]), [small_g[n].shape for n in early])))
    summed["norm_tok"] = _unpack(_allreduce8("allreduce_norm_tok", _pack([g_norm_tok])), [g_norm_tok.shape])[0]
    summed["conv_w"] = lax.dynamic_slice(summed["conv_w"], (0, kk * (CW // _NCHIP)), (3, CW // _NCHIP))
    summed["ffn_conv_w"] = lax.dynamic_slice(summed["ffn_conv_w"], (0, kk * (F // _NCHIP)), (3, F // _NCHIP))

    grads, deltas, new_m, new_v = {}, {}, {}, {}

    def adamw_big(names, after):
        for n in names:
            g_, d_, m_, v_ = _adamw("adamw_" + n, big2d[n], reduced[n], m[n].reshape(big2d[n].shape),
                                    v[n].reshape(big2d[n].shape), after=after)
            grads[n], deltas[n], new_m[n], new_v[n] = (a.reshape(p[n].shape) for a in (g_, d_, m_, v_))
            after = [d_]
        return after

    for n in _SMALL:
        grads[n] = summed[n].reshape(p[n].shape)
    small_d, small_m, small_v = _adamw_whole("adamw_small", [p[n] for n in _SMALL], [grads[n] for n in _SMALL],
                                             [m[n] for n in _SMALL], [v[n] for n in _SMALL])
    deltas.update(zip(_SMALL, small_d))
    new_m.update(zip(_SMALL, small_m))
    new_v.update(zip(_SMALL, small_v))
    done = adamw_big(["w_down", "w_up", "w_o", "w_ssm_out", "w_conv_out", "w_glu"], [deltas["norm_final"]])
    rs_join("in_proj", done + [deltas[n] for n in _SMALL])
    adamw_big(["w_in"], ())

    loss = lax.psum(loss_tile[0, 0], ("x", "y", "c"))
    return (loss, dx.reshape(1, L, D), *[grads[n] for n in _WEIGHTS], *[deltas[n] for n in _WEIGHTS],
            *[new_m[n] for n in _WEIGHTS], *[new_v[n] for n in _WEIGHTS])


def kernel(x, norm_tok, w_in, a_re, a_im, log_dt, b_re, b_im, c_re, c_im, d_skip, w_glu, w_ssm_out, conv_w, conv_b, w_conv_out, w_o, norm_ffn, w_up, ffn_conv_w, ffn_conv_b, w_down, norm_final, loss_target, m_norm_tok, m_w_in, m_a_re, m_a_im, m_log_dt, m_b_re, m_b_im, m_c_re, m_c_im, m_d_skip, m_w_glu, m_w_ssm_out, m_conv_w, m_conv_b, m_w_conv_out, m_w_o, m_norm_ffn, m_w_up, m_ffn_conv_w, m_ffn_conv_b, m_w_down, m_norm_final, v_norm_tok, v_w_in, v_a_re, v_a_im, v_log_dt, v_b_re, v_b_im, v_c_re, v_c_im, v_d_skip, v_w_glu, v_w_ssm_out, v_conv_w, v_conv_b, v_w_conv_out, v_w_o, v_norm_ffn, v_w_up, v_ffn_conv_w, v_ffn_conv_b, v_w_down, v_norm_final):
    p = dict(norm_tok=norm_tok, w_in=w_in, a_re=a_re, a_im=a_im, log_dt=log_dt, b_re=b_re, b_im=b_im, c_re=c_re,
             c_im=c_im, d_skip=d_skip, w_glu=w_glu, w_ssm_out=w_ssm_out, conv_w=conv_w, conv_b=conv_b,
             w_conv_out=w_conv_out, w_o=w_o, norm_ffn=norm_ffn, w_up=w_up, ffn_conv_w=ffn_conv_w,
             ffn_conv_b=ffn_conv_b, w_down=w_down, norm_final=norm_final)
    m = dict(norm_tok=m_norm_tok, w_in=m_w_in, a_re=m_a_re, a_im=m_a_im, log_dt=m_log_dt, b_re=m_b_re, b_im=m_b_im,
             c_re=m_c_re, c_im=m_c_im, d_skip=m_d_skip, w_glu=m_w_glu, w_ssm_out=m_w_ssm_out, conv_w=m_conv_w,
             conv_b=m_conv_b, w_conv_out=m_w_conv_out, w_o=m_w_o, norm_ffn=m_norm_ffn, w_up=m_w_up,
             ffn_conv_w=m_ffn_conv_w, ffn_conv_b=m_ffn_conv_b, w_down=m_w_down, norm_final=m_norm_final)
    v = dict(norm_tok=v_norm_tok, w_in=v_w_in, a_re=v_a_re, a_im=v_a_im, log_dt=v_log_dt, b_re=v_b_re, b_im=v_b_im,
             c_re=v_c_re, c_im=v_c_im, d_skip=v_d_skip, w_glu=v_w_glu, w_ssm_out=v_w_ssm_out, conv_w=v_conv_w,
             conv_b=v_conv_b, w_conv_out=v_w_conv_out, w_o=v_w_o, norm_ffn=v_norm_ffn, w_up=v_w_up,
             ffn_conv_w=v_ffn_conv_w, ffn_conv_b=v_ffn_conv_b, w_down=v_w_down, norm_final=v_norm_final)
    return _step(_Cfg(), x, loss_target, p, m, v)
```

```python
import math
from typing import NamedTuple

import jax
import jax.numpy as jnp
from jax import lax
from jax.experimental import pallas as pl
from jax.experimental.pallas import tpu as pltpu
from jax.experimental.pallas import tpu_sc as plsc

f32 = jnp.float32
bf16 = jnp.bfloat16
_MESH = pl.DeviceIdType.MESH

_EPS = 1e-6
_ADAM_LR = 0.001
_ADAM_B1 = 0.9
_ADAM_B2 = 0.999
_ADAM_EPS = 1e-08
_ADAM_WD = 0.01
_ADAM_STEP = 10
_SSM_GROUP = 16
_SSM_STATE = 64
_SLAB_GROUPS = 16
_NCHIP = 4
_VMEM_LIMIT = 56 * 2**20
_VMEM_S5_BWD = 62 * 2**20
_GELU_C = math.sqrt(2.0 / math.pi)
_GELU_A = 0.044715


class _Cfg(NamedTuple):
    L: int = 4096
    D: int = 2048
    SW: int = 1024
    CW: int = 1024
    F: int = 5632
    T: int = 256


def _tile(n, pref, align):
    t = min(n, pref)
    t -= t % align
    while t > align and n % t:
        t -= align
    assert t > 0 and n % t == 0, (n, pref, align)
    return t


def _cparams(sem, vmem_limit=_VMEM_LIMIT):
    return pltpu.CompilerParams(dimension_semantics=sem, vmem_limit_bytes=vmem_limit)


def _gelu(x):
    return _gelu_and_grad(x)[0]


def _gelu_grad(x):
    return _gelu_and_grad(x)[1]


def _gelu_and_grad(x):
    x2 = x * x
    th = jnp.tanh(x * (_GELU_C + (_GELU_C * _GELU_A) * x2))
    half = 0.5 + 0.5 * th
    return x * half, half + (0.5 * x) * (1.0 - th * th) * (_GELU_C + (3.0 * _GELU_C * _GELU_A) * x2)


_NN = (((1,), (0,)), ((), ()))
_NT = (((1,), (1,)), ((), ()))
_TN = (((0,), (0,)), ((), ()))


def _whole(ref):
    return ref[...]


_ANY = pl.BlockSpec(memory_space=pl.ANY)


def _mm(name, operands, steps, *, grid, contract, outs, extras=(), epilogue=None, acc_shape=None, after=()):
    nop, ne, na = len(operands), len(extras), len(after)
    nk = len(steps)

    def body(*refs):
        op_refs = refs[:nop]
        e_refs = refs[nop:nop + ne]
        o_refs = refs[nop + ne + na:nop + ne + na + len(outs)]

        def partial(terms):
            tot = None
            for ai, av, bi, bv in terms:
                d = lax.dot_general(av(op_refs[ai]), bv(op_refs[bi]), contract, preferred_element_type=f32)
                tot = d if tot is None else tot + d
            return tot

        def finish(res):
            if epilogue is None:
                for o in o_refs:
                    o[...] = res.astype(o.dtype)
            else:
                epilogue(res, e_refs, o_refs)

        if nk == 1:
            finish(partial(steps[0][1]))
            return
        acc = refs[-1]
        kid = pl.program_id(len(grid) - 1)
        for k, terms in steps:
            def run(k=k, terms=terms):
                d = partial(terms)
                if k == 0:
                    acc[...] = d
                elif k < nk - 1:
                    acc[...] += d
                else:
                    finish(acc[...] + d)

            pl.when(kid == k)(run)

    sem = ("parallel",) * (len(grid) - (nk > 1)) + (("arbitrary",) if nk > 1 else ())
    return pl.pallas_call(
        body, grid=grid, in_specs=[o[1] for o in operands] + [e[1] for e in extras] + [_ANY] * na,
        out_specs=[o[1] for o in outs], out_shape=[o[0] for o in outs],
        scratch_shapes=[pltpu.VMEM(acc_shape, f32)] if nk > 1 else [], name=name,
        compiler_params=_cparams(sem))(*[o[0] for o in operands], *[e[0] for e in extras], *after)


def _mm_nn(name, a, w, out_dtypes, *, tm=1024, tn=1024, rows=False, extras_fn=None, epilogue=None):
    M, K = a.shape
    S, Ns = w.shape[0], w.shape[-1]
    N = Ns if rows else Ns * S
    tm, tn = _tile(M, tm, 16), _tile(Ns, tn, 128)
    nb = Ns // tn
    a_spec = pl.BlockSpec((tm, K), lambda i, j: (i, 0))
    if rows:
        b_spec = pl.BlockSpec((S, K // S, tn), lambda i, j: (0, 0, j))
        b_view = lambda r: r[...].reshape(K, tn)
    else:
        b_spec = pl.BlockSpec((None, K, tn), lambda i, j: (j // nb, 0, j % nb))
        b_view = _whole
    o_spec = pl.BlockSpec((tm, tn), lambda i, j: (i, j))
    outs = [(jax.ShapeDtypeStruct((M, N), dt), o_spec) for dt in out_dtypes]
    extras = extras_fn(tm, tn) if extras_fn is not None else ()
    return _mm(name, [(a, a_spec), (w, b_spec)], [(None, [(0, _whole, 1, b_view)])], grid=(M // tm, N // tn),
               contract=_NN, outs=outs, extras=extras, epilogue=epilogue)


def _mm_nt(name, a, w, out_dtypes, *, tm=1024, tn=1024, rows=False, extras_fn=None, epilogue=None):
    M, N = a.shape
    S, Ks, Ns = w.shape
    K = Ks * S if rows else Ks
    tm = _tile(M, tm, 16)
    a_spec = pl.BlockSpec((tm, N), lambda i, j: (i, 0))
    if rows:
        whole_shards = tn > Ks and tn % Ks == 0 and K % tn == 0
        tn = K if tn >= K else (tn if whole_shards else _tile(Ks, tn, 128))
        if tn == K or whole_shards:
            b_spec = pl.BlockSpec((tn // Ks, Ks, N), lambda i, j: (j, 0, 0))
            terms = [(0, _whole, 1, lambda r: r[...].reshape(tn, N))]
        else:
            nbs = Ks // tn
            b_spec = pl.BlockSpec((None, tn, N), lambda i, j: (j // nbs, j % nbs, 0))
            terms = [(0, _whole, 1, _whole)]
    else:
        tn = _tile(K, tn, 128)
        assert S * Ns == N
        b_spec = pl.BlockSpec((S, tn, Ns), lambda i, j: (0, j, 0))
        terms = [(0, lambda r, s=s: r[:, s * Ns:(s + 1) * Ns], 1, lambda r, s=s: r[s]) for s in range(S)]
    o_spec = pl.BlockSpec((tm, tn), lambda i, j: (i, j))
    outs = [(jax.ShapeDtypeStruct((M, K), dt), o_spec) for dt in out_dtypes]
    extras = extras_fn(tm, tn) if extras_fn is not None else ()
    return _mm(name, [(a, a_spec), (w, b_spec)], [(None, terms)], grid=(M // tm, K // tn), contract=_NT,
               outs=outs, extras=extras, epilogue=epilogue)


def _mm_tn(name, a, b, *, shards=None, tm=1024, tn=1024, b_block=None, b_resident=False, after=()):
    M, K = a.shape
    halves = b.shape[0] if b.ndim == 3 else 1
    Nh = b.shape[-1]
    N = Nh * halves
    Ns = N // shards if shards else N
    tm, tn = _tile(K, tm, 128), _tile(math.gcd(Ns, Nh), tn, 128)
    nb, nbh = Ns // tn, Nh // tn
    ij = (lambda g0, g1: (g1, g0)) if b_resident else (lambda g0, g1: (g0, g1))
    bmap = b_block if b_block is not None else (lambda j: j)
    a_spec = pl.BlockSpec((M, tm), lambda g0, g1: (0, ij(g0, g1)[0]))
    if halves > 1:
        b_spec = pl.BlockSpec((None, M, tn), lambda g0, g1: (bmap(ij(g0, g1)[1]) // nbh, 0, bmap(ij(g0, g1)[1]) % nbh))
    else:
        b_spec = pl.BlockSpec((M, tn), lambda g0, g1: (0, bmap(ij(g0, g1)[1])))
    if shards:
        out = (jax.ShapeDtypeStruct((shards, K, Ns), f32),
               pl.BlockSpec((None, tm, tn), lambda g0, g1: (ij(g0, g1)[1] // nb, ij(g0, g1)[0], ij(g0, g1)[1] % nb)))
    else:
        out = (jax.ShapeDtypeStruct((K, N), f32), pl.BlockSpec((tm, tn), lambda g0, g1: ij(g0, g1)))
    grid = (N // tn, K // tm) if b_resident else (K // tm, N // tn)
    return _mm(name, [(a, a_spec), (b, b_spec)], [(None, [(0, _whole, 1, _whole)])], grid=grid, contract=_TN,
               outs=[out], after=after)[0]


def _in_proj_dx(cfg, dproj, w_in):
    L, D, SW, CW = cfg.L, cfg.D, cfg.SW, cfg.CW
    NP = SW + 3 * CW + 2 * D
    Ns = NP // _NCHIP
    assert SW + CW == Ns and 2 * CW == Ns and D == Ns
    tm, tn = _tile(L, 1024, 16), _tile(D, 1024, 128)
    a_spec = pl.BlockSpec((tm, NP // 2), lambda i, j, k: (i, k))
    b_spec = pl.BlockSpec((2, tn, Ns), lambda i, j, k: (k, j, 0))
    first = [(0, lambda r: r[:, 0:CW], 1, lambda r: r[0, :, SW:SW + CW]),
             (0, lambda r: r[:, CW:3 * CW], 1, lambda r: r[1]),
             (0, lambda r: r[:, 3 * CW:3 * CW + SW], 1, lambda r: r[0, :, 0:SW])]
    second = [(0, lambda r: r[:, 0:D], 1, lambda r: r[0]), (0, lambda r: r[:, D:2 * D], 1, lambda r: r[1])]
    out = (jax.ShapeDtypeStruct((L, D), bf16), pl.BlockSpec((tm, tn), lambda i, j, k: (i, j)))
    return _mm("in_proj_dx", [(dproj, a_spec), (w_in, b_spec)], [(0, first), (1, second)], grid=(L // tm, D // tn, 2),
               contract=_NT, outs=[out], acc_shape=(tm, tn))[0]


def _out_proj_dx(cfg, dh1b, w_o, ya, yb, proj):
    L, D = cfg.L, cfg.D
    NP = cfg.SW + 3 * cfg.CW + 2 * D
    assert NP == 4 * D
    tm = _tile(L, 512, 16)

    def epilogue(acc, e, o):
        sa = jax.nn.sigmoid(e[2][:, 0:D].astype(f32))
        sb = jax.nn.sigmoid(e[2][:, D:2 * D].astype(f32))
        o[0][...] = (acc * sa).astype(bf16)
        o[1][...] = (acc * sb).astype(bf16)
        o[2][:, 0:D] = (acc * e[0][...].astype(f32) * sa * (1.0 - sa)).astype(bf16)
        o[2][:, D:2 * D] = (acc * e[1][...].astype(f32) * sb * (1.0 - sb)).astype(bf16)

    row = pl.BlockSpec((tm, D), lambda i, j: (i, 0))
    half = pl.BlockSpec((tm, 2 * D), lambda i, j: (i, 1))
    return _mm("out_proj_dx", [(dh1b, row), (w_o, pl.BlockSpec(w_o.shape, lambda i, j: (0, 0, 0),
                                                                pipeline_mode=pl.Buffered(1)))],
               [(None, [(0, _whole, 1, lambda r: r[...].reshape(D, D))])], grid=(L // tm, 1), contract=_NT,
               outs=[(jax.ShapeDtypeStruct((L, D), bf16), row), (jax.ShapeDtypeStruct((L, D), bf16), row),
                     (jax.ShapeDtypeStruct((L, NP), bf16), half)],
               extras=[(ya, row), (yb, row), (proj, half)], epilogue=epilogue)


def _ffn_up_dx(cfg, dhh, w_up):
    L, D, F = cfg.L, cfg.D, cfg.F
    Fh = F // 2
    tm, tn = _tile(L, 1024, 16), _tile(D, 512, 128)
    a_spec = pl.BlockSpec((None, tm, F), lambda i, j, k: (k, i, 0))
    b_spec = pl.BlockSpec((2, tn, Fh), lambda i, j, k: (k, j, 0))
    terms = [(0, lambda r: r[:, 0:Fh], 1, lambda r: r[0]), (0, lambda r: r[:, Fh:F], 1, lambda r: r[1])]
    out = (jax.ShapeDtypeStruct((L, D), bf16), pl.BlockSpec((tm, tn), lambda i, j, k: (i, j)))
    return _mm("ffn_up_dx", [(dhh, a_spec), (w_up, b_spec)], [(0, terms), (1, terms)], grid=(L // tm, D // tn, 2),
               contract=_NT, outs=[out], acc_shape=(tm, tn))[0]


def _rms_fwd(name, x, g):
    L, D = x.shape
    tm = _tile(L, 512, 16)

    def body(x_ref, g_ref, xn_ref, r_ref):
        xv = x_ref[...]
        r = lax.rsqrt(jnp.mean(xv * xv, axis=-1, keepdims=True) + _EPS)
        xn_ref[...] = (xv * r * g_ref[...]).astype(bf16)
        r_ref[...] = r

    return pl.pallas_call(
        body, grid=(L // tm,),
        in_specs=[pl.BlockSpec((tm, D), lambda i: (i, 0)), pl.BlockSpec((1, D), lambda i: (0, 0))],
        out_specs=[pl.BlockSpec((tm, D), lambda i: (i, 0)), pl.BlockSpec((tm, 1), lambda i: (i, 0))],
        out_shape=[jax.ShapeDtypeStruct((L, D), bf16), jax.ShapeDtypeStruct((L, 1), f32)],
        name=name, compiler_params=_cparams(("parallel",)))(x, g)


def _rms_bwd(name, dxn, h, r, g, dres, after=()):
    L, D = h.shape
    tm = _tile(L, 512, 16)

    def body(dxn_ref, h_ref, r_ref, g_ref, dres_ref, *rest):
        dh_ref, dhb_ref, dg_ref = rest[len(after):]
        i = pl.program_id(0)
        d = dxn_ref[...].astype(f32)
        hv = h_ref[...]
        rv = r_ref[...]
        dyg = d * g_ref[...]
        m = jnp.mean(dyg * hv, axis=-1, keepdims=True)
        dh = dres_ref[...] + rv * dyg - hv * (rv * rv * rv) * m
        dh_ref[...] = dh
        dhb_ref[...] = dh.astype(bf16)

        @pl.when(i == 0)
        def _():
            dg_ref[...] = jnp.zeros_like(dg_ref)

        dg_ref[...] += jnp.sum(d * hv * rv, axis=0, keepdims=True)

    row = lambda i: (i, 0)
    return pl.pallas_call(
        body, grid=(L // tm,),
        in_specs=[pl.BlockSpec((tm, D), row), pl.BlockSpec((tm, D), row), pl.BlockSpec((tm, 1), row),
                  pl.BlockSpec((1, D), lambda i: (0, 0)), pl.BlockSpec((tm, D), row)] + [_ANY] * len(after),
        out_specs=[pl.BlockSpec((tm, D), row), pl.BlockSpec((tm, D), row), pl.BlockSpec((1, D), lambda i: (0, 0))],
        out_shape=[jax.ShapeDtypeStruct((L, D), f32), jax.ShapeDtypeStruct((L, D), bf16), jax.ShapeDtypeStruct((1, D), f32)],
        name=name, compiler_params=_cparams(("arbitrary",)))(dxn, h, r, g, dres, *after)


def _loss_head(name, h2, tgt, g):
    L, D = h2.shape
    tm = _tile(L, 512, 16)

    def body(h_ref, t_ref, g_ref, dh_ref, dhb_ref, dg_ref, loss_ref):
        i = pl.program_id(0)
        hv = h_ref[...]
        gv = g_ref[...]
        r = lax.rsqrt(jnp.mean(hv * hv, axis=-1, keepdims=True) + _EPS)
        err = hv * r * gv - t_ref[...]
        dy = err * (1.0 / D)
        dyg = dy * gv
        m = jnp.mean(dyg * hv, axis=-1, keepdims=True)
        dh = r * dyg - hv * (r * r * r) * m
        dh_ref[...] = dh
        dhb_ref[...] = dh.astype(bf16)

        @pl.when(i == 0)
        def _():
            dg_ref[...] = jnp.zeros_like(dg_ref)
            loss_ref[...] = jnp.zeros_like(loss_ref)

        dg_ref[...] += jnp.sum(dy * hv * r, axis=0, keepdims=True)
        part = jnp.sum(jnp.sum(err * err, axis=-1, keepdims=True), axis=0, keepdims=True) * (0.5 / D)
        loss_ref[...] += jnp.broadcast_to(part, (8, 128))

    row = lambda i: (i, 0)
    return pl.pallas_call(
        body, grid=(L // tm,),
        in_specs=[pl.BlockSpec((tm, D), row), pl.BlockSpec((tm, D), row), pl.BlockSpec((1, D), lambda i: (0, 0))],
        out_specs=[pl.BlockSpec((tm, D), row), pl.BlockSpec((tm, D), row), pl.BlockSpec((1, D), lambda i: (0, 0)),
                   pl.BlockSpec((8, 128), lambda i: (0, 0))],
        out_shape=[jax.ShapeDtypeStruct((L, D), f32), jax.ShapeDtypeStruct((L, D), bf16),
                   jax.ShapeDtypeStruct((1, D), f32), jax.ShapeDtypeStruct((8, 128), f32)],
        name=name, compiler_params=_cparams(("arbitrary",)))(h2, tgt, g)


def _shift_down(tile, halo, k, rows8):
    tm = tile.shape[0]
    r = pltpu.roll(tile, k, axis=0)
    hh = pltpu.roll(halo, k, axis=0)
    top = jnp.where(rows8 < k, hh, r[:8])
    return jnp.concatenate([top, r[8:]], axis=0) if tm > 8 else top


def _shift_up(tile, halo, k, rows8):
    tm = tile.shape[0]
    r = pltpu.roll(tile, tm - k, axis=0)
    hh = pltpu.roll(halo, 8 - k, axis=0)
    bot = jnp.where(rows8 >= 8 - k, hh, r[tm - 8:])
    return jnp.concatenate([r[:tm - 8], bot], axis=0) if tm > 8 else bot


def _conv3(x, halo, w_ref, b_ref, rows8):
    return (w_ref[0:1, :] * _shift_down(x, halo, 2, rows8) + w_ref[1:2, :] * _shift_down(x, halo, 1, rows8)
            + w_ref[2:3, :] * x + b_ref[...])


def _convb_fwd(cfg, proj, w, b):
    L, CW = cfg.L, cfg.CW
    assert cfg.SW == CW
    tm = _tile(L, 512, 16)

    def body(v_ref, vh_ref, gb_ref, gc_ref, gch_ref, w_ref, b_ref, o_ref):
        i = pl.program_id(0)
        rows8 = lax.broadcasted_iota(jnp.int32, (8, CW), 0)
        cv = gc_ref[...].astype(f32) * v_ref[...].astype(f32)
        cvh = gch_ref[...].astype(f32)[8:] * vh_ref[...].astype(f32)[8:]
        cvh = jnp.where(i == 0, 0.0, cvh)
        cc = _conv3(cv, cvh, w_ref, b_ref, rows8)
        o_ref[...] = (gb_ref[...].astype(f32) * cc).astype(bf16)

    blk = lambda col: pl.BlockSpec((tm, CW), lambda i: (i, col))
    halo = lambda col: pl.BlockSpec((16, CW), lambda i: (jnp.maximum(i * (tm // 16) - 1, 0), col))
    return pl.pallas_call(
        body, grid=(L // tm,),
        in_specs=[blk(1), halo(1), blk(2), blk(3), halo(3),
                  pl.BlockSpec((3, CW), lambda i: (0, 0)), pl.BlockSpec((1, CW), lambda i: (0, 0))],
        out_specs=pl.BlockSpec((tm, CW), lambda i: (i, 0)),
        out_shape=jax.ShapeDtypeStruct((L, CW), bf16),
        name="convb_fwd", compiler_params=_cparams(("parallel",)))(proj, proj, proj, proj, proj, w, b)


def _convb_bwd(cfg, proj, dyb0, w, b, dproj):
    L, CW = cfg.L, cfg.CW
    tm = _tile(L, 512, 16)
    nt = L // tm

    def body(v_ref, vh_ref, gb_ref, gbn_ref, gc_ref, gch_ref, d_ref, dn_ref, w_ref, b_ref, dproj_ref,
             o_ref, dw_ref, db_ref):
        i = pl.program_id(0)
        rows8 = lax.broadcasted_iota(jnp.int32, (8, CW), 0)
        v = v_ref[...].astype(f32)
        gb = gb_ref[...].astype(f32)
        gc = gc_ref[...].astype(f32)
        d = d_ref[...].astype(f32)
        cv = gc * v
        cvh = gch_ref[...].astype(f32)[8:] * vh_ref[...].astype(f32)[8:]
        cvh = jnp.where(i == 0, 0.0, cvh)
        s2 = _shift_down(cv, cvh, 2, rows8)
        s1 = _shift_down(cv, cvh, 1, rows8)
        cc = w_ref[0:1, :] * s2 + w_ref[1:2, :] * s1 + w_ref[2:3, :] * cv + b_ref[...]
        dcc = d * gb
        dccn = dn_ref[...].astype(f32)[:8] * gbn_ref[...].astype(f32)[:8]
        dccn = jnp.where(i == nt - 1, 0.0, dccn)
        dcv = (w_ref[2:3, :] * dcc + w_ref[1:2, :] * _shift_up(dcc, dccn, 1, rows8)
               + w_ref[0:1, :] * _shift_up(dcc, dccn, 2, rows8))
        o_ref[:, 0:CW] = (dcv * gc).astype(bf16)
        o_ref[:, CW:2 * CW] = (d * cc).astype(bf16)
        o_ref[:, 2 * CW:3 * CW] = (dcv * v).astype(bf16)

        @pl.when(i == 0)
        def _():
            dw_ref[...] = jnp.zeros_like(dw_ref)
            db_ref[...] = jnp.zeros_like(db_ref)

        dw_ref[0:1, :] += jnp.sum(dcc * s2, axis=0, keepdims=True)
        dw_ref[1:2, :] += jnp.sum(dcc * s1, axis=0, keepdims=True)
        dw_ref[2:3, :] += jnp.sum(dcc * cv, axis=0, keepdims=True)
        db_ref[...] += jnp.sum(dcc, axis=0, keepdims=True)

    blk = lambda col: pl.BlockSpec((tm, CW), lambda i: (i, col))
    prev = lambda col: pl.BlockSpec((16, CW), lambda i: (jnp.maximum(i * (tm // 16) - 1, 0), col))
    nxt = lambda col: pl.BlockSpec((16, CW), lambda i: (jnp.minimum((i + 1) * (tm // 16), L // 16 - 1), col))
    const = lambda r: pl.BlockSpec((r, CW), lambda i: (0, 0))
    return pl.pallas_call(
        body, grid=(nt,),
        in_specs=[blk(1), prev(1), blk(2), nxt(2), blk(3), prev(3), blk(0), nxt(0), const(3), const(1),
                  pl.BlockSpec(memory_space=pl.ANY)],
        out_specs=[pl.BlockSpec((tm, 3 * CW), lambda i: (i, 0)), const(3), const(1)],
        out_shape=[jax.ShapeDtypeStruct(dproj.shape, bf16), jax.ShapeDtypeStruct((3, CW), f32),
                   jax.ShapeDtypeStruct((1, CW), f32)],
        input_output_aliases={10: 0},
        name="convb_bwd", compiler_params=_cparams(("arbitrary",)))(proj, proj, proj, proj, proj, proj, dyb0, dyb0, w, b,
                                                                    dproj)


def _ffn_act(cfg, hh, w, b):
    L, F = cfg.L, cfg.F
    tm = _tile(L, 512, 16)
    tc = _tile(F, 1408, 128)
    ncb = F // tc

    def body(a_ref, ah_ref, g_ref, w_ref, b_ref, o_ref, act_ref):
        i = pl.program_id(0)
        rows8 = lax.broadcasted_iota(jnp.int32, (8, tc), 0)
        a = a_ref[...].astype(f32)
        ah = jnp.where(i == 0, 0.0, ah_ref[...].astype(f32)[8:])
        act = _conv3(a, ah, w_ref, b_ref, rows8)
        act_ref[...] = act.astype(bf16)
        o_ref[...] = (_gelu(act) * g_ref[...].astype(f32)).astype(bf16)

    tile = pl.BlockSpec((tm, tc), lambda i, j: (i, j))
    return pl.pallas_call(
        body, grid=(L // tm, ncb),
        in_specs=[tile, pl.BlockSpec((16, tc), lambda i, j: (jnp.maximum(i * (tm // 16) - 1, 0), j)),
                  pl.BlockSpec((tm, tc), lambda i, j: (i, j + ncb)),
                  pl.BlockSpec((3, tc), lambda i, j: (0, j)), pl.BlockSpec((1, tc), lambda i, j: (0, j))],
        out_specs=[tile, tile], out_shape=[jax.ShapeDtypeStruct((L, F), bf16)] * 2,
        name="ffn_act", compiler_params=_cparams(("parallel", "parallel")))(hh, hh, hh, w, b)


def _ffn_down_dx_act_bwd(cfg, dh2b, w_down, hh, act, w):
    L, D, F = cfg.L, cfg.D, cfg.F
    S, Ks, _ = w_down.shape
    tm = _tile(L, 512, 16)
    tc = _tile(Ks, 1408, 128)
    ncb = F // tc
    nbs = Ks // tc
    nt = L // tm

    slabs = [(c0, min(256, tc - c0)) for c0 in range(0, tc, 256)]

    def body(dh_ref, wd_ref, a_ref, c_ref, g_ref, w_ref, dhh_ref, dw_ref, db_ref, below):
        i = pl.program_id(1)

        @pl.when(i == 0)
        def _():
            below[...] = jnp.zeros_like(below)
            dw_ref[...] = jnp.zeros_like(dw_ref)
            db_ref[...] = jnp.zeros_like(db_ref)

        dh = dh_ref[...]
        for c0, cw in slabs:
            cols = slice(c0, c0 + cw)
            rows8 = lax.broadcasted_iota(jnp.int32, (8, cw), 0)
            d = lax.dot_general(dh, wd_ref[c0:c0 + cw, :], _NT, preferred_element_type=f32)
            a = a_ref[:, cols].astype(f32)
            gelu, gelu_d = _gelu_and_grad(c_ref[:, cols].astype(f32))
            dhh_ref[1, :, cols] = (d * gelu).astype(bf16)
            dact = d * g_ref[:, cols].astype(f32) * gelu_d
            dactn = below[:, cols]
            up1 = _shift_up(dact, dactn, 1, rows8)
            up2 = _shift_up(dact, dactn, 2, rows8)
            below[:, cols] = dact[:8]
            dhh_ref[0, :, cols] = (w_ref[2:3, cols] * dact + w_ref[1:2, cols] * up1 + w_ref[0:1, cols] * up2).astype(bf16)
            dw_ref[0:1, cols] += jnp.sum(up2 * a, axis=0, keepdims=True)
            dw_ref[1:2, cols] += jnp.sum(up1 * a, axis=0, keepdims=True)
            dw_ref[2:3, cols] += jnp.sum(dact * a, axis=0, keepdims=True)
            db_ref[:, cols] += jnp.sum(dact, axis=0, keepdims=True)

    up = lambda i: nt - 1 - i
    blk = lambda off: pl.BlockSpec((tm, tc), lambda j, i: (up(i), j + off))
    const = lambda r: pl.BlockSpec((r, tc), lambda j, i: (0, j))
    return pl.pallas_call(
        body, grid=(ncb, nt),
        in_specs=[pl.BlockSpec((tm, D), lambda j, i: (up(i), 0)),
                  pl.BlockSpec((None, tc, D), lambda j, i: (j // nbs, j % nbs, 0)),
                  blk(0), blk(0), blk(ncb), const(3)],
        out_specs=[pl.BlockSpec((2, tm, tc), lambda j, i: (0, up(i), j)), const(3), const(1)],
        out_shape=[jax.ShapeDtypeStruct((2, L, F), bf16),
                   jax.ShapeDtypeStruct((3, F), f32), jax.ShapeDtypeStruct((1, F), f32)],
        scratch_shapes=[pltpu.VMEM((8, tc), f32)],
        name="ffn_down_dx_act_bwd", compiler_params=_cparams(("parallel", "arbitrary")))(dh2b, w_down, hh, act, hh, w)


def _merge_fwd(cfg, ya1, yb0, proj, wso, wco):
    L, D, SW, CW = cfg.L, cfg.D, cfg.SW, cfg.CW
    Ns = D // _NCHIP
    tm = _tile(L, 1024, 16)
    tn = _tile(Ns, 512, 128)
    nb = Ns // tn
    off_a = (SW + 3 * CW) // tn
    off_b = (SW + 3 * CW + D) // tn

    def body(a_ref, b_ref, wa_ref, wb_ref, ma_ref, mb_ref, m_ref, ya_ref, yb_ref):
        ya = jnp.dot(a_ref[...], wa_ref[...], preferred_element_type=f32)
        yb = jnp.dot(b_ref[...], wb_ref[...], preferred_element_type=f32)
        sa = jax.nn.sigmoid(ma_ref[...].astype(f32))
        sb = jax.nn.sigmoid(mb_ref[...].astype(f32))
        m_ref[...] = (sa * ya + sb * yb).astype(bf16)
        ya_ref[...] = ya.astype(bf16)
        yb_ref[...] = yb.astype(bf16)

    o_spec = pl.BlockSpec((tm, tn), lambda i, j: (i, j))
    o_shape = jax.ShapeDtypeStruct((L, D), bf16)
    return pl.pallas_call(
        body, grid=(L // tm, D // tn),
        in_specs=[pl.BlockSpec((tm, SW), lambda i, j: (i, 0)), pl.BlockSpec((tm, CW), lambda i, j: (i, 0)),
                  pl.BlockSpec((None, SW, tn), lambda i, j: (j // nb, 0, j % nb)),
                  pl.BlockSpec((None, CW, tn), lambda i, j: (j // nb, 0, j % nb)),
                  pl.BlockSpec((tm, tn), lambda i, j: (i, off_a + j)), pl.BlockSpec((tm, tn), lambda i, j: (i, off_b + j))],
        out_specs=[o_spec, o_spec, o_spec], out_shape=[o_shape, o_shape, o_shape],
        name="merge_fwd", compiler_params=_cparams(("parallel", "parallel")))(ya1, yb0, wso, wco, proj, proj)


def _s5_dims(cfg):
    G = cfg.SW // _SSM_GROUP
    NS = G // _SLAB_GROUPS
    SC = _SLAB_GROUPS * _SSM_GROUP
    SH = _SLAB_GROUPS * _SSM_STATE
    NST = 2 * SH * NS
    return G, NS, SC, SH, NST


def _lane_slabs(cfg, W):
    _, NS, _, SH, _ = _s5_dims(cfg)
    return [(2 * SH * s + w0, 2 * SH * s + SH + w0) for s in range(NS) for w0 in range(0, SH, W)]


def _discretize(a_re, a_im, log_dt, b_re, b_im):
    dt = jnp.exp(log_dt)[:, None]
    mag = jnp.exp(dt * a_re)
    abr = mag * jnp.cos(dt * a_im)
    abi = mag * jnp.sin(dt * a_im)
    nr = abr - 1.0
    ni = abi
    den = a_re * a_re + a_im * a_im
    fr = (nr * a_re + ni * a_im) / den
    fi = (ni * a_re - nr * a_im) / den
    bbr = fr[..., None] * b_re - fi[..., None] * b_im
    bbi = fr[..., None] * b_im + fi[..., None] * b_re
    return abr, abi, bbr, bbi


def _state_rows(cfg, re, im):
    _, NS, _, SH, _ = _s5_dims(cfg)
    return jnp.concatenate([re.reshape(NS, SH), im.reshape(NS, SH)], axis=1).reshape(-1)


def _s5_tables(cfg, a_re, a_im, log_dt, abr, abi, bbr, bbi, c_re, c_im):
    G, NS, SC, SH, NST = _s5_dims(cfg)
    S = cfg.T // 8
    eye = jnp.eye(_SLAB_GROUPS, dtype=bf16)
    bb = jnp.stack([bbr, bbi]).reshape(2, NS, _SLAB_GROUPS, _SSM_STATE, _SSM_GROUP).astype(bf16)
    bs = (bb.transpose(1, 2, 4, 0, 3)[:, :, :, :, None, :] * eye[None, :, None, None, :, None]).reshape(NS, SC, 2 * SH)
    cc = jnp.stack([c_re, -c_im]).reshape(2, NS, _SLAB_GROUPS, _SSM_GROUP, _SSM_STATE).astype(bf16)
    cs = (cc.transpose(1, 0, 4, 2, 3)[:, :, None, :, :, :] * eye[None, None, :, None, :, None]).reshape(NS, 2 * SH, SC)
    arep = jnp.broadcast_to(_state_rows(cfg, abr, abi)[None, :], (8, NST))
    sdt = S * jnp.exp(log_dt)[:, None]
    mag = jnp.exp(sdt * a_re)
    apow = jnp.broadcast_to(_state_rows(cfg, mag * jnp.cos(sdt * a_im), mag * jnp.sin(sdt * a_im))[None, :], (8, NST))
    t = jnp.arange(cfg.T)
    perm = (t % 8) * S + t // 8
    pm = jax.nn.one_hot(perm, cfg.T, dtype=bf16)
    return bs, cs, arep, apow, pm, pm.T


def _cmul_add(ar, ai, xr, xi, br, bi):
    return ar * xr - ai * xi + br, ar * xi + ai * xr + bi


def _s5_forward_chunk(cfg, W, upb, bs_ref, arep_ref, apow_ref, st, x0, cin_store):
    _, NS, SC, SH, _ = _s5_dims(cfg)
    S = cfg.T // 8
    for s in range(NS):
        st[:, 2 * SH * s:2 * SH * (s + 1)] = jnp.dot(upb[:, SC * s:SC * (s + 1)], bs_ref[s], preferred_element_type=f32)
    rows = lax.broadcasted_iota(jnp.int32, (8, W), 0)
    zero = jnp.zeros((8, W), f32)
    for rc, ic in _lane_slabs(cfg, W):
        ar = arep_ref[:, rc:rc + W]
        ai = arep_ref[:, ic:ic + W]

        def step(i, carry, rc=rc, ic=ic, ar=ar, ai=ai):
            xr, xi = carry
            r0 = pl.multiple_of(i * 8, 8)
            nr, ni = _cmul_add(ar, ai, xr, xi, st[pl.ds(r0, 8), rc:rc + W], st[pl.ds(r0, 8), ic:ic + W])
            st[pl.ds(r0, 8), rc:rc + W] = nr
            st[pl.ds(r0, 8), ic:ic + W] = ni
            return nr, ni

        er, ei = lax.fori_loop(0, S, step, (zero, zero))
        pr = apow_ref[:, rc:rc + W]
        pi = apow_ref[:, ic:ic + W]
        x0r = x0[:, rc:rc + W]
        x0i = x0[:, ic:ic + W]
        cr = jnp.where(rows == 0, x0r, 0.0)
        ci = jnp.where(rows == 0, x0i, 0.0)
        for _ in range(7):
            fr, fi = _cmul_add(pr, pi, cr, ci, er, ei)
            cr = jnp.where(rows == 0, x0r, pltpu.roll(fr, 1, axis=0))
            ci = jnp.where(rows == 0, x0i, pltpu.roll(fi, 1, axis=0))
        fr, fi = _cmul_add(pr, pi, cr, ci, er, ei)
        x0[:, rc:rc + W] = jnp.broadcast_to(fr[7:8, :], (8, W))
        x0[:, ic:ic + W] = jnp.broadcast_to(fi[7:8, :], (8, W))
        cin_store(rc, ic, cr, ci)

        def fix(i, carry, rc=rc, ic=ic, ar=ar, ai=ai):
            kr, ki = carry
            r0 = pl.multiple_of(i * 8, 8)
            nr, ni = ar * kr - ai * ki, ar * ki + ai * kr
            st[pl.ds(r0, 8), rc:rc + W] = st[pl.ds(r0, 8), rc:rc + W] + nr
            st[pl.ds(r0, 8), ic:ic + W] = st[pl.ds(r0, 8), ic:ic + W] + ni
            return nr, ni

        lax.fori_loop(0, S, fix, (cr, ci))


def _s5_fwd(cfg, proj, tabs, dskip):
    L, SW, T = cfg.L, cfg.SW, cfg.T
    G, NS, SC, SH, NST = _s5_dims(cfg)
    bs, cs, arep, apow, pm, pmt = tabs
    W = min(512, SH)
    NC = L // T

    def body(u_ref, pm_ref, pmt_ref, bs_ref, cs_ref, arep_ref, apow_ref, dskip_ref, y_ref, ya0_ref, cin_ref, st, x0):
        c = pl.program_id(0)

        @pl.when(c == 0)
        def _():
            x0[...] = jnp.zeros_like(x0)

        up = jnp.dot(pm_ref[...], u_ref[...], preferred_element_type=f32)
        upb = up.astype(bf16)

        def cin_store(rc, ic, cr, ci):
            cin_ref[0, :, rc:rc + W] = cr
            cin_ref[0, :, ic:ic + W] = ci

        _s5_forward_chunk(cfg, W, upb, bs_ref, arep_ref, apow_ref, st, x0, cin_store)
        yp = jnp.concatenate(
            [jnp.dot(st[:, 2 * SH * s:2 * SH * (s + 1)].astype(bf16), cs_ref[s], preferred_element_type=f32)
             for s in range(NS)], axis=1) + dskip_ref[...] * up
        y = jnp.dot(pmt_ref[...], yp.astype(bf16), preferred_element_type=f32)
        y_ref[...] = y.astype(bf16)
        ya0_ref[...] = _gelu(y).astype(bf16)

    const2 = lambda shape: pl.BlockSpec(shape, lambda c: (0, 0))
    const3 = lambda shape: pl.BlockSpec(shape, lambda c: (0, 0, 0))
    return pl.pallas_call(
        body, grid=(NC,),
        in_specs=[pl.BlockSpec((T, SW), lambda c: (c, 0)), const2((T, T)), const2((T, T)), const3((NS, SC, 2 * SH)),
                  const3((NS, 2 * SH, SC)), const2((8, NST)), const2((8, NST)), const2((1, SW))],
        out_specs=[pl.BlockSpec((T, SW), lambda c: (c, 0)), pl.BlockSpec((T, SW), lambda c: (c, 0)),
                   pl.BlockSpec((1, 8, NST), lambda c: (c, 0, 0)), pl.BlockSpec((T, NST), lambda c: (c, 0))],
        out_shape=[jax.ShapeDtypeStruct((L, SW), bf16), jax.ShapeDtypeStruct((L, SW), bf16),
                   jax.ShapeDtypeStruct((NC, 8, NST), f32), jax.ShapeDtypeStruct((L, NST), f32)],
        scratch_shapes=[pltpu.VMEM((8, NST), f32)],
        name="s5_fwd", compiler_params=_cparams(("arbitrary",)))(proj, pm, pmt, bs, cs, arep, apow, dskip)


def _s5_bwd(cfg, proj, dy, cin, xs, tabs, dskip, dproj, after=()):
    L, SW, T = cfg.L, cfg.SW, cfg.T
    du_col = 3 * cfg.CW // SW
    G, NS, SC, SH, NST = _s5_dims(cfg)
    bs, cs, arep, apow, pm, pmt = tabs
    W = min(512, SH)
    S = T // 8
    NC = L // T

    def body(u_ref, dy_ref, cin_ref, st, pm_ref, pmt_ref, bs_ref, cs_ref, arep_ref, apow_ref, dskip_ref, dproj_ref, *rest):
        du_ref, da_ref, db_ref, dc_ref, dd_ref, gs, g0, db_acc, dc_acc = rest[len(after):]
        c = pl.program_id(0)

        @pl.when(c == 0)
        def _():
            g0[...] = jnp.zeros_like(g0)
            da_ref[...] = jnp.zeros_like(da_ref)
            dd_ref[...] = jnp.zeros_like(dd_ref)
            db_acc[...] = jnp.zeros_like(db_acc)
            dc_acc[...] = jnp.zeros_like(dc_acc)

        up = jnp.dot(pm_ref[...], u_ref[...], preferred_element_type=f32)
        upb = up.astype(bf16)
        dyp = jnp.dot(pm_ref[...], dy_ref[...], preferred_element_type=f32)
        dypb = dyp.astype(bf16)
        for s in range(NS):
            gs[:, 2 * SH * s:2 * SH * (s + 1)] = lax.dot_general(
                dypb[:, SC * s:SC * (s + 1)], cs_ref[s], (((1,), (1,)), ((), ())), preferred_element_type=f32)
        rows = lax.broadcasted_iota(jnp.int32, (8, W), 0)
        zero = jnp.zeros((8, W), f32)
        for rc, ic in _lane_slabs(cfg, W):
            ar = arep_ref[:, rc:rc + W]
            ai = arep_ref[:, ic:ic + W]

            def rstep(k, carry, rc=rc, ic=ic, ar=ar, ai=ai):
                gr, gi = carry
                r0 = pl.multiple_of((S - 1 - k) * 8, 8)
                nr = ar * gr + ai * gi + gs[pl.ds(r0, 8), rc:rc + W]
                ni = ar * gi - ai * gr + gs[pl.ds(r0, 8), ic:ic + W]
                gs[pl.ds(r0, 8), rc:rc + W] = nr
                gs[pl.ds(r0, 8), ic:ic + W] = ni
                return nr, ni

            er, ei = lax.fori_loop(0, S, rstep, (zero, zero))
            pr = apow_ref[:, rc:rc + W]
            pi = apow_ref[:, ic:ic + W]
            g0r = g0[:, rc:rc + W]
            g0i = g0[:, ic:ic + W]
            cr = jnp.where(rows == 7, g0r, 0.0)
            ci = jnp.where(rows == 7, g0i, 0.0)
            for _ in range(7):
                fr = er + pr * cr + pi * ci
                fi = ei + pr * ci - pi * cr
                cr = jnp.where(rows == 7, g0r, pltpu.roll(fr, 7, axis=0))
                ci = jnp.where(rows == 7, g0i, pltpu.roll(fi, 7, axis=0))
            fr = er + pr * cr + pi * ci
            fi = ei + pr * ci - pi * cr
            g0[:, rc:rc + W] = jnp.broadcast_to(fr[0:1, :], (8, W))
            g0[:, ic:ic + W] = jnp.broadcast_to(fi[0:1, :], (8, W))

            def fix(k, carry, rc=rc, ic=ic, ar=ar, ai=ai):
                kr, ki, accr, acci = carry
                i = S - 1 - k
                r0 = pl.multiple_of(i * 8, 8)
                rp = pl.multiple_of((i - 1) * 8, 8)
                nr = ar * kr + ai * ki
                ni = ar * ki - ai * kr
                gr = gs[pl.ds(r0, 8), rc:rc + W] + nr
                gi = gs[pl.ds(r0, 8), ic:ic + W] + ni
                gs[pl.ds(r0, 8), rc:rc + W] = gr
                gs[pl.ds(r0, 8), ic:ic + W] = gi
                xr = st[pl.ds(rp, 8), rc:rc + W]
                xi = st[pl.ds(rp, 8), ic:ic + W]
                return nr, ni, accr + gr * xr + gi * xi, acci + gi * xr - gr * xi

            kr, ki, accr, acci = lax.fori_loop(0, S - 1, fix, (cr, ci, zero, zero))
            nr = ar * kr + ai * ki
            ni = ar * ki - ai * kr
            gr = gs[0:8, rc:rc + W] + nr
            gi = gs[0:8, ic:ic + W] + ni
            gs[0:8, rc:rc + W] = gr
            gs[0:8, ic:ic + W] = gi
            xr = cin_ref[0, :, rc:rc + W]
            xi = cin_ref[0, :, ic:ic + W]
            da_ref[:, rc:rc + W] += accr + gr * xr + gi * xi
            da_ref[:, ic:ic + W] += acci + gi * xr - gr * xi

        dups = []
        for s in range(NS):
            gsb = gs[:, 2 * SH * s:2 * SH * (s + 1)].astype(bf16)
            dups.append(lax.dot_general(gsb, bs_ref[s], (((1,), (1,)), ((), ())), preferred_element_type=f32))
            db_acc[s] += lax.dot_general(upb[:, SC * s:SC * (s + 1)], gsb, (((0,), (0,)), ((), ())),
                                         preferred_element_type=f32)
            dc_acc[s] += lax.dot_general(st[:, 2 * SH * s:2 * SH * (s + 1)].astype(bf16), dypb[:, SC * s:SC * (s + 1)],
                                         (((0,), (0,)), ((), ())), preferred_element_type=f32)
        dup = jnp.concatenate(dups, axis=1) + dskip_ref[...] * dyp
        du_ref[...] = jnp.dot(pmt_ref[...], dup.astype(bf16), preferred_element_type=f32).astype(bf16)
        dd_ref[...] += jnp.sum(dyp * up, axis=0, keepdims=True)

        @pl.when(c == NC - 1)
        def _():
            PS, GH = _SSM_STATE, _SSM_GROUP
            mask_b = (lax.broadcasted_iota(jnp.int32, (SC, SH), 0) // GH
                      == lax.broadcasted_iota(jnp.int32, (SC, SH), 1) // PS)
            mask_c = (lax.broadcasted_iota(jnp.int32, (SH, SC), 0) // PS
                      == lax.broadcasted_iota(jnp.int32, (SH, SC), 1) // GH)
            for s in range(NS):
                for r in range(2):
                    xb = jnp.where(mask_b, db_acc[s, :, r * SH:(r + 1) * SH], 0.0)
                    zb = xb[:, 0:128]
                    for q in range(1, SH // 128):
                        zb = zb + xb[:, q * 128:(q + 1) * 128]
                    db_ref[s, r] = zb + pltpu.roll(zb, PS, axis=1)
                    xc = jnp.where(mask_c, dc_acc[s, r * SH:(r + 1) * SH, :], 0.0)
                    zc = xc[0:PS]
                    for q in range(1, SH // PS):
                        zc = zc + xc[q * PS:(q + 1) * PS]
                    dc_ref[s, r] = zc

    rev = lambda c: (NC - 1 - c, 0)
    const2 = lambda shape: pl.BlockSpec(shape, lambda c: (0, 0))
    const3 = lambda shape: pl.BlockSpec(shape, lambda c: (0, 0, 0))
    const4 = lambda shape: pl.BlockSpec(shape, lambda c: (0, 0, 0, 0))
    return pl.pallas_call(
        body, grid=(NC,),
        in_specs=[pl.BlockSpec((T, SW), rev), pl.BlockSpec((T, SW), rev), pl.BlockSpec((1, 8, NST), lambda c: (NC - 1 - c, 0, 0)),
                  pl.BlockSpec((T, NST), rev), const2((T, T)), const2((T, T)),
                  pl.BlockSpec((NS, SC, 2 * SH), lambda c: (0, 0, 0), pipeline_mode=pl.Buffered(1)),
                  pl.BlockSpec((NS, 2 * SH, SC), lambda c: (0, 0, 0), pipeline_mode=pl.Buffered(1)),
                  const2((8, NST)), const2((8, NST)), const2((1, SW)), _ANY] + [_ANY] * len(after),
        out_specs=[pl.BlockSpec((T, SW), lambda c: (NC - 1 - c, du_col)), const2((8, NST)),
                   const4((NS, 2, SC, 128)), const4((NS, 2, _SSM_STATE, SC)), const2((1, SW))],
        out_shape=[jax.ShapeDtypeStruct(dproj.shape, bf16), jax.ShapeDtypeStruct((8, NST), f32),
                   jax.ShapeDtypeStruct((NS, 2, SC, 128), f32), jax.ShapeDtypeStruct((NS, 2, _SSM_STATE, SC), f32),
                   jax.ShapeDtypeStruct((1, SW), f32)],
        scratch_shapes=[pltpu.VMEM((T, NST), f32), pltpu.VMEM((8, NST), f32),
                        pltpu.VMEM((NS, SC, 2 * SH), f32), pltpu.VMEM((NS, 2 * SH, SC), f32)],
        input_output_aliases={11: 0},
        name="s5_bwd", compiler_params=_cparams(("arbitrary",), _VMEM_S5_BWD))(proj, dy, cin, xs, pm, pmt, bs, cs, arep, apow, dskip, dproj,
                                                                  *after)


def _s5_param_grads(cfg, da, db_diag, dc_diag):
    G, NS, SC, SH, NST = _s5_dims(cfg)
    das = da.sum(axis=0).reshape(NS, 2, SH)
    dabr = das[:, 0].reshape(G, _SSM_STATE)
    dabi = das[:, 1].reshape(G, _SSM_STATE)
    dbd = db_diag[..., :_SSM_STATE].reshape(NS, 2, _SLAB_GROUPS, _SSM_GROUP, _SSM_STATE)
    dbb = dbd.transpose(1, 0, 2, 4, 3).reshape(2, G, _SSM_STATE, _SSM_GROUP)
    dcd = dc_diag.reshape(NS, 2, _SSM_STATE, _SLAB_GROUPS, _SSM_GROUP)
    dcc = dcd.transpose(1, 0, 3, 4, 2).reshape(2, G, _SSM_GROUP, _SSM_STATE)
    return dabr, dabi, dbb[0], dbb[1], dcc[0], -dcc[1]


def _coords():
    return lax.axis_index("x"), lax.axis_index("y"), lax.axis_index("c")


def _other_chips(x, y):
    return [(1 - x, y), (x, 1 - y), (1 - x, 1 - y)]


def _allreduce8(name, v):
    R = v.shape[0]

    def body(v_ref, o_ref, sib, chips, mine, ssem, rsem):
        x, y, c = _coords()
        d2d = pltpu.make_async_remote_copy(src_ref=v_ref, dst_ref=sib, send_sem=ssem.at[0], recv_sem=rsem.at[0],
                                           device_id=(x, y, 1 - c), device_id_type=_MESH)
        d2d.start()
        d2d.wait()
        mine[...] = v_ref[...] + sib[...]
        cps = [pltpu.make_async_remote_copy(src_ref=mine, dst_ref=chips.at[j], send_sem=ssem.at[1 + j],
                                            recv_sem=rsem.at[1 + j], device_id=(*chip, c), device_id_type=_MESH)
               for j, chip in enumerate(_other_chips(x, y))]
        for cp in cps:
            cp.start()
        for cp in cps:
            cp.wait()
        o_ref[...] = (mine[...] + chips[1]) + (chips[0] + chips[2])

    vm = pl.BlockSpec(memory_space=pltpu.VMEM)
    return pl.pallas_call(
        body, in_specs=[vm], out_specs=vm, out_shape=jax.ShapeDtypeStruct((R, 128), f32),
        scratch_shapes=[pltpu.VMEM((R, 128), f32), pltpu.VMEM((3, R, 128), f32), pltpu.VMEM((R, 128), f32),
                        pltpu.SemaphoreType.DMA((4,)), pltpu.SemaphoreType.DMA((4,))],
        name=name, compiler_params=pltpu.CompilerParams(vmem_limit_bytes=_VMEM_LIMIT))(v)


def _cast_into_slot(name, w, k_idx, also_alone=False):
    R, C = w.shape
    tr = _tile(R, 256, 16)

    def body(k_ref, w_ref, *o_refs):
        for o_ref in o_refs:
            o_ref[...] = w_ref[...].astype(bf16)

    slot = (jax.ShapeDtypeStruct((_NCHIP, R, C), bf16), pl.BlockSpec((None, tr, C), lambda r, kr: (kr[0], r, 0)))
    alone = (jax.ShapeDtypeStruct((R, C), bf16), pl.BlockSpec((tr, C), lambda r, kr: (r, 0)))
    outs = [slot, alone] if also_alone else [slot]
    gs = pltpu.PrefetchScalarGridSpec(
        num_scalar_prefetch=1, grid=(R // tr,),
        in_specs=[pl.BlockSpec((tr, C), lambda r, kr: (r, 0))], out_specs=[o[1] for o in outs])
    res = pl.pallas_call(body, grid_spec=gs, out_shape=[o[0] for o in outs], name=name,
                         compiler_params=_cparams(("parallel",)))(k_idx, w)
    return res if also_alone else res[0]


def _in_proj(cfg, xn1, w_own, w_in, k_idx):
    L, D = xn1.shape
    S, _, Ns = w_in.shape
    tm, tn = _tile(L, 1024, 16), _tile(Ns, 1024, 128)
    nb = Ns // tn

    def body(k_ref, a_ref, w_ref, *rest):
        rest[-1][...] = jnp.dot(a_ref[...], w_ref[...], preferred_element_type=f32).astype(bf16)

    a_spec = pl.BlockSpec((tm, D), lambda i, j, kr: (i, 0))
    shape = jax.ShapeDtypeStruct((L, S * Ns), bf16)
    own = pltpu.PrefetchScalarGridSpec(
        num_scalar_prefetch=1, grid=(L // tm, nb),
        in_specs=[a_spec, pl.BlockSpec((D, tn), lambda i, j, kr: (0, j))],
        out_specs=pl.BlockSpec((tm, tn), lambda i, j, kr: (i, kr[0] * nb + j)))
    proj = pl.pallas_call(body, grid_spec=own, out_shape=shape, name="in_proj_own",
                          compiler_params=_cparams(("parallel", "parallel")))(k_idx, xn1, w_own)
    shard = lambda j, kr: (kr[0] + 1 + j // nb) % S
    rest = pltpu.PrefetchScalarGridSpec(
        num_scalar_prefetch=1, grid=(L // tm, (S - 1) * nb),
        in_specs=[a_spec, pl.BlockSpec((None, D, tn), lambda i, j, kr: (shard(j, kr), 0, j % nb)), _ANY],
        out_specs=pl.BlockSpec((tm, tn), lambda i, j, kr: (i, shard(j, kr) * nb + j % nb)))
    return pl.pallas_call(body, grid_spec=rest, out_shape=shape, input_output_aliases={3: 0}, name="in_proj",
                          compiler_params=_cparams(("parallel", "parallel")))(k_idx, xn1, w_in, proj)


def _handshake(peers):
    barrier = pltpu.get_barrier_semaphore()
    for peer in peers:
        pl.semaphore_signal(barrier, inc=1, device_id=peer, device_id_type=_MESH)
    pl.semaphore_wait(barrier, len(peers))


def _allgather_weights(name, bufs, collective_id):
    n = len(bufs)
    refs = [jax.new_ref(b, memory_space=pltpu.MemorySpace.HBM) for b in bufs]

    def copy(ref, sems, idx, to):
        return pltpu.make_async_remote_copy(src_ref=ref, dst_ref=ref, send_sem=sems[0].at[idx], recv_sem=sems[1].at[idx],
                                            device_id=to, device_id_type=_MESH)

    def launch(ssem, rsem, qssem, qrsem, fssem, frsem):
        x, y, c = _coords()
        k = 2 * x + y
        nbrs = [(1 - x, y), (x, 1 - y)]
        across = 2 * (1 - x) + (1 - y)
        sibling = (x, y, 1 - c)
        _handshake([sibling] + [(*chip, c) for chip in nbrs])
        started = []

        def start(cp):
            cp.start()
            started.append(cp)

        for w in range(n):
            rh = refs[w].shape[1] // 2
            for j, chip in enumerate(nbrs):
                start(copy(refs[w].at[k, pl.ds(c * rh, rh)], (ssem, rsem), (w, j), (*chip, c)))
        for w in range(n):
            rh = refs[w].shape[1] // 2
            rq = rh // 2
            for j, (ox, oy) in enumerate(nbrs):
                ko = 2 * ox + oy
                landed = refs[w].at[ko, pl.ds(c * rh, rh)]
                copy(landed, (ssem, rsem), (w, j), (ox, oy, c)).wait_recv()
                start(copy(refs[w].at[ko, pl.ds(c * rh + j * rq, rq)], (qssem, qrsem), (w, j), (*nbrs[1 - j], c)))
                start(copy(landed, (fssem, frsem), (w, j), sibling))
        for w in range(n):
            rh = refs[w].shape[1] // 2
            rq = rh // 2
            for q in range(2):
                quarter = refs[w].at[across, pl.ds(c * rh + q * rq, rq)]
                copy(quarter, (qssem, qrsem), (w, q), (*nbrs[1 - q], c)).wait_recv()
            start(copy(refs[w].at[across, pl.ds(c * rh, rh)], (fssem, frsem), (w, 2), sibling))
        for w in range(n):
            rh = refs[w].shape[1] // 2
            for j, ko in enumerate([2 * nbrs[0][0] + nbrs[0][1], 2 * nbrs[1][0] + nbrs[1][1], across]):
                copy(refs[w].at[ko, pl.ds((1 - c) * rh, rh)], (fssem, frsem), (w, j), sibling).wait_recv()
        for cp in started:
            cp.wait_send()

    _sequencer_kernel(name, collective_id,
                      (pltpu.SemaphoreType.DMA((n, 2)), pltpu.SemaphoreType.DMA((n, 2)), pltpu.SemaphoreType.DMA((n, 2)),
                       pltpu.SemaphoreType.DMA((n, 2)), pltpu.SemaphoreType.DMA((n, 3)), pltpu.SemaphoreType.DMA((n, 3))),
                      launch)
    return [r[...] for r in refs]


def _sequencer_kernel(name, collective_id, sems, body):
    pl.kernel(body, mesh=plsc.ScalarSubcoreMesh(axis_name="seq", num_cores=1), name=name, scratch_types=sems,
              compiler_params=pltpu.CompilerParams(collective_id=collective_id))()


def _hbm_ref(a):
    return jax.new_ref(a, memory_space=pltpu.MemorySpace.HBM)


def _exchange_halves(name, grads, collective_id):
    n = len(grads)
    srcs = [_hbm_ref(g) for g in grads]
    dsts = [jax.empty_ref(jax.ShapeDtypeStruct((g.shape[0], g.shape[1] // 2, g.shape[2]), g.dtype),
                          memory_space=pltpu.MemorySpace.HBM) for g in grads]

    def body(ssem, rsem):
        x, y, c = _coords()
        _handshake([(x, y, 1 - c)])
        cps = []
        for w in range(n):
            rh = srcs[w].shape[1] // 2
            cp = pltpu.make_async_remote_copy(
                src_ref=srcs[w].at[:, pl.ds((1 - c) * rh, rh)], dst_ref=dsts[w], send_sem=ssem.at[w], recv_sem=rsem.at[w],
                device_id=(x, y, 1 - c), device_id_type=_MESH)
            cp.start()
            cps.append(cp)
        for cp in cps:
            cp.wait()

    _sequencer_kernel(name, collective_id, (pltpu.SemaphoreType.DMA((n,)), pltpu.SemaphoreType.DMA((n,))), body)
    return [d[...] for d in dsts]


def _scatter_shards(name, parts, collective_id):
    n = len(parts)
    srcs = [_hbm_ref(p) for p in parts]
    dsts = [jax.empty_ref(jax.ShapeDtypeStruct((3,) + p.shape[1:], p.dtype), memory_space=pltpu.MemorySpace.HBM)
            for p in parts]

    def body(ssem, rsem):
        x, y, c = _coords()
        k = 2 * x + y
        others = _other_chips(x, y)
        _handshake([(*chip, c) for chip in others])
        cps = []
        for w in range(n):
            for j, (ox, oy) in enumerate(others):
                cp = pltpu.make_async_remote_copy(
                    src_ref=srcs[w].at[(2 * ox + oy - k + 3) % 4], dst_ref=dsts[w].at[j], send_sem=ssem.at[w, j],
                    recv_sem=rsem.at[w, j],
                    device_id=(ox, oy, c), device_id_type=_MESH)
                cp.start()
                cps.append(cp)
        for cp in cps:
            cp.wait()

    _sequencer_kernel(name, collective_id, (pltpu.SemaphoreType.DMA((n, 3)), pltpu.SemaphoreType.DMA((n, 3))), body)
    return [d[...] for d in dsts]


def _join_halves(name, bufs, collective_id):
    n = len(bufs)
    refs = [_hbm_ref(b) for b in bufs]

    def body(ssem, rsem):
        x, y, c = _coords()
        _handshake([(x, y, 1 - c)])
        cps = []
        for w in range(n):
            rh = refs[w].shape[0] // 2
            mine = refs[w].at[pl.ds(c * rh, rh)]
            cp = pltpu.make_async_remote_copy(src_ref=mine, dst_ref=mine, send_sem=ssem.at[w], recv_sem=rsem.at[w],
                                              device_id=(x, y, 1 - c), device_id_type=_MESH)
            cp.start()
            cps.append(cp)
        for w in range(n):
            rh = refs[w].shape[0] // 2
            theirs = refs[w].at[pl.ds((1 - c) * rh, rh)]
            pltpu.make_async_remote_copy(src_ref=theirs, dst_ref=theirs, send_sem=ssem.at[w], recv_sem=rsem.at[w],
                                         device_id=(x, y, 1 - c), device_id_type=_MESH).wait_recv()
        for cp in cps:
            cp.wait_send()

    _sequencer_kernel(name, collective_id, (pltpu.SemaphoreType.DMA((n,)), pltpu.SemaphoreType.DMA((n,))), body)
    return [r[...] for r in refs]


def _allgather_small(name, v, collective_id):
    R = v.shape[0]
    src = _hbm_ref(v)
    dst = jax.empty_ref(jax.ShapeDtypeStruct((8, R, 128), v.dtype), memory_space=pltpu.MemorySpace.HBM)
    flips = [(dx, dy, dc) for dx in (0, 1) for dy in (0, 1) for dc in (0, 1)][1:]

    def body(lsem, ssem, rsem):
        x, y, c = _coords()
        me = 4 * x + 2 * y + c
        flip = lambda v, d: 1 - v if d else v
        peers = [(flip(x, dx), flip(y, dy), flip(c, dc)) for dx, dy, dc in flips]
        _handshake(peers)
        own = pltpu.make_async_copy(src, dst.at[me], lsem)
        own.start()
        cps = []
        for r, peer in enumerate(peers):
            cp = pltpu.make_async_remote_copy(src_ref=src, dst_ref=dst.at[me], send_sem=ssem.at[r], recv_sem=rsem.at[r],
                                              device_id=peer, device_id_type=_MESH)
            cp.start()
            cps.append(cp)
        for r, (px, py, pc) in enumerate(peers):
            theirs = dst.at[4 * px + 2 * py + pc]
            pltpu.make_async_remote_copy(src_ref=theirs, dst_ref=theirs, send_sem=ssem.at[r], recv_sem=rsem.at[r],
                                         device_id=(px, py, pc), device_id_type=_MESH).wait_recv()
        for cp in cps:
            cp.wait_send()
        own.wait()

    _sequencer_kernel(name, collective_id, (pltpu.SemaphoreType.DMA, pltpu.SemaphoreType.DMA((7,)),
                                            pltpu.SemaphoreType.DMA((7,))), body)
    return dst[...]


def _sum8(name, g8, after):
    R = g8.shape[1]
    tr = _tile(R, 512, 8)

    def body(g_ref, *rest):
        rest[-1][...] = (((g_ref[0] + g_ref[1]) + (g_ref[2] + g_ref[3]))
                         + ((g_ref[4] + g_ref[5]) + (g_ref[6] + g_ref[7])))

    return pl.pallas_call(body, grid=(R // tr,),
                          in_specs=[pl.BlockSpec((8, tr, 128), lambda i: (0, i, 0))] + [_ANY] * len(after),
                          out_specs=pl.BlockSpec((tr, 128), lambda i: (i, 0)), out_shape=jax.ShapeDtypeStruct((R, 128), f32),
                          name=name, compiler_params=_cparams(("parallel",)))(g8, *after)


def _add_own_half(name, g, t, kc_idx, after):
    S, R, C = g.shape
    rh = R // 2
    tr = _tile(rh, 512, 16)
    nrb = rh // tr
    shard = lambda s, kc: (kc[0] + 1 + s) % S

    def body(kc_ref, g_ref, t_ref, *rest):
        rest[-1][...] = (g_ref[...] + t_ref[...]).astype(bf16)

    gs = pltpu.PrefetchScalarGridSpec(
        num_scalar_prefetch=1, grid=(S - 1, nrb),
        in_specs=[pl.BlockSpec((None, tr, C), lambda s, r, kc: (shard(s, kc), kc[1] * nrb + r, 0)),
                  pl.BlockSpec((None, tr, C), lambda s, r, kc: (shard(s, kc), r, 0))] + [_ANY] * len(after),
        out_specs=pl.BlockSpec((None, tr, C), lambda s, r, kc: (s, r, 0)))
    return pl.pallas_call(body, grid_spec=gs, out_shape=jax.ShapeDtypeStruct((S - 1, rh, C), bf16), name=name,
                          compiler_params=_cparams(("parallel", "parallel")))(kc_idx, g, t, *after)


def _add_shard_parts(name, g, t, r, kc_idx, after):
    S, R, C = g.shape
    rh = R // 2
    tr = _tile(rh, 256, 16)
    nrb = rh // tr

    def body(kc_ref, g_ref, t_ref, r_ref, *rest):
        own = g_ref[...] + t_ref[...]
        rest[-1][...] = (own + r_ref[1].astype(f32)) + (r_ref[0].astype(f32) + r_ref[2].astype(f32))

    gs = pltpu.PrefetchScalarGridSpec(
        num_scalar_prefetch=1, grid=(nrb,),
        in_specs=[pl.BlockSpec((None, tr, C), lambda i, kc: (kc[0], kc[1] * nrb + i, 0)),
                  pl.BlockSpec((None, tr, C), lambda i, kc: (kc[0], i, 0)),
                  pl.BlockSpec((3, tr, C), lambda i, kc: (0, i, 0))] + [_ANY] * len(after),
        out_specs=pl.BlockSpec((tr, C), lambda i, kc: (kc[1] * nrb + i, 0)))
    return pl.pallas_call(body, grid_spec=gs, out_shape=jax.ShapeDtypeStruct((R, C), f32), name=name,
                          compiler_params=_cparams(("parallel",)))(kc_idx, g, t, r, *after)


def _adamw_update(wv, gv, mv, vv):
    nm = _ADAM_B1 * mv + (1.0 - _ADAM_B1) * gv
    nv = _ADAM_B2 * vv + (1.0 - _ADAM_B2) * (gv * gv)
    m_hat = nm / (1.0 - _ADAM_B1 ** _ADAM_STEP)
    v_hat = nv / (1.0 - _ADAM_B2 ** _ADAM_STEP)
    return -_ADAM_LR * (m_hat / (jnp.sqrt(v_hat) + _ADAM_EPS) + _ADAM_WD * wv), nm, nv


def _adamw(name, w, g, m, v, after=()):
    R, C = w.shape
    tr = _tile(R, 256, 8)

    def body(w_ref, g_ref, m_ref, v_ref, *rest):
        go_ref, d_ref, nm_ref, nv_ref = rest[len(after):]
        gv = g_ref[...]
        go_ref[...] = gv
        d_ref[...], nm_ref[...], nv_ref[...] = _adamw_update(w_ref[...], gv, m_ref[...], v_ref[...])

    spec = pl.BlockSpec((tr, C), lambda i: (i, 0))
    shape = jax.ShapeDtypeStruct((R, C), f32)
    return pl.pallas_call(body, grid=(R // tr,), in_specs=[spec] * 4 + [_ANY] * len(after), out_specs=[spec] * 4,
                          out_shape=[shape] * 4, name=name, compiler_params=_cparams(("parallel",)))(w, g, m, v, *after)


_SC_TILES = 32


def _adamw_tiles(name, w, g, m, v):
    R, C = w.shape
    rows = R // _SC_TILES
    assert R % (8 * _SC_TILES) == 0 and C % 16 == 0

    def body(w_hbm, g_hbm, m_hbm, v_hbm, go_hbm, d_hbm, nm_hbm, nv_hbm, wb, gb, mb, vb, db):
        tile = lax.axis_index("subcore") * 2 + lax.axis_index("sc_core")

        @pl.loop(0, rows, step=8)
        def _(r0):
            sl = pl.ds(tile * rows + r0, 8)
            pltpu.sync_copy(w_hbm.at[sl], wb)
            pltpu.sync_copy(g_hbm.at[sl], gb)
            pltpu.sync_copy(m_hbm.at[sl], mb)
            pltpu.sync_copy(v_hbm.at[sl], vb)
            for r in range(8):
                @pl.loop(0, C, step=16)
                def _(i, r=r):
                    s = pl.ds(i, 16)
                    db[r, s], mb[r, s], vb[r, s] = _adamw_update(wb[r, s], gb[r, s], mb[r, s], vb[r, s])
            pltpu.sync_copy(gb, go_hbm.at[sl])
            pltpu.sync_copy(db, d_hbm.at[sl])
            pltpu.sync_copy(mb, nm_hbm.at[sl])
            pltpu.sync_copy(vb, nv_hbm.at[sl])

    shape = jax.ShapeDtypeStruct((R, C), f32)
    return pl.kernel(body, name=name, out_type=[shape] * 4,
                     mesh=plsc.VectorSubcoreMesh(core_axis_name="sc_core", subcore_axis_name="subcore"),
                     scratch_types=[pltpu.VMEM((8, C), f32)] * 5)(w, g, m, v)


def _adamw_whole(name, ws, gs, ms, vs):
    n = len(ws)

    def body(*refs):
        ins, outs = refs[:4 * n], refs[4 * n:]
        for i in range(n):
            w_ref, g_ref, m_ref, v_ref = (ins[j * n + i] for j in range(4))
            d, nm, nv = _adamw_update(w_ref[...], g_ref[...], m_ref[...], v_ref[...])
            outs[i][...], outs[n + i][...], outs[2 * n + i][...] = d, nm, nv

    vm = pl.BlockSpec(memory_space=pltpu.VMEM)
    res = pl.pallas_call(body, in_specs=[vm] * (4 * n), out_specs=[vm] * (3 * n),
                         out_shape=[jax.ShapeDtypeStruct(w.shape, f32) for w in ws] * 3, name=name,
                         compiler_params=pltpu.CompilerParams(vmem_limit_bytes=_VMEM_LIMIT))(*ws, *gs, *ms, *vs)
    return res[:n], res[n:2 * n], res[2 * n:]


def _pack(arrs):
    flat = jnp.concatenate([a.reshape(-1).astype(f32) for a in arrs])
    n = flat.shape[0]
    pad = (-n) % (128 * 128)
    return jnp.pad(flat, (0, pad)).reshape(-1, 128)


def _unpack(packed, shapes):
    flat = packed.reshape(-1)
    out, off = [], 0
    for s in shapes:
        n = math.prod(s)
        out.append(flat[off:off + n].reshape(s))
        off += n
    return out


_BIG = ("w_in", "w_glu", "w_ssm_out", "w_conv_out", "w_o", "w_up", "w_down")
_SMALL = ("norm_tok", "a_re", "a_im", "log_dt", "b_re", "b_im", "c_re", "c_im", "d_skip", "conv_w", "conv_b",
          "norm_ffn", "ffn_conv_w", "ffn_conv_b", "norm_final")
_WEIGHTS = ("norm_tok", "w_in", "a_re", "a_im", "log_dt", "b_re", "b_im", "c_re", "c_im", "d_skip", "w_glu",
            "w_ssm_out", "conv_w", "conv_b", "w_conv_out", "w_o", "norm_ffn", "w_up", "ffn_conv_w", "ffn_conv_b",
            "w_down", "norm_final")


def _step(cfg, x, tgt, p, m, v):
    L, D, SW, CW, F = cfg.L, cfg.D, cfg.SW, cfg.CW, cfg.F
    xi, yi, ci = _coords()
    k_idx = (2 * xi + yi).astype(jnp.int32).reshape(1)
    c_idx = ci.astype(jnp.int32).reshape(1)
    x = x.reshape(L, D)
    tgt = tgt.reshape(L, D)

    big2d = {n: p[n].reshape(p[n].shape[-2], p[n].shape[-1]) for n in _BIG}
    slots = {n: _cast_into_slot("cast_" + n, big2d[n], k_idx) for n in _BIG if n != "w_in"}
    slots["w_in"], w_in_own = _cast_into_slot("cast_w_in", big2d["w_in"], k_idx, also_alone=True)
    wg = {}
    for cid, (gname, group) in enumerate((("allgather_w_in", ("w_in",)),
                                          ("allgather_w_mixer", ("w_glu", "w_ssm_out", "w_conv_out", "w_o")),
                                          ("allgather_w_up", ("w_up",)), ("allgather_w_down", ("w_down",)))):
        wg.update(zip(group, _allgather_weights(gname, [slots[n] for n in group], cid)))
    w_in, w_so, w_co, w_up = wg["w_in"], wg["w_ssm_out"], wg["w_conv_out"], wg["w_up"]
    w_glu, w_o, w_down = wg["w_glu"], wg["w_o"], wg["w_down"]
    kk = k_idx[0]
    cw_full = lax.dynamic_update_slice(jnp.zeros((3, CW), f32), p["conv_w"].reshape(3, CW // _NCHIP), (0, kk * (CW // _NCHIP)))
    fw_full = lax.dynamic_update_slice(jnp.zeros((3, F), f32), p["ffn_conv_w"].reshape(3, F // _NCHIP), (0, kk * (F // _NCHIP)))
    south = (ci == 0).astype(f32)
    filters8 = _allgather_small("allgather_conv_filters", _pack([cw_full * south, fw_full * south]), 17)
    conv_b = p["conv_b"].reshape(1, CW)
    ffn_conv_b = p["ffn_conv_b"].reshape(1, F)
    norm_tok = p["norm_tok"].reshape(1, D)
    norm_ffn = p["norm_ffn"].reshape(1, D)
    norm_final = p["norm_final"].reshape(1, D)
    dskip = p["d_skip"].reshape(1, SW)

    s5_in = (p["a_re"][0], p["a_im"][0], p["log_dt"][0], p["b_re"][0], p["b_im"][0])
    (abr, abi, bbr, bbi), disc_vjp = jax.vjp(_discretize, *s5_in)
    tabs = _s5_tables(cfg, *s5_in[:3], abr, abi, bbr, bbi, p["c_re"][0], p["c_im"][0])

    xn1, r1 = _rms_fwd("rms_tok", x, norm_tok)
    proj = _in_proj(cfg, xn1, w_in_own, w_in, k_idx)
    conv_w, ffn_conv_w = _unpack(_sum8("sum_conv_filters", filters8, [proj]), [(3, CW), (3, F)])
    y_s, ya0, cin, xs = _s5_fwd(cfg, proj, tabs, dskip)

    def tiles(*arrs):
        return lambda tm, tn: [(a, pl.BlockSpec((tm, tn), lambda i, j: (i, j))) for a in arrs]

    def glu_epi(acc, e, o):
        o[0][...] = (e[0][...].astype(f32) * jax.nn.sigmoid(acc)).astype(bf16)
        o[1][...] = acc.astype(bf16)

    ya1, z = _mm_nn("glu", ya0, w_glu, [bf16, bf16], rows=True, extras_fn=tiles(ya0), epilogue=glu_epi)
    yb0 = _convb_fwd(cfg, proj, conv_w, conv_b)
    merged, ya, yb = _merge_fwd(cfg, ya1, yb0, proj, w_so, w_co)

    def res_epi(acc, e, o):
        o[0][...] = e[0][...] + acc

    h1 = _mm_nn("out_proj", merged, w_o, [f32], tm=512, tn=D, rows=True, extras_fn=tiles(x), epilogue=res_epi)[0]
    xn2, r2 = _rms_fwd("rms_ffn", h1, norm_ffn)
    hh = _mm_nn("ffn_up", xn2, w_up, [bf16], tn=2816)[0]
    fact, ffn_pre = _ffn_act(cfg, hh, ffn_conv_w, ffn_conv_b)
    h2 = _mm_nn("ffn_down", fact, w_down, [f32], tm=512, rows=True, extras_fn=tiles(h1), epilogue=res_epi)[0]
    dh2, dh2b, g_norm_final, loss_tile = _loss_head("loss_head", h2, tgt, norm_final)

    kc_idx = jnp.concatenate([k_idx, c_idx])
    reduced, chains = {}, {}

    def rs_halves(tag, collective_id, names, gs):
        chains[tag] = dict(cid=collective_id, names=names, gs=gs,
                           sib=_exchange_halves("grad_halves_" + tag, gs, collective_id))

    def rs_shards(tag, after):
        ch = chains[tag]
        ch["parts"] = [_add_own_half("grad_add_halves_" + n, g, t, kc_idx, after)
                       for n, g, t in zip(ch["names"], ch["gs"], ch["sib"])]
        ch["chips"] = _scatter_shards("grad_shards_" + tag, ch["parts"], ch["cid"] + 1)
        return ch["parts"]

    def rs_join(tag, after):
        ch = chains[tag]
        ch["halves"] = [_add_shard_parts("grad_add_chips_" + n, g, t, r, kc_idx, after)
                        for n, g, t, r in zip(ch["names"], ch["gs"], ch["sib"], ch["chips"])]
        reduced.update(zip(ch["names"], _join_halves("grad_join_" + tag, ch["halves"], ch["cid"] + 2)))
        return ch["halves"]

    g_w_down = _mm_tn("ffn_down_dw", fact, dh2b, tm=1408, tn=512)
    rs_halves("ffn_down", 4, ["w_down"], [g_w_down.reshape(_NCHIP, F // _NCHIP, D)])
    dhh, g_ffn_conv_w, g_ffn_conv_b = _ffn_down_dx_act_bwd(cfg, dh2b, w_down, hh, ffn_pre, ffn_conv_w)
    sent = rs_shards("ffn_down", [dhh])
    g_w_up = _mm_tn("ffn_up_dw", xn2, dhh, shards=_NCHIP, tm=512, tn=1408, b_resident=True, after=sent)
    rs_halves("ffn_up", 7, ["w_up"], [g_w_up])
    dxn2 = _ffn_up_dx(cfg, dhh, w_up)
    sent = rs_shards("ffn_up", [dxn2]) + rs_join("ffn_down", [dxn2])
    dh1, dh1b, g_norm_ffn = _rms_bwd("rms_ffn_bwd", dxn2, h1, r2, norm_ffn, dh2, after=sent)

    dya, dyb, dproj = _out_proj_dx(cfg, dh1b, w_o, ya, yb, proj)
    g_w_o = _mm_tn("out_proj_dw", merged, dh1b)

    def glu_bwd_epi(acc, e, o):
        a0 = e[0][...].astype(f32)
        s = jax.nn.sigmoid(e[1][...].astype(f32))
        o[0][...] = (acc * a0 * s * (1.0 - s)).astype(bf16)
        o[1][...] = (acc * s).astype(bf16)

    dz, t1 = _mm_nt("ssm_out_dx", dya, w_so, [bf16, bf16], extras_fn=tiles(ya0, z), epilogue=glu_bwd_epi)
    g_w_so = _mm_tn("ssm_out_dw", ya1, dya, shards=_NCHIP, tn=512)
    dyb0 = _mm_nt("conv_out_dx", dyb, w_co, [bf16])[0]
    g_w_co = _mm_tn("conv_out_dw", yb0, dyb, shards=_NCHIP, tn=512)
    dproj, g_conv_w, g_conv_b = _convb_bwd(cfg, proj, dyb0, conv_w, conv_b, dproj)

    def gelu_bwd_epi(acc, e, o):
        o[0][...] = ((e[0][...].astype(f32) + acc) * _gelu_grad(e[1][...].astype(f32))).astype(bf16)

    dy_s = _mm_nt("glu_dx", dz, w_glu, [bf16], rows=True, extras_fn=tiles(t1, y_s), epilogue=gelu_bwd_epi)[0]
    g_w_glu = _mm_tn("glu_dw", ya0, dz)
    rs_halves("mixer", 10, ["w_o", "w_ssm_out", "w_conv_out", "w_glu"],
              [g_w_o.reshape(_NCHIP, D // _NCHIP, D), g_w_so, g_w_co, g_w_glu.reshape(_NCHIP, SW // _NCHIP, SW)])
    sent = rs_join("ffn_up", [g_w_glu])
    dproj, da_acc, db_full, dc_full, g_dskip = _s5_bwd(cfg, proj, dy_s, cin, xs, tabs, dskip, dproj, after=sent)
    sent = rs_shards("mixer", [dproj])

    dabr, dabi, dbbr, dbbi, g_c_re, g_c_im = _s5_param_grads(cfg, da_acc, db_full, dc_full)
    g_a_re, g_a_im, g_log_dt, g_b_re, g_b_im = disc_vjp((dabr, dabi, dbbr, dbbi))
    small_g = {"a_re": g_a_re, "a_im": g_a_im, "log_dt": g_log_dt, "b_re": g_b_re, "b_im": g_b_im,
               "c_re": g_c_re, "c_im": g_c_im, "d_skip": g_dskip, "conv_w": g_conv_w, "conv_b": g_conv_b,
               "norm_ffn": g_norm_ffn, "ffn_conv_w": g_ffn_conv_w, "ffn_conv_b": g_ffn_conv_b, "norm_final": g_norm_final}
    early = [n for n in _SMALL if n != "norm_tok"]
    small8 = _allgather_small("allgather_small_grads", _pack([small_g[n] for n in early]), 16)

    g_w_in = _mm_tn("in_proj_dw", xn1, dproj, shards=_NCHIP, tn=CW, after=sent,
                    b_block=lambda j: jnp.where(j == 0, 3 * CW // SW, jnp.where(j < 4, j - 1, j)))
    rs_halves("in_proj", 13, ["w_in"], [g_w_in])
    dxn1 = _in_proj_dx(cfg, dproj, w_in)
    sent = rs_shards("in_proj", [dxn1]) + rs_join("mixer", [dxn1])
    dx, _, g_norm_tok = _rms_bwd("rms_tok_bwd", dxn1, x, r1, norm_tok, dh1, after=sent)

    summed = dict(zip(early, _unpack(_sum8("sum_small_grads", small8, [dx]), [small_g[n].shape for n in early])))
    summed["norm_tok"] = _unpack(_allreduce8("allreduce_norm_tok", _pack([g_norm_tok])), [g_norm_tok.shape])[0]
    summed["conv_w"] = lax.dynamic_slice(summed["conv_w"], (0, kk * (CW // _NCHIP)), (3, CW // _NCHIP))
    summed["ffn_conv_w"] = lax.dynamic_slice(summed["ffn_conv_w"], (0, kk * (F // _NCHIP)), (3, F // _NCHIP))

    grads, deltas, new_m, new_v = {}, {}, {}, {}

    def adamw_big(names, after):
        for n in names:
            operands = (big2d[n], reduced[n], m[n].reshape(big2d[n].shape), v[n].reshape(big2d[n].shape))
            if n == "w_up":
                g_, d_, m_, v_ = _adamw_tiles("adamw_" + n, *operands)
            else:
                g_, d_, m_, v_ = _adamw("adamw_" + n, *operands, after=after)
                after = [d_]
            grads[n], deltas[n], new_m[n], new_v[n] = (a.reshape(p[n].shape) for a in (g_, d_, m_, v_))
        return after

    for n in _SMALL:
        grads[n] = summed[n].reshape(p[n].shape)
    small_d, small_m, small_v = _adamw_whole("adamw_small", [p[n] for n in _SMALL], [grads[n] for n in _SMALL],
                                             [m[n] for n in _SMALL], [v[n] for n in _SMALL])
    deltas.update(zip(_SMALL, small_d))
    new_m.update(zip(_SMALL, small_m))
    new_v.update(zip(_SMALL, small_v))
    done = adamw_big(["w_down", "w_up", "w_o", "w_ssm_out", "w_conv_out", "w_glu"], [deltas["norm_final"]])
    rs_join("in_proj", done + [deltas[n] for n in _SMALL])
    adamw_big(["w_in"], ())

    loss = lax.psum(loss_tile[0, 0], ("x", "y", "c"))
    return (loss, dx.reshape(1, L, D), *[grads[n] for n in _WEIGHTS], *[deltas[n] for n in _WEIGHTS],
            *[new_m[n] for n in _WEIGHTS], *[new_v[n] for n in _WEIGHTS])


def kernel(x, norm_tok, w_in, a_re, a_im, log_dt, b_re, b_im, c_re, c_im, d_skip, w_glu, w_ssm_out, conv_w, conv_b, w_conv_out, w_o, norm_ffn, w_up, ffn_conv_w, ffn_conv_b, w_down, norm_final, loss_target, m_norm_tok, m_w_in, m_a_re, m_a_im, m_log_dt, m_b_re, m_b_im, m_c_re, m_c_im, m_d_skip, m_w_glu, m_w_ssm_out, m_conv_w, m_conv_b, m_w_conv_out, m_w_o, m_norm_ffn, m_w_up, m_ffn_conv_w, m_ffn_conv_b, m_w_down, m_norm_final, v_norm_tok, v_w_in, v_a_re, v_a_im, v_log_dt, v_b_re, v_b_im, v_c_re, v_c_im, v_d_skip, v_w_glu, v_w_ssm_out, v_conv_w, v_conv_b, v_w_conv_out, v_w_o, v_norm_ffn, v_w_up, v_ffn_conv_w, v_ffn_conv_b, v_w_down, v_norm_final):
    p = dict(norm_tok=norm_tok, w_in=w_in, a_re=a_re, a_im=a_im, log_dt=log_dt, b_re=b_re, b_im=b_im, c_re=c_re,
             c_im=c_im, d_skip=d_skip, w_glu=w_glu, w_ssm_out=w_ssm_out, conv_w=conv_w, conv_b=conv_b,
             w_conv_out=w_conv_out, w_o=w_o, norm_ffn=norm_ffn, w_up=w_up, ffn_conv_w=ffn_conv_w,
             ffn_conv_b=ffn_conv_b, w_down=w_down, norm_final=norm_final)
    m = dict(norm_tok=m_norm_tok, w_in=m_w_in, a_re=m_a_re, a_im=m_a_im, log_dt=m_log_dt, b_re=m_b_re, b_im=m_b_im,
             c_re=m_c_re, c_im=m_c_im, d_skip=m_d_skip, w_glu=m_w_glu, w_ssm_out=m_w_ssm_out, conv_w=m_conv_w,
             conv_b=m_conv_b, w_conv_out=m_w_conv_out, w_o=m_w_o, norm_ffn=m_norm_ffn, w_up=m_w_up,
             ffn_conv_w=m_ffn_conv_w, ffn_conv_b=m_ffn_conv_b, w_down=m_w_down, norm_final=m_norm_final)
    v = dict(norm_tok=v_norm_tok, w_in=v_w_in, a_re=v_a_re, a_im=v_a_im, log_dt=v_log_dt, b_re=v_b_re, b_im=v_b_im,
             c_re=v_c_re, c_im=v_c_im, d_skip=v_d_skip, w_glu=v_w_glu, w_ssm_out=v_w_ssm_out, conv_w=v_conv_w,
             conv_b=v_conv_b, w_conv_out=v_w_conv_out, w_o=v_w_o, norm_ffn=v_norm_ffn, w_up=v_w_up,
             ffn_conv_w=v_ffn_conv_w, ffn_conv_b=v_ffn_conv_b, w_down=v_w_down, norm_final=v_norm_final)
    return _step(_Cfg(), x, loss_target, p, m, v)
```

```python
import math
from typing import NamedTuple

import jax
import jax.numpy as jnp
from jax import lax
from jax.experimental import pallas as pl
from jax.experimental.pallas import tpu as pltpu
from jax.experimental.pallas import tpu_sc as plsc

f32 = jnp.float32
bf16 = jnp.bfloat16
_MESH = pl.DeviceIdType.MESH

_EPS = 1e-6
_ADAM_LR = 0.001
_ADAM_B1 = 0.9
_ADAM_B2 = 0.999
_ADAM_EPS = 1e-08
_ADAM_WD = 0.01
_ADAM_STEP = 10
_SSM_GROUP = 16
_SSM_STATE = 64
_SLAB_GROUPS = 16
_NCHIP = 4
_VMEM_LIMIT = 56 * 2**20
_VMEM_S5_BWD = 62 * 2**20
_GELU_C = math.sqrt(2.0 / math.pi)
_GELU_A = 0.044715


class _Cfg(NamedTuple):
    L: int = 4096
    D: int = 2048
    SW: int = 1024
    CW: int = 1024
    F: int = 5632
    T: int = 256


def _tile(n, pref, align):
    t = min(n, pref)
    t -= t % align
    while t > align and n % t:
        t -= align
    assert t > 0 and n % t == 0, (n, pref, align)
    return t


def _cparams(sem, vmem_limit=_VMEM_LIMIT):
    return pltpu.CompilerParams(dimension_semantics=sem, vmem_limit_bytes=vmem_limit)


def _gelu(x):
    return _gelu_and_grad(x)[0]


def _gelu_grad(x):
    return _gelu_and_grad(x)[1]


def _gelu_and_grad(x):
    x2 = x * x
    th = jnp.tanh(x * (_GELU_C + (_GELU_C * _GELU_A) * x2))
    half = 0.5 + 0.5 * th
    return x * half, half + (0.5 * x) * (1.0 - th * th) * (_GELU_C + (3.0 * _GELU_C * _GELU_A) * x2)


_NN = (((1,), (0,)), ((), ()))
_NT = (((1,), (1,)), ((), ()))
_TN = (((0,), (0,)), ((), ()))


def _whole(ref):
    return ref[...]


_ANY = pl.BlockSpec(memory_space=pl.ANY)


def _mm(name, operands, steps, *, grid, contract, outs, extras=(), epilogue=None, acc_shape=None, after=()):
    nop, ne, na = len(operands), len(extras), len(after)
    nk = len(steps)

    def body(*refs):
        op_refs = refs[:nop]
        e_refs = refs[nop:nop + ne]
        o_refs = refs[nop + ne + na:nop + ne + na + len(outs)]

        def partial(terms):
            tot = None
            for ai, av, bi, bv in terms:
                d = lax.dot_general(av(op_refs[ai]), bv(op_refs[bi]), contract, preferred_element_type=f32)
                tot = d if tot is None else tot + d
            return tot

        def finish(res):
            if epilogue is None:
                for o in o_refs:
                    o[...] = res.astype(o.dtype)
            else:
                epilogue(res, e_refs, o_refs)

        if nk == 1:
            finish(partial(steps[0][1]))
            return
        acc = refs[-1]
        kid = pl.program_id(len(grid) - 1)
        for k, terms in steps:
            def run(k=k, terms=terms):
                d = partial(terms)
                if k == 0:
                    acc[...] = d
                elif k < nk - 1:
                    acc[...] += d
                else:
                    finish(acc[...] + d)

            pl.when(kid == k)(run)

    sem = ("parallel",) * (len(grid) - (nk > 1)) + (("arbitrary",) if nk > 1 else ())
    return pl.pallas_call(
        body, grid=grid, in_specs=[o[1] for o in operands] + [e[1] for e in extras] + [_ANY] * na,
        out_specs=[o[1] for o in outs], out_shape=[o[0] for o in outs],
        scratch_shapes=[pltpu.VMEM(acc_shape, f32)] if nk > 1 else [], name=name,
        compiler_params=_cparams(sem))(*[o[0] for o in operands], *[e[0] for e in extras], *after)


def _mm_nn(name, a, w, out_dtypes, *, tm=1024, tn=1024, rows=False, extras_fn=None, epilogue=None):
    M, K = a.shape
    S, Ns = w.shape[0], w.shape[-1]
    N = Ns if rows else Ns * S
    tm, tn = _tile(M, tm, 16), _tile(Ns, tn, 128)
    nb = Ns // tn
    a_spec = pl.BlockSpec((tm, K), lambda i, j: (i, 0))
    if rows:
        b_spec = pl.BlockSpec((S, K // S, tn), lambda i, j: (0, 0, j))
        b_view = lambda r: r[...].reshape(K, tn)
    else:
        b_spec = pl.BlockSpec((None, K, tn), lambda i, j: (j // nb, 0, j % nb))
        b_view = _whole
    o_spec = pl.BlockSpec((tm, tn), lambda i, j: (i, j))
    outs = [(jax.ShapeDtypeStruct((M, N), dt), o_spec) for dt in out_dtypes]
    extras = extras_fn(tm, tn) if extras_fn is not None else ()
    return _mm(name, [(a, a_spec), (w, b_spec)], [(None, [(0, _whole, 1, b_view)])], grid=(M // tm, N // tn),
               contract=_NN, outs=outs, extras=extras, epilogue=epilogue)


def _mm_nt(name, a, w, out_dtypes, *, tm=1024, tn=1024, rows=False, extras_fn=None, epilogue=None):
    M, N = a.shape
    S, Ks, Ns = w.shape
    K = Ks * S if rows else Ks
    tm = _tile(M, tm, 16)
    a_spec = pl.BlockSpec((tm, N), lambda i, j: (i, 0))
    if rows:
        whole_shards = tn > Ks and tn % Ks == 0 and K % tn == 0
        tn = K if tn >= K else (tn if whole_shards else _tile(Ks, tn, 128))
        if tn == K or whole_shards:
            b_spec = pl.BlockSpec((tn // Ks, Ks, N), lambda i, j: (j, 0, 0))
            terms = [(0, _whole, 1, lambda r: r[...].reshape(tn, N))]
        else:
            nbs = Ks // tn
            b_spec = pl.BlockSpec((None, tn, N), lambda i, j: (j // nbs, j % nbs, 0))
            terms = [(0, _whole, 1, _whole)]
    else:
        tn = _tile(K, tn, 128)
        assert S * Ns == N
        b_spec = pl.BlockSpec((S, tn, Ns), lambda i, j: (0, j, 0))
        terms = [(0, lambda r, s=s: r[:, s * Ns:(s + 1) * Ns], 1, lambda r, s=s: r[s]) for s in range(S)]
    o_spec = pl.BlockSpec((tm, tn), lambda i, j: (i, j))
    outs = [(jax.ShapeDtypeStruct((M, K), dt), o_spec) for dt in out_dtypes]
    extras = extras_fn(tm, tn) if extras_fn is not None else ()
    return _mm(name, [(a, a_spec), (w, b_spec)], [(None, terms)], grid=(M // tm, K // tn), contract=_NT,
               outs=outs, extras=extras, epilogue=epilogue)


def _mm_tn(name, a, b, *, shards=None, tm=1024, tn=1024, b_block=None, b_resident=False, after=()):
    M, K = a.shape
    halves = b.shape[0] if b.ndim == 3 else 1
    Nh = b.shape[-1]
    N = Nh * halves
    Ns = N // shards if shards else N
    tm, tn = _tile(K, tm, 128), _tile(math.gcd(Ns, Nh), tn, 128)
    nb, nbh = Ns // tn, Nh // tn
    ij = (lambda g0, g1: (g1, g0)) if b_resident else (lambda g0, g1: (g0, g1))
    bmap = b_block if b_block is not None else (lambda j: j)
    a_spec = pl.BlockSpec((M, tm), lambda g0, g1: (0, ij(g0, g1)[0]))
    if halves > 1:
        b_spec = pl.BlockSpec((None, M, tn), lambda g0, g1: (bmap(ij(g0, g1)[1]) // nbh, 0, bmap(ij(g0, g1)[1]) % nbh))
    else:
        b_spec = pl.BlockSpec((M, tn), lambda g0, g1: (0, bmap(ij(g0, g1)[1])))
    if shards:
        out = (jax.ShapeDtypeStruct((shards, K, Ns), f32),
               pl.BlockSpec((None, tm, tn), lambda g0, g1: (ij(g0, g1)[1] // nb, ij(g0, g1)[0], ij(g0, g1)[1] % nb)))
    else:
        out = (jax.ShapeDtypeStruct((K, N), f32), pl.BlockSpec((tm, tn), lambda g0, g1: ij(g0, g1)))
    grid = (N // tn, K // tm) if b_resident else (K // tm, N // tn)
    return _mm(name, [(a, a_spec), (b, b_spec)], [(None, [(0, _whole, 1, _whole)])], grid=grid, contract=_TN,
               outs=[out], after=after)[0]


def _in_proj_dx(cfg, dproj, w_in):
    L, D, SW, CW = cfg.L, cfg.D, cfg.SW, cfg.CW
    NP = SW + 3 * CW + 2 * D
    Ns = NP // _NCHIP
    assert SW + CW == Ns and 2 * CW == Ns and D == Ns
    tm, tn = _tile(L, 1024, 16), _tile(D, 1024, 128)
    a_spec = pl.BlockSpec((tm, NP // 2), lambda i, j, k: (i, k))
    b_spec = pl.BlockSpec((2, tn, Ns), lambda i, j, k: (k, j, 0))
    first = [(0, lambda r: r[:, 0:CW], 1, lambda r: r[0, :, SW:SW + CW]),
             (0, lambda r: r[:, CW:3 * CW], 1, lambda r: r[1]),
             (0, lambda r: r[:, 3 * CW:3 * CW + SW], 1, lambda r: r[0, :, 0:SW])]
    second = [(0, lambda r: r[:, 0:D], 1, lambda r: r[0]), (0, lambda r: r[:, D:2 * D], 1, lambda r: r[1])]
    out = (jax.ShapeDtypeStruct((L, D), bf16), pl.BlockSpec((tm, tn), lambda i, j, k: (i, j)))
    return _mm("in_proj_dx", [(dproj, a_spec), (w_in, b_spec)], [(0, first), (1, second)], grid=(L // tm, D // tn, 2),
               contract=_NT, outs=[out], acc_shape=(tm, tn))[0]


def _out_proj_dx(cfg, dh1b, w_o, ya, yb, proj):
    L, D = cfg.L, cfg.D
    NP = cfg.SW + 3 * cfg.CW + 2 * D
    assert NP == 4 * D
    tm = _tile(L, 512, 16)

    def epilogue(acc, e, o):
        sa = jax.nn.sigmoid(e[2][:, 0:D].astype(f32))
        sb = jax.nn.sigmoid(e[2][:, D:2 * D].astype(f32))
        o[0][...] = (acc * sa).astype(bf16)
        o[1][...] = (acc * sb).astype(bf16)
        o[2][:, 0:D] = (acc * e[0][...].astype(f32) * sa * (1.0 - sa)).astype(bf16)
        o[2][:, D:2 * D] = (acc * e[1][...].astype(f32) * sb * (1.0 - sb)).astype(bf16)

    row = pl.BlockSpec((tm, D), lambda i, j: (i, 0))
    half = pl.BlockSpec((tm, 2 * D), lambda i, j: (i, 1))
    return _mm("out_proj_dx", [(dh1b, row), (w_o, pl.BlockSpec(w_o.shape, lambda i, j: (0, 0, 0),
                                                                pipeline_mode=pl.Buffered(1)))],
               [(None, [(0, _whole, 1, lambda r: r[...].reshape(D, D))])], grid=(L // tm, 1), contract=_NT,
               outs=[(jax.ShapeDtypeStruct((L, D), bf16), row), (jax.ShapeDtypeStruct((L, D), bf16), row),
                     (jax.ShapeDtypeStruct((L, NP), bf16), half)],
               extras=[(ya, row), (yb, row), (proj, half)], epilogue=epilogue)


def _ffn_up_dx(cfg, dhh, w_up):
    L, D, F = cfg.L, cfg.D, cfg.F
    Fh = F // 2
    tm, tn = _tile(L, 1024, 16), _tile(D, 512, 128)
    a_spec = pl.BlockSpec((None, tm, F), lambda i, j, k: (k, i, 0))
    b_spec = pl.BlockSpec((2, tn, Fh), lambda i, j, k: (k, j, 0))
    terms = [(0, lambda r: r[:, 0:Fh], 1, lambda r: r[0]), (0, lambda r: r[:, Fh:F], 1, lambda r: r[1])]
    out = (jax.ShapeDtypeStruct((L, D), bf16), pl.BlockSpec((tm, tn), lambda i, j, k: (i, j)))
    return _mm("ffn_up_dx", [(dhh, a_spec), (w_up, b_spec)], [(0, terms), (1, terms)], grid=(L // tm, D // tn, 2),
               contract=_NT, outs=[out], acc_shape=(tm, tn))[0]


def _rms_fwd(name, x, g):
    L, D = x.shape
    tm = _tile(L, 512, 16)

    def body(x_ref, g_ref, xn_ref, r_ref):
        xv = x_ref[...]
        r = lax.rsqrt(jnp.mean(xv * xv, axis=-1, keepdims=True) + _EPS)
        xn_ref[...] = (xv * r * g_ref[...]).astype(bf16)
        r_ref[...] = r

    return pl.pallas_call(
        body, grid=(L // tm,),
        in_specs=[pl.BlockSpec((tm, D), lambda i: (i, 0)), pl.BlockSpec((1, D), lambda i: (0, 0))],
        out_specs=[pl.BlockSpec((tm, D), lambda i: (i, 0)), pl.BlockSpec((tm, 1), lambda i: (i, 0))],
        out_shape=[jax.ShapeDtypeStruct((L, D), bf16), jax.ShapeDtypeStruct((L, 1), f32)],
        name=name, compiler_params=_cparams(("parallel",)))(x, g)


def _rms_bwd(name, dxn, h, r, g, dres, after=()):
    L, D = h.shape
    tm = _tile(L, 512, 16)

    def body(dxn_ref, h_ref, r_ref, g_ref, dres_ref, *rest):
        dh_ref, dhb_ref, dg_ref = rest[len(after):]
        i = pl.program_id(0)
        d = dxn_ref[...].astype(f32)
        hv = h_ref[...]
        rv = r_ref[...]
        dyg = d * g_ref[...]
        m = jnp.mean(dyg * hv, axis=-1, keepdims=True)
        dh = dres_ref[...] + rv * dyg - hv * (rv * rv * rv) * m
        dh_ref[...] = dh
        dhb_ref[...] = dh.astype(bf16)

        @pl.when(i == 0)
        def _():
            dg_ref[...] = jnp.zeros_like(dg_ref)

        dg_ref[...] += jnp.sum(d * hv * rv, axis=0, keepdims=True)

    row = lambda i: (i, 0)
    return pl.pallas_call(
        body, grid=(L // tm,),
        in_specs=[pl.BlockSpec((tm, D), row), pl.BlockSpec((tm, D), row), pl.BlockSpec((tm, 1), row),
                  pl.BlockSpec((1, D), lambda i: (0, 0)), pl.BlockSpec((tm, D), row)] + [_ANY] * len(after),
        out_specs=[pl.BlockSpec((tm, D), row), pl.BlockSpec((tm, D), row), pl.BlockSpec((1, D), lambda i: (0, 0))],
        out_shape=[jax.ShapeDtypeStruct((L, D), f32), jax.ShapeDtypeStruct((L, D), bf16), jax.ShapeDtypeStruct((1, D), f32)],
        name=name, compiler_params=_cparams(("arbitrary",)))(dxn, h, r, g, dres, *after)


def _loss_head(name, h2, tgt, g):
    L, D = h2.shape
    tm = _tile(L, 512, 16)

    def body(h_ref, t_ref, g_ref, dh_ref, dhb_ref, dg_ref, loss_ref):
        i = pl.program_id(0)
        hv = h_ref[...]
        gv = g_ref[...]
        r = lax.rsqrt(jnp.mean(hv * hv, axis=-1, keepdims=True) + _EPS)
        err = hv * r * gv - t_ref[...]
        dy = err * (1.0 / D)
        dyg = dy * gv
        m = jnp.mean(dyg * hv, axis=-1, keepdims=True)
        dh = r * dyg - hv * (r * r * r) * m
        dh_ref[...] = dh
        dhb_ref[...] = dh.astype(bf16)

        @pl.when(i == 0)
        def _():
            dg_ref[...] = jnp.zeros_like(dg_ref)
            loss_ref[...] = jnp.zeros_like(loss_ref)

        dg_ref[...] += jnp.sum(dy * hv * r, axis=0, keepdims=True)
        part = jnp.sum(jnp.sum(err * err, axis=-1, keepdims=True), axis=0, keepdims=True) * (0.5 / D)
        loss_ref[...] += jnp.broadcast_to(part, (8, 128))

    row = lambda i: (i, 0)
    return pl.pallas_call(
        body, grid=(L // tm,),
        in_specs=[pl.BlockSpec((tm, D), row), pl.BlockSpec((tm, D), row), pl.BlockSpec((1, D), lambda i: (0, 0))],
        out_specs=[pl.BlockSpec((tm, D), row), pl.BlockSpec((tm, D), row), pl.BlockSpec((1, D), lambda i: (0, 0)),
                   pl.BlockSpec((8, 128), lambda i: (0, 0))],
        out_shape=[jax.ShapeDtypeStruct((L, D), f32), jax.ShapeDtypeStruct((L, D), bf16),
                   jax.ShapeDtypeStruct((1, D), f32), jax.ShapeDtypeStruct((8, 128), f32)],
        name=name, compiler_params=_cparams(("arbitrary",)))(h2, tgt, g)


def _shift_down(tile, halo, k, rows8):
    tm = tile.shape[0]
    r = pltpu.roll(tile, k, axis=0)
    hh = pltpu.roll(halo, k, axis=0)
    top = jnp.where(rows8 < k, hh, r[:8])
    return jnp.concatenate([top, r[8:]], axis=0) if tm > 8 else top


def _shift_up(tile, halo, k, rows8):
    tm = tile.shape[0]
    r = pltpu.roll(tile, tm - k, axis=0)
    hh = pltpu.roll(halo, 8 - k, axis=0)
    bot = jnp.where(rows8 >= 8 - k, hh, r[tm - 8:])
    return jnp.concatenate([r[:tm - 8], bot], axis=0) if tm > 8 else bot


def _conv3(x, halo, w_ref, b_ref, rows8):
    return (w_ref[0:1, :] * _shift_down(x, halo, 2, rows8) + w_ref[1:2, :] * _shift_down(x, halo, 1, rows8)
            + w_ref[2:3, :] * x + b_ref[...])


def _convb_fwd(cfg, proj, w, b):
    L, CW = cfg.L, cfg.CW
    assert cfg.SW == CW
    tm = _tile(L, 512, 16)

    def body(v_ref, vh_ref, gb_ref, gc_ref, gch_ref, w_ref, b_ref, o_ref):
        i = pl.program_id(0)
        rows8 = lax.broadcasted_iota(jnp.int32, (8, CW), 0)
        cv = gc_ref[...].astype(f32) * v_ref[...].astype(f32)
        cvh = gch_ref[...].astype(f32)[8:] * vh_ref[...].astype(f32)[8:]
        cvh = jnp.where(i == 0, 0.0, cvh)
        cc = _conv3(cv, cvh, w_ref, b_ref, rows8)
        o_ref[...] = (gb_ref[...].astype(f32) * cc).astype(bf16)

    blk = lambda col: pl.BlockSpec((tm, CW), lambda i: (i, col))
    halo = lambda col: pl.BlockSpec((16, CW), lambda i: (jnp.maximum(i * (tm // 16) - 1, 0), col))
    return pl.pallas_call(
        body, grid=(L // tm,),
        in_specs=[blk(1), halo(1), blk(2), blk(3), halo(3),
                  pl.BlockSpec((3, CW), lambda i: (0, 0)), pl.BlockSpec((1, CW), lambda i: (0, 0))],
        out_specs=pl.BlockSpec((tm, CW), lambda i: (i, 0)),
        out_shape=jax.ShapeDtypeStruct((L, CW), bf16),
        name="convb_fwd", compiler_params=_cparams(("parallel",)))(proj, proj, proj, proj, proj, w, b)


def _convb_bwd(cfg, proj, dyb0, w, b, dproj):
    L, CW = cfg.L, cfg.CW
    tm = _tile(L, 512, 16)
    nt = L // tm

    def body(v_ref, vh_ref, gb_ref, gbn_ref, gc_ref, gch_ref, d_ref, dn_ref, w_ref, b_ref, dproj_ref,
             o_ref, dw_ref, db_ref):
        i = pl.program_id(0)
        rows8 = lax.broadcasted_iota(jnp.int32, (8, CW), 0)
        v = v_ref[...].astype(f32)
        gb = gb_ref[...].astype(f32)
        gc = gc_ref[...].astype(f32)
        d = d_ref[...].astype(f32)
        cv = gc * v
        cvh = gch_ref[...].astype(f32)[8:] * vh_ref[...].astype(f32)[8:]
        cvh = jnp.where(i == 0, 0.0, cvh)
        s2 = _shift_down(cv, cvh, 2, rows8)
        s1 = _shift_down(cv, cvh, 1, rows8)
        cc = w_ref[0:1, :] * s2 + w_ref[1:2, :] * s1 + w_ref[2:3, :] * cv + b_ref[...]
        dcc = d * gb
        dccn = dn_ref[...].astype(f32)[:8] * gbn_ref[...].astype(f32)[:8]
        dccn = jnp.where(i == nt - 1, 0.0, dccn)
        dcv = (w_ref[2:3, :] * dcc + w_ref[1:2, :] * _shift_up(dcc, dccn, 1, rows8)
               + w_ref[0:1, :] * _shift_up(dcc, dccn, 2, rows8))
        o_ref[:, 0:CW] = (dcv * gc).astype(bf16)
        o_ref[:, CW:2 * CW] = (d * cc).astype(bf16)
        o_ref[:, 2 * CW:3 * CW] = (dcv * v).astype(bf16)

        @pl.when(i == 0)
        def _():
            dw_ref[...] = jnp.zeros_like(dw_ref)
            db_ref[...] = jnp.zeros_like(db_ref)

        dw_ref[0:1, :] += jnp.sum(dcc * s2, axis=0, keepdims=True)
        dw_ref[1:2, :] += jnp.sum(dcc * s1, axis=0, keepdims=True)
        dw_ref[2:3, :] += jnp.sum(dcc * cv, axis=0, keepdims=True)
        db_ref[...] += jnp.sum(dcc, axis=0, keepdims=True)

    blk = lambda col: pl.BlockSpec((tm, CW), lambda i: (i, col))
    prev = lambda col: pl.BlockSpec((16, CW), lambda i: (jnp.maximum(i * (tm // 16) - 1, 0), col))
    nxt = lambda col: pl.BlockSpec((16, CW), lambda i: (jnp.minimum((i + 1) * (tm // 16), L // 16 - 1), col))
    const = lambda r: pl.BlockSpec((r, CW), lambda i: (0, 0))
    return pl.pallas_call(
        body, grid=(nt,),
        in_specs=[blk(1), prev(1), blk(2), nxt(2), blk(3), prev(3), blk(0), nxt(0), const(3), const(1),
                  pl.BlockSpec(memory_space=pl.ANY)],
        out_specs=[pl.BlockSpec((tm, 3 * CW), lambda i: (i, 0)), const(3), const(1)],
        out_shape=[jax.ShapeDtypeStruct(dproj.shape, bf16), jax.ShapeDtypeStruct((3, CW), f32),
                   jax.ShapeDtypeStruct((1, CW), f32)],
        input_output_aliases={10: 0},
        name="convb_bwd", compiler_params=_cparams(("arbitrary",)))(proj, proj, proj, proj, proj, proj, dyb0, dyb0, w, b,
                                                                    dproj)


def _ffn_act(cfg, hh, w, b):
    L, F = cfg.L, cfg.F
    tm = _tile(L, 512, 16)
    tc = _tile(F, 1408, 128)
    ncb = F // tc

    def body(a_ref, ah_ref, g_ref, w_ref, b_ref, o_ref, act_ref):
        i = pl.program_id(0)
        rows8 = lax.broadcasted_iota(jnp.int32, (8, tc), 0)
        a = a_ref[...].astype(f32)
        ah = jnp.where(i == 0, 0.0, ah_ref[...].astype(f32)[8:])
        act = _conv3(a, ah, w_ref, b_ref, rows8)
        act_ref[...] = act.astype(bf16)
        o_ref[...] = (_gelu(act) * g_ref[...].astype(f32)).astype(bf16)

    tile = pl.BlockSpec((tm, tc), lambda i, j: (i, j))
    return pl.pallas_call(
        body, grid=(L // tm, ncb),
        in_specs=[tile, pl.BlockSpec((16, tc), lambda i, j: (jnp.maximum(i * (tm // 16) - 1, 0), j)),
                  pl.BlockSpec((tm, tc), lambda i, j: (i, j + ncb)),
                  pl.BlockSpec((3, tc), lambda i, j: (0, j)), pl.BlockSpec((1, tc), lambda i, j: (0, j))],
        out_specs=[tile, tile], out_shape=[jax.ShapeDtypeStruct((L, F), bf16)] * 2,
        name="ffn_act", compiler_params=_cparams(("parallel", "parallel")))(hh, hh, hh, w, b)


def _ffn_down_dx_act_bwd(cfg, dh2b, w_down, hh, act, w):
    L, D, F = cfg.L, cfg.D, cfg.F
    S, Ks, _ = w_down.shape
    tm = _tile(L, 512, 16)
    tc = _tile(Ks, 1408, 128)
    ncb = F // tc
    nbs = Ks // tc
    nt = L // tm

    slabs = [(c0, min(256, tc - c0)) for c0 in range(0, tc, 256)]

    def body(dh_ref, wd_ref, a_ref, c_ref, g_ref, w_ref, dhh_ref, dw_ref, db_ref, below):
        i = pl.program_id(1)

        @pl.when(i == 0)
        def _():
            below[...] = jnp.zeros_like(below)
            dw_ref[...] = jnp.zeros_like(dw_ref)
            db_ref[...] = jnp.zeros_like(db_ref)

        dh = dh_ref[...]
        for c0, cw in slabs:
            cols = slice(c0, c0 + cw)
            rows8 = lax.broadcasted_iota(jnp.int32, (8, cw), 0)
            d = lax.dot_general(dh, wd_ref[c0:c0 + cw, :], _NT, preferred_element_type=f32)
            a = a_ref[:, cols].astype(f32)
            gelu, gelu_d = _gelu_and_grad(c_ref[:, cols].astype(f32))
            dhh_ref[1, :, cols] = (d * gelu).astype(bf16)
            dact = d * g_ref[:, cols].astype(f32) * gelu_d
            dactn = below[:, cols]
            up1 = _shift_up(dact, dactn, 1, rows8)
            up2 = _shift_up(dact, dactn, 2, rows8)
            below[:, cols] = dact[:8]
            dhh_ref[0, :, cols] = (w_ref[2:3, cols] * dact + w_ref[1:2, cols] * up1 + w_ref[0:1, cols] * up2).astype(bf16)
            dw_ref[0:1, cols] += jnp.sum(up2 * a, axis=0, keepdims=True)
            dw_ref[1:2, cols] += jnp.sum(up1 * a, axis=0, keepdims=True)
            dw_ref[2:3, cols] += jnp.sum(dact * a, axis=0, keepdims=True)
            db_ref[:, cols] += jnp.sum(dact, axis=0, keepdims=True)

    up = lambda i: nt - 1 - i
    blk = lambda off: pl.BlockSpec((tm, tc), lambda j, i: (up(i), j + off))
    const = lambda r: pl.BlockSpec((r, tc), lambda j, i: (0, j))
    return pl.pallas_call(
        body, grid=(ncb, nt),
        in_specs=[pl.BlockSpec((tm, D), lambda j, i: (up(i), 0)),
                  pl.BlockSpec((None, tc, D), lambda j, i: (j // nbs, j % nbs, 0)),
                  blk(0), blk(0), blk(ncb), const(3)],
        out_specs=[pl.BlockSpec((2, tm, tc), lambda j, i: (0, up(i), j)), const(3), const(1)],
        out_shape=[jax.ShapeDtypeStruct((2, L, F), bf16),
                   jax.ShapeDtypeStruct((3, F), f32), jax.ShapeDtypeStruct((1, F), f32)],
        scratch_shapes=[pltpu.VMEM((8, tc), f32)],
        name="ffn_down_dx_act_bwd", compiler_params=_cparams(("parallel", "arbitrary")))(dh2b, w_down, hh, act, hh, w)


def _merge_fwd(cfg, ya1, yb0, proj, wso, wco):
    L, D, SW, CW = cfg.L, cfg.D, cfg.SW, cfg.CW
    Ns = D // _NCHIP
    tm = _tile(L, 1024, 16)
    tn = _tile(Ns, 512, 128)
    nb = Ns // tn
    off_a = (SW + 3 * CW) // tn
    off_b = (SW + 3 * CW + D) // tn

    def body(a_ref, b_ref, wa_ref, wb_ref, ma_ref, mb_ref, m_ref, ya_ref, yb_ref):
        ya = jnp.dot(a_ref[...], wa_ref[...], preferred_element_type=f32)
        yb = jnp.dot(b_ref[...], wb_ref[...], preferred_element_type=f32)
        sa = jax.nn.sigmoid(ma_ref[...].astype(f32))
        sb = jax.nn.sigmoid(mb_ref[...].astype(f32))
        m_ref[...] = (sa * ya + sb * yb).astype(bf16)
        ya_ref[...] = ya.astype(bf16)
        yb_ref[...] = yb.astype(bf16)

    o_spec = pl.BlockSpec((tm, tn), lambda i, j: (i, j))
    o_shape = jax.ShapeDtypeStruct((L, D), bf16)
    return pl.pallas_call(
        body, grid=(L // tm, D // tn),
        in_specs=[pl.BlockSpec((tm, SW), lambda i, j: (i, 0)), pl.BlockSpec((tm, CW), lambda i, j: (i, 0)),
                  pl.BlockSpec((None, SW, tn), lambda i, j: (j // nb, 0, j % nb)),
                  pl.BlockSpec((None, CW, tn), lambda i, j: (j // nb, 0, j % nb)),
                  pl.BlockSpec((tm, tn), lambda i, j: (i, off_a + j)), pl.BlockSpec((tm, tn), lambda i, j: (i, off_b + j))],
        out_specs=[o_spec, o_spec, o_spec], out_shape=[o_shape, o_shape, o_shape],
        name="merge_fwd", compiler_params=_cparams(("parallel", "parallel")))(ya1, yb0, wso, wco, proj, proj)


def _s5_dims(cfg):
    G = cfg.SW // _SSM_GROUP
    NS = G // _SLAB_GROUPS
    SC = _SLAB_GROUPS * _SSM_GROUP
    SH = _SLAB_GROUPS * _SSM_STATE
    NST = 2 * SH * NS
    return G, NS, SC, SH, NST


def _lane_slabs(cfg, W):
    _, NS, _, SH, _ = _s5_dims(cfg)
    return [(2 * SH * s + w0, 2 * SH * s + SH + w0) for s in range(NS) for w0 in range(0, SH, W)]


def _discretize(a_re, a_im, log_dt, b_re, b_im):
    dt = jnp.exp(log_dt)[:, None]
    mag = jnp.exp(dt * a_re)
    abr = mag * jnp.cos(dt * a_im)
    abi = mag * jnp.sin(dt * a_im)
    nr = abr - 1.0
    ni = abi
    den = a_re * a_re + a_im * a_im
    fr = (nr * a_re + ni * a_im) / den
    fi = (ni * a_re - nr * a_im) / den
    bbr = fr[..., None] * b_re - fi[..., None] * b_im
    bbi = fr[..., None] * b_im + fi[..., None] * b_re
    return abr, abi, bbr, bbi


def _state_rows(cfg, re, im):
    _, NS, _, SH, _ = _s5_dims(cfg)
    return jnp.concatenate([re.reshape(NS, SH), im.reshape(NS, SH)], axis=1).reshape(-1)


def _s5_tables(cfg, a_re, a_im, log_dt, abr, abi, bbr, bbi, c_re, c_im):
    G, NS, SC, SH, NST = _s5_dims(cfg)
    S = cfg.T // 8
    eye = jnp.eye(_SLAB_GROUPS, dtype=bf16)
    bb = jnp.stack([bbr, bbi]).reshape(2, NS, _SLAB_GROUPS, _SSM_STATE, _SSM_GROUP).astype(bf16)
    bs = (bb.transpose(1, 2, 4, 0, 3)[:, :, :, :, None, :] * eye[None, :, None, None, :, None]).reshape(NS, SC, 2 * SH)
    cc = jnp.stack([c_re, -c_im]).reshape(2, NS, _SLAB_GROUPS, _SSM_GROUP, _SSM_STATE).astype(bf16)
    cs = (cc.transpose(1, 0, 4, 2, 3)[:, :, None, :, :, :] * eye[None, None, :, None, :, None]).reshape(NS, 2 * SH, SC)
    arep = jnp.broadcast_to(_state_rows(cfg, abr, abi)[None, :], (8, NST))
    sdt = S * jnp.exp(log_dt)[:, None]
    mag = jnp.exp(sdt * a_re)
    apow = jnp.broadcast_to(_state_rows(cfg, mag * jnp.cos(sdt * a_im), mag * jnp.sin(sdt * a_im))[None, :], (8, NST))
    t = jnp.arange(cfg.T)
    perm = (t % 8) * S + t // 8
    pm = jax.nn.one_hot(perm, cfg.T, dtype=bf16)
    return bs, cs, arep, apow, pm, pm.T


def _cmul_add(ar, ai, xr, xi, br, bi):
    return ar * xr - ai * xi + br, ar * xi + ai * xr + bi


def _s5_forward_chunk(cfg, W, upb, bs_ref, arep_ref, apow_ref, st, x0, cin_store):
    _, NS, SC, SH, _ = _s5_dims(cfg)
    S = cfg.T // 8
    for s in range(NS):
        st[:, 2 * SH * s:2 * SH * (s + 1)] = jnp.dot(upb[:, SC * s:SC * (s + 1)], bs_ref[s], preferred_element_type=f32)
    rows = lax.broadcasted_iota(jnp.int32, (8, W), 0)
    zero = jnp.zeros((8, W), f32)
    for rc, ic in _lane_slabs(cfg, W):
        ar = arep_ref[:, rc:rc + W]
        ai = arep_ref[:, ic:ic + W]

        def step(i, carry, rc=rc, ic=ic, ar=ar, ai=ai):
            xr, xi = carry
            r0 = pl.multiple_of(i * 8, 8)
            nr, ni = _cmul_add(ar, ai, xr, xi, st[pl.ds(r0, 8), rc:rc + W], st[pl.ds(r0, 8), ic:ic + W])
            st[pl.ds(r0, 8), rc:rc + W] = nr
            st[pl.ds(r0, 8), ic:ic + W] = ni
            return nr, ni

        er, ei = lax.fori_loop(0, S, step, (zero, zero))
        pr = apow_ref[:, rc:rc + W]
        pi = apow_ref[:, ic:ic + W]
        x0r = x0[:, rc:rc + W]
        x0i = x0[:, ic:ic + W]
        cr = jnp.where(rows == 0, x0r, 0.0)
        ci = jnp.where(rows == 0, x0i, 0.0)
        for _ in range(7):
            fr, fi = _cmul_add(pr, pi, cr, ci, er, ei)
            cr = jnp.where(rows == 0, x0r, pltpu.roll(fr, 1, axis=0))
            ci = jnp.where(rows == 0, x0i, pltpu.roll(fi, 1, axis=0))
        fr, fi = _cmul_add(pr, pi, cr, ci, er, ei)
        x0[:, rc:rc + W] = jnp.broadcast_to(fr[7:8, :], (8, W))
        x0[:, ic:ic + W] = jnp.broadcast_to(fi[7:8, :], (8, W))
        cin_store(rc, ic, cr, ci)

        def fix(i, carry, rc=rc, ic=ic, ar=ar, ai=ai):
            kr, ki = carry
            r0 = pl.multiple_of(i * 8, 8)
            nr, ni = ar * kr - ai * ki, ar * ki + ai * kr
            st[pl.ds(r0, 8), rc:rc + W] = st[pl.ds(r0, 8), rc:rc + W] + nr
            st[pl.ds(r0, 8), ic:ic + W] = st[pl.ds(r0, 8), ic:ic + W] + ni
            return nr, ni

        lax.fori_loop(0, S, fix, (cr, ci))


def _s5_fwd(cfg, proj, tabs, dskip):
    L, SW, T = cfg.L, cfg.SW, cfg.T
    G, NS, SC, SH, NST = _s5_dims(cfg)
    bs, cs, arep, apow, pm, pmt = tabs
    W = min(512, SH)
    NC = L // T

    def body(u_ref, pm_ref, pmt_ref, bs_ref, cs_ref, arep_ref, apow_ref, dskip_ref, y_ref, ya0_ref, cin_ref, st, x0):
        c = pl.program_id(0)

        @pl.when(c == 0)
        def _():
            x0[...] = jnp.zeros_like(x0)

        up = jnp.dot(pm_ref[...], u_ref[...], preferred_element_type=f32)
        upb = up.astype(bf16)

        def cin_store(rc, ic, cr, ci):
            cin_ref[0, :, rc:rc + W] = cr
            cin_ref[0, :, ic:ic + W] = ci

        _s5_forward_chunk(cfg, W, upb, bs_ref, arep_ref, apow_ref, st, x0, cin_store)
        yp = jnp.concatenate(
            [jnp.dot(st[:, 2 * SH * s:2 * SH * (s + 1)].astype(bf16), cs_ref[s], preferred_element_type=f32)
             for s in range(NS)], axis=1) + dskip_ref[...] * up
        y = jnp.dot(pmt_ref[...], yp.astype(bf16), preferred_element_type=f32)
        y_ref[...] = y.astype(bf16)
        ya0_ref[...] = _gelu(y).astype(bf16)

    const2 = lambda shape: pl.BlockSpec(shape, lambda c: (0, 0))
    const3 = lambda shape: pl.BlockSpec(shape, lambda c: (0, 0, 0))
    return pl.pallas_call(
        body, grid=(NC,),
        in_specs=[pl.BlockSpec((T, SW), lambda c: (c, 0)), const2((T, T)), const2((T, T)), const3((NS, SC, 2 * SH)),
                  const3((NS, 2 * SH, SC)), const2((8, NST)), const2((8, NST)), const2((1, SW))],
        out_specs=[pl.BlockSpec((T, SW), lambda c: (c, 0)), pl.BlockSpec((T, SW), lambda c: (c, 0)),
                   pl.BlockSpec((1, 8, NST), lambda c: (c, 0, 0)), pl.BlockSpec((T, NST), lambda c: (c, 0))],
        out_shape=[jax.ShapeDtypeStruct((L, SW), bf16), jax.ShapeDtypeStruct((L, SW), bf16),
                   jax.ShapeDtypeStruct((NC, 8, NST), f32), jax.ShapeDtypeStruct((L, NST), f32)],
        scratch_shapes=[pltpu.VMEM((8, NST), f32)],
        name="s5_fwd", compiler_params=_cparams(("arbitrary",)))(proj, pm, pmt, bs, cs, arep, apow, dskip)


def _s5_bwd(cfg, proj, dy, cin, xs, tabs, dskip, dproj, after=()):
    L, SW, T = cfg.L, cfg.SW, cfg.T
    du_col = 3 * cfg.CW // SW
    G, NS, SC, SH, NST = _s5_dims(cfg)
    bs, cs, arep, apow, pm, pmt = tabs
    W = min(512, SH)
    S = T // 8
    NC = L // T

    def body(u_ref, dy_ref, cin_ref, st, pm_ref, pmt_ref, bs_ref, cs_ref, arep_ref, apow_ref, dskip_ref, dproj_ref, *rest):
        du_ref, da_ref, db_ref, dc_ref, dd_ref, gs, g0, db_acc, dc_acc = rest[len(after):]
        c = pl.program_id(0)

        @pl.when(c == 0)
        def _():
            g0[...] = jnp.zeros_like(g0)
            da_ref[...] = jnp.zeros_like(da_ref)
            dd_ref[...] = jnp.zeros_like(dd_ref)
            db_acc[...] = jnp.zeros_like(db_acc)
            dc_acc[...] = jnp.zeros_like(dc_acc)

        up = jnp.dot(pm_ref[...], u_ref[...], preferred_element_type=f32)
        upb = up.astype(bf16)
        dyp = jnp.dot(pm_ref[...], dy_ref[...], preferred_element_type=f32)
        dypb = dyp.astype(bf16)
        for s in range(NS):
            gs[:, 2 * SH * s:2 * SH * (s + 1)] = lax.dot_general(
                dypb[:, SC * s:SC * (s + 1)], cs_ref[s], (((1,), (1,)), ((), ())), preferred_element_type=f32)
        rows = lax.broadcasted_iota(jnp.int32, (8, W), 0)
        zero = jnp.zeros((8, W), f32)
        for rc, ic in _lane_slabs(cfg, W):
            ar = arep_ref[:, rc:rc + W]
            ai = arep_ref[:, ic:ic + W]

            def rstep(k, carry, rc=rc, ic=ic, ar=ar, ai=ai):
                gr, gi = carry
                r0 = pl.multiple_of((S - 1 - k) * 8, 8)
                nr = ar * gr + ai * gi + gs[pl.ds(r0, 8), rc:rc + W]
                ni = ar * gi - ai * gr + gs[pl.ds(r0, 8), ic:ic + W]
                gs[pl.ds(r0, 8), rc:rc + W] = nr
                gs[pl.ds(r0, 8), ic:ic + W] = ni
                return nr, ni

            er, ei = lax.fori_loop(0, S, rstep, (zero, zero))
            pr = apow_ref[:, rc:rc + W]
            pi = apow_ref[:, ic:ic + W]
            g0r = g0[:, rc:rc + W]
            g0i = g0[:, ic:ic + W]
            cr = jnp.where(rows == 7, g0r, 0.0)
            ci = jnp.where(rows == 7, g0i, 0.0)
            for _ in range(7):
                fr = er + pr * cr + pi * ci
                fi = ei + pr * ci - pi * cr
                cr = jnp.where(rows == 7, g0r, pltpu.roll(fr, 7, axis=0))
                ci = jnp.where(rows == 7, g0i, pltpu.roll(fi, 7, axis=0))
            fr = er + pr * cr + pi * ci
            fi = ei + pr * ci - pi * cr
            g0[:, rc:rc + W] = jnp.broadcast_to(fr[0:1, :], (8, W))
            g0[:, ic:ic + W] = jnp.broadcast_to(fi[0:1, :], (8, W))

            def fix(k, carry, rc=rc, ic=ic, ar=ar, ai=ai):
                kr, ki, accr, acci = carry
                i = S - 1 - k
                r0 = pl.multiple_of(i * 8, 8)
                rp = pl.multiple_of((i - 1) * 8, 8)
                nr = ar * kr + ai * ki
                ni = ar * ki - ai * kr
                gr = gs[pl.ds(r0, 8), rc:rc + W] + nr
                gi = gs[pl.ds(r0, 8), ic:ic + W] + ni
                gs[pl.ds(r0, 8), rc:rc + W] = gr
                gs[pl.ds(r0, 8), ic:ic + W] = gi
                xr = st[pl.ds(rp, 8), rc:rc + W]
                xi = st[pl.ds(rp, 8), ic:ic + W]
                return nr, ni, accr + gr * xr + gi * xi, acci + gi * xr - gr * xi

            kr, ki, accr, acci = lax.fori_loop(0, S - 1, fix, (cr, ci, zero, zero))
            nr = ar * kr + ai * ki
            ni = ar * ki - ai * kr
            gr = gs[0:8, rc:rc + W] + nr
            gi = gs[0:8, ic:ic + W] + ni
            gs[0:8, rc:rc + W] = gr
            gs[0:8, ic:ic + W] = gi
            xr = cin_ref[0, :, rc:rc + W]
            xi = cin_ref[0, :, ic:ic + W]
            da_ref[:, rc:rc + W] += accr + gr * xr + gi * xi
            da_ref[:, ic:ic + W] += acci + gi * xr - gr * xi

        dups = []
        for s in range(NS):
            gsb = gs[:, 2 * SH * s:2 * SH * (s + 1)].astype(bf16)
            dups.append(lax.dot_general(gsb, bs_ref[s], (((1,), (1,)), ((), ())), preferred_element_type=f32))
            db_acc[s] += lax.dot_general(upb[:, SC * s:SC * (s + 1)], gsb, (((0,), (0,)), ((), ())),
                                         preferred_element_type=f32)
            dc_acc[s] += lax.dot_general(st[:, 2 * SH * s:2 * SH * (s + 1)].astype(bf16), dypb[:, SC * s:SC * (s + 1)],
                                         (((0,), (0,)), ((), ())), preferred_element_type=f32)
        dup = jnp.concatenate(dups, axis=1) + dskip_ref[...] * dyp
        du_ref[...] = jnp.dot(pmt_ref[...], dup.astype(bf16), preferred_element_type=f32).astype(bf16)
        dd_ref[...] += jnp.sum(dyp * up, axis=0, keepdims=True)

        @pl.when(c == NC - 1)
        def _():
            PS, GH = _SSM_STATE, _SSM_GROUP
            mask_b = (lax.broadcasted_iota(jnp.int32, (SC, SH), 0) // GH
                      == lax.broadcasted_iota(jnp.int32, (SC, SH), 1) // PS)
            mask_c = (lax.broadcasted_iota(jnp.int32, (SH, SC), 0) // PS
                      == lax.broadcasted_iota(jnp.int32, (SH, SC), 1) // GH)
            for s in range(NS):
                for r in range(2):
                    xb = jnp.where(mask_b, db_acc[s, :, r * SH:(r + 1) * SH], 0.0)
                    zb = xb[:, 0:128]
                    for q in range(1, SH // 128):
                        zb = zb + xb[:, q * 128:(q + 1) * 128]
                    db_ref[s, r] = zb + pltpu.roll(zb, PS, axis=1)
                    xc = jnp.where(mask_c, dc_acc[s, r * SH:(r + 1) * SH, :], 0.0)
                    zc = xc[0:PS]
                    for q in range(1, SH // PS):
                        zc = zc + xc[q * PS:(q + 1) * PS]
                    dc_ref[s, r] = zc

    rev = lambda c: (NC - 1 - c, 0)
    const2 = lambda shape: pl.BlockSpec(shape, lambda c: (0, 0))
    const3 = lambda shape: pl.BlockSpec(shape, lambda c: (0, 0, 0))
    const4 = lambda shape: pl.BlockSpec(shape, lambda c: (0, 0, 0, 0))
    return pl.pallas_call(
        body, grid=(NC,),
        in_specs=[pl.BlockSpec((T, SW), rev), pl.BlockSpec((T, SW), rev), pl.BlockSpec((1, 8, NST), lambda c: (NC - 1 - c, 0, 0)),
                  pl.BlockSpec((T, NST), rev), const2((T, T)), const2((T, T)),
                  pl.BlockSpec((NS, SC, 2 * SH), lambda c: (0, 0, 0), pipeline_mode=pl.Buffered(1)),
                  pl.BlockSpec((NS, 2 * SH, SC), lambda c: (0, 0, 0), pipeline_mode=pl.Buffered(1)),
                  const2((8, NST)), const2((8, NST)), const2((1, SW)), _ANY] + [_ANY] * len(after),
        out_specs=[pl.BlockSpec((T, SW), lambda c: (NC - 1 - c, du_col)), const2((8, NST)),
                   const4((NS, 2, SC, 128)), const4((NS, 2, _SSM_STATE, SC)), const2((1, SW))],
        out_shape=[jax.ShapeDtypeStruct(dproj.shape, bf16), jax.ShapeDtypeStruct((8, NST), f32),
                   jax.ShapeDtypeStruct((NS, 2, SC, 128), f32), jax.ShapeDtypeStruct((NS, 2, _SSM_STATE, SC), f32),
                   jax.ShapeDtypeStruct((1, SW), f32)],
        scratch_shapes=[pltpu.VMEM((T, NST), f32), pltpu.VMEM((8, NST), f32),
                        pltpu.VMEM((NS, SC, 2 * SH), f32), pltpu.VMEM((NS, 2 * SH, SC), f32)],
        input_output_aliases={11: 0},
        name="s5_bwd", compiler_params=_cparams(("arbitrary",), _VMEM_S5_BWD))(proj, dy, cin, xs, pm, pmt, bs, cs, arep, apow, dskip, dproj,
                                                                  *after)


def _s5_param_grads(cfg, da, db_diag, dc_diag):
    G, NS, SC, SH, NST = _s5_dims(cfg)
    das = da.sum(axis=0).reshape(NS, 2, SH)
    dabr = das[:, 0].reshape(G, _SSM_STATE)
    dabi = das[:, 1].reshape(G, _SSM_STATE)
    dbd = db_diag[..., :_SSM_STATE].reshape(NS, 2, _SLAB_GROUPS, _SSM_GROUP, _SSM_STATE)
    dbb = dbd.transpose(1, 0, 2, 4, 3).reshape(2, G, _SSM_STATE, _SSM_GROUP)
    dcd = dc_diag.reshape(NS, 2, _SSM_STATE, _SLAB_GROUPS, _SSM_GROUP)
    dcc = dcd.transpose(1, 0, 3, 4, 2).reshape(2, G, _SSM_GROUP, _SSM_STATE)
    return dabr, dabi, dbb[0], dbb[1], dcc[0], -dcc[1]


def _coords():
    return lax.axis_index("x"), lax.axis_index("y"), lax.axis_index("c")


def _other_chips(x, y):
    return [(1 - x, y), (x, 1 - y), (1 - x, 1 - y)]


def _allreduce8(name, v):
    R = v.shape[0]

    def body(v_ref, o_ref, sib, chips, mine, ssem, rsem):
        x, y, c = _coords()
        d2d = pltpu.make_async_remote_copy(src_ref=v_ref, dst_ref=sib, send_sem=ssem.at[0], recv_sem=rsem.at[0],
                                           device_id=(x, y, 1 - c), device_id_type=_MESH)
        d2d.start()
        d2d.wait()
        mine[...] = v_ref[...] + sib[...]
        cps = [pltpu.make_async_remote_copy(src_ref=mine, dst_ref=chips.at[j], send_sem=ssem.at[1 + j],
                                            recv_sem=rsem.at[1 + j], device_id=(*chip, c), device_id_type=_MESH)
               for j, chip in enumerate(_other_chips(x, y))]
        for cp in cps:
            cp.start()
        for cp in cps:
            cp.wait()
        o_ref[...] = (mine[...] + chips[1]) + (chips[0] + chips[2])

    vm = pl.BlockSpec(memory_space=pltpu.VMEM)
    return pl.pallas_call(
        body, in_specs=[vm], out_specs=vm, out_shape=jax.ShapeDtypeStruct((R, 128), f32),
        scratch_shapes=[pltpu.VMEM((R, 128), f32), pltpu.VMEM((3, R, 128), f32), pltpu.VMEM((R, 128), f32),
                        pltpu.SemaphoreType.DMA((4,)), pltpu.SemaphoreType.DMA((4,))],
        name=name, compiler_params=pltpu.CompilerParams(vmem_limit_bytes=_VMEM_LIMIT))(v)


def _cast_into_slot(name, w, k_idx, also_alone=False):
    R, C = w.shape
    tr = _tile(R, 256, 16)

    def body(k_ref, w_ref, *o_refs):
        for o_ref in o_refs:
            o_ref[...] = w_ref[...].astype(bf16)

    slot = (jax.ShapeDtypeStruct((_NCHIP, R, C), bf16), pl.BlockSpec((None, tr, C), lambda r, kr: (kr[0], r, 0)))
    alone = (jax.ShapeDtypeStruct((R, C), bf16), pl.BlockSpec((tr, C), lambda r, kr: (r, 0)))
    outs = [slot, alone] if also_alone else [slot]
    gs = pltpu.PrefetchScalarGridSpec(
        num_scalar_prefetch=1, grid=(R // tr,),
        in_specs=[pl.BlockSpec((tr, C), lambda r, kr: (r, 0))], out_specs=[o[1] for o in outs])
    res = pl.pallas_call(body, grid_spec=gs, out_shape=[o[0] for o in outs], name=name,
                         compiler_params=_cparams(("parallel",)))(k_idx, w)
    return res if also_alone else res[0]


def _in_proj(cfg, xn1, w_own, w_in, k_idx):
    L, D = xn1.shape
    S, _, Ns = w_in.shape
    tm, tn = _tile(L, 1024, 16), _tile(Ns, 1024, 128)
    nb = Ns // tn

    def body(k_ref, a_ref, w_ref, *rest):
        rest[-1][...] = jnp.dot(a_ref[...], w_ref[...], preferred_element_type=f32).astype(bf16)

    a_spec = pl.BlockSpec((tm, D), lambda i, j, kr: (i, 0))
    shape = jax.ShapeDtypeStruct((L, S * Ns), bf16)
    own = pltpu.PrefetchScalarGridSpec(
        num_scalar_prefetch=1, grid=(L // tm, nb),
        in_specs=[a_spec, pl.BlockSpec((D, tn), lambda i, j, kr: (0, j))],
        out_specs=pl.BlockSpec((tm, tn), lambda i, j, kr: (i, kr[0] * nb + j)))
    proj = pl.pallas_call(body, grid_spec=own, out_shape=shape, name="in_proj_own",
                          compiler_params=_cparams(("parallel", "parallel")))(k_idx, xn1, w_own)
    shard = lambda j, kr: (kr[0] + 1 + j // nb) % S
    rest = pltpu.PrefetchScalarGridSpec(
        num_scalar_prefetch=1, grid=(L // tm, (S - 1) * nb),
        in_specs=[a_spec, pl.BlockSpec((None, D, tn), lambda i, j, kr: (shard(j, kr), 0, j % nb)), _ANY],
        out_specs=pl.BlockSpec((tm, tn), lambda i, j, kr: (i, shard(j, kr) * nb + j % nb)))
    return pl.pallas_call(body, grid_spec=rest, out_shape=shape, input_output_aliases={3: 0}, name="in_proj",
                          compiler_params=_cparams(("parallel", "parallel")))(k_idx, xn1, w_in, proj)


def _handshake(peers):
    barrier = pltpu.get_barrier_semaphore()
    for peer in peers:
        pl.semaphore_signal(barrier, inc=1, device_id=peer, device_id_type=_MESH)
    pl.semaphore_wait(barrier, len(peers))


def _allgather_weights(name, bufs, collective_id):
    n = len(bufs)
    refs = [jax.new_ref(b, memory_space=pltpu.MemorySpace.HBM) for b in bufs]

    def copy(ref, sems, idx, to):
        return pltpu.make_async_remote_copy(src_ref=ref, dst_ref=ref, send_sem=sems[0].at[idx], recv_sem=sems[1].at[idx],
                                            device_id=to, device_id_type=_MESH)

    def launch(ssem, rsem, qssem, qrsem, fssem, frsem):
        x, y, c = _coords()
        k = 2 * x + y
        nbrs = [(1 - x, y), (x, 1 - y)]
        across = 2 * (1 - x) + (1 - y)
        sibling = (x, y, 1 - c)
        _handshake([sibling] + [(*chip, c) for chip in nbrs])
        started = []

        def start(cp):
            cp.start()
            started.append(cp)

        for w in range(n):
            rh = refs[w].shape[1] // 2
            for j, chip in enumerate(nbrs):
                start(copy(refs[w].at[k, pl.ds(c * rh, rh)], (ssem, rsem), (w, j), (*chip, c)))
        for w in range(n):
            rh = refs[w].shape[1] // 2
            rq = rh // 2
            for j, (ox, oy) in enumerate(nbrs):
                ko = 2 * ox + oy
                landed = refs[w].at[ko, pl.ds(c * rh, rh)]
                copy(landed, (ssem, rsem), (w, j), (ox, oy, c)).wait_recv()
                start(copy(refs[w].at[ko, pl.ds(c * rh + j * rq, rq)], (qssem, qrsem), (w, j), (*nbrs[1 - j], c)))
                start(copy(landed, (fssem, frsem), (w, j), sibling))
        for w in range(n):
            rh = refs[w].shape[1] // 2
            rq = rh // 2
            for q in range(2):
                quarter = refs[w].at[across, pl.ds(c * rh + q * rq, rq)]
                copy(quarter, (qssem, qrsem), (w, q), (*nbrs[1 - q], c)).wait_recv()
            start(copy(refs[w].at[across, pl.ds(c * rh, rh)], (fssem, frsem), (w, 2), sibling))
        for w in range(n):
            rh = refs[w].shape[1] // 2
            for j, ko in enumerate([2 * nbrs[0][0] + nbrs[0][1], 2 * nbrs[1][0] + nbrs[1][1], across]):
                copy(refs[w].at[ko, pl.ds((1 - c) * rh, rh)], (fssem, frsem), (w, j), sibling).wait_recv()
        for cp in started:
            cp.wait_send()

    _sequencer_kernel(name, collective_id,
                      (pltpu.SemaphoreType.DMA((n, 2)), pltpu.SemaphoreType.DMA((n, 2)), pltpu.SemaphoreType.DMA((n, 2)),
                       pltpu.SemaphoreType.DMA((n, 2)), pltpu.SemaphoreType.DMA((n, 3)), pltpu.SemaphoreType.DMA((n, 3))),
                      launch)
    return [r[...] for r in refs]


def _sequencer_kernel(name, collective_id, sems, body):
    pl.kernel(body, mesh=plsc.ScalarSubcoreMesh(axis_name="seq", num_cores=1), name=name, scratch_types=sems,
              compiler_params=pltpu.CompilerParams(collective_id=collective_id))()


def _hbm_ref(a):
    return jax.new_ref(a, memory_space=pltpu.MemorySpace.HBM)


def _exchange_halves(name, grads, collective_id):
    n = len(grads)
    srcs = [_hbm_ref(g) for g in grads]
    dsts = [jax.empty_ref(jax.ShapeDtypeStruct((g.shape[0], g.shape[1] // 2, g.shape[2]), g.dtype),
                          memory_space=pltpu.MemorySpace.HBM) for g in grads]

    def body(ssem, rsem):
        x, y, c = _coords()
        _handshake([(x, y, 1 - c)])
        cps = []
        for w in range(n):
            rh = srcs[w].shape[1] // 2
            cp = pltpu.make_async_remote_copy(
                src_ref=srcs[w].at[:, pl.ds((1 - c) * rh, rh)], dst_ref=dsts[w], send_sem=ssem.at[w], recv_sem=rsem.at[w],
                device_id=(x, y, 1 - c), device_id_type=_MESH)
            cp.start()
            cps.append(cp)
        for cp in cps:
            cp.wait()

    _sequencer_kernel(name, collective_id, (pltpu.SemaphoreType.DMA((n,)), pltpu.SemaphoreType.DMA((n,))), body)
    return [d[...] for d in dsts]


def _scatter_shards(name, parts, collective_id):
    n = len(parts)
    srcs = [_hbm_ref(p) for p in parts]
    dsts = [jax.empty_ref(jax.ShapeDtypeStruct((3,) + p.shape[1:], p.dtype), memory_space=pltpu.MemorySpace.HBM)
            for p in parts]

    def body(ssem, rsem):
        x, y, c = _coords()
        k = 2 * x + y
        others = _other_chips(x, y)
        _handshake([(*chip, c) for chip in others])
        cps = []
        for w in range(n):
            for j, (ox, oy) in enumerate(others):
                cp = pltpu.make_async_remote_copy(
                    src_ref=srcs[w].at[(2 * ox + oy - k + 3) % 4], dst_ref=dsts[w].at[j], send_sem=ssem.at[w, j],
                    recv_sem=rsem.at[w, j],
                    device_id=(ox, oy, c), device_id_type=_MESH)
                cp.start()
                cps.append(cp)
        for cp in cps:
            cp.wait()

    _sequencer_kernel(name, collective_id, (pltpu.SemaphoreType.DMA((n, 3)), pltpu.SemaphoreType.DMA((n, 3))), body)
    return [d[...] for d in dsts]


def _join_halves(name, bufs, collective_id):
    n = len(bufs)
    refs = [_hbm_ref(b) for b in bufs]

    def body(ssem, rsem):
        x, y, c = _coords()
        _handshake([(x, y, 1 - c)])
        cps = []
        for w in range(n):
            rh = refs[w].shape[0] // 2
            mine = refs[w].at[pl.ds(c * rh, rh)]
            cp = pltpu.make_async_remote_copy(src_ref=mine, dst_ref=mine, send_sem=ssem.at[w], recv_sem=rsem.at[w],
                                              device_id=(x, y, 1 - c), device_id_type=_MESH)
            cp.start()
            cps.append(cp)
        for w in range(n):
            rh = refs[w].shape[0] // 2
            theirs = refs[w].at[pl.ds((1 - c) * rh, rh)]
            pltpu.make_async_remote_copy(src_ref=theirs, dst_ref=theirs, send_sem=ssem.at[w], recv_sem=rsem.at[w],
                                         device_id=(x, y, 1 - c), device_id_type=_MESH).wait_recv()
        for cp in cps:
            cp.wait_send()

    _sequencer_kernel(name, collective_id, (pltpu.SemaphoreType.DMA((n,)), pltpu.SemaphoreType.DMA((n,))), body)
    return [r[...] for r in refs]


def _allgather_small(name, v, collective_id):
    R = v.shape[0]
    src = _hbm_ref(v)
    dst = jax.empty_ref(jax.ShapeDtypeStruct((8, R, 128), v.dtype), memory_space=pltpu.MemorySpace.HBM)
    flips = [(dx, dy, dc) for dx in (0, 1) for dy in (0, 1) for dc in (0, 1)][1:]

    def body(lsem, ssem, rsem):
        x, y, c = _coords()
        me = 4 * x + 2 * y + c
        flip = lambda v, d: 1 - v if d else v
        peers = [(flip(x, dx), flip(y, dy), flip(c, dc)) for dx, dy, dc in flips]
        _handshake(peers)
        own = pltpu.make_async_copy(src, dst.at[me], lsem)
        own.start()
        cps = []
        for r, peer in enumerate(peers):
            cp = pltpu.make_async_remote_copy(src_ref=src, dst_ref=dst.at[me], send_sem=ssem.at[r], recv_sem=rsem.at[r],
                                              device_id=peer, device_id_type=_MESH)
            cp.start()
            cps.append(cp)
        for r, (px, py, pc) in enumerate(peers):
            theirs = dst.at[4 * px + 2 * py + pc]
            pltpu.make_async_remote_copy(src_ref=theirs, dst_ref=theirs, send_sem=ssem.at[r], recv_sem=rsem.at[r],
                                         device_id=(px, py, pc), device_id_type=_MESH).wait_recv()
        for cp in cps:
            cp.wait_send()
        own.wait()

    _sequencer_kernel(name, collective_id, (pltpu.SemaphoreType.DMA, pltpu.SemaphoreType.DMA((7,)),
                                            pltpu.SemaphoreType.DMA((7,))), body)
    return dst[...]


def _sum8(name, g8, after):
    R = g8.shape[1]
    tr = _tile(R, 512, 8)

    def body(g_ref, *rest):
        rest[-1][...] = (((g_ref[0] + g_ref[1]) + (g_ref[2] + g_ref[3]))
                         + ((g_ref[4] + g_ref[5]) + (g_ref[6] + g_ref[7])))

    return pl.pallas_call(body, grid=(R // tr,),
                          in_specs=[pl.BlockSpec((8, tr, 128), lambda i: (0, i, 0))] + [_ANY] * len(after),
                          out_specs=pl.BlockSpec((tr, 128), lambda i: (i, 0)), out_shape=jax.ShapeDtypeStruct((R, 128), f32),
                          name=name, compiler_params=_cparams(("parallel",)))(g8, *after)


def _add_own_half(name, g, t, kc_idx, after):
    S, R, C = g.shape
    rh = R // 2
    tr = _tile(rh, 512, 16)
    nrb = rh // tr
    shard = lambda s, kc: (kc[0] + 1 + s) % S

    def body(kc_ref, g_ref, t_ref, *rest):
        rest[-1][...] = (g_ref[...] + t_ref[...]).astype(bf16)

    gs = pltpu.PrefetchScalarGridSpec(
        num_scalar_prefetch=1, grid=(S - 1, nrb),
        in_specs=[pl.BlockSpec((None, tr, C), lambda s, r, kc: (shard(s, kc), kc[1] * nrb + r, 0)),
                  pl.BlockSpec((None, tr, C), lambda s, r, kc: (shard(s, kc), r, 0))] + [_ANY] * len(after),
        out_specs=pl.BlockSpec((None, tr, C), lambda s, r, kc: (s, r, 0)))
    return pl.pallas_call(body, grid_spec=gs, out_shape=jax.ShapeDtypeStruct((S - 1, rh, C), bf16), name=name,
                          compiler_params=_cparams(("parallel", "parallel")))(kc_idx, g, t, *after)


def _add_shard_parts(name, g, t, r, kc_idx, after):
    S, R, C = g.shape
    rh = R // 2
    tr = _tile(rh, 256, 16)
    nrb = rh // tr

    def body(kc_ref, g_ref, t_ref, r_ref, *rest):
        own = g_ref[...] + t_ref[...]
        rest[-1][...] = (own + r_ref[1].astype(f32)) + (r_ref[0].astype(f32) + r_ref[2].astype(f32))

    gs = pltpu.PrefetchScalarGridSpec(
        num_scalar_prefetch=1, grid=(nrb,),
        in_specs=[pl.BlockSpec((None, tr, C), lambda i, kc: (kc[0], kc[1] * nrb + i, 0)),
                  pl.BlockSpec((None, tr, C), lambda i, kc: (kc[0], i, 0)),
                  pl.BlockSpec((3, tr, C), lambda i, kc: (0, i, 0))] + [_ANY] * len(after),
        out_specs=pl.BlockSpec((tr, C), lambda i, kc: (kc[1] * nrb + i, 0)))
    return pl.pallas_call(body, grid_spec=gs, out_shape=jax.ShapeDtypeStruct((R, C), f32), name=name,
                          compiler_params=_cparams(("parallel",)))(kc_idx, g, t, r, *after)


def _adamw_update(wv, gv, mv, vv):
    nm = _ADAM_B1 * mv + (1.0 - _ADAM_B1) * gv
    nv = _ADAM_B2 * vv + (1.0 - _ADAM_B2) * (gv * gv)
    m_hat = nm / (1.0 - _ADAM_B1 ** _ADAM_STEP)
    v_hat = nv / (1.0 - _ADAM_B2 ** _ADAM_STEP)
    return -_ADAM_LR * (m_hat / (jnp.sqrt(v_hat) + _ADAM_EPS) + _ADAM_WD * wv), nm, nv


def _adamw(name, w, g, m, v, after=()):
    R, C = w.shape
    tr = _tile(R, 128, 8)

    def step(w_ref, g_ref, m_ref, v_ref, go_ref, d_ref, nm_ref, nv_ref):
        gv = g_ref[...]
        go_ref[...] = gv
        d_ref[...], nm_ref[...], nv_ref[...] = _adamw_update(w_ref[...], gv, m_ref[...], v_ref[...])

    deep = pl.BlockSpec((tr, C), lambda i: (i, 0), pipeline_mode=pl.Buffered(3))
    spec = pl.BlockSpec((tr, C), lambda i: (i, 0))

    def body(*refs):
        ins, outs = refs[:4], refs[4 + len(after):]
        pltpu.emit_pipeline(step, grid=(R // tr,), in_specs=[deep] * 4, out_specs=[spec] * 4)(*ins, *outs)

    shape = jax.ShapeDtypeStruct((R, C), f32)
    return pl.pallas_call(body, in_specs=[_ANY] * (4 + len(after)), out_specs=[_ANY] * 4, out_shape=[shape] * 4, name=name,
                          compiler_params=pltpu.CompilerParams(vmem_limit_bytes=_VMEM_LIMIT))(w, g, m, v, *after)


def _adamw_whole(name, ws, gs, ms, vs):
    n = len(ws)

    def body(*refs):
        ins, outs = refs[:4 * n], refs[4 * n:]
        for i in range(n):
            w_ref, g_ref, m_ref, v_ref = (ins[j * n + i] for j in range(4))
            d, nm, nv = _adamw_update(w_ref[...], g_ref[...], m_ref[...], v_ref[...])
            outs[i][...], outs[n + i][...], outs[2 * n + i][...] = d, nm, nv

    vm = pl.BlockSpec(memory_space=pltpu.VMEM)
    res = pl.pallas_call(body, in_specs=[vm] * (4 * n), out_specs=[vm] * (3 * n),
                         out_shape=[jax.ShapeDtypeStruct(w.shape, f32) for w in ws] * 3, name=name,
                         compiler_params=pltpu.CompilerParams(vmem_limit_bytes=_VMEM_LIMIT))(*ws, *gs, *ms, *vs)
    return res[:n], res[n:2 * n], res[2 * n:]


def _pack(arrs):
    flat = jnp.concatenate([a.reshape(-1).astype(f32) for a in arrs])
    n = flat.shape[0]
    pad = (-n) % (128 * 128)
    return jnp.pad(flat, (0, pad)).reshape(-1, 128)


def _unpack(packed, shapes):
    flat = packed.reshape(-1)
    out, off = [], 0
    for s in shapes:
        n = math.prod(s)
        out.append(flat[off:off + n].reshape(s))
        off += n
    return out


_BIG = ("w_in", "w_glu", "w_ssm_out", "w_conv_out", "w_o", "w_up", "w_down")
_SMALL = ("norm_tok", "a_re", "a_im", "log_dt", "b_re", "b_im", "c_re", "c_im", "d_skip", "conv_w", "conv_b",
          "norm_ffn", "ffn_conv_w", "ffn_conv_b", "norm_final")
_WEIGHTS = ("norm_tok", "w_in", "a_re", "a_im", "log_dt", "b_re", "b_im", "c_re", "c_im", "d_skip", "w_glu",
            "w_ssm_out", "conv_w", "conv_b", "w_conv_out", "w_o", "norm_ffn", "w_up", "ffn_conv_w", "ffn_conv_b",
            "w_down", "norm_final")


def _step(cfg, x, tgt, p, m, v):
    L, D, SW, CW, F = cfg.L, cfg.D, cfg.SW, cfg.CW, cfg.F
    xi, yi, ci = _coords()
    k_idx = (2 * xi + yi).astype(jnp.int32).reshape(1)
    c_idx = ci.astype(jnp.int32).reshape(1)
    x = x.reshape(L, D)
    tgt = tgt.reshape(L, D)

    big2d = {n: p[n].reshape(p[n].shape[-2], p[n].shape[-1]) for n in _BIG}
    slots = {n: _cast_into_slot("cast_" + n, big2d[n], k_idx) for n in _BIG if n != "w_in"}
    slots["w_in"], w_in_own = _cast_into_slot("cast_w_in", big2d["w_in"], k_idx, also_alone=True)
    wg = {}
    for cid, (gname, group) in enumerate((("allgather_w_in", ("w_in",)),
                                          ("allgather_w_mixer", ("w_glu", "w_ssm_out", "w_conv_out", "w_o")),
                                          ("allgather_w_up", ("w_up",)), ("allgather_w_down", ("w_down",)))):
        wg.update(zip(group, _allgather_weights(gname, [slots[n] for n in group], cid)))
    w_in, w_so, w_co, w_up = wg["w_in"], wg["w_ssm_out"], wg["w_conv_out"], wg["w_up"]
    w_glu, w_o, w_down = wg["w_glu"], wg["w_o"], wg["w_down"]
    kk = k_idx[0]
    cw_full = lax.dynamic_update_slice(jnp.zeros((3, CW), f32), p["conv_w"].reshape(3, CW // _NCHIP), (0, kk * (CW // _NCHIP)))
    fw_full = lax.dynamic_update_slice(jnp.zeros((3, F), f32), p["ffn_conv_w"].reshape(3, F // _NCHIP), (0, kk * (F // _NCHIP)))
    south = (ci == 0).astype(f32)
    filters8 = _allgather_small("allgather_conv_filters", _pack([cw_full * south, fw_full * south]), 17)
    conv_b = p["conv_b"].reshape(1, CW)
    ffn_conv_b = p["ffn_conv_b"].reshape(1, F)
    norm_tok = p["norm_tok"].reshape(1, D)
    norm_ffn = p["norm_ffn"].reshape(1, D)
    norm_final = p["norm_final"].reshape(1, D)
    dskip = p["d_skip"].reshape(1, SW)

    s5_in = (p["a_re"][0], p["a_im"][0], p["log_dt"][0], p["b_re"][0], p["b_im"][0])
    (abr, abi, bbr, bbi), disc_vjp = jax.vjp(_discretize, *s5_in)
    tabs = _s5_tables(cfg, *s5_in[:3], abr, abi, bbr, bbi, p["c_re"][0], p["c_im"][0])

    xn1, r1 = _rms_fwd("rms_tok", x, norm_tok)
    proj = _in_proj(cfg, xn1, w_in_own, w_in, k_idx)
    conv_w, ffn_conv_w = _unpack(_sum8("sum_conv_filters", filters8, [proj]), [(3, CW), (3, F)])
    y_s, ya0, cin, xs = _s5_fwd(cfg, proj, tabs, dskip)

    def tiles(*arrs):
        return lambda tm, tn: [(a, pl.BlockSpec((tm, tn), lambda i, j: (i, j))) for a in arrs]

    def glu_epi(acc, e, o):
        o[0][...] = (e[0][...].astype(f32) * jax.nn.sigmoid(acc)).astype(bf16)
        o[1][...] = acc.astype(bf16)

    ya1, z = _mm_nn("glu", ya0, w_glu, [bf16, bf16], rows=True, extras_fn=tiles(ya0), epilogue=glu_epi)
    yb0 = _convb_fwd(cfg, proj, conv_w, conv_b)
    merged, ya, yb = _merge_fwd(cfg, ya1, yb0, proj, w_so, w_co)

    def res_epi(acc, e, o):
        o[0][...] = e[0][...] + acc

    h1 = _mm_nn("out_proj", merged, w_o, [f32], tm=512, tn=D, rows=True, extras_fn=tiles(x), epilogue=res_epi)[0]
    xn2, r2 = _rms_fwd("rms_ffn", h1, norm_ffn)
    hh = _mm_nn("ffn_up", xn2, w_up, [bf16], tn=2816)[0]
    fact, ffn_pre = _ffn_act(cfg, hh, ffn_conv_w, ffn_conv_b)
    h2 = _mm_nn("ffn_down", fact, w_down, [f32], tm=512, rows=True, extras_fn=tiles(h1), epilogue=res_epi)[0]
    dh2, dh2b, g_norm_final, loss_tile = _loss_head("loss_head", h2, tgt, norm_final)

    kc_idx = jnp.concatenate([k_idx, c_idx])
    reduced, chains = {}, {}

    def rs_halves(tag, collective_id, names, gs):
        chains[tag] = dict(cid=collective_id, names=names, gs=gs,
                           sib=_exchange_halves("grad_halves_" + tag, gs, collective_id))

    def rs_shards(tag, after):
        ch = chains[tag]
        ch["parts"] = [_add_own_half("grad_add_halves_" + n, g, t, kc_idx, after)
                       for n, g, t in zip(ch["names"], ch["gs"], ch["sib"])]
        ch["chips"] = _scatter_shards("grad_shards_" + tag, ch["parts"], ch["cid"] + 1)
        return ch["parts"]

    def rs_join(tag, after):
        ch = chains[tag]
        ch["halves"] = [_add_shard_parts("grad_add_chips_" + n, g, t, r, kc_idx, after)
                        for n, g, t, r in zip(ch["names"], ch["gs"], ch["sib"], ch["chips"])]
        reduced.update(zip(ch["names"], _join_halves("grad_join_" + tag, ch["halves"], ch["cid"] + 2)))
        return ch["halves"]

    g_w_down = _mm_tn("ffn_down_dw", fact, dh2b, tm=1408, tn=512)
    rs_halves("ffn_down", 4, ["w_down"], [g_w_down.reshape(_NCHIP, F // _NCHIP, D)])
    dhh, g_ffn_conv_w, g_ffn_conv_b = _ffn_down_dx_act_bwd(cfg, dh2b, w_down, hh, ffn_pre, ffn_conv_w)
    sent = rs_shards("ffn_down", [dhh])
    g_w_up = _mm_tn("ffn_up_dw", xn2, dhh, shards=_NCHIP, tm=512, tn=1408, b_resident=True, after=sent)
    rs_halves("ffn_up", 7, ["w_up"], [g_w_up])
    dxn2 = _ffn_up_dx(cfg, dhh, w_up)
    sent = rs_shards("ffn_up", [dxn2]) + rs_join("ffn_down", [dxn2])
    dh1, dh1b, g_norm_ffn = _rms_bwd("rms_ffn_bwd", dxn2, h1, r2, norm_ffn, dh2, after=sent)

    dya, dyb, dproj = _out_proj_dx(cfg, dh1b, w_o, ya, yb, proj)
    g_w_o = _mm_tn("out_proj_dw", merged, dh1b)

    def glu_bwd_epi(acc, e, o):
        a0 = e[0][...].astype(f32)
        s = jax.nn.sigmoid(e[1][...].astype(f32))
        o[0][...] = (acc * a0 * s * (1.0 - s)).astype(bf16)
        o[1][...] = (acc * s).astype(bf16)

    dz, t1 = _mm_nt("ssm_out_dx", dya, w_so, [bf16, bf16], extras_fn=tiles(ya0, z), epilogue=glu_bwd_epi)
    g_w_so = _mm_tn("ssm_out_dw", ya1, dya, shards=_NCHIP, tn=512)
    dyb0 = _mm_nt("conv_out_dx", dyb, w_co, [bf16])[0]
    g_w_co = _mm_tn("conv_out_dw", yb0, dyb, shards=_NCHIP, tn=512)
    dproj, g_conv_w, g_conv_b = _convb_bwd(cfg, proj, dyb0, conv_w, conv_b, dproj)

    def gelu_bwd_epi(acc, e, o):
        o[0][...] = ((e[0][...].astype(f32) + acc) * _gelu_grad(e[1][...].astype(f32))).astype(bf16)

    dy_s = _mm_nt("glu_dx", dz, w_glu, [bf16], rows=True, extras_fn=tiles(t1, y_s), epilogue=gelu_bwd_epi)[0]
    g_w_glu = _mm_tn("glu_dw", ya0, dz)
    rs_halves("mixer", 10, ["w_o", "w_ssm_out", "w_conv_out", "w_glu"],
              [g_w_o.reshape(_NCHIP, D // _NCHIP, D), g_w_so, g_w_co, g_w_glu.reshape(_NCHIP, SW // _NCHIP, SW)])
    sent = rs_join("ffn_up", [g_w_glu])
    dproj, da_acc, db_full, dc_full, g_dskip = _s5_bwd(cfg, proj, dy_s, cin, xs, tabs, dskip, dproj, after=sent)
    sent = rs_shards("mixer", [dproj])

    dabr, dabi, dbbr, dbbi, g_c_re, g_c_im = _s5_param_grads(cfg, da_acc, db_full, dc_full)
    g_a_re, g_a_im, g_log_dt, g_b_re, g_b_im = disc_vjp((dabr, dabi, dbbr, dbbi))
    small_g = {"a_re": g_a_re, "a_im": g_a_im, "log_dt": g_log_dt, "b_re": g_b_re, "b_im": g_b_im,
               "c_re": g_c_re, "c_im": g_c_im, "d_skip": g_dskip, "conv_w": g_conv_w, "conv_b": g_conv_b,
               "norm_ffn": g_norm_ffn, "ffn_conv_w": g_ffn_conv_w, "ffn_conv_b": g_ffn_conv_b, "norm_final": g_norm_final}
    early = [n for n in _SMALL if n != "norm_tok"]
    small8 = _allgather_small("allgather_small_grads", _pack([small_g[n] for n in early]), 16)

    g_w_in = _mm_tn("in_proj_dw", xn1, dproj, shards=_NCHIP, tn=CW, after=sent,
                    b_block=lambda j: jnp.where(j == 0, 3 * CW // SW, jnp.where(j < 4, j - 1, j)))
    rs_halves("in_proj", 13, ["w_in"], [g_w_in])
    dxn1 = _in_proj_dx(cfg, dproj, w_in)
    sent = rs_shards("in_proj", [dxn1]) + rs_join("mixer", [dxn1])
    dx, _, g_norm_tok = _rms_bwd("rms_tok_bwd", dxn1, x, r1, norm_tok, dh1, after=sent)

    summed = dict(zip(early, _unpack(_sum8("sum_small_grads", small8, [dx]), [small_g[n].shape for n in early])))
    summed["norm_tok"] = _unpack(_allreduce8("allreduce_norm_tok", _pack([g_norm_tok])), [g_norm_tok.shape])[0]
    summed["conv_w"] = lax.dynamic_slice(summed["conv_w"], (0, kk * (CW // _NCHIP)), (3, CW // _NCHIP))
    summed["ffn_conv_w"] = lax.dynamic_slice(summed["ffn_conv_w"], (0, kk * (F // _NCHIP)), (3, F // _NCHIP))

    grads, deltas, new_m, new_v = {}, {}, {}, {}

    def adamw_big(names, after):
        for n in names:
            g_, d_, m_, v_ = _adamw("adamw_" + n, big2d[n], reduced[n], m[n].reshape(big2d[n].shape),
                                    v[n].reshape(big2d[n].shape), after=after)
            grads[n], deltas[n], new_m[n], new_v[n] = (a.reshape(p[n].shape) for a in (g_, d_, m_, v_))
            after = [d_]
        return after

    for n in _SMALL:
        grads[n] = summed[n].reshape(p[n].shape)
    small_d, small_m, small_v = _adamw_whole("adamw_small", [p[n] for n in _SMALL], [grads[n] for n in _SMALL],
                                             [m[n] for n in _SMALL], [v[n] for n in _SMALL])
    deltas.update(zip(_SMALL, small_d))
    new_m.update(zip(_SMALL, small_m))
    new_v.update(zip(_SMALL, small_v))
    done = adamw_big(["w_down", "w_up", "w_o", "w_ssm_out", "w_conv_out", "w_glu"], [deltas["norm_final"]])
    rs_join("in_proj", done + [deltas[n] for n in _SMALL])
    adamw_big(["w_in"], ())

    loss = lax.psum(loss_tile[0, 0], ("x", "y", "c"))
    return (loss, dx.reshape(1, L, D), *[grads[n] for n in _WEIGHTS], *[deltas[n] for n in _WEIGHTS],
            *[new_m[n] for n in _WEIGHTS], *[new_v[n] for n in _WEIGHTS])


def kernel(x, norm_tok, w_in, a_re, a_im, log_dt, b_re, b_im, c_re, c_im, d_skip, w_glu, w_ssm_out, conv_w, conv_b, w_conv_out, w_o, norm_ffn, w_up, ffn_conv_w, ffn_conv_b, w_down, norm_final, loss_target, m_norm_tok, m_w_in, m_a_re, m_a_im, m_log_dt, m_b_re, m_b_im, m_c_re, m_c_im, m_d_skip, m_w_glu, m_w_ssm_out, m_conv_w, m_conv_b, m_w_conv_out, m_w_o, m_norm_ffn, m_w_up, m_ffn_conv_w, m_ffn_conv_b, m_w_down, m_norm_final, v_norm_tok, v_w_in, v_a_re, v_a_im, v_log_dt, v_b_re, v_b_im, v_c_re, v_c_im, v_d_skip, v_w_glu, v_w_ssm_out, v_conv_w, v_conv_b, v_w_conv_out, v_w_o, v_norm_ffn, v_w_up, v_ffn_conv_w, v_ffn_conv_b, v_w_down, v_norm_final):
    p = dict(norm_tok=norm_tok, w_in=w_in, a_re=a_re, a_im=a_im, log_dt=log_dt, b_re=b_re, b_im=b_im, c_re=c_re,
             c_im=c_im, d_skip=d_skip, w_glu=w_glu, w_ssm_out=w_ssm_out, conv_w=conv_w, conv_b=conv_b,
             w_conv_out=w_conv_out, w_o=w_o, norm_ffn=norm_ffn, w_up=w_up, ffn_conv_w=ffn_conv_w,
             ffn_conv_b=ffn_conv_b, w_down=w_down, norm_final=norm_final)
    m = dict(norm_tok=m_norm_tok, w_in=m_w_in, a_re=m_a_re, a_im=m_a_im, log_dt=m_log_dt, b_re=m_b_re, b_im=m_b_im,
             c_re=m_c_re, c_im=m_c_im, d_skip=m_d_skip, w_glu=m_w_glu, w_ssm_out=m_w_ssm_out, conv_w=m_conv_w,
             conv_b=m_conv_b, w_conv_out=m_w_conv_out, w_o=m_w_o, norm_ffn=m_norm_ffn, w_up=m_w_up,
             ffn_conv_w=m_ffn_conv_w, ffn_conv_b=m_ffn_conv_b, w_down=m_w_down, norm_final=m_norm_final)
    v = dict(norm_tok=v_norm_tok, w_in=v_w_in, a_re=v_a_re, a_im=v_a_im, log_dt=v_log_dt, b_re=v_b_re, b_im=v_b_im,
             c_re=v_c_re, c_im=v_c_im, d_skip=v_d_skip, w_glu=v_w_glu, w_ssm_out=v_w_ssm_out, conv_w=v_conv_w,
             conv_b=v_conv_b, w_conv_out=v_w_conv_out, w_o=v_w_o, norm_ffn=v_norm_ffn, w_up=v_w_up,
             ffn_conv_w=v_ffn_conv_w, ffn_conv_b=v_ffn_conv_b, w_down=v_w_down, norm_final=v_norm_final)
    return _step(_Cfg(), x, loss_target, p, m, v)
```
